```python
import math
import jax, jax.numpy as jnp
from jax import lax
import numpy as np

D_MODEL = 1024
BATCH = 8
SEQ = 4096
DEPTH = 2

ATTN_WIDTH = D_MODEL // 2
HEAD_DIM = 64
N_ATTN_HEADS = ATTN_WIDTH // HEAD_DIM
DILATED_PATTERNS = ((128, 1), (512, 4), (2048, 16))
REL_BUCKETS = 32
REL_MAX_DISTANCE = 1024
SSM_WIDTH = D_MODEL - ATTN_WIDTH
SSM_GROUP = 16
SSM_GROUPS = SSM_WIDTH // SSM_GROUP
SSM_STATE = 64
STEP_MIN = 1e-3
STEP_MAX = 1e-1
PEER_HEADS = 8
PEER_KEYS = 128
PEER_EXPERTS = PEER_KEYS * PEER_KEYS
PEER_QUERY_DIM = 128
PEER_TOPK = 16
PEER_CHUNK = 128
EPS = 1e-6
NEG_INF = -1e30

kernel_name = "hybrid_dilated_attn_s5_peer_encoder"


def _rmsnorm(x, g):
    x32 = x.astype(jnp.float32)
    y = x32 * lax.rsqrt(jnp.mean(x32 * x32, axis=-1, keepdims=True) + EPS) * g.astype(jnp.float32)
    return y.astype(x.dtype)


def _t5_buckets(rel):
    half = REL_BUCKETS // 2
    max_exact = half // 2
    n = np.abs(rel)
    large = max_exact + (np.log(np.maximum(n, 1) / max_exact)
                         / np.log(REL_MAX_DISTANCE / max_exact) * (half - max_exact)).astype(np.int32)
    large = np.minimum(large, half - 1)
    return (np.where(rel > 0, half, 0) + np.where(n < max_exact, n, large)).astype(np.int32)


def _band_structure(length, dilation, half):
    blk = half
    nb = -(-length // blk)
    ql = np.arange(blk)[:, None]
    t = np.arange(3 * blk)[None, :]
    delta = t - blk - ql
    key_pos = (np.arange(nb)[:, None, None] - 1) * blk + t[None]
    valid = (np.abs(delta)[None] <= half) & (key_pos >= 0) & (key_pos < length)
    buckets = _t5_buckets(delta * dilation)
    return nb, valid, buckets


def _dilated_band_attention(q, k, v, rel_bias, window, dilation):
    bsz, seq, heads, hd = q.shape
    half = window // (2 * dilation)
    blk = half
    length = seq // dilation
    nb, valid, buckets = _band_structure(length, dilation, half)
    pad = nb * blk - length

    def to_sub(t):
        return t.reshape(bsz, length, dilation, heads, hd).transpose(0, 2, 3, 1, 4)

    qs = jnp.pad(to_sub(q), ((0, 0), (0, 0), (0, 0), (0, pad), (0, 0)))
    qs = qs.reshape(bsz, dilation, heads, nb, blk, hd)

    def windows(t):
        tp = jnp.pad(to_sub(t), ((0, 0), (0, 0), (0, 0), (blk, pad + blk), (0, 0)))
        tp = tp.reshape(bsz, dilation, heads, nb + 2, blk, hd)
        return jnp.concatenate([tp[:, :, :, :-2], tp[:, :, :, 1:-1], tp[:, :, :, 2:]], axis=4)

    kw, vw = windows(k), windows(v)
    bias = jnp.transpose(rel_bias[buckets], (2, 0, 1)).astype(jnp.float32)
    logits = jnp.einsum('brhnqd,brhnkd->brhnqk', qs, kw,
                        preferred_element_type=jnp.float32) * (hd ** -0.5)
    logits = jnp.where(valid[None, None, None], logits + bias[None, None, :, None], NEG_INF)
    m = jnp.max(logits, axis=-1, keepdims=True)
    p = jnp.exp(logits - m)
    s = jnp.sum(p, axis=-1, keepdims=True)
    o = jnp.einsum('brhnqk,brhnkd->brhnqd', p, vw.astype(jnp.float32)) / s
    lse = (m + jnp.log(s))[..., 0]

    o = o.reshape(bsz, dilation, heads, nb * blk, hd)[:, :, :, :length]
    o = o.transpose(0, 3, 1, 2, 4).reshape(bsz, seq, heads, hd)
    lse = lse.reshape(bsz, dilation, heads, nb * blk)[:, :, :, :length]
    lse = lse.transpose(0, 3, 1, 2).reshape(bsz, seq, heads)
    return o, lse


def _dilated_attention_mixture(q, k, v, rel_bias):
    outs, lses = zip(*[_dilated_band_attention(q, k, v, rel_bias, w, d) for w, d in DILATED_PATTERNS])
    weights = jax.nn.softmax(jnp.stack(lses), axis=0)
    return jnp.sum(weights[..., None] * jnp.stack(outs), axis=0)


def _ssm_combine(e_i, e_j):
    a_i, b_i = e_i
    a_j, b_j = e_j
    return a_j * a_i, a_j * b_i + b_j


def _s5_bidirectional(u, lam_re, lam_im, log_step, b_re, b_im, c_re, c_im, d_skip):
    bsz, seq, _ = u.shape
    f32 = jnp.float32
    u_g = u.astype(f32).reshape(bsz, seq, SSM_GROUPS, SSM_GROUP)
    lam = lax.complex(lam_re.astype(f32), lam_im.astype(f32))
    step = jnp.exp(log_step.astype(f32))[..., None]
    lam_bar = jnp.exp(lam * step)
    b = lax.complex(b_re.astype(f32), b_im.astype(f32))
    b_bar = ((lam_bar - 1.0) / lam)[..., None] * b
    c = lax.complex(c_re.astype(f32), c_im.astype(f32))
    u_c = u_g.astype(jnp.complex64)
    y = d_skip.astype(f32).reshape(SSM_GROUPS, SSM_GROUP) * u_g
    for direction, rev in ((0, False), (1, True)):
        bu = jnp.einsum('bsgc,gpc->bsgp', u_c, b_bar[direction])
        a = jnp.broadcast_to(lam_bar[direction], (1, seq, SSM_GROUPS, SSM_STATE))
        _, states = lax.associative_scan(_ssm_combine, (a, bu), axis=1, reverse=rev)
        y = y + jnp.einsum('bsgp,gcp->bsgc', states, c[direction]).real
    return y.reshape(bsz, seq, SSM_WIDTH)


def _peer(h, w_query, sub_keys, expert_u, expert_v):
    bsz, seq, dm = h.shape
    f32 = jnp.float32
    q = (h @ w_query).reshape(bsz, seq, PEER_HEADS, PEER_QUERY_DIM)
    half = PEER_QUERY_DIM // 2
    s1 = jnp.einsum('bshk,hnk->bshn', q[..., :half], sub_keys[0], preferred_element_type=f32)
    s2 = jnp.einsum('bshk,hnk->bshn', q[..., half:], sub_keys[1], preferred_element_type=f32)
    t1, i1 = lax.top_k(s1, PEER_TOPK)
    t2, i2 = lax.top_k(s2, PEER_TOPK)
    cand = (t1[..., :, None] + t2[..., None, :]).reshape(bsz, seq, PEER_HEADS, PEER_TOPK * PEER_TOPK)
    top_s, top_c = lax.top_k(cand, PEER_TOPK)
    e1 = jnp.take_along_axis(i1, top_c // PEER_TOPK, axis=-1)
    e2 = jnp.take_along_axis(i2, top_c % PEER_TOPK, axis=-1)
    experts = e1 * PEER_KEYS + e2
    gates = jax.nn.softmax(top_s, axis=-1)
    n_chunk = (bsz * seq) // PEER_CHUNK
    sel = PEER_HEADS * PEER_TOPK
    hc = h.reshape(n_chunk, PEER_CHUNK, dm)
    ec = experts.reshape(n_chunk, PEER_CHUNK, sel)
    gc = gates.reshape(n_chunk, PEER_CHUNK, sel)

    def chunk(args):
        h_c, e_c, g_c = args
        act = jax.nn.gelu(jnp.einsum('td,ted->te', h_c, expert_u[e_c], preferred_element_type=f32),
                          approximate=False)
        out = jnp.einsum('te,ted->td', g_c * act, expert_v[e_c].astype(f32))
        return out.astype(h.dtype)

    return lax.map(chunk, (hc, ec, gc)).reshape(bsz, seq, dm)


def setup_inputs(seed: int = 0) -> dict:
    key = jax.random.key(seed)
    ks = jax.random.split(key, 24)
    f32 = jnp.float32

    def nrm(k, shape, scale):
        return jax.random.normal(k, shape, f32) * scale

    mix_in = 3 * ATTN_WIDTH + SSM_WIDTH
    ssm_shape = (DEPTH, 2, SSM_GROUPS, SSM_STATE)
    return {
        "x": nrm(ks[0], (BATCH, SEQ, D_MODEL), 1.0),
        "w_in": nrm(ks[1], (DEPTH, D_MODEL, mix_in), D_MODEL ** -0.5),
        "w_out": nrm(ks[2], (DEPTH, D_MODEL, D_MODEL), D_MODEL ** -0.5),
        "rel_bias": nrm(ks[3], (REL_BUCKETS, N_ATTN_HEADS), 0.5),
        "g_attn": 1.0 + nrm(ks[4], (DEPTH, ATTN_WIDTH), 0.05),
        "g_ssm": 1.0 + nrm(ks[5], (DEPTH, SSM_WIDTH), 0.05),
        "norm_mix": 1.0 + nrm(ks[6], (DEPTH, D_MODEL), 0.05),
        "norm_ffn": 1.0 + nrm(ks[7], (DEPTH, D_MODEL), 0.05),
        "lam_re": -0.5 + nrm(ks[8], ssm_shape, 0.01),
        "lam_im": math.pi * jnp.arange(SSM_STATE, dtype=f32) + nrm(ks[9], ssm_shape, 0.01),
        "log_step": jax.random.uniform(ks[10], (DEPTH, 2, SSM_GROUPS), f32,
                                       minval=math.log(STEP_MIN), maxval=math.log(STEP_MAX)),
        "b_re": nrm(ks[11], (DEPTH, 2, SSM_GROUPS, SSM_STATE, SSM_GROUP), (2 * SSM_GROUP) ** -0.5),
        "b_im": nrm(ks[12], (DEPTH, 2, SSM_GROUPS, SSM_STATE, SSM_GROUP), (2 * SSM_GROUP) ** -0.5),
        "c_re": nrm(ks[13], (DEPTH, 2, SSM_GROUPS, SSM_GROUP, SSM_STATE), (2 * SSM_STATE) ** -0.5),
        "c_im": nrm(ks[14], (DEPTH, 2, SSM_GROUPS, SSM_GROUP, SSM_STATE), (2 * SSM_STATE) ** -0.5),
        "d_skip": nrm(ks[15], (DEPTH, SSM_WIDTH), 1.0),
        "w_glu": nrm(ks[16], (DEPTH, SSM_WIDTH, 2 * SSM_WIDTH), SSM_WIDTH ** -0.5),
        "w_query": nrm(ks[17], (DEPTH, D_MODEL, PEER_HEADS * PEER_QUERY_DIM), D_MODEL ** -0.5),
        "sub_keys": nrm(ks[18], (DEPTH, 2, PEER_HEADS, PEER_KEYS, PEER_QUERY_DIM // 2),
                         (PEER_QUERY_DIM // 2) ** -0.5),
        "expert_u": nrm(ks[19], (DEPTH, PEER_EXPERTS, D_MODEL), D_MODEL ** -0.5),
        "expert_v": nrm(ks[20], (DEPTH, PEER_EXPERTS, D_MODEL), PEER_HEADS ** -0.5),
        "norm_final": 1.0 + nrm(ks[21], (D_MODEL,), 0.05),
    }


def reference(x, w_in, w_out, rel_bias, g_attn, g_ssm, norm_mix, norm_ffn, lam_re, lam_im,
              log_step, b_re, b_im, c_re, c_im, d_skip, w_glu, w_query, sub_keys,
              expert_u, expert_v, norm_final):
    bsz, seq, _ = x.shape
    for l in range(DEPTH):
        h = _rmsnorm(x, norm_mix[l])
        proj = h @ w_in[l]
        q = proj[..., :ATTN_WIDTH].reshape(bsz, seq, N_ATTN_HEADS, HEAD_DIM)
        k = proj[..., ATTN_WIDTH:2 * ATTN_WIDTH].reshape(bsz, seq, N_ATTN_HEADS, HEAD_DIM)
        v = proj[..., 2 * ATTN_WIDTH:3 * ATTN_WIDTH].reshape(bsz, seq, N_ATTN_HEADS, HEAD_DIM)
        u = proj[..., 3 * ATTN_WIDTH:]

        attn = _dilated_attention_mixture(q, k, v, rel_bias)
        attn = attn.reshape(bsz, seq, ATTN_WIDTH).astype(x.dtype)

        y = jax.nn.gelu(_s5_bidirectional(u, lam_re[l], lam_im[l], log_step[l], b_re[l], b_im[l],
                                          c_re[l], c_im[l], d_skip[l]).astype(x.dtype),
                        approximate=False)
        ab = y @ w_glu[l]
        ssm = ab[..., :SSM_WIDTH] * jax.nn.sigmoid(ab[..., SSM_WIDTH:])

        mix = jnp.concatenate([_rmsnorm(attn, g_attn[l]), _rmsnorm(ssm, g_ssm[l])], axis=-1)
        x = x + mix @ w_out[l]

        x = x + _peer(_rmsnorm(x, norm_ffn[l]), w_query[l], sub_keys[l], expert_u[l], expert_v[l])
    return _rmsnorm(x, norm_final)
```

```python
import functools
import math

import numpy as np
import jax
import jax.numpy as jnp
from jax import lax
from jax.experimental import pallas as pl
from jax.experimental.pallas import tpu as pltpu

F32 = jnp.float32
BF16 = jnp.bfloat16

EPS = 1e-6
NEG_INF = -1e30
HEAD_DIM = 64
ATTN_WIDTH = 512
SSM_WIDTH = 512
SSM_GROUP = 16
SSM_STATE = 64
DILATED_PATTERNS = ((128, 1), (512, 4), (2048, 16))
REL_BUCKETS = 32
REL_MAX_DISTANCE = 1024
PEER_HEADS = 8
PEER_KEYS = 128
PEER_TOPK = 16
PEER_SEL = PEER_HEADS * PEER_TOPK

LANES = 128
SUBLANES = 8
QBLK = 128
KWIN = 256
BAND = 64
VMEM_LIMIT = 52 * 1024 * 1024


def _cparams(sem, vmem=None):
    return pltpu.CompilerParams(dimension_semantics=sem, vmem_limit_bytes=vmem)


def _rms(x, g):
    return x * lax.rsqrt(jnp.mean(x * x, axis=-1, keepdims=True) + EPS) * g


def _gelu(x):
    return 0.5 * x * (1.0 + lax.erf(x * (1.0 / math.sqrt(2.0))))


def _in_proj_kernel(*refs, has_prev):
    if has_prev:
        x_ref, p_ref, g_ref, w_ref, xo_ref, q_ref, k_ref, v_ref, u_ref = refs
        x = x_ref[0] + p_ref[0]
    else:
        x_ref, g_ref, w_ref, xo_ref, q_ref, k_ref, v_ref, u_ref = refs
        x = x_ref[0]
    xo_ref[0] = x
    h = _rms(x, g_ref[...]).astype(BF16)
    proj = jnp.dot(h, w_ref[...], preferred_element_type=F32)
    a = ATTN_WIDTH
    q_ref[0] = (proj[:, :a] * (HEAD_DIM ** -0.5)).astype(BF16)
    k_ref[0] = proj[:, a:2 * a].astype(BF16)
    v_ref[0] = proj[:, 2 * a:3 * a].astype(BF16)
    u_ref[...] = proj[:, 3 * a:]


def _in_proj(x, prev, g, w_bf16, *, ts=512):
    bsz, seq, dm = x.shape
    ts = min(ts, seq)
    row = pl.BlockSpec((1, ts, dm), lambda b, i: (b, i, 0))
    qkv = pl.BlockSpec((1, ts, ATTN_WIDTH), lambda b, i: (b, i, 0))
    ins = [x] + ([prev] if prev is not None else []) + [g.reshape(1, dm), w_bf16]
    in_specs = [row] + ([row] if prev is not None else []) + [
        pl.BlockSpec((1, dm), lambda b, i: (0, 0)),
        pl.BlockSpec(w_bf16.shape, lambda b, i: (0, 0)),
    ]
    return pl.pallas_call(
        functools.partial(_in_proj_kernel, has_prev=prev is not None),
        grid=(bsz, seq // ts),
        in_specs=in_specs,
        out_specs=[row, qkv, qkv, qkv, pl.BlockSpec((ts, SSM_WIDTH), lambda b, i: (i, b))],
        out_shape=[
            jax.ShapeDtypeStruct((bsz, seq, dm), F32),
            jax.ShapeDtypeStruct((bsz, seq, ATTN_WIDTH), BF16),
            jax.ShapeDtypeStruct((bsz, seq, ATTN_WIDTH), BF16),
            jax.ShapeDtypeStruct((bsz, seq, ATTN_WIDTH), BF16),
            jax.ShapeDtypeStruct((seq, bsz * SSM_WIDTH), F32),
        ],
        compiler_params=_cparams(("parallel", "arbitrary"), VMEM_LIMIT),
        name="in_proj",
    )(*ins)


def _t5_buckets(rel):
    half = REL_BUCKETS // 2
    max_exact = half // 2
    n = np.abs(rel)
    large = max_exact + (np.log(np.maximum(n, 1) / max_exact)
                         / np.log(REL_MAX_DISTANCE / max_exact) * (half - max_exact)).astype(np.int32)
    large = np.minimum(large, half - 1)
    return (np.where(rel > 0, half, 0) + np.where(n < max_exact, n, large)).astype(np.int32)


def _attn_bias_tables(rel_bias, dilation):
    ql = np.arange(QBLK)[:, None]
    kl = np.arange(KWIN)[None, :]
    tabs = []
    for off in (0, -BAND, -2 * BAND):
        delta = kl + off - ql
        valid = np.abs(delta) <= BAND
        b = rel_bias.astype(F32)[_t5_buckets(delta * dilation)]
        b = jnp.where(valid[..., None], b, NEG_INF)
        tabs.append(jnp.transpose(b, (2, 0, 1)))
    return jnp.stack(tabs, axis=1)


def _attn_kernel(q_ref, k_ref, v_ref, bias_ref, o_ref, lse_ref, *, length):
    nblk = length // QBLK
    lane = lax.broadcasted_iota(jnp.int32, (QBLK, LANES), 1)
    is_h0 = lane < HEAD_DIM
    dn = (((1,), (1,)), ((), ()))

    def body(i, carry):
        s = pl.multiple_of(i * QBLK, QBLK)
        ks = pl.multiple_of(jnp.clip(s - BAND, 0, length - KWIN), BAND)
        var = jnp.where(i == 0, 0, jnp.where(i == nblk - 1, 2, 1))
        qb = q_ref[0, pl.ds(s, QBLK), :]
        kb = k_ref[0, pl.ds(ks, KWIN), :]
        vb = v_ref[0, pl.ds(ks, KWIN), :]
        outs, lses = [], []
        for h in range(2):
            keep = is_h0 if h == 0 else jnp.logical_not(is_h0)
            qh = jnp.where(keep, qb, jnp.zeros_like(qb))
            logits = lax.dot_general(qh, kb, dn, preferred_element_type=F32) + bias_ref[h, var]
            m = jnp.max(logits, axis=-1, keepdims=True)
            p = jnp.exp(logits - m)
            ssum = jnp.sum(p, axis=-1, keepdims=True)
            o = jnp.dot(p.astype(BF16), vb, preferred_element_type=F32) / ssum
            outs.append(o)
            lses.append(jnp.broadcast_to(m + jnp.log(ssum), (QBLK, LANES)))
        o_ref[0, pl.ds(s, QBLK), :] = jnp.where(is_h0, outs[0], outs[1])
        lse_ref[0, pl.ds(s, QBLK), :] = jnp.where(is_h0, lses[0], lses[1])
        return carry

    lax.fori_loop(0, nblk, body, 0)


def _attn_pattern(q, k, v, rel_bias, dilation):
    bsz, seq, width = q.shape
    length = seq // dilation
    assert length % QBLK == 0 and length >= KWIN
    cols = dilation * width // LANES
    view = lambda t: t.reshape(bsz, length, dilation * width)
    bias = _attn_bias_tables(rel_bias, dilation)
    blk = pl.BlockSpec((1, length, LANES), lambda b, c: (b, 0, c))
    o, lse = pl.pallas_call(
        functools.partial(_attn_kernel, length=length),
        grid=(bsz, cols),
        in_specs=[blk, blk, blk,
                  pl.BlockSpec((2, 3, QBLK, KWIN), lambda b, c: (c % (width // LANES), 0, 0, 0))],
        out_specs=[blk, blk],
        out_shape=[jax.ShapeDtypeStruct((bsz, length, dilation * width), F32)] * 2,
        compiler_params=_cparams(("parallel", "arbitrary"), VMEM_LIMIT),
        name=f"attn_d{dilation}",
    )(view(q), view(k), view(v), bias)
    return o.reshape(bsz, seq, width), lse.reshape(bsz, seq, width)


SSM_LANE_GROUPS = SSM_WIDTH // LANES
SSM_GB_STATES = (LANES // SSM_GROUP) * SSM_STATE


def _ssm_params(lam_re, lam_im, log_step, b_re, b_im, c_re, c_im):
    f = lambda t: t.astype(F32)
    lr, li = f(lam_re), f(lam_im)
    step = jnp.exp(f(log_step))[..., None]
    mag = jnp.exp(lr * step)
    ar, ai = mag * jnp.cos(li * step), mag * jnp.sin(li * step)
    nr, ni = ar - 1.0, ai
    den = lr * lr + li * li
    cr, ci = (nr * lr + ni * li) / den, (ni * lr - nr * li) / den
    br, bi = f(b_re), f(b_im)
    bbr = cr[..., None] * br - ci[..., None] * bi
    bbi = cr[..., None] * bi + ci[..., None] * br
    gpb = LANES // SSM_GROUP
    eye = jnp.eye(gpb, dtype=F32)

    def in_map(t):
        t = t.reshape(2, SSM_LANE_GROUPS, gpb, SSM_STATE, SSM_GROUP)
        return jnp.einsum('dbgpc,gh->dbgchp', t, eye).reshape(2, SSM_LANE_GROUPS, LANES, SSM_GB_STATES)

    def out_map(t):
        t = t.reshape(2, SSM_LANE_GROUPS, gpb, SSM_GROUP, SSM_STATE)
        return jnp.einsum('dbgcp,gh->dbgphc', t, eye).reshape(2, SSM_LANE_GROUPS, SSM_GB_STATES, LANES)

    bw = jnp.concatenate([in_map(bbr), in_map(bbi)], axis=-1).astype(BF16)
    cw = jnp.concatenate([out_map(f(c_re)), -out_map(f(c_im))], axis=-2).astype(BF16)
    a = jnp.stack([ar.reshape(2, SSM_LANE_GROUPS, SSM_GB_STATES),
                   ai.reshape(2, SSM_LANE_GROUPS, SSM_GB_STATES)], axis=2)
    return a, bw, cw


def _ssm_kernel(u_ref, a_ref, bw_ref, cw_ref, y_ref, st_ref, bu_ref, *, ts, bsz):
    d = pl.program_id(0)
    ns = SSM_GB_STATES

    @pl.when(pl.program_id(1) == 0)
    def _():
        st_ref[...] = jnp.zeros_like(st_ref)

    for gb in range(SSM_LANE_GROUPS):
        ub = u_ref[:, gb * LANES:(gb + 1) * LANES].astype(BF16)
        bu_ref[...] = jnp.dot(ub, bw_ref[0, gb], preferred_element_type=F32)
        ar = jnp.broadcast_to(a_ref[0, gb, 0:1, :], (bsz, ns))
        ai = jnp.broadcast_to(a_ref[0, gb, 1:2, :], (bsz, ns))

        def step(j, carry, ar=ar, ai=ai):
            xr, xi = carry
            tl = jnp.where(d == 0, j, ts - 1 - j)
            r = pl.multiple_of(tl * bsz, bsz)
            nr = ar * xr - ai * xi + bu_ref[pl.ds(r, bsz), :ns]
            ni = ar * xi + ai * xr + bu_ref[pl.ds(r, bsz), ns:]
            bu_ref[pl.ds(r, bsz), :ns] = nr
            bu_ref[pl.ds(r, bsz), ns:] = ni
            return nr, ni

        xr, xi = lax.fori_loop(0, ts, step, (st_ref[gb, :, :ns], st_ref[gb, :, ns:]))
        st_ref[gb, :, :ns] = xr
        st_ref[gb, :, ns:] = xi
        y_ref[0, :, gb * LANES:(gb + 1) * LANES] = jnp.dot(
            bu_ref[...].astype(BF16), cw_ref[0, gb], preferred_element_type=F32)


def _ssm_scan(u_tm, a, bw, cw, *, bsz, ts=64):
    rows, width = u_tm.shape
    seq = rows // bsz
    ts = min(ts, seq)
    nt = seq // ts
    tblk = lambda d, i: jnp.where(d == 0, i, nt - 1 - i)
    return pl.pallas_call(
        functools.partial(_ssm_kernel, ts=ts, bsz=bsz),
        grid=(2, nt),
        in_specs=[
            pl.BlockSpec((ts * bsz, width), lambda d, i: (tblk(d, i), 0)),
            pl.BlockSpec((1,) + a.shape[1:], lambda d, i: (d, 0, 0, 0)),
            pl.BlockSpec((1,) + bw.shape[1:], lambda d, i: (d, 0, 0, 0)),
            pl.BlockSpec((1,) + cw.shape[1:], lambda d, i: (d, 0, 0, 0)),
        ],
        out_specs=pl.BlockSpec((1, ts * bsz, width), lambda d, i: (d, tblk(d, i), 0)),
        out_shape=jax.ShapeDtypeStruct((2, rows, width), F32),
        scratch_shapes=[pltpu.VMEM((SSM_LANE_GROUPS, bsz, 2 * SSM_GB_STATES), F32),
                        pltpu.VMEM((ts * bsz, 2 * SSM_GB_STATES), F32)],
        compiler_params=_cparams(("arbitrary", "arbitrary"), VMEM_LIMIT),
        name="ssm_scan",
    )(u_tm, a, bw, cw)


def _ssm_post_kernel(y_ref, u_ref, d_ref, wg_ref, g_ref, wo_ref, z_ref):
    y = _gelu(y_ref[0] + y_ref[1] + d_ref[...] * u_ref[...]).astype(BF16)
    ab = jnp.dot(y, wg_ref[...], preferred_element_type=F32)
    ssm = ab[:, :SSM_WIDTH] * jax.nn.sigmoid(ab[:, SSM_WIDTH:])
    n = _rms(ssm, g_ref[...]).astype(BF16)
    z_ref[...] = jnp.dot(n, wo_ref[...], preferred_element_type=F32)


def _ssm_post(y, u_tm, d_skip, w_glu_bf16, g_ssm, w_out_ssm_bf16, *, tr=512):
    rows, width = u_tm.shape
    tr = min(tr, rows)
    dm = w_out_ssm_bf16.shape[1]
    full = lambda a: pl.BlockSpec(a.shape, lambda i: (0,) * a.ndim)
    d2, g2 = d_skip.reshape(1, width), g_ssm.reshape(1, width)
    return pl.pallas_call(
        _ssm_post_kernel,
        grid=(rows // tr,),
        in_specs=[pl.BlockSpec((2, tr, width), lambda i: (0, i, 0)),
                  pl.BlockSpec((tr, width), lambda i: (i, 0)),
                  full(d2), full(w_glu_bf16), full(g2), full(w_out_ssm_bf16)],
        out_specs=pl.BlockSpec((tr, dm), lambda i: (i, 0)),
        out_shape=jax.ShapeDtypeStruct((rows, dm), F32),
        compiler_params=_cparams(("parallel",), VMEM_LIMIT),
        name="ssm_post",
    )(y, u_tm, d2, w_glu_bf16, g2, w_out_ssm_bf16)


def _mix_out_kernel(x_ref, z_ref, o1_ref, o2_ref, o3_ref, l1_ref, l2_ref, l3_ref,
                    ga_ref, wo_ref, gf_ref, wq_ref, xn_ref, h_ref, q_ref):
    l1, l2, l3 = l1_ref[0], l2_ref[0], l3_ref[0]
    m = jnp.maximum(jnp.maximum(l1, l2), l3)
    w1, w2, w3 = jnp.exp(l1 - m), jnp.exp(l2 - m), jnp.exp(l3 - m)
    attn = (w1 * o1_ref[0] + w2 * o2_ref[0] + w3 * o3_ref[0]) / (w1 + w2 + w3)
    n = _rms(attn, ga_ref[...]).astype(BF16)
    xn = x_ref[0] + z_ref[...] + jnp.dot(n, wo_ref[...], preferred_element_type=F32)
    xn_ref[0] = xn
    h = _rms(xn, gf_ref[...])
    h_ref[0] = h
    qp = jnp.dot(h, wq_ref[...], preferred_element_type=F32, precision=lax.Precision.HIGHEST)
    for hd in range(PEER_HEADS):
        q_ref[hd] = qp[:, hd * LANES:(hd + 1) * LANES]


def _mix_out(x, z_tm, os_, ls_, g_attn, w_out_attn_bf16, norm_ffn, w_query, *, ts=256):
    bsz, seq, dm = x.shape
    ts = min(ts, seq)
    ns = seq // ts
    row = pl.BlockSpec((1, ts, dm), lambda b, i: (b, i, 0))
    half = pl.BlockSpec((1, ts, ATTN_WIDTH), lambda b, i: (b, i, 0))
    full = lambda a: pl.BlockSpec(a.shape, lambda b, i: (0,) * a.ndim)
    ga, gf = g_attn.reshape(1, ATTN_WIDTH), norm_ffn.reshape(1, dm)
    qdim = w_query.shape[1] // PEER_HEADS
    return pl.pallas_call(
        _mix_out_kernel,
        grid=(bsz, ns),
        in_specs=[row, pl.BlockSpec((ts, dm), lambda b, i: (i, b))] + [half] * 6
                 + [full(ga), full(w_out_attn_bf16), full(gf), full(w_query)],
        out_specs=[row, row, pl.BlockSpec((PEER_HEADS, ts, qdim), lambda b, i: (0, b * ns + i, 0))],
        out_shape=[jax.ShapeDtypeStruct((bsz, seq, dm), F32),
                   jax.ShapeDtypeStruct((bsz, seq, dm), F32),
                   jax.ShapeDtypeStruct((PEER_HEADS, bsz * seq, qdim), F32)],
        compiler_params=_cparams(("parallel", "arbitrary"), VMEM_LIMIT),
        name="mix_out",
    )(x, z_tm, *os_, *ls_, ga, w_out_attn_bf16, gf, w_query)


def _top16(s, payload=None):
    n = s.shape[0]
    iota = lax.broadcasted_iota(jnp.int32, s.shape, 0)
    vals, picks = [], []
    for _ in range(PEER_TOPK):
        m = jnp.max(s, axis=0, keepdims=True)
        am = jnp.min(jnp.where(s == m, iota, n), axis=0, keepdims=True)
        hit = iota == am
        vals.append(m)
        if payload is None:
            picks.append(am)
        else:
            picks.append(jnp.max(jnp.where(hit, payload, -1), axis=0, keepdims=True))
        s = jnp.where(hit, -jnp.inf, s)
    return jnp.concatenate(vals, axis=0), jnp.concatenate(picks, axis=0)


def _peer_topk_kernel(q_ref, k_ref, idx_ref, gate_ref):
    tt = q_ref.shape[1]
    dn = (((1,), (1,)), ((), ()))
    hp = lax.Precision.HIGHEST

    def head(h, carry):
        q = q_ref[h]
        s1 = lax.dot_general(k_ref[0, h], q, dn, preferred_element_type=F32, precision=hp)
        s2 = lax.dot_general(k_ref[1, h], q, dn, preferred_element_type=F32, precision=hp)
        t1, i1 = _top16(s1)
        t2, i2 = _top16(s2)
        cand = (t1[:, None, :] + t2[None, :, :]).reshape(PEER_TOPK * PEER_TOPK, tt)
        pay = (i1[:, None, :] * PEER_KEYS + i2[None, :, :]).reshape(PEER_TOPK * PEER_TOPK, tt)
        top_s, experts = _top16(cand, pay)
        e = jnp.exp(top_s - jnp.max(top_s, axis=0, keepdims=True))
        r = pl.multiple_of(h * PEER_TOPK, PEER_TOPK)
        gate_ref[pl.ds(r, PEER_TOPK), :] = e / jnp.sum(e, axis=0, keepdims=True)
        idx_ref[pl.ds(r, PEER_TOPK), :] = experts * ROWS_PER_EXPERT
        return carry

    lax.fori_loop(0, PEER_HEADS, head, 0)


def _peer_topk(qp, keys_pad, *, tt=256):
    n_tok = qp.shape[1]
    tt = min(tt, n_tok)
    out = pl.BlockSpec((PEER_SEL, tt), lambda i: (0, i))
    return pl.pallas_call(
        _peer_topk_kernel,
        grid=(n_tok // tt,),
        in_specs=[pl.BlockSpec((PEER_HEADS, tt, qp.shape[2]), lambda i: (0, i, 0)),
                  pl.BlockSpec(keys_pad.shape, lambda i: (0, 0, 0, 0))],
        out_specs=[out, out],
        out_shape=[jax.ShapeDtypeStruct((PEER_SEL, n_tok), jnp.int32),
                   jax.ShapeDtypeStruct((PEER_SEL, n_tok), F32)],
        compiler_params=_cparams(("parallel",), VMEM_LIMIT),
        name="peer_topk",
    )(qp, keys_pad)


ROWS_PER_EXPERT = 4
PAIR_CHUNK = 8
N_CHUNKS = PEER_SEL // PAIR_CHUNK
_CHUNK_SLOT = tuple(
    2 * (2 * (0 if (j % 2) else 1) + (1 if (j // 2) in (0, 2) else 0)) + (0 if (j // 2) < 2 else 1)
    for j in range(PAIR_CHUNK))


def _pack_table(tbl):
    e, dm = tbl.shape
    tb = tbl.astype(BF16)
    lo = lax.bitcast_convert_type(tb[:, :dm // 2], jnp.uint16).astype(jnp.uint32)
    hi = lax.bitcast_convert_type(tb[:, dm // 2:], jnp.uint16).astype(jnp.uint32)
    w = lax.bitcast_convert_type(lo | (hi << 16), jnp.int32)
    return w.reshape(e * ROWS_PER_EXPERT, LANES)


def _unpack_words(w):
    lo = lax.bitcast_convert_type(w << 16, F32)
    hi = lax.bitcast_convert_type(w & jnp.int32(-65536), F32)
    return lo, hi


def _aligned(v, m):
    return v if isinstance(v, int) else pl.multiple_of(v, m)


def _gather_chunk(idx_ref, tab_ref, g_ref, t, kc):
    base = _aligned(kc * (PAIR_CHUNK * ROWS_PER_EXPERT), PAIR_CHUNK * ROWS_PER_EXPERT)
    flat = t * PEER_SEL + kc * PAIR_CHUNK
    for j in range(PAIR_CHUNK):
        e4 = pl.multiple_of(idx_ref[flat + j], ROWS_PER_EXPERT)
        g_ref[pl.ds(base + _CHUNK_SLOT[j] * ROWS_PER_EXPERT, ROWS_PER_EXPERT), :] = (
            tab_ref[pl.ds(e4, ROWS_PER_EXPERT), :])


def _peer_u_kernel(idx_ref, ha_ref, hb_ref, gate_ref, tab_ref, coef_ref, g_ref, r_ref, d_ref, *, tt):
    sub = lax.broadcasted_iota(jnp.int32, (SUBLANES, LANES), 0)
    m_a = ((sub % 4) >= 2)[None]
    m_b = (sub % 2) == 1
    rows = PAIR_CHUNK * ROWS_PER_EXPERT

    def tok(t, carry):
        ha = ha_ref[t][None]
        hb = hb_ref[t][None]

        def reduce_chunk(kc):
            lo, hi = _unpack_words(g_ref[pl.ds(_aligned(kc * rows, rows), rows), :])
            x = (lo.reshape(4, 8, LANES) * ha + hi.reshape(4, 8, LANES) * hb).reshape(2, 2, 8, LANES)
            xe, xo = x[:, 0], x[:, 1]
            a = xe + pltpu.roll(xe, 2, axis=1)
            b = xo + pltpu.roll(xo, 6, axis=1)
            m = jnp.where(m_a, a, b)
            a2 = m[0] + pltpu.roll(m[0], 1, axis=0)
            b2 = m[1] + pltpu.roll(m[1], 7, axis=0)
            r_ref[t, pl.ds(_aligned(kc * PAIR_CHUNK, PAIR_CHUNK), PAIR_CHUNK), :] = jnp.where(m_b, a2, b2)

        _gather_chunk(idx_ref, tab_ref, g_ref, t, 0)

        def chunk(kc, c):
            reduce_chunk(kc - 1)
            _gather_chunk(idx_ref, tab_ref, g_ref, t, kc)
            return c

        lax.fori_loop(1, N_CHUNKS, chunk, 0)
        reduce_chunk(N_CHUNKS - 1)
        return carry

    lax.fori_loop(0, tt, tok, 0)

    eye = (lax.broadcasted_iota(jnp.int32, (PEER_SEL, LANES), 0)
           == lax.broadcasted_iota(jnp.int32, (PEER_SEL, LANES), 1))

    def lane_sums(g, carry):
        for i in range(SUBLANES):
            t = g * SUBLANES + i
            s = jnp.sum(r_ref[t], axis=-1, keepdims=True)
            d_ref[pl.ds(t, 1), :] = jnp.sum(jnp.where(eye, s, 0.0), axis=0, keepdims=True)
        return carry

    lax.fori_loop(0, tt // SUBLANES, lane_sums, 0)
    coef_ref[...] = gate_ref[...] * _gelu(d_ref[...])


def _peer_u(idx4, ha, hb, gates, tab, *, tt=128):
    n_tok = ha.shape[0]
    tt = min(tt, n_tok)
    tok2 = pl.BlockSpec((tt, PEER_SEL), lambda i: (i, 0))
    tok3 = pl.BlockSpec((tt, SUBLANES, LANES), lambda i: (i, 0, 0))
    return pl.pallas_call(
        functools.partial(_peer_u_kernel, tt=tt),
        grid=(n_tok // tt,),
        in_specs=[pl.BlockSpec((tt * PEER_SEL,), lambda i: (i,), memory_space=pltpu.SMEM),
                  tok3, tok3, tok2,
                  pl.BlockSpec(tab.shape, lambda i: (0, 0), pipeline_mode=pl.Buffered(1))],
        out_specs=tok2,
        out_shape=jax.ShapeDtypeStruct((n_tok, PEER_SEL), F32),
        scratch_shapes=[pltpu.VMEM((PEER_SEL * ROWS_PER_EXPERT, LANES), jnp.int32),
                        pltpu.VMEM((tt, PEER_SEL, LANES), F32),
                        pltpu.VMEM((tt, PEER_SEL), F32)],
        compiler_params=_cparams(("arbitrary",), VMEM_LIMIT),
        name="peer_u",
    )(idx4, ha, hb, gates, tab)


def _peer_v_kernel(idx_ref, coef_ref, tab_ref, o_ref, g_ref, *, tt):
    sub = lax.broadcasted_iota(jnp.int32, (SUBLANES, LANES), 0)
    low = sub < ROWS_PER_EXPERT
    rows = PAIR_CHUNK * ROWS_PER_EXPERT
    nv = PAIR_CHUNK // 2
    pair_at = {s: j for j, s in enumerate(_CHUNK_SLOT)}

    def tok(t, carry):
        def acc_chunk(kc, acc):
            acc_lo, acc_hi = acc
            lo, hi = _unpack_words(g_ref[pl.ds(_aligned(kc * rows, rows), rows), :])
            lo = lo.reshape(nv, SUBLANES, LANES)
            hi = hi.reshape(nv, SUBLANES, LANES)
            flat = t * PEER_SEL + kc * PAIR_CHUNK
            for v in range(nv):
                c0 = coef_ref[flat + pair_at[2 * v]]
                c1 = coef_ref[flat + pair_at[2 * v + 1]]
                cv = jnp.where(low, c0, c1)
                acc_lo = acc_lo + cv * lo[v]
                acc_hi = acc_hi + cv * hi[v]
            return acc_lo, acc_hi

        _gather_chunk(idx_ref, tab_ref, g_ref, t, 0)

        def chunk(kc, acc):
            acc = acc_chunk(kc - 1, acc)
            _gather_chunk(idx_ref, tab_ref, g_ref, t, kc)
            return acc

        z = jnp.zeros((SUBLANES, LANES), F32)
        acc = lax.fori_loop(1, N_CHUNKS, chunk, (z, z))
        acc_lo, acc_hi = acc_chunk(N_CHUNKS - 1, acc)
        lo4 = acc_lo + pltpu.roll(acc_lo, ROWS_PER_EXPERT, axis=0)
        hi4 = acc_hi + pltpu.roll(acc_hi, ROWS_PER_EXPERT, axis=0)
        o_ref[t] = jnp.where(low, lo4, hi4)
        return carry

    lax.fori_loop(0, tt, tok, 0)


def _peer_v(idx4, coef, tab, *, tt=128):
    n_tok = coef.shape[0] // PEER_SEL
    tt = min(tt, n_tok)
    smem = pl.BlockSpec((tt * PEER_SEL,), lambda i: (i,), memory_space=pltpu.SMEM)
    return pl.pallas_call(
        functools.partial(_peer_v_kernel, tt=tt),
        grid=(n_tok // tt,),
        in_specs=[smem, smem,
                  pl.BlockSpec(tab.shape, lambda i: (0, 0), pipeline_mode=pl.Buffered(1))],
        out_specs=pl.BlockSpec((tt, SUBLANES, LANES), lambda i: (i, 0, 0)),
        out_shape=jax.ShapeDtypeStruct((n_tok, SUBLANES, LANES), F32),
        scratch_shapes=[pltpu.VMEM((PEER_SEL * ROWS_PER_EXPERT, LANES), jnp.int32)],
        compiler_params=_cparams(("arbitrary",), VMEM_LIMIT),
        name="peer_v",
    )(idx4, coef, tab)


def _peer(h, qp, sub_keys, expert_u, expert_v):
    bsz, seq, dm = h.shape
    n_tok = bsz * seq
    half = sub_keys.shape[-1]
    z = jnp.zeros_like(sub_keys[0])
    keys_pad = jnp.stack([jnp.concatenate([sub_keys[0], z], axis=-1),
                          jnp.concatenate([z, sub_keys[1]], axis=-1)]).astype(F32)
    assert keys_pad.shape[-1] == 2 * half == qp.shape[-1]
    idx_t, gate_t = _peer_topk(qp, keys_pad)
    idx4, gates = idx_t.T, gate_t.T
    hr = h.reshape(n_tok, 2, dm // (2 * LANES), LANES)
    ha = jnp.concatenate([hr[:, 0], hr[:, 0]], axis=1)
    hb = jnp.concatenate([hr[:, 1], hr[:, 1]], axis=1)
    idx4 = idx4.reshape(n_tok * PEER_SEL)
    coef = _peer_u(idx4, ha, hb, gates, _pack_table(expert_u))
    out = _peer_v(idx4, coef.reshape(n_tok * PEER_SEL), _pack_table(expert_v))
    return out.reshape(bsz, seq, dm)


def _final_kernel(x_ref, p_ref, g_ref, o_ref):
    o_ref[...] = _rms(x_ref[...] + p_ref[...], g_ref[...])


def _final_norm(x, p, g, *, tr=1024):
    rows, dm = x.shape
    tr = min(tr, rows)
    blk = pl.BlockSpec((tr, dm), lambda i: (i, 0))
    return pl.pallas_call(
        _final_kernel,
        grid=(rows // tr,),
        in_specs=[blk, blk, pl.BlockSpec((1, dm), lambda i: (0, 0))],
        out_specs=blk,
        out_shape=jax.ShapeDtypeStruct((rows, dm), F32),
        compiler_params=_cparams(("parallel",), VMEM_LIMIT),
        name="final_norm",
    )(x, p, g.reshape(1, dm))


def kernel(x, w_in, w_out, rel_bias, g_attn, g_ssm, norm_mix, norm_ffn, lam_re, lam_im, log_step, b_re, b_im, c_re, c_im, d_skip, w_glu, w_query, sub_keys, expert_u, expert_v, norm_final):
    bsz, seq, dm = x.shape
    depth = w_in.shape[0]
    prev = None
    for l in range(depth):
        x, q, k, v, u = _in_proj(x, prev, norm_mix[l], w_in[l].astype(BF16))
        attn = [_attn_pattern(q, k, v, rel_bias, d) for _, d in DILATED_PATTERNS]
        u_tm = u.reshape(seq * bsz, SSM_WIDTH)
        a, bw, cw = _ssm_params(lam_re[l], lam_im[l], log_step[l], b_re[l], b_im[l], c_re[l], c_im[l])
        y = _ssm_scan(u_tm, a, bw, cw, bsz=bsz)
        wo = w_out[l].astype(BF16)
        z = _ssm_post(y, u_tm, d_skip[l], w_glu[l].astype(BF16), g_ssm[l], wo[ATTN_WIDTH:])
        x, h, qp = _mix_out(x, z.reshape(seq, bsz * dm), [o for o, _ in attn], [s for _, s in attn],
                            g_attn[l], wo[:ATTN_WIDTH], norm_ffn[l], w_query[l])
        prev = _peer(h, qp, sub_keys[l], expert_u[l], expert_v[l])
    out = _final_norm(x.reshape(bsz * seq, dm), prev.reshape(bsz * seq, dm), norm_final)
    return out.reshape(bsz, seq, dm)
```

```python
import functools
import math

import numpy as np
import jax
import jax.numpy as jnp
from jax import lax
from jax.experimental import pallas as pl
from jax.experimental.pallas import tpu as pltpu

F32 = jnp.float32
BF16 = jnp.bfloat16

EPS = 1e-6
NEG_INF = -1e30
HEAD_DIM = 64
ATTN_WIDTH = 512
SSM_WIDTH = 512
SSM_GROUP = 16
SSM_STATE = 64
DILATED_PATTERNS = ((128, 1), (512, 4), (2048, 16))
REL_BUCKETS = 32
REL_MAX_DISTANCE = 1024
PEER_HEADS = 8
PEER_KEYS = 128
PEER_TOPK = 16
PEER_SEL = PEER_HEADS * PEER_TOPK

LANES = 128
SUBLANES = 8
QBLK = 128
KWIN = 256
BAND = 64
VMEM_LIMIT = 52 * 1024 * 1024


def _cparams(sem, vmem=None):
    return pltpu.CompilerParams(dimension_semantics=sem, vmem_limit_bytes=vmem)


def _rms(x, g):
    return x * lax.rsqrt(jnp.mean(x * x, axis=-1, keepdims=True) + EPS) * g


def _gelu(x):
    return 0.5 * x * (1.0 + lax.erf(x * (1.0 / math.sqrt(2.0))))


def _in_proj_kernel(*refs, has_prev):
    if has_prev:
        x_ref, p_ref, g_ref, w_ref, xo_ref, q_ref, k_ref, v_ref, u_ref = refs
        x = x_ref[0] + p_ref[0]
    else:
        x_ref, g_ref, w_ref, xo_ref, q_ref, k_ref, v_ref, u_ref = refs
        x = x_ref[0]
    xo_ref[0] = x
    h = _rms(x, g_ref[...]).astype(BF16)
    proj = jnp.dot(h, w_ref[...], preferred_element_type=F32)
    a = ATTN_WIDTH
    q_ref[0] = (proj[:, :a] * (HEAD_DIM ** -0.5)).astype(BF16)
    k_ref[0] = proj[:, a:2 * a].astype(BF16)
    v_ref[0] = proj[:, 2 * a:3 * a].astype(BF16)
    u_ref[...] = proj[:, 3 * a:]


def _in_proj(x, prev, g, w_bf16, *, ts=512):
    bsz, seq, dm = x.shape
    ts = min(ts, seq)
    row = pl.BlockSpec((1, ts, dm), lambda b, i: (b, i, 0))
    qkv = pl.BlockSpec((1, ts, ATTN_WIDTH), lambda b, i: (b, i, 0))
    ins = [x] + ([prev] if prev is not None else []) + [g.reshape(1, dm), w_bf16]
    in_specs = [row] + ([row] if prev is not None else []) + [
        pl.BlockSpec((1, dm), lambda b, i: (0, 0)),
        pl.BlockSpec(w_bf16.shape, lambda b, i: (0, 0)),
    ]
    return pl.pallas_call(
        functools.partial(_in_proj_kernel, has_prev=prev is not None),
        grid=(bsz, seq // ts),
        in_specs=in_specs,
        out_specs=[row, qkv, qkv, qkv, pl.BlockSpec((ts, SSM_WIDTH), lambda b, i: (i, b))],
        out_shape=[
            jax.ShapeDtypeStruct((bsz, seq, dm), F32),
            jax.ShapeDtypeStruct((bsz, seq, ATTN_WIDTH), BF16),
            jax.ShapeDtypeStruct((bsz, seq, ATTN_WIDTH), BF16),
            jax.ShapeDtypeStruct((bsz, seq, ATTN_WIDTH), BF16),
            jax.ShapeDtypeStruct((seq, bsz * SSM_WIDTH), F32),
        ],
        compiler_params=_cparams(("parallel", "arbitrary"), VMEM_LIMIT),
        name="in_proj",
    )(*ins)


def _t5_buckets(rel):
    half = REL_BUCKETS // 2
    max_exact = half // 2
    n = np.abs(rel)
    large = max_exact + (np.log(np.maximum(n, 1) / max_exact)
                         / np.log(REL_MAX_DISTANCE / max_exact) * (half - max_exact)).astype(np.int32)
    large = np.minimum(large, half - 1)
    return (np.where(rel > 0, half, 0) + np.where(n < max_exact, n, large)).astype(np.int32)


def _attn_bias_tables(rel_bias, dilation):
    ql = np.arange(QBLK)[:, None]
    kl = np.arange(KWIN)[None, :]
    tabs = []
    for off in (0, -BAND, -2 * BAND):
        delta = kl + off - ql
        valid = np.abs(delta) <= BAND
        b = rel_bias.astype(F32)[_t5_buckets(delta * dilation)]
        b = jnp.where(valid[..., None], b, NEG_INF)
        tabs.append(jnp.transpose(b, (2, 0, 1)))
    return jnp.stack(tabs, axis=1)


def _attn_kernel(q_ref, k_ref, v_ref, bias_ref, o_ref, lse_ref, *, length):
    nblk = length // QBLK
    lane = lax.broadcasted_iota(jnp.int32, (QBLK, LANES), 1)
    is_h0 = lane < HEAD_DIM
    dn = (((1,), (1,)), ((), ()))

    def body(i, carry):
        s = pl.multiple_of(i * QBLK, QBLK)
        ks = pl.multiple_of(jnp.clip(s - BAND, 0, length - KWIN), BAND)
        var = jnp.where(i == 0, 0, jnp.where(i == nblk - 1, 2, 1))
        qb = q_ref[0, pl.ds(s, QBLK), :]
        kb = k_ref[0, pl.ds(ks, KWIN), :]
        vb = v_ref[0, pl.ds(ks, KWIN), :]
        outs, lses = [], []
        for h in range(2):
            keep = is_h0 if h == 0 else jnp.logical_not(is_h0)
            qh = jnp.where(keep, qb, jnp.zeros_like(qb))
            logits = lax.dot_general(qh, kb, dn, preferred_element_type=F32) + bias_ref[h, var]
            m = jnp.max(logits, axis=-1, keepdims=True)
            p = jnp.exp(logits - m)
            ssum = jnp.sum(p, axis=-1, keepdims=True)
            o = jnp.dot(p.astype(BF16), vb, preferred_element_type=F32) / ssum
            outs.append(o)
            lses.append(jnp.broadcast_to(m + jnp.log(ssum), (QBLK, LANES)))
        o_ref[0, pl.ds(s, QBLK), :] = jnp.where(is_h0, outs[0], outs[1])
        lse_ref[0, pl.ds(s, QBLK), :] = jnp.where(is_h0, lses[0], lses[1])
        return carry

    lax.fori_loop(0, nblk, body, 0)


def _attn_pattern(q, k, v, rel_bias, dilation):
    bsz, seq, width = q.shape
    length = seq // dilation
    assert length % QBLK == 0 and length >= KWIN
    cols = dilation * width // LANES
    view = lambda t: t.reshape(bsz, length, dilation * width)
    bias = _attn_bias_tables(rel_bias, dilation)
    blk = pl.BlockSpec((1, length, LANES), lambda b, c: (b, 0, c))
    o, lse = pl.pallas_call(
        functools.partial(_attn_kernel, length=length),
        grid=(bsz, cols),
        in_specs=[blk, blk, blk,
                  pl.BlockSpec((2, 3, QBLK, KWIN), lambda b, c: (c % (width // LANES), 0, 0, 0))],
        out_specs=[blk, blk],
        out_shape=[jax.ShapeDtypeStruct((bsz, length, dilation * width), F32)] * 2,
        compiler_params=_cparams(("parallel", "arbitrary"), VMEM_LIMIT),
        name=f"attn_d{dilation}",
    )(view(q), view(k), view(v), bias)
    return o.reshape(bsz, seq, width), lse.reshape(bsz, seq, width)


SSM_LANE_GROUPS = SSM_WIDTH // LANES
SSM_GB_STATES = (LANES // SSM_GROUP) * SSM_STATE


def _ssm_params(lam_re, lam_im, log_step, b_re, b_im, c_re, c_im):
    f = lambda t: t.astype(F32)
    lr, li = f(lam_re), f(lam_im)
    step = jnp.exp(f(log_step))[..., None]
    mag = jnp.exp(lr * step)
    ar, ai = mag * jnp.cos(li * step), mag * jnp.sin(li * step)
    nr, ni = ar - 1.0, ai
    den = lr * lr + li * li
    cr, ci = (nr * lr + ni * li) / den, (ni * lr - nr * li) / den
    br, bi = f(b_re), f(b_im)
    bbr = cr[..., None] * br - ci[..., None] * bi
    bbi = cr[..., None] * bi + ci[..., None] * br
    gpb = LANES // SSM_GROUP
    eye = jnp.eye(gpb, dtype=F32)

    def in_map(t):
        t = t.reshape(2, SSM_LANE_GROUPS, gpb, SSM_STATE, SSM_GROUP)
        return jnp.einsum('dbgpc,gh->dbgchp', t, eye).reshape(2, SSM_LANE_GROUPS, LANES, SSM_GB_STATES)

    def out_map(t):
        t = t.reshape(2, SSM_LANE_GROUPS, gpb, SSM_GROUP, SSM_STATE)
        return jnp.einsum('dbgcp,gh->dbgphc', t, eye).reshape(2, SSM_LANE_GROUPS, SSM_GB_STATES, LANES)

    bw = jnp.concatenate([in_map(bbr), in_map(bbi)], axis=-1).astype(BF16)
    cw = jnp.concatenate([out_map(f(c_re)), -out_map(f(c_im))], axis=-2).astype(BF16)
    a = jnp.stack([ar.reshape(2, SSM_LANE_GROUPS, SSM_GB_STATES),
                   ai.reshape(2, SSM_LANE_GROUPS, SSM_GB_STATES)], axis=2)
    return a, bw, cw


def _ssm_kernel(u_ref, a_ref, bw_ref, cw_ref, y_ref, st_ref, bu_ref, *, ts, bsz):
    d = pl.program_id(0)
    ns = SSM_GB_STATES

    @pl.when(pl.program_id(1) == 0)
    def _():
        st_ref[...] = jnp.zeros_like(st_ref)

    for gb in range(SSM_LANE_GROUPS):
        ub = u_ref[:, gb * LANES:(gb + 1) * LANES].astype(BF16)
        bu_ref[...] = jnp.dot(ub, bw_ref[0, gb], preferred_element_type=F32)
        ar = jnp.broadcast_to(a_ref[0, gb, 0:1, :], (bsz, ns))
        ai = jnp.broadcast_to(a_ref[0, gb, 1:2, :], (bsz, ns))

        def step(j, carry, ar=ar, ai=ai):
            xr, xi = carry
            tl = jnp.where(d == 0, j, ts - 1 - j)
            r = pl.multiple_of(tl * bsz, bsz)
            nr = ar * xr - ai * xi + bu_ref[pl.ds(r, bsz), :ns]
            ni = ar * xi + ai * xr + bu_ref[pl.ds(r, bsz), ns:]
            bu_ref[pl.ds(r, bsz), :ns] = nr
            bu_ref[pl.ds(r, bsz), ns:] = ni
            return nr, ni

        xr, xi = lax.fori_loop(0, ts, step, (st_ref[gb, :, :ns], st_ref[gb, :, ns:]))
        st_ref[gb, :, :ns] = xr
        st_ref[gb, :, ns:] = xi
        y_ref[0, :, gb * LANES:(gb + 1) * LANES] = jnp.dot(
            bu_ref[...].astype(BF16), cw_ref[0, gb], preferred_element_type=F32)


def _ssm_scan(u_tm, a, bw, cw, *, bsz, ts=64):
    rows, width = u_tm.shape
    seq = rows // bsz
    ts = min(ts, seq)
    nt = seq // ts
    tblk = lambda d, i: jnp.where(d == 0, i, nt - 1 - i)
    return pl.pallas_call(
        functools.partial(_ssm_kernel, ts=ts, bsz=bsz),
        grid=(2, nt),
        in_specs=[
            pl.BlockSpec((ts * bsz, width), lambda d, i: (tblk(d, i), 0)),
            pl.BlockSpec((1,) + a.shape[1:], lambda d, i: (d, 0, 0, 0)),
            pl.BlockSpec((1,) + bw.shape[1:], lambda d, i: (d, 0, 0, 0)),
            pl.BlockSpec((1,) + cw.shape[1:], lambda d, i: (d, 0, 0, 0)),
        ],
        out_specs=pl.BlockSpec((1, ts * bsz, width), lambda d, i: (d, tblk(d, i), 0)),
        out_shape=jax.ShapeDtypeStruct((2, rows, width), F32),
        scratch_shapes=[pltpu.VMEM((SSM_LANE_GROUPS, bsz, 2 * SSM_GB_STATES), F32),
                        pltpu.VMEM((ts * bsz, 2 * SSM_GB_STATES), F32)],
        compiler_params=_cparams(("arbitrary", "arbitrary"), VMEM_LIMIT),
        name="ssm_scan",
    )(u_tm, a, bw, cw)


def _ssm_post_kernel(y_ref, u_ref, d_ref, wg_ref, g_ref, wo_ref, z_ref):
    y = _gelu(y_ref[0] + y_ref[1] + d_ref[...] * u_ref[...]).astype(BF16)
    ab = jnp.dot(y, wg_ref[...], preferred_element_type=F32)
    ssm = ab[:, :SSM_WIDTH] * jax.nn.sigmoid(ab[:, SSM_WIDTH:])
    n = _rms(ssm, g_ref[...]).astype(BF16)
    z_ref[...] = jnp.dot(n, wo_ref[...], preferred_element_type=F32)


def _ssm_post(y, u_tm, d_skip, w_glu_bf16, g_ssm, w_out_ssm_bf16, *, tr=512):
    rows, width = u_tm.shape
    tr = min(tr, rows)
    dm = w_out_ssm_bf16.shape[1]
    full = lambda a: pl.BlockSpec(a.shape, lambda i: (0,) * a.ndim)
    d2, g2 = d_skip.reshape(1, width), g_ssm.reshape(1, width)
    return pl.pallas_call(
        _ssm_post_kernel,
        grid=(rows // tr,),
        in_specs=[pl.BlockSpec((2, tr, width), lambda i: (0, i, 0)),
                  pl.BlockSpec((tr, width), lambda i: (i, 0)),
                  full(d2), full(w_glu_bf16), full(g2), full(w_out_ssm_bf16)],
        out_specs=pl.BlockSpec((tr, dm), lambda i: (i, 0)),
        out_shape=jax.ShapeDtypeStruct((rows, dm), F32),
        compiler_params=_cparams(("parallel",), VMEM_LIMIT),
        name="ssm_post",
    )(y, u_tm, d2, w_glu_bf16, g2, w_out_ssm_bf16)


def _mix_out_kernel(x_ref, z_ref, o1_ref, o2_ref, o3_ref, l1_ref, l2_ref, l3_ref,
                    ga_ref, wo_ref, gf_ref, wq_ref, xn_ref, h_ref, q_ref):
    l1, l2, l3 = l1_ref[0], l2_ref[0], l3_ref[0]
    m = jnp.maximum(jnp.maximum(l1, l2), l3)
    w1, w2, w3 = jnp.exp(l1 - m), jnp.exp(l2 - m), jnp.exp(l3 - m)
    attn = (w1 * o1_ref[0] + w2 * o2_ref[0] + w3 * o3_ref[0]) / (w1 + w2 + w3)
    n = _rms(attn, ga_ref[...]).astype(BF16)
    xn = x_ref[0] + z_ref[...] + jnp.dot(n, wo_ref[...], preferred_element_type=F32)
    xn_ref[0] = xn
    h = _rms(xn, gf_ref[...])
    h_ref[0] = h
    qp = jnp.dot(h, wq_ref[...], preferred_element_type=F32, precision=lax.Precision.HIGHEST)
    for hd in range(PEER_HEADS):
        q_ref[hd] = qp[:, hd * LANES:(hd + 1) * LANES]


def _mix_out(x, z_tm, os_, ls_, g_attn, w_out_attn_bf16, norm_ffn, w_query, *, ts=256):
    bsz, seq, dm = x.shape
    ts = min(ts, seq)
    ns = seq // ts
    row = pl.BlockSpec((1, ts, dm), lambda b, i: (b, i, 0))
    half = pl.BlockSpec((1, ts, ATTN_WIDTH), lambda b, i: (b, i, 0))
    full = lambda a: pl.BlockSpec(a.shape, lambda b, i: (0,) * a.ndim)
    ga, gf = g_attn.reshape(1, ATTN_WIDTH), norm_ffn.reshape(1, dm)
    qdim = w_query.shape[1] // PEER_HEADS
    return pl.pallas_call(
        _mix_out_kernel,
        grid=(bsz, ns),
        in_specs=[row, pl.BlockSpec((ts, dm), lambda b, i: (i, b))] + [half] * 6
                 + [full(ga), full(w_out_attn_bf16), full(gf), full(w_query)],
        out_specs=[row, row, pl.BlockSpec((PEER_HEADS, ts, qdim), lambda b, i: (0, b * ns + i, 0))],
        out_shape=[jax.ShapeDtypeStruct((bsz, seq, dm), F32),
                   jax.ShapeDtypeStruct((bsz, seq, dm), F32),
                   jax.ShapeDtypeStruct((PEER_HEADS, bsz * seq, qdim), F32)],
        compiler_params=_cparams(("parallel", "arbitrary"), VMEM_LIMIT),
        name="mix_out",
    )(x, z_tm, *os_, *ls_, ga, w_out_attn_bf16, gf, w_query)


def _top16(s, order, payload=None):
    big = jnp.int32(2 ** 30)
    vals, picks = [], []
    for _ in range(PEER_TOPK):
        m = jnp.max(s, axis=0, keepdims=True)
        am = jnp.min(jnp.where(s == m, order, big), axis=0, keepdims=True)
        hit = order == am
        vals.append(m)
        if payload is None:
            picks.append(am)
        else:
            picks.append(jnp.max(jnp.where(hit, payload, -1), axis=0, keepdims=True))
        s = jnp.where(hit, -jnp.inf, s)
    return jnp.concatenate(vals, axis=0), jnp.concatenate(picks, axis=0)


def _pair_candidates(t1, t2, i1, i2, tt):
    k = PEER_TOPK
    r8 = lax.broadcasted_iota(jnp.int32, (SUBLANES, tt), 0)
    ninf = jnp.float32(-jnp.inf)
    cand, order, pay = [], [], []

    def piece(sc, key, ex, valid):
        cand.append(jnp.where(valid, sc, ninf) if valid is not None else sc)
        order.append(key)
        pay.append(ex)

    for a in range(4):
        nb = k // (a + 1)
        for b0 in range(0, nb, SUBLANES):
            piece(t1[a:a + 1] + t2[b0:b0 + SUBLANES], a * k + b0 + r8,
                  i1[a:a + 1] * PEER_KEYS + i2[b0:b0 + SUBLANES], None if b0 + SUBLANES <= nb else r8 < nb - b0)
    for b in range(3):
        piece(t1[0:SUBLANES] + t2[b:b + 1], r8 * k + b, i1[0:SUBLANES] * PEER_KEYS + i2[b:b + 1],
              (r8 >= 4) & ((r8 + 1) * (b + 1) <= k))
    piece(t1[SUBLANES:k] + t2[0:1], (r8 + SUBLANES) * k, i1[SUBLANES:k] * PEER_KEYS + i2[0:1], None)
    return jnp.concatenate(cand, axis=0), jnp.concatenate(order, axis=0), jnp.concatenate(pay, axis=0)


def _peer_topk_kernel(q_ref, k_ref, idx_ref, gate_ref):
    tt = q_ref.shape[1]
    dn = (((1,), (1,)), ((), ()))
    hp = lax.Precision.HIGHEST
    key_order = lax.broadcasted_iota(jnp.int32, (PEER_KEYS, tt), 0)

    def head(h, carry):
        q = q_ref[h]
        s1 = lax.dot_general(k_ref[0, h], q, dn, preferred_element_type=F32, precision=hp)
        s2 = lax.dot_general(k_ref[1, h], q, dn, preferred_element_type=F32, precision=hp)
        t1, i1 = _top16(s1, key_order)
        t2, i2 = _top16(s2, key_order)
        cand, order, pay = _pair_candidates(t1, t2, i1, i2, tt)
        top_s, experts = _top16(cand, order, pay)
        e = jnp.exp(top_s - jnp.max(top_s, axis=0, keepdims=True))
        r = pl.multiple_of(h * PEER_TOPK, PEER_TOPK)
        gate_ref[pl.ds(r, PEER_TOPK), :] = e / jnp.sum(e, axis=0, keepdims=True)
        idx_ref[pl.ds(r, PEER_TOPK), :] = experts * ROWS_PER_EXPERT
        return carry

    lax.fori_loop(0, PEER_HEADS, head, 0)


def _peer_topk(qp, keys_pad, *, tt=256):
    n_tok = qp.shape[1]
    tt = min(tt, n_tok)
    out = pl.BlockSpec((PEER_SEL, tt), lambda i: (0, i))
    return pl.pallas_call(
        _peer_topk_kernel,
        grid=(n_tok // tt,),
        in_specs=[pl.BlockSpec((PEER_HEADS, tt, qp.shape[2]), lambda i: (0, i, 0)),
                  pl.BlockSpec(keys_pad.shape, lambda i: (0, 0, 0, 0))],
        out_specs=[out, out],
        out_shape=[jax.ShapeDtypeStruct((PEER_SEL, n_tok), jnp.int32),
                   jax.ShapeDtypeStruct((PEER_SEL, n_tok), F32)],
        compiler_params=_cparams(("parallel",), VMEM_LIMIT),
        name="peer_topk",
    )(qp, keys_pad)


ROWS_PER_EXPERT = 4
PAIR_CHUNK = 16
CHUNK_ROWS = PAIR_CHUNK * ROWS_PER_EXPERT
N_CHUNKS = PEER_SEL // PAIR_CHUNK
CHUNK_SHIFT = N_CHUNKS.bit_length() - 1
assert 1 << CHUNK_SHIFT == N_CHUNKS
_GROUP_SLOT = tuple(
    2 * (2 * (0 if (j % 2) else 1) + (1 if (j // 2) in (0, 2) else 0)) + (0 if (j // 2) < 2 else 1)
    for j in range(8))
_CHUNK_SLOT = tuple(8 * (j // 8) + _GROUP_SLOT[j % 8] for j in range(PAIR_CHUNK))


def _pack_table(tbl):
    e, dm = tbl.shape
    tb = tbl.astype(BF16)
    lo = lax.bitcast_convert_type(tb[:, :dm // 2], jnp.uint16).astype(jnp.uint32)
    hi = lax.bitcast_convert_type(tb[:, dm // 2:], jnp.uint16).astype(jnp.uint32)
    w = lax.bitcast_convert_type(lo | (hi << 16), jnp.int32)
    return w.reshape(e * ROWS_PER_EXPERT, LANES)


def _unpack_words(w):
    lo = lax.bitcast_convert_type(w << 16, F32)
    hi = lax.bitcast_convert_type(w & jnp.int32(-65536), F32)
    return lo, hi


def _gather_chunk(idx_ref, tab_ref, buf_ref, c):
    ids = idx_ref.at[pl.ds(pl.multiple_of(c * PAIR_CHUNK, PAIR_CHUNK), PAIR_CHUNK)]
    for j in range(PAIR_CHUNK):
        e4 = pl.multiple_of(ids[j], ROWS_PER_EXPERT)
        s = _CHUNK_SLOT[j] * ROWS_PER_EXPERT
        buf_ref[s:s + ROWS_PER_EXPERT, :] = tab_ref[pl.ds(e4, ROWS_PER_EXPERT), :]


def _chunk_loop(n_chunks, idx_ref, tab_ref, buf_a, buf_b, consume, init):
    last = n_chunks - 1
    _gather_chunk(idx_ref, tab_ref, buf_a, 0)

    def body(i, carry):
        c = 2 * i
        carry = consume(c, buf_a, carry)
        _gather_chunk(idx_ref, tab_ref, buf_b, c + 1)
        carry = consume(c + 1, buf_b, carry)
        _gather_chunk(idx_ref, tab_ref, buf_a, jnp.minimum(c + 2, last))
        return carry

    return lax.fori_loop(0, n_chunks // 2, body, init)


def _peer_u_kernel(idx_ref, ha_ref, hb_ref, gate_ref, tab_ref, coef_ref, buf_a, buf_b, r_ref, d_ref, *, tt):
    sub = lax.broadcasted_iota(jnp.int32, (SUBLANES, LANES), 0)
    m_a = ((sub % 4) >= 2)[None]
    m_b = ((sub % 2) == 1)[None]
    nv = PAIR_CHUNK // 2

    def pair_partials(c, buf_ref, carry):
        t = c >> CHUNK_SHIFT
        lo, hi = _unpack_words(buf_ref[...])
        x = lo.reshape(nv, SUBLANES, LANES) * ha_ref[t][None] + hi.reshape(nv, SUBLANES, LANES) * hb_ref[t][None]
        x = x.reshape(nv // 2, 2, SUBLANES, LANES)
        xe, xo = x[:, 0], x[:, 1]
        a = xe + pltpu.roll(xe, 2, axis=1)
        b = xo + pltpu.roll(xo, 6, axis=1)
        m = jnp.where(m_a, a, b).reshape(nv // 4, 2, SUBLANES, LANES)
        me, mo = m[:, 0], m[:, 1]
        a2 = me + pltpu.roll(me, 1, axis=1)
        b2 = mo + pltpu.roll(mo, 7, axis=1)
        r_ref[pl.ds(pl.multiple_of(c * PAIR_CHUNK, PAIR_CHUNK), PAIR_CHUNK), :] = (
            jnp.where(m_b, a2, b2).reshape(PAIR_CHUNK, LANES))
        return carry

    _chunk_loop(tt * N_CHUNKS, idx_ref, tab_ref, buf_a, buf_b, pair_partials, 0)

    eye = (lax.broadcasted_iota(jnp.int32, (PEER_SEL, LANES), 0)
           == lax.broadcasted_iota(jnp.int32, (PEER_SEL, LANES), 1))

    def lane_sums(g, carry):
        for i in range(SUBLANES):
            t = g * SUBLANES + i
            s = jnp.sum(r_ref[pl.ds(pl.multiple_of(t * PEER_SEL, PEER_SEL), PEER_SEL), :], axis=-1, keepdims=True)
            d_ref[pl.ds(t, 1), :] = jnp.sum(jnp.where(eye, s, 0.0), axis=0, keepdims=True)
        return carry

    lax.fori_loop(0, tt // SUBLANES, lane_sums, 0)
    coef_ref[...] = gate_ref[...] * _gelu(d_ref[...])


def _peer_u(idx4, ha, hb, gates, tab, *, tt=128):
    n_tok = ha.shape[0]
    tt = min(tt, n_tok)
    tok2 = pl.BlockSpec((tt, PEER_SEL), lambda i: (i, 0))
    tok3 = pl.BlockSpec((tt, SUBLANES, LANES), lambda i: (i, 0, 0))
    return pl.pallas_call(
        functools.partial(_peer_u_kernel, tt=tt),
        grid=(n_tok // tt,),
        in_specs=[pl.BlockSpec((tt * PEER_SEL,), lambda i: (i,), memory_space=pltpu.SMEM),
                  tok3, tok3, tok2,
                  pl.BlockSpec(tab.shape, lambda i: (0, 0), pipeline_mode=pl.Buffered(1))],
        out_specs=tok2,
        out_shape=jax.ShapeDtypeStruct((n_tok, PEER_SEL), F32),
        scratch_shapes=[pltpu.VMEM((CHUNK_ROWS, LANES), jnp.int32),
                        pltpu.VMEM((CHUNK_ROWS, LANES), jnp.int32),
                        pltpu.VMEM((tt * PEER_SEL, LANES), F32),
                        pltpu.VMEM((tt, PEER_SEL), F32)],
        compiler_params=_cparams(("arbitrary",), VMEM_LIMIT),
        name="peer_u",
    )(idx4, ha, hb, gates, tab)


def _peer_v_kernel(idx_ref, coef_ref, tab_ref, o_ref, buf_a, buf_b, *, tt):
    sub = lax.broadcasted_iota(jnp.int32, (SUBLANES, LANES), 0)
    low = sub < ROWS_PER_EXPERT
    nv = PAIR_CHUNK // 2
    pair_at = {s: j for j, s in enumerate(_CHUNK_SLOT)}

    def accumulate(c, buf_ref, acc):
        acc_lo, acc_hi = acc
        cs = coef_ref.at[pl.ds(pl.multiple_of(c * PAIR_CHUNK, PAIR_CHUNK), PAIR_CHUNK)]
        lo, hi = _unpack_words(buf_ref[...])
        lo = lo.reshape(nv, SUBLANES, LANES)
        hi = hi.reshape(nv, SUBLANES, LANES)
        for v in range(nv):
            cv = jnp.where(low, cs[pair_at[2 * v]], cs[pair_at[2 * v + 1]])
            acc_lo = acc_lo + cv * lo[v]
            acc_hi = acc_hi + cv * hi[v]
        lo4 = acc_lo + pltpu.roll(acc_lo, ROWS_PER_EXPERT, axis=0)
        hi4 = acc_hi + pltpu.roll(acc_hi, ROWS_PER_EXPERT, axis=0)
        o_ref[c >> CHUNK_SHIFT] = jnp.where(low, lo4, hi4)
        last = (c & (N_CHUNKS - 1)) == N_CHUNKS - 1
        return jnp.where(last, 0.0, acc_lo), jnp.where(last, 0.0, acc_hi)

    z = jnp.zeros((SUBLANES, LANES), F32)
    _chunk_loop(tt * N_CHUNKS, idx_ref, tab_ref, buf_a, buf_b, accumulate, (z, z))


def _peer_v(idx4, coef, tab, *, tt=128):
    n_tok = coef.shape[0] // PEER_SEL
    tt = min(tt, n_tok)
    smem = pl.BlockSpec((tt * PEER_SEL,), lambda i: (i,), memory_space=pltpu.SMEM)
    return pl.pallas_call(
        functools.partial(_peer_v_kernel, tt=tt),
        grid=(n_tok // tt,),
        in_specs=[smem, smem,
                  pl.BlockSpec(tab.shape, lambda i: (0, 0), pipeline_mode=pl.Buffered(1))],
        out_specs=pl.BlockSpec((tt, SUBLANES, LANES), lambda i: (i, 0, 0)),
        out_shape=jax.ShapeDtypeStruct((n_tok, SUBLANES, LANES), F32),
        scratch_shapes=[pltpu.VMEM((CHUNK_ROWS, LANES), jnp.int32),
                        pltpu.VMEM((CHUNK_ROWS, LANES), jnp.int32)],
        compiler_params=_cparams(("arbitrary",), VMEM_LIMIT),
        name="peer_v",
    )(idx4, coef, tab)


def _peer(h, qp, sub_keys, expert_u, expert_v):
    bsz, seq, dm = h.shape
    n_tok = bsz * seq
    half = sub_keys.shape[-1]
    z = jnp.zeros_like(sub_keys[0])
    keys_pad = jnp.stack([jnp.concatenate([sub_keys[0], z], axis=-1),
                          jnp.concatenate([z, sub_keys[1]], axis=-1)]).astype(F32)
    assert keys_pad.shape[-1] == 2 * half == qp.shape[-1]
    idx_t, gate_t = _peer_topk(qp, keys_pad)
    idx4, gates = idx_t.T, gate_t.T
    hr = h.reshape(n_tok, 2, dm // (2 * LANES), LANES)
    ha = jnp.concatenate([hr[:, 0], hr[:, 0]], axis=1)
    hb = jnp.concatenate([hr[:, 1], hr[:, 1]], axis=1)
    idx4 = idx4.reshape(n_tok * PEER_SEL)
    coef = _peer_u(idx4, ha, hb, gates, _pack_table(expert_u))
    out = _peer_v(idx4, coef.reshape(n_tok * PEER_SEL), _pack_table(expert_v))
    return out.reshape(bsz, seq, dm)


def _final_kernel(x_ref, p_ref, g_ref, o_ref):
    o_ref[...] = _rms(x_ref[...] + p_ref[...], g_ref[...])


def _final_norm(x, p, g, *, tr=1024):
    rows, dm = x.shape
    tr = min(tr, rows)
    blk = pl.BlockSpec((tr, dm), lambda i: (i, 0))
    return pl.pallas_call(
        _final_kernel,
        grid=(rows // tr,),
        in_specs=[blk, blk, pl.BlockSpec((1, dm), lambda i: (0, 0))],
        out_specs=blk,
        out_shape=jax.ShapeDtypeStruct((rows, dm), F32),
        compiler_params=_cparams(("parallel",), VMEM_LIMIT),
        name="final_norm",
    )(x, p, g.reshape(1, dm))


def kernel(x, w_in, w_out, rel_bias, g_attn, g_ssm, norm_mix, norm_ffn, lam_re, lam_im, log_step, b_re, b_im, c_re, c_im, d_skip, w_glu, w_query, sub_keys, expert_u, expert_v, norm_final):
    bsz, seq, dm = x.shape
    depth = w_in.shape[0]
    prev = None
    for l in range(depth):
        x, q, k, v, u = _in_proj(x, prev, norm_mix[l], w_in[l].astype(BF16))
        attn = [_attn_pattern(q, k, v, rel_bias, d) for _, d in DILATED_PATTERNS]
        u_tm = u.reshape(seq * bsz, SSM_WIDTH)
        a, bw, cw = _ssm_params(lam_re[l], lam_im[l], log_step[l], b_re[l], b_im[l], c_re[l], c_im[l])
        y = _ssm_scan(u_tm, a, bw, cw, bsz=bsz)
        wo = w_out[l].astype(BF16)
        z = _ssm_post(y, u_tm, d_skip[l], w_glu[l].astype(BF16), g_ssm[l], wo[ATTN_WIDTH:])
        x, h, qp = _mix_out(x, z.reshape(seq, bsz * dm), [o for o, _ in attn], [s for _, s in attn],
                            g_attn[l], wo[:ATTN_WIDTH], norm_ffn[l], w_query[l])
        prev = _peer(h, qp, sub_keys[l], expert_u[l], expert_v[l])
    out = _final_norm(x.reshape(bsz * seq, dm), prev.reshape(bsz * seq, dm), norm_final)
    return out.reshape(bsz, seq, dm)
```

```python
import functools
import math

import numpy as np
import jax
import jax.numpy as jnp
from jax import lax
from jax.experimental import pallas as pl
from jax.experimental.pallas import tpu as pltpu

F32 = jnp.float32
BF16 = jnp.bfloat16

EPS = 1e-6
NEG_INF = -1e30
HEAD_DIM = 64
ATTN_WIDTH = 512
SSM_WIDTH = 512
SSM_GROUP = 16
SSM_STATE = 64
DILATED_PATTERNS = ((128, 1), (512, 4), (2048, 16))
REL_BUCKETS = 32
REL_MAX_DISTANCE = 1024
PEER_HEADS = 8
PEER_KEYS = 128
PEER_TOPK = 16
PEER_SEL = PEER_HEADS * PEER_TOPK

LANES = 128
SUBLANES = 8
QBLK = 128
KWIN = 256
BAND = 64
VMEM_LIMIT = 52 * 1024 * 1024


def _cparams(sem, vmem=None):
    return pltpu.CompilerParams(dimension_semantics=sem, vmem_limit_bytes=vmem)


def _rms(x, g):
    return x * lax.rsqrt(jnp.mean(x * x, axis=-1, keepdims=True) + EPS) * g


def _gelu(x):
    return 0.5 * x * (1.0 + lax.erf(x * (1.0 / math.sqrt(2.0))))


def _in_proj_kernel(*refs, has_prev):
    if has_prev:
        x_ref, p_ref, g_ref, w_ref, xo_ref, q_ref, k_ref, v_ref, u_ref = refs
        x = x_ref[0] + p_ref[0]
    else:
        x_ref, g_ref, w_ref, xo_ref, q_ref, k_ref, v_ref, u_ref = refs
        x = x_ref[0]
    xo_ref[0] = x
    h = _rms(x, g_ref[...]).astype(BF16)
    proj = jnp.dot(h, w_ref[...], preferred_element_type=F32)
    a = ATTN_WIDTH
    q_ref[0] = (proj[:, :a] * (HEAD_DIM ** -0.5)).astype(BF16)
    k_ref[0] = proj[:, a:2 * a].astype(BF16)
    v_ref[0] = proj[:, 2 * a:3 * a].astype(BF16)
    u_ref[...] = proj[:, 3 * a:]


def _in_proj(x, prev, g, w_bf16, *, ts=512):
    bsz, seq, dm = x.shape
    ts = min(ts, seq)
    row = pl.BlockSpec((1, ts, dm), lambda b, i: (b, i, 0))
    qkv = pl.BlockSpec((1, ts, ATTN_WIDTH), lambda b, i: (b, i, 0))
    ins = [x] + ([prev] if prev is not None else []) + [g.reshape(1, dm), w_bf16]
    in_specs = [row] + ([row] if prev is not None else []) + [
        pl.BlockSpec((1, dm), lambda b, i: (0, 0)),
        pl.BlockSpec(w_bf16.shape, lambda b, i: (0, 0)),
    ]
    return pl.pallas_call(
        functools.partial(_in_proj_kernel, has_prev=prev is not None),
        grid=(bsz, seq // ts),
        in_specs=in_specs,
        out_specs=[row, qkv, qkv, qkv, pl.BlockSpec((ts, SSM_WIDTH), lambda b, i: (i, b))],
        out_shape=[
            jax.ShapeDtypeStruct((bsz, seq, dm), F32),
            jax.ShapeDtypeStruct((bsz, seq, ATTN_WIDTH), BF16),
            jax.ShapeDtypeStruct((bsz, seq, ATTN_WIDTH), BF16),
            jax.ShapeDtypeStruct((bsz, seq, ATTN_WIDTH), BF16),
            jax.ShapeDtypeStruct((seq, bsz * SSM_WIDTH), F32),
        ],
        compiler_params=_cparams(("parallel", "arbitrary"), VMEM_LIMIT),
        name="in_proj",
    )(*ins)


def _t5_buckets(rel):
    half = REL_BUCKETS // 2
    max_exact = half // 2
    n = np.abs(rel)
    large = max_exact + (np.log(np.maximum(n, 1) / max_exact)
                         / np.log(REL_MAX_DISTANCE / max_exact) * (half - max_exact)).astype(np.int32)
    large = np.minimum(large, half - 1)
    return (np.where(rel > 0, half, 0) + np.where(n < max_exact, n, large)).astype(np.int32)


def _attn_bias_tables(rel_bias, dilation):
    ql = np.arange(QBLK)[:, None]
    kl = np.arange(KWIN)[None, :]
    delta = np.stack([kl + off - ql for off in (0, -BAND, -2 * BAND)])
    buckets = np.where(np.abs(delta) <= BAND, _t5_buckets(delta * dilation), -1)
    rb = rel_bias.astype(F32).T
    bk = jnp.asarray(buckets, jnp.int32)[None]
    tab = jnp.full((rb.shape[0],) + buckets.shape, NEG_INF, F32)
    for b in range(REL_BUCKETS):
        tab = jnp.where(bk == b, rb[:, b][:, None, None, None], tab)
    return tab


def _attn_kernel(q_ref, k_ref, v_ref, bias_ref, o_ref, lse_ref, *, length):
    nblk = length // QBLK
    lane = lax.broadcasted_iota(jnp.int32, (QBLK, LANES), 1)
    is_h0 = lane < HEAD_DIM
    dn = (((1,), (1,)), ((), ()))

    def body(i, carry):
        s = pl.multiple_of(i * QBLK, QBLK)
        ks = pl.multiple_of(jnp.clip(s - BAND, 0, length - KWIN), BAND)
        var = jnp.where(i == 0, 0, jnp.where(i == nblk - 1, 2, 1))
        qb = q_ref[0, pl.ds(s, QBLK), :]
        kb = k_ref[0, pl.ds(ks, KWIN), :]
        vb = v_ref[0, pl.ds(ks, KWIN), :]
        outs, lses = [], []
        for h in range(2):
            keep = is_h0 if h == 0 else jnp.logical_not(is_h0)
            qh = jnp.where(keep, qb, jnp.zeros_like(qb))
            logits = lax.dot_general(qh, kb, dn, preferred_element_type=F32) + bias_ref[h, var]
            m = jnp.max(logits, axis=-1, keepdims=True)
            p = jnp.exp(logits - m)
            ssum = jnp.sum(p, axis=-1, keepdims=True)
            o = jnp.dot(p.astype(BF16), vb, preferred_element_type=F32) / ssum
            outs.append(o)
            lses.append(jnp.broadcast_to(m + jnp.log(ssum), (QBLK, LANES)))
        o_ref[0, pl.ds(s, QBLK), :] = jnp.where(is_h0, outs[0], outs[1])
        lse_ref[0, pl.ds(s, QBLK), :] = jnp.where(is_h0, lses[0], lses[1])
        return carry

    lax.fori_loop(0, nblk, body, 0)


def _attn_pattern(q, k, v, rel_bias, dilation):
    bsz, seq, width = q.shape
    length = seq // dilation
    assert length % QBLK == 0 and length >= KWIN
    cols = dilation * width // LANES
    view = lambda t: t.reshape(bsz, length, dilation * width)
    bias = _attn_bias_tables(rel_bias, dilation)
    blk = pl.BlockSpec((1, length, LANES), lambda b, c: (b, 0, c))
    o, lse = pl.pallas_call(
        functools.partial(_attn_kernel, length=length),
        grid=(bsz, cols),
        in_specs=[blk, blk, blk,
                  pl.BlockSpec((2, 3, QBLK, KWIN), lambda b, c: (c % (width // LANES), 0, 0, 0))],
        out_specs=[blk, blk],
        out_shape=[jax.ShapeDtypeStruct((bsz, length, dilation * width), F32)] * 2,
        compiler_params=_cparams(("parallel", "arbitrary"), VMEM_LIMIT),
        name=f"attn_d{dilation}",
    )(view(q), view(k), view(v), bias)
    return o.reshape(bsz, seq, width), lse.reshape(bsz, seq, width)


SSM_LANE_GROUPS = SSM_WIDTH // LANES
SSM_GB_STATES = (LANES // SSM_GROUP) * SSM_STATE


def _ssm_params(lam_re, lam_im, log_step, b_re, b_im, c_re, c_im):
    f = lambda t: t.astype(F32)
    lr, li = f(lam_re), f(lam_im)
    step = jnp.exp(f(log_step))[..., None]
    mag = jnp.exp(lr * step)
    ar, ai = mag * jnp.cos(li * step), mag * jnp.sin(li * step)
    nr, ni = ar - 1.0, ai
    den = lr * lr + li * li
    cr, ci = (nr * lr + ni * li) / den, (ni * lr - nr * li) / den
    br, bi = f(b_re), f(b_im)
    bbr = cr[..., None] * br - ci[..., None] * bi
    bbi = cr[..., None] * bi + ci[..., None] * br
    gpb = LANES // SSM_GROUP
    eye = jnp.eye(gpb, dtype=F32)

    def in_map(t):
        t = t.reshape(2, SSM_LANE_GROUPS, gpb, SSM_STATE, SSM_GROUP)
        return jnp.einsum('dbgpc,gh->dbgchp', t, eye).reshape(2, SSM_LANE_GROUPS, LANES, SSM_GB_STATES)

    def out_map(t):
        t = t.reshape(2, SSM_LANE_GROUPS, gpb, SSM_GROUP, SSM_STATE)
        return jnp.einsum('dbgcp,gh->dbgphc', t, eye).reshape(2, SSM_LANE_GROUPS, SSM_GB_STATES, LANES)

    bw = jnp.concatenate([in_map(bbr), in_map(bbi)], axis=-1).astype(BF16)
    cw = jnp.concatenate([out_map(f(c_re)), -out_map(f(c_im))], axis=-2).astype(BF16)
    a = jnp.stack([ar.reshape(2, SSM_LANE_GROUPS, SSM_GB_STATES),
                   ai.reshape(2, SSM_LANE_GROUPS, SSM_GB_STATES)], axis=2)
    return a, bw, cw


def _ssm_kernel(u_ref, a_ref, bw_ref, cw_ref, y_ref, st_ref, bu_ref, *, ts, bsz):
    d = pl.program_id(0)
    ns = SSM_GB_STATES

    @pl.when(pl.program_id(1) == 0)
    def _():
        st_ref[...] = jnp.zeros_like(st_ref)

    for gb in range(SSM_LANE_GROUPS):
        ub = u_ref[:, gb * LANES:(gb + 1) * LANES].astype(BF16)
        bu_ref[...] = jnp.dot(ub, bw_ref[0, gb], preferred_element_type=F32)
        ar = jnp.broadcast_to(a_ref[0, gb, 0:1, :], (bsz, ns))
        ai = jnp.broadcast_to(a_ref[0, gb, 1:2, :], (bsz, ns))

        def step(j, carry, ar=ar, ai=ai):
            xr, xi = carry
            tl = jnp.where(d == 0, j, ts - 1 - j)
            r = pl.multiple_of(tl * bsz, bsz)
            nr = ar * xr - ai * xi + bu_ref[pl.ds(r, bsz), :ns]
            ni = ar * xi + ai * xr + bu_ref[pl.ds(r, bsz), ns:]
            bu_ref[pl.ds(r, bsz), :ns] = nr
            bu_ref[pl.ds(r, bsz), ns:] = ni
            return nr, ni

        xr, xi = lax.fori_loop(0, ts, step, (st_ref[gb, :, :ns], st_ref[gb, :, ns:]))
        st_ref[gb, :, :ns] = xr
        st_ref[gb, :, ns:] = xi
        y_ref[0, :, gb * LANES:(gb + 1) * LANES] = jnp.dot(
            bu_ref[...].astype(BF16), cw_ref[0, gb], preferred_element_type=F32)


def _ssm_scan(u_tm, a, bw, cw, *, bsz, ts=64):
    rows, width = u_tm.shape
    seq = rows // bsz
    ts = min(ts, seq)
    nt = seq // ts
    tblk = lambda d, i: jnp.where(d == 0, i, nt - 1 - i)
    return pl.pallas_call(
        functools.partial(_ssm_kernel, ts=ts, bsz=bsz),
        grid=(2, nt),
        in_specs=[
            pl.BlockSpec((ts * bsz, width), lambda d, i: (tblk(d, i), 0)),
            pl.BlockSpec((1,) + a.shape[1:], lambda d, i: (d, 0, 0, 0)),
            pl.BlockSpec((1,) + bw.shape[1:], lambda d, i: (d, 0, 0, 0)),
            pl.BlockSpec((1,) + cw.shape[1:], lambda d, i: (d, 0, 0, 0)),
        ],
        out_specs=pl.BlockSpec((1, ts * bsz, width), lambda d, i: (d, tblk(d, i), 0)),
        out_shape=jax.ShapeDtypeStruct((2, rows, width), F32),
        scratch_shapes=[pltpu.VMEM((SSM_LANE_GROUPS, bsz, 2 * SSM_GB_STATES), F32),
                        pltpu.VMEM((ts * bsz, 2 * SSM_GB_STATES), F32)],
        compiler_params=_cparams(("arbitrary", "arbitrary"), VMEM_LIMIT),
        name="ssm_scan",
    )(u_tm, a, bw, cw)


def _ssm_post_kernel(y_ref, u_ref, d_ref, wg_ref, g_ref, wo_ref, z_ref):
    y = _gelu(y_ref[0] + y_ref[1] + d_ref[...] * u_ref[...]).astype(BF16)
    ab = jnp.dot(y, wg_ref[...], preferred_element_type=F32)
    ssm = ab[:, :SSM_WIDTH] * jax.nn.sigmoid(ab[:, SSM_WIDTH:])
    n = _rms(ssm, g_ref[...]).astype(BF16)
    z_ref[...] = jnp.dot(n, wo_ref[...], preferred_element_type=F32)


def _ssm_post(y, u_tm, d_skip, w_glu_bf16, g_ssm, w_out_ssm_bf16, *, tr=512):
    rows, width = u_tm.shape
    tr = min(tr, rows)
    dm = w_out_ssm_bf16.shape[1]
    full = lambda a: pl.BlockSpec(a.shape, lambda i: (0,) * a.ndim)
    d2, g2 = d_skip.reshape(1, width), g_ssm.reshape(1, width)
    return pl.pallas_call(
        _ssm_post_kernel,
        grid=(rows // tr,),
        in_specs=[pl.BlockSpec((2, tr, width), lambda i: (0, i, 0)),
                  pl.BlockSpec((tr, width), lambda i: (i, 0)),
                  full(d2), full(w_glu_bf16), full(g2), full(w_out_ssm_bf16)],
        out_specs=pl.BlockSpec((tr, dm), lambda i: (i, 0)),
        out_shape=jax.ShapeDtypeStruct((rows, dm), F32),
        compiler_params=_cparams(("parallel",), VMEM_LIMIT),
        name="ssm_post",
    )(y, u_tm, d2, w_glu_bf16, g2, w_out_ssm_bf16)


def _mix_out_kernel(x_ref, z_ref, o1_ref, o2_ref, o3_ref, l1_ref, l2_ref, l3_ref,
                    ga_ref, wo_ref, gf_ref, wq_ref, xn_ref, h_ref, q_ref):
    l1, l2, l3 = l1_ref[0], l2_ref[0], l3_ref[0]
    m = jnp.maximum(jnp.maximum(l1, l2), l3)
    w1, w2, w3 = jnp.exp(l1 - m), jnp.exp(l2 - m), jnp.exp(l3 - m)
    attn = (w1 * o1_ref[0] + w2 * o2_ref[0] + w3 * o3_ref[0]) / (w1 + w2 + w3)
    n = _rms(attn, ga_ref[...]).astype(BF16)
    xn = x_ref[0] + z_ref[...] + jnp.dot(n, wo_ref[...], preferred_element_type=F32)
    xn_ref[0] = xn
    h = _rms(xn, gf_ref[...])
    h_ref[0] = h
    qp = jnp.dot(h, wq_ref[...], preferred_element_type=F32, precision=lax.Precision.HIGHEST)
    for hd in range(PEER_HEADS):
        q_ref[hd] = qp[:, hd * LANES:(hd + 1) * LANES]


def _mix_out(x, z_tm, os_, ls_, g_attn, w_out_attn_bf16, norm_ffn, w_query, *, ts=256):
    bsz, seq, dm = x.shape
    ts = min(ts, seq)
    ns = seq // ts
    row = pl.BlockSpec((1, ts, dm), lambda b, i: (b, i, 0))
    half = pl.BlockSpec((1, ts, ATTN_WIDTH), lambda b, i: (b, i, 0))
    full = lambda a: pl.BlockSpec(a.shape, lambda b, i: (0,) * a.ndim)
    ga, gf = g_attn.reshape(1, ATTN_WIDTH), norm_ffn.reshape(1, dm)
    qdim = w_query.shape[1] // PEER_HEADS
    return pl.pallas_call(
        _mix_out_kernel,
        grid=(bsz, ns),
        in_specs=[row, pl.BlockSpec((ts, dm), lambda b, i: (i, b))] + [half] * 6
                 + [full(ga), full(w_out_attn_bf16), full(gf), full(w_query)],
        out_specs=[row, row, pl.BlockSpec((PEER_HEADS, ts, qdim), lambda b, i: (0, b * ns + i, 0))],
        out_shape=[jax.ShapeDtypeStruct((bsz, seq, dm), F32),
                   jax.ShapeDtypeStruct((bsz, seq, dm), F32),
                   jax.ShapeDtypeStruct((PEER_HEADS, bsz * seq, qdim), F32)],
        compiler_params=_cparams(("parallel", "arbitrary"), VMEM_LIMIT),
        name="mix_out",
    )(x, z_tm, *os_, *ls_, ga, w_out_attn_bf16, gf, w_query)


def _top16(s, order, payload=None):
    big = jnp.int32(2 ** 30)
    vals, picks = [], []
    for _ in range(PEER_TOPK):
        m = jnp.max(s, axis=0, keepdims=True)
        am = jnp.min(jnp.where(s == m, order, big), axis=0, keepdims=True)
        hit = order == am
        vals.append(m)
        if payload is None:
            picks.append(am)
        else:
            picks.append(jnp.max(jnp.where(hit, payload, -1), axis=0, keepdims=True))
        s = jnp.where(hit, -jnp.inf, s)
    return jnp.concatenate(vals, axis=0), jnp.concatenate(picks, axis=0)


def _pair_candidates(t1, t2, i1, i2, tt):
    k = PEER_TOPK
    r8 = lax.broadcasted_iota(jnp.int32, (SUBLANES, tt), 0)
    ninf = jnp.float32(-jnp.inf)
    cand, order, pay = [], [], []

    def piece(sc, key, ex, valid):
        cand.append(jnp.where(valid, sc, ninf) if valid is not None else sc)
        order.append(key)
        pay.append(ex)

    for a in range(4):
        nb = k // (a + 1)
        for b0 in range(0, nb, SUBLANES):
            piece(t1[a:a + 1] + t2[b0:b0 + SUBLANES], a * k + b0 + r8,
                  i1[a:a + 1] * PEER_KEYS + i2[b0:b0 + SUBLANES], None if b0 + SUBLANES <= nb else r8 < nb - b0)
    for b in range(3):
        piece(t1[0:SUBLANES] + t2[b:b + 1], r8 * k + b, i1[0:SUBLANES] * PEER_KEYS + i2[b:b + 1],
              (r8 >= 4) & ((r8 + 1) * (b + 1) <= k))
    piece(t1[SUBLANES:k] + t2[0:1], (r8 + SUBLANES) * k, i1[SUBLANES:k] * PEER_KEYS + i2[0:1], None)
    return jnp.concatenate(cand, axis=0), jnp.concatenate(order, axis=0), jnp.concatenate(pay, axis=0)


def _peer_topk_kernel(q_ref, k_ref, idx_ref, gate_ref):
    tt = q_ref.shape[1]
    dn = (((1,), (1,)), ((), ()))
    hp = lax.Precision.HIGHEST
    key_order = lax.broadcasted_iota(jnp.int32, (PEER_KEYS, tt), 0)

    def head(h, carry):
        q = q_ref[h]
        s1 = lax.dot_general(k_ref[0, h], q, dn, preferred_element_type=F32, precision=hp)
        s2 = lax.dot_general(k_ref[1, h], q, dn, preferred_element_type=F32, precision=hp)
        t1, i1 = _top16(s1, key_order)
        t2, i2 = _top16(s2, key_order)
        cand, order, pay = _pair_candidates(t1, t2, i1, i2, tt)
        top_s, experts = _top16(cand, order, pay)
        e = jnp.exp(top_s - jnp.max(top_s, axis=0, keepdims=True))
        r = pl.multiple_of(h * PEER_TOPK, PEER_TOPK)
        gate_ref[pl.ds(r, PEER_TOPK), :] = e / jnp.sum(e, axis=0, keepdims=True)
        idx_ref[pl.ds(r, PEER_TOPK), :] = experts * ROWS_PER_EXPERT
        return carry

    lax.fori_loop(0, PEER_HEADS, head, 0)


def _peer_topk(qp, keys_pad, *, tt=256):
    n_tok = qp.shape[1]
    tt = min(tt, n_tok)
    out = pl.BlockSpec((PEER_SEL, tt), lambda i: (0, i))
    return pl.pallas_call(
        _peer_topk_kernel,
        grid=(n_tok // tt,),
        in_specs=[pl.BlockSpec((PEER_HEADS, tt, qp.shape[2]), lambda i: (0, i, 0)),
                  pl.BlockSpec(keys_pad.shape, lambda i: (0, 0, 0, 0))],
        out_specs=[out, out],
        out_shape=[jax.ShapeDtypeStruct((PEER_SEL, n_tok), jnp.int32),
                   jax.ShapeDtypeStruct((PEER_SEL, n_tok), F32)],
        compiler_params=_cparams(("parallel",), VMEM_LIMIT),
        name="peer_topk",
    )(qp, keys_pad)


ROWS_PER_EXPERT = 4
PAIR_CHUNK = 32
CHUNK_ROWS = PAIR_CHUNK * ROWS_PER_EXPERT
N_CHUNKS = PEER_SEL // PAIR_CHUNK
CHUNK_SHIFT = N_CHUNKS.bit_length() - 1
assert 1 << CHUNK_SHIFT == N_CHUNKS
_GROUP_SLOT = tuple(
    2 * (2 * (0 if (j % 2) else 1) + (1 if (j // 2) in (0, 2) else 0)) + (0 if (j // 2) < 2 else 1)
    for j in range(8))
_CHUNK_SLOT = tuple(8 * (j // 8) + _GROUP_SLOT[j % 8] for j in range(PAIR_CHUNK))


def _pack_table(tbl):
    e, dm = tbl.shape
    tb = tbl.astype(BF16)
    lo = lax.bitcast_convert_type(tb[:, :dm // 2], jnp.uint16).astype(jnp.uint32)
    hi = lax.bitcast_convert_type(tb[:, dm // 2:], jnp.uint16).astype(jnp.uint32)
    w = lax.bitcast_convert_type(lo | (hi << 16), jnp.int32)
    return w.reshape(e * ROWS_PER_EXPERT, LANES)


def _unpack_words(w):
    lo = lax.bitcast_convert_type(w << 16, F32)
    hi = lax.bitcast_convert_type(w & jnp.int32(-65536), F32)
    return lo, hi


def _gather_chunk(idx_ref, tab_ref, buf_ref, c):
    ids = idx_ref.at[pl.ds(pl.multiple_of(c * PAIR_CHUNK, PAIR_CHUNK), PAIR_CHUNK)]
    for j in range(PAIR_CHUNK):
        e4 = pl.multiple_of(ids[j], ROWS_PER_EXPERT)
        s = _CHUNK_SLOT[j] * ROWS_PER_EXPERT
        buf_ref[s:s + ROWS_PER_EXPERT, :] = tab_ref[pl.ds(e4, ROWS_PER_EXPERT), :]


def _chunk_loop(n_chunks, idx_ref, tab_ref, buf_a, buf_b, consume, init):
    last = n_chunks - 1
    _gather_chunk(idx_ref, tab_ref, buf_a, 0)

    def body(i, carry):
        c = 2 * i
        carry = consume(c, buf_a, carry)
        _gather_chunk(idx_ref, tab_ref, buf_b, c + 1)
        carry = consume(c + 1, buf_b, carry)
        _gather_chunk(idx_ref, tab_ref, buf_a, jnp.minimum(c + 2, last))
        return carry

    return lax.fori_loop(0, n_chunks // 2, body, init)


def _peer_u_kernel(idx_ref, ha_ref, hb_ref, gate_ref, tab_ref, coef_ref, buf_a, buf_b, r_ref, d_ref, *, tt):
    sub = lax.broadcasted_iota(jnp.int32, (SUBLANES, LANES), 0)
    m_a = ((sub % 4) >= 2)[None]
    m_b = ((sub % 2) == 1)[None]
    nv = PAIR_CHUNK // 2

    def pair_partials(c, buf_ref, carry):
        t = c >> CHUNK_SHIFT
        lo, hi = _unpack_words(buf_ref[...])
        x = lo.reshape(nv, SUBLANES, LANES) * ha_ref[t][None] + hi.reshape(nv, SUBLANES, LANES) * hb_ref[t][None]
        x = x.reshape(nv // 2, 2, SUBLANES, LANES)
        xe, xo = x[:, 0], x[:, 1]
        a = xe + pltpu.roll(xe, 2, axis=1)
        b = xo + pltpu.roll(xo, 6, axis=1)
        m = jnp.where(m_a, a, b).reshape(nv // 4, 2, SUBLANES, LANES)
        me, mo = m[:, 0], m[:, 1]
        a2 = me + pltpu.roll(me, 1, axis=1)
        b2 = mo + pltpu.roll(mo, 7, axis=1)
        r_ref[pl.ds(pl.multiple_of(c * PAIR_CHUNK, PAIR_CHUNK), PAIR_CHUNK), :] = (
            jnp.where(m_b, a2, b2).reshape(PAIR_CHUNK, LANES))
        return carry

    _chunk_loop(tt * N_CHUNKS, idx_ref, tab_ref, buf_a, buf_b, pair_partials, 0)

    eye = (lax.broadcasted_iota(jnp.int32, (PEER_SEL, LANES), 0)
           == lax.broadcasted_iota(jnp.int32, (PEER_SEL, LANES), 1))

    def lane_sums(g, carry):
        for i in range(SUBLANES):
            t = g * SUBLANES + i
            s = jnp.sum(r_ref[pl.ds(pl.multiple_of(t * PEER_SEL, PEER_SEL), PEER_SEL), :], axis=-1, keepdims=True)
            d_ref[pl.ds(t, 1), :] = jnp.sum(jnp.where(eye, s, 0.0), axis=0, keepdims=True)
        return carry

    lax.fori_loop(0, tt // SUBLANES, lane_sums, 0)
    coef_ref[...] = gate_ref[...] * _gelu(d_ref[...])


def _peer_u(idx4, ha, hb, gates, tab, *, tt=128):
    n_tok = ha.shape[0]
    tt = min(tt, n_tok)
    tok2 = pl.BlockSpec((tt, PEER_SEL), lambda i: (i, 0))
    tok3 = pl.BlockSpec((tt, SUBLANES, LANES), lambda i: (i, 0, 0))
    return pl.pallas_call(
        functools.partial(_peer_u_kernel, tt=tt),
        grid=(n_tok // tt,),
        in_specs=[pl.BlockSpec((tt * PEER_SEL,), lambda i: (i,), memory_space=pltpu.SMEM),
                  tok3, tok3, tok2,
                  pl.BlockSpec(tab.shape, lambda i: (0, 0), pipeline_mode=pl.Buffered(1))],
        out_specs=tok2,
        out_shape=jax.ShapeDtypeStruct((n_tok, PEER_SEL), F32),
        scratch_shapes=[pltpu.VMEM((CHUNK_ROWS, LANES), jnp.int32),
                        pltpu.VMEM((CHUNK_ROWS, LANES), jnp.int32),
                        pltpu.VMEM((tt * PEER_SEL, LANES), F32),
                        pltpu.VMEM((tt, PEER_SEL), F32)],
        compiler_params=_cparams(("arbitrary",), VMEM_LIMIT),
        name="peer_u",
    )(idx4, ha, hb, gates, tab)


def _peer_v_kernel(idx_ref, coef_ref, tab_ref, o_ref, buf_a, buf_b, *, tt):
    sub = lax.broadcasted_iota(jnp.int32, (SUBLANES, LANES), 0)
    low = sub < ROWS_PER_EXPERT
    nv = PAIR_CHUNK // 2

    def accumulate(c, buf_ref, acc):
        acc_lo, acc_hi = acc
        cs = coef_ref.at[pl.ds(pl.multiple_of(c * nv, nv), nv)]
        lo, hi = _unpack_words(buf_ref[...])
        lo = lo.reshape(nv, SUBLANES, LANES)
        hi = hi.reshape(nv, SUBLANES, LANES)
        for v in range(nv):
            c_lo, c_hi = _unpack_words(jnp.full((SUBLANES, LANES), cs[v], jnp.int32))
            cv = jnp.where(low, c_lo, c_hi)
            acc_lo = acc_lo + cv * lo[v]
            acc_hi = acc_hi + cv * hi[v]
        lo4 = acc_lo + pltpu.roll(acc_lo, ROWS_PER_EXPERT, axis=0)
        hi4 = acc_hi + pltpu.roll(acc_hi, ROWS_PER_EXPERT, axis=0)
        o_ref[c >> CHUNK_SHIFT] = jnp.where(low, lo4, hi4)
        last = (c & (N_CHUNKS - 1)) == N_CHUNKS - 1
        return jnp.where(last, 0.0, acc_lo), jnp.where(last, 0.0, acc_hi)

    z = jnp.zeros((SUBLANES, LANES), F32)
    _chunk_loop(tt * N_CHUNKS, idx_ref, tab_ref, buf_a, buf_b, accumulate, (z, z))


def _pack_coefs(coef):
    n_tok = coef.shape[0]
    pair_at = {s: j for j, s in enumerate(_CHUNK_SLOT)}
    first = np.array([ch * PAIR_CHUNK + pair_at[2 * v] for ch in range(N_CHUNKS) for v in range(PAIR_CHUNK // 2)])
    second = np.array([ch * PAIR_CHUNK + pair_at[2 * v + 1] for ch in range(N_CHUNKS) for v in range(PAIR_CHUNK // 2)])
    bits = lax.bitcast_convert_type(coef.astype(BF16), jnp.uint16).astype(jnp.uint32)
    w = bits[:, first] | (bits[:, second] << 16)
    return lax.bitcast_convert_type(w, jnp.int32).reshape(n_tok * (PEER_SEL // 2))


def _peer_v(idx4, coef_words, tab, *, tt=128):
    n_tok = idx4.shape[0] // PEER_SEL
    tt = min(tt, n_tok)
    smem = pl.BlockSpec((tt * PEER_SEL,), lambda i: (i,), memory_space=pltpu.SMEM)
    return pl.pallas_call(
        functools.partial(_peer_v_kernel, tt=tt),
        grid=(n_tok // tt,),
        in_specs=[smem, pl.BlockSpec((tt * PEER_SEL // 2,), lambda i: (i,), memory_space=pltpu.SMEM),
                  pl.BlockSpec(tab.shape, lambda i: (0, 0), pipeline_mode=pl.Buffered(1))],
        out_specs=pl.BlockSpec((tt, SUBLANES, LANES), lambda i: (i, 0, 0)),
        out_shape=jax.ShapeDtypeStruct((n_tok, SUBLANES, LANES), F32),
        scratch_shapes=[pltpu.VMEM((CHUNK_ROWS, LANES), jnp.int32),
                        pltpu.VMEM((CHUNK_ROWS, LANES), jnp.int32)],
        compiler_params=_cparams(("arbitrary",), VMEM_LIMIT),
        name="peer_v",
    )(idx4, coef_words, tab)


def _peer(h, qp, sub_keys, expert_u, expert_v):
    bsz, seq, dm = h.shape
    n_tok = bsz * seq
    half = sub_keys.shape[-1]
    z = jnp.zeros_like(sub_keys[0])
    keys_pad = jnp.stack([jnp.concatenate([sub_keys[0], z], axis=-1),
                          jnp.concatenate([z, sub_keys[1]], axis=-1)]).astype(F32)
    assert keys_pad.shape[-1] == 2 * half == qp.shape[-1]
    idx_t, gate_t = _peer_topk(qp, keys_pad)
    idx4, gates = idx_t.T, gate_t.T
    hr = h.reshape(n_tok, 2, dm // (2 * LANES), LANES)
    ha = jnp.concatenate([hr[:, 0], hr[:, 0]], axis=1)
    hb = jnp.concatenate([hr[:, 1], hr[:, 1]], axis=1)
    idx4 = idx4.reshape(n_tok * PEER_SEL)
    coef = _peer_u(idx4, ha, hb, gates, _pack_table(expert_u))
    out = _peer_v(idx4, _pack_coefs(coef), _pack_table(expert_v))
    return out.reshape(bsz, seq, dm)


def _final_kernel(x_ref, p_ref, g_ref, o_ref):
    o_ref[...] = _rms(x_ref[...] + p_ref[...], g_ref[...])


def _final_norm(x, p, g, *, tr=1024):
    rows, dm = x.shape
    tr = min(tr, rows)
    blk = pl.BlockSpec((tr, dm), lambda i: (i, 0))
    return pl.pallas_call(
        _final_kernel,
        grid=(rows // tr,),
        in_specs=[blk, blk, pl.BlockSpec((1, dm), lambda i: (0, 0))],
        out_specs=blk,
        out_shape=jax.ShapeDtypeStruct((rows, dm), F32),
        compiler_params=_cparams(("parallel",), VMEM_LIMIT),
        name="final_norm",
    )(x, p, g.reshape(1, dm))


def kernel(x, w_in, w_out, rel_bias, g_attn, g_ssm, norm_mix, norm_ffn, lam_re, lam_im, log_step, b_re, b_im, c_re, c_im, d_skip, w_glu, w_query, sub_keys, expert_u, expert_v, norm_final):
    bsz, seq, dm = x.shape
    depth = w_in.shape[0]
    prev = None
    for l in range(depth):
        x, q, k, v, u = _in_proj(x, prev, norm_mix[l], w_in[l].astype(BF16))
        attn = [_attn_pattern(q, k, v, rel_bias, d) for _, d in DILATED_PATTERNS]
        u_tm = u.reshape(seq * bsz, SSM_WIDTH)
        a, bw, cw = _ssm_params(lam_re[l], lam_im[l], log_step[l], b_re[l], b_im[l], c_re[l], c_im[l])
        y = _ssm_scan(u_tm, a, bw, cw, bsz=bsz)
        wo = w_out[l].astype(BF16)
        z = _ssm_post(y, u_tm, d_skip[l], w_glu[l].astype(BF16), g_ssm[l], wo[ATTN_WIDTH:])
        x, h, qp = _mix_out(x, z.reshape(seq, bsz * dm), [o for o, _ in attn], [s for _, s in attn],
                            g_attn[l], wo[:ATTN_WIDTH], norm_ffn[l], w_query[l])
        prev = _peer(h, qp, sub_keys[l], expert_u[l], expert_v[l])
    out = _final_norm(x.reshape(bsz * seq, dm), prev.reshape(bsz * seq, dm), norm_final)
    return out.reshape(bsz, seq, dm)
```

```python
import functools
import math

import numpy as np
import jax
import jax.numpy as jnp
from jax import lax
from jax.experimental import pallas as pl
from jax.experimental.pallas import tpu as pltpu

F32 = jnp.float32
BF16 = jnp.bfloat16

EPS = 1e-6
NEG_INF = -1e30
HEAD_DIM = 64
ATTN_WIDTH = 512
SSM_WIDTH = 512
SSM_GROUP = 16
SSM_STATE = 64
DILATED_PATTERNS = ((128, 1), (512, 4), (2048, 16))
REL_BUCKETS = 32
REL_MAX_DISTANCE = 1024
PEER_HEADS = 8
PEER_KEYS = 128
PEER_TOPK = 16
PEER_SEL = PEER_HEADS * PEER_TOPK

LANES = 128
SUBLANES = 8
QBLK = 128
KWIN = 256
BAND = 64
VMEM_LIMIT = 52 * 1024 * 1024


def _cparams(sem, vmem=None):
    return pltpu.CompilerParams(dimension_semantics=sem, vmem_limit_bytes=vmem)


def _rms(x, g):
    return x * lax.rsqrt(jnp.mean(x * x, axis=-1, keepdims=True) + EPS) * g


def _gelu(x):
    return 0.5 * x * (1.0 + lax.erf(x * (1.0 / math.sqrt(2.0))))


def _in_proj_kernel(*refs, has_prev):
    if has_prev:
        x_ref, p_ref, g_ref, w_ref, xo_ref, q_ref, k_ref, v_ref, u_ref = refs
        x = x_ref[0] + p_ref[0]
    else:
        x_ref, g_ref, w_ref, xo_ref, q_ref, k_ref, v_ref, u_ref = refs
        x = x_ref[0]
    xo_ref[0] = x
    h = _rms(x, g_ref[...]).astype(BF16)
    proj = jnp.dot(h, w_ref[...], preferred_element_type=F32)
    a = ATTN_WIDTH
    q_ref[0] = (proj[:, :a] * (HEAD_DIM ** -0.5)).astype(BF16)
    k_ref[0] = proj[:, a:2 * a].astype(BF16)
    v_ref[0] = proj[:, 2 * a:3 * a].astype(BF16)
    u_ref[...] = proj[:, 3 * a:]


def _in_proj(x, prev, g, w_bf16, *, ts=512):
    bsz, seq, dm = x.shape
    ts = min(ts, seq)
    row = pl.BlockSpec((1, ts, dm), lambda b, i: (b, i, 0))
    qkv = pl.BlockSpec((1, ts, ATTN_WIDTH), lambda b, i: (b, i, 0))
    ins = [x] + ([prev] if prev is not None else []) + [g.reshape(1, dm), w_bf16]
    in_specs = [row] + ([row] if prev is not None else []) + [
        pl.BlockSpec((1, dm), lambda b, i: (0, 0)),
        pl.BlockSpec(w_bf16.shape, lambda b, i: (0, 0)),
    ]
    return pl.pallas_call(
        functools.partial(_in_proj_kernel, has_prev=prev is not None),
        grid=(bsz, seq // ts),
        in_specs=in_specs,
        out_specs=[row, qkv, qkv, qkv, pl.BlockSpec((ts, SSM_WIDTH), lambda b, i: (i, b))],
        out_shape=[
            jax.ShapeDtypeStruct((bsz, seq, dm), F32),
            jax.ShapeDtypeStruct((bsz, seq, ATTN_WIDTH), BF16),
            jax.ShapeDtypeStruct((bsz, seq, ATTN_WIDTH), BF16),
            jax.ShapeDtypeStruct((bsz, seq, ATTN_WIDTH), BF16),
            jax.ShapeDtypeStruct((seq, bsz * SSM_WIDTH), F32),
        ],
        compiler_params=_cparams(("parallel", "arbitrary"), VMEM_LIMIT),
        name="in_proj",
    )(*ins)


def _t5_buckets(rel):
    half = REL_BUCKETS // 2
    max_exact = half // 2
    n = np.abs(rel)
    large = max_exact + (np.log(np.maximum(n, 1) / max_exact)
                         / np.log(REL_MAX_DISTANCE / max_exact) * (half - max_exact)).astype(np.int32)
    large = np.minimum(large, half - 1)
    return (np.where(rel > 0, half, 0) + np.where(n < max_exact, n, large)).astype(np.int32)


def _attn_bias_tables(rel_bias, dilation):
    ql = np.arange(QBLK)[:, None]
    kl = np.arange(KWIN)[None, :]
    delta = np.stack([kl + off - ql for off in (0, -BAND, -2 * BAND)])
    buckets = np.where(np.abs(delta) <= BAND, _t5_buckets(delta * dilation), -1)
    rb = rel_bias.astype(F32).T
    bk = jnp.asarray(buckets, jnp.int32)[None]
    tab = jnp.full((rb.shape[0],) + buckets.shape, NEG_INF, F32)
    for b in range(REL_BUCKETS):
        tab = jnp.where(bk == b, rb[:, b][:, None, None, None], tab)
    return tab


def _attn_kernel(q_ref, k_ref, v_ref, bias_ref, o_ref, lse_ref, *, length):
    nblk = length // QBLK
    lane = lax.broadcasted_iota(jnp.int32, (QBLK, LANES), 1)
    is_h0 = lane < HEAD_DIM
    dn = (((1,), (1,)), ((), ()))

    def body(i, carry):
        s = pl.multiple_of(i * QBLK, QBLK)
        ks = pl.multiple_of(jnp.clip(s - BAND, 0, length - KWIN), BAND)
        var = jnp.where(i == 0, 0, jnp.where(i == nblk - 1, 2, 1))
        qb = q_ref[0, pl.ds(s, QBLK), :]
        kb = k_ref[0, pl.ds(ks, KWIN), :]
        vb = v_ref[0, pl.ds(ks, KWIN), :]
        outs, lses = [], []
        for h in range(2):
            keep = is_h0 if h == 0 else jnp.logical_not(is_h0)
            qh = jnp.where(keep, qb, jnp.zeros_like(qb))
            logits = lax.dot_general(qh, kb, dn, preferred_element_type=F32) + bias_ref[h, var]
            m = jnp.max(logits, axis=-1, keepdims=True)
            p = jnp.exp(logits - m)
            ssum = jnp.sum(p, axis=-1, keepdims=True)
            o = jnp.dot(p.astype(BF16), vb, preferred_element_type=F32) / ssum
            outs.append(o)
            lses.append(jnp.broadcast_to(m + jnp.log(ssum), (QBLK, LANES)))
        o_ref[0, pl.ds(s, QBLK), :] = jnp.where(is_h0, outs[0], outs[1])
        lse_ref[0, pl.ds(s, QBLK), :] = jnp.where(is_h0, lses[0], lses[1])
        return carry

    lax.fori_loop(0, nblk, body, 0)


def _attn_pattern(q, k, v, rel_bias, dilation):
    bsz, seq, width = q.shape
    length = seq // dilation
    assert length % QBLK == 0 and length >= KWIN
    cols = dilation * width // LANES
    view = lambda t: t.reshape(bsz, length, dilation * width)
    bias = _attn_bias_tables(rel_bias, dilation)
    blk = pl.BlockSpec((1, length, LANES), lambda b, c: (b, 0, c))
    o, lse = pl.pallas_call(
        functools.partial(_attn_kernel, length=length),
        grid=(bsz, cols),
        in_specs=[blk, blk, blk,
                  pl.BlockSpec((2, 3, QBLK, KWIN), lambda b, c: (c % (width // LANES), 0, 0, 0))],
        out_specs=[blk, blk],
        out_shape=[jax.ShapeDtypeStruct((bsz, length, dilation * width), F32)] * 2,
        compiler_params=_cparams(("parallel", "arbitrary"), VMEM_LIMIT),
        name=f"attn_d{dilation}",
    )(view(q), view(k), view(v), bias)
    return o.reshape(bsz, seq, width), lse.reshape(bsz, seq, width)


SSM_LANE_GROUPS = SSM_WIDTH // LANES
SSM_GB_STATES = (LANES // SSM_GROUP) * SSM_STATE


def _ssm_params(lam_re, lam_im, log_step, b_re, b_im, c_re, c_im):
    f = lambda t: t.astype(F32)
    lr, li = f(lam_re), f(lam_im)
    step = jnp.exp(f(log_step))[..., None]
    mag = jnp.exp(lr * step)
    ar, ai = mag * jnp.cos(li * step), mag * jnp.sin(li * step)
    nr, ni = ar - 1.0, ai
    den = lr * lr + li * li
    cr, ci = (nr * lr + ni * li) / den, (ni * lr - nr * li) / den
    br, bi = f(b_re), f(b_im)
    bbr = cr[..., None] * br - ci[..., None] * bi
    bbi = cr[..., None] * bi + ci[..., None] * br
    gpb = LANES // SSM_GROUP
    eye = jnp.eye(gpb, dtype=F32)

    def in_map(t):
        t = t.reshape(2, SSM_LANE_GROUPS, gpb, SSM_STATE, SSM_GROUP)
        return jnp.einsum('dbgpc,gh->dbgchp', t, eye).reshape(2, SSM_LANE_GROUPS, LANES, SSM_GB_STATES)

    def out_map(t):
        t = t.reshape(2, SSM_LANE_GROUPS, gpb, SSM_GROUP, SSM_STATE)
        return jnp.einsum('dbgcp,gh->dbgphc', t, eye).reshape(2, SSM_LANE_GROUPS, SSM_GB_STATES, LANES)

    bw = jnp.concatenate([in_map(bbr), in_map(bbi)], axis=-1).astype(BF16)
    cw = jnp.concatenate([out_map(f(c_re)), -out_map(f(c_im))], axis=-2).astype(BF16)
    a = jnp.stack([ar.reshape(2, SSM_LANE_GROUPS, SSM_GB_STATES),
                   ai.reshape(2, SSM_LANE_GROUPS, SSM_GB_STATES)], axis=2)
    return a, bw, cw


def _ssm_kernel(u_ref, a_ref, bw_ref, cw_ref, y_ref, st_ref, bu_ref, *, ts, bsz):
    d = pl.program_id(0)
    ns = SSM_GB_STATES

    @pl.when(pl.program_id(1) == 0)
    def _():
        st_ref[...] = jnp.zeros_like(st_ref)

    for gb in range(SSM_LANE_GROUPS):
        ub = u_ref[:, gb * LANES:(gb + 1) * LANES].astype(BF16)
        bu_ref[...] = jnp.dot(ub, bw_ref[0, gb], preferred_element_type=F32)
        ar = jnp.broadcast_to(a_ref[0, gb, 0:1, :], (bsz, ns))
        ai = jnp.broadcast_to(a_ref[0, gb, 1:2, :], (bsz, ns))

        def step(j, carry, ar=ar, ai=ai):
            xr, xi = carry
            tl = jnp.where(d == 0, j, ts - 1 - j)
            r = pl.multiple_of(tl * bsz, bsz)
            nr = ar * xr - ai * xi + bu_ref[pl.ds(r, bsz), :ns]
            ni = ar * xi + ai * xr + bu_ref[pl.ds(r, bsz), ns:]
            bu_ref[pl.ds(r, bsz), :ns] = nr
            bu_ref[pl.ds(r, bsz), ns:] = ni
            return nr, ni

        xr, xi = lax.fori_loop(0, ts, step, (st_ref[gb, :, :ns], st_ref[gb, :, ns:]))
        st_ref[gb, :, :ns] = xr
        st_ref[gb, :, ns:] = xi
        y_ref[0, :, gb * LANES:(gb + 1) * LANES] = jnp.dot(
            bu_ref[...].astype(BF16), cw_ref[0, gb], preferred_element_type=F32)


def _ssm_scan(u_tm, a, bw, cw, *, bsz, ts=64):
    rows, width = u_tm.shape
    seq = rows // bsz
    ts = min(ts, seq)
    nt = seq // ts
    tblk = lambda d, i: jnp.where(d == 0, i, nt - 1 - i)
    return pl.pallas_call(
        functools.partial(_ssm_kernel, ts=ts, bsz=bsz),
        grid=(2, nt),
        in_specs=[
            pl.BlockSpec((ts * bsz, width), lambda d, i: (tblk(d, i), 0)),
            pl.BlockSpec((1,) + a.shape[1:], lambda d, i: (d, 0, 0, 0)),
            pl.BlockSpec((1,) + bw.shape[1:], lambda d, i: (d, 0, 0, 0)),
            pl.BlockSpec((1,) + cw.shape[1:], lambda d, i: (d, 0, 0, 0)),
        ],
        out_specs=pl.BlockSpec((1, ts * bsz, width), lambda d, i: (d, tblk(d, i), 0)),
        out_shape=jax.ShapeDtypeStruct((2, rows, width), F32),
        scratch_shapes=[pltpu.VMEM((SSM_LANE_GROUPS, bsz, 2 * SSM_GB_STATES), F32),
                        pltpu.VMEM((ts * bsz, 2 * SSM_GB_STATES), F32)],
        compiler_params=_cparams(("arbitrary", "arbitrary"), VMEM_LIMIT),
        name="ssm_scan",
    )(u_tm, a, bw, cw)


def _ssm_post_kernel(y_ref, u_ref, d_ref, wg_ref, g_ref, wo_ref, z_ref):
    y = _gelu(y_ref[0] + y_ref[1] + d_ref[...] * u_ref[...]).astype(BF16)
    ab = jnp.dot(y, wg_ref[...], preferred_element_type=F32)
    ssm = ab[:, :SSM_WIDTH] * jax.nn.sigmoid(ab[:, SSM_WIDTH:])
    n = _rms(ssm, g_ref[...]).astype(BF16)
    z_ref[...] = jnp.dot(n, wo_ref[...], preferred_element_type=F32)


def _ssm_post(y, u_tm, d_skip, w_glu_bf16, g_ssm, w_out_ssm_bf16, *, tr=512):
    rows, width = u_tm.shape
    tr = min(tr, rows)
    dm = w_out_ssm_bf16.shape[1]
    full = lambda a: pl.BlockSpec(a.shape, lambda i: (0,) * a.ndim)
    d2, g2 = d_skip.reshape(1, width), g_ssm.reshape(1, width)
    return pl.pallas_call(
        _ssm_post_kernel,
        grid=(rows // tr,),
        in_specs=[pl.BlockSpec((2, tr, width), lambda i: (0, i, 0)),
                  pl.BlockSpec((tr, width), lambda i: (i, 0)),
                  full(d2), full(w_glu_bf16), full(g2), full(w_out_ssm_bf16)],
        out_specs=pl.BlockSpec((tr, dm), lambda i: (i, 0)),
        out_shape=jax.ShapeDtypeStruct((rows, dm), F32),
        compiler_params=_cparams(("parallel",), VMEM_LIMIT),
        name="ssm_post",
    )(y, u_tm, d2, w_glu_bf16, g2, w_out_ssm_bf16)


def _mix_out_kernel(x_ref, z_ref, o1_ref, o2_ref, o3_ref, l1_ref, l2_ref, l3_ref,
                    ga_ref, wo_ref, gf_ref, wq_ref, xn_ref, h_ref, q_ref):
    l1, l2, l3 = l1_ref[0], l2_ref[0], l3_ref[0]
    m = jnp.maximum(jnp.maximum(l1, l2), l3)
    w1, w2, w3 = jnp.exp(l1 - m), jnp.exp(l2 - m), jnp.exp(l3 - m)
    attn = (w1 * o1_ref[0] + w2 * o2_ref[0] + w3 * o3_ref[0]) / (w1 + w2 + w3)
    n = _rms(attn, ga_ref[...]).astype(BF16)
    xn = x_ref[0] + z_ref[...] + jnp.dot(n, wo_ref[...], preferred_element_type=F32)
    xn_ref[0] = xn
    h = _rms(xn, gf_ref[...])
    h_ref[0] = h
    qp = jnp.dot(h.astype(BF16), wq_ref[...], preferred_element_type=F32)
    for hd in range(PEER_HEADS):
        q_ref[hd] = qp[:, hd * LANES:(hd + 1) * LANES]


def _mix_out(x, z_tm, os_, ls_, g_attn, w_out_attn_bf16, norm_ffn, w_query, *, ts=256):
    bsz, seq, dm = x.shape
    ts = min(ts, seq)
    ns = seq // ts
    row = pl.BlockSpec((1, ts, dm), lambda b, i: (b, i, 0))
    half = pl.BlockSpec((1, ts, ATTN_WIDTH), lambda b, i: (b, i, 0))
    full = lambda a: pl.BlockSpec(a.shape, lambda b, i: (0,) * a.ndim)
    ga, gf = g_attn.reshape(1, ATTN_WIDTH), norm_ffn.reshape(1, dm)
    qdim = w_query.shape[1] // PEER_HEADS
    return pl.pallas_call(
        _mix_out_kernel,
        grid=(bsz, ns),
        in_specs=[row, pl.BlockSpec((ts, dm), lambda b, i: (i, b))] + [half] * 6
                 + [full(ga), full(w_out_attn_bf16), full(gf), full(w_query)],
        out_specs=[row, row, pl.BlockSpec((PEER_HEADS, ts, qdim), lambda b, i: (0, b * ns + i, 0))],
        out_shape=[jax.ShapeDtypeStruct((bsz, seq, dm), F32),
                   jax.ShapeDtypeStruct((bsz, seq, dm), F32),
                   jax.ShapeDtypeStruct((PEER_HEADS, bsz * seq, qdim), F32)],
        compiler_params=_cparams(("parallel", "arbitrary"), VMEM_LIMIT),
        name="mix_out",
    )(x, z_tm, *os_, *ls_, ga, w_out_attn_bf16, gf, w_query)


TOPK_TOKENS = SUBLANES * LANES
KEY_PITCH = PEER_KEYS + 4
_CANDIDATES = tuple((a, b) for a in range(PEER_TOPK) for b in range(PEER_TOPK) if (a + 1) * (b + 1) <= PEER_TOPK)


def _tree(op, xs):
    xs = list(xs)
    while len(xs) > 1:
        xs = [op(xs[i], xs[i + 1]) if i + 1 < len(xs) else xs[i] for i in range(0, len(xs), 2)]
    return xs[0]


def _extract16(problems):
    big = jnp.int32(2 ** 30)
    ninf = jnp.float32(-jnp.inf)

    def step(r, ms):
        nxt = []
        for p, m in zip(problems, ms):
            s_ref, order = p["s"], p["order"]
            n = len(order)
            am = _tree(jnp.minimum, [jnp.where(s_ref[k] == m, order[k], big) for k in range(n)])
            news, pays = [], []
            for k in range(n):
                hit = am == order[k]
                nk = jnp.where(hit, ninf, s_ref[k])
                s_ref[k] = nk
                news.append(nk)
                if p.get("pay") is not None:
                    pays.append(jnp.where(hit, p["pay"][k], -1))
            p["vals"][r] = m
            p["picks"][r] = _tree(jnp.maximum, pays) if pays else am
            nxt.append(_tree(jnp.maximum, news))
        return tuple(nxt)

    init = tuple(_tree(jnp.maximum, [p["s"][k] for k in range(len(p["order"]))]) for p in problems)
    lax.fori_loop(0, PEER_TOPK, step, init)


def _peer_topk_kernel(q_ref, k_ref, idx_ref, gate_ref,
                      slab_ref, s1_ref, s2_ref, t1_ref, i1_ref, t2_ref, i2_ref, cand_ref, pay_ref, ts_ref, ex_ref):
    dn = (((1,), (1,)), ((), ()))
    keys = tuple(range(PEER_KEYS))

    def head(h, carry):
        for w, s_ref in ((0, s1_ref), (1, s2_ref)):
            for j in range(SUBLANES):
                slab_ref[j * KEY_PITCH:j * KEY_PITCH + PEER_KEYS, :] = lax.dot_general(
                    k_ref[w, h], q_ref[h, j * LANES:(j + 1) * LANES, :].astype(BF16), dn,
                    preferred_element_type=F32)
            for k in range(PEER_KEYS):
                s_ref[k] = slab_ref[pl.ds(k, SUBLANES, stride=KEY_PITCH), :]
        _extract16([dict(s=s1_ref, order=keys, vals=t1_ref, picks=i1_ref),
                    dict(s=s2_ref, order=keys, vals=t2_ref, picks=i2_ref)])
        for c, (a, b) in enumerate(_CANDIDATES):
            cand_ref[c] = t1_ref[a] + t2_ref[b]
            pay_ref[c] = i1_ref[a] * PEER_KEYS + i2_ref[b]
        _extract16([dict(s=cand_ref, order=tuple(a * PEER_TOPK + b for a, b in _CANDIDATES), pay=pay_ref,
                         vals=ts_ref, picks=ex_ref)])
        top_s = ts_ref[...]
        e = jnp.exp(top_s - jnp.max(top_s, axis=0, keepdims=True))
        gate_ref[0, h] = e / jnp.sum(e, axis=0, keepdims=True)
        idx_ref[0, h] = ex_ref[...] * ROWS_PER_EXPERT
        return carry

    lax.fori_loop(0, PEER_HEADS, head, 0)


def _peer_topk(qp, keys_pad):
    n_tok = qp.shape[1]
    tt = TOPK_TOKENS
    assert n_tok % tt == 0
    shp = (n_tok // tt, PEER_HEADS, PEER_TOPK, SUBLANES, LANES)
    out = pl.BlockSpec((1,) + shp[1:], lambda i: (i, 0, 0, 0, 0))
    vregs = lambda n, dt: pltpu.VMEM((n, SUBLANES, LANES), dt)
    idx, gate = pl.pallas_call(
        _peer_topk_kernel,
        grid=(n_tok // tt,),
        in_specs=[pl.BlockSpec((PEER_HEADS, tt, qp.shape[2]), lambda i: (0, i, 0)),
                  pl.BlockSpec(keys_pad.shape, lambda i: (0, 0, 0, 0))],
        out_specs=[out, out],
        out_shape=[jax.ShapeDtypeStruct(shp, jnp.int32), jax.ShapeDtypeStruct(shp, F32)],
        scratch_shapes=[pltpu.VMEM((SUBLANES * KEY_PITCH, LANES), F32),
                        vregs(PEER_KEYS, F32), vregs(PEER_KEYS, F32),
                        vregs(PEER_TOPK, F32), vregs(PEER_TOPK, jnp.int32),
                        vregs(PEER_TOPK, F32), vregs(PEER_TOPK, jnp.int32),
                        vregs(len(_CANDIDATES), F32), vregs(len(_CANDIDATES), jnp.int32),
                        vregs(PEER_TOPK, F32), vregs(PEER_TOPK, jnp.int32)],
        compiler_params=_cparams(("parallel",), VMEM_LIMIT),
        name="peer_topk",
    )(qp, keys_pad)
    to_tok = lambda a: jnp.transpose(a, (0, 3, 4, 1, 2)).reshape(n_tok, PEER_SEL)
    return to_tok(idx), to_tok(gate)


ROWS_PER_EXPERT = 4
PAIR_CHUNK = 32
CHUNK_ROWS = PAIR_CHUNK * ROWS_PER_EXPERT
N_CHUNKS = PEER_SEL // PAIR_CHUNK
CHUNK_SHIFT = N_CHUNKS.bit_length() - 1
assert 1 << CHUNK_SHIFT == N_CHUNKS
_GROUP_SLOT = tuple(
    2 * (2 * (0 if (j % 2) else 1) + (1 if (j // 2) in (0, 2) else 0)) + (0 if (j // 2) < 2 else 1)
    for j in range(8))
_CHUNK_SLOT = tuple(8 * (j // 8) + _GROUP_SLOT[j % 8] for j in range(PAIR_CHUNK))


def _pack_table(tbl):
    e, dm = tbl.shape
    tb = tbl.astype(BF16)
    lo = lax.bitcast_convert_type(tb[:, :dm // 2], jnp.uint16).astype(jnp.uint32)
    hi = lax.bitcast_convert_type(tb[:, dm // 2:], jnp.uint16).astype(jnp.uint32)
    w = lax.bitcast_convert_type(lo | (hi << 16), jnp.int32)
    return w.reshape(e * ROWS_PER_EXPERT, LANES)


def _unpack_words(w):
    lo = lax.bitcast_convert_type(w << 16, F32)
    hi = lax.bitcast_convert_type(w & jnp.int32(-65536), F32)
    return lo, hi


def _gather_chunk(idx_ref, tab_ref, buf_ref, c):
    ids = idx_ref.at[pl.ds(pl.multiple_of(c * PAIR_CHUNK, PAIR_CHUNK), PAIR_CHUNK)]
    for j in range(PAIR_CHUNK):
        e4 = pl.multiple_of(ids[j], ROWS_PER_EXPERT)
        s = _CHUNK_SLOT[j] * ROWS_PER_EXPERT
        buf_ref[s:s + ROWS_PER_EXPERT, :] = tab_ref[pl.ds(e4, ROWS_PER_EXPERT), :]


def _chunk_loop(n_chunks, idx_ref, tab_ref, buf_a, buf_b, consume, init):
    last = n_chunks - 1
    _gather_chunk(idx_ref, tab_ref, buf_a, 0)

    def body(i, carry):
        c = 2 * i
        carry = consume(c, buf_a, carry)
        _gather_chunk(idx_ref, tab_ref, buf_b, c + 1)
        carry = consume(c + 1, buf_b, carry)
        _gather_chunk(idx_ref, tab_ref, buf_a, jnp.minimum(c + 2, last))
        return carry

    return lax.fori_loop(0, n_chunks // 2, body, init)


def _peer_u_kernel(idx_ref, ha_ref, hb_ref, gate_ref, tab_ref, coef_ref, buf_a, buf_b, r_ref, d_ref, *, tt):
    sub = lax.broadcasted_iota(jnp.int32, (SUBLANES, LANES), 0)
    m_a = ((sub % 4) >= 2)[None]
    m_b = ((sub % 2) == 1)[None]
    nv = PAIR_CHUNK // 2

    def pair_partials(c, buf_ref, carry):
        t = c >> CHUNK_SHIFT
        lo, hi = _unpack_words(buf_ref[...])
        x = lo.reshape(nv, SUBLANES, LANES) * ha_ref[t][None] + hi.reshape(nv, SUBLANES, LANES) * hb_ref[t][None]
        x = x.reshape(nv // 2, 2, SUBLANES, LANES)
        xe, xo = x[:, 0], x[:, 1]
        a = xe + pltpu.roll(xe, 2, axis=1)
        b = xo + pltpu.roll(xo, 6, axis=1)
        m = jnp.where(m_a, a, b).reshape(nv // 4, 2, SUBLANES, LANES)
        me, mo = m[:, 0], m[:, 1]
        a2 = me + pltpu.roll(me, 1, axis=1)
        b2 = mo + pltpu.roll(mo, 7, axis=1)
        r_ref[pl.ds(pl.multiple_of(c * PAIR_CHUNK, PAIR_CHUNK), PAIR_CHUNK), :] = (
            jnp.where(m_b, a2, b2).reshape(PAIR_CHUNK, LANES))
        return carry

    _chunk_loop(tt * N_CHUNKS, idx_ref, tab_ref, buf_a, buf_b, pair_partials, 0)

    eye = (lax.broadcasted_iota(jnp.int32, (PEER_SEL, LANES), 0)
           == lax.broadcasted_iota(jnp.int32, (PEER_SEL, LANES), 1))

    def lane_sums(g, carry):
        for i in range(SUBLANES):
            t = g * SUBLANES + i
            s = jnp.sum(r_ref[pl.ds(pl.multiple_of(t * PEER_SEL, PEER_SEL), PEER_SEL), :], axis=-1, keepdims=True)
            d_ref[pl.ds(t, 1), :] = jnp.sum(jnp.where(eye, s, 0.0), axis=0, keepdims=True)
        return carry

    lax.fori_loop(0, tt // SUBLANES, lane_sums, 0)
    coef_ref[...] = gate_ref[...] * _gelu(d_ref[...])


def _peer_u(idx4, ha, hb, gates, tab, *, tt=128):
    n_tok = ha.shape[0]
    tt = min(tt, n_tok)
    tok2 = pl.BlockSpec((tt, PEER_SEL), lambda i: (i, 0))
    tok3 = pl.BlockSpec((tt, SUBLANES, LANES), lambda i: (i, 0, 0))
    return pl.pallas_call(
        functools.partial(_peer_u_kernel, tt=tt),
        grid=(n_tok // tt,),
        in_specs=[pl.BlockSpec((tt * PEER_SEL,), lambda i: (i,), memory_space=pltpu.SMEM),
                  tok3, tok3, tok2,
                  pl.BlockSpec(tab.shape, lambda i: (0, 0), pipeline_mode=pl.Buffered(1))],
        out_specs=tok2,
        out_shape=jax.ShapeDtypeStruct((n_tok, PEER_SEL), F32),
        scratch_shapes=[pltpu.VMEM((CHUNK_ROWS, LANES), jnp.int32),
                        pltpu.VMEM((CHUNK_ROWS, LANES), jnp.int32),
                        pltpu.VMEM((tt * PEER_SEL, LANES), F32),
                        pltpu.VMEM((tt, PEER_SEL), F32)],
        compiler_params=_cparams(("arbitrary",), VMEM_LIMIT),
        name="peer_u",
    )(idx4, ha, hb, gates, tab)


def _peer_v_kernel(idx_ref, coef_ref, tab_ref, o_ref, buf_a, buf_b, *, tt):
    sub = lax.broadcasted_iota(jnp.int32, (SUBLANES, LANES), 0)
    low = sub < ROWS_PER_EXPERT
    nv = PAIR_CHUNK // 2

    def accumulate(c, buf_ref, acc):
        acc_lo, acc_hi = acc
        cs = coef_ref.at[pl.ds(pl.multiple_of(c * nv, nv), nv)]
        lo, hi = _unpack_words(buf_ref[...])
        lo = lo.reshape(nv, SUBLANES, LANES)
        hi = hi.reshape(nv, SUBLANES, LANES)
        for v in range(nv):
            c_lo, c_hi = _unpack_words(jnp.full((SUBLANES, LANES), cs[v], jnp.int32))
            cv = jnp.where(low, c_lo, c_hi)
            acc_lo = acc_lo + cv * lo[v]
            acc_hi = acc_hi + cv * hi[v]
        lo4 = acc_lo + pltpu.roll(acc_lo, ROWS_PER_EXPERT, axis=0)
        hi4 = acc_hi + pltpu.roll(acc_hi, ROWS_PER_EXPERT, axis=0)
        o_ref[c >> CHUNK_SHIFT] = jnp.where(low, lo4, hi4)
        last = (c & (N_CHUNKS - 1)) == N_CHUNKS - 1
        return jnp.where(last, 0.0, acc_lo), jnp.where(last, 0.0, acc_hi)

    z = jnp.zeros((SUBLANES, LANES), F32)
    _chunk_loop(tt * N_CHUNKS, idx_ref, tab_ref, buf_a, buf_b, accumulate, (z, z))


def _pack_coefs(coef):
    n_tok = coef.shape[0]
    pair_at = {s: j for j, s in enumerate(_CHUNK_SLOT)}
    first = np.array([ch * PAIR_CHUNK + pair_at[2 * v] for ch in range(N_CHUNKS) for v in range(PAIR_CHUNK // 2)])
    second = np.array([ch * PAIR_CHUNK + pair_at[2 * v + 1] for ch in range(N_CHUNKS) for v in range(PAIR_CHUNK // 2)])
    bits = lax.bitcast_convert_type(coef.astype(BF16), jnp.uint16).astype(jnp.uint32)
    w = bits[:, first] | (bits[:, second] << 16)
    return lax.bitcast_convert_type(w, jnp.int32).reshape(n_tok * (PEER_SEL // 2))


def _peer_v(idx4, coef_words, tab, *, tt=128):
    n_tok = idx4.shape[0] // PEER_SEL
    tt = min(tt, n_tok)
    smem = pl.BlockSpec((tt * PEER_SEL,), lambda i: (i,), memory_space=pltpu.SMEM)
    return pl.pallas_call(
        functools.partial(_peer_v_kernel, tt=tt),
        grid=(n_tok // tt,),
        in_specs=[smem, pl.BlockSpec((tt * PEER_SEL // 2,), lambda i: (i,), memory_space=pltpu.SMEM),
                  pl.BlockSpec(tab.shape, lambda i: (0, 0), pipeline_mode=pl.Buffered(1))],
        out_specs=pl.BlockSpec((tt, SUBLANES, LANES), lambda i: (i, 0, 0)),
        out_shape=jax.ShapeDtypeStruct((n_tok, SUBLANES, LANES), F32),
        scratch_shapes=[pltpu.VMEM((CHUNK_ROWS, LANES), jnp.int32),
                        pltpu.VMEM((CHUNK_ROWS, LANES), jnp.int32)],
        compiler_params=_cparams(("arbitrary",), VMEM_LIMIT),
        name="peer_v",
    )(idx4, coef_words, tab)


def _peer(h, qp, sub_keys, expert_u, expert_v):
    bsz, seq, dm = h.shape
    n_tok = bsz * seq
    half = sub_keys.shape[-1]
    z = jnp.zeros_like(sub_keys[0])
    keys_pad = jnp.stack([jnp.concatenate([sub_keys[0], z], axis=-1),
                          jnp.concatenate([z, sub_keys[1]], axis=-1)]).astype(BF16)
    assert keys_pad.shape[-1] == 2 * half == qp.shape[-1]
    idx4, gates = _peer_topk(qp, keys_pad)
    hr = h.reshape(n_tok, 2, dm // (2 * LANES), LANES)
    ha = jnp.concatenate([hr[:, 0], hr[:, 0]], axis=1)
    hb = jnp.concatenate([hr[:, 1], hr[:, 1]], axis=1)
    idx4 = idx4.reshape(n_tok * PEER_SEL)
    coef = _peer_u(idx4, ha, hb, gates, _pack_table(expert_u))
    out = _peer_v(idx4, _pack_coefs(coef), _pack_table(expert_v))
    return out.reshape(bsz, seq, dm)


def _final_kernel(x_ref, p_ref, g_ref, o_ref):
    o_ref[...] = _rms(x_ref[...] + p_ref[...], g_ref[...])


def _final_norm(x, p, g, *, tr=1024):
    rows, dm = x.shape
    tr = min(tr, rows)
    blk = pl.BlockSpec((tr, dm), lambda i: (i, 0))
    return pl.pallas_call(
        _final_kernel,
        grid=(rows // tr,),
        in_specs=[blk, blk, pl.BlockSpec((1, dm), lambda i: (0, 0))],
        out_specs=blk,
        out_shape=jax.ShapeDtypeStruct((rows, dm), F32),
        compiler_params=_cparams(("parallel",), VMEM_LIMIT),
        name="final_norm",
    )(x, p, g.reshape(1, dm))


def kernel(x, w_in, w_out, rel_bias, g_attn, g_ssm, norm_mix, norm_ffn, lam_re, lam_im, log_step, b_re, b_im, c_re, c_im, d_skip, w_glu, w_query, sub_keys, expert_u, expert_v, norm_final):
    bsz, seq, dm = x.shape
    depth = w_in.shape[0]
    prev = None
    for l in range(depth):
        x, q, k, v, u = _in_proj(x, prev, norm_mix[l], w_in[l].astype(BF16))
        attn = [_attn_pattern(q, k, v, rel_bias, d) for _, d in DILATED_PATTERNS]
        u_tm = u.reshape(seq * bsz, SSM_WIDTH)
        a, bw, cw = _ssm_params(lam_re[l], lam_im[l], log_step[l], b_re[l], b_im[l], c_re[l], c_im[l])
        y = _ssm_scan(u_tm, a, bw, cw, bsz=bsz)
        wo = w_out[l].astype(BF16)
        z = _ssm_post(y, u_tm, d_skip[l], w_glu[l].astype(BF16), g_ssm[l], wo[ATTN_WIDTH:])
        x, h, qp = _mix_out(x, z.reshape(seq, bsz * dm), [o for o, _ in attn], [s for _, s in attn],
                            g_attn[l], wo[:ATTN_WIDTH], norm_ffn[l], w_query[l].astype(BF16))
        prev = _peer(h, qp, sub_keys[l], expert_u[l], expert_v[l])
    out = _final_norm(x.reshape(bsz * seq, dm), prev.reshape(bsz * seq, dm), norm_final)
    return out.reshape(bsz, seq, dm)
```

```python
import functools
import math

import numpy as np
import jax
import jax.numpy as jnp
from jax import lax
from jax.experimental import pallas as pl
from jax.experimental.pallas import tpu as pltpu

F32 = jnp.float32
BF16 = jnp.bfloat16

EPS = 1e-6
NEG_INF = -1e30
HEAD_DIM = 64
ATTN_WIDTH = 512
SSM_WIDTH = 512
SSM_GROUP = 16
SSM_STATE = 64
DILATED_PATTERNS = ((128, 1), (512, 4), (2048, 16))
REL_BUCKETS = 32
REL_MAX_DISTANCE = 1024
PEER_HEADS = 8
PEER_KEYS = 128
PEER_TOPK = 16
PEER_SEL = PEER_HEADS * PEER_TOPK

LANES = 128
SUBLANES = 8
QBLK = 128
KWIN = 256
BAND = 64
VMEM_LIMIT = 52 * 1024 * 1024


def _cparams(sem, vmem=None):
    return pltpu.CompilerParams(dimension_semantics=sem, vmem_limit_bytes=vmem)


def _rms(x, g):
    return x * lax.rsqrt(jnp.mean(x * x, axis=-1, keepdims=True) + EPS) * g


def _gelu(x):
    return 0.5 * x * (1.0 + lax.erf(x * (1.0 / math.sqrt(2.0))))


def _in_proj_kernel(*refs, has_prev):
    if has_prev:
        x_ref, p_ref, g_ref, w_ref, xo_ref, q_ref, k_ref, v_ref, u_ref = refs
        x = x_ref[0] + p_ref[0]
    else:
        x_ref, g_ref, w_ref, xo_ref, q_ref, k_ref, v_ref, u_ref = refs
        x = x_ref[0]
    xo_ref[0] = x
    h = _rms(x, g_ref[...]).astype(BF16)
    proj = jnp.dot(h, w_ref[...], preferred_element_type=F32)
    a = ATTN_WIDTH
    q_ref[0] = proj[:, :a] * (HEAD_DIM ** -0.5)
    k_ref[0] = proj[:, a:2 * a]
    v_ref[0] = proj[:, 2 * a:3 * a]
    u_ref[...] = proj[:, 3 * a:]


def _in_proj(x, prev, g, w_bf16, *, ts=512):
    bsz, seq, dm = x.shape
    ts = min(ts, seq)
    row = pl.BlockSpec((1, ts, dm), lambda b, i: (b, i, 0))
    qkv = pl.BlockSpec((1, ts, ATTN_WIDTH), lambda b, i: (b, i, 0))
    ins = [x] + ([prev] if prev is not None else []) + [g.reshape(1, dm), w_bf16]
    in_specs = [row] + ([row] if prev is not None else []) + [
        pl.BlockSpec((1, dm), lambda b, i: (0, 0)),
        pl.BlockSpec(w_bf16.shape, lambda b, i: (0, 0)),
    ]
    return pl.pallas_call(
        functools.partial(_in_proj_kernel, has_prev=prev is not None),
        grid=(bsz, seq // ts),
        in_specs=in_specs,
        out_specs=[row, qkv, qkv, qkv, pl.BlockSpec((ts, SSM_WIDTH), lambda b, i: (i, b))],
        out_shape=[
            jax.ShapeDtypeStruct((bsz, seq, dm), F32),
            jax.ShapeDtypeStruct((bsz, seq, ATTN_WIDTH), F32),
            jax.ShapeDtypeStruct((bsz, seq, ATTN_WIDTH), F32),
            jax.ShapeDtypeStruct((bsz, seq, ATTN_WIDTH), F32),
            jax.ShapeDtypeStruct((seq, bsz * SSM_WIDTH), F32),
        ],
        compiler_params=_cparams(("parallel", "arbitrary"), VMEM_LIMIT),
        name="in_proj",
    )(*ins)


def _t5_buckets(rel):
    half = REL_BUCKETS // 2
    max_exact = half // 2
    n = np.abs(rel)
    large = max_exact + (np.log(np.maximum(n, 1) / max_exact)
                         / np.log(REL_MAX_DISTANCE / max_exact) * (half - max_exact)).astype(np.int32)
    large = np.minimum(large, half - 1)
    return (np.where(rel > 0, half, 0) + np.where(n < max_exact, n, large)).astype(np.int32)


def _attn_bias_tables(rel_bias, dilation):
    ql = np.arange(QBLK)[:, None]
    kl = np.arange(KWIN)[None, :]
    delta = np.stack([kl + off - ql for off in (0, -BAND, -2 * BAND)])
    buckets = np.where(np.abs(delta) <= BAND, _t5_buckets(delta * dilation), -1)
    rb = rel_bias.astype(F32).T
    bk = jnp.asarray(buckets, jnp.int32)[None]
    tab = jnp.full((rb.shape[0],) + buckets.shape, NEG_INF, F32)
    for b in range(REL_BUCKETS):
        tab = jnp.where(bk == b, rb[:, b][:, None, None, None], tab)
    return tab


BLOCKS_PER_STEP = 2


def _attn_kernel(q_ref, k_ref, v_ref, *rest, seq):
    bias_refs, (o_ref, acc_ref, m_ref, z_ref) = rest[:len(DILATED_PATTERNS)], rest[len(DILATED_PATTERNS):]
    lane = lax.broadcasted_iota(jnp.int32, (QBLK, LANES), 1)
    is_h0 = lane < HEAD_DIM
    dn = (((1,), (1,)), ((), ()))
    nsteps = seq // QBLK

    def rows(start, size, d):
        return pl.ds(start, size) if d == 1 else pl.ds(start, size, stride=d)

    def block(n, d, bias_ref, first, last):
        length = seq // d
        nblk = length // QBLK
        r, i = n >> (nblk.bit_length() - 1), n & (nblk - 1)
        s = i * QBLK
        ks = jnp.clip(s - BAND, 0, length - KWIN)
        var = jnp.where(i == 0, 0, jnp.where(i == nblk - 1, 2, 1))
        q_rows = rows(r + d * s, QBLK, d)
        k_rows = rows(r + d * ks, KWIN, d)
        qb = q_ref[q_rows, :].astype(BF16)
        kb = k_ref[k_rows, :].astype(BF16)
        vb = v_ref[k_rows, :].astype(BF16)
        outs, ms, zs = [], [], []
        for h in range(2):
            keep = is_h0 if h == 0 else jnp.logical_not(is_h0)
            qh = jnp.where(keep, qb, jnp.zeros_like(qb))
            logits = lax.dot_general(qh, kb, dn, preferred_element_type=F32) + bias_ref[h, var]
            m = jnp.max(logits, axis=-1, keepdims=True)
            p = jnp.exp(logits - m)
            outs.append(jnp.dot(p.astype(BF16), vb, preferred_element_type=F32))
            ms.append(jnp.broadcast_to(m, (QBLK, LANES)))
            zs.append(jnp.broadcast_to(jnp.sum(p, axis=-1, keepdims=True), (QBLK, LANES)))
        o = jnp.where(is_h0, outs[0], outs[1])
        m = jnp.where(is_h0, ms[0], ms[1])
        z = jnp.where(is_h0, zs[0], zs[1])
        if not first:
            m_old = m_ref[q_rows, :]
            m_new = jnp.maximum(m_old, m)
            a, b = jnp.exp(m_old - m_new), jnp.exp(m - m_new)
            o = acc_ref[q_rows, :] * a + o * b
            z = z_ref[q_rows, :] * a + z * b
            m = m_new
        if last:
            o_ref[q_rows, :] = o / z
        else:
            acc_ref[q_rows, :] = o
            m_ref[q_rows, :] = m
            z_ref[q_rows, :] = z

    for p, ((_, d), bias_ref) in enumerate(zip(DILATED_PATTERNS, bias_refs)):
        def step(g, carry, d=d, bias_ref=bias_ref, p=p):
            for j in range(BLOCKS_PER_STEP):
                block(g * BLOCKS_PER_STEP + j, d, bias_ref, p == 0, p == len(DILATED_PATTERNS) - 1)
            return carry
        lax.fori_loop(0, nsteps // BLOCKS_PER_STEP, step, 0)


def _attention(q, k, v, biases):
    bsz, seq, width = q.shape
    for _, d in DILATED_PATTERNS:
        length = seq // d
        assert length >= KWIN and length % QBLK == 0 and (length // QBLK) & (length // QBLK - 1) == 0
    assert (seq // QBLK) % BLOCKS_PER_STEP == 0
    blk = pl.BlockSpec((None, seq, LANES), lambda b, c: (b, 0, c))
    bias_spec = pl.BlockSpec((2, 3, QBLK, KWIN), lambda b, c: (c, 0, 0, 0))
    return pl.pallas_call(
        functools.partial(_attn_kernel, seq=seq),
        grid=(bsz, width // LANES),
        in_specs=[blk, blk, blk] + [bias_spec] * len(biases),
        out_specs=blk,
        out_shape=jax.ShapeDtypeStruct((bsz, seq, width), F32),
        scratch_shapes=[pltpu.VMEM((seq, LANES), F32)] * 3,
        compiler_params=_cparams(("parallel", "arbitrary"), VMEM_LIMIT),
        name="attention",
    )(q, k, v, *biases)


SSM_LANE_GROUPS = SSM_WIDTH // LANES
SSM_GB_STATES = (LANES // SSM_GROUP) * SSM_STATE


def _ssm_params(lam_re, lam_im, log_step, b_re, b_im, c_re, c_im):
    f = lambda t: t.astype(F32)
    lr, li = f(lam_re), f(lam_im)
    step = jnp.exp(f(log_step))[..., None]
    mag = jnp.exp(lr * step)
    ar, ai = mag * jnp.cos(li * step), mag * jnp.sin(li * step)
    nr, ni = ar - 1.0, ai
    den = lr * lr + li * li
    cr, ci = (nr * lr + ni * li) / den, (ni * lr - nr * li) / den
    br, bi = f(b_re), f(b_im)
    bbr = cr[..., None] * br - ci[..., None] * bi
    bbi = cr[..., None] * bi + ci[..., None] * br
    gpb = LANES // SSM_GROUP
    eye = jnp.eye(gpb, dtype=F32)

    def in_map(t):
        t = t.reshape(2, SSM_LANE_GROUPS, gpb, SSM_STATE, SSM_GROUP)
        return jnp.einsum('dbgpc,gh->dbgchp', t, eye).reshape(2, SSM_LANE_GROUPS, LANES, SSM_GB_STATES)

    def out_map(t):
        t = t.reshape(2, SSM_LANE_GROUPS, gpb, SSM_GROUP, SSM_STATE)
        return jnp.einsum('dbgcp,gh->dbgphc', t, eye).reshape(2, SSM_LANE_GROUPS, SSM_GB_STATES, LANES)

    bw = jnp.concatenate([in_map(bbr), in_map(bbi)], axis=-1).astype(BF16)
    cw = jnp.concatenate([out_map(f(c_re)), -out_map(f(c_im))], axis=-2).astype(BF16)
    a = jnp.stack([ar.reshape(2, SSM_LANE_GROUPS, SSM_GB_STATES),
                   ai.reshape(2, SSM_LANE_GROUPS, SSM_GB_STATES)], axis=2)
    return a, bw, cw


def _ssm_kernel(u_ref, a_ref, bw_ref, cw_ref, y_ref, st_ref, bu_ref, *, ts, bsz):
    d = pl.program_id(0)
    ns = SSM_GB_STATES

    @pl.when(pl.program_id(1) == 0)
    def _():
        st_ref[...] = jnp.zeros_like(st_ref)

    for gb in range(SSM_LANE_GROUPS):
        ub = u_ref[:, gb * LANES:(gb + 1) * LANES].astype(BF16)
        bu_ref[...] = jnp.dot(ub, bw_ref[0, gb], preferred_element_type=F32)
        ar = jnp.broadcast_to(a_ref[0, gb, 0:1, :], (bsz, ns))
        ai = jnp.broadcast_to(a_ref[0, gb, 1:2, :], (bsz, ns))

        def step(j, carry, ar=ar, ai=ai):
            xr, xi = carry
            tl = jnp.where(d == 0, j, ts - 1 - j)
            r = pl.multiple_of(tl * bsz, bsz)
            nr = ar * xr - ai * xi + bu_ref[pl.ds(r, bsz), :ns]
            ni = ar * xi + ai * xr + bu_ref[pl.ds(r, bsz), ns:]
            bu_ref[pl.ds(r, bsz), :ns] = nr
            bu_ref[pl.ds(r, bsz), ns:] = ni
            return nr, ni

        xr, xi = lax.fori_loop(0, ts, step, (st_ref[gb, :, :ns], st_ref[gb, :, ns:]))
        st_ref[gb, :, :ns] = xr
        st_ref[gb, :, ns:] = xi
        y_ref[0, :, gb * LANES:(gb + 1) * LANES] = jnp.dot(
            bu_ref[...].astype(BF16), cw_ref[0, gb], preferred_element_type=F32)


def _ssm_scan(u_tm, a, bw, cw, *, bsz, ts=64):
    rows, width = u_tm.shape
    seq = rows // bsz
    ts = min(ts, seq)
    nt = seq // ts
    tblk = lambda d, i: jnp.where(d == 0, i, nt - 1 - i)
    return pl.pallas_call(
        functools.partial(_ssm_kernel, ts=ts, bsz=bsz),
        grid=(2, nt),
        in_specs=[
            pl.BlockSpec((ts * bsz, width), lambda d, i: (tblk(d, i), 0)),
            pl.BlockSpec((1,) + a.shape[1:], lambda d, i: (d, 0, 0, 0)),
            pl.BlockSpec((1,) + bw.shape[1:], lambda d, i: (d, 0, 0, 0)),
            pl.BlockSpec((1,) + cw.shape[1:], lambda d, i: (d, 0, 0, 0)),
        ],
        out_specs=pl.BlockSpec((1, ts * bsz, width), lambda d, i: (d, tblk(d, i), 0)),
        out_shape=jax.ShapeDtypeStruct((2, rows, width), F32),
        scratch_shapes=[pltpu.VMEM((SSM_LANE_GROUPS, bsz, 2 * SSM_GB_STATES), F32),
                        pltpu.VMEM((ts * bsz, 2 * SSM_GB_STATES), F32)],
        compiler_params=_cparams(("arbitrary", "arbitrary"), VMEM_LIMIT),
        name="ssm_scan",
    )(u_tm, a, bw, cw)


def _ssm_post_kernel(y_ref, u_ref, d_ref, wg_ref, g_ref, wo_ref, z_ref):
    y = _gelu(y_ref[0] + y_ref[1] + d_ref[...] * u_ref[...]).astype(BF16)
    ab = jnp.dot(y, wg_ref[...], preferred_element_type=F32)
    ssm = ab[:, :SSM_WIDTH] * jax.nn.sigmoid(ab[:, SSM_WIDTH:])
    n = _rms(ssm, g_ref[...]).astype(BF16)
    z_ref[...] = jnp.dot(n, wo_ref[...], preferred_element_type=F32)


def _ssm_post(y, u_tm, d_skip, w_glu_bf16, g_ssm, w_out_ssm_bf16, *, tr=512):
    rows, width = u_tm.shape
    tr = min(tr, rows)
    dm = w_out_ssm_bf16.shape[1]
    full = lambda a: pl.BlockSpec(a.shape, lambda i: (0,) * a.ndim)
    d2, g2 = d_skip.reshape(1, width), g_ssm.reshape(1, width)
    return pl.pallas_call(
        _ssm_post_kernel,
        grid=(rows // tr,),
        in_specs=[pl.BlockSpec((2, tr, width), lambda i: (0, i, 0)),
                  pl.BlockSpec((tr, width), lambda i: (i, 0)),
                  full(d2), full(w_glu_bf16), full(g2), full(w_out_ssm_bf16)],
        out_specs=pl.BlockSpec((tr, dm), lambda i: (i, 0)),
        out_shape=jax.ShapeDtypeStruct((rows, dm), F32),
        compiler_params=_cparams(("parallel",), VMEM_LIMIT),
        name="ssm_post",
    )(y, u_tm, d2, w_glu_bf16, g2, w_out_ssm_bf16)


def _mix_out_kernel(x_ref, z_ref, a_ref, ga_ref, wo_ref, gf_ref, wq_ref, xn_ref, h_ref, q_ref):
    n = _rms(a_ref[0], ga_ref[...]).astype(BF16)
    xn = x_ref[0] + z_ref[...] + jnp.dot(n, wo_ref[...], preferred_element_type=F32)
    xn_ref[0] = xn
    h = _rms(xn, gf_ref[...])
    h_ref[0] = h
    qp = jnp.dot(h.astype(BF16), wq_ref[...], preferred_element_type=F32)
    for hd in range(PEER_HEADS):
        q_ref[hd] = qp[:, hd * LANES:(hd + 1) * LANES]


def _mix_out(x, z_tm, attn, g_attn, w_out_attn_bf16, norm_ffn, w_query, *, ts=256):
    bsz, seq, dm = x.shape
    ts = min(ts, seq)
    ns = seq // ts
    row = pl.BlockSpec((1, ts, dm), lambda b, i: (b, i, 0))
    half = pl.BlockSpec((1, ts, ATTN_WIDTH), lambda b, i: (b, i, 0))
    full = lambda a: pl.BlockSpec(a.shape, lambda b, i: (0,) * a.ndim)
    ga, gf = g_attn.reshape(1, ATTN_WIDTH), norm_ffn.reshape(1, dm)
    qdim = w_query.shape[1] // PEER_HEADS
    return pl.pallas_call(
        _mix_out_kernel,
        grid=(bsz, ns),
        in_specs=[row, pl.BlockSpec((ts, dm), lambda b, i: (i, b)), half,
                  full(ga), full(w_out_attn_bf16), full(gf), full(w_query)],
        out_specs=[row, row, pl.BlockSpec((PEER_HEADS, ts, qdim), lambda b, i: (0, b * ns + i, 0))],
        out_shape=[jax.ShapeDtypeStruct((bsz, seq, dm), F32),
                   jax.ShapeDtypeStruct((bsz, seq, dm), F32),
                   jax.ShapeDtypeStruct((PEER_HEADS, bsz * seq, qdim), F32)],
        compiler_params=_cparams(("parallel", "arbitrary"), VMEM_LIMIT),
        name="mix_out",
    )(x, z_tm, attn, ga, w_out_attn_bf16, gf, w_query)


TOPK_TOKENS = SUBLANES * LANES
KEY_PITCH = PEER_KEYS + 4
_CANDIDATES = tuple((a, b) for a in range(PEER_TOPK) for b in range(PEER_TOPK) if (a + 1) * (b + 1) <= PEER_TOPK)


def _tree(op, xs):
    xs = list(xs)
    while len(xs) > 1:
        xs = [op(xs[i], xs[i + 1]) if i + 1 < len(xs) else xs[i] for i in range(0, len(xs), 2)]
    return xs[0]


def _extract16(problems):
    big = jnp.int32(2 ** 30)
    ninf = jnp.float32(-jnp.inf)

    def step(r, ms):
        nxt = []
        for p, m in zip(problems, ms):
            s_ref, order = p["s"], p["order"]
            n = len(order)
            am = _tree(jnp.minimum, [jnp.where(s_ref[k] == m, order[k], big) for k in range(n)])
            news, pays = [], []
            for k in range(n):
                hit = am == order[k]
                nk = jnp.where(hit, ninf, s_ref[k])
                s_ref[k] = nk
                news.append(nk)
                if p.get("pay") is not None:
                    pays.append(jnp.where(hit, p["pay"][k], -1))
            p["vals"][r] = m
            p["picks"][r] = _tree(jnp.maximum, pays) if pays else am
            nxt.append(_tree(jnp.maximum, news))
        return tuple(nxt)

    init = tuple(_tree(jnp.maximum, [p["s"][k] for k in range(len(p["order"]))]) for p in problems)
    lax.fori_loop(0, PEER_TOPK, step, init)


def _peer_topk_kernel(q_ref, k_ref, idx_ref, gate_ref,
                      slab_ref, s1_ref, s2_ref, t1_ref, i1_ref, t2_ref, i2_ref, cand_ref, pay_ref, ts_ref, ex_ref):
    dn = (((1,), (1,)), ((), ()))
    keys = tuple(range(PEER_KEYS))

    def head(h, carry):
        for w, s_ref in ((0, s1_ref), (1, s2_ref)):
            for j in range(SUBLANES):
                slab_ref[j * KEY_PITCH:j * KEY_PITCH + PEER_KEYS, :] = lax.dot_general(
                    k_ref[w, h], q_ref[h, j * LANES:(j + 1) * LANES, :].astype(BF16), dn,
                    preferred_element_type=F32)
            for k in range(PEER_KEYS):
                s_ref[k] = slab_ref[pl.ds(k, SUBLANES, stride=KEY_PITCH), :]
        _extract16([dict(s=s1_ref, order=keys, vals=t1_ref, picks=i1_ref),
                    dict(s=s2_ref, order=keys, vals=t2_ref, picks=i2_ref)])
        for c, (a, b) in enumerate(_CANDIDATES):
            cand_ref[c] = t1_ref[a] + t2_ref[b]
            pay_ref[c] = i1_ref[a] * PEER_KEYS + i2_ref[b]
        _extract16([dict(s=cand_ref, order=tuple(a * PEER_TOPK + b for a, b in _CANDIDATES), pay=pay_ref,
                         vals=ts_ref, picks=ex_ref)])
        top_s = ts_ref[...]
        e = jnp.exp(top_s - jnp.max(top_s, axis=0, keepdims=True))
        gate_ref[0, h] = e / jnp.sum(e, axis=0, keepdims=True)
        idx_ref[0, h] = ex_ref[...] * ROWS_PER_EXPERT
        return carry

    lax.fori_loop(0, PEER_HEADS, head, 0)


def _peer_topk(qp, keys_pad):
    n_tok = qp.shape[1]
    tt = TOPK_TOKENS
    assert n_tok % tt == 0
    shp = (n_tok // tt, PEER_HEADS, PEER_TOPK, SUBLANES, LANES)
    out = pl.BlockSpec((1,) + shp[1:], lambda i: (i, 0, 0, 0, 0))
    vregs = lambda n, dt: pltpu.VMEM((n, SUBLANES, LANES), dt)
    idx, gate = pl.pallas_call(
        _peer_topk_kernel,
        grid=(n_tok // tt,),
        in_specs=[pl.BlockSpec((PEER_HEADS, tt, qp.shape[2]), lambda i: (0, i, 0)),
                  pl.BlockSpec(keys_pad.shape, lambda i: (0, 0, 0, 0))],
        out_specs=[out, out],
        out_shape=[jax.ShapeDtypeStruct(shp, jnp.int32), jax.ShapeDtypeStruct(shp, F32)],
        scratch_shapes=[pltpu.VMEM((SUBLANES * KEY_PITCH, LANES), F32),
                        vregs(PEER_KEYS, F32), vregs(PEER_KEYS, F32),
                        vregs(PEER_TOPK, F32), vregs(PEER_TOPK, jnp.int32),
                        vregs(PEER_TOPK, F32), vregs(PEER_TOPK, jnp.int32),
                        vregs(len(_CANDIDATES), F32), vregs(len(_CANDIDATES), jnp.int32),
                        vregs(PEER_TOPK, F32), vregs(PEER_TOPK, jnp.int32)],
        compiler_params=_cparams(("parallel",), VMEM_LIMIT),
        name="peer_topk",
    )(qp, keys_pad)
    to_tok = lambda a: jnp.transpose(a, (0, 3, 4, 1, 2)).reshape(n_tok, PEER_SEL)
    return to_tok(idx), to_tok(gate)


ROWS_PER_EXPERT = 4
PAIR_CHUNK = 32
CHUNK_ROWS = PAIR_CHUNK * ROWS_PER_EXPERT
N_CHUNKS = PEER_SEL // PAIR_CHUNK
CHUNK_SHIFT = N_CHUNKS.bit_length() - 1
assert 1 << CHUNK_SHIFT == N_CHUNKS
_GROUP_SLOT = tuple(
    2 * (2 * (0 if (j % 2) else 1) + (1 if (j // 2) in (0, 2) else 0)) + (0 if (j // 2) < 2 else 1)
    for j in range(8))
_CHUNK_SLOT = tuple(8 * (j // 8) + _GROUP_SLOT[j % 8] for j in range(PAIR_CHUNK))


def _pack_table(tbl):
    e, dm = tbl.shape
    tb = tbl.astype(BF16)
    lo = lax.bitcast_convert_type(tb[:, :dm // 2], jnp.uint16).astype(jnp.uint32)
    hi = lax.bitcast_convert_type(tb[:, dm // 2:], jnp.uint16).astype(jnp.uint32)
    w = lax.bitcast_convert_type(lo | (hi << 16), jnp.int32)
    return w.reshape(e * ROWS_PER_EXPERT, LANES)


def _unpack_words(w):
    lo = lax.bitcast_convert_type(w << 16, F32)
    hi = lax.bitcast_convert_type(w & jnp.int32(-65536), F32)
    return lo, hi


def _gather_chunk(idx_ref, tab_ref, buf_ref, c):
    ids = idx_ref.at[pl.ds(pl.multiple_of(c * PAIR_CHUNK, PAIR_CHUNK), PAIR_CHUNK)]
    for j in range(PAIR_CHUNK):
        e4 = pl.multiple_of(ids[j], ROWS_PER_EXPERT)
        s = _CHUNK_SLOT[j] * ROWS_PER_EXPERT
        buf_ref[s:s + ROWS_PER_EXPERT, :] = tab_ref[pl.ds(e4, ROWS_PER_EXPERT), :]


def _chunk_loop(n_chunks, idx_ref, tab_ref, buf_a, buf_b, consume, init):
    last = n_chunks - 1
    _gather_chunk(idx_ref, tab_ref, buf_a, 0)

    def body(i, carry):
        c = 2 * i
        carry = consume(c, buf_a, carry)
        _gather_chunk(idx_ref, tab_ref, buf_b, c + 1)
        carry = consume(c + 1, buf_b, carry)
        _gather_chunk(idx_ref, tab_ref, buf_a, jnp.minimum(c + 2, last))
        return carry

    return lax.fori_loop(0, n_chunks // 2, body, init)


def _peer_u_kernel(idx_ref, ha_ref, hb_ref, gate_ref, tab_ref, coef_ref, buf_a, buf_b, r_ref, d_ref, *, tt):
    sub = lax.broadcasted_iota(jnp.int32, (SUBLANES, LANES), 0)
    m_a = ((sub % 4) >= 2)[None]
    m_b = ((sub % 2) == 1)[None]
    nv = PAIR_CHUNK // 2

    def pair_partials(c, buf_ref, carry):
        t = c >> CHUNK_SHIFT
        lo, hi = _unpack_words(buf_ref[...])
        x = lo.reshape(nv, SUBLANES, LANES) * ha_ref[t][None] + hi.reshape(nv, SUBLANES, LANES) * hb_ref[t][None]
        x = x.reshape(nv // 2, 2, SUBLANES, LANES)
        xe, xo = x[:, 0], x[:, 1]
        a = xe + pltpu.roll(xe, 2, axis=1)
        b = xo + pltpu.roll(xo, 6, axis=1)
        m = jnp.where(m_a, a, b).reshape(nv // 4, 2, SUBLANES, LANES)
        me, mo = m[:, 0], m[:, 1]
        a2 = me + pltpu.roll(me, 1, axis=1)
        b2 = mo + pltpu.roll(mo, 7, axis=1)
        r_ref[pl.ds(pl.multiple_of(c * PAIR_CHUNK, PAIR_CHUNK), PAIR_CHUNK), :] = (
            jnp.where(m_b, a2, b2).reshape(PAIR_CHUNK, LANES))
        return carry

    _chunk_loop(tt * N_CHUNKS, idx_ref, tab_ref, buf_a, buf_b, pair_partials, 0)

    eye = (lax.broadcasted_iota(jnp.int32, (PEER_SEL, LANES), 0)
           == lax.broadcasted_iota(jnp.int32, (PEER_SEL, LANES), 1))

    def lane_sums(g, carry):
        for i in range(SUBLANES):
            t = g * SUBLANES + i
            s = jnp.sum(r_ref[pl.ds(pl.multiple_of(t * PEER_SEL, PEER_SEL), PEER_SEL), :], axis=-1, keepdims=True)
            d_ref[pl.ds(t, 1), :] = jnp.sum(jnp.where(eye, s, 0.0), axis=0, keepdims=True)
        return carry

    lax.fori_loop(0, tt // SUBLANES, lane_sums, 0)
    coef_ref[...] = gate_ref[...] * _gelu(d_ref[...])


def _peer_u(idx4, ha, hb, gates, tab, *, tt=128):
    n_tok = ha.shape[0]
    tt = min(tt, n_tok)
    tok2 = pl.BlockSpec((tt, PEER_SEL), lambda i: (i, 0))
    tok3 = pl.BlockSpec((tt, SUBLANES, LANES), lambda i: (i, 0, 0))
    return pl.pallas_call(
        functools.partial(_peer_u_kernel, tt=tt),
        grid=(n_tok // tt,),
        in_specs=[pl.BlockSpec((tt * PEER_SEL,), lambda i: (i,), memory_space=pltpu.SMEM),
                  tok3, tok3, tok2,
                  pl.BlockSpec(tab.shape, lambda i: (0, 0), pipeline_mode=pl.Buffered(1))],
        out_specs=tok2,
        out_shape=jax.ShapeDtypeStruct((n_tok, PEER_SEL), F32),
        scratch_shapes=[pltpu.VMEM((CHUNK_ROWS, LANES), jnp.int32),
                        pltpu.VMEM((CHUNK_ROWS, LANES), jnp.int32),
                        pltpu.VMEM((tt * PEER_SEL, LANES), F32),
                        pltpu.VMEM((tt, PEER_SEL), F32)],
        compiler_params=_cparams(("arbitrary",), VMEM_LIMIT),
        name="peer_u",
    )(idx4, ha, hb, gates, tab)


def _peer_v_kernel(idx_ref, coef_ref, tab_ref, o_ref, buf_a, buf_b, *, tt):
    sub = lax.broadcasted_iota(jnp.int32, (SUBLANES, LANES), 0)
    low = sub < ROWS_PER_EXPERT
    nv = PAIR_CHUNK // 2

    def accumulate(c, buf_ref, acc):
        acc_lo, acc_hi = acc
        cs = coef_ref.at[pl.ds(pl.multiple_of(c * nv, nv), nv)]
        lo, hi = _unpack_words(buf_ref[...])
        lo = lo.reshape(nv, SUBLANES, LANES)
        hi = hi.reshape(nv, SUBLANES, LANES)
        for v in range(nv):
            c_lo, c_hi = _unpack_words(jnp.full((SUBLANES, LANES), cs[v], jnp.int32))
            cv = jnp.where(low, c_lo, c_hi)
            acc_lo = acc_lo + cv * lo[v]
            acc_hi = acc_hi + cv * hi[v]
        lo4 = acc_lo + pltpu.roll(acc_lo, ROWS_PER_EXPERT, axis=0)
        hi4 = acc_hi + pltpu.roll(acc_hi, ROWS_PER_EXPERT, axis=0)
        o_ref[c >> CHUNK_SHIFT] = jnp.where(low, lo4, hi4)
        last = (c & (N_CHUNKS - 1)) == N_CHUNKS - 1
        return jnp.where(last, 0.0, acc_lo), jnp.where(last, 0.0, acc_hi)

    z = jnp.zeros((SUBLANES, LANES), F32)
    _chunk_loop(tt * N_CHUNKS, idx_ref, tab_ref, buf_a, buf_b, accumulate, (z, z))


def _pack_coefs(coef):
    n_tok = coef.shape[0]
    pair_at = {s: j for j, s in enumerate(_CHUNK_SLOT)}
    first = np.array([ch * PAIR_CHUNK + pair_at[2 * v] for ch in range(N_CHUNKS) for v in range(PAIR_CHUNK // 2)])
    second = np.array([ch * PAIR_CHUNK + pair_at[2 * v + 1] for ch in range(N_CHUNKS) for v in range(PAIR_CHUNK // 2)])
    bits = lax.bitcast_convert_type(coef.astype(BF16), jnp.uint16).astype(jnp.uint32)
    w = bits[:, first] | (bits[:, second] << 16)
    return lax.bitcast_convert_type(w, jnp.int32).reshape(n_tok * (PEER_SEL // 2))


def _peer_v(idx4, coef_words, tab, *, tt=128):
    n_tok = idx4.shape[0] // PEER_SEL
    tt = min(tt, n_tok)
    smem = pl.BlockSpec((tt * PEER_SEL,), lambda i: (i,), memory_space=pltpu.SMEM)
    return pl.pallas_call(
        functools.partial(_peer_v_kernel, tt=tt),
        grid=(n_tok // tt,),
        in_specs=[smem, pl.BlockSpec((tt * PEER_SEL // 2,), lambda i: (i,), memory_space=pltpu.SMEM),
                  pl.BlockSpec(tab.shape, lambda i: (0, 0), pipeline_mode=pl.Buffered(1))],
        out_specs=pl.BlockSpec((tt, SUBLANES, LANES), lambda i: (i, 0, 0)),
        out_shape=jax.ShapeDtypeStruct((n_tok, SUBLANES, LANES), F32),
        scratch_shapes=[pltpu.VMEM((CHUNK_ROWS, LANES), jnp.int32),
                        pltpu.VMEM((CHUNK_ROWS, LANES), jnp.int32)],
        compiler_params=_cparams(("arbitrary",), VMEM_LIMIT),
        name="peer_v",
    )(idx4, coef_words, tab)


def _peer(h, qp, sub_keys, expert_u, expert_v):
    bsz, seq, dm = h.shape
    n_tok = bsz * seq
    half = sub_keys.shape[-1]
    z = jnp.zeros_like(sub_keys[0])
    keys_pad = jnp.stack([jnp.concatenate([sub_keys[0], z], axis=-1),
                          jnp.concatenate([z, sub_keys[1]], axis=-1)]).astype(BF16)
    assert keys_pad.shape[-1] == 2 * half == qp.shape[-1]
    idx4, gates = _peer_topk(qp, keys_pad)
    hr = h.reshape(n_tok, 2, dm // (2 * LANES), LANES)
    ha = jnp.concatenate([hr[:, 0], hr[:, 0]], axis=1)
    hb = jnp.concatenate([hr[:, 1], hr[:, 1]], axis=1)
    idx4 = idx4.reshape(n_tok * PEER_SEL)
    coef = _peer_u(idx4, ha, hb, gates, _pack_table(expert_u))
    out = _peer_v(idx4, _pack_coefs(coef), _pack_table(expert_v))
    return out.reshape(bsz, seq, dm)


def _final_kernel(x_ref, p_ref, g_ref, o_ref):
    o_ref[...] = _rms(x_ref[...] + p_ref[...], g_ref[...])


def _final_norm(x, p, g, *, tr=1024):
    rows, dm = x.shape
    tr = min(tr, rows)
    blk = pl.BlockSpec((tr, dm), lambda i: (i, 0))
    return pl.pallas_call(
        _final_kernel,
        grid=(rows // tr,),
        in_specs=[blk, blk, pl.BlockSpec((1, dm), lambda i: (0, 0))],
        out_specs=blk,
        out_shape=jax.ShapeDtypeStruct((rows, dm), F32),
        compiler_params=_cparams(("parallel",), VMEM_LIMIT),
        name="final_norm",
    )(x, p, g.reshape(1, dm))


def kernel(x, w_in, w_out, rel_bias, g_attn, g_ssm, norm_mix, norm_ffn, lam_re, lam_im, log_step, b_re, b_im, c_re, c_im, d_skip, w_glu, w_query, sub_keys, expert_u, expert_v, norm_final):
    bsz, seq, dm = x.shape
    depth = w_in.shape[0]
    prev = None
    biases = [_attn_bias_tables(rel_bias, d) for _, d in DILATED_PATTERNS]
    for l in range(depth):
        x, q, k, v, u = _in_proj(x, prev, norm_mix[l], w_in[l].astype(BF16))
        attn = _attention(q, k, v, biases)
        u_tm = u.reshape(seq * bsz, SSM_WIDTH)
        a, bw, cw = _ssm_params(lam_re[l], lam_im[l], log_step[l], b_re[l], b_im[l], c_re[l], c_im[l])
        y = _ssm_scan(u_tm, a, bw, cw, bsz=bsz)
        wo = w_out[l].astype(BF16)
        z = _ssm_post(y, u_tm, d_skip[l], w_glu[l].astype(BF16), g_ssm[l], wo[ATTN_WIDTH:])
        x, h, qp = _mix_out(x, z.reshape(seq, bsz * dm), attn,
                            g_attn[l], wo[:ATTN_WIDTH], norm_ffn[l], w_query[l].astype(BF16))
        prev = _peer(h, qp, sub_keys[l], expert_u[l], expert_v[l])
    out = _final_norm(x.reshape(bsz * seq, dm), prev.reshape(bsz * seq, dm), norm_final)
    return out.reshape(bsz, seq, dm)
```

```python
import functools
import math

import numpy as np
import jax
import jax.numpy as jnp
from jax import lax
from jax.experimental import pallas as pl
from jax.experimental.pallas import tpu as pltpu

F32 = jnp.float32
BF16 = jnp.bfloat16

EPS = 1e-6
NEG_INF = -1e30
HEAD_DIM = 64
ATTN_WIDTH = 512
SSM_WIDTH = 512
SSM_GROUP = 16
SSM_STATE = 64
DILATED_PATTERNS = ((128, 1), (512, 4), (2048, 16))
REL_BUCKETS = 32
REL_MAX_DISTANCE = 1024
PEER_HEADS = 8
PEER_KEYS = 128
PEER_TOPK = 16
PEER_SEL = PEER_HEADS * PEER_TOPK

LANES = 128
SUBLANES = 8
QBLK = 128
KWIN = 256
BAND = 64
VMEM_LIMIT = 52 * 1024 * 1024


def _cparams(sem, vmem=None):
    return pltpu.CompilerParams(dimension_semantics=sem, vmem_limit_bytes=vmem)


def _rms(x, g):
    return x * lax.rsqrt(jnp.mean(x * x, axis=-1, keepdims=True) + EPS) * g


def _gelu(x):
    return 0.5 * x * (1.0 + lax.erf(x * (1.0 / math.sqrt(2.0))))


def _rows_from_chunks(p_ref, n_rows):
    return jnp.concatenate([p_ref[pl.ds(c, n_rows, stride=SUBLANES), :] for c in range(SUBLANES)], axis=-1)


def _in_proj_kernel(*refs, has_prev):
    if has_prev:
        x_ref, p_ref, g_ref, w_ref, xo_ref, q_ref, k_ref, v_ref, u_ref = refs
        x = x_ref[0] + _rows_from_chunks(p_ref, x_ref.shape[1])
    else:
        x_ref, g_ref, w_ref, xo_ref, q_ref, k_ref, v_ref, u_ref = refs
        x = x_ref[0]
    xo_ref[0] = x
    h = _rms(x, g_ref[...]).astype(BF16)
    proj = jnp.dot(h, w_ref[...], preferred_element_type=F32)
    a = ATTN_WIDTH
    q_ref[0] = proj[:, :a] * (HEAD_DIM ** -0.5)
    k_ref[0] = proj[:, a:2 * a]
    v_ref[0] = proj[:, 2 * a:3 * a]
    u_ref[...] = proj[:, 3 * a:]


def _in_proj(x, prev, g, w_bf16, *, ts=512):
    bsz, seq, dm = x.shape
    ts = min(ts, seq)
    row = pl.BlockSpec((1, ts, dm), lambda b, i: (b, i, 0))
    qkv = pl.BlockSpec((1, ts, ATTN_WIDTH), lambda b, i: (b, i, 0))
    ns = seq // ts
    chunks = pl.BlockSpec((ts * SUBLANES, LANES), lambda b, i: (b * ns + i, 0))
    ins = [x] + ([prev] if prev is not None else []) + [g.reshape(1, dm), w_bf16]
    in_specs = [row] + ([chunks] if prev is not None else []) + [
        pl.BlockSpec((1, dm), lambda b, i: (0, 0)),
        pl.BlockSpec(w_bf16.shape, lambda b, i: (0, 0)),
    ]
    return pl.pallas_call(
        functools.partial(_in_proj_kernel, has_prev=prev is not None),
        grid=(bsz, seq // ts),
        in_specs=in_specs,
        out_specs=[row, qkv, qkv, qkv, pl.BlockSpec((ts, SSM_WIDTH), lambda b, i: (i, b))],
        out_shape=[
            jax.ShapeDtypeStruct((bsz, seq, dm), F32),
            jax.ShapeDtypeStruct((bsz, seq, ATTN_WIDTH), F32),
            jax.ShapeDtypeStruct((bsz, seq, ATTN_WIDTH), F32),
            jax.ShapeDtypeStruct((bsz, seq, ATTN_WIDTH), F32),
            jax.ShapeDtypeStruct((seq, bsz * SSM_WIDTH), F32),
        ],
        compiler_params=_cparams(("parallel", "arbitrary"), VMEM_LIMIT),
        name="in_proj",
    )(*ins)


def _t5_buckets(rel):
    half = REL_BUCKETS // 2
    max_exact = half // 2
    n = np.abs(rel)
    large = max_exact + (np.log(np.maximum(n, 1) / max_exact)
                         / np.log(REL_MAX_DISTANCE / max_exact) * (half - max_exact)).astype(np.int32)
    large = np.minimum(large, half - 1)
    return (np.where(rel > 0, half, 0) + np.where(n < max_exact, n, large)).astype(np.int32)


def _attn_bias_tables(rel_bias, dilation):
    ql = np.arange(QBLK)[:, None]
    kl = np.arange(KWIN)[None, :]
    delta = np.stack([kl + off - ql for off in (0, -BAND, -2 * BAND)])
    buckets = np.where(np.abs(delta) <= BAND, _t5_buckets(delta * dilation), -1)
    rb = rel_bias.astype(F32).T
    bk = jnp.asarray(buckets, jnp.int32)[None]
    tab = jnp.full((rb.shape[0],) + buckets.shape, NEG_INF, F32)
    for b in range(REL_BUCKETS):
        tab = jnp.where(bk == b, rb[:, b][:, None, None, None], tab)
    return tab


BLOCKS_PER_STEP = 2


def _attn_kernel(q_ref, k_ref, v_ref, *rest, seq):
    bias_refs, (o_ref, acc_ref, m_ref, z_ref) = rest[:len(DILATED_PATTERNS)], rest[len(DILATED_PATTERNS):]
    lane = lax.broadcasted_iota(jnp.int32, (QBLK, LANES), 1)
    is_h0 = lane < HEAD_DIM
    dn = (((1,), (1,)), ((), ()))
    nsteps = seq // QBLK

    def rows(start, size, d):
        return pl.ds(start, size) if d == 1 else pl.ds(start, size, stride=d)

    def block(n, d, bias_ref, first, last):
        length = seq // d
        nblk = length // QBLK
        r, i = n >> (nblk.bit_length() - 1), n & (nblk - 1)
        s = i * QBLK
        ks = jnp.clip(s - BAND, 0, length - KWIN)
        var = jnp.where(i == 0, 0, jnp.where(i == nblk - 1, 2, 1))
        q_rows = rows(r + d * s, QBLK, d)
        k_rows = rows(r + d * ks, KWIN, d)
        qb = q_ref[q_rows, :].astype(BF16)
        kb = k_ref[k_rows, :].astype(BF16)
        vb = v_ref[k_rows, :].astype(BF16)
        outs, ms, zs = [], [], []
        for h in range(2):
            keep = is_h0 if h == 0 else jnp.logical_not(is_h0)
            qh = jnp.where(keep, qb, jnp.zeros_like(qb))
            logits = lax.dot_general(qh, kb, dn, preferred_element_type=F32) + bias_ref[h, var]
            m = jnp.max(logits, axis=-1, keepdims=True)
            p = jnp.exp(logits - m)
            outs.append(jnp.dot(p.astype(BF16), vb, preferred_element_type=F32))
            ms.append(jnp.broadcast_to(m, (QBLK, LANES)))
            zs.append(jnp.broadcast_to(jnp.sum(p, axis=-1, keepdims=True), (QBLK, LANES)))
        o = jnp.where(is_h0, outs[0], outs[1])
        m = jnp.where(is_h0, ms[0], ms[1])
        z = jnp.where(is_h0, zs[0], zs[1])
        if not first:
            m_old = m_ref[q_rows, :]
            m_new = jnp.maximum(m_old, m)
            a, b = jnp.exp(m_old - m_new), jnp.exp(m - m_new)
            o = acc_ref[q_rows, :] * a + o * b
            z = z_ref[q_rows, :] * a + z * b
            m = m_new
        if last:
            o_ref[q_rows, :] = o / z
        else:
            acc_ref[q_rows, :] = o
            m_ref[q_rows, :] = m
            z_ref[q_rows, :] = z

    for p, ((_, d), bias_ref) in enumerate(zip(DILATED_PATTERNS, bias_refs)):
        def step(g, carry, d=d, bias_ref=bias_ref, p=p):
            for j in range(BLOCKS_PER_STEP):
                block(g * BLOCKS_PER_STEP + j, d, bias_ref, p == 0, p == len(DILATED_PATTERNS) - 1)
            return carry
        lax.fori_loop(0, nsteps // BLOCKS_PER_STEP, step, 0)


def _attention(q, k, v, biases):
    bsz, seq, width = q.shape
    for _, d in DILATED_PATTERNS:
        length = seq // d
        assert length >= KWIN and length % QBLK == 0 and (length // QBLK) & (length // QBLK - 1) == 0
    assert (seq // QBLK) % BLOCKS_PER_STEP == 0
    blk = pl.BlockSpec((None, seq, LANES), lambda b, c: (b, 0, c))
    bias_spec = pl.BlockSpec((2, 3, QBLK, KWIN), lambda b, c: (c, 0, 0, 0))
    return pl.pallas_call(
        functools.partial(_attn_kernel, seq=seq),
        grid=(bsz, width // LANES),
        in_specs=[blk, blk, blk] + [bias_spec] * len(biases),
        out_specs=blk,
        out_shape=jax.ShapeDtypeStruct((bsz, seq, width), F32),
        scratch_shapes=[pltpu.VMEM((seq, LANES), F32)] * 3,
        compiler_params=_cparams(("parallel", "arbitrary"), VMEM_LIMIT),
        name="attention",
    )(q, k, v, *biases)


SSM_LANE_GROUPS = SSM_WIDTH // LANES
SSM_GB_STATES = (LANES // SSM_GROUP) * SSM_STATE


def _ssm_params(lam_re, lam_im, log_step, b_re, b_im, c_re, c_im):
    f = lambda t: t.astype(F32)
    lr, li = f(lam_re), f(lam_im)
    step = jnp.exp(f(log_step))[..., None]
    mag = jnp.exp(lr * step)
    ar, ai = mag * jnp.cos(li * step), mag * jnp.sin(li * step)
    nr, ni = ar - 1.0, ai
    den = lr * lr + li * li
    cr, ci = (nr * lr + ni * li) / den, (ni * lr - nr * li) / den
    br, bi = f(b_re), f(b_im)
    bbr = cr[..., None] * br - ci[..., None] * bi
    bbi = cr[..., None] * bi + ci[..., None] * br
    gpb = LANES // SSM_GROUP
    eye = jnp.eye(gpb, dtype=F32)

    def in_map(t):
        t = t.reshape(2, SSM_LANE_GROUPS, gpb, SSM_STATE, SSM_GROUP)
        return jnp.einsum('dbgpc,gh->dbgchp', t, eye).reshape(2, SSM_LANE_GROUPS, LANES, SSM_GB_STATES)

    def out_map(t):
        t = t.reshape(2, SSM_LANE_GROUPS, gpb, SSM_GROUP, SSM_STATE)
        return jnp.einsum('dbgcp,gh->dbgphc', t, eye).reshape(2, SSM_LANE_GROUPS, SSM_GB_STATES, LANES)

    bw = jnp.concatenate([in_map(bbr), in_map(bbi)], axis=-1).astype(BF16)
    cw = jnp.concatenate([out_map(f(c_re)), -out_map(f(c_im))], axis=-2).astype(BF16)
    a = jnp.stack([ar.reshape(2, SSM_LANE_GROUPS, SSM_GB_STATES),
                   ai.reshape(2, SSM_LANE_GROUPS, SSM_GB_STATES)], axis=2)
    return a, bw, cw


def _ssm_kernel(u_ref, a_ref, bw_ref, cw_ref, y_ref, st_ref, bu_ref, *, ts, bsz):
    d = pl.program_id(0)
    ns = SSM_GB_STATES

    @pl.when(pl.program_id(1) == 0)
    def _():
        st_ref[...] = jnp.zeros_like(st_ref)

    for gb in range(SSM_LANE_GROUPS):
        ub = u_ref[:, gb * LANES:(gb + 1) * LANES].astype(BF16)
        bu_ref[...] = jnp.dot(ub, bw_ref[0, gb], preferred_element_type=F32)
        ar = jnp.broadcast_to(a_ref[0, gb, 0:1, :], (bsz, ns))
        ai = jnp.broadcast_to(a_ref[0, gb, 1:2, :], (bsz, ns))

        def step(j, carry, ar=ar, ai=ai):
            xr, xi = carry
            tl = jnp.where(d == 0, j, ts - 1 - j)
            r = pl.multiple_of(tl * bsz, bsz)
            nr = ar * xr - ai * xi + bu_ref[pl.ds(r, bsz), :ns]
            ni = ar * xi + ai * xr + bu_ref[pl.ds(r, bsz), ns:]
            bu_ref[pl.ds(r, bsz), :ns] = nr
            bu_ref[pl.ds(r, bsz), ns:] = ni
            return nr, ni

        xr, xi = lax.fori_loop(0, ts, step, (st_ref[gb, :, :ns], st_ref[gb, :, ns:]))
        st_ref[gb, :, :ns] = xr
        st_ref[gb, :, ns:] = xi
        y_ref[0, :, gb * LANES:(gb + 1) * LANES] = jnp.dot(
            bu_ref[...].astype(BF16), cw_ref[0, gb], preferred_element_type=F32)


def _ssm_scan(u_tm, a, bw, cw, *, bsz, ts=64):
    rows, width = u_tm.shape
    seq = rows // bsz
    ts = min(ts, seq)
    nt = seq // ts
    tblk = lambda d, i: jnp.where(d == 0, i, nt - 1 - i)
    return pl.pallas_call(
        functools.partial(_ssm_kernel, ts=ts, bsz=bsz),
        grid=(2, nt),
        in_specs=[
            pl.BlockSpec((ts * bsz, width), lambda d, i: (tblk(d, i), 0)),
            pl.BlockSpec((1,) + a.shape[1:], lambda d, i: (d, 0, 0, 0)),
            pl.BlockSpec((1,) + bw.shape[1:], lambda d, i: (d, 0, 0, 0)),
            pl.BlockSpec((1,) + cw.shape[1:], lambda d, i: (d, 0, 0, 0)),
        ],
        out_specs=pl.BlockSpec((1, ts * bsz, width), lambda d, i: (d, tblk(d, i), 0)),
        out_shape=jax.ShapeDtypeStruct((2, rows, width), F32),
        scratch_shapes=[pltpu.VMEM((SSM_LANE_GROUPS, bsz, 2 * SSM_GB_STATES), F32),
                        pltpu.VMEM((ts * bsz, 2 * SSM_GB_STATES), F32)],
        compiler_params=_cparams(("arbitrary", "arbitrary"), VMEM_LIMIT),
        name="ssm_scan",
    )(u_tm, a, bw, cw)


def _ssm_post_kernel(y_ref, u_ref, d_ref, wg_ref, g_ref, wo_ref, z_ref):
    y = _gelu(y_ref[0] + y_ref[1] + d_ref[...] * u_ref[...]).astype(BF16)
    ab = jnp.dot(y, wg_ref[...], preferred_element_type=F32)
    ssm = ab[:, :SSM_WIDTH] * jax.nn.sigmoid(ab[:, SSM_WIDTH:])
    n = _rms(ssm, g_ref[...]).astype(BF16)
    z_ref[...] = jnp.dot(n, wo_ref[...], preferred_element_type=F32)


def _ssm_post(y, u_tm, d_skip, w_glu_bf16, g_ssm, w_out_ssm_bf16, *, tr=512):
    rows, width = u_tm.shape
    tr = min(tr, rows)
    dm = w_out_ssm_bf16.shape[1]
    full = lambda a: pl.BlockSpec(a.shape, lambda i: (0,) * a.ndim)
    d2, g2 = d_skip.reshape(1, width), g_ssm.reshape(1, width)
    return pl.pallas_call(
        _ssm_post_kernel,
        grid=(rows // tr,),
        in_specs=[pl.BlockSpec((2, tr, width), lambda i: (0, i, 0)),
                  pl.BlockSpec((tr, width), lambda i: (i, 0)),
                  full(d2), full(w_glu_bf16), full(g2), full(w_out_ssm_bf16)],
        out_specs=pl.BlockSpec((tr, dm), lambda i: (i, 0)),
        out_shape=jax.ShapeDtypeStruct((rows, dm), F32),
        compiler_params=_cparams(("parallel",), VMEM_LIMIT),
        name="ssm_post",
    )(y, u_tm, d2, w_glu_bf16, g2, w_out_ssm_bf16)


def _mix_out_kernel(x_ref, z_ref, a_ref, ga_ref, wo_ref, gf_ref, wq_ref, xn_ref, h_ref, q_ref):
    n = _rms(a_ref[0], ga_ref[...]).astype(BF16)
    xn = x_ref[0] + z_ref[...] + jnp.dot(n, wo_ref[...], preferred_element_type=F32)
    xn_ref[0] = xn
    h = _rms(xn, gf_ref[...])
    for c in range(SUBLANES):
        h_ref[pl.ds(c, h.shape[0], stride=SUBLANES), :] = h[:, c * LANES:(c + 1) * LANES]
    qp = jnp.dot(h.astype(BF16), wq_ref[...], preferred_element_type=F32)
    for hd in range(PEER_HEADS):
        q_ref[hd] = qp[:, hd * LANES:(hd + 1) * LANES]


def _mix_out(x, z_tm, attn, g_attn, w_out_attn_bf16, norm_ffn, w_query, *, ts=256):
    bsz, seq, dm = x.shape
    ts = min(ts, seq)
    ns = seq // ts
    row = pl.BlockSpec((1, ts, dm), lambda b, i: (b, i, 0))
    half = pl.BlockSpec((1, ts, ATTN_WIDTH), lambda b, i: (b, i, 0))
    full = lambda a: pl.BlockSpec(a.shape, lambda b, i: (0,) * a.ndim)
    ga, gf = g_attn.reshape(1, ATTN_WIDTH), norm_ffn.reshape(1, dm)
    qdim = w_query.shape[1] // PEER_HEADS
    return pl.pallas_call(
        _mix_out_kernel,
        grid=(bsz, ns),
        in_specs=[row, pl.BlockSpec((ts, dm), lambda b, i: (i, b)), half,
                  full(ga), full(w_out_attn_bf16), full(gf), full(w_query)],
        out_specs=[row, pl.BlockSpec((ts * SUBLANES, LANES), lambda b, i: (b * ns + i, 0)),
                   pl.BlockSpec((PEER_HEADS, ts, qdim), lambda b, i: (0, b * ns + i, 0))],
        out_shape=[jax.ShapeDtypeStruct((bsz, seq, dm), F32),
                   jax.ShapeDtypeStruct((bsz * seq * SUBLANES, LANES), F32),

                   jax.ShapeDtypeStruct((PEER_HEADS, bsz * seq, qdim), F32)],
        compiler_params=_cparams(("parallel", "arbitrary"), VMEM_LIMIT),
        name="mix_out",
    )(x, z_tm, attn, ga, w_out_attn_bf16, gf, w_query)


TOPK_TOKENS = SUBLANES * LANES
KEY_PITCH = PEER_KEYS + 4
_CANDIDATES = tuple((a, b) for a in range(PEER_TOPK) for b in range(PEER_TOPK) if (a + 1) * (b + 1) <= PEER_TOPK)


def _tree(op, xs):
    xs = list(xs)
    while len(xs) > 1:
        xs = [op(xs[i], xs[i + 1]) if i + 1 < len(xs) else xs[i] for i in range(0, len(xs), 2)]
    return xs[0]


def _extract16(problems):
    big = jnp.int32(2 ** 30)
    ninf = jnp.float32(-jnp.inf)

    def step(r, ms):
        nxt = []
        for p, m in zip(problems, ms):
            s_ref, order = p["s"], p["order"]
            n = len(order)
            am = _tree(jnp.minimum, [jnp.where(s_ref[k] == m, order[k], big) for k in range(n)])
            news, pays = [], []
            for k in range(n):
                hit = am == order[k]
                nk = jnp.where(hit, ninf, s_ref[k])
                s_ref[k] = nk
                news.append(nk)
                if p.get("pay") is not None:
                    pays.append(jnp.where(hit, p["pay"][k], -1))
            p["vals"][r] = m
            p["picks"][r] = _tree(jnp.maximum, pays) if pays else am
            nxt.append(_tree(jnp.maximum, news))
        return tuple(nxt)

    init = tuple(_tree(jnp.maximum, [p["s"][k] for k in range(len(p["order"]))]) for p in problems)
    lax.fori_loop(0, PEER_TOPK, step, init)


def _peer_topk_kernel(q_ref, k_ref, idx_ref, gate_ref,
                      slab_ref, s1_ref, s2_ref, t1_ref, i1_ref, t2_ref, i2_ref, cand_ref, pay_ref, ts_ref, ex_ref):
    dn = (((1,), (1,)), ((), ()))
    keys = tuple(range(PEER_KEYS))

    def head(h, carry):
        for w, s_ref in ((0, s1_ref), (1, s2_ref)):
            for j in range(SUBLANES):
                slab_ref[j * KEY_PITCH:j * KEY_PITCH + PEER_KEYS, :] = lax.dot_general(
                    k_ref[w, h], q_ref[h, j * LANES:(j + 1) * LANES, :].astype(BF16), dn,
                    preferred_element_type=F32)
            for k in range(PEER_KEYS):
                s_ref[k] = slab_ref[pl.ds(k, SUBLANES, stride=KEY_PITCH), :]
        _extract16([dict(s=s1_ref, order=keys, vals=t1_ref, picks=i1_ref),
                    dict(s=s2_ref, order=keys, vals=t2_ref, picks=i2_ref)])
        for c, (a, b) in enumerate(_CANDIDATES):
            cand_ref[c] = t1_ref[a] + t2_ref[b]
            pay_ref[c] = i1_ref[a] * PEER_KEYS + i2_ref[b]
        _extract16([dict(s=cand_ref, order=tuple(a * PEER_TOPK + b for a, b in _CANDIDATES), pay=pay_ref,
                         vals=ts_ref, picks=ex_ref)])
        top_s = ts_ref[...]
        e = jnp.exp(top_s - jnp.max(top_s, axis=0, keepdims=True))
        gate_ref[0, h] = e / jnp.sum(e, axis=0, keepdims=True)
        idx_ref[0, h] = ex_ref[...] * ROWS_PER_EXPERT
        return carry

    lax.fori_loop(0, PEER_HEADS, head, 0)


def _peer_topk(qp, keys_pad):
    n_tok = qp.shape[1]
    tt = TOPK_TOKENS
    assert n_tok % tt == 0
    shp = (n_tok // tt, PEER_HEADS, PEER_TOPK, SUBLANES, LANES)
    out = pl.BlockSpec((1,) + shp[1:], lambda i: (i, 0, 0, 0, 0))
    vregs = lambda n, dt: pltpu.VMEM((n, SUBLANES, LANES), dt)
    idx, gate = pl.pallas_call(
        _peer_topk_kernel,
        grid=(n_tok // tt,),
        in_specs=[pl.BlockSpec((PEER_HEADS, tt, qp.shape[2]), lambda i: (0, i, 0)),
                  pl.BlockSpec(keys_pad.shape, lambda i: (0, 0, 0, 0))],
        out_specs=[out, out],
        out_shape=[jax.ShapeDtypeStruct(shp, jnp.int32), jax.ShapeDtypeStruct(shp, F32)],
        scratch_shapes=[pltpu.VMEM((SUBLANES * KEY_PITCH, LANES), F32),
                        vregs(PEER_KEYS, F32), vregs(PEER_KEYS, F32),
                        vregs(PEER_TOPK, F32), vregs(PEER_TOPK, jnp.int32),
                        vregs(PEER_TOPK, F32), vregs(PEER_TOPK, jnp.int32),
                        vregs(len(_CANDIDATES), F32), vregs(len(_CANDIDATES), jnp.int32),
                        vregs(PEER_TOPK, F32), vregs(PEER_TOPK, jnp.int32)],
        compiler_params=_cparams(("parallel",), VMEM_LIMIT),
        name="peer_topk",
    )(qp, keys_pad)
    to_tok = lambda a: jnp.transpose(a, (0, 3, 4, 1, 2)).reshape(n_tok, PEER_SEL)
    return to_tok(idx), to_tok(gate)


ROWS_PER_EXPERT = 4
PAIR_CHUNK = 32
CHUNK_ROWS = PAIR_CHUNK * ROWS_PER_EXPERT
N_CHUNKS = PEER_SEL // PAIR_CHUNK
CHUNK_SHIFT = N_CHUNKS.bit_length() - 1
assert 1 << CHUNK_SHIFT == N_CHUNKS
_GROUP_SLOT = tuple(
    2 * (2 * (0 if (j % 2) else 1) + (1 if (j // 2) in (0, 2) else 0)) + (0 if (j // 2) < 2 else 1)
    for j in range(8))
_CHUNK_SLOT = tuple(8 * (j // 8) + _GROUP_SLOT[j % 8] for j in range(PAIR_CHUNK))


def _pack_table(tbl):
    e, dm = tbl.shape
    tb = tbl.astype(BF16)
    lo = lax.bitcast_convert_type(tb[:, :dm // 2], jnp.uint16).astype(jnp.uint32)
    hi = lax.bitcast_convert_type(tb[:, dm // 2:], jnp.uint16).astype(jnp.uint32)
    w = lax.bitcast_convert_type(lo | (hi << 16), jnp.int32)
    return w.reshape(e * ROWS_PER_EXPERT, LANES)


def _unpack_words(w):
    lo = lax.bitcast_convert_type(w << 16, F32)
    hi = lax.bitcast_convert_type(w & jnp.int32(-65536), F32)
    return lo, hi


def _gather_chunk(idx_ref, tab_ref, buf_ref, c):
    ids = idx_ref.at[pl.ds(pl.multiple_of(c * PAIR_CHUNK, PAIR_CHUNK), PAIR_CHUNK)]
    for j in range(PAIR_CHUNK):
        e4 = pl.multiple_of(ids[j], ROWS_PER_EXPERT)
        s = _CHUNK_SLOT[j] * ROWS_PER_EXPERT
        buf_ref[s:s + ROWS_PER_EXPERT, :] = tab_ref[pl.ds(e4, ROWS_PER_EXPERT), :]


def _chunk_loop(n_chunks, idx_ref, tab_ref, buf_a, buf_b, consume, init):
    last = n_chunks - 1
    _gather_chunk(idx_ref, tab_ref, buf_a, 0)

    def body(i, carry):
        c = 2 * i
        carry = consume(c, buf_a, carry)
        _gather_chunk(idx_ref, tab_ref, buf_b, c + 1)
        carry = consume(c + 1, buf_b, carry)
        _gather_chunk(idx_ref, tab_ref, buf_a, jnp.minimum(c + 2, last))
        return carry

    return lax.fori_loop(0, n_chunks // 2, body, init)


def _peer_u_kernel(idx_ref, h_ref, gate_ref, tab_ref, coef_ref, buf_a, buf_b, r_ref, d_ref, *, tt):
    sub = lax.broadcasted_iota(jnp.int32, (SUBLANES, LANES), 0)
    low = sub < ROWS_PER_EXPERT
    m_a = ((sub % 4) >= 2)[None]
    m_b = ((sub % 2) == 1)[None]
    nv = PAIR_CHUNK // 2

    def pair_partials(c, buf_ref, carry):
        t = c >> CHUNK_SHIFT
        h8 = h_ref[pl.ds(pl.multiple_of(t * SUBLANES, SUBLANES), SUBLANES), :]
        sw = pltpu.roll(h8, ROWS_PER_EXPERT, axis=0)
        ha = jnp.where(low, h8, sw)[None]
        hb = jnp.where(low, sw, h8)[None]
        lo, hi = _unpack_words(buf_ref[...])
        x = lo.reshape(nv, SUBLANES, LANES) * ha + hi.reshape(nv, SUBLANES, LANES) * hb
        x = x.reshape(nv // 2, 2, SUBLANES, LANES)
        xe, xo = x[:, 0], x[:, 1]
        a = xe + pltpu.roll(xe, 2, axis=1)
        b = xo + pltpu.roll(xo, 6, axis=1)
        m = jnp.where(m_a, a, b).reshape(nv // 4, 2, SUBLANES, LANES)
        me, mo = m[:, 0], m[:, 1]
        a2 = me + pltpu.roll(me, 1, axis=1)
        b2 = mo + pltpu.roll(mo, 7, axis=1)
        r_ref[pl.ds(pl.multiple_of(c * PAIR_CHUNK, PAIR_CHUNK), PAIR_CHUNK), :] = (
            jnp.where(m_b, a2, b2).reshape(PAIR_CHUNK, LANES))
        return carry

    _chunk_loop(tt * N_CHUNKS, idx_ref, tab_ref, buf_a, buf_b, pair_partials, 0)

    eye = (lax.broadcasted_iota(jnp.int32, (PEER_SEL, LANES), 0)
           == lax.broadcasted_iota(jnp.int32, (PEER_SEL, LANES), 1))

    def lane_sums(g, carry):
        for i in range(SUBLANES):
            t = g * SUBLANES + i
            s = jnp.sum(r_ref[pl.ds(pl.multiple_of(t * PEER_SEL, PEER_SEL), PEER_SEL), :], axis=-1, keepdims=True)
            d_ref[pl.ds(t, 1), :] = jnp.sum(jnp.where(eye, s, 0.0), axis=0, keepdims=True)
        return carry

    lax.fori_loop(0, tt // SUBLANES, lane_sums, 0)
    coef_ref[...] = gate_ref[...] * _gelu(d_ref[...])


def _peer_u(idx4, h8, gates, tab, *, tt=128):
    n_tok = gates.shape[0]
    tt = min(tt, n_tok)
    tok2 = pl.BlockSpec((tt, PEER_SEL), lambda i: (i, 0))
    return pl.pallas_call(
        functools.partial(_peer_u_kernel, tt=tt),
        grid=(n_tok // tt,),
        in_specs=[pl.BlockSpec((tt * PEER_SEL,), lambda i: (i,), memory_space=pltpu.SMEM),
                  pl.BlockSpec((tt * SUBLANES, LANES), lambda i: (i, 0)), tok2,
                  pl.BlockSpec(tab.shape, lambda i: (0, 0), pipeline_mode=pl.Buffered(1))],
        out_specs=tok2,
        out_shape=jax.ShapeDtypeStruct((n_tok, PEER_SEL), F32),
        scratch_shapes=[pltpu.VMEM((CHUNK_ROWS, LANES), jnp.int32),
                        pltpu.VMEM((CHUNK_ROWS, LANES), jnp.int32),
                        pltpu.VMEM((tt * PEER_SEL, LANES), F32),
                        pltpu.VMEM((tt, PEER_SEL), F32)],
        compiler_params=_cparams(("arbitrary",), VMEM_LIMIT),
        name="peer_u",
    )(idx4, h8, gates, tab)


def _peer_v_kernel(idx_ref, coef_ref, tab_ref, o_ref, buf_a, buf_b, *, tt):
    sub = lax.broadcasted_iota(jnp.int32, (SUBLANES, LANES), 0)
    low = sub < ROWS_PER_EXPERT
    nv = PAIR_CHUNK // 2

    def accumulate(c, buf_ref, acc):
        acc_lo, acc_hi = acc
        cs = coef_ref.at[pl.ds(pl.multiple_of(c * nv, nv), nv)]
        lo, hi = _unpack_words(buf_ref[...])
        lo = lo.reshape(nv, SUBLANES, LANES)
        hi = hi.reshape(nv, SUBLANES, LANES)
        for v in range(nv):
            c_lo, c_hi = _unpack_words(jnp.full((SUBLANES, LANES), cs[v], jnp.int32))
            cv = jnp.where(low, c_lo, c_hi)
            acc_lo = acc_lo + cv * lo[v]
            acc_hi = acc_hi + cv * hi[v]
        lo4 = acc_lo + pltpu.roll(acc_lo, ROWS_PER_EXPERT, axis=0)
        hi4 = acc_hi + pltpu.roll(acc_hi, ROWS_PER_EXPERT, axis=0)
        o_ref[c >> CHUNK_SHIFT] = jnp.where(low, lo4, hi4)
        last = (c & (N_CHUNKS - 1)) == N_CHUNKS - 1
        return jnp.where(last, 0.0, acc_lo), jnp.where(last, 0.0, acc_hi)

    z = jnp.zeros((SUBLANES, LANES), F32)
    _chunk_loop(tt * N_CHUNKS, idx_ref, tab_ref, buf_a, buf_b, accumulate, (z, z))


def _pack_coefs(coef):
    n_tok = coef.shape[0]
    pair_at = {s: j for j, s in enumerate(_CHUNK_SLOT)}
    first = np.array([ch * PAIR_CHUNK + pair_at[2 * v] for ch in range(N_CHUNKS) for v in range(PAIR_CHUNK // 2)])
    second = np.array([ch * PAIR_CHUNK + pair_at[2 * v + 1] for ch in range(N_CHUNKS) for v in range(PAIR_CHUNK // 2)])
    bits = lax.bitcast_convert_type(coef.astype(BF16), jnp.uint16).astype(jnp.uint32)
    w = bits[:, first] | (bits[:, second] << 16)
    return lax.bitcast_convert_type(w, jnp.int32).reshape(n_tok * (PEER_SEL // 2))


def _peer_v(idx4, coef_words, tab, *, tt=128):
    n_tok = idx4.shape[0] // PEER_SEL
    tt = min(tt, n_tok)
    smem = pl.BlockSpec((tt * PEER_SEL,), lambda i: (i,), memory_space=pltpu.SMEM)
    return pl.pallas_call(
        functools.partial(_peer_v_kernel, tt=tt),
        grid=(n_tok // tt,),
        in_specs=[smem, pl.BlockSpec((tt * PEER_SEL // 2,), lambda i: (i,), memory_space=pltpu.SMEM),
                  pl.BlockSpec(tab.shape, lambda i: (0, 0), pipeline_mode=pl.Buffered(1))],
        out_specs=pl.BlockSpec((tt, SUBLANES, LANES), lambda i: (i, 0, 0)),
        out_shape=jax.ShapeDtypeStruct((n_tok, SUBLANES, LANES), F32),
        scratch_shapes=[pltpu.VMEM((CHUNK_ROWS, LANES), jnp.int32),
                        pltpu.VMEM((CHUNK_ROWS, LANES), jnp.int32)],
        compiler_params=_cparams(("arbitrary",), VMEM_LIMIT),
        name="peer_v",
    )(idx4, coef_words, tab)


def _peer(h8, qp, sub_keys, expert_u, expert_v):
    n_tok = qp.shape[1]
    half = sub_keys.shape[-1]
    z = jnp.zeros_like(sub_keys[0])
    keys_pad = jnp.stack([jnp.concatenate([sub_keys[0], z], axis=-1),
                          jnp.concatenate([z, sub_keys[1]], axis=-1)]).astype(BF16)
    assert keys_pad.shape[-1] == 2 * half == qp.shape[-1]
    idx4, gates = _peer_topk(qp, keys_pad)
    idx4 = idx4.reshape(n_tok * PEER_SEL)
    coef = _peer_u(idx4, h8, gates, _pack_table(expert_u))
    out = _peer_v(idx4, _pack_coefs(coef), _pack_table(expert_v))
    return out.reshape(n_tok * SUBLANES, LANES)


def _final_kernel(x_ref, p_ref, g_ref, o_ref):
    o_ref[...] = _rms(x_ref[...] + _rows_from_chunks(p_ref, x_ref.shape[0]), g_ref[...])


def _final_norm(x, p, g, *, tr=1024):
    rows, dm = x.shape
    tr = min(tr, rows)
    blk = pl.BlockSpec((tr, dm), lambda i: (i, 0))
    return pl.pallas_call(
        _final_kernel,
        grid=(rows // tr,),
        in_specs=[blk, pl.BlockSpec((tr * SUBLANES, LANES), lambda i: (i, 0)),
                  pl.BlockSpec((1, dm), lambda i: (0, 0))],
        out_specs=blk,
        out_shape=jax.ShapeDtypeStruct((rows, dm), F32),
        compiler_params=_cparams(("parallel",), VMEM_LIMIT),
        name="final_norm",
    )(x, p, g.reshape(1, dm))


def kernel(x, w_in, w_out, rel_bias, g_attn, g_ssm, norm_mix, norm_ffn, lam_re, lam_im, log_step, b_re, b_im, c_re, c_im, d_skip, w_glu, w_query, sub_keys, expert_u, expert_v, norm_final):
    bsz, seq, dm = x.shape
    depth = w_in.shape[0]
    prev = None
    biases = [_attn_bias_tables(rel_bias, d) for _, d in DILATED_PATTERNS]
    for l in range(depth):
        x, q, k, v, u = _in_proj(x, prev, norm_mix[l], w_in[l].astype(BF16))
        attn = _attention(q, k, v, biases)
        u_tm = u.reshape(seq * bsz, SSM_WIDTH)
        a, bw, cw = _ssm_params(lam_re[l], lam_im[l], log_step[l], b_re[l], b_im[l], c_re[l], c_im[l])
        y = _ssm_scan(u_tm, a, bw, cw, bsz=bsz)
        wo = w_out[l].astype(BF16)
        z = _ssm_post(y, u_tm, d_skip[l], w_glu[l].astype(BF16), g_ssm[l], wo[ATTN_WIDTH:])
        x, h, qp = _mix_out(x, z.reshape(seq, bsz * dm), attn,
                            g_attn[l], wo[:ATTN_WIDTH], norm_ffn[l], w_query[l].astype(BF16))
        prev = _peer(h, qp, sub_keys[l], expert_u[l], expert_v[l])
    out = _final_norm(x.reshape(bsz * seq, dm), prev, norm_final)
    return out.reshape(bsz, seq, dm)
```

```python
import functools
import math

import numpy as np
import jax
import jax.numpy as jnp
from jax import lax
from jax.experimental import pallas as pl
from jax.experimental.pallas import tpu as pltpu
from jax.experimental.pallas import tpu_sc as plsc

F32 = jnp.float32
BF16 = jnp.bfloat16

EPS = 1e-6
NEG_INF = -1e30
HEAD_DIM = 64
ATTN_WIDTH = 512
SSM_WIDTH = 512
SSM_GROUP = 16
SSM_STATE = 64
DILATED_PATTERNS = ((128, 1), (512, 4), (2048, 16))
REL_BUCKETS = 32
REL_MAX_DISTANCE = 1024
PEER_HEADS = 8
PEER_KEYS = 128
PEER_TOPK = 16
PEER_SEL = PEER_HEADS * PEER_TOPK

LANES = 128
SUBLANES = 8
QBLK = 128
KWIN = 256
BAND = 64
VMEM_LIMIT = 52 * 1024 * 1024


def _cparams(sem, vmem=None):
    return pltpu.CompilerParams(dimension_semantics=sem, vmem_limit_bytes=vmem)


def _rms(x, g):
    return x * lax.rsqrt(jnp.mean(x * x, axis=-1, keepdims=True) + EPS) * g


def _gelu(x):
    return 0.5 * x * (1.0 + lax.erf(x * (1.0 / math.sqrt(2.0))))


def _rows_from_chunks(p_ref, n_rows):
    return jnp.concatenate([p_ref[pl.ds(c, n_rows, stride=SUBLANES), :] for c in range(SUBLANES)], axis=-1)


def _in_proj_kernel(*refs, has_prev):
    if has_prev:
        x_ref, p_ref, g_ref, w_ref, xo_ref, q_ref, k_ref, v_ref, u_ref = refs
        x = x_ref[0] + _rows_from_chunks(p_ref, x_ref.shape[1])
    else:
        x_ref, g_ref, w_ref, xo_ref, q_ref, k_ref, v_ref, u_ref = refs
        x = x_ref[0]
    xo_ref[0] = x
    h = _rms(x, g_ref[...]).astype(BF16)
    proj = jnp.dot(h, w_ref[...], preferred_element_type=F32)
    a = ATTN_WIDTH
    q_ref[0] = proj[:, :a] * (HEAD_DIM ** -0.5)
    k_ref[0] = proj[:, a:2 * a]
    v_ref[0] = proj[:, 2 * a:3 * a]
    u_ref[...] = proj[:, 3 * a:]


def _in_proj(x, prev, g, w_bf16, *, ts=512):
    bsz, seq, dm = x.shape
    ts = min(ts, seq)
    row = pl.BlockSpec((1, ts, dm), lambda b, i: (b, i, 0))
    qkv = pl.BlockSpec((1, ts, ATTN_WIDTH), lambda b, i: (b, i, 0))
    ns = seq // ts
    chunks = pl.BlockSpec((ts * SUBLANES, LANES), lambda b, i: (b * ns + i, 0))
    ins = [x] + ([prev] if prev is not None else []) + [g.reshape(1, dm), w_bf16]
    in_specs = [row] + ([chunks] if prev is not None else []) + [
        pl.BlockSpec((1, dm), lambda b, i: (0, 0)),
        pl.BlockSpec(w_bf16.shape, lambda b, i: (0, 0)),
    ]
    return pl.pallas_call(
        functools.partial(_in_proj_kernel, has_prev=prev is not None),
        grid=(bsz, seq // ts),
        in_specs=in_specs,
        out_specs=[row, qkv, qkv, qkv, pl.BlockSpec((ts, SSM_WIDTH), lambda b, i: (i, b))],
        out_shape=[
            jax.ShapeDtypeStruct((bsz, seq, dm), F32),
            jax.ShapeDtypeStruct((bsz, seq, ATTN_WIDTH), F32),
            jax.ShapeDtypeStruct((bsz, seq, ATTN_WIDTH), F32),
            jax.ShapeDtypeStruct((bsz, seq, ATTN_WIDTH), F32),
            jax.ShapeDtypeStruct((seq, bsz * SSM_WIDTH), F32),
        ],
        compiler_params=_cparams(("parallel", "arbitrary"), VMEM_LIMIT),
        name="in_proj",
    )(*ins)


def _t5_buckets(rel):
    half = REL_BUCKETS // 2
    max_exact = half // 2
    n = np.abs(rel)
    large = max_exact + (np.log(np.maximum(n, 1) / max_exact)
                         / np.log(REL_MAX_DISTANCE / max_exact) * (half - max_exact)).astype(np.int32)
    large = np.minimum(large, half - 1)
    return (np.where(rel > 0, half, 0) + np.where(n < max_exact, n, large)).astype(np.int32)


def _attn_bias_tables(rel_bias, dilation):
    ql = np.arange(QBLK)[:, None]
    kl = np.arange(KWIN)[None, :]
    delta = np.stack([kl + off - ql for off in (0, -BAND, -2 * BAND)])
    buckets = np.where(np.abs(delta) <= BAND, _t5_buckets(delta * dilation), -1)
    rb = rel_bias.astype(F32).T
    bk = jnp.asarray(buckets, jnp.int32)[None]
    tab = jnp.full((rb.shape[0],) + buckets.shape, NEG_INF, F32)
    for b in range(REL_BUCKETS):
        tab = jnp.where(bk == b, rb[:, b][:, None, None, None], tab)
    return tab


BLOCKS_PER_STEP = 2


def _attn_kernel(q_ref, k_ref, v_ref, *rest, seq):
    bias_refs, (o_ref, acc_ref, m_ref, z_ref) = rest[:len(DILATED_PATTERNS)], rest[len(DILATED_PATTERNS):]
    lane = lax.broadcasted_iota(jnp.int32, (QBLK, LANES), 1)
    is_h0 = lane < HEAD_DIM
    dn = (((1,), (1,)), ((), ()))
    nsteps = seq // QBLK

    def rows(start, size, d):
        return pl.ds(start, size) if d == 1 else pl.ds(start, size, stride=d)

    def block(n, d, bias_ref, first, last):
        length = seq // d
        nblk = length // QBLK
        r, i = n >> (nblk.bit_length() - 1), n & (nblk - 1)
        s = i * QBLK
        ks = jnp.clip(s - BAND, 0, length - KWIN)
        var = jnp.where(i == 0, 0, jnp.where(i == nblk - 1, 2, 1))
        q_rows = rows(r + d * s, QBLK, d)
        k_rows = rows(r + d * ks, KWIN, d)
        qb = q_ref[q_rows, :].astype(BF16)
        kb = k_ref[k_rows, :].astype(BF16)
        vb = v_ref[k_rows, :].astype(BF16)
        outs, ms, zs = [], [], []
        for h in range(2):
            keep = is_h0 if h == 0 else jnp.logical_not(is_h0)
            qh = jnp.where(keep, qb, jnp.zeros_like(qb))
            logits = lax.dot_general(qh, kb, dn, preferred_element_type=F32) + bias_ref[h, var]
            m = jnp.max(logits, axis=-1, keepdims=True)
            p = jnp.exp(logits - m)
            outs.append(jnp.dot(p.astype(BF16), vb, preferred_element_type=F32))
            ms.append(jnp.broadcast_to(m, (QBLK, LANES)))
            zs.append(jnp.broadcast_to(jnp.sum(p, axis=-1, keepdims=True), (QBLK, LANES)))
        o = jnp.where(is_h0, outs[0], outs[1])
        m = jnp.where(is_h0, ms[0], ms[1])
        z = jnp.where(is_h0, zs[0], zs[1])
        if not first:
            m_old = m_ref[q_rows, :]
            m_new = jnp.maximum(m_old, m)
            a, b = jnp.exp(m_old - m_new), jnp.exp(m - m_new)
            o = acc_ref[q_rows, :] * a + o * b
            z = z_ref[q_rows, :] * a + z * b
            m = m_new
        if last:
            o_ref[q_rows, :] = o / z
        else:
            acc_ref[q_rows, :] = o
            m_ref[q_rows, :] = m
            z_ref[q_rows, :] = z

    for p, ((_, d), bias_ref) in enumerate(zip(DILATED_PATTERNS, bias_refs)):
        def step(g, carry, d=d, bias_ref=bias_ref, p=p):
            for j in range(BLOCKS_PER_STEP):
                block(g * BLOCKS_PER_STEP + j, d, bias_ref, p == 0, p == len(DILATED_PATTERNS) - 1)
            return carry
        lax.fori_loop(0, nsteps // BLOCKS_PER_STEP, step, 0)


def _attention(q, k, v, biases):
    bsz, seq, width = q.shape
    for _, d in DILATED_PATTERNS:
        length = seq // d
        assert length >= KWIN and length % QBLK == 0 and (length // QBLK) & (length // QBLK - 1) == 0
    assert (seq // QBLK) % BLOCKS_PER_STEP == 0
    blk = pl.BlockSpec((None, seq, LANES), lambda b, c: (b, 0, c))
    bias_spec = pl.BlockSpec((2, 3, QBLK, KWIN), lambda b, c: (c, 0, 0, 0))
    return pl.pallas_call(
        functools.partial(_attn_kernel, seq=seq),
        grid=(bsz, width // LANES),
        in_specs=[blk, blk, blk] + [bias_spec] * len(biases),
        out_specs=blk,
        out_shape=jax.ShapeDtypeStruct((bsz, seq, width), F32),
        scratch_shapes=[pltpu.VMEM((seq, LANES), F32)] * 3,
        compiler_params=_cparams(("parallel", "arbitrary"), VMEM_LIMIT),
        name="attention",
    )(q, k, v, *biases)


SSM_LANE_GROUPS = SSM_WIDTH // LANES
SSM_GB_STATES = (LANES // SSM_GROUP) * SSM_STATE


def _ssm_params(lam_re, lam_im, log_step, b_re, b_im, c_re, c_im):
    f = lambda t: t.astype(F32)
    lr, li = f(lam_re), f(lam_im)
    step = jnp.exp(f(log_step))[..., None]
    mag = jnp.exp(lr * step)
    ar, ai = mag * jnp.cos(li * step), mag * jnp.sin(li * step)
    nr, ni = ar - 1.0, ai
    den = lr * lr + li * li
    cr, ci = (nr * lr + ni * li) / den, (ni * lr - nr * li) / den
    br, bi = f(b_re), f(b_im)
    bbr = cr[..., None] * br - ci[..., None] * bi
    bbi = cr[..., None] * bi + ci[..., None] * br
    gpb = LANES // SSM_GROUP
    eye = jnp.eye(gpb, dtype=F32)

    def in_map(t):
        t = t.reshape(2, SSM_LANE_GROUPS, gpb, SSM_STATE, SSM_GROUP)
        return jnp.einsum('dbgpc,gh->dbgchp', t, eye).reshape(2, SSM_LANE_GROUPS, LANES, SSM_GB_STATES)

    def out_map(t):
        t = t.reshape(2, SSM_LANE_GROUPS, gpb, SSM_GROUP, SSM_STATE)
        return jnp.einsum('dbgcp,gh->dbgphc', t, eye).reshape(2, SSM_LANE_GROUPS, SSM_GB_STATES, LANES)

    bw = jnp.concatenate([in_map(bbr), in_map(bbi)], axis=-1).astype(BF16)
    cw = jnp.concatenate([out_map(f(c_re)), -out_map(f(c_im))], axis=-2).astype(BF16)
    a = jnp.stack([ar.reshape(2, SSM_LANE_GROUPS, SSM_GB_STATES),
                   ai.reshape(2, SSM_LANE_GROUPS, SSM_GB_STATES)], axis=2)
    return a, bw, cw


def _ssm_kernel(u_ref, a_ref, bw_ref, cw_ref, y_ref, st_ref, bu_ref, *, ts, bsz):
    d = pl.program_id(0)
    ns = SSM_GB_STATES

    @pl.when(pl.program_id(1) == 0)
    def _():
        st_ref[...] = jnp.zeros_like(st_ref)

    for gb in range(SSM_LANE_GROUPS):
        ub = u_ref[:, gb * LANES:(gb + 1) * LANES].astype(BF16)
        bu_ref[...] = jnp.dot(ub, bw_ref[0, gb], preferred_element_type=F32)
        ar = jnp.broadcast_to(a_ref[0, gb, 0:1, :], (bsz, ns))
        ai = jnp.broadcast_to(a_ref[0, gb, 1:2, :], (bsz, ns))

        def step(j, carry, ar=ar, ai=ai):
            xr, xi = carry
            tl = jnp.where(d == 0, j, ts - 1 - j)
            r = pl.multiple_of(tl * bsz, bsz)
            nr = ar * xr - ai * xi + bu_ref[pl.ds(r, bsz), :ns]
            ni = ar * xi + ai * xr + bu_ref[pl.ds(r, bsz), ns:]
            bu_ref[pl.ds(r, bsz), :ns] = nr
            bu_ref[pl.ds(r, bsz), ns:] = ni
            return nr, ni

        xr, xi = lax.fori_loop(0, ts, step, (st_ref[gb, :, :ns], st_ref[gb, :, ns:]))
        st_ref[gb, :, :ns] = xr
        st_ref[gb, :, ns:] = xi
        y_ref[0, :, gb * LANES:(gb + 1) * LANES] = jnp.dot(
            bu_ref[...].astype(BF16), cw_ref[0, gb], preferred_element_type=F32)


def _ssm_scan(u_tm, a, bw, cw, *, bsz, ts=64):
    rows, width = u_tm.shape
    seq = rows // bsz
    ts = min(ts, seq)
    nt = seq // ts
    tblk = lambda d, i: jnp.where(d == 0, i, nt - 1 - i)
    return pl.pallas_call(
        functools.partial(_ssm_kernel, ts=ts, bsz=bsz),
        grid=(2, nt),
        in_specs=[
            pl.BlockSpec((ts * bsz, width), lambda d, i: (tblk(d, i), 0)),
            pl.BlockSpec((1,) + a.shape[1:], lambda d, i: (d, 0, 0, 0)),
            pl.BlockSpec((1,) + bw.shape[1:], lambda d, i: (d, 0, 0, 0)),
            pl.BlockSpec((1,) + cw.shape[1:], lambda d, i: (d, 0, 0, 0)),
        ],
        out_specs=pl.BlockSpec((1, ts * bsz, width), lambda d, i: (d, tblk(d, i), 0)),
        out_shape=jax.ShapeDtypeStruct((2, rows, width), F32),
        scratch_shapes=[pltpu.VMEM((SSM_LANE_GROUPS, bsz, 2 * SSM_GB_STATES), F32),
                        pltpu.VMEM((ts * bsz, 2 * SSM_GB_STATES), F32)],
        compiler_params=_cparams(("arbitrary", "arbitrary"), VMEM_LIMIT),
        name="ssm_scan",
    )(u_tm, a, bw, cw)


def _ssm_post_kernel(y_ref, u_ref, d_ref, wg_ref, g_ref, wo_ref, z_ref):
    y = _gelu(y_ref[0] + y_ref[1] + d_ref[...] * u_ref[...]).astype(BF16)
    ab = jnp.dot(y, wg_ref[...], preferred_element_type=F32)
    ssm = ab[:, :SSM_WIDTH] * jax.nn.sigmoid(ab[:, SSM_WIDTH:])
    n = _rms(ssm, g_ref[...]).astype(BF16)
    z_ref[...] = jnp.dot(n, wo_ref[...], preferred_element_type=F32)


def _ssm_post(y, u_tm, d_skip, w_glu_bf16, g_ssm, w_out_ssm_bf16, *, tr=512):
    rows, width = u_tm.shape
    tr = min(tr, rows)
    dm = w_out_ssm_bf16.shape[1]
    full = lambda a: pl.BlockSpec(a.shape, lambda i: (0,) * a.ndim)
    d2, g2 = d_skip.reshape(1, width), g_ssm.reshape(1, width)
    return pl.pallas_call(
        _ssm_post_kernel,
        grid=(rows // tr,),
        in_specs=[pl.BlockSpec((2, tr, width), lambda i: (0, i, 0)),
                  pl.BlockSpec((tr, width), lambda i: (i, 0)),
                  full(d2), full(w_glu_bf16), full(g2), full(w_out_ssm_bf16)],
        out_specs=pl.BlockSpec((tr, dm), lambda i: (i, 0)),
        out_shape=jax.ShapeDtypeStruct((rows, dm), F32),
        compiler_params=_cparams(("parallel",), VMEM_LIMIT),
        name="ssm_post",
    )(y, u_tm, d2, w_glu_bf16, g2, w_out_ssm_bf16)


def _mix_out_kernel(x_ref, z_ref, a_ref, ga_ref, wo_ref, gf_ref, wq_ref, xn_ref, h_ref, q_ref):
    n = _rms(a_ref[0], ga_ref[...]).astype(BF16)
    xn = x_ref[0] + z_ref[...] + jnp.dot(n, wo_ref[...], preferred_element_type=F32)
    xn_ref[0] = xn
    h = _rms(xn, gf_ref[...])
    for c in range(SUBLANES):
        h_ref[pl.ds(c, h.shape[0], stride=SUBLANES), :] = h[:, c * LANES:(c + 1) * LANES]
    qp = jnp.dot(h.astype(BF16), wq_ref[...], preferred_element_type=F32)
    for hd in range(PEER_HEADS):
        q_ref[hd] = qp[:, hd * LANES:(hd + 1) * LANES]


def _mix_out(x, z_tm, attn, g_attn, w_out_attn_bf16, norm_ffn, w_query, *, ts=256):
    bsz, seq, dm = x.shape
    ts = min(ts, seq)
    ns = seq // ts
    row = pl.BlockSpec((1, ts, dm), lambda b, i: (b, i, 0))
    half = pl.BlockSpec((1, ts, ATTN_WIDTH), lambda b, i: (b, i, 0))
    full = lambda a: pl.BlockSpec(a.shape, lambda b, i: (0,) * a.ndim)
    ga, gf = g_attn.reshape(1, ATTN_WIDTH), norm_ffn.reshape(1, dm)
    qdim = w_query.shape[1] // PEER_HEADS
    return pl.pallas_call(
        _mix_out_kernel,
        grid=(bsz, ns),
        in_specs=[row, pl.BlockSpec((ts, dm), lambda b, i: (i, b)), half,
                  full(ga), full(w_out_attn_bf16), full(gf), full(w_query)],
        out_specs=[row, pl.BlockSpec((ts * SUBLANES, LANES), lambda b, i: (b * ns + i, 0)),
                   pl.BlockSpec((PEER_HEADS, ts, qdim), lambda b, i: (0, b * ns + i, 0))],
        out_shape=[jax.ShapeDtypeStruct((bsz, seq, dm), F32),
                   jax.ShapeDtypeStruct((bsz * seq * SUBLANES, LANES), F32),

                   jax.ShapeDtypeStruct((PEER_HEADS, bsz * seq, qdim), F32)],
        compiler_params=_cparams(("parallel", "arbitrary"), VMEM_LIMIT),
        name="mix_out",
    )(x, z_tm, attn, ga, w_out_attn_bf16, gf, w_query)


TOPK_TOKENS = SUBLANES * LANES
KEY_PITCH = PEER_KEYS + 4
_CANDIDATES = tuple((a, b) for a in range(PEER_TOPK) for b in range(PEER_TOPK) if (a + 1) * (b + 1) <= PEER_TOPK)


def _tree(op, xs):
    xs = list(xs)
    while len(xs) > 1:
        xs = [op(xs[i], xs[i + 1]) if i + 1 < len(xs) else xs[i] for i in range(0, len(xs), 2)]
    return xs[0]


def _extract16(problems):
    big = jnp.int32(2 ** 30)
    ninf = jnp.float32(-jnp.inf)

    def step(r, ms):
        nxt = []
        for p, m in zip(problems, ms):
            s_ref, order = p["s"], p["order"]
            n = len(order)
            am = _tree(jnp.minimum, [jnp.where(s_ref[k] == m, order[k], big) for k in range(n)])
            news, pays = [], []
            for k in range(n):
                hit = am == order[k]
                nk = jnp.where(hit, ninf, s_ref[k])
                s_ref[k] = nk
                news.append(nk)
                if p.get("pay") is not None:
                    pays.append(jnp.where(hit, p["pay"][k], -1))
            p["vals"][r] = m
            p["picks"][r] = _tree(jnp.maximum, pays) if pays else am
            nxt.append(_tree(jnp.maximum, news))
        return tuple(nxt)

    init = tuple(_tree(jnp.maximum, [p["s"][k] for k in range(len(p["order"]))]) for p in problems)
    lax.fori_loop(0, PEER_TOPK, step, init)


def _peer_topk_kernel(q_ref, k_ref, idx_ref, gate_ref,
                      slab_ref, s1_ref, s2_ref, t1_ref, i1_ref, t2_ref, i2_ref, cand_ref, pay_ref, ts_ref, ex_ref):
    dn = (((1,), (1,)), ((), ()))
    keys = tuple(range(PEER_KEYS))

    def head(h, carry):
        for w, s_ref in ((0, s1_ref), (1, s2_ref)):
            for j in range(SUBLANES):
                slab_ref[j * KEY_PITCH:j * KEY_PITCH + PEER_KEYS, :] = lax.dot_general(
                    k_ref[w, h], q_ref[h, j * LANES:(j + 1) * LANES, :].astype(BF16), dn,
                    preferred_element_type=F32)
            for k in range(PEER_KEYS):
                s_ref[k] = slab_ref[pl.ds(k, SUBLANES, stride=KEY_PITCH), :]
        _extract16([dict(s=s1_ref, order=keys, vals=t1_ref, picks=i1_ref),
                    dict(s=s2_ref, order=keys, vals=t2_ref, picks=i2_ref)])
        for c, (a, b) in enumerate(_CANDIDATES):
            cand_ref[c] = t1_ref[a] + t2_ref[b]
            pay_ref[c] = i1_ref[a] * PEER_KEYS + i2_ref[b]
        _extract16([dict(s=cand_ref, order=tuple(a * PEER_TOPK + b for a, b in _CANDIDATES), pay=pay_ref,
                         vals=ts_ref, picks=ex_ref)])
        top_s = ts_ref[...]
        e = jnp.exp(top_s - jnp.max(top_s, axis=0, keepdims=True))
        gate_ref[0, h] = e / jnp.sum(e, axis=0, keepdims=True)
        idx_ref[0, h] = ex_ref[...] * ROWS_PER_EXPERT
        return carry

    lax.fori_loop(0, PEER_HEADS, head, 0)


def _peer_topk(qp, keys_pad):
    n_tok = qp.shape[1]
    tt = TOPK_TOKENS
    assert n_tok % tt == 0
    shp = (n_tok // tt, PEER_HEADS, PEER_TOPK, SUBLANES, LANES)
    out = pl.BlockSpec((1,) + shp[1:], lambda i: (i, 0, 0, 0, 0))
    vregs = lambda n, dt: pltpu.VMEM((n, SUBLANES, LANES), dt)
    idx, gate = pl.pallas_call(
        _peer_topk_kernel,
        grid=(n_tok // tt,),
        in_specs=[pl.BlockSpec((PEER_HEADS, tt, qp.shape[2]), lambda i: (0, i, 0)),
                  pl.BlockSpec(keys_pad.shape, lambda i: (0, 0, 0, 0))],
        out_specs=[out, out],
        out_shape=[jax.ShapeDtypeStruct(shp, jnp.int32), jax.ShapeDtypeStruct(shp, F32)],
        scratch_shapes=[pltpu.VMEM((SUBLANES * KEY_PITCH, LANES), F32),
                        vregs(PEER_KEYS, F32), vregs(PEER_KEYS, F32),
                        vregs(PEER_TOPK, F32), vregs(PEER_TOPK, jnp.int32),
                        vregs(PEER_TOPK, F32), vregs(PEER_TOPK, jnp.int32),
                        vregs(len(_CANDIDATES), F32), vregs(len(_CANDIDATES), jnp.int32),
                        vregs(PEER_TOPK, F32), vregs(PEER_TOPK, jnp.int32)],
        compiler_params=_cparams(("parallel",), VMEM_LIMIT),
        name="peer_topk",
    )(qp, keys_pad)
    to_tok = lambda a: jnp.transpose(a, (0, 3, 4, 1, 2)).reshape(n_tok, PEER_SEL)
    return to_tok(idx), to_tok(gate)


ROWS_PER_EXPERT = 4
PAIR_CHUNK = 32
CHUNK_ROWS = PAIR_CHUNK * ROWS_PER_EXPERT
SMEM_GROUP = 8
N_CHUNKS = PEER_SEL // PAIR_CHUNK
CHUNK_SHIFT = N_CHUNKS.bit_length() - 1
assert 1 << CHUNK_SHIFT == N_CHUNKS
_GROUP_SLOT = tuple(
    2 * (2 * (0 if (j % 2) else 1) + (1 if (j // 2) in (0, 2) else 0)) + (0 if (j // 2) < 2 else 1)
    for j in range(8))
_CHUNK_SLOT = tuple(8 * (j // 8) + _GROUP_SLOT[j % 8] for j in range(PAIR_CHUNK))


def _pack_table(tbl):
    e, dm = tbl.shape
    tb = tbl.astype(BF16)
    lo = lax.bitcast_convert_type(tb[:, :dm // 2], jnp.uint16).astype(jnp.uint32)
    hi = lax.bitcast_convert_type(tb[:, dm // 2:], jnp.uint16).astype(jnp.uint32)
    w = lax.bitcast_convert_type(lo | (hi << 16), jnp.int32)
    return w.reshape(e * ROWS_PER_EXPERT, LANES)


def _unpack_words(w):
    lo = lax.bitcast_convert_type(w << 16, F32)
    hi = lax.bitcast_convert_type(w & jnp.int32(-65536), F32)
    return lo, hi


def _gather_chunk(idx_ref, tab_ref, buf_ref, c):
    for g in range(PAIR_CHUNK // SMEM_GROUP):
        ids = idx_ref.at[pl.ds(pl.multiple_of(c * PAIR_CHUNK + g * SMEM_GROUP, SMEM_GROUP), SMEM_GROUP)]
        for i in range(SMEM_GROUP):
            e4 = pl.multiple_of(ids[i], ROWS_PER_EXPERT)
            s = _CHUNK_SLOT[g * SMEM_GROUP + i] * ROWS_PER_EXPERT
            buf_ref[s:s + ROWS_PER_EXPERT, :] = tab_ref[pl.ds(e4, ROWS_PER_EXPERT), :]


def _chunk_loop(n_chunks, idx_ref, tab_ref, buf_a, buf_b, consume, init):
    last = n_chunks - 1
    _gather_chunk(idx_ref, tab_ref, buf_a, 0)

    def body(i, carry):
        c = 2 * i
        carry = consume(c, buf_a, carry)
        _gather_chunk(idx_ref, tab_ref, buf_b, c + 1)
        carry = consume(c + 1, buf_b, carry)
        _gather_chunk(idx_ref, tab_ref, buf_a, jnp.minimum(c + 2, last))
        return carry

    return lax.fori_loop(0, n_chunks // 2, body, init)


def _peer_u_kernel(idx_ref, h_ref, gate_ref, tab_ref, coef_ref, buf_a, buf_b, r_ref, d_ref, *, tt):
    sub = lax.broadcasted_iota(jnp.int32, (SUBLANES, LANES), 0)
    low = sub < ROWS_PER_EXPERT
    m_a = ((sub % 4) >= 2)[None]
    m_b = ((sub % 2) == 1)[None]
    nv = PAIR_CHUNK // 2

    def pair_partials(c, buf_ref, carry):
        t = c >> CHUNK_SHIFT
        h8 = h_ref[pl.ds(pl.multiple_of(t * SUBLANES, SUBLANES), SUBLANES), :]
        sw = pltpu.roll(h8, ROWS_PER_EXPERT, axis=0)
        ha = jnp.where(low, h8, sw)[None]
        hb = jnp.where(low, sw, h8)[None]
        lo, hi = _unpack_words(buf_ref[...])
        x = lo.reshape(nv, SUBLANES, LANES) * ha + hi.reshape(nv, SUBLANES, LANES) * hb
        x = x.reshape(nv // 2, 2, SUBLANES, LANES)
        xe, xo = x[:, 0], x[:, 1]
        a = xe + pltpu.roll(xe, 2, axis=1)
        b = xo + pltpu.roll(xo, 6, axis=1)
        m = jnp.where(m_a, a, b).reshape(nv // 4, 2, SUBLANES, LANES)
        me, mo = m[:, 0], m[:, 1]
        a2 = me + pltpu.roll(me, 1, axis=1)
        b2 = mo + pltpu.roll(mo, 7, axis=1)
        r_ref[pl.ds(pl.multiple_of(c * PAIR_CHUNK, PAIR_CHUNK), PAIR_CHUNK), :] = (
            jnp.where(m_b, a2, b2).reshape(PAIR_CHUNK, LANES))
        return carry

    _chunk_loop(tt * N_CHUNKS, idx_ref, tab_ref, buf_a, buf_b, pair_partials, 0)

    eye = (lax.broadcasted_iota(jnp.int32, (PEER_SEL, LANES), 0)
           == lax.broadcasted_iota(jnp.int32, (PEER_SEL, LANES), 1))

    def lane_sums(g, carry):
        for i in range(SUBLANES):
            t = g * SUBLANES + i
            s = jnp.sum(r_ref[pl.ds(pl.multiple_of(t * PEER_SEL, PEER_SEL), PEER_SEL), :], axis=-1, keepdims=True)
            d_ref[pl.ds(t, 1), :] = jnp.sum(jnp.where(eye, s, 0.0), axis=0, keepdims=True)
        return carry

    lax.fori_loop(0, tt // SUBLANES, lane_sums, 0)
    coef_ref[...] = gate_ref[...] * _gelu(d_ref[...])


def _peer_u(idx4, h8, gates, tab, *, tt=128):
    n_tok = gates.shape[0]
    tt = min(tt, n_tok)
    tok2 = pl.BlockSpec((tt, PEER_SEL), lambda i: (i, 0))
    return pl.pallas_call(
        functools.partial(_peer_u_kernel, tt=tt),
        grid=(n_tok // tt,),
        in_specs=[pl.BlockSpec((tt * PEER_SEL,), lambda i: (i,), memory_space=pltpu.SMEM),
                  pl.BlockSpec((tt * SUBLANES, LANES), lambda i: (i, 0)), tok2,
                  pl.BlockSpec(tab.shape, lambda i: (0, 0), pipeline_mode=pl.Buffered(1))],
        out_specs=tok2,
        out_shape=jax.ShapeDtypeStruct((n_tok, PEER_SEL), F32),
        scratch_shapes=[pltpu.VMEM((CHUNK_ROWS, LANES), jnp.int32),
                        pltpu.VMEM((CHUNK_ROWS, LANES), jnp.int32),
                        pltpu.VMEM((tt * PEER_SEL, LANES), F32),
                        pltpu.VMEM((tt, PEER_SEL), F32)],
        compiler_params=_cparams(("arbitrary",), VMEM_LIMIT),
        name="peer_u",
    )(idx4, h8, gates, tab)


def _peer_v_kernel(idx_ref, coef_ref, tab_ref, o_ref, buf_a, buf_b, *, tt):
    sub = lax.broadcasted_iota(jnp.int32, (SUBLANES, LANES), 0)
    low = sub < ROWS_PER_EXPERT
    nv = PAIR_CHUNK // 2

    def accumulate(c, buf_ref, acc):
        acc_lo, acc_hi = acc
        lo, hi = _unpack_words(buf_ref[...])
        lo = lo.reshape(nv, SUBLANES, LANES)
        hi = hi.reshape(nv, SUBLANES, LANES)
        for v in range(nv):
            if v % SMEM_GROUP == 0:
                cs = coef_ref.at[pl.ds(pl.multiple_of(c * nv + v, SMEM_GROUP), SMEM_GROUP)]
            c_lo, c_hi = _unpack_words(jnp.full((SUBLANES, LANES), cs[v % SMEM_GROUP], jnp.int32))
            cv = jnp.where(low, c_lo, c_hi)
            acc_lo = acc_lo + cv * lo[v]
            acc_hi = acc_hi + cv * hi[v]
        lo4 = acc_lo + pltpu.roll(acc_lo, ROWS_PER_EXPERT, axis=0)
        hi4 = acc_hi + pltpu.roll(acc_hi, ROWS_PER_EXPERT, axis=0)
        o_ref[c >> CHUNK_SHIFT] = jnp.where(low, lo4, hi4)
        last = (c & (N_CHUNKS - 1)) == N_CHUNKS - 1
        return jnp.where(last, 0.0, acc_lo), jnp.where(last, 0.0, acc_hi)

    z = jnp.zeros((SUBLANES, LANES), F32)
    _chunk_loop(tt * N_CHUNKS, idx_ref, tab_ref, buf_a, buf_b, accumulate, (z, z))


def _pack_coefs(coef):
    n_tok = coef.shape[0]
    pair_at = {s: j for j, s in enumerate(_CHUNK_SLOT)}
    first = np.array([ch * PAIR_CHUNK + pair_at[2 * v] for ch in range(N_CHUNKS) for v in range(PAIR_CHUNK // 2)])
    second = np.array([ch * PAIR_CHUNK + pair_at[2 * v + 1] for ch in range(N_CHUNKS) for v in range(PAIR_CHUNK // 2)])
    bits = lax.bitcast_convert_type(coef.astype(BF16), jnp.uint16).astype(jnp.uint32)
    w = bits[:, first] | (bits[:, second] << 16)
    return lax.bitcast_convert_type(w, jnp.int32).reshape(n_tok * (PEER_SEL // 2))


def _peer_v(idx4, coef_words, tab, *, tt=128):
    n_tok = idx4.shape[0] // PEER_SEL
    tt = min(tt, n_tok)
    smem = pl.BlockSpec((tt * PEER_SEL,), lambda i: (i,), memory_space=pltpu.SMEM)
    return pl.pallas_call(
        functools.partial(_peer_v_kernel, tt=tt),
        grid=(n_tok // tt,),
        in_specs=[smem, pl.BlockSpec((tt * PEER_SEL // 2,), lambda i: (i,), memory_space=pltpu.SMEM),
                  pl.BlockSpec(tab.shape, lambda i: (0, 0), pipeline_mode=pl.Buffered(1))],
        out_specs=pl.BlockSpec((tt, SUBLANES, LANES), lambda i: (i, 0, 0)),
        out_shape=jax.ShapeDtypeStruct((n_tok, SUBLANES, LANES), F32),
        scratch_shapes=[pltpu.VMEM((CHUNK_ROWS, LANES), jnp.int32),
                        pltpu.VMEM((CHUNK_ROWS, LANES), jnp.int32)],
        compiler_params=_cparams(("arbitrary",), VMEM_LIMIT),
        name="peer_v",
    )(idx4, coef_words, tab)


SC_LANES = 16
SC_TOKENS_PER_STEP = 8
SC_WORD_BLOCK = 128


def _peer_v_sc(idx4, coef, tab):
    n_tok = idx4.shape[0]
    words = ROWS_PER_EXPERT * LANES
    tab2 = tab.reshape(tab.shape[0] // ROWS_PER_EXPERT, words)
    experts = idx4 // ROWS_PER_EXPERT
    mesh = plsc.VectorSubcoreMesh(core_axis_name="core", subcore_axis_name="subcore")
    nj = SC_WORD_BLOCK // SC_LANES

    @functools.partial(
        pl.kernel,
        out_type=jax.ShapeDtypeStruct((n_tok, 2 * words), F32),
        mesh=mesh,
        scratch_types=[pltpu.VMEM((PEER_SEL, words), jnp.int32)],
        compiler_params=pltpu.CompilerParams(needs_layout_passes=False),
        name="peer_v_sc",
    )
    def run(tab_hbm, idx_hbm, coef_hbm, out_hbm, rows_vmem):
        def body(i_vmem, c_vmem, o_vmem):
            @pl.loop(0, SC_TOKENS_PER_STEP)
            def _(tok):
                pltpu.sync_copy(tab_hbm.at[i_vmem.at[tok]], rows_vmem)
                for j in range(2 * words // SC_LANES):
                    o_vmem[tok, pl.ds(j * SC_LANES, SC_LANES)] = jnp.zeros((SC_LANES,), F32)

                @pl.loop(0, PEER_SEL)
                def _(k):
                    ck = plsc.load_gather(c_vmem, [jnp.full((SC_LANES,), tok, jnp.int32),
                                                   jnp.full((SC_LANES,), k, jnp.int32)])
                    for j in range(words // SC_LANES):
                        lo, hi = _unpack_words(rows_vmem[k, pl.ds(j * SC_LANES, SC_LANES)])
                        plsc.addupdate(o_vmem.at[tok, pl.ds(j * SC_LANES, SC_LANES)], ck * lo)
                        plsc.addupdate(o_vmem.at[tok, pl.ds(words + j * SC_LANES, SC_LANES)], ck * hi)

        pltpu.emit_pipeline(
            body,
            grid=(n_tok // SC_TOKENS_PER_STEP,),
            in_specs=[pl.BlockSpec((SC_TOKENS_PER_STEP, PEER_SEL), lambda i: (i, 0)),
                      pl.BlockSpec((SC_TOKENS_PER_STEP, PEER_SEL), lambda i: (i, 0))],
            out_specs=[pl.BlockSpec((SC_TOKENS_PER_STEP, 2 * words), lambda i: (i, 0))],
            core_axis_name=("core", "subcore"),
            dimension_semantics=(pltpu.PARALLEL,),
        )(idx_hbm, coef_hbm, out_hbm)

    return run(tab2, experts, coef)


SC_TOKEN_SHARE = 8


def _peer(h8, qp, sub_keys, expert_u, expert_v):
    n_tok = qp.shape[1]
    half = sub_keys.shape[-1]
    z = jnp.zeros_like(sub_keys[0])
    keys_pad = jnp.stack([jnp.concatenate([sub_keys[0], z], axis=-1),
                          jnp.concatenate([z, sub_keys[1]], axis=-1)]).astype(BF16)
    assert keys_pad.shape[-1] == 2 * half == qp.shape[-1]
    idx4, gates = _peer_topk(qp, keys_pad)
    coef = _peer_u(idx4.reshape(n_tok * PEER_SEL), h8, gates, _pack_table(expert_u))
    tab_v = _pack_table(expert_v)
    n_sc = n_tok // SC_TOKEN_SHARE
    n_tc = n_tok - n_sc
    out_tc = _peer_v(idx4[:n_tc].reshape(n_tc * PEER_SEL), _pack_coefs(coef[:n_tc]), tab_v)
    out_sc = _peer_v_sc(idx4[n_tc:], coef[n_tc:], tab_v)
    return jnp.concatenate([out_tc.reshape(n_tc * SUBLANES, LANES), out_sc.reshape(n_sc * SUBLANES, LANES)], axis=0)


def _final_kernel(x_ref, p_ref, g_ref, o_ref):
    o_ref[...] = _rms(x_ref[...] + _rows_from_chunks(p_ref, x_ref.shape[0]), g_ref[...])


def _final_norm(x, p, g, *, tr=1024):
    rows, dm = x.shape
    tr = min(tr, rows)
    blk = pl.BlockSpec((tr, dm), lambda i: (i, 0))
    return pl.pallas_call(
        _final_kernel,
        grid=(rows // tr,),
        in_specs=[blk, pl.BlockSpec((tr * SUBLANES, LANES), lambda i: (i, 0)),
                  pl.BlockSpec((1, dm), lambda i: (0, 0))],
        out_specs=blk,
        out_shape=jax.ShapeDtypeStruct((rows, dm), F32),
        compiler_params=_cparams(("parallel",), VMEM_LIMIT),
        name="final_norm",
    )(x, p, g.reshape(1, dm))


def kernel(x, w_in, w_out, rel_bias, g_attn, g_ssm, norm_mix, norm_ffn, lam_re, lam_im, log_step, b_re, b_im, c_re, c_im, d_skip, w_glu, w_query, sub_keys, expert_u, expert_v, norm_final):
    bsz, seq, dm = x.shape
    depth = w_in.shape[0]
    prev = None
    biases = [_attn_bias_tables(rel_bias, d) for _, d in DILATED_PATTERNS]
    for l in range(depth):
        x, q, k, v, u = _in_proj(x, prev, norm_mix[l], w_in[l].astype(BF16))
        attn = _attention(q, k, v, biases)
        u_tm = u.reshape(seq * bsz, SSM_WIDTH)
        a, bw, cw = _ssm_params(lam_re[l], lam_im[l], log_step[l], b_re[l], b_im[l], c_re[l], c_im[l])
        y = _ssm_scan(u_tm, a, bw, cw, bsz=bsz)
        wo = w_out[l].astype(BF16)
        z = _ssm_post(y, u_tm, d_skip[l], w_glu[l].astype(BF16), g_ssm[l], wo[ATTN_WIDTH:])
        x, h, qp = _mix_out(x, z.reshape(seq, bsz * dm), attn,
                            g_attn[l], wo[:ATTN_WIDTH], norm_ffn[l], w_query[l].astype(BF16))
        prev = _peer(h, qp, sub_keys[l], expert_u[l], expert_v[l])
    out = _final_norm(x.reshape(bsz * seq, dm), prev, norm_final)
    return out.reshape(bsz, seq, dm)
```

```python
import functools
import math

import numpy as np
import jax
import jax.numpy as jnp
from jax import lax
from jax.experimental import pallas as pl
from jax.experimental.pallas import tpu as pltpu

F32 = jnp.float32
BF16 = jnp.bfloat16

EPS = 1e-6
NEG_INF = -1e30
HEAD_DIM = 64
ATTN_WIDTH = 512
SSM_WIDTH = 512
SSM_GROUP = 16
SSM_STATE = 64
DILATED_PATTERNS = ((128, 1), (512, 4), (2048, 16))
REL_BUCKETS = 32
REL_MAX_DISTANCE = 1024
PEER_HEADS = 8
PEER_KEYS = 128
PEER_TOPK = 16
PEER_SEL = PEER_HEADS * PEER_TOPK

LANES = 128
SUBLANES = 8
QBLK = 128
KWIN = 256
BAND = 64
VMEM_LIMIT = 52 * 1024 * 1024


def _cparams(sem, vmem=None):
    return pltpu.CompilerParams(dimension_semantics=sem, vmem_limit_bytes=vmem)


def _rms(x, g):
    return x * lax.rsqrt(jnp.mean(x * x, axis=-1, keepdims=True) + EPS) * g


def _gelu(x):
    return 0.5 * x * (1.0 + lax.erf(x * (1.0 / math.sqrt(2.0))))


def _rows_from_chunks(p_ref, n_rows):
    return jnp.concatenate([p_ref[pl.ds(c, n_rows, stride=SUBLANES), :] for c in range(SUBLANES)], axis=-1)


def _in_proj_kernel(*refs, has_prev):
    if has_prev:
        x_ref, p_ref, g_ref, w_ref, xo_ref, q_ref, k_ref, v_ref, u_ref = refs
        x = x_ref[0] + _rows_from_chunks(p_ref, x_ref.shape[1])
    else:
        x_ref, g_ref, w_ref, xo_ref, q_ref, k_ref, v_ref, u_ref = refs
        x = x_ref[0]
    xo_ref[0] = x
    h = _rms(x, g_ref[...]).astype(BF16)
    proj = jnp.dot(h, w_ref[...], preferred_element_type=F32)
    a = ATTN_WIDTH
    q_ref[0] = proj[:, :a] * (HEAD_DIM ** -0.5)
    k_ref[0] = proj[:, a:2 * a]
    v_ref[0] = proj[:, 2 * a:3 * a]
    u_ref[...] = proj[:, 3 * a:]


def _in_proj(x, prev, g, w_bf16, *, ts=512):
    bsz, seq, dm = x.shape
    ts = min(ts, seq)
    row = pl.BlockSpec((1, ts, dm), lambda b, i: (b, i, 0))
    qkv = pl.BlockSpec((1, ts, ATTN_WIDTH), lambda b, i: (b, i, 0))
    ns = seq // ts
    chunks = pl.BlockSpec((ts * SUBLANES, LANES), lambda b, i: (b * ns + i, 0))
    ins = [x] + ([prev] if prev is not None else []) + [g.reshape(1, dm), w_bf16]
    in_specs = [row] + ([chunks] if prev is not None else []) + [
        pl.BlockSpec((1, dm), lambda b, i: (0, 0)),
        pl.BlockSpec(w_bf16.shape, lambda b, i: (0, 0)),
    ]
    return pl.pallas_call(
        functools.partial(_in_proj_kernel, has_prev=prev is not None),
        grid=(bsz, seq // ts),
        in_specs=in_specs,
        out_specs=[row, qkv, qkv, qkv, pl.BlockSpec((ts, SSM_WIDTH), lambda b, i: (i, b))],
        out_shape=[
            jax.ShapeDtypeStruct((bsz, seq, dm), F32),
            jax.ShapeDtypeStruct((bsz, seq, ATTN_WIDTH), F32),
            jax.ShapeDtypeStruct((bsz, seq, ATTN_WIDTH), F32),
            jax.ShapeDtypeStruct((bsz, seq, ATTN_WIDTH), F32),
            jax.ShapeDtypeStruct((seq, bsz * SSM_WIDTH), F32),
        ],
        compiler_params=_cparams(("parallel", "arbitrary"), VMEM_LIMIT),
        name="in_proj",
    )(*ins)


def _t5_buckets(rel):
    half = REL_BUCKETS // 2
    max_exact = half // 2
    n = np.abs(rel)
    large = max_exact + (np.log(np.maximum(n, 1) / max_exact)
                         / np.log(REL_MAX_DISTANCE / max_exact) * (half - max_exact)).astype(np.int32)
    large = np.minimum(large, half - 1)
    return (np.where(rel > 0, half, 0) + np.where(n < max_exact, n, large)).astype(np.int32)


def _attn_bias_tables(rel_bias, dilation):
    ql = np.arange(QBLK)[:, None]
    kl = np.arange(KWIN)[None, :]
    delta = np.stack([kl + off - ql for off in (0, -BAND, -2 * BAND)])
    buckets = np.where(np.abs(delta) <= BAND, _t5_buckets(delta * dilation), -1)
    rb = rel_bias.astype(F32).T
    bk = jnp.asarray(buckets, jnp.int32)[None]
    tab = jnp.full((rb.shape[0],) + buckets.shape, NEG_INF, F32)
    for b in range(REL_BUCKETS):
        tab = jnp.where(bk == b, rb[:, b][:, None, None, None], tab)
    return tab


BLOCKS_PER_STEP = 2


def _attn_kernel(q_ref, k_ref, v_ref, *rest, seq):
    bias_refs, (o_ref, acc_ref, m_ref, z_ref) = rest[:len(DILATED_PATTERNS)], rest[len(DILATED_PATTERNS):]
    lane = lax.broadcasted_iota(jnp.int32, (QBLK, LANES), 1)
    is_h0 = lane < HEAD_DIM
    dn = (((1,), (1,)), ((), ()))
    nsteps = seq // QBLK

    def rows(start, size, d):
        return pl.ds(start, size) if d == 1 else pl.ds(start, size, stride=d)

    def block(n, d, bias_ref, first, last):
        length = seq // d
        nblk = length // QBLK
        r, i = n >> (nblk.bit_length() - 1), n & (nblk - 1)
        s = i * QBLK
        ks = jnp.clip(s - BAND, 0, length - KWIN)
        var = jnp.where(i == 0, 0, jnp.where(i == nblk - 1, 2, 1))
        q_rows = rows(r + d * s, QBLK, d)
        k_rows = rows(r + d * ks, KWIN, d)
        qb = q_ref[q_rows, :].astype(BF16)
        kb = k_ref[k_rows, :].astype(BF16)
        vb = v_ref[k_rows, :].astype(BF16)
        outs, ms, zs = [], [], []
        for h in range(2):
            keep = is_h0 if h == 0 else jnp.logical_not(is_h0)
            qh = jnp.where(keep, qb, jnp.zeros_like(qb))
            logits = lax.dot_general(qh, kb, dn, preferred_element_type=F32) + bias_ref[h, var]
            m = jnp.max(logits, axis=-1, keepdims=True)
            p = jnp.exp(logits - m)
            outs.append(jnp.dot(p.astype(BF16), vb, preferred_element_type=F32))
            ms.append(jnp.broadcast_to(m, (QBLK, LANES)))
            zs.append(jnp.broadcast_to(jnp.sum(p, axis=-1, keepdims=True), (QBLK, LANES)))
        o = jnp.where(is_h0, outs[0], outs[1])
        m = jnp.where(is_h0, ms[0], ms[1])
        z = jnp.where(is_h0, zs[0], zs[1])
        if not first:
            m_old = m_ref[q_rows, :]
            m_new = jnp.maximum(m_old, m)
            a, b = jnp.exp(m_old - m_new), jnp.exp(m - m_new)
            o = acc_ref[q_rows, :] * a + o * b
            z = z_ref[q_rows, :] * a + z * b
            m = m_new
        if last:
            o_ref[q_rows, :] = o / z
        else:
            acc_ref[q_rows, :] = o
            m_ref[q_rows, :] = m
            z_ref[q_rows, :] = z

    for p, ((_, d), bias_ref) in enumerate(zip(DILATED_PATTERNS, bias_refs)):
        def step(g, carry, d=d, bias_ref=bias_ref, p=p):
            for j in range(BLOCKS_PER_STEP):
                block(g * BLOCKS_PER_STEP + j, d, bias_ref, p == 0, p == len(DILATED_PATTERNS) - 1)
            return carry
        lax.fori_loop(0, nsteps // BLOCKS_PER_STEP, step, 0)


def _attention(q, k, v, biases):
    bsz, seq, width = q.shape
    for _, d in DILATED_PATTERNS:
        length = seq // d
        assert length >= KWIN and length % QBLK == 0 and (length // QBLK) & (length // QBLK - 1) == 0
    assert (seq // QBLK) % BLOCKS_PER_STEP == 0
    blk = pl.BlockSpec((None, seq, LANES), lambda b, c: (b, 0, c))
    bias_spec = pl.BlockSpec((2, 3, QBLK, KWIN), lambda b, c: (c, 0, 0, 0))
    return pl.pallas_call(
        functools.partial(_attn_kernel, seq=seq),
        grid=(bsz, width // LANES),
        in_specs=[blk, blk, blk] + [bias_spec] * len(biases),
        out_specs=blk,
        out_shape=jax.ShapeDtypeStruct((bsz, seq, width), F32),
        scratch_shapes=[pltpu.VMEM((seq, LANES), F32)] * 3,
        compiler_params=_cparams(("parallel", "arbitrary"), VMEM_LIMIT),
        name="attention",
    )(q, k, v, *biases)


SSM_LANE_GROUPS = SSM_WIDTH // LANES
SSM_GB_STATES = (LANES // SSM_GROUP) * SSM_STATE


def _ssm_params(lam_re, lam_im, log_step, b_re, b_im, c_re, c_im):
    f = lambda t: t.astype(F32)
    lr, li = f(lam_re), f(lam_im)
    step = jnp.exp(f(log_step))[..., None]
    mag = jnp.exp(lr * step)
    ar, ai = mag * jnp.cos(li * step), mag * jnp.sin(li * step)
    nr, ni = ar - 1.0, ai
    den = lr * lr + li * li
    cr, ci = (nr * lr + ni * li) / den, (ni * lr - nr * li) / den
    br, bi = f(b_re), f(b_im)
    bbr = cr[..., None] * br - ci[..., None] * bi
    bbi = cr[..., None] * bi + ci[..., None] * br
    gpb = LANES // SSM_GROUP
    eye = jnp.eye(gpb, dtype=F32)

    def in_map(t):
        t = t.reshape(2, SSM_LANE_GROUPS, gpb, SSM_STATE, SSM_GROUP)
        return jnp.einsum('dbgpc,gh->dbgchp', t, eye).reshape(2, SSM_LANE_GROUPS, LANES, SSM_GB_STATES)

    def out_map(t):
        t = t.reshape(2, SSM_LANE_GROUPS, gpb, SSM_GROUP, SSM_STATE)
        return jnp.einsum('dbgcp,gh->dbgphc', t, eye).reshape(2, SSM_LANE_GROUPS, SSM_GB_STATES, LANES)

    bw = jnp.concatenate([in_map(bbr), in_map(bbi)], axis=-1).astype(BF16)
    cw = jnp.concatenate([out_map(f(c_re)), -out_map(f(c_im))], axis=-2).astype(BF16)
    a = jnp.stack([ar.reshape(2, SSM_LANE_GROUPS, SSM_GB_STATES),
                   ai.reshape(2, SSM_LANE_GROUPS, SSM_GB_STATES)], axis=2)
    return a, bw, cw


def _ssm_kernel(u_ref, a_ref, bw_ref, cw_ref, y_ref, st_ref, bu_ref, *, ts, bsz):
    d = pl.program_id(0)
    ns = SSM_GB_STATES

    @pl.when(pl.program_id(1) == 0)
    def _():
        st_ref[...] = jnp.zeros_like(st_ref)

    for gb in range(SSM_LANE_GROUPS):
        ub = u_ref[:, gb * LANES:(gb + 1) * LANES].astype(BF16)
        bu_ref[...] = jnp.dot(ub, bw_ref[0, gb], preferred_element_type=F32)
        ar = jnp.broadcast_to(a_ref[0, gb, 0:1, :], (bsz, ns))
        ai = jnp.broadcast_to(a_ref[0, gb, 1:2, :], (bsz, ns))

        def step(j, carry, ar=ar, ai=ai):
            xr, xi = carry
            tl = jnp.where(d == 0, j, ts - 1 - j)
            r = pl.multiple_of(tl * bsz, bsz)
            nr = ar * xr - ai * xi + bu_ref[pl.ds(r, bsz), :ns]
            ni = ar * xi + ai * xr + bu_ref[pl.ds(r, bsz), ns:]
            bu_ref[pl.ds(r, bsz), :ns] = nr
            bu_ref[pl.ds(r, bsz), ns:] = ni
            return nr, ni

        xr, xi = lax.fori_loop(0, ts, step, (st_ref[gb, :, :ns], st_ref[gb, :, ns:]))
        st_ref[gb, :, :ns] = xr
        st_ref[gb, :, ns:] = xi
        y_ref[0, :, gb * LANES:(gb + 1) * LANES] = jnp.dot(
            bu_ref[...].astype(BF16), cw_ref[0, gb], preferred_element_type=F32)


def _ssm_scan(u_tm, a, bw, cw, *, bsz, ts=64):
    rows, width = u_tm.shape
    seq = rows // bsz
    ts = min(ts, seq)
    nt = seq // ts
    tblk = lambda d, i: jnp.where(d == 0, i, nt - 1 - i)
    return pl.pallas_call(
        functools.partial(_ssm_kernel, ts=ts, bsz=bsz),
        grid=(2, nt),
        in_specs=[
            pl.BlockSpec((ts * bsz, width), lambda d, i: (tblk(d, i), 0)),
            pl.BlockSpec((1,) + a.shape[1:], lambda d, i: (d, 0, 0, 0)),
            pl.BlockSpec((1,) + bw.shape[1:], lambda d, i: (d, 0, 0, 0)),
            pl.BlockSpec((1,) + cw.shape[1:], lambda d, i: (d, 0, 0, 0)),
        ],
        out_specs=pl.BlockSpec((1, ts * bsz, width), lambda d, i: (d, tblk(d, i), 0)),
        out_shape=jax.ShapeDtypeStruct((2, rows, width), F32),
        scratch_shapes=[pltpu.VMEM((SSM_LANE_GROUPS, bsz, 2 * SSM_GB_STATES), F32),
                        pltpu.VMEM((ts * bsz, 2 * SSM_GB_STATES), F32)],
        compiler_params=_cparams(("arbitrary", "arbitrary"), VMEM_LIMIT),
        name="ssm_scan",
    )(u_tm, a, bw, cw)


def _ssm_post_kernel(y_ref, u_ref, d_ref, wg_ref, g_ref, wo_ref, z_ref):
    y = _gelu(y_ref[0] + y_ref[1] + d_ref[...] * u_ref[...]).astype(BF16)
    ab = jnp.dot(y, wg_ref[...], preferred_element_type=F32)
    ssm = ab[:, :SSM_WIDTH] * jax.nn.sigmoid(ab[:, SSM_WIDTH:])
    n = _rms(ssm, g_ref[...]).astype(BF16)
    z_ref[...] = jnp.dot(n, wo_ref[...], preferred_element_type=F32)


def _ssm_post(y, u_tm, d_skip, w_glu_bf16, g_ssm, w_out_ssm_bf16, *, tr=512):
    rows, width = u_tm.shape
    tr = min(tr, rows)
    dm = w_out_ssm_bf16.shape[1]
    full = lambda a: pl.BlockSpec(a.shape, lambda i: (0,) * a.ndim)
    d2, g2 = d_skip.reshape(1, width), g_ssm.reshape(1, width)
    return pl.pallas_call(
        _ssm_post_kernel,
        grid=(rows // tr,),
        in_specs=[pl.BlockSpec((2, tr, width), lambda i: (0, i, 0)),
                  pl.BlockSpec((tr, width), lambda i: (i, 0)),
                  full(d2), full(w_glu_bf16), full(g2), full(w_out_ssm_bf16)],
        out_specs=pl.BlockSpec((tr, dm), lambda i: (i, 0)),
        out_shape=jax.ShapeDtypeStruct((rows, dm), F32),
        compiler_params=_cparams(("parallel",), VMEM_LIMIT),
        name="ssm_post",
    )(y, u_tm, d2, w_glu_bf16, g2, w_out_ssm_bf16)


def _mix_out_kernel(x_ref, z_ref, a_ref, ga_ref, wo_ref, gf_ref, wq_ref, xn_ref, h_ref, q_ref):
    n = _rms(a_ref[0], ga_ref[...]).astype(BF16)
    xn = x_ref[0] + z_ref[...] + jnp.dot(n, wo_ref[...], preferred_element_type=F32)
    xn_ref[0] = xn
    h = _rms(xn, gf_ref[...])
    for c in range(SUBLANES):
        h_ref[pl.ds(c, h.shape[0], stride=SUBLANES), :] = h[:, c * LANES:(c + 1) * LANES]
    qp = jnp.dot(h.astype(BF16), wq_ref[...], preferred_element_type=F32)
    for hd in range(PEER_HEADS):
        q_ref[hd] = qp[:, hd * LANES:(hd + 1) * LANES]


def _mix_out(x, z_tm, attn, g_attn, w_out_attn_bf16, norm_ffn, w_query, *, ts=256):
    bsz, seq, dm = x.shape
    ts = min(ts, seq)
    ns = seq // ts
    row = pl.BlockSpec((1, ts, dm), lambda b, i: (b, i, 0))
    half = pl.BlockSpec((1, ts, ATTN_WIDTH), lambda b, i: (b, i, 0))
    full = lambda a: pl.BlockSpec(a.shape, lambda b, i: (0,) * a.ndim)
    ga, gf = g_attn.reshape(1, ATTN_WIDTH), norm_ffn.reshape(1, dm)
    qdim = w_query.shape[1] // PEER_HEADS
    return pl.pallas_call(
        _mix_out_kernel,
        grid=(bsz, ns),
        in_specs=[row, pl.BlockSpec((ts, dm), lambda b, i: (i, b)), half,
                  full(ga), full(w_out_attn_bf16), full(gf), full(w_query)],
        out_specs=[row, pl.BlockSpec((ts * SUBLANES, LANES), lambda b, i: (b * ns + i, 0)),
                   pl.BlockSpec((PEER_HEADS, ts, qdim), lambda b, i: (0, b * ns + i, 0))],
        out_shape=[jax.ShapeDtypeStruct((bsz, seq, dm), F32),
                   jax.ShapeDtypeStruct((bsz * seq * SUBLANES, LANES), F32),

                   jax.ShapeDtypeStruct((PEER_HEADS, bsz * seq, qdim), F32)],
        compiler_params=_cparams(("parallel", "arbitrary"), VMEM_LIMIT),
        name="mix_out",
    )(x, z_tm, attn, ga, w_out_attn_bf16, gf, w_query)


TOPK_TOKENS = SUBLANES * LANES
KEY_PITCH = PEER_KEYS + 4
_CANDIDATES = tuple((a, b) for a in range(PEER_TOPK) for b in range(PEER_TOPK) if (a + 1) * (b + 1) <= PEER_TOPK)


def _tree(op, xs):
    xs = list(xs)
    while len(xs) > 1:
        xs = [op(xs[i], xs[i + 1]) if i + 1 < len(xs) else xs[i] for i in range(0, len(xs), 2)]
    return xs[0]


def _extract16(problems):
    big = jnp.int32(2 ** 30)
    ninf = jnp.float32(-jnp.inf)

    def step(r, ms):
        nxt = []
        for p, m in zip(problems, ms):
            s_ref, order = p["s"], p["order"]
            n = len(order)
            am = _tree(jnp.minimum, [jnp.where(s_ref[k] == m, order[k], big) for k in range(n)])
            news, pays = [], []
            for k in range(n):
                hit = am == order[k]
                nk = jnp.where(hit, ninf, s_ref[k])
                s_ref[k] = nk
                news.append(nk)
                if p.get("pay") is not None:
                    pays.append(jnp.where(hit, p["pay"][k], -1))
            p["vals"][r] = m
            p["picks"][r] = _tree(jnp.maximum, pays) if pays else am
            nxt.append(_tree(jnp.maximum, news))
        return tuple(nxt)

    init = tuple(_tree(jnp.maximum, [p["s"][k] for k in range(len(p["order"]))]) for p in problems)
    lax.fori_loop(0, PEER_TOPK, step, init)


def _peer_topk_kernel(q_ref, k_ref, idx_ref, gate_ref,
                      slab_ref, s1_ref, s2_ref, t1_ref, i1_ref, t2_ref, i2_ref, cand_ref, pay_ref, ts_ref, ex_ref):
    dn = (((1,), (1,)), ((), ()))
    keys = tuple(range(PEER_KEYS))

    def head(h, carry):
        for w, s_ref in ((0, s1_ref), (1, s2_ref)):
            for j in range(SUBLANES):
                slab_ref[j * KEY_PITCH:j * KEY_PITCH + PEER_KEYS, :] = lax.dot_general(
                    k_ref[w, h], q_ref[h, j * LANES:(j + 1) * LANES, :].astype(BF16), dn,
                    preferred_element_type=F32)
            for k in range(PEER_KEYS):
                s_ref[k] = slab_ref[pl.ds(k, SUBLANES, stride=KEY_PITCH), :]
        _extract16([dict(s=s1_ref, order=keys, vals=t1_ref, picks=i1_ref),
                    dict(s=s2_ref, order=keys, vals=t2_ref, picks=i2_ref)])
        for c, (a, b) in enumerate(_CANDIDATES):
            cand_ref[c] = t1_ref[a] + t2_ref[b]
            pay_ref[c] = i1_ref[a] * PEER_KEYS + i2_ref[b]
        _extract16([dict(s=cand_ref, order=tuple(a * PEER_TOPK + b for a, b in _CANDIDATES), pay=pay_ref,
                         vals=ts_ref, picks=ex_ref)])
        top_s = ts_ref[...]
        e = jnp.exp(top_s - jnp.max(top_s, axis=0, keepdims=True))
        gate_ref[0, h] = e / jnp.sum(e, axis=0, keepdims=True)
        idx_ref[0, h] = ex_ref[...] * ROWS_PER_EXPERT
        return carry

    lax.fori_loop(0, PEER_HEADS, head, 0)


def _peer_topk(qp, keys_pad):
    n_tok = qp.shape[1]
    tt = TOPK_TOKENS
    assert n_tok % tt == 0
    shp = (n_tok // tt, PEER_HEADS, PEER_TOPK, SUBLANES, LANES)
    out = pl.BlockSpec((1,) + shp[1:], lambda i: (i, 0, 0, 0, 0))
    vregs = lambda n, dt: pltpu.VMEM((n, SUBLANES, LANES), dt)
    idx, gate = pl.pallas_call(
        _peer_topk_kernel,
        grid=(n_tok // tt,),
        in_specs=[pl.BlockSpec((PEER_HEADS, tt, qp.shape[2]), lambda i: (0, i, 0)),
                  pl.BlockSpec(keys_pad.shape, lambda i: (0, 0, 0, 0))],
        out_specs=[out, out],
        out_shape=[jax.ShapeDtypeStruct(shp, jnp.int32), jax.ShapeDtypeStruct(shp, F32)],
        scratch_shapes=[pltpu.VMEM((SUBLANES * KEY_PITCH, LANES), F32),
                        vregs(PEER_KEYS, F32), vregs(PEER_KEYS, F32),
                        vregs(PEER_TOPK, F32), vregs(PEER_TOPK, jnp.int32),
                        vregs(PEER_TOPK, F32), vregs(PEER_TOPK, jnp.int32),
                        vregs(len(_CANDIDATES), F32), vregs(len(_CANDIDATES), jnp.int32),
                        vregs(PEER_TOPK, F32), vregs(PEER_TOPK, jnp.int32)],
        compiler_params=_cparams(("parallel",), VMEM_LIMIT),
        name="peer_topk",
    )(qp, keys_pad)
    to_tok = lambda a: jnp.transpose(a, (0, 3, 4, 1, 2)).reshape(n_tok, PEER_SEL)
    return to_tok(idx), to_tok(gate)


ROWS_PER_EXPERT = 4
PAIR_CHUNK = 32
CHUNK_ROWS = PAIR_CHUNK * ROWS_PER_EXPERT
SMEM_GROUP = 8
N_CHUNKS = PEER_SEL // PAIR_CHUNK
CHUNK_SHIFT = N_CHUNKS.bit_length() - 1
assert 1 << CHUNK_SHIFT == N_CHUNKS
_GROUP_SLOT = tuple(
    2 * (2 * (0 if (j % 2) else 1) + (1 if (j // 2) in (0, 2) else 0)) + (0 if (j // 2) < 2 else 1)
    for j in range(8))
_CHUNK_SLOT = tuple(8 * (j // 8) + _GROUP_SLOT[j % 8] for j in range(PAIR_CHUNK))


def _pack_table(tbl):
    e, dm = tbl.shape
    tb = tbl.astype(BF16)
    lo = lax.bitcast_convert_type(tb[:, :dm // 2], jnp.uint16).astype(jnp.uint32)
    hi = lax.bitcast_convert_type(tb[:, dm // 2:], jnp.uint16).astype(jnp.uint32)
    w = lax.bitcast_convert_type(lo | (hi << 16), jnp.int32)
    return w.reshape(e * ROWS_PER_EXPERT, LANES)


def _unpack_words(w):
    lo = lax.bitcast_convert_type(w << 16, F32)
    hi = lax.bitcast_convert_type(w & jnp.int32(-65536), F32)
    return lo, hi


def _gather_chunk(idx_ref, tab_ref, buf_ref, c, group, slots):
    for g in range(PAIR_CHUNK // group):
        ids = idx_ref.at[pl.ds(pl.multiple_of(c * PAIR_CHUNK + g * group, group), group)]
        for i in range(group):
            e4 = pl.multiple_of(ids[i], ROWS_PER_EXPERT)
            s = slots[g * group + i] * ROWS_PER_EXPERT
            buf_ref[s:s + ROWS_PER_EXPERT, :] = tab_ref[pl.ds(e4, ROWS_PER_EXPERT), :]


def _chunk_loop(n_chunks, idx_ref, tab_ref, buf_a, buf_b, consume, init, group, slots):
    last = n_chunks - 1
    gather = functools.partial(_gather_chunk, idx_ref, tab_ref, group=group, slots=slots)
    gather(buf_a, 0)

    def body(i, carry):
        c = 2 * i
        carry = consume(c, buf_a, carry)
        gather(buf_b, c + 1)
        carry = consume(c + 1, buf_b, carry)
        gather(buf_a, jnp.minimum(c + 2, last))
        return carry

    return lax.fori_loop(0, n_chunks // 2, body, init)


def _peer_u_kernel(idx_ref, h_ref, gate_ref, tab_ref, coef_ref, buf_a, buf_b, r_ref, d_ref, *, tt):
    sub = lax.broadcasted_iota(jnp.int32, (SUBLANES, LANES), 0)
    low = sub < ROWS_PER_EXPERT
    m_a = ((sub % 4) >= 2)[None]
    m_b = ((sub % 2) == 1)[None]
    nv = PAIR_CHUNK // 2

    def pair_partials(c, buf_ref, carry):
        t = c >> CHUNK_SHIFT
        h8 = h_ref[pl.ds(pl.multiple_of(t * SUBLANES, SUBLANES), SUBLANES), :]
        sw = pltpu.roll(h8, ROWS_PER_EXPERT, axis=0)
        ha = jnp.where(low, h8, sw)[None]
        hb = jnp.where(low, sw, h8)[None]
        lo, hi = _unpack_words(buf_ref[...])
        x = lo.reshape(nv, SUBLANES, LANES) * ha + hi.reshape(nv, SUBLANES, LANES) * hb
        x = x.reshape(nv // 2, 2, SUBLANES, LANES)
        xe, xo = x[:, 0], x[:, 1]
        a = xe + pltpu.roll(xe, 2, axis=1)
        b = xo + pltpu.roll(xo, 6, axis=1)
        m = jnp.where(m_a, a, b).reshape(nv // 4, 2, SUBLANES, LANES)
        me, mo = m[:, 0], m[:, 1]
        a2 = me + pltpu.roll(me, 1, axis=1)
        b2 = mo + pltpu.roll(mo, 7, axis=1)
        r_ref[pl.ds(pl.multiple_of(c * PAIR_CHUNK, PAIR_CHUNK), PAIR_CHUNK), :] = (
            jnp.where(m_b, a2, b2).reshape(PAIR_CHUNK, LANES))
        return carry

    _chunk_loop(tt * N_CHUNKS, idx_ref, tab_ref, buf_a, buf_b, pair_partials, 0, SMEM_GROUP, _CHUNK_SLOT)

    def lane_sums(g, carry):
        for i in range(SUBLANES):
            t = g * SUBLANES + i
            r_t = r_ref[pl.ds(pl.multiple_of(t * PEER_SEL, PEER_SEL), PEER_SEL), :]
            d_ref[pl.ds(t, 1), :] = jnp.sum(r_t.T, axis=0, keepdims=True)
        return carry

    lax.fori_loop(0, tt // SUBLANES, lane_sums, 0)
    coef_ref[...] = gate_ref[...] * _gelu(d_ref[...])


def _peer_u(idx4, h8, gates, tab, *, tt=128):
    n_tok = gates.shape[0]
    tt = min(tt, n_tok)
    tok2 = pl.BlockSpec((tt, PEER_SEL), lambda i: (i, 0))
    return pl.pallas_call(
        functools.partial(_peer_u_kernel, tt=tt),
        grid=(n_tok // tt,),
        in_specs=[pl.BlockSpec((tt * PEER_SEL,), lambda i: (i,), memory_space=pltpu.SMEM),
                  pl.BlockSpec((tt * SUBLANES, LANES), lambda i: (i, 0)), tok2,
                  pl.BlockSpec(tab.shape, lambda i: (0, 0), pipeline_mode=pl.Buffered(1))],
        out_specs=tok2,
        out_shape=jax.ShapeDtypeStruct((n_tok, PEER_SEL), F32),
        scratch_shapes=[pltpu.VMEM((CHUNK_ROWS, LANES), jnp.int32),
                        pltpu.VMEM((CHUNK_ROWS, LANES), jnp.int32),
                        pltpu.VMEM((tt * PEER_SEL, LANES), F32),
                        pltpu.VMEM((tt, PEER_SEL), F32)],
        compiler_params=_cparams(("arbitrary",), VMEM_LIMIT),
        name="peer_u",
    )(idx4, h8, gates, tab)


def _peer_v_kernel(idx_ref, coef_ref, tab_ref, o_ref, buf_a, buf_b, *, tt):
    sub = lax.broadcasted_iota(jnp.int32, (SUBLANES, LANES), 0)
    low = sub < ROWS_PER_EXPERT
    nv = PAIR_CHUNK // 2

    def accumulate(c, buf_ref, acc):
        acc_lo, acc_hi = acc
        lo, hi = _unpack_words(buf_ref[...])
        lo = lo.reshape(nv, SUBLANES, LANES)
        hi = hi.reshape(nv, SUBLANES, LANES)
        cs = coef_ref.at[pl.ds(pl.multiple_of(c * nv, nv), nv)]
        for v in range(nv):
            c_lo, c_hi = _unpack_words(jnp.full((SUBLANES, LANES), cs[v], jnp.int32))
            cv = jnp.where(low, c_lo, c_hi)
            acc_lo = acc_lo + cv * lo[v]
            acc_hi = acc_hi + cv * hi[v]
        lo4 = acc_lo + pltpu.roll(acc_lo, ROWS_PER_EXPERT, axis=0)
        hi4 = acc_hi + pltpu.roll(acc_hi, ROWS_PER_EXPERT, axis=0)
        o_ref[c >> CHUNK_SHIFT] = jnp.where(low, lo4, hi4)
        last = (c & (N_CHUNKS - 1)) == N_CHUNKS - 1
        return jnp.where(last, 0.0, acc_lo), jnp.where(last, 0.0, acc_hi)

    z = jnp.zeros((SUBLANES, LANES), F32)
    _chunk_loop(tt * N_CHUNKS, idx_ref, tab_ref, buf_a, buf_b, accumulate, (z, z), PAIR_CHUNK,
                tuple(range(PAIR_CHUNK)))


def _pack_coefs(coef):
    n_tok = coef.shape[0]
    bits = lax.bitcast_convert_type(coef.astype(BF16), jnp.uint16).astype(jnp.uint32)
    bits = bits.reshape(n_tok, PEER_SEL // 2, 2)
    w = bits[:, :, 0] | (bits[:, :, 1] << 16)
    return lax.bitcast_convert_type(w, jnp.int32).reshape(n_tok * (PEER_SEL // 2))


def _peer_v(idx4, coef_words, tab, *, tt=128):
    n_tok = idx4.shape[0] // PEER_SEL
    tt = min(tt, n_tok)
    smem = pl.BlockSpec((tt * PEER_SEL,), lambda i: (i,), memory_space=pltpu.SMEM)
    return pl.pallas_call(
        functools.partial(_peer_v_kernel, tt=tt),
        grid=(n_tok // tt,),
        in_specs=[smem, pl.BlockSpec((tt * PEER_SEL // 2,), lambda i: (i,), memory_space=pltpu.SMEM),
                  pl.BlockSpec(tab.shape, lambda i: (0, 0), pipeline_mode=pl.Buffered(1))],
        out_specs=pl.BlockSpec((tt, SUBLANES, LANES), lambda i: (i, 0, 0)),
        out_shape=jax.ShapeDtypeStruct((n_tok, SUBLANES, LANES), F32),
        scratch_shapes=[pltpu.VMEM((CHUNK_ROWS, LANES), jnp.int32),
                        pltpu.VMEM((CHUNK_ROWS, LANES), jnp.int32)],
        compiler_params=_cparams(("arbitrary",), VMEM_LIMIT),
        name="peer_v",
    )(idx4, coef_words, tab)


def _peer(h8, qp, sub_keys, expert_u, expert_v):
    n_tok = qp.shape[1]
    half = sub_keys.shape[-1]
    z = jnp.zeros_like(sub_keys[0])
    keys_pad = jnp.stack([jnp.concatenate([sub_keys[0], z], axis=-1),
                          jnp.concatenate([z, sub_keys[1]], axis=-1)]).astype(BF16)
    assert keys_pad.shape[-1] == 2 * half == qp.shape[-1]
    idx4, gates = _peer_topk(qp, keys_pad)
    idx4 = idx4.reshape(n_tok * PEER_SEL)
    coef = _peer_u(idx4, h8, gates, _pack_table(expert_u))
    out = _peer_v(idx4, _pack_coefs(coef), _pack_table(expert_v))
    return out.reshape(n_tok * SUBLANES, LANES)


def _final_kernel(x_ref, p_ref, g_ref, o_ref):
    o_ref[...] = _rms(x_ref[...] + _rows_from_chunks(p_ref, x_ref.shape[0]), g_ref[...])


def _final_norm(x, p, g, *, tr=1024):
    rows, dm = x.shape
    tr = min(tr, rows)
    blk = pl.BlockSpec((tr, dm), lambda i: (i, 0))
    return pl.pallas_call(
        _final_kernel,
        grid=(rows // tr,),
        in_specs=[blk, pl.BlockSpec((tr * SUBLANES, LANES), lambda i: (i, 0)),
                  pl.BlockSpec((1, dm), lambda i: (0, 0))],
        out_specs=blk,
        out_shape=jax.ShapeDtypeStruct((rows, dm), F32),
        compiler_params=_cparams(("parallel",), VMEM_LIMIT),
        name="final_norm",
    )(x, p, g.reshape(1, dm))


def kernel(x, w_in, w_out, rel_bias, g_attn, g_ssm, norm_mix, norm_ffn, lam_re, lam_im, log_step, b_re, b_im, c_re, c_im, d_skip, w_glu, w_query, sub_keys, expert_u, expert_v, norm_final):
    bsz, seq, dm = x.shape
    depth = w_in.shape[0]
    prev = None
    biases = [_attn_bias_tables(rel_bias, d) for _, d in DILATED_PATTERNS]
    for l in range(depth):
        x, q, k, v, u = _in_proj(x, prev, norm_mix[l], w_in[l].astype(BF16))
        attn = _attention(q, k, v, biases)
        u_tm = u.reshape(seq * bsz, SSM_WIDTH)
        a, bw, cw = _ssm_params(lam_re[l], lam_im[l], log_step[l], b_re[l], b_im[l], c_re[l], c_im[l])
        y = _ssm_scan(u_tm, a, bw, cw, bsz=bsz)
        wo = w_out[l].astype(BF16)
        z = _ssm_post(y, u_tm, d_skip[l], w_glu[l].astype(BF16), g_ssm[l], wo[ATTN_WIDTH:])
        x, h, qp = _mix_out(x, z.reshape(seq, bsz * dm), attn,
                            g_attn[l], wo[:ATTN_WIDTH], norm_ffn[l], w_query[l].astype(BF16))
        prev = _peer(h, qp, sub_keys[l], expert_u[l], expert_v[l])
    out = _final_norm(x.reshape(bsz * seq, dm), prev, norm_final)
    return out.reshape(bsz, seq, dm)
```

```python
import functools
import math

import numpy as np
import jax
import jax.numpy as jnp
from jax import lax
from jax.experimental import pallas as pl
from jax.experimental.pallas import tpu as pltpu

F32 = jnp.float32
BF16 = jnp.bfloat16

EPS = 1e-6
NEG_INF = -1e30
HEAD_DIM = 64
ATTN_WIDTH = 512
SSM_WIDTH = 512
SSM_GROUP = 16
SSM_STATE = 64
DILATED_PATTERNS = ((128, 1), (512, 4), (2048, 16))
REL_BUCKETS = 32
REL_MAX_DISTANCE = 1024
PEER_HEADS = 8
PEER_KEYS = 128
PEER_TOPK = 16
PEER_SEL = PEER_HEADS * PEER_TOPK

LANES = 128
SUBLANES = 8
QBLK = 128
KWIN = 256
BAND = 64
VMEM_LIMIT = 52 * 1024 * 1024
PEER_U_VMEM_LIMIT = 58 * 1024 * 1024


def _cparams(sem, vmem=None):
    return pltpu.CompilerParams(dimension_semantics=sem, vmem_limit_bytes=vmem)


def _rms(x, g):
    return x * lax.rsqrt(jnp.mean(x * x, axis=-1, keepdims=True) + EPS) * g


def _gelu(x):
    return 0.5 * x * (1.0 + lax.erf(x * (1.0 / math.sqrt(2.0))))


def _rows_from_chunks(p_ref, n_rows):
    return jnp.concatenate([p_ref[pl.ds(c, n_rows, stride=SUBLANES), :] for c in range(SUBLANES)], axis=-1)


def _in_proj_kernel(*refs, has_prev):
    if has_prev:
        x_ref, p_ref, g_ref, w_ref, xo_ref, q_ref, k_ref, v_ref, u_ref = refs
        x = x_ref[0] + _rows_from_chunks(p_ref, x_ref.shape[1])
    else:
        x_ref, g_ref, w_ref, xo_ref, q_ref, k_ref, v_ref, u_ref = refs
        x = x_ref[0]
    xo_ref[0] = x
    h = _rms(x, g_ref[...]).astype(BF16)
    proj = jnp.dot(h, w_ref[...], preferred_element_type=F32)
    a = ATTN_WIDTH
    q_ref[0] = proj[:, :a] * (HEAD_DIM ** -0.5)
    k_ref[0] = proj[:, a:2 * a]
    v_ref[0] = proj[:, 2 * a:3 * a]
    u_ref[...] = proj[:, 3 * a:]


def _in_proj(x, prev, g, w_bf16, *, ts=512):
    bsz, seq, dm = x.shape
    ts = min(ts, seq)
    row = pl.BlockSpec((1, ts, dm), lambda b, i: (b, i, 0))
    qkv = pl.BlockSpec((1, ts, ATTN_WIDTH), lambda b, i: (b, i, 0))
    ns = seq // ts
    chunks = pl.BlockSpec((ts * SUBLANES, LANES), lambda b, i: (b * ns + i, 0))
    ins = [x] + ([prev] if prev is not None else []) + [g.reshape(1, dm), w_bf16]
    in_specs = [row] + ([chunks] if prev is not None else []) + [
        pl.BlockSpec((1, dm), lambda b, i: (0, 0)),
        pl.BlockSpec(w_bf16.shape, lambda b, i: (0, 0)),
    ]
    return pl.pallas_call(
        functools.partial(_in_proj_kernel, has_prev=prev is not None),
        grid=(bsz, seq // ts),
        in_specs=in_specs,
        out_specs=[row, qkv, qkv, qkv, pl.BlockSpec((ts, SSM_WIDTH), lambda b, i: (i, b))],
        out_shape=[
            jax.ShapeDtypeStruct((bsz, seq, dm), F32),
            jax.ShapeDtypeStruct((bsz, seq, ATTN_WIDTH), F32),
            jax.ShapeDtypeStruct((bsz, seq, ATTN_WIDTH), F32),
            jax.ShapeDtypeStruct((bsz, seq, ATTN_WIDTH), F32),
            jax.ShapeDtypeStruct((seq, bsz * SSM_WIDTH), F32),
        ],
        compiler_params=_cparams(("parallel", "arbitrary"), VMEM_LIMIT),
        name="in_proj",
    )(*ins)


def _t5_buckets(rel):
    half = REL_BUCKETS // 2
    max_exact = half // 2
    n = np.abs(rel)
    large = max_exact + (np.log(np.maximum(n, 1) / max_exact)
                         / np.log(REL_MAX_DISTANCE / max_exact) * (half - max_exact)).astype(np.int32)
    large = np.minimum(large, half - 1)
    return (np.where(rel > 0, half, 0) + np.where(n < max_exact, n, large)).astype(np.int32)


def _attn_bias_tables(rel_bias, dilation):
    ql = np.arange(QBLK)[:, None]
    kl = np.arange(KWIN)[None, :]
    delta = np.stack([kl + off - ql for off in (0, -BAND, -2 * BAND)])
    buckets = np.where(np.abs(delta) <= BAND, _t5_buckets(delta * dilation), -1)
    rb = rel_bias.astype(F32).T
    bk = jnp.asarray(buckets, jnp.int32)[None]
    tab = jnp.full((rb.shape[0],) + buckets.shape, NEG_INF, F32)
    for b in range(REL_BUCKETS):
        tab = jnp.where(bk == b, rb[:, b][:, None, None, None], tab)
    return tab


BLOCKS_PER_STEP = 2


def _attn_kernel(q_ref, k_ref, v_ref, *rest, seq):
    bias_refs, (o_ref, acc_ref, m_ref, z_ref) = rest[:len(DILATED_PATTERNS)], rest[len(DILATED_PATTERNS):]
    lane = lax.broadcasted_iota(jnp.int32, (QBLK, LANES), 1)
    is_h0 = lane < HEAD_DIM
    dn = (((1,), (1,)), ((), ()))
    nsteps = seq // QBLK

    def rows(start, size, d):
        return pl.ds(start, size) if d == 1 else pl.ds(start, size, stride=d)

    def block(n, d, bias_ref, first, last):
        length = seq // d
        nblk = length // QBLK
        r, i = n >> (nblk.bit_length() - 1), n & (nblk - 1)
        s = i * QBLK
        ks = jnp.clip(s - BAND, 0, length - KWIN)
        var = jnp.where(i == 0, 0, jnp.where(i == nblk - 1, 2, 1))
        q_rows = rows(r + d * s, QBLK, d)
        k_rows = rows(r + d * ks, KWIN, d)
        qb = q_ref[q_rows, :].astype(BF16)
        kb = k_ref[k_rows, :].astype(BF16)
        vb = v_ref[k_rows, :].astype(BF16)
        outs, ms, zs = [], [], []
        for h in range(2):
            keep = is_h0 if h == 0 else jnp.logical_not(is_h0)
            qh = jnp.where(keep, qb, jnp.zeros_like(qb))
            logits = lax.dot_general(qh, kb, dn, preferred_element_type=F32) + bias_ref[h, var]
            m = jnp.max(logits, axis=-1, keepdims=True)
            p = jnp.exp(logits - m)
            outs.append(jnp.dot(p.astype(BF16), vb, preferred_element_type=F32))
            ms.append(jnp.broadcast_to(m, (QBLK, LANES)))
            zs.append(jnp.broadcast_to(jnp.sum(p, axis=-1, keepdims=True), (QBLK, LANES)))
        o = jnp.where(is_h0, outs[0], outs[1])
        m = jnp.where(is_h0, ms[0], ms[1])
        z = jnp.where(is_h0, zs[0], zs[1])
        if not first:
            m_old = m_ref[q_rows, :]
            m_new = jnp.maximum(m_old, m)
            a, b = jnp.exp(m_old - m_new), jnp.exp(m - m_new)
            o = acc_ref[q_rows, :] * a + o * b
            z = z_ref[q_rows, :] * a + z * b
            m = m_new
        if last:
            o_ref[q_rows, :] = o / z
        else:
            acc_ref[q_rows, :] = o
            m_ref[q_rows, :] = m
            z_ref[q_rows, :] = z

    for p, ((_, d), bias_ref) in enumerate(zip(DILATED_PATTERNS, bias_refs)):
        def step(g, carry, d=d, bias_ref=bias_ref, p=p):
            for j in range(BLOCKS_PER_STEP):
                block(g * BLOCKS_PER_STEP + j, d, bias_ref, p == 0, p == len(DILATED_PATTERNS) - 1)
            return carry
        lax.fori_loop(0, nsteps // BLOCKS_PER_STEP, step, 0)


def _attention(q, k, v, biases):
    bsz, seq, width = q.shape
    for _, d in DILATED_PATTERNS:
        length = seq // d
        assert length >= KWIN and length % QBLK == 0 and (length // QBLK) & (length // QBLK - 1) == 0
    assert (seq // QBLK) % BLOCKS_PER_STEP == 0
    blk = pl.BlockSpec((None, seq, LANES), lambda b, c: (b, 0, c))
    bias_spec = pl.BlockSpec((2, 3, QBLK, KWIN), lambda b, c: (c, 0, 0, 0))
    return pl.pallas_call(
        functools.partial(_attn_kernel, seq=seq),
        grid=(bsz, width // LANES),
        in_specs=[blk, blk, blk] + [bias_spec] * len(biases),
        out_specs=blk,
        out_shape=jax.ShapeDtypeStruct((bsz, seq, width), F32),
        scratch_shapes=[pltpu.VMEM((seq, LANES), F32)] * 3,
        compiler_params=_cparams(("parallel", "arbitrary"), VMEM_LIMIT),
        name="attention",
    )(q, k, v, *biases)


SSM_LANE_GROUPS = SSM_WIDTH // LANES
SSM_GB_STATES = (LANES // SSM_GROUP) * SSM_STATE


def _ssm_params(lam_re, lam_im, log_step, b_re, b_im, c_re, c_im):
    f = lambda t: t.astype(F32)
    lr, li = f(lam_re), f(lam_im)
    step = jnp.exp(f(log_step))[..., None]
    mag = jnp.exp(lr * step)
    ar, ai = mag * jnp.cos(li * step), mag * jnp.sin(li * step)
    nr, ni = ar - 1.0, ai
    den = lr * lr + li * li
    cr, ci = (nr * lr + ni * li) / den, (ni * lr - nr * li) / den
    br, bi = f(b_re), f(b_im)
    bbr = cr[..., None] * br - ci[..., None] * bi
    bbi = cr[..., None] * bi + ci[..., None] * br
    gpb = LANES // SSM_GROUP
    eye = jnp.eye(gpb, dtype=F32)

    def in_map(t):
        t = t.reshape(2, SSM_LANE_GROUPS, gpb, SSM_STATE, SSM_GROUP)
        return jnp.einsum('dbgpc,gh->dbgchp', t, eye).reshape(2, SSM_LANE_GROUPS, LANES, SSM_GB_STATES)

    def out_map(t):
        t = t.reshape(2, SSM_LANE_GROUPS, gpb, SSM_GROUP, SSM_STATE)
        return jnp.einsum('dbgcp,gh->dbgphc', t, eye).reshape(2, SSM_LANE_GROUPS, SSM_GB_STATES, LANES)

    bw = jnp.concatenate([in_map(bbr), in_map(bbi)], axis=-1).astype(BF16)
    cw = jnp.concatenate([out_map(f(c_re)), -out_map(f(c_im))], axis=-2).astype(BF16)
    a = jnp.stack([ar.reshape(2, SSM_LANE_GROUPS, SSM_GB_STATES),
                   ai.reshape(2, SSM_LANE_GROUPS, SSM_GB_STATES)], axis=2)
    return a, bw, cw


def _ssm_kernel(u_ref, a_ref, bw_ref, cw_ref, y_ref, st_ref, bu_ref, *, ts, bsz):
    d = pl.program_id(0)
    ns = SSM_GB_STATES

    @pl.when(pl.program_id(1) == 0)
    def _():
        st_ref[...] = jnp.zeros_like(st_ref)

    for gb in range(SSM_LANE_GROUPS):
        ub = u_ref[:, gb * LANES:(gb + 1) * LANES].astype(BF16)
        bu_ref[...] = jnp.dot(ub, bw_ref[0, gb], preferred_element_type=F32)
        ar = jnp.broadcast_to(a_ref[0, gb, 0:1, :], (bsz, ns))
        ai = jnp.broadcast_to(a_ref[0, gb, 1:2, :], (bsz, ns))

        def step(j, carry, ar=ar, ai=ai):
            xr, xi = carry
            tl = jnp.where(d == 0, j, ts - 1 - j)
            r = pl.multiple_of(tl * bsz, bsz)
            nr = ar * xr - ai * xi + bu_ref[pl.ds(r, bsz), :ns]
            ni = ar * xi + ai * xr + bu_ref[pl.ds(r, bsz), ns:]
            bu_ref[pl.ds(r, bsz), :ns] = nr
            bu_ref[pl.ds(r, bsz), ns:] = ni
            return nr, ni

        xr, xi = lax.fori_loop(0, ts, step, (st_ref[gb, :, :ns], st_ref[gb, :, ns:]))
        st_ref[gb, :, :ns] = xr
        st_ref[gb, :, ns:] = xi
        y_ref[0, :, gb * LANES:(gb + 1) * LANES] = jnp.dot(
            bu_ref[...].astype(BF16), cw_ref[0, gb], preferred_element_type=F32)


def _ssm_scan(u_tm, a, bw, cw, *, bsz, ts=64):
    rows, width = u_tm.shape
    seq = rows // bsz
    ts = min(ts, seq)
    nt = seq // ts
    tblk = lambda d, i: jnp.where(d == 0, i, nt - 1 - i)
    return pl.pallas_call(
        functools.partial(_ssm_kernel, ts=ts, bsz=bsz),
        grid=(2, nt),
        in_specs=[
            pl.BlockSpec((ts * bsz, width), lambda d, i: (tblk(d, i), 0)),
            pl.BlockSpec((1,) + a.shape[1:], lambda d, i: (d, 0, 0, 0)),
            pl.BlockSpec((1,) + bw.shape[1:], lambda d, i: (d, 0, 0, 0)),
            pl.BlockSpec((1,) + cw.shape[1:], lambda d, i: (d, 0, 0, 0)),
        ],
        out_specs=pl.BlockSpec((1, ts * bsz, width), lambda d, i: (d, tblk(d, i), 0)),
        out_shape=jax.ShapeDtypeStruct((2, rows, width), F32),
        scratch_shapes=[pltpu.VMEM((SSM_LANE_GROUPS, bsz, 2 * SSM_GB_STATES), F32),
                        pltpu.VMEM((ts * bsz, 2 * SSM_GB_STATES), F32)],
        compiler_params=_cparams(("arbitrary", "arbitrary"), VMEM_LIMIT),
        name="ssm_scan",
    )(u_tm, a, bw, cw)


def _ssm_post_kernel(y_ref, u_ref, d_ref, wg_ref, g_ref, wo_ref, z_ref):
    y = _gelu(y_ref[0] + y_ref[1] + d_ref[...] * u_ref[...]).astype(BF16)
    ab = jnp.dot(y, wg_ref[...], preferred_element_type=F32)
    ssm = ab[:, :SSM_WIDTH] * jax.nn.sigmoid(ab[:, SSM_WIDTH:])
    n = _rms(ssm, g_ref[...]).astype(BF16)
    z_ref[...] = jnp.dot(n, wo_ref[...], preferred_element_type=F32)


def _ssm_post(y, u_tm, d_skip, w_glu_bf16, g_ssm, w_out_ssm_bf16, *, tr=512):
    rows, width = u_tm.shape
    tr = min(tr, rows)
    dm = w_out_ssm_bf16.shape[1]
    full = lambda a: pl.BlockSpec(a.shape, lambda i: (0,) * a.ndim)
    d2, g2 = d_skip.reshape(1, width), g_ssm.reshape(1, width)
    return pl.pallas_call(
        _ssm_post_kernel,
        grid=(rows // tr,),
        in_specs=[pl.BlockSpec((2, tr, width), lambda i: (0, i, 0)),
                  pl.BlockSpec((tr, width), lambda i: (i, 0)),
                  full(d2), full(w_glu_bf16), full(g2), full(w_out_ssm_bf16)],
        out_specs=pl.BlockSpec((tr, dm), lambda i: (i, 0)),
        out_shape=jax.ShapeDtypeStruct((rows, dm), F32),
        compiler_params=_cparams(("parallel",), VMEM_LIMIT),
        name="ssm_post",
    )(y, u_tm, d2, w_glu_bf16, g2, w_out_ssm_bf16)


def _mix_out_kernel(x_ref, z_ref, a_ref, ga_ref, wo_ref, gf_ref, wq_ref, xn_ref, h_ref, q_ref):
    n = _rms(a_ref[0], ga_ref[...]).astype(BF16)
    xn = x_ref[0] + z_ref[...] + jnp.dot(n, wo_ref[...], preferred_element_type=F32)
    xn_ref[0] = xn
    h = _rms(xn, gf_ref[...])
    for c in range(SUBLANES):
        h_ref[pl.ds(c, h.shape[0], stride=SUBLANES), :] = h[:, c * LANES:(c + 1) * LANES]
    qp = jnp.dot(h.astype(BF16), wq_ref[...], preferred_element_type=F32)
    for hd in range(PEER_HEADS):
        q_ref[hd] = qp[:, hd * LANES:(hd + 1) * LANES]


def _mix_out(x, z_tm, attn, g_attn, w_out_attn_bf16, norm_ffn, w_query, *, ts=256):
    bsz, seq, dm = x.shape
    ts = min(ts, seq)
    ns = seq // ts
    row = pl.BlockSpec((1, ts, dm), lambda b, i: (b, i, 0))
    half = pl.BlockSpec((1, ts, ATTN_WIDTH), lambda b, i: (b, i, 0))
    full = lambda a: pl.BlockSpec(a.shape, lambda b, i: (0,) * a.ndim)
    ga, gf = g_attn.reshape(1, ATTN_WIDTH), norm_ffn.reshape(1, dm)
    qdim = w_query.shape[1] // PEER_HEADS
    return pl.pallas_call(
        _mix_out_kernel,
        grid=(bsz, ns),
        in_specs=[row, pl.BlockSpec((ts, dm), lambda b, i: (i, b)), half,
                  full(ga), full(w_out_attn_bf16), full(gf), full(w_query)],
        out_specs=[row, pl.BlockSpec((ts * SUBLANES, LANES), lambda b, i: (b * ns + i, 0)),
                   pl.BlockSpec((PEER_HEADS, ts, qdim), lambda b, i: (0, b * ns + i, 0))],
        out_shape=[jax.ShapeDtypeStruct((bsz, seq, dm), F32),
                   jax.ShapeDtypeStruct((bsz * seq * SUBLANES, LANES), F32),

                   jax.ShapeDtypeStruct((PEER_HEADS, bsz * seq, qdim), F32)],
        compiler_params=_cparams(("parallel", "arbitrary"), VMEM_LIMIT),
        name="mix_out",
    )(x, z_tm, attn, ga, w_out_attn_bf16, gf, w_query)


TOPK_TOKENS = SUBLANES * LANES
KEY_PITCH = PEER_KEYS + 4
_CANDIDATES = tuple((a, b) for a in range(PEER_TOPK) for b in range(PEER_TOPK) if (a + 1) * (b + 1) <= PEER_TOPK)


def _tree(op, xs):
    xs = list(xs)
    while len(xs) > 1:
        xs = [op(xs[i], xs[i + 1]) if i + 1 < len(xs) else xs[i] for i in range(0, len(xs), 2)]
    return xs[0]


def _extract16(problems):
    big = jnp.int32(2 ** 30)
    ninf = jnp.float32(-jnp.inf)

    def step(r, ms):
        nxt = []
        for p, m in zip(problems, ms):
            s_ref, order = p["s"], p["order"]
            n = len(order)
            am = _tree(jnp.minimum, [jnp.where(s_ref[k] == m, order[k], big) for k in range(n)])
            news, pays = [], []
            for k in range(n):
                hit = am == order[k]
                nk = jnp.where(hit, ninf, s_ref[k])
                s_ref[k] = nk
                news.append(nk)
                if p.get("pay") is not None:
                    pays.append(jnp.where(hit, p["pay"][k], -1))
            p["vals"][r] = m
            p["picks"][r] = _tree(jnp.maximum, pays) if pays else am
            nxt.append(_tree(jnp.maximum, news))
        return tuple(nxt)

    init = tuple(_tree(jnp.maximum, [p["s"][k] for k in range(len(p["order"]))]) for p in problems)
    lax.fori_loop(0, PEER_TOPK, step, init)


def _peer_topk_kernel(q_ref, k_ref, idx_ref, gate_ref,
                      slab_ref, s1_ref, s2_ref, t1_ref, i1_ref, t2_ref, i2_ref, cand_ref, pay_ref, ts_ref, ex_ref):
    dn = (((1,), (1,)), ((), ()))
    keys = tuple(range(PEER_KEYS))

    def head(h, carry):
        for w, s_ref in ((0, s1_ref), (1, s2_ref)):
            for j in range(SUBLANES):
                slab_ref[j * KEY_PITCH:j * KEY_PITCH + PEER_KEYS, :] = lax.dot_general(
                    k_ref[w, h], q_ref[h, j * LANES:(j + 1) * LANES, :].astype(BF16), dn,
                    preferred_element_type=F32)
            for k in range(PEER_KEYS):
                s_ref[k] = slab_ref[pl.ds(k, SUBLANES, stride=KEY_PITCH), :]
        _extract16([dict(s=s1_ref, order=keys, vals=t1_ref, picks=i1_ref),
                    dict(s=s2_ref, order=keys, vals=t2_ref, picks=i2_ref)])
        for c, (a, b) in enumerate(_CANDIDATES):
            cand_ref[c] = t1_ref[a] + t2_ref[b]
            pay_ref[c] = i1_ref[a] * PEER_KEYS + i2_ref[b]
        _extract16([dict(s=cand_ref, order=tuple(a * PEER_TOPK + b for a, b in _CANDIDATES), pay=pay_ref,
                         vals=ts_ref, picks=ex_ref)])
        top_s = ts_ref[...]
        e = jnp.exp(top_s - jnp.max(top_s, axis=0, keepdims=True))
        gate_ref[0, h] = e / jnp.sum(e, axis=0, keepdims=True)
        idx_ref[0, h] = ex_ref[...] * ROWS_PER_EXPERT
        return carry

    lax.fori_loop(0, PEER_HEADS, head, 0)


def _peer_topk(qp, keys_pad):
    n_tok = qp.shape[1]
    tt = TOPK_TOKENS
    assert n_tok % tt == 0
    shp = (n_tok // tt, PEER_HEADS, PEER_TOPK, SUBLANES, LANES)
    out = pl.BlockSpec((1,) + shp[1:], lambda i: (i, 0, 0, 0, 0))
    vregs = lambda n, dt: pltpu.VMEM((n, SUBLANES, LANES), dt)
    idx, gate = pl.pallas_call(
        _peer_topk_kernel,
        grid=(n_tok // tt,),
        in_specs=[pl.BlockSpec((PEER_HEADS, tt, qp.shape[2]), lambda i: (0, i, 0)),
                  pl.BlockSpec(keys_pad.shape, lambda i: (0, 0, 0, 0))],
        out_specs=[out, out],
        out_shape=[jax.ShapeDtypeStruct(shp, jnp.int32), jax.ShapeDtypeStruct(shp, F32)],
        scratch_shapes=[pltpu.VMEM((SUBLANES * KEY_PITCH, LANES), F32),
                        vregs(PEER_KEYS, F32), vregs(PEER_KEYS, F32),
                        vregs(PEER_TOPK, F32), vregs(PEER_TOPK, jnp.int32),
                        vregs(PEER_TOPK, F32), vregs(PEER_TOPK, jnp.int32),
                        vregs(len(_CANDIDATES), F32), vregs(len(_CANDIDATES), jnp.int32),
                        vregs(PEER_TOPK, F32), vregs(PEER_TOPK, jnp.int32)],
        compiler_params=_cparams(("parallel",), VMEM_LIMIT),
        name="peer_topk",
    )(qp, keys_pad)
    to_tok = lambda a: jnp.transpose(a, (0, 3, 4, 1, 2)).reshape(n_tok, PEER_SEL)
    return to_tok(idx), to_tok(gate)


ROWS_PER_EXPERT = 4
PAIR_CHUNK = 32
CHUNK_ROWS = PAIR_CHUNK * ROWS_PER_EXPERT
SMEM_GROUP = 8
N_CHUNKS = PEER_SEL // PAIR_CHUNK
CHUNK_SHIFT = N_CHUNKS.bit_length() - 1
assert 1 << CHUNK_SHIFT == N_CHUNKS
_GROUP_SLOT = tuple(
    2 * (2 * (0 if (j % 2) else 1) + (1 if (j // 2) in (0, 2) else 0)) + (0 if (j // 2) < 2 else 1)
    for j in range(8))
_CHUNK_SLOT = tuple(8 * (j // 8) + _GROUP_SLOT[j % 8] for j in range(PAIR_CHUNK))


def _pack_table(tbl):
    e, dm = tbl.shape
    tb = tbl.astype(BF16)
    lo = lax.bitcast_convert_type(tb[:, :dm // 2], jnp.uint16).astype(jnp.uint32)
    hi = lax.bitcast_convert_type(tb[:, dm // 2:], jnp.uint16).astype(jnp.uint32)
    w = lax.bitcast_convert_type(lo | (hi << 16), jnp.int32)
    return w.reshape(e * ROWS_PER_EXPERT, LANES)


def _unpack_words(w):
    lo = lax.bitcast_convert_type(w << 16, F32)
    hi = lax.bitcast_convert_type(w & jnp.int32(-65536), F32)
    return lo, hi


def _gather_chunk(idx_ref, tab_ref, buf_ref, c, group, slots):
    for g in range(PAIR_CHUNK // group):
        ids = idx_ref.at[pl.ds(pl.multiple_of(c * PAIR_CHUNK + g * group, group), group)]
        for i in range(group):
            e4 = pl.multiple_of(ids[i], ROWS_PER_EXPERT)
            s = slots[g * group + i] * ROWS_PER_EXPERT
            buf_ref[s:s + ROWS_PER_EXPERT, :] = tab_ref[pl.ds(e4, ROWS_PER_EXPERT), :]


def _chunk_loop(n_chunks, idx_ref, tab_ref, buf_a, buf_b, consume, init, group, slots, per_step=2, extra=None):
    last = n_chunks - 1
    gather = functools.partial(_gather_chunk, idx_ref, tab_ref, group=group, slots=slots)
    gather(buf_a, 0)
    bufs = (buf_a, buf_b)

    def body(i, carry):
        if extra is not None:
            extra(i)
        c = per_step * i
        for u in range(per_step):
            carry = consume(c + u, bufs[u % 2], carry)
            nxt = c + u + 1
            gather(bufs[(u + 1) % 2], jnp.minimum(nxt, last) if u == per_step - 1 else nxt)
        return carry

    return lax.fori_loop(0, n_chunks // per_step, body, init)


def _peer_u_kernel(idx_ref, h_ref, gate_ref, tab_ref, coef_ref, buf_a, buf_b, r_even, r_odd, d_ref, *, tt, n_tiles):
    step = pl.program_id(0)
    sub = lax.broadcasted_iota(jnp.int32, (SUBLANES, LANES), 0)
    low = sub < ROWS_PER_EXPERT
    m_a = ((sub % 4) >= 2)[None]
    m_b = ((sub % 2) == 1)[None]
    nv = PAIR_CHUNK // 2

    def pair_partials(r_ref, c, buf_ref, carry):
        t = c >> CHUNK_SHIFT
        h8 = h_ref[pl.ds(pl.multiple_of(t * SUBLANES, SUBLANES), SUBLANES), :]
        sw = pltpu.roll(h8, ROWS_PER_EXPERT, axis=0)
        ha = jnp.where(low, h8, sw)[None]
        hb = jnp.where(low, sw, h8)[None]
        lo, hi = _unpack_words(buf_ref[...])
        x = lo.reshape(nv, SUBLANES, LANES) * ha + hi.reshape(nv, SUBLANES, LANES) * hb
        x = x.reshape(nv // 2, 2, SUBLANES, LANES)
        xe, xo = x[:, 0], x[:, 1]
        a = xe + pltpu.roll(xe, 2, axis=1)
        b = xo + pltpu.roll(xo, 6, axis=1)
        m = jnp.where(m_a, a, b).reshape(nv // 4, 2, SUBLANES, LANES)
        me, mo = m[:, 0], m[:, 1]
        a2 = me + pltpu.roll(me, 1, axis=1)
        b2 = mo + pltpu.roll(mo, 7, axis=1)
        r_ref[pl.ds(pl.multiple_of(c * PAIR_CHUNK, PAIR_CHUNK), PAIR_CHUNK), :] = (
            jnp.where(m_b, a2, b2).reshape(PAIR_CHUNK, LANES))
        return carry

    def token_sum(r_prev, t):
        r_t = r_prev[pl.ds(pl.multiple_of(t * PEER_SEL, PEER_SEL), PEER_SEL), :]
        d_ref[pl.ds(t, 1), :] = jnp.sum(r_t.T, axis=0, keepdims=True)

    def loop(r_cur, **kw):
        _chunk_loop(tt * N_CHUNKS, idx_ref, tab_ref, buf_a, buf_b, functools.partial(pair_partials, r_cur), 0,
                    SMEM_GROUP, _CHUNK_SLOT, **kw)

    @pl.when(step == 0)
    def _():
        loop(r_even)

    for parity, (r_cur, r_prev) in enumerate(((r_even, r_odd), (r_odd, r_even))):
        @pl.when(jnp.logical_and(jnp.logical_and(step > 0, step < n_tiles), step % 2 == parity))
        def _(r_cur=r_cur, r_prev=r_prev):
            loop(r_cur, per_step=N_CHUNKS, extra=functools.partial(token_sum, r_prev))

    @pl.when(step == n_tiles)
    def _():
        def only_sums(t, carry):
            token_sum(r_even if n_tiles % 2 else r_odd, t)
            return carry
        lax.fori_loop(0, tt, only_sums, 0)

    @pl.when(step > 0)
    def _():
        coef_ref[...] = gate_ref[...] * _gelu(d_ref[...])


def _peer_u(idx4, h8, gates, tab, *, tt=128):
    n_tok = gates.shape[0]
    tt = min(tt, n_tok)
    n_tiles = n_tok // tt
    cur = lambda i: jnp.minimum(i, n_tiles - 1)
    done = lambda i: jnp.maximum(i - 1, 0)
    return pl.pallas_call(
        functools.partial(_peer_u_kernel, tt=tt, n_tiles=n_tiles),
        grid=(n_tiles + 1,),
        in_specs=[pl.BlockSpec((tt * PEER_SEL,), lambda i: (cur(i),), memory_space=pltpu.SMEM),
                  pl.BlockSpec((tt * SUBLANES, LANES), lambda i: (cur(i), 0)),
                  pl.BlockSpec((tt, PEER_SEL), lambda i: (done(i), 0)),
                  pl.BlockSpec(tab.shape, lambda i: (0, 0), pipeline_mode=pl.Buffered(1))],
        out_specs=pl.BlockSpec((tt, PEER_SEL), lambda i: (done(i), 0)),
        out_shape=jax.ShapeDtypeStruct((n_tok, PEER_SEL), F32),
        scratch_shapes=[pltpu.VMEM((CHUNK_ROWS, LANES), jnp.int32),
                        pltpu.VMEM((CHUNK_ROWS, LANES), jnp.int32),
                        pltpu.VMEM((tt * PEER_SEL, LANES), F32),
                        pltpu.VMEM((tt * PEER_SEL, LANES), F32),
                        pltpu.VMEM((tt, PEER_SEL), F32)],
        compiler_params=_cparams(("arbitrary",), PEER_U_VMEM_LIMIT),
        name="peer_u",
    )(idx4, h8, gates, tab)


def _peer_v_kernel(idx_ref, coef_ref, tab_ref, o_ref, buf_a, buf_b, *, tt):
    sub = lax.broadcasted_iota(jnp.int32, (SUBLANES, LANES), 0)
    low = sub < ROWS_PER_EXPERT
    nv = PAIR_CHUNK // 2

    def accumulate(c, buf_ref, acc):
        acc_lo, acc_hi = acc
        lo, hi = _unpack_words(buf_ref[...])
        lo = lo.reshape(nv, SUBLANES, LANES)
        hi = hi.reshape(nv, SUBLANES, LANES)
        cs = coef_ref.at[pl.ds(pl.multiple_of(c * nv, nv), nv)]
        for v in range(nv):
            c_lo, c_hi = _unpack_words(jnp.full((SUBLANES, LANES), cs[v], jnp.int32))
            cv = jnp.where(low, c_lo, c_hi)
            acc_lo = acc_lo + cv * lo[v]
            acc_hi = acc_hi + cv * hi[v]
        lo4 = acc_lo + pltpu.roll(acc_lo, ROWS_PER_EXPERT, axis=0)
        hi4 = acc_hi + pltpu.roll(acc_hi, ROWS_PER_EXPERT, axis=0)
        o_ref[c >> CHUNK_SHIFT] = jnp.where(low, lo4, hi4)
        last = (c & (N_CHUNKS - 1)) == N_CHUNKS - 1
        return jnp.where(last, 0.0, acc_lo), jnp.where(last, 0.0, acc_hi)

    z = jnp.zeros((SUBLANES, LANES), F32)
    _chunk_loop(tt * N_CHUNKS, idx_ref, tab_ref, buf_a, buf_b, accumulate, (z, z), PAIR_CHUNK,
                tuple(range(PAIR_CHUNK)))


def _pack_coefs(coef):
    n_tok = coef.shape[0]
    bits = lax.bitcast_convert_type(coef.astype(BF16), jnp.uint16).astype(jnp.uint32)
    bits = bits.reshape(n_tok, PEER_SEL // 2, 2)
    w = bits[:, :, 0] | (bits[:, :, 1] << 16)
    return lax.bitcast_convert_type(w, jnp.int32).reshape(n_tok * (PEER_SEL // 2))


def _peer_v(idx4, coef_words, tab, *, tt=128):
    n_tok = idx4.shape[0] // PEER_SEL
    tt = min(tt, n_tok)
    smem = pl.BlockSpec((tt * PEER_SEL,), lambda i: (i,), memory_space=pltpu.SMEM)
    return pl.pallas_call(
        functools.partial(_peer_v_kernel, tt=tt),
        grid=(n_tok // tt,),
        in_specs=[smem, pl.BlockSpec((tt * PEER_SEL // 2,), lambda i: (i,), memory_space=pltpu.SMEM),
                  pl.BlockSpec(tab.shape, lambda i: (0, 0), pipeline_mode=pl.Buffered(1))],
        out_specs=pl.BlockSpec((tt, SUBLANES, LANES), lambda i: (i, 0, 0)),
        out_shape=jax.ShapeDtypeStruct((n_tok, SUBLANES, LANES), F32),
        scratch_shapes=[pltpu.VMEM((CHUNK_ROWS, LANES), jnp.int32),
                        pltpu.VMEM((CHUNK_ROWS, LANES), jnp.int32)],
        compiler_params=_cparams(("arbitrary",), VMEM_LIMIT),
        name="peer_v",
    )(idx4, coef_words, tab)


def _peer(h8, qp, sub_keys, expert_u, expert_v):
    n_tok = qp.shape[1]
    half = sub_keys.shape[-1]
    z = jnp.zeros_like(sub_keys[0])
    keys_pad = jnp.stack([jnp.concatenate([sub_keys[0], z], axis=-1),
                          jnp.concatenate([z, sub_keys[1]], axis=-1)]).astype(BF16)
    assert keys_pad.shape[-1] == 2 * half == qp.shape[-1]
    idx4, gates = _peer_topk(qp, keys_pad)
    idx4 = idx4.reshape(n_tok * PEER_SEL)
    coef = _peer_u(idx4, h8, gates, _pack_table(expert_u))
    out = _peer_v(idx4, _pack_coefs(coef), _pack_table(expert_v))
    return out.reshape(n_tok * SUBLANES, LANES)


def _final_kernel(x_ref, p_ref, g_ref, o_ref):
    o_ref[...] = _rms(x_ref[...] + _rows_from_chunks(p_ref, x_ref.shape[0]), g_ref[...])


def _final_norm(x, p, g, *, tr=1024):
    rows, dm = x.shape
    tr = min(tr, rows)
    blk = pl.BlockSpec((tr, dm), lambda i: (i, 0))
    return pl.pallas_call(
        _final_kernel,
        grid=(rows // tr,),
        in_specs=[blk, pl.BlockSpec((tr * SUBLANES, LANES), lambda i: (i, 0)),
                  pl.BlockSpec((1, dm), lambda i: (0, 0))],
        out_specs=blk,
        out_shape=jax.ShapeDtypeStruct((rows, dm), F32),
        compiler_params=_cparams(("parallel",), VMEM_LIMIT),
        name="final_norm",
    )(x, p, g.reshape(1, dm))


def kernel(x, w_in, w_out, rel_bias, g_attn, g_ssm, norm_mix, norm_ffn, lam_re, lam_im, log_step, b_re, b_im, c_re, c_im, d_skip, w_glu, w_query, sub_keys, expert_u, expert_v, norm_final):
    bsz, seq, dm = x.shape
    depth = w_in.shape[0]
    prev = None
    biases = [_attn_bias_tables(rel_bias, d) for _, d in DILATED_PATTERNS]
    for l in range(depth):
        x, q, k, v, u = _in_proj(x, prev, norm_mix[l], w_in[l].astype(BF16))
        attn = _attention(q, k, v, biases)
        u_tm = u.reshape(seq * bsz, SSM_WIDTH)
        a, bw, cw = _ssm_params(lam_re[l], lam_im[l], log_step[l], b_re[l], b_im[l], c_re[l], c_im[l])
        y = _ssm_scan(u_tm, a, bw, cw, bsz=bsz)
        wo = w_out[l].astype(BF16)
        z = _ssm_post(y, u_tm, d_skip[l], w_glu[l].astype(BF16), g_ssm[l], wo[ATTN_WIDTH:])
        x, h, qp = _mix_out(x, z.reshape(seq, bsz * dm), attn,
                            g_attn[l], wo[:ATTN_WIDTH], norm_ffn[l], w_query[l].astype(BF16))
        prev = _peer(h, qp, sub_keys[l], expert_u[l], expert_v[l])
    out = _final_norm(x.reshape(bsz * seq, dm), prev, norm_final)
    return out.reshape(bsz, seq, dm)
```

```python
import functools
import math

import numpy as np
import jax
import jax.numpy as jnp
from jax import lax
from jax.experimental import pallas as pl
from jax.experimental.pallas import tpu as pltpu

F32 = jnp.float32
BF16 = jnp.bfloat16

EPS = 1e-6
NEG_INF = -1e30
HEAD_DIM = 64
ATTN_WIDTH = 512
SSM_WIDTH = 512
SSM_GROUP = 16
SSM_STATE = 64
DILATED_PATTERNS = ((128, 1), (512, 4), (2048, 16))
REL_BUCKETS = 32
REL_MAX_DISTANCE = 1024
PEER_HEADS = 8
PEER_KEYS = 128
PEER_TOPK = 16
PEER_SEL = PEER_HEADS * PEER_TOPK

LANES = 128
SUBLANES = 8
QBLK = 128
KWIN = 256
BAND = 64
VMEM_LIMIT = 52 * 1024 * 1024
PEER_U_VMEM_LIMIT = 58 * 1024 * 1024


def _cparams(sem, vmem=None):
    return pltpu.CompilerParams(dimension_semantics=sem, vmem_limit_bytes=vmem)


def _rms(x, g):
    return x * lax.rsqrt(jnp.mean(x * x, axis=-1, keepdims=True) + EPS) * g


def _gelu(x):
    return 0.5 * x * (1.0 + lax.erf(x * (1.0 / math.sqrt(2.0))))


def _rows_from_chunks(p_ref, n_rows):
    return jnp.concatenate([p_ref[pl.ds(c, n_rows, stride=SUBLANES), :] for c in range(SUBLANES)], axis=-1)


def _in_proj_kernel(*refs, has_prev):
    if has_prev:
        x_ref, p_ref, g_ref, w_ref, xo_ref, q_ref, k_ref, v_ref, u_ref = refs
        x = x_ref[0] + _rows_from_chunks(p_ref, x_ref.shape[1])
    else:
        x_ref, g_ref, w_ref, xo_ref, q_ref, k_ref, v_ref, u_ref = refs
        x = x_ref[0]
    xo_ref[0] = x
    h = _rms(x, g_ref[...]).astype(BF16)
    proj = jnp.dot(h, w_ref[...], preferred_element_type=F32)
    a = ATTN_WIDTH
    q_ref[0] = proj[:, :a] * (HEAD_DIM ** -0.5)
    k_ref[0] = proj[:, a:2 * a]
    v_ref[0] = proj[:, 2 * a:3 * a]
    u_ref[...] = proj[:, 3 * a:]


def _in_proj(x, prev, g, w_bf16, *, ts=512):
    bsz, seq, dm = x.shape
    ts = min(ts, seq)
    row = pl.BlockSpec((1, ts, dm), lambda b, i: (b, i, 0))
    qkv = pl.BlockSpec((1, ts, ATTN_WIDTH), lambda b, i: (b, i, 0))
    ns = seq // ts
    chunks = pl.BlockSpec((ts * SUBLANES, LANES), lambda b, i: (b * ns + i, 0))
    ins = [x] + ([prev] if prev is not None else []) + [g.reshape(1, dm), w_bf16]
    in_specs = [row] + ([chunks] if prev is not None else []) + [
        pl.BlockSpec((1, dm), lambda b, i: (0, 0)),
        pl.BlockSpec(w_bf16.shape, lambda b, i: (0, 0)),
    ]
    return pl.pallas_call(
        functools.partial(_in_proj_kernel, has_prev=prev is not None),
        grid=(bsz, seq // ts),
        in_specs=in_specs,
        out_specs=[row, qkv, qkv, qkv, pl.BlockSpec((ts, SSM_WIDTH), lambda b, i: (i, b))],
        out_shape=[
            jax.ShapeDtypeStruct((bsz, seq, dm), F32),
            jax.ShapeDtypeStruct((bsz, seq, ATTN_WIDTH), F32),
            jax.ShapeDtypeStruct((bsz, seq, ATTN_WIDTH), F32),
            jax.ShapeDtypeStruct((bsz, seq, ATTN_WIDTH), F32),
            jax.ShapeDtypeStruct((seq, bsz * SSM_WIDTH), F32),
        ],
        compiler_params=_cparams(("parallel", "arbitrary"), VMEM_LIMIT),
        name="in_proj",
    )(*ins)


def _t5_buckets(rel):
    half = REL_BUCKETS // 2
    max_exact = half // 2
    n = np.abs(rel)
    large = max_exact + (np.log(np.maximum(n, 1) / max_exact)
                         / np.log(REL_MAX_DISTANCE / max_exact) * (half - max_exact)).astype(np.int32)
    large = np.minimum(large, half - 1)
    return (np.where(rel > 0, half, 0) + np.where(n < max_exact, n, large)).astype(np.int32)


def _attn_bias_tables(rel_bias, dilation):
    ql = np.arange(QBLK)[:, None]
    kl = np.arange(KWIN)[None, :]
    delta = np.stack([kl + off - ql for off in (0, -BAND, -2 * BAND)])
    buckets = np.where(np.abs(delta) <= BAND, _t5_buckets(delta * dilation), -1)
    rb = rel_bias.astype(F32).T
    bk = jnp.asarray(buckets, jnp.int32)[None]
    tab = jnp.full((rb.shape[0],) + buckets.shape, NEG_INF, F32)
    for b in range(REL_BUCKETS):
        tab = jnp.where(bk == b, rb[:, b][:, None, None, None], tab)
    return tab


BLOCKS_PER_STEP = 4


def _attn_kernel(q_ref, k_ref, v_ref, *rest, seq):
    bias_refs, (o_ref, acc_ref, m_ref, z_ref) = rest[:len(DILATED_PATTERNS)], rest[len(DILATED_PATTERNS):]
    lane = lax.broadcasted_iota(jnp.int32, (QBLK, LANES), 1)
    is_h0 = lane < HEAD_DIM
    dn = (((1,), (1,)), ((), ()))
    nsteps = seq // QBLK

    def rows(start, size, d):
        return pl.ds(start, size) if d == 1 else pl.ds(start, size, stride=d)

    def block(n, d, bias_ref, first, last):
        length = seq // d
        nblk = length // QBLK
        r, i = n >> (nblk.bit_length() - 1), n & (nblk - 1)
        s = i * QBLK
        ks = jnp.clip(s - BAND, 0, length - KWIN)
        var = jnp.where(i == 0, 0, jnp.where(i == nblk - 1, 2, 1))
        q_rows = rows(r + d * s, QBLK, d)
        k_rows = rows(r + d * ks, KWIN, d)
        qb = q_ref[q_rows, :].astype(BF16)
        kb = k_ref[k_rows, :].astype(BF16)
        vb = v_ref[k_rows, :].astype(BF16)
        outs, ms, zs = [], [], []
        for h in range(2):
            keep = is_h0 if h == 0 else jnp.logical_not(is_h0)
            qh = jnp.where(keep, qb, jnp.zeros_like(qb))
            logits = lax.dot_general(qh, kb, dn, preferred_element_type=F32) + bias_ref[h, var]
            m = jnp.max(logits, axis=-1, keepdims=True)
            p = jnp.exp(logits - m)
            outs.append(jnp.dot(p.astype(BF16), vb, preferred_element_type=F32))
            ms.append(jnp.broadcast_to(m, (QBLK, LANES)))
            zs.append(jnp.broadcast_to(jnp.sum(p, axis=-1, keepdims=True), (QBLK, LANES)))
        o = jnp.where(is_h0, outs[0], outs[1])
        m = jnp.where(is_h0, ms[0], ms[1])
        z = jnp.where(is_h0, zs[0], zs[1])
        if not first:
            m_old = m_ref[q_rows, :]
            m_new = jnp.maximum(m_old, m)
            a, b = jnp.exp(m_old - m_new), jnp.exp(m - m_new)
            o = acc_ref[q_rows, :] * a + o * b
            z = z_ref[q_rows, :] * a + z * b
            m = m_new
        if last:
            o_ref[q_rows, :] = o / z
        else:
            acc_ref[q_rows, :] = o
            m_ref[q_rows, :] = m
            z_ref[q_rows, :] = z

    for p, ((_, d), bias_ref) in enumerate(zip(DILATED_PATTERNS, bias_refs)):
        def step(g, carry, d=d, bias_ref=bias_ref, p=p):
            for j in range(BLOCKS_PER_STEP):
                block(g * BLOCKS_PER_STEP + j, d, bias_ref, p == 0, p == len(DILATED_PATTERNS) - 1)
            return carry
        lax.fori_loop(0, nsteps // BLOCKS_PER_STEP, step, 0)


def _attention(q, k, v, biases):
    bsz, seq, width = q.shape
    for _, d in DILATED_PATTERNS:
        length = seq // d
        assert length >= KWIN and length % QBLK == 0 and (length // QBLK) & (length // QBLK - 1) == 0
    assert (seq // QBLK) % BLOCKS_PER_STEP == 0
    blk = pl.BlockSpec((None, seq, LANES), lambda b, c: (b, 0, c))
    bias_spec = pl.BlockSpec((2, 3, QBLK, KWIN), lambda b, c: (c, 0, 0, 0))
    return pl.pallas_call(
        functools.partial(_attn_kernel, seq=seq),
        grid=(bsz, width // LANES),
        in_specs=[blk, blk, blk] + [bias_spec] * len(biases),
        out_specs=blk,
        out_shape=jax.ShapeDtypeStruct((bsz, seq, width), F32),
        scratch_shapes=[pltpu.VMEM((seq, LANES), F32)] * 3,
        compiler_params=_cparams(("parallel", "arbitrary"), VMEM_LIMIT),
        name="attention",
    )(q, k, v, *biases)


SSM_LANE_GROUPS = SSM_WIDTH // LANES
SSM_GB_STATES = (LANES // SSM_GROUP) * SSM_STATE


def _ssm_params(lam_re, lam_im, log_step, b_re, b_im, c_re, c_im):
    f = lambda t: t.astype(F32)
    lr, li = f(lam_re), f(lam_im)
    step = jnp.exp(f(log_step))[..., None]
    mag = jnp.exp(lr * step)
    ar, ai = mag * jnp.cos(li * step), mag * jnp.sin(li * step)
    nr, ni = ar - 1.0, ai
    den = lr * lr + li * li
    cr, ci = (nr * lr + ni * li) / den, (ni * lr - nr * li) / den
    br, bi = f(b_re), f(b_im)
    bbr = cr[..., None] * br - ci[..., None] * bi
    bbi = cr[..., None] * bi + ci[..., None] * br
    gpb = LANES // SSM_GROUP
    eye = jnp.eye(gpb, dtype=F32)

    def in_map(t):
        t = t.reshape(2, SSM_LANE_GROUPS, gpb, SSM_STATE, SSM_GROUP)
        return jnp.einsum('dbgpc,gh->dbgchp', t, eye).reshape(2, SSM_LANE_GROUPS, LANES, SSM_GB_STATES)

    def out_map(t):
        t = t.reshape(2, SSM_LANE_GROUPS, gpb, SSM_GROUP, SSM_STATE)
        return jnp.einsum('dbgcp,gh->dbgphc', t, eye).reshape(2, SSM_LANE_GROUPS, SSM_GB_STATES, LANES)

    bw = jnp.concatenate([in_map(bbr), in_map(bbi)], axis=-1).astype(BF16)
    cw = jnp.concatenate([out_map(f(c_re)), -out_map(f(c_im))], axis=-2).astype(BF16)
    a = jnp.stack([ar.reshape(2, SSM_LANE_GROUPS, SSM_GB_STATES),
                   ai.reshape(2, SSM_LANE_GROUPS, SSM_GB_STATES)], axis=2)
    return a, bw, cw


def _ssm_kernel(u_ref, a_ref, bw_ref, cw_ref, y_ref, st_ref, bu_ref, *, ts, bsz):
    d = pl.program_id(0)
    ns = SSM_GB_STATES

    @pl.when(pl.program_id(1) == 0)
    def _():
        st_ref[...] = jnp.zeros_like(st_ref)

    for gb in range(SSM_LANE_GROUPS):
        ub = u_ref[:, gb * LANES:(gb + 1) * LANES].astype(BF16)
        bu_ref[...] = jnp.dot(ub, bw_ref[0, gb], preferred_element_type=F32)
        ar = jnp.broadcast_to(a_ref[0, gb, 0:1, :], (bsz, ns))
        ai = jnp.broadcast_to(a_ref[0, gb, 1:2, :], (bsz, ns))

        def step(j, carry, ar=ar, ai=ai):
            xr, xi = carry
            tl = jnp.where(d == 0, j, ts - 1 - j)
            r = pl.multiple_of(tl * bsz, bsz)
            nr = ar * xr - ai * xi + bu_ref[pl.ds(r, bsz), :ns]
            ni = ar * xi + ai * xr + bu_ref[pl.ds(r, bsz), ns:]
            bu_ref[pl.ds(r, bsz), :ns] = nr
            bu_ref[pl.ds(r, bsz), ns:] = ni
            return nr, ni

        xr, xi = lax.fori_loop(0, ts, step, (st_ref[gb, :, :ns], st_ref[gb, :, ns:]))
        st_ref[gb, :, :ns] = xr
        st_ref[gb, :, ns:] = xi
        y_ref[0, :, gb * LANES:(gb + 1) * LANES] = jnp.dot(
            bu_ref[...].astype(BF16), cw_ref[0, gb], preferred_element_type=F32)


def _ssm_scan(u_tm, a, bw, cw, *, bsz, ts=64):
    rows, width = u_tm.shape
    seq = rows // bsz
    ts = min(ts, seq)
    nt = seq // ts
    tblk = lambda d, i: jnp.where(d == 0, i, nt - 1 - i)
    return pl.pallas_call(
        functools.partial(_ssm_kernel, ts=ts, bsz=bsz),
        grid=(2, nt),
        in_specs=[
            pl.BlockSpec((ts * bsz, width), lambda d, i: (tblk(d, i), 0)),
            pl.BlockSpec((1,) + a.shape[1:], lambda d, i: (d, 0, 0, 0)),
            pl.BlockSpec((1,) + bw.shape[1:], lambda d, i: (d, 0, 0, 0)),
            pl.BlockSpec((1,) + cw.shape[1:], lambda d, i: (d, 0, 0, 0)),
        ],
        out_specs=pl.BlockSpec((1, ts * bsz, width), lambda d, i: (d, tblk(d, i), 0)),
        out_shape=jax.ShapeDtypeStruct((2, rows, width), F32),
        scratch_shapes=[pltpu.VMEM((SSM_LANE_GROUPS, bsz, 2 * SSM_GB_STATES), F32),
                        pltpu.VMEM((ts * bsz, 2 * SSM_GB_STATES), F32)],
        compiler_params=_cparams(("arbitrary", "arbitrary"), VMEM_LIMIT),
        name="ssm_scan",
    )(u_tm, a, bw, cw)


def _ssm_post_kernel(y_ref, u_ref, d_ref, wg_ref, g_ref, wo_ref, z_ref):
    y = _gelu(y_ref[0] + y_ref[1] + d_ref[...] * u_ref[...]).astype(BF16)
    ab = jnp.dot(y, wg_ref[...], preferred_element_type=F32)
    ssm = ab[:, :SSM_WIDTH] * jax.nn.sigmoid(ab[:, SSM_WIDTH:])
    n = _rms(ssm, g_ref[...]).astype(BF16)
    z_ref[...] = jnp.dot(n, wo_ref[...], preferred_element_type=F32)


def _ssm_post(y, u_tm, d_skip, w_glu_bf16, g_ssm, w_out_ssm_bf16, *, tr=512):
    rows, width = u_tm.shape
    tr = min(tr, rows)
    dm = w_out_ssm_bf16.shape[1]
    full = lambda a: pl.BlockSpec(a.shape, lambda i: (0,) * a.ndim)
    d2, g2 = d_skip.reshape(1, width), g_ssm.reshape(1, width)
    return pl.pallas_call(
        _ssm_post_kernel,
        grid=(rows // tr,),
        in_specs=[pl.BlockSpec((2, tr, width), lambda i: (0, i, 0)),
                  pl.BlockSpec((tr, width), lambda i: (i, 0)),
                  full(d2), full(w_glu_bf16), full(g2), full(w_out_ssm_bf16)],
        out_specs=pl.BlockSpec((tr, dm), lambda i: (i, 0)),
        out_shape=jax.ShapeDtypeStruct((rows, dm), F32),
        compiler_params=_cparams(("parallel",), VMEM_LIMIT),
        name="ssm_post",
    )(y, u_tm, d2, w_glu_bf16, g2, w_out_ssm_bf16)


def _mix_out_kernel(x_ref, z_ref, a_ref, ga_ref, wo_ref, gf_ref, wq_ref, xn_ref, h_ref, q_ref):
    n = _rms(a_ref[0], ga_ref[...]).astype(BF16)
    xn = x_ref[0] + z_ref[...] + jnp.dot(n, wo_ref[...], preferred_element_type=F32)
    xn_ref[0] = xn
    h = _rms(xn, gf_ref[...])
    for c in range(SUBLANES):
        h_ref[pl.ds(c, h.shape[0], stride=SUBLANES), :] = h[:, c * LANES:(c + 1) * LANES]
    qp = jnp.dot(h.astype(BF16), wq_ref[...], preferred_element_type=F32)
    for hd in range(PEER_HEADS):
        q_ref[hd] = qp[:, hd * LANES:(hd + 1) * LANES]


def _mix_out(x, z_tm, attn, g_attn, w_out_attn_bf16, norm_ffn, w_query, *, ts=256):
    bsz, seq, dm = x.shape
    ts = min(ts, seq)
    ns = seq // ts
    row = pl.BlockSpec((1, ts, dm), lambda b, i: (b, i, 0))
    half = pl.BlockSpec((1, ts, ATTN_WIDTH), lambda b, i: (b, i, 0))
    full = lambda a: pl.BlockSpec(a.shape, lambda b, i: (0,) * a.ndim)
    ga, gf = g_attn.reshape(1, ATTN_WIDTH), norm_ffn.reshape(1, dm)
    qdim = w_query.shape[1] // PEER_HEADS
    return pl.pallas_call(
        _mix_out_kernel,
        grid=(bsz, ns),
        in_specs=[row, pl.BlockSpec((ts, dm), lambda b, i: (i, b)), half,
                  full(ga), full(w_out_attn_bf16), full(gf), full(w_query)],
        out_specs=[row, pl.BlockSpec((ts * SUBLANES, LANES), lambda b, i: (b * ns + i, 0)),
                   pl.BlockSpec((PEER_HEADS, ts, qdim), lambda b, i: (0, b * ns + i, 0))],
        out_shape=[jax.ShapeDtypeStruct((bsz, seq, dm), F32),
                   jax.ShapeDtypeStruct((bsz * seq * SUBLANES, LANES), F32),

                   jax.ShapeDtypeStruct((PEER_HEADS, bsz * seq, qdim), F32)],
        compiler_params=_cparams(("parallel", "arbitrary"), VMEM_LIMIT),
        name="mix_out",
    )(x, z_tm, attn, ga, w_out_attn_bf16, gf, w_query)


TOPK_TOKENS = SUBLANES * LANES
KEY_PITCH = PEER_KEYS + 4
_CANDIDATES = tuple((a, b) for a in range(PEER_TOPK) for b in range(PEER_TOPK) if (a + 1) * (b + 1) <= PEER_TOPK)


def _tree(op, xs):
    xs = list(xs)
    while len(xs) > 1:
        xs = [op(xs[i], xs[i + 1]) if i + 1 < len(xs) else xs[i] for i in range(0, len(xs), 2)]
    return xs[0]


def _extract16(problems):
    big = jnp.int32(2 ** 30)
    ninf = jnp.float32(-jnp.inf)

    def step(r, ms):
        nxt = []
        for p, m in zip(problems, ms):
            s_ref, order = p["s"], p["order"]
            n = len(order)
            am = _tree(jnp.minimum, [jnp.where(s_ref[k] == m, order[k], big) for k in range(n)])
            news, pays = [], []
            for k in range(n):
                hit = am == order[k]
                nk = jnp.where(hit, ninf, s_ref[k])
                s_ref[k] = nk
                news.append(nk)
                if p.get("pay") is not None:
                    pays.append(jnp.where(hit, p["pay"][k], -1))
            p["vals"][r] = m
            p["picks"][r] = _tree(jnp.maximum, pays) if pays else am
            nxt.append(_tree(jnp.maximum, news))
        return tuple(nxt)

    init = tuple(_tree(jnp.maximum, [p["s"][k] for k in range(len(p["order"]))]) for p in problems)
    lax.fori_loop(0, PEER_TOPK, step, init)


def _peer_topk_kernel(q_ref, k_ref, idx_ref, gate_ref,
                      slab_ref, s1_ref, s2_ref, t1_ref, i1_ref, t2_ref, i2_ref, cand_ref, pay_ref, ts_ref, ex_ref):
    dn = (((1,), (1,)), ((), ()))
    keys = tuple(range(PEER_KEYS))

    def head(h, carry):
        for w, s_ref in ((0, s1_ref), (1, s2_ref)):
            for j in range(SUBLANES):
                slab_ref[j * KEY_PITCH:j * KEY_PITCH + PEER_KEYS, :] = lax.dot_general(
                    k_ref[w, h], q_ref[h, j * LANES:(j + 1) * LANES, :].astype(BF16), dn,
                    preferred_element_type=F32)
            for k in range(PEER_KEYS):
                s_ref[k] = slab_ref[pl.ds(k, SUBLANES, stride=KEY_PITCH), :]
        _extract16([dict(s=s1_ref, order=keys, vals=t1_ref, picks=i1_ref),
                    dict(s=s2_ref, order=keys, vals=t2_ref, picks=i2_ref)])
        for c, (a, b) in enumerate(_CANDIDATES):
            cand_ref[c] = t1_ref[a] + t2_ref[b]
            pay_ref[c] = i1_ref[a] * PEER_KEYS + i2_ref[b]
        _extract16([dict(s=cand_ref, order=tuple(a * PEER_TOPK + b for a, b in _CANDIDATES), pay=pay_ref,
                         vals=ts_ref, picks=ex_ref)])
        top_s = ts_ref[...]
        e = jnp.exp(top_s - jnp.max(top_s, axis=0, keepdims=True))
        gate_ref[0, h] = e / jnp.sum(e, axis=0, keepdims=True)
        idx_ref[0, h] = ex_ref[...] * ROWS_PER_EXPERT
        return carry

    lax.fori_loop(0, PEER_HEADS, head, 0)


def _peer_topk(qp, keys_pad):
    n_tok = qp.shape[1]
    tt = TOPK_TOKENS
    assert n_tok % tt == 0
    shp = (n_tok // tt, PEER_HEADS, PEER_TOPK, SUBLANES, LANES)
    out = pl.BlockSpec((1,) + shp[1:], lambda i: (i, 0, 0, 0, 0))
    vregs = lambda n, dt: pltpu.VMEM((n, SUBLANES, LANES), dt)
    idx, gate = pl.pallas_call(
        _peer_topk_kernel,
        grid=(n_tok // tt,),
        in_specs=[pl.BlockSpec((PEER_HEADS, tt, qp.shape[2]), lambda i: (0, i, 0)),
                  pl.BlockSpec(keys_pad.shape, lambda i: (0, 0, 0, 0))],
        out_specs=[out, out],
        out_shape=[jax.ShapeDtypeStruct(shp, jnp.int32), jax.ShapeDtypeStruct(shp, F32)],
        scratch_shapes=[pltpu.VMEM((SUBLANES * KEY_PITCH, LANES), F32),
                        vregs(PEER_KEYS, F32), vregs(PEER_KEYS, F32),
                        vregs(PEER_TOPK, F32), vregs(PEER_TOPK, jnp.int32),
                        vregs(PEER_TOPK, F32), vregs(PEER_TOPK, jnp.int32),
                        vregs(len(_CANDIDATES), F32), vregs(len(_CANDIDATES), jnp.int32),
                        vregs(PEER_TOPK, F32), vregs(PEER_TOPK, jnp.int32)],
        compiler_params=_cparams(("parallel",), VMEM_LIMIT),
        name="peer_topk",
    )(qp, keys_pad)
    to_tok = lambda a: jnp.transpose(a, (0, 3, 4, 1, 2)).reshape(n_tok, PEER_SEL)
    return to_tok(idx), to_tok(gate)


ROWS_PER_EXPERT = 4
PAIR_CHUNK = 32
CHUNK_ROWS = PAIR_CHUNK * ROWS_PER_EXPERT
SMEM_GROUP = 8
N_CHUNKS = PEER_SEL // PAIR_CHUNK
CHUNK_SHIFT = N_CHUNKS.bit_length() - 1
assert 1 << CHUNK_SHIFT == N_CHUNKS
_GROUP_SLOT = tuple(
    2 * (2 * (0 if (j % 2) else 1) + (1 if (j // 2) in (0, 2) else 0)) + (0 if (j // 2) < 2 else 1)
    for j in range(8))
_CHUNK_SLOT = tuple(8 * (j // 8) + _GROUP_SLOT[j % 8] for j in range(PAIR_CHUNK))


def _pack_table(tbl):
    e, dm = tbl.shape
    tb = tbl.astype(BF16)
    lo = lax.bitcast_convert_type(tb[:, :dm // 2], jnp.uint16).astype(jnp.uint32)
    hi = lax.bitcast_convert_type(tb[:, dm // 2:], jnp.uint16).astype(jnp.uint32)
    w = lax.bitcast_convert_type(lo | (hi << 16), jnp.int32)
    return w.reshape(e * ROWS_PER_EXPERT, LANES)


def _unpack_words(w):
    lo = lax.bitcast_convert_type(w << 16, F32)
    hi = lax.bitcast_convert_type(w & jnp.int32(-65536), F32)
    return lo, hi


def _gather_chunk(idx_ref, tab_ref, buf_ref, c, group, slots):
    for g in range(PAIR_CHUNK // group):
        ids = idx_ref.at[pl.ds(pl.multiple_of(c * PAIR_CHUNK + g * group, group), group)]
        for i in range(group):
            e4 = pl.multiple_of(ids[i], ROWS_PER_EXPERT)
            s = slots[g * group + i] * ROWS_PER_EXPERT
            buf_ref[s:s + ROWS_PER_EXPERT, :] = tab_ref[pl.ds(e4, ROWS_PER_EXPERT), :]


def _chunk_loop(n_chunks, idx_ref, tab_ref, buf_a, buf_b, consume, init, group, slots, per_step=2, extra=None):
    last = n_chunks - 1
    gather = functools.partial(_gather_chunk, idx_ref, tab_ref, group=group, slots=slots)
    gather(buf_a, 0)
    bufs = (buf_a, buf_b)

    def body(i, carry):
        if extra is not None:
            extra(i)
        c = per_step * i
        for u in range(per_step):
            carry = consume(c + u, bufs[u % 2], carry)
            nxt = c + u + 1
            gather(bufs[(u + 1) % 2], jnp.minimum(nxt, last) if u == per_step - 1 else nxt)
        return carry

    return lax.fori_loop(0, n_chunks // per_step, body, init)


def _peer_u_kernel(idx_ref, h_ref, gate_ref, tab_ref, coef_ref, buf_a, buf_b, r_even, r_odd, d_ref, *, tt, n_tiles):
    step = pl.program_id(0)
    sub = lax.broadcasted_iota(jnp.int32, (SUBLANES, LANES), 0)
    low = sub < ROWS_PER_EXPERT
    m_a = ((sub % 4) >= 2)[None]
    m_b = ((sub % 2) == 1)[None]
    nv = PAIR_CHUNK // 2

    def pair_partials(r_ref, c, buf_ref, carry):
        t = c >> CHUNK_SHIFT
        h8 = h_ref[pl.ds(pl.multiple_of(t * SUBLANES, SUBLANES), SUBLANES), :]
        sw = pltpu.roll(h8, ROWS_PER_EXPERT, axis=0)
        ha = jnp.where(low, h8, sw)[None]
        hb = jnp.where(low, sw, h8)[None]
        lo, hi = _unpack_words(buf_ref[...])
        x = lo.reshape(nv, SUBLANES, LANES) * ha + hi.reshape(nv, SUBLANES, LANES) * hb
        x = x.reshape(nv // 2, 2, SUBLANES, LANES)
        xe, xo = x[:, 0], x[:, 1]
        a = xe + pltpu.roll(xe, 2, axis=1)
        b = xo + pltpu.roll(xo, 6, axis=1)
        m = jnp.where(m_a, a, b).reshape(nv // 4, 2, SUBLANES, LANES)
        me, mo = m[:, 0], m[:, 1]
        a2 = me + pltpu.roll(me, 1, axis=1)
        b2 = mo + pltpu.roll(mo, 7, axis=1)
        r_ref[pl.ds(pl.multiple_of(c * PAIR_CHUNK, PAIR_CHUNK), PAIR_CHUNK), :] = (
            jnp.where(m_b, a2, b2).reshape(PAIR_CHUNK, LANES))
        return carry

    def token_sum(r_prev, t):
        r_t = r_prev[pl.ds(pl.multiple_of(t * PEER_SEL, PEER_SEL), PEER_SEL), :]
        d_ref[pl.ds(t, 1), :] = jnp.sum(r_t.T, axis=0, keepdims=True)

    def loop(r_cur, **kw):
        _chunk_loop(tt * N_CHUNKS, idx_ref, tab_ref, buf_a, buf_b, functools.partial(pair_partials, r_cur), 0,
                    SMEM_GROUP, _CHUNK_SLOT, **kw)

    @pl.when(step == 0)
    def _():
        loop(r_even)

    for parity, (r_cur, r_prev) in enumerate(((r_even, r_odd), (r_odd, r_even))):
        @pl.when(jnp.logical_and(jnp.logical_and(step > 0, step < n_tiles), step % 2 == parity))
        def _(r_cur=r_cur, r_prev=r_prev):
            loop(r_cur, per_step=N_CHUNKS, extra=functools.partial(token_sum, r_prev))

    @pl.when(step == n_tiles)
    def _():
        def only_sums(t, carry):
            token_sum(r_even if n_tiles % 2 else r_odd, t)
            return carry
        lax.fori_loop(0, tt, only_sums, 0)

    @pl.when(step > 0)
    def _():
        coef_ref[...] = gate_ref[...] * _gelu(d_ref[...])


def _peer_u(idx4, h8, gates, tab, *, tt=128):
    n_tok = gates.shape[0]
    tt = min(tt, n_tok)
    n_tiles = n_tok // tt
    cur = lambda i: jnp.minimum(i, n_tiles - 1)
    done = lambda i: jnp.maximum(i - 1, 0)
    return pl.pallas_call(
        functools.partial(_peer_u_kernel, tt=tt, n_tiles=n_tiles),
        grid=(n_tiles + 1,),
        in_specs=[pl.BlockSpec((tt * PEER_SEL,), lambda i: (cur(i),), memory_space=pltpu.SMEM),
                  pl.BlockSpec((tt * SUBLANES, LANES), lambda i: (cur(i), 0)),
                  pl.BlockSpec((tt, PEER_SEL), lambda i: (done(i), 0)),
                  pl.BlockSpec(tab.shape, lambda i: (0, 0), pipeline_mode=pl.Buffered(1))],
        out_specs=pl.BlockSpec((tt, PEER_SEL), lambda i: (done(i), 0)),
        out_shape=jax.ShapeDtypeStruct((n_tok, PEER_SEL), F32),
        scratch_shapes=[pltpu.VMEM((CHUNK_ROWS, LANES), jnp.int32),
                        pltpu.VMEM((CHUNK_ROWS, LANES), jnp.int32),
                        pltpu.VMEM((tt * PEER_SEL, LANES), F32),
                        pltpu.VMEM((tt * PEER_SEL, LANES), F32),
                        pltpu.VMEM((tt, PEER_SEL), F32)],
        compiler_params=_cparams(("arbitrary",), PEER_U_VMEM_LIMIT),
        name="peer_u",
    )(idx4, h8, gates, tab)


def _peer_v_kernel(idx_ref, coef_ref, tab_ref, o_ref, buf_a, buf_b, *, tt):
    sub = lax.broadcasted_iota(jnp.int32, (SUBLANES, LANES), 0)
    low = sub < ROWS_PER_EXPERT
    nv = PAIR_CHUNK // 2

    def accumulate(c, buf_ref, acc):
        acc_lo, acc_hi = acc
        lo, hi = _unpack_words(buf_ref[...])
        lo = lo.reshape(nv, SUBLANES, LANES)
        hi = hi.reshape(nv, SUBLANES, LANES)
        cs = coef_ref.at[pl.ds(pl.multiple_of(c * nv, nv), nv)]
        for v in range(nv):
            c_lo, c_hi = _unpack_words(jnp.full((SUBLANES, LANES), cs[v], jnp.int32))
            cv = jnp.where(low, c_lo, c_hi)
            acc_lo = acc_lo + cv * lo[v]
            acc_hi = acc_hi + cv * hi[v]
        lo4 = acc_lo + pltpu.roll(acc_lo, ROWS_PER_EXPERT, axis=0)
        hi4 = acc_hi + pltpu.roll(acc_hi, ROWS_PER_EXPERT, axis=0)
        o_ref[c >> CHUNK_SHIFT] = jnp.where(low, lo4, hi4)
        last = (c & (N_CHUNKS - 1)) == N_CHUNKS - 1
        return jnp.where(last, 0.0, acc_lo), jnp.where(last, 0.0, acc_hi)

    z = jnp.zeros((SUBLANES, LANES), F32)
    _chunk_loop(tt * N_CHUNKS, idx_ref, tab_ref, buf_a, buf_b, accumulate, (z, z), PAIR_CHUNK,
                tuple(range(PAIR_CHUNK)))


def _pack_coefs(coef):
    n_tok = coef.shape[0]
    bits = lax.bitcast_convert_type(coef.astype(BF16), jnp.uint16).astype(jnp.uint32)
    bits = bits.reshape(n_tok, PEER_SEL // 2, 2)
    w = bits[:, :, 0] | (bits[:, :, 1] << 16)
    return lax.bitcast_convert_type(w, jnp.int32).reshape(n_tok * (PEER_SEL // 2))


def _peer_v(idx4, coef_words, tab, *, tt=128):
    n_tok = idx4.shape[0] // PEER_SEL
    tt = min(tt, n_tok)
    smem = pl.BlockSpec((tt * PEER_SEL,), lambda i: (i,), memory_space=pltpu.SMEM)
    return pl.pallas_call(
        functools.partial(_peer_v_kernel, tt=tt),
        grid=(n_tok // tt,),
        in_specs=[smem, pl.BlockSpec((tt * PEER_SEL // 2,), lambda i: (i,), memory_space=pltpu.SMEM),
                  pl.BlockSpec(tab.shape, lambda i: (0, 0), pipeline_mode=pl.Buffered(1))],
        out_specs=pl.BlockSpec((tt, SUBLANES, LANES), lambda i: (i, 0, 0)),
        out_shape=jax.ShapeDtypeStruct((n_tok, SUBLANES, LANES), F32),
        scratch_shapes=[pltpu.VMEM((CHUNK_ROWS, LANES), jnp.int32),
                        pltpu.VMEM((CHUNK_ROWS, LANES), jnp.int32)],
        compiler_params=_cparams(("arbitrary",), VMEM_LIMIT),
        name="peer_v",
    )(idx4, coef_words, tab)


def _peer(h8, qp, sub_keys, expert_u, expert_v):
    n_tok = qp.shape[1]
    half = sub_keys.shape[-1]
    z = jnp.zeros_like(sub_keys[0])
    keys_pad = jnp.stack([jnp.concatenate([sub_keys[0], z], axis=-1),
                          jnp.concatenate([z, sub_keys[1]], axis=-1)]).astype(BF16)
    assert keys_pad.shape[-1] == 2 * half == qp.shape[-1]
    idx4, gates = _peer_topk(qp, keys_pad)
    idx4 = idx4.reshape(n_tok * PEER_SEL)
    coef = _peer_u(idx4, h8, gates, _pack_table(expert_u))
    out = _peer_v(idx4, _pack_coefs(coef), _pack_table(expert_v))
    return out.reshape(n_tok * SUBLANES, LANES)


def _final_kernel(x_ref, p_ref, g_ref, o_ref):
    o_ref[...] = _rms(x_ref[...] + _rows_from_chunks(p_ref, x_ref.shape[0]), g_ref[...])


def _final_norm(x, p, g, *, tr=1024):
    rows, dm = x.shape
    tr = min(tr, rows)
    blk = pl.BlockSpec((tr, dm), lambda i: (i, 0))
    return pl.pallas_call(
        _final_kernel,
        grid=(rows // tr,),
        in_specs=[blk, pl.BlockSpec((tr * SUBLANES, LANES), lambda i: (i, 0)),
                  pl.BlockSpec((1, dm), lambda i: (0, 0))],
        out_specs=blk,
        out_shape=jax.ShapeDtypeStruct((rows, dm), F32),
        compiler_params=_cparams(("parallel",), VMEM_LIMIT),
        name="final_norm",
    )(x, p, g.reshape(1, dm))


def kernel(x, w_in, w_out, rel_bias, g_attn, g_ssm, norm_mix, norm_ffn, lam_re, lam_im, log_step, b_re, b_im, c_re, c_im, d_skip, w_glu, w_query, sub_keys, expert_u, expert_v, norm_final):
    bsz, seq, dm = x.shape
    depth = w_in.shape[0]
    prev = None
    biases = [_attn_bias_tables(rel_bias, d) for _, d in DILATED_PATTERNS]
    for l in range(depth):
        x, q, k, v, u = _in_proj(x, prev, norm_mix[l], w_in[l].astype(BF16))
        attn = _attention(q, k, v, biases)
        u_tm = u.reshape(seq * bsz, SSM_WIDTH)
        a, bw, cw = _ssm_params(lam_re[l], lam_im[l], log_step[l], b_re[l], b_im[l], c_re[l], c_im[l])
        y = _ssm_scan(u_tm, a, bw, cw, bsz=bsz)
        wo = w_out[l].astype(BF16)
        z = _ssm_post(y, u_tm, d_skip[l], w_glu[l].astype(BF16), g_ssm[l], wo[ATTN_WIDTH:])
        x, h, qp = _mix_out(x, z.reshape(seq, bsz * dm), attn,
                            g_attn[l], wo[:ATTN_WIDTH], norm_ffn[l], w_query[l].astype(BF16))
        prev = _peer(h, qp, sub_keys[l], expert_u[l], expert_v[l])
    out = _final_norm(x.reshape(bsz * seq, dm), prev, norm_final)
    return out.reshape(bsz, seq, dm)
```

```python
import functools
import math

import numpy as np
import jax
import jax.numpy as jnp
from jax import lax
from jax.experimental import pallas as pl
from jax.experimental.pallas import tpu as pltpu

F32 = jnp.float32
BF16 = jnp.bfloat16

EPS = 1e-6
NEG_INF = -1e30
HEAD_DIM = 64
ATTN_WIDTH = 512
SSM_WIDTH = 512
SSM_GROUP = 16
SSM_STATE = 64
DILATED_PATTERNS = ((128, 1), (512, 4), (2048, 16))
REL_BUCKETS = 32
REL_MAX_DISTANCE = 1024
PEER_HEADS = 8
PEER_KEYS = 128
PEER_TOPK = 16
PEER_SEL = PEER_HEADS * PEER_TOPK

LANES = 128
SUBLANES = 8
QBLK = 128
KWIN = 256
BAND = 64
VMEM_LIMIT = 52 * 1024 * 1024
PEER_U_VMEM_LIMIT = 58 * 1024 * 1024


def _cparams(sem, vmem=None):
    return pltpu.CompilerParams(dimension_semantics=sem, vmem_limit_bytes=vmem)


def _rms(x, g):
    return x * lax.rsqrt(jnp.mean(x * x, axis=-1, keepdims=True) + EPS) * g


def _gelu(x):
    return 0.5 * x * (1.0 + lax.erf(x * (1.0 / math.sqrt(2.0))))


def _rows_from_chunks(p_ref, n_rows):
    return jnp.concatenate([p_ref[pl.ds(c, n_rows, stride=SUBLANES), :] for c in range(SUBLANES)], axis=-1)


def _in_proj_kernel(*refs, has_prev):
    if has_prev:
        x_ref, p_ref, g_ref, w_ref, xo_ref, q_ref, k_ref, v_ref, u_ref = refs
        x = x_ref[0] + _rows_from_chunks(p_ref, x_ref.shape[1])
    else:
        x_ref, g_ref, w_ref, xo_ref, q_ref, k_ref, v_ref, u_ref = refs
        x = x_ref[0]
    xo_ref[0] = x
    h = _rms(x, g_ref[...]).astype(BF16)
    proj = jnp.dot(h, w_ref[...], preferred_element_type=F32)
    a = ATTN_WIDTH
    q_ref[0] = proj[:, :a] * (HEAD_DIM ** -0.5)
    k_ref[0] = proj[:, a:2 * a]
    v_ref[0] = proj[:, 2 * a:3 * a]
    u_ref[...] = proj[:, 3 * a:]


def _in_proj(x, prev, g, w_bf16, *, ts=512):
    bsz, seq, dm = x.shape
    ts = min(ts, seq)
    row = pl.BlockSpec((1, ts, dm), lambda b, i: (b, i, 0))
    qkv = pl.BlockSpec((1, ts, ATTN_WIDTH), lambda b, i: (b, i, 0))
    ns = seq // ts
    chunks = pl.BlockSpec((ts * SUBLANES, LANES), lambda b, i: (b * ns + i, 0))
    ins = [x] + ([prev] if prev is not None else []) + [g.reshape(1, dm), w_bf16]
    in_specs = [row] + ([chunks] if prev is not None else []) + [
        pl.BlockSpec((1, dm), lambda b, i: (0, 0)),
        pl.BlockSpec(w_bf16.shape, lambda b, i: (0, 0)),
    ]
    return pl.pallas_call(
        functools.partial(_in_proj_kernel, has_prev=prev is not None),
        grid=(bsz, seq // ts),
        in_specs=in_specs,
        out_specs=[row, qkv, qkv, qkv, pl.BlockSpec((ts, SSM_WIDTH), lambda b, i: (i, b))],
        out_shape=[
            jax.ShapeDtypeStruct((bsz, seq, dm), F32),
            jax.ShapeDtypeStruct((bsz, seq, ATTN_WIDTH), F32),
            jax.ShapeDtypeStruct((bsz, seq, ATTN_WIDTH), F32),
            jax.ShapeDtypeStruct((bsz, seq, ATTN_WIDTH), F32),
            jax.ShapeDtypeStruct((seq, bsz * SSM_WIDTH), F32),
        ],
        compiler_params=_cparams(("parallel", "arbitrary"), VMEM_LIMIT),
        name="in_proj",
    )(*ins)


def _t5_buckets(rel):
    half = REL_BUCKETS // 2
    max_exact = half // 2
    n = np.abs(rel)
    large = max_exact + (np.log(np.maximum(n, 1) / max_exact)
                         / np.log(REL_MAX_DISTANCE / max_exact) * (half - max_exact)).astype(np.int32)
    large = np.minimum(large, half - 1)
    return (np.where(rel > 0, half, 0) + np.where(n < max_exact, n, large)).astype(np.int32)


def _attn_bias_tables(rel_bias, dilation):
    ql = np.arange(QBLK)[:, None]
    kl = np.arange(KWIN)[None, :]
    delta = np.stack([kl + off - ql for off in (0, -BAND, -2 * BAND)])
    buckets = np.where(np.abs(delta) <= BAND, _t5_buckets(delta * dilation), -1)
    rb = rel_bias.astype(F32).T
    bk = jnp.asarray(buckets, jnp.int32)[None]
    tab = jnp.full((rb.shape[0],) + buckets.shape, NEG_INF, F32)
    for b in range(REL_BUCKETS):
        tab = jnp.where(bk == b, rb[:, b][:, None, None, None], tab)
    return tab


BLOCKS_PER_STEP = 4


def _attn_kernel(q_ref, k_ref, v_ref, *rest, seq):
    bias_refs, (o_ref, acc_ref, m_ref, z_ref) = rest[:len(DILATED_PATTERNS)], rest[len(DILATED_PATTERNS):]
    lane = lax.broadcasted_iota(jnp.int32, (QBLK, LANES), 1)
    is_h0 = lane < HEAD_DIM
    dn = (((1,), (1,)), ((), ()))
    nsteps = seq // QBLK

    def rows(start, size, d):
        return pl.ds(start, size) if d == 1 else pl.ds(start, size, stride=d)

    def block(n, d, bias_ref, first, last):
        length = seq // d
        nblk = length // QBLK
        r, i = n >> (nblk.bit_length() - 1), n & (nblk - 1)
        s = i * QBLK
        ks = jnp.clip(s - BAND, 0, length - KWIN)
        var = jnp.where(i == 0, 0, jnp.where(i == nblk - 1, 2, 1))
        q_rows = rows(r + d * s, QBLK, d)
        k_rows = rows(r + d * ks, KWIN, d)
        qb = q_ref[q_rows, :].astype(BF16)
        kb = k_ref[k_rows, :].astype(BF16)
        vb = v_ref[k_rows, :].astype(BF16)
        outs, ms, zs = [], [], []
        for h in range(2):
            keep = is_h0 if h == 0 else jnp.logical_not(is_h0)
            qh = jnp.where(keep, qb, jnp.zeros_like(qb))
            logits = lax.dot_general(qh, kb, dn, preferred_element_type=F32) + bias_ref[h, var]
            m = jnp.max(logits, axis=-1, keepdims=True)
            p = jnp.exp(logits - m)
            outs.append(jnp.dot(p.astype(BF16), vb, preferred_element_type=F32))
            ms.append(jnp.broadcast_to(m, (QBLK, LANES)))
            zs.append(jnp.broadcast_to(jnp.sum(p, axis=-1, keepdims=True), (QBLK, LANES)))
        o = jnp.where(is_h0, outs[0], outs[1])
        m = jnp.where(is_h0, ms[0], ms[1])
        z = jnp.where(is_h0, zs[0], zs[1])
        if not first:
            m_old = m_ref[q_rows, :]
            m_new = jnp.maximum(m_old, m)
            a, b = jnp.exp(m_old - m_new), jnp.exp(m - m_new)
            o = acc_ref[q_rows, :] * a + o * b
            z = z_ref[q_rows, :] * a + z * b
            m = m_new
        if last:
            o_ref[q_rows, :] = o / z
        else:
            acc_ref[q_rows, :] = o
            m_ref[q_rows, :] = m
            z_ref[q_rows, :] = z

    for p, ((_, d), bias_ref) in enumerate(zip(DILATED_PATTERNS, bias_refs)):
        def step(g, carry, d=d, bias_ref=bias_ref, p=p):
            for j in range(BLOCKS_PER_STEP):
                block(g * BLOCKS_PER_STEP + j, d, bias_ref, p == 0, p == len(DILATED_PATTERNS) - 1)
            return carry
        lax.fori_loop(0, nsteps // BLOCKS_PER_STEP, step, 0)


def _attention(q, k, v, biases):
    bsz, seq, width = q.shape
    for _, d in DILATED_PATTERNS:
        length = seq // d
        assert length >= KWIN and length % QBLK == 0 and (length // QBLK) & (length // QBLK - 1) == 0
    assert (seq // QBLK) % BLOCKS_PER_STEP == 0
    blk = pl.BlockSpec((None, seq, LANES), lambda b, c: (b, 0, c))
    bias_spec = pl.BlockSpec((2, 3, QBLK, KWIN), lambda b, c: (c, 0, 0, 0))
    return pl.pallas_call(
        functools.partial(_attn_kernel, seq=seq),
        grid=(bsz, width // LANES),
        in_specs=[blk, blk, blk] + [bias_spec] * len(biases),
        out_specs=blk,
        out_shape=jax.ShapeDtypeStruct((bsz, seq, width), F32),
        scratch_shapes=[pltpu.VMEM((seq, LANES), F32)] * 3,
        compiler_params=_cparams(("parallel", "arbitrary"), VMEM_LIMIT),
        name="attention",
    )(q, k, v, *biases)


SSM_LANE_GROUPS = SSM_WIDTH // LANES
SSM_GB_STATES = (LANES // SSM_GROUP) * SSM_STATE


def _ssm_params(lam_re, lam_im, log_step, b_re, b_im, c_re, c_im):
    f = lambda t: t.astype(F32)
    lr, li = f(lam_re), f(lam_im)
    step = jnp.exp(f(log_step))[..., None]
    mag = jnp.exp(lr * step)
    ar, ai = mag * jnp.cos(li * step), mag * jnp.sin(li * step)
    nr, ni = ar - 1.0, ai
    den = lr * lr + li * li
    cr, ci = (nr * lr + ni * li) / den, (ni * lr - nr * li) / den
    br, bi = f(b_re), f(b_im)
    bbr = cr[..., None] * br - ci[..., None] * bi
    bbi = cr[..., None] * bi + ci[..., None] * br
    gpb = LANES // SSM_GROUP
    eye = jnp.eye(gpb, dtype=F32)

    def in_map(t):
        t = t.reshape(2, SSM_LANE_GROUPS, gpb, SSM_STATE, SSM_GROUP)
        return jnp.einsum('dbgpc,gh->dbgchp', t, eye).reshape(2, SSM_LANE_GROUPS, LANES, SSM_GB_STATES)

    def out_map(t):
        t = t.reshape(2, SSM_LANE_GROUPS, gpb, SSM_GROUP, SSM_STATE)
        return jnp.einsum('dbgcp,gh->dbgphc', t, eye).reshape(2, SSM_LANE_GROUPS, SSM_GB_STATES, LANES)

    bw = jnp.concatenate([in_map(bbr), in_map(bbi)], axis=-1).astype(BF16)
    cw = jnp.concatenate([out_map(f(c_re)), -out_map(f(c_im))], axis=-2).astype(BF16)
    a = jnp.stack([ar.reshape(2, SSM_LANE_GROUPS, SSM_GB_STATES),
                   ai.reshape(2, SSM_LANE_GROUPS, SSM_GB_STATES)], axis=2)
    return a, bw, cw


def _ssm_kernel(u_ref, a_ref, bw_ref, cw_ref, y_ref, st_ref, bu_ref, *, ts, bsz):
    d = pl.program_id(0)
    ns = SSM_GB_STATES

    @pl.when(pl.program_id(1) == 0)
    def _():
        st_ref[...] = jnp.zeros_like(st_ref)

    for gb in range(SSM_LANE_GROUPS):
        ub = u_ref[:, gb * LANES:(gb + 1) * LANES].astype(BF16)
        bu_ref[...] = jnp.dot(ub, bw_ref[0, gb], preferred_element_type=F32)
        ar = jnp.broadcast_to(a_ref[0, gb, 0:1, :], (bsz, ns))
        ai = jnp.broadcast_to(a_ref[0, gb, 1:2, :], (bsz, ns))

        def step(j, carry, ar=ar, ai=ai):
            xr, xi = carry
            tl = jnp.where(d == 0, j, ts - 1 - j)
            r = pl.multiple_of(tl * bsz, bsz)
            nr = ar * xr - ai * xi + bu_ref[pl.ds(r, bsz), :ns]
            ni = ar * xi + ai * xr + bu_ref[pl.ds(r, bsz), ns:]
            bu_ref[pl.ds(r, bsz), :ns] = nr
            bu_ref[pl.ds(r, bsz), ns:] = ni
            return nr, ni

        xr, xi = lax.fori_loop(0, ts, step, (st_ref[gb, :, :ns], st_ref[gb, :, ns:]))
        st_ref[gb, :, :ns] = xr
        st_ref[gb, :, ns:] = xi
        y_ref[0, :, gb * LANES:(gb + 1) * LANES] = jnp.dot(
            bu_ref[...].astype(BF16), cw_ref[0, gb], preferred_element_type=F32)


def _ssm_scan(u_tm, a, bw, cw, *, bsz, ts=64):
    rows, width = u_tm.shape
    seq = rows // bsz
    ts = min(ts, seq)
    nt = seq // ts
    tblk = lambda d, i: jnp.where(d == 0, i, nt - 1 - i)
    return pl.pallas_call(
        functools.partial(_ssm_kernel, ts=ts, bsz=bsz),
        grid=(2, nt),
        in_specs=[
            pl.BlockSpec((ts * bsz, width), lambda d, i: (tblk(d, i), 0)),
            pl.BlockSpec((1,) + a.shape[1:], lambda d, i: (d, 0, 0, 0)),
            pl.BlockSpec((1,) + bw.shape[1:], lambda d, i: (d, 0, 0, 0)),
            pl.BlockSpec((1,) + cw.shape[1:], lambda d, i: (d, 0, 0, 0)),
        ],
        out_specs=pl.BlockSpec((1, ts * bsz, width), lambda d, i: (d, tblk(d, i), 0)),
        out_shape=jax.ShapeDtypeStruct((2, rows, width), F32),
        scratch_shapes=[pltpu.VMEM((SSM_LANE_GROUPS, bsz, 2 * SSM_GB_STATES), F32),
                        pltpu.VMEM((ts * bsz, 2 * SSM_GB_STATES), F32)],
        compiler_params=_cparams(("arbitrary", "arbitrary"), VMEM_LIMIT),
        name="ssm_scan",
    )(u_tm, a, bw, cw)


def _ssm_post_kernel(y_ref, u_ref, d_ref, wg_ref, g_ref, wo_ref, z_ref, slab_ref, *, bsz):
    y = _gelu(y_ref[0] + y_ref[1] + d_ref[...] * u_ref[...]).astype(BF16)
    ab = jnp.dot(y, wg_ref[...], preferred_element_type=F32)
    ssm = ab[:, :SSM_WIDTH] * jax.nn.sigmoid(ab[:, SSM_WIDTH:])
    n = _rms(ssm, g_ref[...]).astype(BF16)
    z = jnp.dot(n, wo_ref[...], preferred_element_type=F32)
    nt, dm = z.shape[0] // bsz, z.shape[1]
    for c in range(dm // LANES):
        slab_ref[c] = z[:, c * LANES:(c + 1) * LANES]
    for b in range(bsz):
        for c in range(dm // LANES):
            z_ref[:, b * dm + c * LANES:b * dm + (c + 1) * LANES] = slab_ref[c, pl.ds(b, nt, stride=bsz), :]


def _ssm_post(y, u_tm, d_skip, w_glu_bf16, g_ssm, w_out_ssm_bf16, *, bsz, tr=512):
    rows, width = u_tm.shape
    tr = min(tr, rows)
    assert tr % bsz == 0
    dm = w_out_ssm_bf16.shape[1]
    full = lambda a: pl.BlockSpec(a.shape, lambda i: (0,) * a.ndim)
    d2, g2 = d_skip.reshape(1, width), g_ssm.reshape(1, width)
    return pl.pallas_call(
        functools.partial(_ssm_post_kernel, bsz=bsz),
        grid=(rows // tr,),
        in_specs=[pl.BlockSpec((2, tr, width), lambda i: (0, i, 0)),
                  pl.BlockSpec((tr, width), lambda i: (i, 0)),
                  full(d2), full(w_glu_bf16), full(g2), full(w_out_ssm_bf16)],
        out_specs=pl.BlockSpec((tr // bsz, bsz * dm), lambda i: (i, 0)),
        out_shape=jax.ShapeDtypeStruct((rows // bsz, bsz * dm), F32),
        scratch_shapes=[pltpu.VMEM((dm // LANES, tr, LANES), F32)],
        compiler_params=_cparams(("parallel",), VMEM_LIMIT),
        name="ssm_post",
    )(y, u_tm, d2, w_glu_bf16, g2, w_out_ssm_bf16)


def _mix_out_kernel(x_ref, z_ref, a_ref, ga_ref, wo_ref, gf_ref, wq_ref, xn_ref, h_ref, q_ref):
    n = _rms(a_ref[0], ga_ref[...]).astype(BF16)
    xn = x_ref[0] + z_ref[...] + jnp.dot(n, wo_ref[...], preferred_element_type=F32)
    xn_ref[0] = xn
    h = _rms(xn, gf_ref[...])
    for c in range(SUBLANES):
        h_ref[pl.ds(c, h.shape[0], stride=SUBLANES), :] = h[:, c * LANES:(c + 1) * LANES]
    qp = jnp.dot(h.astype(BF16), wq_ref[...], preferred_element_type=F32)
    for hd in range(PEER_HEADS):
        q_ref[hd] = qp[:, hd * LANES:(hd + 1) * LANES]


def _mix_out(x, z_tm, attn, g_attn, w_out_attn_bf16, norm_ffn, w_query, *, ts=256):
    bsz, seq, dm = x.shape
    ts = min(ts, seq)
    ns = seq // ts
    row = pl.BlockSpec((1, ts, dm), lambda b, i: (b, i, 0))
    half = pl.BlockSpec((1, ts, ATTN_WIDTH), lambda b, i: (b, i, 0))
    full = lambda a: pl.BlockSpec(a.shape, lambda b, i: (0,) * a.ndim)
    ga, gf = g_attn.reshape(1, ATTN_WIDTH), norm_ffn.reshape(1, dm)
    qdim = w_query.shape[1] // PEER_HEADS
    return pl.pallas_call(
        _mix_out_kernel,
        grid=(bsz, ns),
        in_specs=[row, pl.BlockSpec((ts, dm), lambda b, i: (i, b)), half,
                  full(ga), full(w_out_attn_bf16), full(gf), full(w_query)],
        out_specs=[row, pl.BlockSpec((ts * SUBLANES, LANES), lambda b, i: (b * ns + i, 0)),
                   pl.BlockSpec((PEER_HEADS, ts, qdim), lambda b, i: (0, b * ns + i, 0))],
        out_shape=[jax.ShapeDtypeStruct((bsz, seq, dm), F32),
                   jax.ShapeDtypeStruct((bsz * seq * SUBLANES, LANES), F32),

                   jax.ShapeDtypeStruct((PEER_HEADS, bsz * seq, qdim), F32)],
        compiler_params=_cparams(("parallel", "arbitrary"), VMEM_LIMIT),
        name="mix_out",
    )(x, z_tm, attn, ga, w_out_attn_bf16, gf, w_query)


TOPK_TOKENS = SUBLANES * LANES
KEY_PITCH = PEER_KEYS + 4
_CANDIDATES = tuple((a, b) for a in range(PEER_TOPK) for b in range(PEER_TOPK) if (a + 1) * (b + 1) <= PEER_TOPK)


def _tree(op, xs):
    xs = list(xs)
    while len(xs) > 1:
        xs = [op(xs[i], xs[i + 1]) if i + 1 < len(xs) else xs[i] for i in range(0, len(xs), 2)]
    return xs[0]


def _extract16(problems):
    big = jnp.int32(2 ** 30)
    ninf = jnp.float32(-jnp.inf)

    def step(r, ms):
        nxt = []
        for p, m in zip(problems, ms):
            s_ref, order = p["s"], p["order"]
            n = len(order)
            am = _tree(jnp.minimum, [jnp.where(s_ref[k] == m, order[k], big) for k in range(n)])
            news, pays = [], []
            for k in range(n):
                hit = am == order[k]
                nk = jnp.where(hit, ninf, s_ref[k])
                s_ref[k] = nk
                news.append(nk)
                if p.get("pay") is not None:
                    pays.append(jnp.where(hit, p["pay"][k], -1))
            p["vals"][r] = m
            p["picks"][r] = _tree(jnp.maximum, pays) if pays else am
            nxt.append(_tree(jnp.maximum, news))
        return tuple(nxt)

    init = tuple(_tree(jnp.maximum, [p["s"][k] for k in range(len(p["order"]))]) for p in problems)
    lax.fori_loop(0, PEER_TOPK, step, init)


def _peer_topk_kernel(q_ref, k_ref, idx_ref, gate_ref,
                      slab_ref, s1_ref, s2_ref, t1_ref, i1_ref, t2_ref, i2_ref, cand_ref, pay_ref, ts_ref, ex_ref):
    dn = (((1,), (1,)), ((), ()))
    keys = tuple(range(PEER_KEYS))

    def head(h, carry):
        for w, s_ref in ((0, s1_ref), (1, s2_ref)):
            for j in range(SUBLANES):
                slab_ref[j * KEY_PITCH:j * KEY_PITCH + PEER_KEYS, :] = lax.dot_general(
                    k_ref[w, h], q_ref[h, j * LANES:(j + 1) * LANES, :].astype(BF16), dn,
                    preferred_element_type=F32)
            for k in range(PEER_KEYS):
                s_ref[k] = slab_ref[pl.ds(k, SUBLANES, stride=KEY_PITCH), :]
        _extract16([dict(s=s1_ref, order=keys, vals=t1_ref, picks=i1_ref),
                    dict(s=s2_ref, order=keys, vals=t2_ref, picks=i2_ref)])
        for c, (a, b) in enumerate(_CANDIDATES):
            cand_ref[c] = t1_ref[a] + t2_ref[b]
            pay_ref[c] = i1_ref[a] * PEER_KEYS + i2_ref[b]
        _extract16([dict(s=cand_ref, order=tuple(a * PEER_TOPK + b for a, b in _CANDIDATES), pay=pay_ref,
                         vals=ts_ref, picks=ex_ref)])
        top_s = ts_ref[...]
        e = jnp.exp(top_s - jnp.max(top_s, axis=0, keepdims=True))
        gate_ref[0, h] = e / jnp.sum(e, axis=0, keepdims=True)
        idx_ref[0, h] = ex_ref[...] * ROWS_PER_EXPERT
        return carry

    lax.fori_loop(0, PEER_HEADS, head, 0)


def _peer_topk(qp, keys_pad):
    n_tok = qp.shape[1]
    tt = TOPK_TOKENS
    assert n_tok % tt == 0
    shp = (n_tok // tt, PEER_HEADS, PEER_TOPK, SUBLANES, LANES)
    out = pl.BlockSpec((1,) + shp[1:], lambda i: (i, 0, 0, 0, 0))
    vregs = lambda n, dt: pltpu.VMEM((n, SUBLANES, LANES), dt)
    idx, gate = pl.pallas_call(
        _peer_topk_kernel,
        grid=(n_tok // tt,),
        in_specs=[pl.BlockSpec((PEER_HEADS, tt, qp.shape[2]), lambda i: (0, i, 0)),
                  pl.BlockSpec(keys_pad.shape, lambda i: (0, 0, 0, 0))],
        out_specs=[out, out],
        out_shape=[jax.ShapeDtypeStruct(shp, jnp.int32), jax.ShapeDtypeStruct(shp, F32)],
        scratch_shapes=[pltpu.VMEM((SUBLANES * KEY_PITCH, LANES), F32),
                        vregs(PEER_KEYS, F32), vregs(PEER_KEYS, F32),
                        vregs(PEER_TOPK, F32), vregs(PEER_TOPK, jnp.int32),
                        vregs(PEER_TOPK, F32), vregs(PEER_TOPK, jnp.int32),
                        vregs(len(_CANDIDATES), F32), vregs(len(_CANDIDATES), jnp.int32),
                        vregs(PEER_TOPK, F32), vregs(PEER_TOPK, jnp.int32)],
        compiler_params=_cparams(("parallel",), VMEM_LIMIT),
        name="peer_topk",
    )(qp, keys_pad)
    to_tok = lambda a: jnp.transpose(a, (0, 3, 4, 1, 2)).reshape(n_tok, PEER_SEL)
    return to_tok(idx), to_tok(gate)


ROWS_PER_EXPERT = 4
PAIR_CHUNK = 32
CHUNK_ROWS = PAIR_CHUNK * ROWS_PER_EXPERT
SMEM_GROUP = 8
N_CHUNKS = PEER_SEL // PAIR_CHUNK
CHUNK_SHIFT = N_CHUNKS.bit_length() - 1
assert 1 << CHUNK_SHIFT == N_CHUNKS
_GROUP_SLOT = tuple(
    2 * (2 * (0 if (j % 2) else 1) + (1 if (j // 2) in (0, 2) else 0)) + (0 if (j // 2) < 2 else 1)
    for j in range(8))
_CHUNK_SLOT = tuple(8 * (j // 8) + _GROUP_SLOT[j % 8] for j in range(PAIR_CHUNK))


def _pack_table_kernel(t_ref, o_ref):
    rows, dm = t_ref.shape
    for c in range(ROWS_PER_EXPERT):
        lo = lax.bitcast_convert_type(t_ref[:, c * LANES:(c + 1) * LANES].astype(BF16).astype(F32), jnp.int32)
        hi = lax.bitcast_convert_type(
            t_ref[:, dm // 2 + c * LANES:dm // 2 + (c + 1) * LANES].astype(BF16).astype(F32), jnp.int32)
        o_ref[pl.ds(c, rows, stride=ROWS_PER_EXPERT), :] = lax.shift_right_logical(lo, 16) | hi


def _pack_table(tbl, *, rows=512):
    e, dm = tbl.shape
    assert dm == 2 * ROWS_PER_EXPERT * LANES
    rows = min(rows, e)
    return pl.pallas_call(
        _pack_table_kernel,
        grid=(e // rows,),
        in_specs=[pl.BlockSpec((rows, dm), lambda i: (i, 0))],
        out_specs=pl.BlockSpec((rows * ROWS_PER_EXPERT, LANES), lambda i: (i, 0)),
        out_shape=jax.ShapeDtypeStruct((e * ROWS_PER_EXPERT, LANES), jnp.int32),
        compiler_params=_cparams(("parallel",), VMEM_LIMIT),
        name="pack_table",
    )(tbl)


def _unpack_words(w):
    lo = lax.bitcast_convert_type(w << 16, F32)
    hi = lax.bitcast_convert_type(w & jnp.int32(-65536), F32)
    return lo, hi


def _gather_chunk(idx_ref, tab_ref, buf_ref, c, group, slots):
    for g in range(PAIR_CHUNK // group):
        ids = idx_ref.at[pl.ds(pl.multiple_of(c * PAIR_CHUNK + g * group, group), group)]
        for i in range(group):
            e4 = pl.multiple_of(ids[i], ROWS_PER_EXPERT)
            s = slots[g * group + i] * ROWS_PER_EXPERT
            buf_ref[s:s + ROWS_PER_EXPERT, :] = tab_ref[pl.ds(e4, ROWS_PER_EXPERT), :]


def _chunk_loop(n_chunks, idx_ref, tab_ref, buf_a, buf_b, consume, init, group, slots, per_step=2, extra=None):
    last = n_chunks - 1
    gather = functools.partial(_gather_chunk, idx_ref, tab_ref, group=group, slots=slots)
    gather(buf_a, 0)
    bufs = (buf_a, buf_b)

    def body(i, carry):
        if extra is not None:
            extra(i)
        c = per_step * i
        for u in range(per_step):
            carry = consume(c + u, bufs[u % 2], carry)
            nxt = c + u + 1
            gather(bufs[(u + 1) % 2], jnp.minimum(nxt, last) if u == per_step - 1 else nxt)
        return carry

    return lax.fori_loop(0, n_chunks // per_step, body, init)


def _peer_u_kernel(idx_ref, h_ref, gate_ref, tab_ref, coef_ref, buf_a, buf_b, r_even, r_odd, d_ref, *, tt, n_tiles):
    step = pl.program_id(0)
    sub = lax.broadcasted_iota(jnp.int32, (SUBLANES, LANES), 0)
    low = sub < ROWS_PER_EXPERT
    m_a = ((sub % 4) >= 2)[None]
    m_b = ((sub % 2) == 1)[None]
    nv = PAIR_CHUNK // 2

    def pair_partials(r_ref, c, buf_ref, carry):
        t = c >> CHUNK_SHIFT
        h8 = h_ref[pl.ds(pl.multiple_of(t * SUBLANES, SUBLANES), SUBLANES), :]
        sw = pltpu.roll(h8, ROWS_PER_EXPERT, axis=0)
        ha = jnp.where(low, h8, sw)[None]
        hb = jnp.where(low, sw, h8)[None]
        lo, hi = _unpack_words(buf_ref[...])
        x = lo.reshape(nv, SUBLANES, LANES) * ha + hi.reshape(nv, SUBLANES, LANES) * hb
        x = x.reshape(nv // 2, 2, SUBLANES, LANES)
        xe, xo = x[:, 0], x[:, 1]
        a = xe + pltpu.roll(xe, 2, axis=1)
        b = xo + pltpu.roll(xo, 6, axis=1)
        m = jnp.where(m_a, a, b).reshape(nv // 4, 2, SUBLANES, LANES)
        me, mo = m[:, 0], m[:, 1]
        a2 = me + pltpu.roll(me, 1, axis=1)
        b2 = mo + pltpu.roll(mo, 7, axis=1)
        r_ref[pl.ds(pl.multiple_of(c * PAIR_CHUNK, PAIR_CHUNK), PAIR_CHUNK), :] = (
            jnp.where(m_b, a2, b2).reshape(PAIR_CHUNK, LANES))
        return carry

    def token_sum(r_prev, t):
        r_t = r_prev[pl.ds(pl.multiple_of(t * PEER_SEL, PEER_SEL), PEER_SEL), :]
        d_ref[pl.ds(t, 1), :] = jnp.sum(r_t.T, axis=0, keepdims=True)

    def loop(r_cur, **kw):
        _chunk_loop(tt * N_CHUNKS, idx_ref, tab_ref, buf_a, buf_b, functools.partial(pair_partials, r_cur), 0,
                    SMEM_GROUP, _CHUNK_SLOT, **kw)

    @pl.when(step == 0)
    def _():
        loop(r_even)

    for parity, (r_cur, r_prev) in enumerate(((r_even, r_odd), (r_odd, r_even))):
        @pl.when(jnp.logical_and(jnp.logical_and(step > 0, step < n_tiles), step % 2 == parity))
        def _(r_cur=r_cur, r_prev=r_prev):
            loop(r_cur, per_step=N_CHUNKS, extra=functools.partial(token_sum, r_prev))

    @pl.when(step == n_tiles)
    def _():
        def only_sums(t, carry):
            token_sum(r_even if n_tiles % 2 else r_odd, t)
            return carry
        lax.fori_loop(0, tt, only_sums, 0)

    @pl.when(step > 0)
    def _():
        coef_ref[...] = gate_ref[...] * _gelu(d_ref[...])


def _peer_u(idx4, h8, gates, tab, *, tt=128):
    n_tok = gates.shape[0]
    tt = min(tt, n_tok)
    n_tiles = n_tok // tt
    cur = lambda i: jnp.minimum(i, n_tiles - 1)
    done = lambda i: jnp.maximum(i - 1, 0)
    return pl.pallas_call(
        functools.partial(_peer_u_kernel, tt=tt, n_tiles=n_tiles),
        grid=(n_tiles + 1,),
        in_specs=[pl.BlockSpec((tt * PEER_SEL,), lambda i: (cur(i),), memory_space=pltpu.SMEM),
                  pl.BlockSpec((tt * SUBLANES, LANES), lambda i: (cur(i), 0)),
                  pl.BlockSpec((tt, PEER_SEL), lambda i: (done(i), 0)),
                  pl.BlockSpec(tab.shape, lambda i: (0, 0), pipeline_mode=pl.Buffered(1))],
        out_specs=pl.BlockSpec((tt, PEER_SEL), lambda i: (done(i), 0)),
        out_shape=jax.ShapeDtypeStruct((n_tok, PEER_SEL), F32),
        scratch_shapes=[pltpu.VMEM((CHUNK_ROWS, LANES), jnp.int32),
                        pltpu.VMEM((CHUNK_ROWS, LANES), jnp.int32),
                        pltpu.VMEM((tt * PEER_SEL, LANES), F32),
                        pltpu.VMEM((tt * PEER_SEL, LANES), F32),
                        pltpu.VMEM((tt, PEER_SEL), F32)],
        compiler_params=_cparams(("arbitrary",), PEER_U_VMEM_LIMIT),
        name="peer_u",
    )(idx4, h8, gates, tab)


def _peer_v_kernel(idx_ref, coef_ref, tab_ref, o_ref, buf_a, buf_b, *, tt):
    sub = lax.broadcasted_iota(jnp.int32, (SUBLANES, LANES), 0)
    low = sub < ROWS_PER_EXPERT
    nv = PAIR_CHUNK // 2

    def accumulate(c, buf_ref, acc):
        acc_lo, acc_hi = acc
        lo, hi = _unpack_words(buf_ref[...])
        lo = lo.reshape(nv, SUBLANES, LANES)
        hi = hi.reshape(nv, SUBLANES, LANES)
        cs = coef_ref.at[pl.ds(pl.multiple_of(c * nv, nv), nv)]
        for v in range(nv):
            c_lo, c_hi = _unpack_words(jnp.full((SUBLANES, LANES), cs[v], jnp.int32))
            cv = jnp.where(low, c_lo, c_hi)
            acc_lo = acc_lo + cv * lo[v]
            acc_hi = acc_hi + cv * hi[v]
        lo4 = acc_lo + pltpu.roll(acc_lo, ROWS_PER_EXPERT, axis=0)
        hi4 = acc_hi + pltpu.roll(acc_hi, ROWS_PER_EXPERT, axis=0)
        o_ref[c >> CHUNK_SHIFT] = jnp.where(low, lo4, hi4)
        last = (c & (N_CHUNKS - 1)) == N_CHUNKS - 1
        return jnp.where(last, 0.0, acc_lo), jnp.where(last, 0.0, acc_hi)

    z = jnp.zeros((SUBLANES, LANES), F32)
    _chunk_loop(tt * N_CHUNKS, idx_ref, tab_ref, buf_a, buf_b, accumulate, (z, z), PAIR_CHUNK,
                tuple(range(PAIR_CHUNK)))


def _pack_coefs(coef):
    n_tok = coef.shape[0]
    bits = lax.bitcast_convert_type(coef.astype(BF16), jnp.uint16).astype(jnp.uint32)
    bits = bits.reshape(n_tok, PEER_SEL // 2, 2)
    w = bits[:, :, 0] | (bits[:, :, 1] << 16)
    return lax.bitcast_convert_type(w, jnp.int32).reshape(n_tok * (PEER_SEL // 2))


def _peer_v(idx4, coef_words, tab, *, tt=128):
    n_tok = idx4.shape[0] // PEER_SEL
    tt = min(tt, n_tok)
    smem = pl.BlockSpec((tt * PEER_SEL,), lambda i: (i,), memory_space=pltpu.SMEM)
    return pl.pallas_call(
        functools.partial(_peer_v_kernel, tt=tt),
        grid=(n_tok // tt,),
        in_specs=[smem, pl.BlockSpec((tt * PEER_SEL // 2,), lambda i: (i,), memory_space=pltpu.SMEM),
                  pl.BlockSpec(tab.shape, lambda i: (0, 0), pipeline_mode=pl.Buffered(1))],
        out_specs=pl.BlockSpec((tt, SUBLANES, LANES), lambda i: (i, 0, 0)),
        out_shape=jax.ShapeDtypeStruct((n_tok, SUBLANES, LANES), F32),
        scratch_shapes=[pltpu.VMEM((CHUNK_ROWS, LANES), jnp.int32),
                        pltpu.VMEM((CHUNK_ROWS, LANES), jnp.int32)],
        compiler_params=_cparams(("arbitrary",), VMEM_LIMIT),
        name="peer_v",
    )(idx4, coef_words, tab)


def _peer(h8, qp, sub_keys, expert_u, expert_v):
    n_tok = qp.shape[1]
    half = sub_keys.shape[-1]
    z = jnp.zeros_like(sub_keys[0])
    keys_pad = jnp.stack([jnp.concatenate([sub_keys[0], z], axis=-1),
                          jnp.concatenate([z, sub_keys[1]], axis=-1)]).astype(BF16)
    assert keys_pad.shape[-1] == 2 * half == qp.shape[-1]
    idx4, gates = _peer_topk(qp, keys_pad)
    idx4 = idx4.reshape(n_tok * PEER_SEL)
    coef = _peer_u(idx4, h8, gates, _pack_table(expert_u))
    out = _peer_v(idx4, _pack_coefs(coef), _pack_table(expert_v))
    return out.reshape(n_tok * SUBLANES, LANES)


def _final_kernel(x_ref, p_ref, g_ref, o_ref):
    o_ref[...] = _rms(x_ref[...] + _rows_from_chunks(p_ref, x_ref.shape[0]), g_ref[...])


def _final_norm(x, p, g, *, tr=1024):
    rows, dm = x.shape
    tr = min(tr, rows)
    blk = pl.BlockSpec((tr, dm), lambda i: (i, 0))
    return pl.pallas_call(
        _final_kernel,
        grid=(rows // tr,),
        in_specs=[blk, pl.BlockSpec((tr * SUBLANES, LANES), lambda i: (i, 0)),
                  pl.BlockSpec((1, dm), lambda i: (0, 0))],
        out_specs=blk,
        out_shape=jax.ShapeDtypeStruct((rows, dm), F32),
        compiler_params=_cparams(("parallel",), VMEM_LIMIT),
        name="final_norm",
    )(x, p, g.reshape(1, dm))


def kernel(x, w_in, w_out, rel_bias, g_attn, g_ssm, norm_mix, norm_ffn, lam_re, lam_im, log_step, b_re, b_im, c_re, c_im, d_skip, w_glu, w_query, sub_keys, expert_u, expert_v, norm_final):
    bsz, seq, dm = x.shape
    depth = w_in.shape[0]
    prev = None
    biases = [_attn_bias_tables(rel_bias, d) for _, d in DILATED_PATTERNS]
    for l in range(depth):
        x, q, k, v, u = _in_proj(x, prev, norm_mix[l], w_in[l].astype(BF16))
        attn = _attention(q, k, v, biases)
        u_tm = u.reshape(seq * bsz, SSM_WIDTH)
        a, bw, cw = _ssm_params(lam_re[l], lam_im[l], log_step[l], b_re[l], b_im[l], c_re[l], c_im[l])
        y = _ssm_scan(u_tm, a, bw, cw, bsz=bsz)
        wo = w_out[l].astype(BF16)
        z = _ssm_post(y, u_tm, d_skip[l], w_glu[l].astype(BF16), g_ssm[l], wo[ATTN_WIDTH:], bsz=bsz)
        x, h, qp = _mix_out(x, z, attn,
                            g_attn[l], wo[:ATTN_WIDTH], norm_ffn[l], w_query[l].astype(BF16))
        prev = _peer(h, qp, sub_keys[l], expert_u[l], expert_v[l])
    out = _final_norm(x.reshape(bsz * seq, dm), prev, norm_final)
    return out.reshape(bsz, seq, dm)
```

```python
import functools
import math

import numpy as np
import jax
import jax.numpy as jnp
from jax import lax
from jax.experimental import pallas as pl
from jax.experimental.pallas import tpu as pltpu

F32 = jnp.float32
BF16 = jnp.bfloat16

EPS = 1e-6
NEG_INF = -1e30
HEAD_DIM = 64
ATTN_WIDTH = 512
SSM_WIDTH = 512
SSM_GROUP = 16
SSM_STATE = 64
DILATED_PATTERNS = ((128, 1), (512, 4), (2048, 16))
REL_BUCKETS = 32
REL_MAX_DISTANCE = 1024
PEER_HEADS = 8
PEER_KEYS = 128
PEER_TOPK = 16
PEER_SEL = PEER_HEADS * PEER_TOPK

LANES = 128
SUBLANES = 8
QBLK = 128
KWIN = 256
BAND = 64
VMEM_LIMIT = 52 * 1024 * 1024
PEER_U_VMEM_LIMIT = 58 * 1024 * 1024


def _cparams(sem, vmem=None):
    return pltpu.CompilerParams(dimension_semantics=sem, vmem_limit_bytes=vmem)


def _rms(x, g):
    return x * lax.rsqrt(jnp.mean(x * x, axis=-1, keepdims=True) + EPS) * g


def _gelu(x):
    return 0.5 * x * (1.0 + lax.erf(x * (1.0 / math.sqrt(2.0))))


def _rows_from_chunks(p_ref, n_rows):
    return jnp.concatenate([p_ref[pl.ds(c, n_rows, stride=SUBLANES), :] for c in range(SUBLANES)], axis=-1)


def _in_proj_kernel(*refs, has_prev):
    if has_prev:
        x_ref, p_ref, g_ref, w_ref, xo_ref, q_ref, k_ref, v_ref, u_ref = refs
        x = x_ref[0] + _rows_from_chunks(p_ref, x_ref.shape[1])
    else:
        x_ref, g_ref, w_ref, xo_ref, q_ref, k_ref, v_ref, u_ref = refs
        x = x_ref[0]
    xo_ref[0] = x
    h = _rms(x, g_ref[...]).astype(BF16)
    proj = jnp.dot(h, w_ref[...], preferred_element_type=F32)
    a = ATTN_WIDTH
    q_ref[0] = proj[:, :a] * (HEAD_DIM ** -0.5)
    k_ref[0] = proj[:, a:2 * a]
    v_ref[0] = proj[:, 2 * a:3 * a]
    b, bsz = pl.program_id(1), pl.num_programs(1)
    for c in range(SSM_WIDTH // LANES):
        u_ref[c, pl.ds(b, proj.shape[0], stride=bsz), :] = proj[:, 3 * a + c * LANES:3 * a + (c + 1) * LANES]


def _in_proj(x, prev, g, w_bf16, *, ts=256):
    bsz, seq, dm = x.shape
    ts = min(ts, seq)
    row = pl.BlockSpec((1, ts, dm), lambda i, b: (b, i, 0))
    qkv = pl.BlockSpec((1, ts, ATTN_WIDTH), lambda i, b: (b, i, 0))
    ns = seq // ts
    chunks = pl.BlockSpec((ts * SUBLANES, LANES), lambda i, b: (b * ns + i, 0))
    ins = [x] + ([prev] if prev is not None else []) + [g.reshape(1, dm), w_bf16]
    in_specs = [row] + ([chunks] if prev is not None else []) + [
        pl.BlockSpec((1, dm), lambda i, b: (0, 0)),
        pl.BlockSpec(w_bf16.shape, lambda i, b: (0, 0)),
    ]
    slabs = SSM_WIDTH // LANES
    return pl.pallas_call(
        functools.partial(_in_proj_kernel, has_prev=prev is not None),
        grid=(seq // ts, bsz),
        in_specs=in_specs,
        out_specs=[row, qkv, qkv, qkv, pl.BlockSpec((slabs, ts * bsz, LANES), lambda i, b: (0, i, 0))],
        out_shape=[
            jax.ShapeDtypeStruct((bsz, seq, dm), F32),
            jax.ShapeDtypeStruct((bsz, seq, ATTN_WIDTH), F32),
            jax.ShapeDtypeStruct((bsz, seq, ATTN_WIDTH), F32),
            jax.ShapeDtypeStruct((bsz, seq, ATTN_WIDTH), F32),
            jax.ShapeDtypeStruct((slabs, seq * bsz, LANES), F32),
        ],
        compiler_params=_cparams(("parallel", "arbitrary"), VMEM_LIMIT),
        name="in_proj",
    )(*ins)


def _t5_buckets(rel):
    half = REL_BUCKETS // 2
    max_exact = half // 2
    n = np.abs(rel)
    large = max_exact + (np.log(np.maximum(n, 1) / max_exact)
                         / np.log(REL_MAX_DISTANCE / max_exact) * (half - max_exact)).astype(np.int32)
    large = np.minimum(large, half - 1)
    return (np.where(rel > 0, half, 0) + np.where(n < max_exact, n, large)).astype(np.int32)


def _attn_bias_tables(rel_bias, dilation):
    ql = np.arange(QBLK)[:, None]
    kl = np.arange(KWIN)[None, :]
    delta = np.stack([kl + off - ql for off in (0, -BAND, -2 * BAND)])
    buckets = np.where(np.abs(delta) <= BAND, _t5_buckets(delta * dilation), -1)
    rb = rel_bias.astype(F32).T
    bk = jnp.asarray(buckets, jnp.int32)[None]
    tab = jnp.full((rb.shape[0],) + buckets.shape, NEG_INF, F32)
    for b in range(REL_BUCKETS):
        tab = jnp.where(bk == b, rb[:, b][:, None, None, None], tab)
    return tab


BLOCKS_PER_STEP = 4


def _attn_kernel(q_ref, k_ref, v_ref, *rest, seq):
    bias_refs, (o_ref, acc_ref, m_ref, z_ref) = rest[:len(DILATED_PATTERNS)], rest[len(DILATED_PATTERNS):]
    lane = lax.broadcasted_iota(jnp.int32, (QBLK, LANES), 1)
    is_h0 = lane < HEAD_DIM
    dn = (((1,), (1,)), ((), ()))
    nsteps = seq // QBLK

    def rows(start, size, d):
        return pl.ds(start, size) if d == 1 else pl.ds(start, size, stride=d)

    def block(n, d, bias_ref, first, last):
        length = seq // d
        nblk = length // QBLK
        r, i = n >> (nblk.bit_length() - 1), n & (nblk - 1)
        s = i * QBLK
        ks = jnp.clip(s - BAND, 0, length - KWIN)
        var = jnp.where(i == 0, 0, jnp.where(i == nblk - 1, 2, 1))
        q_rows = rows(r + d * s, QBLK, d)
        k_rows = rows(r + d * ks, KWIN, d)
        qb = q_ref[q_rows, :].astype(BF16)
        kb = k_ref[k_rows, :].astype(BF16)
        vb = v_ref[k_rows, :].astype(BF16)
        outs, ms, zs = [], [], []
        for h in range(2):
            keep = is_h0 if h == 0 else jnp.logical_not(is_h0)
            qh = jnp.where(keep, qb, jnp.zeros_like(qb))
            logits = lax.dot_general(qh, kb, dn, preferred_element_type=F32) + bias_ref[h, var]
            m = jnp.max(logits, axis=-1, keepdims=True)
            p = jnp.exp(logits - m)
            outs.append(jnp.dot(p.astype(BF16), vb, preferred_element_type=F32))
            ms.append(jnp.broadcast_to(m, (QBLK, LANES)))
            zs.append(jnp.broadcast_to(jnp.sum(p, axis=-1, keepdims=True), (QBLK, LANES)))
        o = jnp.where(is_h0, outs[0], outs[1])
        m = jnp.where(is_h0, ms[0], ms[1])
        z = jnp.where(is_h0, zs[0], zs[1])
        if not first:
            m_old = m_ref[q_rows, :]
            m_new = jnp.maximum(m_old, m)
            a, b = jnp.exp(m_old - m_new), jnp.exp(m - m_new)
            o = acc_ref[q_rows, :] * a + o * b
            z = z_ref[q_rows, :] * a + z * b
            m = m_new
        if last:
            o_ref[q_rows, :] = o / z
        else:
            acc_ref[q_rows, :] = o
            m_ref[q_rows, :] = m
            z_ref[q_rows, :] = z

    for p, ((_, d), bias_ref) in enumerate(zip(DILATED_PATTERNS, bias_refs)):
        def step(g, carry, d=d, bias_ref=bias_ref, p=p):
            for j in range(BLOCKS_PER_STEP):
                block(g * BLOCKS_PER_STEP + j, d, bias_ref, p == 0, p == len(DILATED_PATTERNS) - 1)
            return carry
        lax.fori_loop(0, nsteps // BLOCKS_PER_STEP, step, 0)


def _attention(q, k, v, biases):
    bsz, seq, width = q.shape
    for _, d in DILATED_PATTERNS:
        length = seq // d
        assert length >= KWIN and length % QBLK == 0 and (length // QBLK) & (length // QBLK - 1) == 0
    assert (seq // QBLK) % BLOCKS_PER_STEP == 0
    blk = pl.BlockSpec((None, seq, LANES), lambda b, c: (b, 0, c))
    bias_spec = pl.BlockSpec((2, 3, QBLK, KWIN), lambda b, c: (c, 0, 0, 0))
    return pl.pallas_call(
        functools.partial(_attn_kernel, seq=seq),
        grid=(bsz, width // LANES),
        in_specs=[blk, blk, blk] + [bias_spec] * len(biases),
        out_specs=blk,
        out_shape=jax.ShapeDtypeStruct((bsz, seq, width), F32),
        scratch_shapes=[pltpu.VMEM((seq, LANES), F32)] * 3,
        compiler_params=_cparams(("parallel", "arbitrary"), VMEM_LIMIT),
        name="attention",
    )(q, k, v, *biases)


SSM_LANE_GROUPS = SSM_WIDTH // LANES
SSM_GB_STATES = (LANES // SSM_GROUP) * SSM_STATE


def _ssm_params(lam_re, lam_im, log_step, b_re, b_im, c_re, c_im):
    f = lambda t: t.astype(F32)
    lr, li = f(lam_re), f(lam_im)
    step = jnp.exp(f(log_step))[..., None]
    mag = jnp.exp(lr * step)
    ar, ai = mag * jnp.cos(li * step), mag * jnp.sin(li * step)
    nr, ni = ar - 1.0, ai
    den = lr * lr + li * li
    cr, ci = (nr * lr + ni * li) / den, (ni * lr - nr * li) / den
    br, bi = f(b_re), f(b_im)
    bbr = cr[..., None] * br - ci[..., None] * bi
    bbi = cr[..., None] * bi + ci[..., None] * br
    gpb = LANES // SSM_GROUP
    eye = jnp.eye(gpb, dtype=F32)

    def in_map(t):
        t = t.reshape(2, SSM_LANE_GROUPS, gpb, SSM_STATE, SSM_GROUP)
        return jnp.einsum('dbgpc,gh->dbgchp', t, eye).reshape(2, SSM_LANE_GROUPS, LANES, SSM_GB_STATES)

    def out_map(t):
        t = t.reshape(2, SSM_LANE_GROUPS, gpb, SSM_GROUP, SSM_STATE)
        return jnp.einsum('dbgcp,gh->dbgphc', t, eye).reshape(2, SSM_LANE_GROUPS, SSM_GB_STATES, LANES)

    bw = jnp.concatenate([in_map(bbr), in_map(bbi)], axis=-1).astype(BF16)
    cw = jnp.concatenate([out_map(f(c_re)), -out_map(f(c_im))], axis=-2).astype(BF16)
    a = jnp.stack([ar.reshape(2, SSM_LANE_GROUPS, SSM_GB_STATES),
                   ai.reshape(2, SSM_LANE_GROUPS, SSM_GB_STATES)], axis=2)
    return a, bw, cw


def _ssm_kernel(u_ref, a_ref, bw_ref, cw_ref, y_ref, st_ref, bu_ref, *, ts, bsz):
    d = pl.program_id(0)
    ns = SSM_GB_STATES

    @pl.when(pl.program_id(1) == 0)
    def _():
        st_ref[...] = jnp.zeros_like(st_ref)

    for gb in range(SSM_LANE_GROUPS):
        ub = u_ref[gb].astype(BF16)
        bu_ref[...] = jnp.dot(ub, bw_ref[0, gb], preferred_element_type=F32)
        ar = jnp.broadcast_to(a_ref[0, gb, 0:1, :], (bsz, ns))
        ai = jnp.broadcast_to(a_ref[0, gb, 1:2, :], (bsz, ns))

        def step(j, carry, ar=ar, ai=ai):
            xr, xi = carry
            tl = jnp.where(d == 0, j, ts - 1 - j)
            r = pl.multiple_of(tl * bsz, bsz)
            nr = ar * xr - ai * xi + bu_ref[pl.ds(r, bsz), :ns]
            ni = ar * xi + ai * xr + bu_ref[pl.ds(r, bsz), ns:]
            bu_ref[pl.ds(r, bsz), :ns] = nr
            bu_ref[pl.ds(r, bsz), ns:] = ni
            return nr, ni

        xr, xi = lax.fori_loop(0, ts, step, (st_ref[gb, :, :ns], st_ref[gb, :, ns:]))
        st_ref[gb, :, :ns] = xr
        st_ref[gb, :, ns:] = xi
        y_ref[0, :, gb * LANES:(gb + 1) * LANES] = jnp.dot(
            bu_ref[...].astype(BF16), cw_ref[0, gb], preferred_element_type=F32)


def _ssm_scan(u_tm, a, bw, cw, *, bsz, ts=64):
    slabs, rows, _ = u_tm.shape
    width = slabs * LANES
    seq = rows // bsz
    ts = min(ts, seq)
    nt = seq // ts
    tblk = lambda d, i: jnp.where(d == 0, i, nt - 1 - i)
    return pl.pallas_call(
        functools.partial(_ssm_kernel, ts=ts, bsz=bsz),
        grid=(2, nt),
        in_specs=[
            pl.BlockSpec((slabs, ts * bsz, LANES), lambda d, i: (0, tblk(d, i), 0)),
            pl.BlockSpec((1,) + a.shape[1:], lambda d, i: (d, 0, 0, 0)),
            pl.BlockSpec((1,) + bw.shape[1:], lambda d, i: (d, 0, 0, 0)),
            pl.BlockSpec((1,) + cw.shape[1:], lambda d, i: (d, 0, 0, 0)),
        ],
        out_specs=pl.BlockSpec((1, ts * bsz, width), lambda d, i: (d, tblk(d, i), 0)),
        out_shape=jax.ShapeDtypeStruct((2, rows, width), F32),
        scratch_shapes=[pltpu.VMEM((SSM_LANE_GROUPS, bsz, 2 * SSM_GB_STATES), F32),
                        pltpu.VMEM((ts * bsz, 2 * SSM_GB_STATES), F32)],
        compiler_params=_cparams(("arbitrary", "arbitrary"), VMEM_LIMIT),
        name="ssm_scan",
    )(u_tm, a, bw, cw)


def _ssm_post_kernel(y_ref, u_ref, d_ref, wg_ref, g_ref, wo_ref, z_ref, slab_ref, *, bsz):
    u = jnp.concatenate([u_ref[c] for c in range(u_ref.shape[0])], axis=-1)
    y = _gelu(y_ref[0] + y_ref[1] + d_ref[...] * u).astype(BF16)
    ab = jnp.dot(y, wg_ref[...], preferred_element_type=F32)
    ssm = ab[:, :SSM_WIDTH] * jax.nn.sigmoid(ab[:, SSM_WIDTH:])
    n = _rms(ssm, g_ref[...]).astype(BF16)
    z = jnp.dot(n, wo_ref[...], preferred_element_type=F32)
    nt, dm = z.shape[0] // bsz, z.shape[1]
    for c in range(dm // LANES):
        slab_ref[c] = z[:, c * LANES:(c + 1) * LANES]
    for b in range(bsz):
        for c in range(dm // LANES):
            z_ref[:, b * dm + c * LANES:b * dm + (c + 1) * LANES] = slab_ref[c, pl.ds(b, nt, stride=bsz), :]


def _ssm_post(y, u_tm, d_skip, w_glu_bf16, g_ssm, w_out_ssm_bf16, *, bsz, tr=512):
    slabs, rows, _ = u_tm.shape
    width = slabs * LANES
    tr = min(tr, rows)
    assert tr % bsz == 0
    dm = w_out_ssm_bf16.shape[1]
    full = lambda a: pl.BlockSpec(a.shape, lambda i: (0,) * a.ndim)
    d2, g2 = d_skip.reshape(1, width), g_ssm.reshape(1, width)
    return pl.pallas_call(
        functools.partial(_ssm_post_kernel, bsz=bsz),
        grid=(rows // tr,),
        in_specs=[pl.BlockSpec((2, tr, width), lambda i: (0, i, 0)),
                  pl.BlockSpec((slabs, tr, LANES), lambda i: (0, i, 0)),
                  full(d2), full(w_glu_bf16), full(g2), full(w_out_ssm_bf16)],
        out_specs=pl.BlockSpec((tr // bsz, bsz * dm), lambda i: (i, 0)),
        out_shape=jax.ShapeDtypeStruct((rows // bsz, bsz * dm), F32),
        scratch_shapes=[pltpu.VMEM((dm // LANES, tr, LANES), F32)],
        compiler_params=_cparams(("parallel",), VMEM_LIMIT),
        name="ssm_post",
    )(y, u_tm, d2, w_glu_bf16, g2, w_out_ssm_bf16)


def _mix_out_kernel(x_ref, z_ref, a_ref, ga_ref, wo_ref, gf_ref, wq_ref, xn_ref, h_ref, q_ref):
    n = _rms(a_ref[0], ga_ref[...]).astype(BF16)
    xn = x_ref[0] + z_ref[...] + jnp.dot(n, wo_ref[...], preferred_element_type=F32)
    xn_ref[0] = xn
    h = _rms(xn, gf_ref[...])
    for c in range(SUBLANES):
        h_ref[pl.ds(c, h.shape[0], stride=SUBLANES), :] = h[:, c * LANES:(c + 1) * LANES]
    qp = jnp.dot(h.astype(BF16), wq_ref[...], preferred_element_type=F32)
    for hd in range(PEER_HEADS):
        q_ref[hd] = qp[:, hd * LANES:(hd + 1) * LANES]


def _mix_out(x, z_tm, attn, g_attn, w_out_attn_bf16, norm_ffn, w_query, *, ts=256):
    bsz, seq, dm = x.shape
    ts = min(ts, seq)
    ns = seq // ts
    row = pl.BlockSpec((1, ts, dm), lambda b, i: (b, i, 0))
    half = pl.BlockSpec((1, ts, ATTN_WIDTH), lambda b, i: (b, i, 0))
    full = lambda a: pl.BlockSpec(a.shape, lambda b, i: (0,) * a.ndim)
    ga, gf = g_attn.reshape(1, ATTN_WIDTH), norm_ffn.reshape(1, dm)
    qdim = w_query.shape[1] // PEER_HEADS
    return pl.pallas_call(
        _mix_out_kernel,
        grid=(bsz, ns),
        in_specs=[row, pl.BlockSpec((ts, dm), lambda b, i: (i, b)), half,
                  full(ga), full(w_out_attn_bf16), full(gf), full(w_query)],
        out_specs=[row, pl.BlockSpec((ts * SUBLANES, LANES), lambda b, i: (b * ns + i, 0)),
                   pl.BlockSpec((PEER_HEADS, ts, qdim), lambda b, i: (0, b * ns + i, 0))],
        out_shape=[jax.ShapeDtypeStruct((bsz, seq, dm), F32),
                   jax.ShapeDtypeStruct((bsz * seq * SUBLANES, LANES), F32),

                   jax.ShapeDtypeStruct((PEER_HEADS, bsz * seq, qdim), F32)],
        compiler_params=_cparams(("parallel", "arbitrary"), VMEM_LIMIT),
        name="mix_out",
    )(x, z_tm, attn, ga, w_out_attn_bf16, gf, w_query)


TOPK_TOKENS = SUBLANES * LANES
KEY_PITCH = PEER_KEYS + 4
_CANDIDATES = tuple((a, b) for a in range(PEER_TOPK) for b in range(PEER_TOPK) if (a + 1) * (b + 1) <= PEER_TOPK)


def _tree(op, xs):
    xs = list(xs)
    while len(xs) > 1:
        xs = [op(xs[i], xs[i + 1]) if i + 1 < len(xs) else xs[i] for i in range(0, len(xs), 2)]
    return xs[0]


def _extract16(problems):
    big = jnp.int32(2 ** 30)
    ninf = jnp.float32(-jnp.inf)

    def step(r, ms):
        nxt = []
        for p, m in zip(problems, ms):
            s_ref, order = p["s"], p["order"]
            n = len(order)
            am = _tree(jnp.minimum, [jnp.where(s_ref[k] == m, order[k], big) for k in range(n)])
            news, pays = [], []
            for k in range(n):
                hit = am == order[k]
                nk = jnp.where(hit, ninf, s_ref[k])
                s_ref[k] = nk
                news.append(nk)
                if p.get("pay") is not None:
                    pays.append(jnp.where(hit, p["pay"][k], -1))
            p["vals"][r] = m
            p["picks"][r] = _tree(jnp.maximum, pays) if pays else am
            nxt.append(_tree(jnp.maximum, news))
        return tuple(nxt)

    init = tuple(_tree(jnp.maximum, [p["s"][k] for k in range(len(p["order"]))]) for p in problems)
    lax.fori_loop(0, PEER_TOPK, step, init)


def _peer_topk_kernel(q_ref, k_ref, idx_ref, gate_ref,
                      slab_ref, s1_ref, s2_ref, t1_ref, i1_ref, t2_ref, i2_ref, cand_ref, pay_ref, ts_ref, ex_ref):
    dn = (((1,), (1,)), ((), ()))
    keys = tuple(range(PEER_KEYS))

    def head(h, carry):
        for w, s_ref in ((0, s1_ref), (1, s2_ref)):
            for j in range(SUBLANES):
                slab_ref[j * KEY_PITCH:j * KEY_PITCH + PEER_KEYS, :] = lax.dot_general(
                    k_ref[w, h], q_ref[h, j * LANES:(j + 1) * LANES, :].astype(BF16), dn,
                    preferred_element_type=F32)
            for k in range(PEER_KEYS):
                s_ref[k] = slab_ref[pl.ds(k, SUBLANES, stride=KEY_PITCH), :]
        _extract16([dict(s=s1_ref, order=keys, vals=t1_ref, picks=i1_ref),
                    dict(s=s2_ref, order=keys, vals=t2_ref, picks=i2_ref)])
        for c, (a, b) in enumerate(_CANDIDATES):
            cand_ref[c] = t1_ref[a] + t2_ref[b]
            pay_ref[c] = i1_ref[a] * PEER_KEYS + i2_ref[b]
        _extract16([dict(s=cand_ref, order=tuple(a * PEER_TOPK + b for a, b in _CANDIDATES), pay=pay_ref,
                         vals=ts_ref, picks=ex_ref)])
        top_s = ts_ref[...]
        e = jnp.exp(top_s - jnp.max(top_s, axis=0, keepdims=True))
        gate_ref[0, h] = e / jnp.sum(e, axis=0, keepdims=True)
        idx_ref[0, h] = ex_ref[...] * ROWS_PER_EXPERT
        return carry

    lax.fori_loop(0, PEER_HEADS, head, 0)


def _peer_topk(qp, keys_pad):
    n_tok = qp.shape[1]
    tt = TOPK_TOKENS
    assert n_tok % tt == 0
    shp = (n_tok // tt, PEER_HEADS, PEER_TOPK, SUBLANES, LANES)
    out = pl.BlockSpec((1,) + shp[1:], lambda i: (i, 0, 0, 0, 0))
    vregs = lambda n, dt: pltpu.VMEM((n, SUBLANES, LANES), dt)
    idx, gate = pl.pallas_call(
        _peer_topk_kernel,
        grid=(n_tok // tt,),
        in_specs=[pl.BlockSpec((PEER_HEADS, tt, qp.shape[2]), lambda i: (0, i, 0)),
                  pl.BlockSpec(keys_pad.shape, lambda i: (0, 0, 0, 0))],
        out_specs=[out, out],
        out_shape=[jax.ShapeDtypeStruct(shp, jnp.int32), jax.ShapeDtypeStruct(shp, F32)],
        scratch_shapes=[pltpu.VMEM((SUBLANES * KEY_PITCH, LANES), F32),
                        vregs(PEER_KEYS, F32), vregs(PEER_KEYS, F32),
                        vregs(PEER_TOPK, F32), vregs(PEER_TOPK, jnp.int32),
                        vregs(PEER_TOPK, F32), vregs(PEER_TOPK, jnp.int32),
                        vregs(len(_CANDIDATES), F32), vregs(len(_CANDIDATES), jnp.int32),
                        vregs(PEER_TOPK, F32), vregs(PEER_TOPK, jnp.int32)],
        compiler_params=_cparams(("parallel",), VMEM_LIMIT),
        name="peer_topk",
    )(qp, keys_pad)
    to_tok = lambda a: jnp.transpose(a, (0, 3, 4, 1, 2)).reshape(n_tok, PEER_SEL)
    return to_tok(idx), to_tok(gate)


ROWS_PER_EXPERT = 4
PAIR_CHUNK = 32
CHUNK_ROWS = PAIR_CHUNK * ROWS_PER_EXPERT
SMEM_GROUP = 8
N_CHUNKS = PEER_SEL // PAIR_CHUNK
CHUNK_SHIFT = N_CHUNKS.bit_length() - 1
assert 1 << CHUNK_SHIFT == N_CHUNKS
_GROUP_SLOT = tuple(
    2 * (2 * (0 if (j % 2) else 1) + (1 if (j // 2) in (0, 2) else 0)) + (0 if (j // 2) < 2 else 1)
    for j in range(8))
_CHUNK_SLOT = tuple(8 * (j // 8) + _GROUP_SLOT[j % 8] for j in range(PAIR_CHUNK))


def _pack_table_kernel(t_ref, o_ref):
    rows, dm = t_ref.shape
    for c in range(ROWS_PER_EXPERT):
        lo = lax.bitcast_convert_type(t_ref[:, c * LANES:(c + 1) * LANES].astype(BF16).astype(F32), jnp.int32)
        hi = lax.bitcast_convert_type(
            t_ref[:, dm // 2 + c * LANES:dm // 2 + (c + 1) * LANES].astype(BF16).astype(F32), jnp.int32)
        o_ref[pl.ds(c, rows, stride=ROWS_PER_EXPERT), :] = lax.shift_right_logical(lo, 16) | hi


def _pack_table(tables, layer, *, rows=512):
    _, e, dm = tables.shape
    assert dm == 2 * ROWS_PER_EXPERT * LANES
    rows = min(rows, e)
    return pl.pallas_call(
        _pack_table_kernel,
        grid=(e // rows,),
        in_specs=[pl.BlockSpec((None, rows, dm), lambda i: (layer, i, 0))],
        out_specs=pl.BlockSpec((rows * ROWS_PER_EXPERT, LANES), lambda i: (i, 0)),
        out_shape=jax.ShapeDtypeStruct((e * ROWS_PER_EXPERT, LANES), jnp.int32),
        compiler_params=_cparams(("parallel",), VMEM_LIMIT),
        name="pack_table",
    )(tables)


def _unpack_words(w):
    lo = lax.bitcast_convert_type(w << 16, F32)
    hi = lax.bitcast_convert_type(w & jnp.int32(-65536), F32)
    return lo, hi


def _gather_chunk(idx_ref, tab_ref, buf_ref, c, group, slots):
    for g in range(PAIR_CHUNK // group):
        ids = idx_ref.at[pl.ds(pl.multiple_of(c * PAIR_CHUNK + g * group, group), group)]
        for i in range(group):
            e4 = pl.multiple_of(ids[i], ROWS_PER_EXPERT)
            s = slots[g * group + i] * ROWS_PER_EXPERT
            buf_ref[s:s + ROWS_PER_EXPERT, :] = tab_ref[pl.ds(e4, ROWS_PER_EXPERT), :]


def _chunk_loop(n_chunks, idx_ref, tab_ref, buf_a, buf_b, consume, init, group, slots, per_step=2, extra=None):
    last = n_chunks - 1
    gather = functools.partial(_gather_chunk, idx_ref, tab_ref, group=group, slots=slots)
    gather(buf_a, 0)
    bufs = (buf_a, buf_b)

    def body(i, carry):
        if extra is not None:
            extra(i)
        c = per_step * i
        for u in range(per_step):
            carry = consume(c + u, bufs[u % 2], carry)
            nxt = c + u + 1
            gather(bufs[(u + 1) % 2], jnp.minimum(nxt, last) if u == per_step - 1 else nxt)
        return carry

    return lax.fori_loop(0, n_chunks // per_step, body, init)


def _peer_u_kernel(idx_ref, h_ref, gate_ref, tab_ref, coef_ref, buf_a, buf_b, r_even, r_odd, d_ref, *, tt, n_tiles):
    step = pl.program_id(0)
    sub = lax.broadcasted_iota(jnp.int32, (SUBLANES, LANES), 0)
    low = sub < ROWS_PER_EXPERT
    m_a = ((sub % 4) >= 2)[None]
    m_b = ((sub % 2) == 1)[None]
    nv = PAIR_CHUNK // 2

    def pair_partials(r_ref, c, buf_ref, carry):
        t = c >> CHUNK_SHIFT
        h8 = h_ref[pl.ds(pl.multiple_of(t * SUBLANES, SUBLANES), SUBLANES), :]
        sw = pltpu.roll(h8, ROWS_PER_EXPERT, axis=0)
        ha = jnp.where(low, h8, sw)[None]
        hb = jnp.where(low, sw, h8)[None]
        lo, hi = _unpack_words(buf_ref[...])
        x = lo.reshape(nv, SUBLANES, LANES) * ha + hi.reshape(nv, SUBLANES, LANES) * hb
        x = x.reshape(nv // 2, 2, SUBLANES, LANES)
        xe, xo = x[:, 0], x[:, 1]
        a = xe + pltpu.roll(xe, 2, axis=1)
        b = xo + pltpu.roll(xo, 6, axis=1)
        m = jnp.where(m_a, a, b).reshape(nv // 4, 2, SUBLANES, LANES)
        me, mo = m[:, 0], m[:, 1]
        a2 = me + pltpu.roll(me, 1, axis=1)
        b2 = mo + pltpu.roll(mo, 7, axis=1)
        r_ref[pl.ds(pl.multiple_of(c * PAIR_CHUNK, PAIR_CHUNK), PAIR_CHUNK), :] = (
            jnp.where(m_b, a2, b2).reshape(PAIR_CHUNK, LANES))
        return carry

    def token_sum(r_prev, t):
        r_t = r_prev[pl.ds(pl.multiple_of(t * PEER_SEL, PEER_SEL), PEER_SEL), :]
        d_ref[pl.ds(t, 1), :] = jnp.sum(r_t.T, axis=0, keepdims=True)

    def loop(r_cur, **kw):
        _chunk_loop(tt * N_CHUNKS, idx_ref, tab_ref, buf_a, buf_b, functools.partial(pair_partials, r_cur), 0,
                    SMEM_GROUP, _CHUNK_SLOT, **kw)

    @pl.when(step == 0)
    def _():
        loop(r_even)

    for parity, (r_cur, r_prev) in enumerate(((r_even, r_odd), (r_odd, r_even))):
        @pl.when(jnp.logical_and(jnp.logical_and(step > 0, step < n_tiles), step % 2 == parity))
        def _(r_cur=r_cur, r_prev=r_prev):
            loop(r_cur, per_step=N_CHUNKS, extra=functools.partial(token_sum, r_prev))

    @pl.when(step == n_tiles)
    def _():
        def only_sums(t, carry):
            token_sum(r_even if n_tiles % 2 else r_odd, t)
            return carry
        lax.fori_loop(0, tt, only_sums, 0)

    @pl.when(step > 0)
    def _():
        coef_ref[...] = gate_ref[...] * _gelu(d_ref[...])


def _peer_u(idx4, h8, gates, tab, *, tt=128):
    n_tok = gates.shape[0]
    tt = min(tt, n_tok)
    n_tiles = n_tok // tt
    cur = lambda i: jnp.minimum(i, n_tiles - 1)
    done = lambda i: jnp.maximum(i - 1, 0)
    return pl.pallas_call(
        functools.partial(_peer_u_kernel, tt=tt, n_tiles=n_tiles),
        grid=(n_tiles + 1,),
        in_specs=[pl.BlockSpec((tt * PEER_SEL,), lambda i: (cur(i),), memory_space=pltpu.SMEM),
                  pl.BlockSpec((tt * SUBLANES, LANES), lambda i: (cur(i), 0)),
                  pl.BlockSpec((tt, PEER_SEL), lambda i: (done(i), 0)),
                  pl.BlockSpec(tab.shape, lambda i: (0, 0), pipeline_mode=pl.Buffered(1))],
        out_specs=pl.BlockSpec((tt, PEER_SEL), lambda i: (done(i), 0)),
        out_shape=jax.ShapeDtypeStruct((n_tok, PEER_SEL), F32),
        scratch_shapes=[pltpu.VMEM((CHUNK_ROWS, LANES), jnp.int32),
                        pltpu.VMEM((CHUNK_ROWS, LANES), jnp.int32),
                        pltpu.VMEM((tt * PEER_SEL, LANES), F32),
                        pltpu.VMEM((tt * PEER_SEL, LANES), F32),
                        pltpu.VMEM((tt, PEER_SEL), F32)],
        compiler_params=_cparams(("arbitrary",), PEER_U_VMEM_LIMIT),
        name="peer_u",
    )(idx4, h8, gates, tab)


def _peer_v_kernel(idx_ref, coef_ref, tab_ref, o_ref, buf_a, buf_b, *, tt):
    sub = lax.broadcasted_iota(jnp.int32, (SUBLANES, LANES), 0)
    low = sub < ROWS_PER_EXPERT
    nv = PAIR_CHUNK // 2

    def accumulate(c, buf_ref, acc):
        acc_lo, acc_hi = acc
        lo, hi = _unpack_words(buf_ref[...])
        lo = lo.reshape(nv, SUBLANES, LANES)
        hi = hi.reshape(nv, SUBLANES, LANES)
        cs = coef_ref.at[pl.ds(pl.multiple_of(c * nv, nv), nv)]
        for v in range(nv):
            c_lo, c_hi = _unpack_words(jnp.full((SUBLANES, LANES), cs[v], jnp.int32))
            cv = jnp.where(low, c_lo, c_hi)
            acc_lo = acc_lo + cv * lo[v]
            acc_hi = acc_hi + cv * hi[v]
        lo4 = acc_lo + pltpu.roll(acc_lo, ROWS_PER_EXPERT, axis=0)
        hi4 = acc_hi + pltpu.roll(acc_hi, ROWS_PER_EXPERT, axis=0)
        o_ref[c >> CHUNK_SHIFT] = jnp.where(low, lo4, hi4)
        last = (c & (N_CHUNKS - 1)) == N_CHUNKS - 1
        return jnp.where(last, 0.0, acc_lo), jnp.where(last, 0.0, acc_hi)

    z = jnp.zeros((SUBLANES, LANES), F32)
    _chunk_loop(tt * N_CHUNKS, idx_ref, tab_ref, buf_a, buf_b, accumulate, (z, z), PAIR_CHUNK,
                tuple(range(PAIR_CHUNK)))


def _pack_coefs(coef):
    n_tok = coef.shape[0]
    bits = lax.bitcast_convert_type(coef.astype(BF16), jnp.uint16).astype(jnp.uint32)
    bits = bits.reshape(n_tok, PEER_SEL // 2, 2)
    w = bits[:, :, 0] | (bits[:, :, 1] << 16)
    return lax.bitcast_convert_type(w, jnp.int32).reshape(n_tok * (PEER_SEL // 2))


def _peer_v(idx4, coef_words, tab, *, tt=128):
    n_tok = idx4.shape[0] // PEER_SEL
    tt = min(tt, n_tok)
    smem = pl.BlockSpec((tt * PEER_SEL,), lambda i: (i,), memory_space=pltpu.SMEM)
    return pl.pallas_call(
        functools.partial(_peer_v_kernel, tt=tt),
        grid=(n_tok // tt,),
        in_specs=[smem, pl.BlockSpec((tt * PEER_SEL // 2,), lambda i: (i,), memory_space=pltpu.SMEM),
                  pl.BlockSpec(tab.shape, lambda i: (0, 0), pipeline_mode=pl.Buffered(1))],
        out_specs=pl.BlockSpec((tt, SUBLANES, LANES), lambda i: (i, 0, 0)),
        out_shape=jax.ShapeDtypeStruct((n_tok, SUBLANES, LANES), F32),
        scratch_shapes=[pltpu.VMEM((CHUNK_ROWS, LANES), jnp.int32),
                        pltpu.VMEM((CHUNK_ROWS, LANES), jnp.int32)],
        compiler_params=_cparams(("arbitrary",), VMEM_LIMIT),
        name="peer_v",
    )(idx4, coef_words, tab)


def _peer(h8, qp, sub_keys, tab_u, tab_v):
    n_tok = qp.shape[1]
    half = sub_keys.shape[-1]
    z = jnp.zeros_like(sub_keys[0])
    keys_pad = jnp.stack([jnp.concatenate([sub_keys[0], z], axis=-1),
                          jnp.concatenate([z, sub_keys[1]], axis=-1)]).astype(BF16)
    assert keys_pad.shape[-1] == 2 * half == qp.shape[-1]
    idx4, gates = _peer_topk(qp, keys_pad)
    idx4 = idx4.reshape(n_tok * PEER_SEL)
    coef = _peer_u(idx4, h8, gates, tab_u)
    out = _peer_v(idx4, _pack_coefs(coef), tab_v)
    return out.reshape(n_tok * SUBLANES, LANES)


def _final_kernel(x_ref, p_ref, g_ref, o_ref):
    o_ref[...] = _rms(x_ref[...] + _rows_from_chunks(p_ref, x_ref.shape[0]), g_ref[...])


def _final_norm(x, p, g, *, tr=1024):
    rows, dm = x.shape
    tr = min(tr, rows)
    blk = pl.BlockSpec((tr, dm), lambda i: (i, 0))
    return pl.pallas_call(
        _final_kernel,
        grid=(rows // tr,),
        in_specs=[blk, pl.BlockSpec((tr * SUBLANES, LANES), lambda i: (i, 0)),
                  pl.BlockSpec((1, dm), lambda i: (0, 0))],
        out_specs=blk,
        out_shape=jax.ShapeDtypeStruct((rows, dm), F32),
        compiler_params=_cparams(("parallel",), VMEM_LIMIT),
        name="final_norm",
    )(x, p, g.reshape(1, dm))


def kernel(x, w_in, w_out, rel_bias, g_attn, g_ssm, norm_mix, norm_ffn, lam_re, lam_im, log_step, b_re, b_im, c_re, c_im, d_skip, w_glu, w_query, sub_keys, expert_u, expert_v, norm_final):
    bsz, seq, dm = x.shape
    depth = w_in.shape[0]
    prev = None
    biases = [_attn_bias_tables(rel_bias, d) for _, d in DILATED_PATTERNS]
    for l in range(depth):
        x, q, k, v, u_tm = _in_proj(x, prev, norm_mix[l], w_in[l].astype(BF16))
        attn = _attention(q, k, v, biases)
        a, bw, cw = _ssm_params(lam_re[l], lam_im[l], log_step[l], b_re[l], b_im[l], c_re[l], c_im[l])
        y = _ssm_scan(u_tm, a, bw, cw, bsz=bsz)
        wo = w_out[l].astype(BF16)
        z = _ssm_post(y, u_tm, d_skip[l], w_glu[l].astype(BF16), g_ssm[l], wo[ATTN_WIDTH:], bsz=bsz)
        x, h, qp = _mix_out(x, z, attn,
                            g_attn[l], wo[:ATTN_WIDTH], norm_ffn[l], w_query[l].astype(BF16))
        prev = _peer(h, qp, sub_keys[l], _pack_table(expert_u, l), _pack_table(expert_v, l))
    out = _final_norm(x.reshape(bsz * seq, dm), prev, norm_final)
    return out.reshape(bsz, seq, dm)
```

```python
import functools
import math

import numpy as np
import jax
import jax.numpy as jnp
from jax import lax
from jax.experimental import pallas as pl
from jax.experimental.pallas import tpu as pltpu

F32 = jnp.float32
BF16 = jnp.bfloat16

EPS = 1e-6
NEG_INF = -1e30
HEAD_DIM = 64
ATTN_WIDTH = 512
SSM_WIDTH = 512
SSM_GROUP = 16
SSM_STATE = 64
DILATED_PATTERNS = ((128, 1), (512, 4), (2048, 16))
REL_BUCKETS = 32
REL_MAX_DISTANCE = 1024
PEER_HEADS = 8
PEER_KEYS = 128
PEER_TOPK = 16
PEER_SEL = PEER_HEADS * PEER_TOPK

LANES = 128
SUBLANES = 8
QBLK = 128
KWIN = 256
BAND = 64
VMEM_LIMIT = 52 * 1024 * 1024
PEER_U_VMEM_LIMIT = 58 * 1024 * 1024


def _cparams(sem, vmem=None):
    return pltpu.CompilerParams(dimension_semantics=sem, vmem_limit_bytes=vmem)


def _rms(x, g):
    return x * lax.rsqrt(jnp.mean(x * x, axis=-1, keepdims=True) + EPS) * g


def _gelu(x):
    return 0.5 * x * (1.0 + lax.erf(x * (1.0 / math.sqrt(2.0))))


def _rows_from_chunks(p_ref, n_rows):
    return jnp.concatenate([p_ref[pl.ds(c, n_rows, stride=SUBLANES), :] for c in range(SUBLANES)], axis=-1)


def _in_proj_kernel(*refs, has_prev):
    if has_prev:
        x_ref, p_ref, g_ref, w_ref, xo_ref, q_ref, k_ref, v_ref, u_ref = refs
        x = x_ref[0] + _rows_from_chunks(p_ref, x_ref.shape[1])
    else:
        x_ref, g_ref, w_ref, xo_ref, q_ref, k_ref, v_ref, u_ref = refs
        x = x_ref[0]
    xo_ref[0] = x
    h = _rms(x, g_ref[...]).astype(BF16)
    proj = jnp.dot(h, w_ref[...], preferred_element_type=F32)
    a = ATTN_WIDTH
    q_ref[0] = proj[:, :a] * (HEAD_DIM ** -0.5)
    k_ref[0] = proj[:, a:2 * a]
    v_ref[0] = proj[:, 2 * a:3 * a]
    b, bsz = pl.program_id(1), pl.num_programs(1)
    for c in range(SSM_WIDTH // LANES):
        u_ref[c, pl.ds(b, proj.shape[0], stride=bsz), :] = proj[:, 3 * a + c * LANES:3 * a + (c + 1) * LANES]


def _in_proj(x, prev, g, w_bf16, *, ts=256):
    bsz, seq, dm = x.shape
    ts = min(ts, seq)
    row = pl.BlockSpec((1, ts, dm), lambda i, b: (b, i, 0))
    qkv = pl.BlockSpec((1, ts, ATTN_WIDTH), lambda i, b: (b, i, 0))
    ns = seq // ts
    chunks = pl.BlockSpec((ts * SUBLANES, LANES), lambda i, b: (b * ns + i, 0))
    ins = [x] + ([prev] if prev is not None else []) + [g.reshape(1, dm), w_bf16]
    in_specs = [row] + ([chunks] if prev is not None else []) + [
        pl.BlockSpec((1, dm), lambda i, b: (0, 0)),
        pl.BlockSpec(w_bf16.shape, lambda i, b: (0, 0)),
    ]
    slabs = SSM_WIDTH // LANES
    return pl.pallas_call(
        functools.partial(_in_proj_kernel, has_prev=prev is not None),
        grid=(seq // ts, bsz),
        in_specs=in_specs,
        out_specs=[row, qkv, qkv, qkv, pl.BlockSpec((slabs, ts * bsz, LANES), lambda i, b: (0, i, 0))],
        out_shape=[
            jax.ShapeDtypeStruct((bsz, seq, dm), F32),
            jax.ShapeDtypeStruct((bsz, seq, ATTN_WIDTH), F32),
            jax.ShapeDtypeStruct((bsz, seq, ATTN_WIDTH), F32),
            jax.ShapeDtypeStruct((bsz, seq, ATTN_WIDTH), F32),
            jax.ShapeDtypeStruct((slabs, seq * bsz, LANES), F32),
        ],
        compiler_params=_cparams(("parallel", "arbitrary"), VMEM_LIMIT),
        name="in_proj",
    )(*ins)


def _t5_buckets(rel):
    half = REL_BUCKETS // 2
    max_exact = half // 2
    n = np.abs(rel)
    large = max_exact + (np.log(np.maximum(n, 1) / max_exact)
                         / np.log(REL_MAX_DISTANCE / max_exact) * (half - max_exact)).astype(np.int32)
    large = np.minimum(large, half - 1)
    return (np.where(rel > 0, half, 0) + np.where(n < max_exact, n, large)).astype(np.int32)


def _attn_bias_tables(rel_bias, dilation):
    ql = np.arange(QBLK)[:, None]
    kl = np.arange(KWIN)[None, :]
    delta = np.stack([kl + off - ql for off in (0, -BAND, -2 * BAND)])
    buckets = np.where(np.abs(delta) <= BAND, _t5_buckets(delta * dilation), -1)
    rb = rel_bias.astype(F32).T
    bk = jnp.asarray(buckets, jnp.int32)[None]
    tab = jnp.full((rb.shape[0],) + buckets.shape, NEG_INF, F32)
    for b in range(REL_BUCKETS):
        tab = jnp.where(bk == b, rb[:, b][:, None, None, None], tab)
    return tab


BLOCKS_PER_STEP = 4


def _attn_kernel(q_ref, k_ref, v_ref, *rest, seq):
    bias_refs, (o_ref, acc_ref, m_ref, z_ref) = rest[:len(DILATED_PATTERNS)], rest[len(DILATED_PATTERNS):]
    lane = lax.broadcasted_iota(jnp.int32, (QBLK, LANES), 1)
    is_h0 = lane < HEAD_DIM
    dn = (((1,), (1,)), ((), ()))
    nsteps = seq // QBLK

    def rows(start, size, d):
        return pl.ds(start, size) if d == 1 else pl.ds(start, size, stride=d)

    def block(n, d, bias_ref, first, last):
        length = seq // d
        nblk = length // QBLK
        r, i = n >> (nblk.bit_length() - 1), n & (nblk - 1)
        s = i * QBLK
        ks = jnp.clip(s - BAND, 0, length - KWIN)
        var = jnp.where(i == 0, 0, jnp.where(i == nblk - 1, 2, 1))
        q_rows = rows(r + d * s, QBLK, d)
        k_rows = rows(r + d * ks, KWIN, d)
        qb = q_ref[q_rows, :].astype(BF16)
        kb = k_ref[k_rows, :].astype(BF16)
        vb = v_ref[k_rows, :].astype(BF16)
        outs, ms, zs = [], [], []
        for h in range(2):
            keep = is_h0 if h == 0 else jnp.logical_not(is_h0)
            qh = jnp.where(keep, qb, jnp.zeros_like(qb))
            logits = lax.dot_general(qh, kb, dn, preferred_element_type=F32) + bias_ref[h, var]
            m = jnp.max(logits, axis=-1, keepdims=True)
            p = jnp.exp(logits - m)
            outs.append(jnp.dot(p.astype(BF16), vb, preferred_element_type=F32))
            ms.append(jnp.broadcast_to(m, (QBLK, LANES)))
            zs.append(jnp.broadcast_to(jnp.sum(p, axis=-1, keepdims=True), (QBLK, LANES)))
        o = jnp.where(is_h0, outs[0], outs[1])
        m = jnp.where(is_h0, ms[0], ms[1])
        z = jnp.where(is_h0, zs[0], zs[1])
        if not first:
            m_old = m_ref[q_rows, :]
            m_new = jnp.maximum(m_old, m)
            a, b = jnp.exp(m_old - m_new), jnp.exp(m - m_new)
            o = acc_ref[q_rows, :] * a + o * b
            z = z_ref[q_rows, :] * a + z * b
            m = m_new
        if last:
            o_ref[q_rows, :] = o / z
        else:
            acc_ref[q_rows, :] = o
            m_ref[q_rows, :] = m
            z_ref[q_rows, :] = z

    for p, ((_, d), bias_ref) in enumerate(zip(DILATED_PATTERNS, bias_refs)):
        def step(g, carry, d=d, bias_ref=bias_ref, p=p):
            for j in range(BLOCKS_PER_STEP):
                block(g * BLOCKS_PER_STEP + j, d, bias_ref, p == 0, p == len(DILATED_PATTERNS) - 1)
            return carry
        lax.fori_loop(0, nsteps // BLOCKS_PER_STEP, step, 0)


def _attention(q, k, v, biases):
    bsz, seq, width = q.shape
    for _, d in DILATED_PATTERNS:
        length = seq // d
        assert length >= KWIN and length % QBLK == 0 and (length // QBLK) & (length // QBLK - 1) == 0
    assert (seq // QBLK) % BLOCKS_PER_STEP == 0
    blk = pl.BlockSpec((None, seq, LANES), lambda b, c: (b, 0, c))
    bias_spec = pl.BlockSpec((2, 3, QBLK, KWIN), lambda b, c: (c, 0, 0, 0))
    return pl.pallas_call(
        functools.partial(_attn_kernel, seq=seq),
        grid=(bsz, width // LANES),
        in_specs=[blk, blk, blk] + [bias_spec] * len(biases),
        out_specs=blk,
        out_shape=jax.ShapeDtypeStruct((bsz, seq, width), F32),
        scratch_shapes=[pltpu.VMEM((seq, LANES), F32)] * 3,
        compiler_params=_cparams(("parallel", "arbitrary"), VMEM_LIMIT),
        name="attention",
    )(q, k, v, *biases)


SSM_LANE_GROUPS = SSM_WIDTH // LANES
SSM_GB_STATES = (LANES // SSM_GROUP) * SSM_STATE


def _ssm_params(lam_re, lam_im, log_step, b_re, b_im, c_re, c_im):
    f = lambda t: t.astype(F32)
    lr, li = f(lam_re), f(lam_im)
    step = jnp.exp(f(log_step))[..., None]
    mag = jnp.exp(lr * step)
    ar, ai = mag * jnp.cos(li * step), mag * jnp.sin(li * step)
    nr, ni = ar - 1.0, ai
    den = lr * lr + li * li
    cr, ci = (nr * lr + ni * li) / den, (ni * lr - nr * li) / den
    br, bi = f(b_re), f(b_im)
    bbr = cr[..., None] * br - ci[..., None] * bi
    bbi = cr[..., None] * bi + ci[..., None] * br
    gpb = LANES // SSM_GROUP
    eye = jnp.eye(gpb, dtype=F32)

    def in_map(t):
        t = t.reshape(2, SSM_LANE_GROUPS, gpb, SSM_STATE, SSM_GROUP)
        return jnp.einsum('dbgpc,gh->dbgchp', t, eye).reshape(2, SSM_LANE_GROUPS, LANES, SSM_GB_STATES)

    def out_map(t):
        t = t.reshape(2, SSM_LANE_GROUPS, gpb, SSM_GROUP, SSM_STATE)
        return jnp.einsum('dbgcp,gh->dbgphc', t, eye).reshape(2, SSM_LANE_GROUPS, SSM_GB_STATES, LANES)

    bw = jnp.concatenate([in_map(bbr), in_map(bbi)], axis=-1).astype(BF16)
    cw = jnp.concatenate([out_map(f(c_re)), -out_map(f(c_im))], axis=-2).astype(BF16)
    a = jnp.stack([ar.reshape(2, SSM_LANE_GROUPS, SSM_GB_STATES),
                   ai.reshape(2, SSM_LANE_GROUPS, SSM_GB_STATES)], axis=2)
    return a, bw, cw


def _ssm_kernel(u_ref, a_ref, bw_ref, cw_ref, y_ref, st_ref, bu_ref, *, ts, bsz):
    d = pl.program_id(0)
    ns = SSM_GB_STATES

    @pl.when(pl.program_id(1) == 0)
    def _():
        st_ref[...] = jnp.zeros_like(st_ref)

    for gb in range(SSM_LANE_GROUPS):
        ub = u_ref[gb].astype(BF16)
        bu_ref[...] = jnp.dot(ub, bw_ref[0, gb], preferred_element_type=F32)
        ar = jnp.broadcast_to(a_ref[0, gb, 0:1, :], (bsz, ns))
        ai = jnp.broadcast_to(a_ref[0, gb, 1:2, :], (bsz, ns))

        def step(j, carry, ar=ar, ai=ai):
            xr, xi = carry
            tl = jnp.where(d == 0, j, ts - 1 - j)
            r = pl.multiple_of(tl * bsz, bsz)
            nr = ar * xr - ai * xi + bu_ref[pl.ds(r, bsz), :ns]
            ni = ar * xi + ai * xr + bu_ref[pl.ds(r, bsz), ns:]
            bu_ref[pl.ds(r, bsz), :ns] = nr
            bu_ref[pl.ds(r, bsz), ns:] = ni
            return nr, ni

        xr, xi = lax.fori_loop(0, ts, step, (st_ref[gb, :, :ns], st_ref[gb, :, ns:]))
        st_ref[gb, :, :ns] = xr
        st_ref[gb, :, ns:] = xi
        y_ref[0, :, gb * LANES:(gb + 1) * LANES] = jnp.dot(
            bu_ref[...].astype(BF16), cw_ref[0, gb], preferred_element_type=F32)


def _ssm_scan(u_tm, a, bw, cw, *, bsz, ts=64):
    slabs, rows, _ = u_tm.shape
    width = slabs * LANES
    seq = rows // bsz
    ts = min(ts, seq)
    nt = seq // ts
    tblk = lambda d, i: jnp.where(d == 0, i, nt - 1 - i)
    return pl.pallas_call(
        functools.partial(_ssm_kernel, ts=ts, bsz=bsz),
        grid=(2, nt),
        in_specs=[
            pl.BlockSpec((slabs, ts * bsz, LANES), lambda d, i: (0, tblk(d, i), 0)),
            pl.BlockSpec((1,) + a.shape[1:], lambda d, i: (d, 0, 0, 0)),
            pl.BlockSpec((1,) + bw.shape[1:], lambda d, i: (d, 0, 0, 0)),
            pl.BlockSpec((1,) + cw.shape[1:], lambda d, i: (d, 0, 0, 0)),
        ],
        out_specs=pl.BlockSpec((1, ts * bsz, width), lambda d, i: (d, tblk(d, i), 0)),
        out_shape=jax.ShapeDtypeStruct((2, rows, width), F32),
        scratch_shapes=[pltpu.VMEM((SSM_LANE_GROUPS, bsz, 2 * SSM_GB_STATES), F32),
                        pltpu.VMEM((ts * bsz, 2 * SSM_GB_STATES), F32)],
        compiler_params=_cparams(("arbitrary", "arbitrary"), VMEM_LIMIT),
        name="ssm_scan",
    )(u_tm, a, bw, cw)


def _ssm_post_kernel(y_ref, u_ref, d_ref, wg_ref, g_ref, wo_ref, z_ref, slab_ref, *, bsz):
    u = jnp.concatenate([u_ref[c] for c in range(u_ref.shape[0])], axis=-1)
    y = _gelu(y_ref[0] + y_ref[1] + d_ref[...] * u).astype(BF16)
    ab = jnp.dot(y, wg_ref[...], preferred_element_type=F32)
    ssm = ab[:, :SSM_WIDTH] * jax.nn.sigmoid(ab[:, SSM_WIDTH:])
    n = _rms(ssm, g_ref[...]).astype(BF16)
    z = jnp.dot(n, wo_ref[...], preferred_element_type=F32)
    nt, dm = z.shape[0] // bsz, z.shape[1]
    for c in range(dm // LANES):
        slab_ref[c] = z[:, c * LANES:(c + 1) * LANES]
    for b in range(bsz):
        for c in range(dm // LANES):
            z_ref[:, b * dm + c * LANES:b * dm + (c + 1) * LANES] = slab_ref[c, pl.ds(b, nt, stride=bsz), :]


def _ssm_post(y, u_tm, d_skip, w_glu_bf16, g_ssm, w_out_ssm_bf16, *, bsz, tr=512):
    slabs, rows, _ = u_tm.shape
    width = slabs * LANES
    tr = min(tr, rows)
    assert tr % bsz == 0
    dm = w_out_ssm_bf16.shape[1]
    full = lambda a: pl.BlockSpec(a.shape, lambda i: (0,) * a.ndim)
    d2, g2 = d_skip.reshape(1, width), g_ssm.reshape(1, width)
    return pl.pallas_call(
        functools.partial(_ssm_post_kernel, bsz=bsz),
        grid=(rows // tr,),
        in_specs=[pl.BlockSpec((2, tr, width), lambda i: (0, i, 0)),
                  pl.BlockSpec((slabs, tr, LANES), lambda i: (0, i, 0)),
                  full(d2), full(w_glu_bf16), full(g2), full(w_out_ssm_bf16)],
        out_specs=pl.BlockSpec((tr // bsz, bsz * dm), lambda i: (i, 0)),
        out_shape=jax.ShapeDtypeStruct((rows // bsz, bsz * dm), F32),
        scratch_shapes=[pltpu.VMEM((dm // LANES, tr, LANES), F32)],
        compiler_params=_cparams(("parallel",), VMEM_LIMIT),
        name="ssm_post",
    )(y, u_tm, d2, w_glu_bf16, g2, w_out_ssm_bf16)


def _mix_out_kernel(x_ref, z_ref, a_ref, ga_ref, wo_ref, gf_ref, wq_ref, xn_ref, h_ref, q_ref):
    n = _rms(a_ref[0], ga_ref[...]).astype(BF16)
    xn = x_ref[0] + z_ref[...] + jnp.dot(n, wo_ref[...], preferred_element_type=F32)
    xn_ref[0] = xn
    h = _rms(xn, gf_ref[...])
    for c in range(SUBLANES):
        h_ref[pl.ds(c, h.shape[0], stride=SUBLANES), :] = h[:, c * LANES:(c + 1) * LANES]
    qp = jnp.dot(h.astype(BF16), wq_ref[...], preferred_element_type=F32)
    for hd in range(PEER_HEADS):
        q_ref[hd] = qp[:, hd * LANES:(hd + 1) * LANES]


def _mix_out(x, z_tm, attn, g_attn, w_out_attn_bf16, norm_ffn, w_query, *, ts=256):
    bsz, seq, dm = x.shape
    ts = min(ts, seq)
    ns = seq // ts
    row = pl.BlockSpec((1, ts, dm), lambda b, i: (b, i, 0))
    half = pl.BlockSpec((1, ts, ATTN_WIDTH), lambda b, i: (b, i, 0))
    full = lambda a: pl.BlockSpec(a.shape, lambda b, i: (0,) * a.ndim)
    ga, gf = g_attn.reshape(1, ATTN_WIDTH), norm_ffn.reshape(1, dm)
    qdim = w_query.shape[1] // PEER_HEADS
    return pl.pallas_call(
        _mix_out_kernel,
        grid=(bsz, ns),
        in_specs=[row, pl.BlockSpec((ts, dm), lambda b, i: (i, b)), half,
                  full(ga), full(w_out_attn_bf16), full(gf), full(w_query)],
        out_specs=[row, pl.BlockSpec((ts * SUBLANES, LANES), lambda b, i: (b * ns + i, 0)),
                   pl.BlockSpec((PEER_HEADS, ts, qdim), lambda b, i: (0, b * ns + i, 0))],
        out_shape=[jax.ShapeDtypeStruct((bsz, seq, dm), F32),
                   jax.ShapeDtypeStruct((bsz * seq * SUBLANES, LANES), F32),

                   jax.ShapeDtypeStruct((PEER_HEADS, bsz * seq, qdim), F32)],
        compiler_params=_cparams(("parallel", "arbitrary"), VMEM_LIMIT),
        name="mix_out",
    )(x, z_tm, attn, ga, w_out_attn_bf16, gf, w_query)


TOPK_TOKENS = SUBLANES * LANES
KEY_PITCH = PEER_KEYS + 4
_CANDIDATES = tuple((a, b) for a in range(PEER_TOPK) for b in range(PEER_TOPK) if (a + 1) * (b + 1) <= PEER_TOPK)


def _tree(op, xs):
    xs = list(xs)
    while len(xs) > 1:
        xs = [op(xs[i], xs[i + 1]) if i + 1 < len(xs) else xs[i] for i in range(0, len(xs), 2)]
    return xs[0]


def _extract16(problems):
    ninf = jnp.float32(-jnp.inf)

    def better(a, b):
        gt = b[0] > a[0]
        return tuple(jnp.where(gt, y, x) for x, y in zip(a, b))

    def step(r, carry):
        for p in problems:
            s_ref, order = p["s"], p["order"]
            assert list(order) == sorted(order)
            n = len(order)
            rows = [(s_ref[k], order[k]) + ((p["pay"][k],) if p.get("pay") is not None else ())
                    for k in range(n)]
            win = _tree(better, rows)
            m, am = win[0], win[1]
            for k in range(n):
                s_ref[k] = jnp.where(am == order[k], ninf, s_ref[k])
            p["vals"][r] = m
            p["picks"][r] = win[2] if len(win) > 2 else am.astype(jnp.int32)
        return carry

    lax.fori_loop(0, PEER_TOPK, step, 0)


def _peer_topk_kernel(q_ref, k_ref, idx_ref, gate_ref,
                      slab_ref, s1_ref, s2_ref, t1_ref, i1_ref, t2_ref, i2_ref, cand_ref, pay_ref, ts_ref, ex_ref):
    dn = (((1,), (1,)), ((), ()))
    keys = tuple(range(PEER_KEYS))

    def head(h, carry):
        for w, s_ref in ((0, s1_ref), (1, s2_ref)):
            for j in range(SUBLANES):
                slab_ref[j * KEY_PITCH:j * KEY_PITCH + PEER_KEYS, :] = lax.dot_general(
                    k_ref[w, h], q_ref[h, j * LANES:(j + 1) * LANES, :].astype(BF16), dn,
                    preferred_element_type=F32)
            for k in range(PEER_KEYS):
                s_ref[k] = slab_ref[pl.ds(k, SUBLANES, stride=KEY_PITCH), :]
        _extract16([dict(s=s1_ref, order=keys, vals=t1_ref, picks=i1_ref),
                    dict(s=s2_ref, order=keys, vals=t2_ref, picks=i2_ref)])
        for c, (a, b) in enumerate(_CANDIDATES):
            cand_ref[c] = t1_ref[a] + t2_ref[b]
            pay_ref[c] = i1_ref[a] * PEER_KEYS + i2_ref[b]
        _extract16([dict(s=cand_ref, order=tuple(a * PEER_TOPK + b for a, b in _CANDIDATES), pay=pay_ref,
                         vals=ts_ref, picks=ex_ref)])
        top_s = ts_ref[...]
        e = jnp.exp(top_s - jnp.max(top_s, axis=0, keepdims=True))
        gate_ref[0, h] = e / jnp.sum(e, axis=0, keepdims=True)
        idx_ref[0, h] = ex_ref[...] * ROWS_PER_EXPERT
        return carry

    lax.fori_loop(0, PEER_HEADS, head, 0)


def _peer_topk(qp, keys_pad):
    n_tok = qp.shape[1]
    tt = TOPK_TOKENS
    assert n_tok % tt == 0
    shp = (n_tok // tt, PEER_HEADS, PEER_TOPK, SUBLANES, LANES)
    out = pl.BlockSpec((1,) + shp[1:], lambda i: (i, 0, 0, 0, 0))
    vregs = lambda n, dt: pltpu.VMEM((n, SUBLANES, LANES), dt)
    idx, gate = pl.pallas_call(
        _peer_topk_kernel,
        grid=(n_tok // tt,),
        in_specs=[pl.BlockSpec((PEER_HEADS, tt, qp.shape[2]), lambda i: (0, i, 0)),
                  pl.BlockSpec(keys_pad.shape, lambda i: (0, 0, 0, 0))],
        out_specs=[out, out],
        out_shape=[jax.ShapeDtypeStruct(shp, jnp.int32), jax.ShapeDtypeStruct(shp, F32)],
        scratch_shapes=[pltpu.VMEM((SUBLANES * KEY_PITCH, LANES), F32),
                        vregs(PEER_KEYS, F32), vregs(PEER_KEYS, F32),
                        vregs(PEER_TOPK, F32), vregs(PEER_TOPK, jnp.int32),
                        vregs(PEER_TOPK, F32), vregs(PEER_TOPK, jnp.int32),
                        vregs(len(_CANDIDATES), F32), vregs(len(_CANDIDATES), jnp.int32),
                        vregs(PEER_TOPK, F32), vregs(PEER_TOPK, jnp.int32)],
        compiler_params=_cparams(("parallel",), VMEM_LIMIT),
        name="peer_topk",
    )(qp, keys_pad)
    to_tok = lambda a: jnp.transpose(a, (0, 3, 4, 1, 2)).reshape(n_tok, PEER_SEL)
    return to_tok(idx), to_tok(gate)


ROWS_PER_EXPERT = 4
PAIR_CHUNK = 32
CHUNK_ROWS = PAIR_CHUNK * ROWS_PER_EXPERT
SMEM_GROUP = 8
N_CHUNKS = PEER_SEL // PAIR_CHUNK
CHUNK_SHIFT = N_CHUNKS.bit_length() - 1
assert 1 << CHUNK_SHIFT == N_CHUNKS
_GROUP_SLOT = tuple(
    2 * (2 * (0 if (j % 2) else 1) + (1 if (j // 2) in (0, 2) else 0)) + (0 if (j // 2) < 2 else 1)
    for j in range(8))
_CHUNK_SLOT = tuple(8 * (j // 8) + _GROUP_SLOT[j % 8] for j in range(PAIR_CHUNK))


def _pack_table_kernel(t_ref, o_ref):
    rows, dm = t_ref.shape
    for c in range(ROWS_PER_EXPERT):
        lo = lax.bitcast_convert_type(t_ref[:, c * LANES:(c + 1) * LANES].astype(BF16).astype(F32), jnp.int32)
        hi = lax.bitcast_convert_type(
            t_ref[:, dm // 2 + c * LANES:dm // 2 + (c + 1) * LANES].astype(BF16).astype(F32), jnp.int32)
        o_ref[pl.ds(c, rows, stride=ROWS_PER_EXPERT), :] = lax.shift_right_logical(lo, 16) | hi


def _pack_table(tables, layer, *, rows=512):
    _, e, dm = tables.shape
    assert dm == 2 * ROWS_PER_EXPERT * LANES
    rows = min(rows, e)
    return pl.pallas_call(
        _pack_table_kernel,
        grid=(e // rows,),
        in_specs=[pl.BlockSpec((None, rows, dm), lambda i: (layer, i, 0))],
        out_specs=pl.BlockSpec((rows * ROWS_PER_EXPERT, LANES), lambda i: (i, 0)),
        out_shape=jax.ShapeDtypeStruct((e * ROWS_PER_EXPERT, LANES), jnp.int32),
        compiler_params=_cparams(("parallel",), VMEM_LIMIT),
        name="pack_table",
    )(tables)


def _unpack_words(w):
    lo = lax.bitcast_convert_type(w << 16, F32)
    hi = lax.bitcast_convert_type(w & jnp.int32(-65536), F32)
    return lo, hi


def _gather_chunk(idx_ref, tab_ref, buf_ref, c, group, slots):
    for g in range(PAIR_CHUNK // group):
        ids = idx_ref.at[pl.ds(pl.multiple_of(c * PAIR_CHUNK + g * group, group), group)]
        for i in range(group):
            e4 = pl.multiple_of(ids[i], ROWS_PER_EXPERT)
            s = slots[g * group + i] * ROWS_PER_EXPERT
            buf_ref[s:s + ROWS_PER_EXPERT, :] = tab_ref[pl.ds(e4, ROWS_PER_EXPERT), :]


def _chunk_loop(n_chunks, idx_ref, tab_ref, buf_a, buf_b, consume, init, group, slots, per_step=2, extra=None):
    last = n_chunks - 1
    gather = functools.partial(_gather_chunk, idx_ref, tab_ref, group=group, slots=slots)
    gather(buf_a, 0)
    bufs = (buf_a, buf_b)

    def body(i, carry):
        if extra is not None:
            extra(i)
        c = per_step * i
        for u in range(per_step):
            carry = consume(c + u, bufs[u % 2], carry)
            nxt = c + u + 1
            gather(bufs[(u + 1) % 2], jnp.minimum(nxt, last) if u == per_step - 1 else nxt)
        return carry

    return lax.fori_loop(0, n_chunks // per_step, body, init)


def _peer_u_kernel(idx_ref, h_ref, gate_ref, tab_ref, coef_ref, buf_a, buf_b, r_even, r_odd, d_ref, *, tt, n_tiles):
    step = pl.program_id(0)
    sub = lax.broadcasted_iota(jnp.int32, (SUBLANES, LANES), 0)
    low = sub < ROWS_PER_EXPERT
    m_a = ((sub % 4) >= 2)[None]
    m_b = ((sub % 2) == 1)[None]
    nv = PAIR_CHUNK // 2

    def pair_partials(r_ref, c, buf_ref, carry):
        t = c >> CHUNK_SHIFT
        h8 = h_ref[pl.ds(pl.multiple_of(t * SUBLANES, SUBLANES), SUBLANES), :]
        sw = pltpu.roll(h8, ROWS_PER_EXPERT, axis=0)
        ha = jnp.where(low, h8, sw)[None]
        hb = jnp.where(low, sw, h8)[None]
        lo, hi = _unpack_words(buf_ref[...])
        x = lo.reshape(nv, SUBLANES, LANES) * ha + hi.reshape(nv, SUBLANES, LANES) * hb
        x = x.reshape(nv // 2, 2, SUBLANES, LANES)
        xe, xo = x[:, 0], x[:, 1]
        a = xe + pltpu.roll(xe, 2, axis=1)
        b = xo + pltpu.roll(xo, 6, axis=1)
        m = jnp.where(m_a, a, b).reshape(nv // 4, 2, SUBLANES, LANES)
        me, mo = m[:, 0], m[:, 1]
        a2 = me + pltpu.roll(me, 1, axis=1)
        b2 = mo + pltpu.roll(mo, 7, axis=1)
        r_ref[pl.ds(pl.multiple_of(c * PAIR_CHUNK, PAIR_CHUNK), PAIR_CHUNK), :] = (
            jnp.where(m_b, a2, b2).reshape(PAIR_CHUNK, LANES))
        return carry

    def token_sum(r_prev, t):
        r_t = r_prev[pl.ds(pl.multiple_of(t * PEER_SEL, PEER_SEL), PEER_SEL), :]
        d_ref[pl.ds(t, 1), :] = jnp.sum(r_t.T, axis=0, keepdims=True)

    def loop(r_cur, **kw):
        _chunk_loop(tt * N_CHUNKS, idx_ref, tab_ref, buf_a, buf_b, functools.partial(pair_partials, r_cur), 0,
                    SMEM_GROUP, _CHUNK_SLOT, **kw)

    @pl.when(step == 0)
    def _():
        loop(r_even)

    for parity, (r_cur, r_prev) in enumerate(((r_even, r_odd), (r_odd, r_even))):
        @pl.when(jnp.logical_and(jnp.logical_and(step > 0, step < n_tiles), step % 2 == parity))
        def _(r_cur=r_cur, r_prev=r_prev):
            loop(r_cur, per_step=N_CHUNKS, extra=functools.partial(token_sum, r_prev))

    @pl.when(step == n_tiles)
    def _():
        def only_sums(t, carry):
            token_sum(r_even if n_tiles % 2 else r_odd, t)
            return carry
        lax.fori_loop(0, tt, only_sums, 0)

    @pl.when(step > 0)
    def _():
        coef_ref[...] = gate_ref[...] * _gelu(d_ref[...])


def _peer_u(idx4, h8, gates, tab, *, tt=128):
    n_tok = gates.shape[0]
    tt = min(tt, n_tok)
    n_tiles = n_tok // tt
    cur = lambda i: jnp.minimum(i, n_tiles - 1)
    done = lambda i: jnp.maximum(i - 1, 0)
    return pl.pallas_call(
        functools.partial(_peer_u_kernel, tt=tt, n_tiles=n_tiles),
        grid=(n_tiles + 1,),
        in_specs=[pl.BlockSpec((tt * PEER_SEL,), lambda i: (cur(i),), memory_space=pltpu.SMEM),
                  pl.BlockSpec((tt * SUBLANES, LANES), lambda i: (cur(i), 0)),
                  pl.BlockSpec((tt, PEER_SEL), lambda i: (done(i), 0)),
                  pl.BlockSpec(tab.shape, lambda i: (0, 0), pipeline_mode=pl.Buffered(1))],
        out_specs=pl.BlockSpec((tt, PEER_SEL), lambda i: (done(i), 0)),
        out_shape=jax.ShapeDtypeStruct((n_tok, PEER_SEL), F32),
        scratch_shapes=[pltpu.VMEM((CHUNK_ROWS, LANES), jnp.int32),
                        pltpu.VMEM((CHUNK_ROWS, LANES), jnp.int32),
                        pltpu.VMEM((tt * PEER_SEL, LANES), F32),
                        pltpu.VMEM((tt * PEER_SEL, LANES), F32),
                        pltpu.VMEM((tt, PEER_SEL), F32)],
        compiler_params=_cparams(("arbitrary",), PEER_U_VMEM_LIMIT),
        name="peer_u",
    )(idx4, h8, gates, tab)


def _peer_v_kernel(idx_ref, coef_ref, tab_ref, o_ref, buf_a, buf_b, *, tt):
    sub = lax.broadcasted_iota(jnp.int32, (SUBLANES, LANES), 0)
    low = sub < ROWS_PER_EXPERT
    nv = PAIR_CHUNK // 2

    def accumulate(c, buf_ref, acc):
        acc_lo, acc_hi = acc
        lo, hi = _unpack_words(buf_ref[...])
        lo = lo.reshape(nv, SUBLANES, LANES)
        hi = hi.reshape(nv, SUBLANES, LANES)
        cs = coef_ref.at[pl.ds(pl.multiple_of(c * nv, nv), nv)]
        for v in range(nv):
            c_lo, c_hi = _unpack_words(jnp.full((SUBLANES, LANES), cs[v], jnp.int32))
            cv = jnp.where(low, c_lo, c_hi)
            acc_lo = acc_lo + cv * lo[v]
            acc_hi = acc_hi + cv * hi[v]
        lo4 = acc_lo + pltpu.roll(acc_lo, ROWS_PER_EXPERT, axis=0)
        hi4 = acc_hi + pltpu.roll(acc_hi, ROWS_PER_EXPERT, axis=0)
        o_ref[c >> CHUNK_SHIFT] = jnp.where(low, lo4, hi4)
        last = (c & (N_CHUNKS - 1)) == N_CHUNKS - 1
        return jnp.where(last, 0.0, acc_lo), jnp.where(last, 0.0, acc_hi)

    z = jnp.zeros((SUBLANES, LANES), F32)
    _chunk_loop(tt * N_CHUNKS, idx_ref, tab_ref, buf_a, buf_b, accumulate, (z, z), PAIR_CHUNK,
                tuple(range(PAIR_CHUNK)))


def _pack_coefs(coef):
    n_tok = coef.shape[0]
    bits = lax.bitcast_convert_type(coef.astype(BF16), jnp.uint16).astype(jnp.uint32)
    bits = bits.reshape(n_tok, PEER_SEL // 2, 2)
    w = bits[:, :, 0] | (bits[:, :, 1] << 16)
    return lax.bitcast_convert_type(w, jnp.int32).reshape(n_tok * (PEER_SEL // 2))


def _peer_v(idx4, coef_words, tab, *, tt=128):
    n_tok = idx4.shape[0] // PEER_SEL
    tt = min(tt, n_tok)
    smem = pl.BlockSpec((tt * PEER_SEL,), lambda i: (i,), memory_space=pltpu.SMEM)
    return pl.pallas_call(
        functools.partial(_peer_v_kernel, tt=tt),
        grid=(n_tok // tt,),
        in_specs=[smem, pl.BlockSpec((tt * PEER_SEL // 2,), lambda i: (i,), memory_space=pltpu.SMEM),
                  pl.BlockSpec(tab.shape, lambda i: (0, 0), pipeline_mode=pl.Buffered(1))],
        out_specs=pl.BlockSpec((tt, SUBLANES, LANES), lambda i: (i, 0, 0)),
        out_shape=jax.ShapeDtypeStruct((n_tok, SUBLANES, LANES), F32),
        scratch_shapes=[pltpu.VMEM((CHUNK_ROWS, LANES), jnp.int32),
                        pltpu.VMEM((CHUNK_ROWS, LANES), jnp.int32)],
        compiler_params=_cparams(("arbitrary",), VMEM_LIMIT),
        name="peer_v",
    )(idx4, coef_words, tab)


def _peer(h8, qp, sub_keys, tab_u, tab_v):
    n_tok = qp.shape[1]
    half = sub_keys.shape[-1]
    z = jnp.zeros_like(sub_keys[0])
    keys_pad = jnp.stack([jnp.concatenate([sub_keys[0], z], axis=-1),
                          jnp.concatenate([z, sub_keys[1]], axis=-1)]).astype(BF16)
    assert keys_pad.shape[-1] == 2 * half == qp.shape[-1]
    idx4, gates = _peer_topk(qp, keys_pad)
    idx4 = idx4.reshape(n_tok * PEER_SEL)
    coef = _peer_u(idx4, h8, gates, tab_u)
    out = _peer_v(idx4, _pack_coefs(coef), tab_v)
    return out.reshape(n_tok * SUBLANES, LANES)


def _final_kernel(x_ref, p_ref, g_ref, o_ref):
    o_ref[...] = _rms(x_ref[...] + _rows_from_chunks(p_ref, x_ref.shape[0]), g_ref[...])


def _final_norm(x, p, g, *, tr=1024):
    rows, dm = x.shape
    tr = min(tr, rows)
    blk = pl.BlockSpec((tr, dm), lambda i: (i, 0))
    return pl.pallas_call(
        _final_kernel,
        grid=(rows // tr,),
        in_specs=[blk, pl.BlockSpec((tr * SUBLANES, LANES), lambda i: (i, 0)),
                  pl.BlockSpec((1, dm), lambda i: (0, 0))],
        out_specs=blk,
        out_shape=jax.ShapeDtypeStruct((rows, dm), F32),
        compiler_params=_cparams(("parallel",), VMEM_LIMIT),
        name="final_norm",
    )(x, p, g.reshape(1, dm))


def kernel(x, w_in, w_out, rel_bias, g_attn, g_ssm, norm_mix, norm_ffn, lam_re, lam_im, log_step, b_re, b_im, c_re, c_im, d_skip, w_glu, w_query, sub_keys, expert_u, expert_v, norm_final):
    bsz, seq, dm = x.shape
    depth = w_in.shape[0]
    prev = None
    biases = [_attn_bias_tables(rel_bias, d) for _, d in DILATED_PATTERNS]
    for l in range(depth):
        x, q, k, v, u_tm = _in_proj(x, prev, norm_mix[l], w_in[l].astype(BF16))
        attn = _attention(q, k, v, biases)
        a, bw, cw = _ssm_params(lam_re[l], lam_im[l], log_step[l], b_re[l], b_im[l], c_re[l], c_im[l])
        y = _ssm_scan(u_tm, a, bw, cw, bsz=bsz)
        wo = w_out[l].astype(BF16)
        z = _ssm_post(y, u_tm, d_skip[l], w_glu[l].astype(BF16), g_ssm[l], wo[ATTN_WIDTH:], bsz=bsz)
        x, h, qp = _mix_out(x, z, attn,
                            g_attn[l], wo[:ATTN_WIDTH], norm_ffn[l], w_query[l].astype(BF16))
        prev = _peer(h, qp, sub_keys[l], _pack_table(expert_u, l), _pack_table(expert_v, l))
    out = _final_norm(x.reshape(bsz * seq, dm), prev, norm_final)
    return out.reshape(bsz, seq, dm)
```

```python
import functools
import math

import numpy as np
import jax
import jax.numpy as jnp
from jax import lax
from jax.experimental import pallas as pl
from jax.experimental.pallas import tpu as pltpu

F32 = jnp.float32
BF16 = jnp.bfloat16

EPS = 1e-6
NEG_INF = -1e30
HEAD_DIM = 64
ATTN_WIDTH = 512
SSM_WIDTH = 512
SSM_GROUP = 16
SSM_STATE = 64
DILATED_PATTERNS = ((128, 1), (512, 4), (2048, 16))
REL_BUCKETS = 32
REL_MAX_DISTANCE = 1024
PEER_HEADS = 8
PEER_KEYS = 128
PEER_TOPK = 16
PEER_SEL = PEER_HEADS * PEER_TOPK

LANES = 128
SUBLANES = 8
QBLK = 128
KWIN = 256
BAND = 64
VMEM_LIMIT = 52 * 1024 * 1024
PEER_U_VMEM_LIMIT = 58 * 1024 * 1024


def _cparams(sem, vmem=None):
    return pltpu.CompilerParams(dimension_semantics=sem, vmem_limit_bytes=vmem)


def _rms(x, g):
    return x * lax.rsqrt(jnp.mean(x * x, axis=-1, keepdims=True) + EPS) * g


def _gelu(x):
    return 0.5 * x * (1.0 + lax.erf(x * (1.0 / math.sqrt(2.0))))


def _rows_from_chunks(p_ref, n_rows):
    return jnp.concatenate([p_ref[pl.ds(c, n_rows, stride=SUBLANES), :] for c in range(SUBLANES)], axis=-1)


def _in_proj_kernel(*refs, has_prev):
    if has_prev:
        x_ref, p_ref, g_ref, w_ref, xo_ref, q_ref, k_ref, v_ref, u_ref = refs
        x = x_ref[0] + _rows_from_chunks(p_ref, x_ref.shape[1])
    else:
        x_ref, g_ref, w_ref, xo_ref, q_ref, k_ref, v_ref, u_ref = refs
        x = x_ref[0]
    xo_ref[0] = x
    h = _rms(x, g_ref[...]).astype(BF16)
    proj = jnp.dot(h, w_ref[...], preferred_element_type=F32)
    a = ATTN_WIDTH
    q_ref[0] = proj[:, :a] * (HEAD_DIM ** -0.5)
    k_ref[0] = proj[:, a:2 * a]
    v_ref[0] = proj[:, 2 * a:3 * a]
    b, bsz = pl.program_id(1), pl.num_programs(1)
    for c in range(SSM_WIDTH // LANES):
        u_ref[c, pl.ds(b, proj.shape[0], stride=bsz), :] = proj[:, 3 * a + c * LANES:3 * a + (c + 1) * LANES]


def _in_proj(x, prev, g, w_bf16, *, ts=256):
    bsz, seq, dm = x.shape
    ts = min(ts, seq)
    row = pl.BlockSpec((1, ts, dm), lambda i, b: (b, i, 0))
    qkv = pl.BlockSpec((1, ts, ATTN_WIDTH), lambda i, b: (b, i, 0))
    ns = seq // ts
    chunks = pl.BlockSpec((ts * SUBLANES, LANES), lambda i, b: (b * ns + i, 0))
    ins = [x] + ([prev] if prev is not None else []) + [g.reshape(1, dm), w_bf16]
    in_specs = [row] + ([chunks] if prev is not None else []) + [
        pl.BlockSpec((1, dm), lambda i, b: (0, 0)),
        pl.BlockSpec(w_bf16.shape, lambda i, b: (0, 0)),
    ]
    slabs = SSM_WIDTH // LANES
    return pl.pallas_call(
        functools.partial(_in_proj_kernel, has_prev=prev is not None),
        grid=(seq // ts, bsz),
        in_specs=in_specs,
        out_specs=[row, qkv, qkv, qkv, pl.BlockSpec((slabs, ts * bsz, LANES), lambda i, b: (0, i, 0))],
        out_shape=[
            jax.ShapeDtypeStruct((bsz, seq, dm), F32),
            jax.ShapeDtypeStruct((bsz, seq, ATTN_WIDTH), F32),
            jax.ShapeDtypeStruct((bsz, seq, ATTN_WIDTH), F32),
            jax.ShapeDtypeStruct((bsz, seq, ATTN_WIDTH), F32),
            jax.ShapeDtypeStruct((slabs, seq * bsz, LANES), F32),
        ],
        compiler_params=_cparams(("parallel", "arbitrary"), VMEM_LIMIT),
        name="in_proj",
    )(*ins)


def _t5_buckets(rel):
    half = REL_BUCKETS // 2
    max_exact = half // 2
    n = np.abs(rel)
    large = max_exact + (np.log(np.maximum(n, 1) / max_exact)
                         / np.log(REL_MAX_DISTANCE / max_exact) * (half - max_exact)).astype(np.int32)
    large = np.minimum(large, half - 1)
    return (np.where(rel > 0, half, 0) + np.where(n < max_exact, n, large)).astype(np.int32)


def _attn_bias_tables(rel_bias, dilation):
    ql = np.arange(QBLK)[:, None]
    kl = np.arange(KWIN)[None, :]
    delta = np.stack([kl + off - ql for off in (0, -BAND, -2 * BAND)])
    buckets = np.where(np.abs(delta) <= BAND, _t5_buckets(delta * dilation), -1)
    rb = rel_bias.astype(F32).T
    bk = jnp.asarray(buckets, jnp.int32)[None]
    tab = jnp.full((rb.shape[0],) + buckets.shape, NEG_INF, F32)
    for b in range(REL_BUCKETS):
        tab = jnp.where(bk == b, rb[:, b][:, None, None, None], tab)
    return tab


BLOCKS_PER_STEP = 4


def _attn_kernel(q_ref, k_ref, v_ref, *rest, seq):
    bias_refs, (o_ref, acc_ref, m_ref, z_ref) = rest[:len(DILATED_PATTERNS)], rest[len(DILATED_PATTERNS):]
    lane = lax.broadcasted_iota(jnp.int32, (QBLK, LANES), 1)
    is_h0 = lane < HEAD_DIM
    dn = (((1,), (1,)), ((), ()))
    nsteps = seq // QBLK

    def rows(start, size, d):
        return pl.ds(start, size) if d == 1 else pl.ds(start, size, stride=d)

    def block(n, d, bias_ref, first, last):
        length = seq // d
        nblk = length // QBLK
        r, i = n >> (nblk.bit_length() - 1), n & (nblk - 1)
        s = i * QBLK
        ks = jnp.clip(s - BAND, 0, length - KWIN)
        var = jnp.where(i == 0, 0, jnp.where(i == nblk - 1, 2, 1))
        q_rows = rows(r + d * s, QBLK, d)
        k_rows = rows(r + d * ks, KWIN, d)
        qb = q_ref[q_rows, :].astype(BF16)
        kb = k_ref[k_rows, :].astype(BF16)
        vb = v_ref[k_rows, :].astype(BF16)
        outs, ms, zs = [], [], []
        for h in range(2):
            keep = is_h0 if h == 0 else jnp.logical_not(is_h0)
            qh = jnp.where(keep, qb, jnp.zeros_like(qb))
            logits = lax.dot_general(qh, kb, dn, preferred_element_type=F32) + bias_ref[h, var]
            m = jnp.max(logits, axis=-1, keepdims=True)
            p = jnp.exp(logits - m)
            outs.append(jnp.dot(p.astype(BF16), vb, preferred_element_type=F32))
            ms.append(jnp.broadcast_to(m, (QBLK, LANES)))
            zs.append(jnp.broadcast_to(jnp.sum(p, axis=-1, keepdims=True), (QBLK, LANES)))
        o = jnp.where(is_h0, outs[0], outs[1])
        m = jnp.where(is_h0, ms[0], ms[1])
        z = jnp.where(is_h0, zs[0], zs[1])
        if not first:
            m_old = m_ref[q_rows, :]
            m_new = jnp.maximum(m_old, m)
            a, b = jnp.exp(m_old - m_new), jnp.exp(m - m_new)
            o = acc_ref[q_rows, :] * a + o * b
            z = z_ref[q_rows, :] * a + z * b
            m = m_new
        if last:
            o_ref[q_rows, :] = o / z
        else:
            acc_ref[q_rows, :] = o
            m_ref[q_rows, :] = m
            z_ref[q_rows, :] = z

    for p, ((_, d), bias_ref) in enumerate(zip(DILATED_PATTERNS, bias_refs)):
        def step(g, carry, d=d, bias_ref=bias_ref, p=p):
            for j in range(BLOCKS_PER_STEP):
                block(g * BLOCKS_PER_STEP + j, d, bias_ref, p == 0, p == len(DILATED_PATTERNS) - 1)
            return carry
        lax.fori_loop(0, nsteps // BLOCKS_PER_STEP, step, 0)


def _attention(q, k, v, biases):
    bsz, seq, width = q.shape
    for _, d in DILATED_PATTERNS:
        length = seq // d
        assert length >= KWIN and length % QBLK == 0 and (length // QBLK) & (length // QBLK - 1) == 0
    assert (seq // QBLK) % BLOCKS_PER_STEP == 0
    blk = pl.BlockSpec((None, seq, LANES), lambda b, c: (b, 0, c))
    bias_spec = pl.BlockSpec((2, 3, QBLK, KWIN), lambda b, c: (c, 0, 0, 0))
    return pl.pallas_call(
        functools.partial(_attn_kernel, seq=seq),
        grid=(bsz, width // LANES),
        in_specs=[blk, blk, blk] + [bias_spec] * len(biases),
        out_specs=blk,
        out_shape=jax.ShapeDtypeStruct((bsz, seq, width), F32),
        scratch_shapes=[pltpu.VMEM((seq, LANES), F32)] * 3,
        compiler_params=_cparams(("parallel", "arbitrary"), VMEM_LIMIT),
        name="attention",
    )(q, k, v, *biases)


SSM_LANE_GROUPS = SSM_WIDTH // LANES
SSM_GB_STATES = (LANES // SSM_GROUP) * SSM_STATE


def _ssm_params(lam_re, lam_im, log_step, b_re, b_im, c_re, c_im):
    f = lambda t: t.astype(F32)
    lr, li = f(lam_re), f(lam_im)
    step = jnp.exp(f(log_step))[..., None]
    mag = jnp.exp(lr * step)
    ar, ai = mag * jnp.cos(li * step), mag * jnp.sin(li * step)
    nr, ni = ar - 1.0, ai
    den = lr * lr + li * li
    cr, ci = (nr * lr + ni * li) / den, (ni * lr - nr * li) / den
    br, bi = f(b_re), f(b_im)
    bbr = cr[..., None] * br - ci[..., None] * bi
    bbi = cr[..., None] * bi + ci[..., None] * br
    gpb = LANES // SSM_GROUP
    eye = jnp.eye(gpb, dtype=F32)

    def in_map(t):
        t = t.reshape(2, SSM_LANE_GROUPS, gpb, SSM_STATE, SSM_GROUP)
        return jnp.einsum('dbgpc,gh->dbgchp', t, eye).reshape(2, SSM_LANE_GROUPS, LANES, SSM_GB_STATES)

    def out_map(t):
        t = t.reshape(2, SSM_LANE_GROUPS, gpb, SSM_GROUP, SSM_STATE)
        return jnp.einsum('dbgcp,gh->dbgphc', t, eye).reshape(2, SSM_LANE_GROUPS, SSM_GB_STATES, LANES)

    bw = jnp.concatenate([in_map(bbr), in_map(bbi)], axis=-1).astype(BF16)
    cw = jnp.concatenate([out_map(f(c_re)), -out_map(f(c_im))], axis=-2).astype(BF16)
    a = jnp.stack([ar.reshape(2, SSM_LANE_GROUPS, SSM_GB_STATES),
                   ai.reshape(2, SSM_LANE_GROUPS, SSM_GB_STATES)], axis=2)
    return a, bw, cw


def _ssm_kernel(u_ref, a_ref, bw_ref, cw_ref, y_ref, st_ref, bu_ref, *, ts, bsz):
    d = pl.program_id(0)
    ns = SSM_GB_STATES

    @pl.when(pl.program_id(1) == 0)
    def _():
        st_ref[...] = jnp.zeros_like(st_ref)

    for gb in range(SSM_LANE_GROUPS):
        ub = u_ref[gb].astype(BF16)
        bu_ref[...] = jnp.dot(ub, bw_ref[0, gb], preferred_element_type=F32)
        ar = jnp.broadcast_to(a_ref[0, gb, 0:1, :], (bsz, ns))
        ai = jnp.broadcast_to(a_ref[0, gb, 1:2, :], (bsz, ns))

        def step(j, carry, ar=ar, ai=ai):
            xr, xi = carry
            tl = jnp.where(d == 0, j, ts - 1 - j)
            r = pl.multiple_of(tl * bsz, bsz)
            nr = ar * xr - ai * xi + bu_ref[pl.ds(r, bsz), :ns]
            ni = ar * xi + ai * xr + bu_ref[pl.ds(r, bsz), ns:]
            bu_ref[pl.ds(r, bsz), :ns] = nr
            bu_ref[pl.ds(r, bsz), ns:] = ni
            return nr, ni

        xr, xi = lax.fori_loop(0, ts, step, (st_ref[gb, :, :ns], st_ref[gb, :, ns:]))
        st_ref[gb, :, :ns] = xr
        st_ref[gb, :, ns:] = xi
        y_ref[0, :, gb * LANES:(gb + 1) * LANES] = jnp.dot(
            bu_ref[...].astype(BF16), cw_ref[0, gb], preferred_element_type=F32)


def _ssm_scan(u_tm, a, bw, cw, *, bsz, ts=64):
    slabs, rows, _ = u_tm.shape
    width = slabs * LANES
    seq = rows // bsz
    ts = min(ts, seq)
    nt = seq // ts
    tblk = lambda d, i: jnp.where(d == 0, i, nt - 1 - i)
    return pl.pallas_call(
        functools.partial(_ssm_kernel, ts=ts, bsz=bsz),
        grid=(2, nt),
        in_specs=[
            pl.BlockSpec((slabs, ts * bsz, LANES), lambda d, i: (0, tblk(d, i), 0)),
            pl.BlockSpec((1,) + a.shape[1:], lambda d, i: (d, 0, 0, 0)),
            pl.BlockSpec((1,) + bw.shape[1:], lambda d, i: (d, 0, 0, 0)),
            pl.BlockSpec((1,) + cw.shape[1:], lambda d, i: (d, 0, 0, 0)),
        ],
        out_specs=pl.BlockSpec((1, ts * bsz, width), lambda d, i: (d, tblk(d, i), 0)),
        out_shape=jax.ShapeDtypeStruct((2, rows, width), F32),
        scratch_shapes=[pltpu.VMEM((SSM_LANE_GROUPS, bsz, 2 * SSM_GB_STATES), F32),
                        pltpu.VMEM((ts * bsz, 2 * SSM_GB_STATES), F32)],
        compiler_params=_cparams(("arbitrary", "arbitrary"), VMEM_LIMIT),
        name="ssm_scan",
    )(u_tm, a, bw, cw)


def _ssm_post_kernel(y_ref, u_ref, d_ref, wg_ref, g_ref, wo_ref, z_ref, slab_ref, *, bsz):
    u = jnp.concatenate([u_ref[c] for c in range(u_ref.shape[0])], axis=-1)
    y = _gelu(y_ref[0] + y_ref[1] + d_ref[...] * u).astype(BF16)
    ab = jnp.dot(y, wg_ref[...], preferred_element_type=F32)
    ssm = ab[:, :SSM_WIDTH] * jax.nn.sigmoid(ab[:, SSM_WIDTH:])
    n = _rms(ssm, g_ref[...]).astype(BF16)
    z = jnp.dot(n, wo_ref[...], preferred_element_type=F32)
    nt, dm = z.shape[0] // bsz, z.shape[1]
    for c in range(dm // LANES):
        slab_ref[c] = z[:, c * LANES:(c + 1) * LANES]
    for b in range(bsz):
        for c in range(dm // LANES):
            z_ref[:, b * dm + c * LANES:b * dm + (c + 1) * LANES] = slab_ref[c, pl.ds(b, nt, stride=bsz), :]


def _ssm_post(y, u_tm, d_skip, w_glu_bf16, g_ssm, w_out_ssm_bf16, *, bsz, tr=512):
    slabs, rows, _ = u_tm.shape
    width = slabs * LANES
    tr = min(tr, rows)
    assert tr % bsz == 0
    dm = w_out_ssm_bf16.shape[1]
    full = lambda a: pl.BlockSpec(a.shape, lambda i: (0,) * a.ndim)
    d2, g2 = d_skip.reshape(1, width), g_ssm.reshape(1, width)
    return pl.pallas_call(
        functools.partial(_ssm_post_kernel, bsz=bsz),
        grid=(rows // tr,),
        in_specs=[pl.BlockSpec((2, tr, width), lambda i: (0, i, 0)),
                  pl.BlockSpec((slabs, tr, LANES), lambda i: (0, i, 0)),
                  full(d2), full(w_glu_bf16), full(g2), full(w_out_ssm_bf16)],
        out_specs=pl.BlockSpec((tr // bsz, bsz * dm), lambda i: (i, 0)),
        out_shape=jax.ShapeDtypeStruct((rows // bsz, bsz * dm), F32),
        scratch_shapes=[pltpu.VMEM((dm // LANES, tr, LANES), F32)],
        compiler_params=_cparams(("parallel",), VMEM_LIMIT),
        name="ssm_post",
    )(y, u_tm, d2, w_glu_bf16, g2, w_out_ssm_bf16)


def _mix_out_kernel(x_ref, z_ref, a_ref, ga_ref, wo_ref, gf_ref, wq_ref, xn_ref, h_ref, q_ref):
    n = _rms(a_ref[0], ga_ref[...]).astype(BF16)
    xn = x_ref[0] + z_ref[...] + jnp.dot(n, wo_ref[...], preferred_element_type=F32)
    xn_ref[0] = xn
    h = _rms(xn, gf_ref[...])
    for c in range(SUBLANES):
        h_ref[pl.ds(c, h.shape[0], stride=SUBLANES), :] = h[:, c * LANES:(c + 1) * LANES]
    qp = jnp.dot(h.astype(BF16), wq_ref[...], preferred_element_type=F32)
    for hd in range(PEER_HEADS):
        q_ref[hd] = qp[:, hd * LANES:(hd + 1) * LANES]


def _mix_out(x, z_tm, attn, g_attn, w_out_attn_bf16, norm_ffn, w_query, *, ts=256):
    bsz, seq, dm = x.shape
    ts = min(ts, seq)
    ns = seq // ts
    row = pl.BlockSpec((1, ts, dm), lambda b, i: (b, i, 0))
    half = pl.BlockSpec((1, ts, ATTN_WIDTH), lambda b, i: (b, i, 0))
    full = lambda a: pl.BlockSpec(a.shape, lambda b, i: (0,) * a.ndim)
    ga, gf = g_attn.reshape(1, ATTN_WIDTH), norm_ffn.reshape(1, dm)
    qdim = w_query.shape[1] // PEER_HEADS
    return pl.pallas_call(
        _mix_out_kernel,
        grid=(bsz, ns),
        in_specs=[row, pl.BlockSpec((ts, dm), lambda b, i: (i, b)), half,
                  full(ga), full(w_out_attn_bf16), full(gf), full(w_query)],
        out_specs=[row, pl.BlockSpec((ts * SUBLANES, LANES), lambda b, i: (b * ns + i, 0)),
                   pl.BlockSpec((PEER_HEADS, ts, qdim), lambda b, i: (0, b * ns + i, 0))],
        out_shape=[jax.ShapeDtypeStruct((bsz, seq, dm), F32),
                   jax.ShapeDtypeStruct((bsz * seq * SUBLANES, LANES), F32),

                   jax.ShapeDtypeStruct((PEER_HEADS, bsz * seq, qdim), F32)],
        compiler_params=_cparams(("parallel", "arbitrary"), VMEM_LIMIT),
        name="mix_out",
    )(x, z_tm, attn, ga, w_out_attn_bf16, gf, w_query)


TOPK_TOKENS = SUBLANES * LANES
KEY_PITCH = PEER_KEYS + 4
_CANDIDATES = tuple((a, b) for a in range(PEER_TOPK) for b in range(PEER_TOPK) if (a + 1) * (b + 1) <= PEER_TOPK)


def _tree(op, xs):
    xs = list(xs)
    while len(xs) > 1:
        xs = [op(xs[i], xs[i + 1]) if i + 1 < len(xs) else xs[i] for i in range(0, len(xs), 2)]
    return xs[0]


def _extract16(problems):
    ninf = jnp.float32(-jnp.inf)

    def better(a, b):
        gt = b[0] > a[0]
        return tuple(jnp.where(gt, y, x) for x, y in zip(a, b))

    def step(r, carry):
        for p in problems:
            s_ref, order = p["s"], p["order"]
            assert list(order) == sorted(order)
            n = len(order)
            rows = [(s_ref[k], order[k]) + ((p["pay"][k],) if p.get("pay") is not None else ())
                    for k in range(n)]
            win = _tree(better, rows)
            m, am = win[0], win[1]
            for k in range(n):
                s_ref[k] = jnp.where(am == order[k], ninf, s_ref[k])
            p["vals"][r] = m
            p["picks"][r] = win[2] if len(win) > 2 else am.astype(jnp.int32)
        return carry

    lax.fori_loop(0, PEER_TOPK, step, 0)


def _peer_topk_kernel(q_ref, k_ref, idx_ref, gate_ref,
                      slab_ref, s1_ref, s2_ref, t1_ref, i1_ref, t2_ref, i2_ref, cand_ref, pay_ref, ts_ref, ex_ref):
    dn = (((1,), (1,)), ((), ()))
    keys = tuple(range(PEER_KEYS))

    def head(h, carry):
        for w, s_ref in ((0, s1_ref), (1, s2_ref)):
            for j in range(SUBLANES):
                slab_ref[j * KEY_PITCH:j * KEY_PITCH + PEER_KEYS, :] = lax.dot_general(
                    k_ref[w, h], q_ref[h, j * LANES:(j + 1) * LANES, :].astype(BF16), dn,
                    preferred_element_type=F32)
            for k in range(PEER_KEYS):
                s_ref[k] = slab_ref[pl.ds(k, SUBLANES, stride=KEY_PITCH), :]
        _extract16([dict(s=s1_ref, order=keys, vals=t1_ref, picks=i1_ref),
                    dict(s=s2_ref, order=keys, vals=t2_ref, picks=i2_ref)])
        for c, (a, b) in enumerate(_CANDIDATES):
            cand_ref[c] = t1_ref[a] + t2_ref[b]
            pay_ref[c] = i1_ref[a] * PEER_KEYS + i2_ref[b]
        _extract16([dict(s=cand_ref, order=tuple(a * PEER_TOPK + b for a, b in _CANDIDATES), pay=pay_ref,
                         vals=ts_ref, picks=ex_ref)])
        top_s = ts_ref[...]
        e = jnp.exp(top_s - jnp.max(top_s, axis=0, keepdims=True))
        gate_ref[0, h] = e / jnp.sum(e, axis=0, keepdims=True)
        idx_ref[0, h] = ex_ref[...] * ROWS_PER_EXPERT
        return carry

    lax.fori_loop(0, PEER_HEADS, head, 0)


def _peer_topk(qp, keys_pad):
    n_tok = qp.shape[1]
    tt = TOPK_TOKENS
    assert n_tok % tt == 0
    shp = (n_tok // tt, PEER_HEADS, PEER_TOPK, SUBLANES, LANES)
    out = pl.BlockSpec((1,) + shp[1:], lambda i: (i, 0, 0, 0, 0))
    vregs = lambda n, dt: pltpu.VMEM((n, SUBLANES, LANES), dt)
    idx, gate = pl.pallas_call(
        _peer_topk_kernel,
        grid=(n_tok // tt,),
        in_specs=[pl.BlockSpec((PEER_HEADS, tt, qp.shape[2]), lambda i: (0, i, 0)),
                  pl.BlockSpec(keys_pad.shape, lambda i: (0, 0, 0, 0))],
        out_specs=[out, out],
        out_shape=[jax.ShapeDtypeStruct(shp, jnp.int32), jax.ShapeDtypeStruct(shp, F32)],
        scratch_shapes=[pltpu.VMEM((SUBLANES * KEY_PITCH, LANES), F32),
                        vregs(PEER_KEYS, F32), vregs(PEER_KEYS, F32),
                        vregs(PEER_TOPK, F32), vregs(PEER_TOPK, jnp.int32),
                        vregs(PEER_TOPK, F32), vregs(PEER_TOPK, jnp.int32),
                        vregs(len(_CANDIDATES), F32), vregs(len(_CANDIDATES), jnp.int32),
                        vregs(PEER_TOPK, F32), vregs(PEER_TOPK, jnp.int32)],
        compiler_params=_cparams(("parallel",), VMEM_LIMIT),
        name="peer_topk",
    )(qp, keys_pad)
    to_tok = lambda a: jnp.transpose(a, (0, 3, 4, 1, 2)).reshape(n_tok, PEER_SEL)
    return to_tok(idx), to_tok(gate)


ROWS_PER_EXPERT = 4
PAIR_CHUNK = 32
CHUNK_ROWS = PAIR_CHUNK * ROWS_PER_EXPERT
SMEM_GROUP = 8
N_CHUNKS = PEER_SEL // PAIR_CHUNK
CHUNK_SHIFT = N_CHUNKS.bit_length() - 1
assert 1 << CHUNK_SHIFT == N_CHUNKS
_GROUP_SLOT = tuple(
    2 * (2 * (0 if (j % 2) else 1) + (1 if (j // 2) in (0, 2) else 0)) + (0 if (j // 2) < 2 else 1)
    for j in range(8))
_CHUNK_SLOT = tuple(8 * (j // 8) + _GROUP_SLOT[j % 8] for j in range(PAIR_CHUNK))


def _pack_table_kernel(t_ref, o_ref):
    rows, dm = t_ref.shape
    for c in range(ROWS_PER_EXPERT):
        lo = lax.bitcast_convert_type(t_ref[:, c * LANES:(c + 1) * LANES].astype(BF16).astype(F32), jnp.int32)
        hi = lax.bitcast_convert_type(
            t_ref[:, dm // 2 + c * LANES:dm // 2 + (c + 1) * LANES].astype(BF16).astype(F32), jnp.int32)
        o_ref[pl.ds(c, rows, stride=ROWS_PER_EXPERT), :] = lax.shift_right_logical(lo, 16) | hi


def _pack_table(tables, layer, *, rows=512):
    _, e, dm = tables.shape
    assert dm == 2 * ROWS_PER_EXPERT * LANES
    rows = min(rows, e)
    return pl.pallas_call(
        _pack_table_kernel,
        grid=(e // rows,),
        in_specs=[pl.BlockSpec((None, rows, dm), lambda i: (layer, i, 0))],
        out_specs=pl.BlockSpec((rows * ROWS_PER_EXPERT, LANES), lambda i: (i, 0)),
        out_shape=jax.ShapeDtypeStruct((e * ROWS_PER_EXPERT, LANES), jnp.int32),
        compiler_params=_cparams(("parallel",), VMEM_LIMIT),
        name="pack_table",
    )(tables)


def _unpack_words(w):
    lo = lax.bitcast_convert_type(w << 16, F32)
    hi = lax.bitcast_convert_type(w & jnp.int32(-65536), F32)
    return lo, hi


def _gather_chunk(idx_ref, tab_ref, buf_ref, c, group, slots):
    for g in range(PAIR_CHUNK // group):
        ids = idx_ref.at[pl.ds(pl.multiple_of(c * PAIR_CHUNK + g * group, group), group)]
        for i in range(group):
            e4 = pl.multiple_of(ids[i], ROWS_PER_EXPERT)
            s = slots[g * group + i] * ROWS_PER_EXPERT
            buf_ref[s:s + ROWS_PER_EXPERT, :] = tab_ref[pl.ds(e4, ROWS_PER_EXPERT), :]


def _chunk_loop(n_chunks, idx_ref, tab_ref, buf_a, buf_b, consume, init, group, slots, per_step=2, extra=None):
    last = n_chunks - 1
    gather = functools.partial(_gather_chunk, idx_ref, tab_ref, group=group, slots=slots)
    gather(buf_a, 0)
    bufs = (buf_a, buf_b)

    def body(i, carry):
        if extra is not None:
            extra(i)
        c = per_step * i
        for u in range(per_step):
            carry = consume(c + u, bufs[u % 2], carry)
            nxt = c + u + 1
            gather(bufs[(u + 1) % 2], jnp.minimum(nxt, last) if u == per_step - 1 else nxt)
        return carry

    return lax.fori_loop(0, n_chunks // per_step, body, init)


def _peer_u_kernel(idx_ref, h_ref, gate_ref, tab_ref, coef_ref, buf_a, buf_b, r_even, r_odd, d_ref, *, tt, n_tiles):
    step = pl.program_id(0)
    sub = lax.broadcasted_iota(jnp.int32, (SUBLANES, LANES), 0)
    low = sub < ROWS_PER_EXPERT
    m_a = ((sub % 4) >= 2)[None]
    m_b = ((sub % 2) == 1)[None]
    nv = PAIR_CHUNK // 2

    def pair_partials(r_ref, c, buf_ref, carry):
        t = c >> CHUNK_SHIFT
        h8 = h_ref[pl.ds(pl.multiple_of(t * SUBLANES, SUBLANES), SUBLANES), :]
        sw = pltpu.roll(h8, ROWS_PER_EXPERT, axis=0)
        ha = jnp.where(low, h8, sw)[None]
        hb = jnp.where(low, sw, h8)[None]
        lo, hi = _unpack_words(buf_ref[...])
        x = lo.reshape(nv, SUBLANES, LANES) * ha + hi.reshape(nv, SUBLANES, LANES) * hb
        x = x.reshape(nv // 2, 2, SUBLANES, LANES)
        xe, xo = x[:, 0], x[:, 1]
        a = xe + pltpu.roll(xe, 2, axis=1)
        b = xo + pltpu.roll(xo, 6, axis=1)
        m = jnp.where(m_a, a, b).reshape(nv // 4, 2, SUBLANES, LANES)
        me, mo = m[:, 0], m[:, 1]
        a2 = me + pltpu.roll(me, 1, axis=1)
        b2 = mo + pltpu.roll(mo, 7, axis=1)
        r_ref[pl.ds(pl.multiple_of(c * PAIR_CHUNK, PAIR_CHUNK), PAIR_CHUNK), :] = (
            jnp.where(m_b, a2, b2).reshape(PAIR_CHUNK, LANES))
        return carry

    def token_sum(r_prev, t):
        r_t = r_prev[pl.ds(pl.multiple_of(t * PEER_SEL, PEER_SEL), PEER_SEL), :]
        d_ref[pl.ds(t, 1), :] = jnp.sum(r_t.T, axis=0, keepdims=True)

    def loop(r_cur, **kw):
        _chunk_loop(tt * N_CHUNKS, idx_ref, tab_ref, buf_a, buf_b, functools.partial(pair_partials, r_cur), 0,
                    SMEM_GROUP, _CHUNK_SLOT, **kw)

    @pl.when(step == 0)
    def _():
        loop(r_even)

    for parity, (r_cur, r_prev) in enumerate(((r_even, r_odd), (r_odd, r_even))):
        @pl.when(jnp.logical_and(jnp.logical_and(step > 0, step < n_tiles), step % 2 == parity))
        def _(r_cur=r_cur, r_prev=r_prev):
            loop(r_cur, per_step=N_CHUNKS, extra=functools.partial(token_sum, r_prev))

    @pl.when(step == n_tiles)
    def _():
        def only_sums(t, carry):
            token_sum(r_even if n_tiles % 2 else r_odd, t)
            return carry
        lax.fori_loop(0, tt, only_sums, 0)

    @pl.when(step > 0)
    def _():
        coef_ref[...] = gate_ref[...] * _gelu(d_ref[...])


def _peer_u(idx4, h8, gates, tab, *, tt=128):
    n_tok = gates.shape[0]
    tt = min(tt, n_tok)
    n_tiles = n_tok // tt
    cur = lambda i: jnp.minimum(i, n_tiles - 1)
    done = lambda i: jnp.maximum(i - 1, 0)
    return pl.pallas_call(
        functools.partial(_peer_u_kernel, tt=tt, n_tiles=n_tiles),
        grid=(n_tiles + 1,),
        in_specs=[pl.BlockSpec((tt * PEER_SEL,), lambda i: (cur(i),), memory_space=pltpu.SMEM),
                  pl.BlockSpec((tt * SUBLANES, LANES), lambda i: (cur(i), 0)),
                  pl.BlockSpec((tt, PEER_SEL), lambda i: (done(i), 0)),
                  pl.BlockSpec(tab.shape, lambda i: (0, 0), pipeline_mode=pl.Buffered(1))],
        out_specs=pl.BlockSpec((tt, PEER_SEL), lambda i: (done(i), 0)),
        out_shape=jax.ShapeDtypeStruct((n_tok, PEER_SEL), F32),
        scratch_shapes=[pltpu.VMEM((CHUNK_ROWS, LANES), jnp.int32),
                        pltpu.VMEM((CHUNK_ROWS, LANES), jnp.int32),
                        pltpu.VMEM((tt * PEER_SEL, LANES), F32),
                        pltpu.VMEM((tt * PEER_SEL, LANES), F32),
                        pltpu.VMEM((tt, PEER_SEL), F32)],
        compiler_params=_cparams(("arbitrary",), PEER_U_VMEM_LIMIT),
        name="peer_u",
    )(idx4, h8, gates, tab)


V_TOKENS_PER_STEP = 8


def _peer_v_kernel(idx_ref, cx_ref, tab_ref, o_ref, *scratch, tt):
    bufs, parts_ref = (scratch[:N_CHUNKS], scratch[N_CHUNKS:2 * N_CHUNKS]), scratch[2 * N_CHUNKS]
    n_chunks = tt * N_CHUNKS
    row = lax.broadcasted_iota(jnp.int32, (SUBLANES, CHUNK_ROWS), 0)
    col = lax.broadcasted_iota(jnp.int32, (SUBLANES, CHUNK_ROWS), 1)
    quarter = (col % ROWS_PER_EXPERT) == (row % ROWS_PER_EXPERT)
    m_lo = jnp.logical_and(row < ROWS_PER_EXPERT, quarter)
    m_hi = jnp.logical_and(row >= ROWS_PER_EXPERT, quarter)
    gather = functools.partial(_gather_chunk, idx_ref, tab_ref, group=PAIR_CHUNK, slots=tuple(range(PAIR_CHUNK)))

    def chunk_sum(t, parity, part, buf_ref):
        cx = jnp.broadcast_to(cx_ref[t, pl.ds(part, 1), :], (SUBLANES, CHUNK_ROWS))
        a = jnp.concatenate([jnp.where(m_lo, cx, 0.0), jnp.where(m_hi, cx, 0.0)], axis=1).astype(BF16)
        lo, hi = _unpack_words(buf_ref[...])
        b = jnp.concatenate([lo.astype(BF16), hi.astype(BF16)], axis=0)
        parts_ref[parity, part] = jnp.dot(a, b, preferred_element_type=F32)

    parts_ref[...] = jnp.zeros_like(parts_ref)
    for j in range(N_CHUNKS):
        gather(bufs[0][j], j)

    def body(i, carry):
        for s in range(V_TOKENS_PER_STEP):
            half = s % 2
            t = V_TOKENS_PER_STEP * i + s
            prev = jnp.maximum(t - 1, 0)
            o_ref[prev] = _tree(jnp.add, [parts_ref[1 - half, j] for j in range(N_CHUNKS)])
            for j in range(N_CHUNKS):
                chunk_sum(t, half, j, bufs[half][j])
            for j in range(N_CHUNKS):
                gather(bufs[1 - half][j], jnp.minimum((t + 1) * N_CHUNKS + j, n_chunks - 1))
        return carry

    lax.fori_loop(0, tt // V_TOKENS_PER_STEP, body, 0)
    o_ref[tt - 1] = _tree(jnp.add, [parts_ref[1, j] for j in range(N_CHUNKS)])


def _peer_v(idx4, coef, tab, *, tt=128):
    n_tok = idx4.shape[0] // PEER_SEL
    tt = min(tt, n_tok)
    assert V_TOKENS_PER_STEP % 2 == 0 and tt % V_TOKENS_PER_STEP == 0
    cx = jnp.repeat(coef, ROWS_PER_EXPERT, axis=1).reshape(n_tok, N_CHUNKS, CHUNK_ROWS)
    smem = pl.BlockSpec((tt * PEER_SEL,), lambda i: (i,), memory_space=pltpu.SMEM)
    return pl.pallas_call(
        functools.partial(_peer_v_kernel, tt=tt),
        grid=(n_tok // tt,),
        in_specs=[smem, pl.BlockSpec((tt, N_CHUNKS, CHUNK_ROWS), lambda i: (i, 0, 0)),
                  pl.BlockSpec(tab.shape, lambda i: (0, 0), pipeline_mode=pl.Buffered(1))],
        out_specs=pl.BlockSpec((tt, SUBLANES, LANES), lambda i: (i, 0, 0)),
        out_shape=jax.ShapeDtypeStruct((n_tok, SUBLANES, LANES), F32),
        scratch_shapes=[pltpu.VMEM((CHUNK_ROWS, LANES), jnp.int32)] * (2 * N_CHUNKS)
                       + [pltpu.VMEM((2, N_CHUNKS, SUBLANES, LANES), F32)],
        compiler_params=_cparams(("arbitrary",), VMEM_LIMIT),
        name="peer_v",
    )(idx4, cx, tab)


def _peer(h8, qp, sub_keys, tab_u, tab_v):
    n_tok = qp.shape[1]
    half = sub_keys.shape[-1]
    z = jnp.zeros_like(sub_keys[0])
    keys_pad = jnp.stack([jnp.concatenate([sub_keys[0], z], axis=-1),
                          jnp.concatenate([z, sub_keys[1]], axis=-1)]).astype(BF16)
    assert keys_pad.shape[-1] == 2 * half == qp.shape[-1]
    idx4, gates = _peer_topk(qp, keys_pad)
    idx4 = idx4.reshape(n_tok * PEER_SEL)
    coef = _peer_u(idx4, h8, gates, tab_u)
    out = _peer_v(idx4, coef, tab_v)
    return out.reshape(n_tok * SUBLANES, LANES)


def _final_kernel(x_ref, p_ref, g_ref, o_ref):
    o_ref[...] = _rms(x_ref[...] + _rows_from_chunks(p_ref, x_ref.shape[0]), g_ref[...])


def _final_norm(x, p, g, *, tr=1024):
    rows, dm = x.shape
    tr = min(tr, rows)
    blk = pl.BlockSpec((tr, dm), lambda i: (i, 0))
    return pl.pallas_call(
        _final_kernel,
        grid=(rows // tr,),
        in_specs=[blk, pl.BlockSpec((tr * SUBLANES, LANES), lambda i: (i, 0)),
                  pl.BlockSpec((1, dm), lambda i: (0, 0))],
        out_specs=blk,
        out_shape=jax.ShapeDtypeStruct((rows, dm), F32),
        compiler_params=_cparams(("parallel",), VMEM_LIMIT),
        name="final_norm",
    )(x, p, g.reshape(1, dm))


def kernel(x, w_in, w_out, rel_bias, g_attn, g_ssm, norm_mix, norm_ffn, lam_re, lam_im, log_step, b_re, b_im, c_re, c_im, d_skip, w_glu, w_query, sub_keys, expert_u, expert_v, norm_final):
    bsz, seq, dm = x.shape
    depth = w_in.shape[0]
    prev = None
    biases = [_attn_bias_tables(rel_bias, d) for _, d in DILATED_PATTERNS]
    for l in range(depth):
        x, q, k, v, u_tm = _in_proj(x, prev, norm_mix[l], w_in[l].astype(BF16))
        attn = _attention(q, k, v, biases)
        a, bw, cw = _ssm_params(lam_re[l], lam_im[l], log_step[l], b_re[l], b_im[l], c_re[l], c_im[l])
        y = _ssm_scan(u_tm, a, bw, cw, bsz=bsz)
        wo = w_out[l].astype(BF16)
        z = _ssm_post(y, u_tm, d_skip[l], w_glu[l].astype(BF16), g_ssm[l], wo[ATTN_WIDTH:], bsz=bsz)
        x, h, qp = _mix_out(x, z, attn,
                            g_attn[l], wo[:ATTN_WIDTH], norm_ffn[l], w_query[l].astype(BF16))
        prev = _peer(h, qp, sub_keys[l], _pack_table(expert_u, l), _pack_table(expert_v, l))
    out = _final_norm(x.reshape(bsz * seq, dm), prev, norm_final)
    return out.reshape(bsz, seq, dm)
```

```python
import functools
import math

import numpy as np
import jax
import jax.numpy as jnp
from jax import lax
from jax.experimental import pallas as pl
from jax.experimental.pallas import tpu as pltpu

F32 = jnp.float32
BF16 = jnp.bfloat16

EPS = 1e-6
NEG_INF = -1e30
HEAD_DIM = 64
ATTN_WIDTH = 512
SSM_WIDTH = 512
SSM_GROUP = 16
SSM_STATE = 64
DILATED_PATTERNS = ((128, 1), (512, 4), (2048, 16))
REL_BUCKETS = 32
REL_MAX_DISTANCE = 1024
PEER_HEADS = 8
PEER_KEYS = 128
PEER_TOPK = 16
PEER_SEL = PEER_HEADS * PEER_TOPK

LANES = 128
SUBLANES = 8
QBLK = 128
KWIN = 256
BAND = 64
VMEM_LIMIT = 52 * 1024 * 1024
PEER_U_VMEM_LIMIT = 58 * 1024 * 1024


def _cparams(sem, vmem=None):
    return pltpu.CompilerParams(dimension_semantics=sem, vmem_limit_bytes=vmem)


def _rms(x, g):
    return x * lax.rsqrt(jnp.mean(x * x, axis=-1, keepdims=True) + EPS) * g


def _gelu(x):
    return 0.5 * x * (1.0 + lax.erf(x * (1.0 / math.sqrt(2.0))))


def _rows_from_chunks(p_ref, n_rows):
    return jnp.concatenate([p_ref[pl.ds(c, n_rows, stride=SUBLANES), :] for c in range(SUBLANES)], axis=-1)


def _in_proj_kernel(*refs, has_prev):
    if has_prev:
        x_ref, p_ref, g_ref, w_ref, xo_ref, q_ref, k_ref, v_ref, u_ref = refs
        x = x_ref[0] + _rows_from_chunks(p_ref, x_ref.shape[1])
    else:
        x_ref, g_ref, w_ref, xo_ref, q_ref, k_ref, v_ref, u_ref = refs
        x = x_ref[0]
    xo_ref[0] = x
    h = _rms(x, g_ref[...]).astype(BF16)
    proj = jnp.dot(h, w_ref[...], preferred_element_type=F32)
    a = ATTN_WIDTH
    q_ref[0] = proj[:, :a] * (HEAD_DIM ** -0.5)
    k_ref[0] = proj[:, a:2 * a]
    v_ref[0] = proj[:, 2 * a:3 * a]
    b, bsz = pl.program_id(1), pl.num_programs(1)
    for c in range(SSM_WIDTH // LANES):
        u_ref[c, pl.ds(b, proj.shape[0], stride=bsz), :] = proj[:, 3 * a + c * LANES:3 * a + (c + 1) * LANES]


def _in_proj(x, prev, g, w_bf16, *, ts=256):
    bsz, seq, dm = x.shape
    ts = min(ts, seq)
    row = pl.BlockSpec((1, ts, dm), lambda i, b: (b, i, 0))
    qkv = pl.BlockSpec((1, ts, ATTN_WIDTH), lambda i, b: (b, i, 0))
    ns = seq // ts
    chunks = pl.BlockSpec((ts * SUBLANES, LANES), lambda i, b: (b * ns + i, 0))
    ins = [x] + ([prev] if prev is not None else []) + [g.reshape(1, dm), w_bf16]
    in_specs = [row] + ([chunks] if prev is not None else []) + [
        pl.BlockSpec((1, dm), lambda i, b: (0, 0)),
        pl.BlockSpec(w_bf16.shape, lambda i, b: (0, 0)),
    ]
    slabs = SSM_WIDTH // LANES
    return pl.pallas_call(
        functools.partial(_in_proj_kernel, has_prev=prev is not None),
        grid=(seq // ts, bsz),
        in_specs=in_specs,
        out_specs=[row, qkv, qkv, qkv, pl.BlockSpec((slabs, ts * bsz, LANES), lambda i, b: (0, i, 0))],
        out_shape=[
            jax.ShapeDtypeStruct((bsz, seq, dm), F32),
            jax.ShapeDtypeStruct((bsz, seq, ATTN_WIDTH), F32),
            jax.ShapeDtypeStruct((bsz, seq, ATTN_WIDTH), F32),
            jax.ShapeDtypeStruct((bsz, seq, ATTN_WIDTH), F32),
            jax.ShapeDtypeStruct((slabs, seq * bsz, LANES), F32),
        ],
        compiler_params=_cparams(("parallel", "arbitrary"), VMEM_LIMIT),
        name="in_proj",
    )(*ins)


def _t5_buckets(rel):
    half = REL_BUCKETS // 2
    max_exact = half // 2
    n = np.abs(rel)
    large = max_exact + (np.log(np.maximum(n, 1) / max_exact)
                         / np.log(REL_MAX_DISTANCE / max_exact) * (half - max_exact)).astype(np.int32)
    large = np.minimum(large, half - 1)
    return (np.where(rel > 0, half, 0) + np.where(n < max_exact, n, large)).astype(np.int32)


def _attn_bias_tables(rel_bias, dilation):
    ql = np.arange(QBLK)[:, None]
    kl = np.arange(KWIN)[None, :]
    delta = np.stack([kl + off - ql for off in (0, -BAND, -2 * BAND)])
    buckets = np.where(np.abs(delta) <= BAND, _t5_buckets(delta * dilation), -1)
    rb = rel_bias.astype(F32).T
    bk = jnp.asarray(buckets, jnp.int32)[None]
    tab = jnp.full((rb.shape[0],) + buckets.shape, NEG_INF, F32)
    for b in range(REL_BUCKETS):
        tab = jnp.where(bk == b, rb[:, b][:, None, None, None], tab)
    return tab


BLOCKS_PER_STEP = 4


def _attn_kernel(q_ref, k_ref, v_ref, *rest, seq):
    bias_refs, (o_ref, acc_ref, m_ref, z_ref) = rest[:len(DILATED_PATTERNS)], rest[len(DILATED_PATTERNS):]
    lane = lax.broadcasted_iota(jnp.int32, (QBLK, LANES), 1)
    is_h0 = lane < HEAD_DIM
    dn = (((1,), (1,)), ((), ()))
    nsteps = seq // QBLK

    def rows(start, size, d):
        return pl.ds(start, size) if d == 1 else pl.ds(start, size, stride=d)

    def block(n, d, bias_ref, first, last):
        length = seq // d
        nblk = length // QBLK
        r, i = n >> (nblk.bit_length() - 1), n & (nblk - 1)
        s = i * QBLK
        ks = jnp.clip(s - BAND, 0, length - KWIN)
        var = jnp.where(i == 0, 0, jnp.where(i == nblk - 1, 2, 1))
        q_rows = rows(r + d * s, QBLK, d)
        k_rows = rows(r + d * ks, KWIN, d)
        qb = q_ref[q_rows, :].astype(BF16)
        kb = k_ref[k_rows, :].astype(BF16)
        vb = v_ref[k_rows, :].astype(BF16)
        outs, ms, zs = [], [], []
        for h in range(2):
            keep = is_h0 if h == 0 else jnp.logical_not(is_h0)
            qh = jnp.where(keep, qb, jnp.zeros_like(qb))
            logits = lax.dot_general(qh, kb, dn, preferred_element_type=F32) + bias_ref[h, var]
            m = jnp.max(logits, axis=-1, keepdims=True)
            p = jnp.exp(logits - m)
            outs.append(jnp.dot(p.astype(BF16), vb, preferred_element_type=F32))
            ms.append(jnp.broadcast_to(m, (QBLK, LANES)))
            zs.append(jnp.broadcast_to(jnp.sum(p, axis=-1, keepdims=True), (QBLK, LANES)))
        o = jnp.where(is_h0, outs[0], outs[1])
        m = jnp.where(is_h0, ms[0], ms[1])
        z = jnp.where(is_h0, zs[0], zs[1])
        if not first:
            m_old = m_ref[q_rows, :]
            m_new = jnp.maximum(m_old, m)
            a, b = jnp.exp(m_old - m_new), jnp.exp(m - m_new)
            o = acc_ref[q_rows, :] * a + o * b
            z = z_ref[q_rows, :] * a + z * b
            m = m_new
        if last:
            o_ref[q_rows, :] = o / z
        else:
            acc_ref[q_rows, :] = o
            m_ref[q_rows, :] = m
            z_ref[q_rows, :] = z

    for p, ((_, d), bias_ref) in enumerate(zip(DILATED_PATTERNS, bias_refs)):
        def step(g, carry, d=d, bias_ref=bias_ref, p=p):
            for j in range(BLOCKS_PER_STEP):
                block(g * BLOCKS_PER_STEP + j, d, bias_ref, p == 0, p == len(DILATED_PATTERNS) - 1)
            return carry
        lax.fori_loop(0, nsteps // BLOCKS_PER_STEP, step, 0)


def _attention(q, k, v, biases):
    bsz, seq, width = q.shape
    for _, d in DILATED_PATTERNS:
        length = seq // d
        assert length >= KWIN and length % QBLK == 0 and (length // QBLK) & (length // QBLK - 1) == 0
    assert (seq // QBLK) % BLOCKS_PER_STEP == 0
    blk = pl.BlockSpec((None, seq, LANES), lambda b, c: (b, 0, c))
    bias_spec = pl.BlockSpec((2, 3, QBLK, KWIN), lambda b, c: (c, 0, 0, 0))
    return pl.pallas_call(
        functools.partial(_attn_kernel, seq=seq),
        grid=(bsz, width // LANES),
        in_specs=[blk, blk, blk] + [bias_spec] * len(biases),
        out_specs=blk,
        out_shape=jax.ShapeDtypeStruct((bsz, seq, width), F32),
        scratch_shapes=[pltpu.VMEM((seq, LANES), F32)] * 3,
        compiler_params=_cparams(("parallel", "arbitrary"), VMEM_LIMIT),
        name="attention",
    )(q, k, v, *biases)


SSM_LANE_GROUPS = SSM_WIDTH // LANES
SSM_GB_STATES = (LANES // SSM_GROUP) * SSM_STATE


def _ssm_params(lam_re, lam_im, log_step, b_re, b_im, c_re, c_im):
    f = lambda t: t.astype(F32)
    lr, li = f(lam_re), f(lam_im)
    step = jnp.exp(f(log_step))[..., None]
    mag = jnp.exp(lr * step)
    ar, ai = mag * jnp.cos(li * step), mag * jnp.sin(li * step)
    nr, ni = ar - 1.0, ai
    den = lr * lr + li * li
    cr, ci = (nr * lr + ni * li) / den, (ni * lr - nr * li) / den
    br, bi = f(b_re), f(b_im)
    bbr = cr[..., None] * br - ci[..., None] * bi
    bbi = cr[..., None] * bi + ci[..., None] * br
    gpb = LANES // SSM_GROUP
    eye = jnp.eye(gpb, dtype=F32)

    def in_map(t):
        t = t.reshape(2, SSM_LANE_GROUPS, gpb, SSM_STATE, SSM_GROUP)
        return jnp.einsum('dbgpc,gh->dbgchp', t, eye).reshape(2, SSM_LANE_GROUPS, LANES, SSM_GB_STATES)

    def out_map(t):
        t = t.reshape(2, SSM_LANE_GROUPS, gpb, SSM_GROUP, SSM_STATE)
        return jnp.einsum('dbgcp,gh->dbgphc', t, eye).reshape(2, SSM_LANE_GROUPS, SSM_GB_STATES, LANES)

    bw = jnp.concatenate([in_map(bbr), in_map(bbi)], axis=-1).astype(BF16)
    cw = jnp.concatenate([out_map(f(c_re)), -out_map(f(c_im))], axis=-2).astype(BF16)
    a = jnp.stack([ar.reshape(2, SSM_LANE_GROUPS, SSM_GB_STATES),
                   ai.reshape(2, SSM_LANE_GROUPS, SSM_GB_STATES)], axis=2)
    return a, bw, cw


def _ssm_kernel(u_ref, a_ref, bw_ref, cw_ref, y_ref, st_ref, bu_ref, *, ts, bsz):
    d = pl.program_id(0)
    ns = SSM_GB_STATES

    @pl.when(pl.program_id(1) == 0)
    def _():
        st_ref[...] = jnp.zeros_like(st_ref)

    for gb in range(SSM_LANE_GROUPS):
        ub = u_ref[gb].astype(BF16)
        bu_ref[...] = jnp.dot(ub, bw_ref[0, gb], preferred_element_type=F32)
        ar = jnp.broadcast_to(a_ref[0, gb, 0:1, :], (bsz, ns))
        ai = jnp.broadcast_to(a_ref[0, gb, 1:2, :], (bsz, ns))

        def step(j, carry, ar=ar, ai=ai):
            xr, xi = carry
            tl = jnp.where(d == 0, j, ts - 1 - j)
            r = pl.multiple_of(tl * bsz, bsz)
            nr = ar * xr - ai * xi + bu_ref[pl.ds(r, bsz), :ns]
            ni = ar * xi + ai * xr + bu_ref[pl.ds(r, bsz), ns:]
            bu_ref[pl.ds(r, bsz), :ns] = nr
            bu_ref[pl.ds(r, bsz), ns:] = ni
            return nr, ni

        xr, xi = lax.fori_loop(0, ts, step, (st_ref[gb, :, :ns], st_ref[gb, :, ns:]))
        st_ref[gb, :, :ns] = xr
        st_ref[gb, :, ns:] = xi
        y_ref[0, :, gb * LANES:(gb + 1) * LANES] = jnp.dot(
            bu_ref[...].astype(BF16), cw_ref[0, gb], preferred_element_type=F32)


def _ssm_scan(u_tm, a, bw, cw, *, bsz, ts=64):
    slabs, rows, _ = u_tm.shape
    width = slabs * LANES
    seq = rows // bsz
    ts = min(ts, seq)
    nt = seq // ts
    tblk = lambda d, i: jnp.where(d == 0, i, nt - 1 - i)
    return pl.pallas_call(
        functools.partial(_ssm_kernel, ts=ts, bsz=bsz),
        grid=(2, nt),
        in_specs=[
            pl.BlockSpec((slabs, ts * bsz, LANES), lambda d, i: (0, tblk(d, i), 0)),
            pl.BlockSpec((1,) + a.shape[1:], lambda d, i: (d, 0, 0, 0)),
            pl.BlockSpec((1,) + bw.shape[1:], lambda d, i: (d, 0, 0, 0)),
            pl.BlockSpec((1,) + cw.shape[1:], lambda d, i: (d, 0, 0, 0)),
        ],
        out_specs=pl.BlockSpec((1, ts * bsz, width), lambda d, i: (d, tblk(d, i), 0)),
        out_shape=jax.ShapeDtypeStruct((2, rows, width), F32),
        scratch_shapes=[pltpu.VMEM((SSM_LANE_GROUPS, bsz, 2 * SSM_GB_STATES), F32),
                        pltpu.VMEM((ts * bsz, 2 * SSM_GB_STATES), F32)],
        compiler_params=_cparams(("arbitrary", "arbitrary"), VMEM_LIMIT),
        name="ssm_scan",
    )(u_tm, a, bw, cw)


def _ssm_post_kernel(y_ref, u_ref, d_ref, wg_ref, g_ref, wo_ref, z_ref, slab_ref, *, bsz):
    u = jnp.concatenate([u_ref[c] for c in range(u_ref.shape[0])], axis=-1)
    y = _gelu(y_ref[0] + y_ref[1] + d_ref[...] * u).astype(BF16)
    ab = jnp.dot(y, wg_ref[...], preferred_element_type=F32)
    ssm = ab[:, :SSM_WIDTH] * jax.nn.sigmoid(ab[:, SSM_WIDTH:])
    n = _rms(ssm, g_ref[...]).astype(BF16)
    z = jnp.dot(n, wo_ref[...], preferred_element_type=F32)
    nt, dm = z.shape[0] // bsz, z.shape[1]
    for c in range(dm // LANES):
        slab_ref[c] = z[:, c * LANES:(c + 1) * LANES]
    for b in range(bsz):
        for c in range(dm // LANES):
            z_ref[:, b * dm + c * LANES:b * dm + (c + 1) * LANES] = slab_ref[c, pl.ds(b, nt, stride=bsz), :]


def _ssm_post(y, u_tm, d_skip, w_glu_bf16, g_ssm, w_out_ssm_bf16, *, bsz, tr=512):
    slabs, rows, _ = u_tm.shape
    width = slabs * LANES
    tr = min(tr, rows)
    assert tr % bsz == 0
    dm = w_out_ssm_bf16.shape[1]
    full = lambda a: pl.BlockSpec(a.shape, lambda i: (0,) * a.ndim)
    d2, g2 = d_skip.reshape(1, width), g_ssm.reshape(1, width)
    return pl.pallas_call(
        functools.partial(_ssm_post_kernel, bsz=bsz),
        grid=(rows // tr,),
        in_specs=[pl.BlockSpec((2, tr, width), lambda i: (0, i, 0)),
                  pl.BlockSpec((slabs, tr, LANES), lambda i: (0, i, 0)),
                  full(d2), full(w_glu_bf16), full(g2), full(w_out_ssm_bf16)],
        out_specs=pl.BlockSpec((tr // bsz, bsz * dm), lambda i: (i, 0)),
        out_shape=jax.ShapeDtypeStruct((rows // bsz, bsz * dm), F32),
        scratch_shapes=[pltpu.VMEM((dm // LANES, tr, LANES), F32)],
        compiler_params=_cparams(("parallel",), VMEM_LIMIT),
        name="ssm_post",
    )(y, u_tm, d2, w_glu_bf16, g2, w_out_ssm_bf16)


def _mix_out_kernel(x_ref, z_ref, a_ref, ga_ref, wo_ref, gf_ref, wq_ref, xn_ref, h_ref, q_ref):
    n = _rms(a_ref[0], ga_ref[...]).astype(BF16)
    xn = x_ref[0] + z_ref[...] + jnp.dot(n, wo_ref[...], preferred_element_type=F32)
    xn_ref[0] = xn
    h = _rms(xn, gf_ref[...])
    for c in range(SUBLANES):
        h_ref[pl.ds(c, h.shape[0], stride=SUBLANES), :] = h[:, c * LANES:(c + 1) * LANES]
    qp = jnp.dot(h.astype(BF16), wq_ref[...], preferred_element_type=F32)
    for hd in range(PEER_HEADS):
        q_ref[hd] = qp[:, hd * LANES:(hd + 1) * LANES]


def _mix_out(x, z_tm, attn, g_attn, w_out_attn_bf16, norm_ffn, w_query, *, ts=256):
    bsz, seq, dm = x.shape
    ts = min(ts, seq)
    ns = seq // ts
    row = pl.BlockSpec((1, ts, dm), lambda b, i: (b, i, 0))
    half = pl.BlockSpec((1, ts, ATTN_WIDTH), lambda b, i: (b, i, 0))
    full = lambda a: pl.BlockSpec(a.shape, lambda b, i: (0,) * a.ndim)
    ga, gf = g_attn.reshape(1, ATTN_WIDTH), norm_ffn.reshape(1, dm)
    qdim = w_query.shape[1] // PEER_HEADS
    return pl.pallas_call(
        _mix_out_kernel,
        grid=(bsz, ns),
        in_specs=[row, pl.BlockSpec((ts, dm), lambda b, i: (i, b)), half,
                  full(ga), full(w_out_attn_bf16), full(gf), full(w_query)],
        out_specs=[row, pl.BlockSpec((ts * SUBLANES, LANES), lambda b, i: (b * ns + i, 0)),
                   pl.BlockSpec((PEER_HEADS, ts, qdim), lambda b, i: (0, b * ns + i, 0))],
        out_shape=[jax.ShapeDtypeStruct((bsz, seq, dm), F32),
                   jax.ShapeDtypeStruct((bsz * seq * SUBLANES, LANES), F32),

                   jax.ShapeDtypeStruct((PEER_HEADS, bsz * seq, qdim), F32)],
        compiler_params=_cparams(("parallel", "arbitrary"), VMEM_LIMIT),
        name="mix_out",
    )(x, z_tm, attn, ga, w_out_attn_bf16, gf, w_query)


TOPK_TOKENS = SUBLANES * LANES
KEY_PITCH = PEER_KEYS + 4
_CANDIDATES = tuple((a, b) for a in range(PEER_TOPK) for b in range(PEER_TOPK) if (a + 1) * (b + 1) <= PEER_TOPK)


def _tree(op, xs):
    xs = list(xs)
    while len(xs) > 1:
        xs = [op(xs[i], xs[i + 1]) if i + 1 < len(xs) else xs[i] for i in range(0, len(xs), 2)]
    return xs[0]


def _extract16(problems):
    ninf = jnp.float32(-jnp.inf)

    def better(a, b):
        gt = b[0] > a[0]
        return tuple(jnp.where(gt, y, x) for x, y in zip(a, b))

    def step(r, carry):
        for p in problems:
            s_ref, order = p["s"], p["order"]
            assert list(order) == sorted(order)
            n = len(order)
            rows = [(s_ref[k], order[k]) + ((p["pay"][k],) if p.get("pay") is not None else ())
                    for k in range(n)]
            win = _tree(better, rows)
            m, am = win[0], win[1]
            for k in range(n):
                s_ref[k] = jnp.where(am == order[k], ninf, s_ref[k])
            p["vals"][r] = m
            p["picks"][r] = win[2] if len(win) > 2 else am.astype(jnp.int32)
        return carry

    lax.fori_loop(0, PEER_TOPK, step, 0)


def _peer_topk_kernel(q_ref, k_ref, idx_ref, gate_ref,
                      slab_ref, s1_ref, s2_ref, t1_ref, i1_ref, t2_ref, i2_ref, cand_ref, pay_ref, ts_ref, ex_ref):
    dn = (((1,), (1,)), ((), ()))
    keys = tuple(range(PEER_KEYS))

    def head(h, carry):
        for w, s_ref in ((0, s1_ref), (1, s2_ref)):
            for j in range(SUBLANES):
                slab_ref[j * KEY_PITCH:j * KEY_PITCH + PEER_KEYS, :] = lax.dot_general(
                    k_ref[w, h], q_ref[h, j * LANES:(j + 1) * LANES, :].astype(BF16), dn,
                    preferred_element_type=F32)
            for k in range(PEER_KEYS):
                s_ref[k] = slab_ref[pl.ds(k, SUBLANES, stride=KEY_PITCH), :]
        _extract16([dict(s=s1_ref, order=keys, vals=t1_ref, picks=i1_ref),
                    dict(s=s2_ref, order=keys, vals=t2_ref, picks=i2_ref)])
        for c, (a, b) in enumerate(_CANDIDATES):
            cand_ref[c] = t1_ref[a] + t2_ref[b]
            pay_ref[c] = i1_ref[a] * PEER_KEYS + i2_ref[b]
        _extract16([dict(s=cand_ref, order=tuple(a * PEER_TOPK + b for a, b in _CANDIDATES), pay=pay_ref,
                         vals=ts_ref, picks=ex_ref)])
        top_s = ts_ref[...]
        e = jnp.exp(top_s - jnp.max(top_s, axis=0, keepdims=True))
        gate_ref[0, h] = e / jnp.sum(e, axis=0, keepdims=True)
        idx_ref[0, h] = ex_ref[...] * ROWS_PER_EXPERT
        return carry

    lax.fori_loop(0, PEER_HEADS, head, 0)


def _peer_topk(qp, keys_pad):
    n_tok = qp.shape[1]
    tt = TOPK_TOKENS
    assert n_tok % tt == 0
    shp = (n_tok // tt, PEER_HEADS, PEER_TOPK, SUBLANES, LANES)
    out = pl.BlockSpec((1,) + shp[1:], lambda i: (i, 0, 0, 0, 0))
    vregs = lambda n, dt: pltpu.VMEM((n, SUBLANES, LANES), dt)
    idx, gate = pl.pallas_call(
        _peer_topk_kernel,
        grid=(n_tok // tt,),
        in_specs=[pl.BlockSpec((PEER_HEADS, tt, qp.shape[2]), lambda i: (0, i, 0)),
                  pl.BlockSpec(keys_pad.shape, lambda i: (0, 0, 0, 0))],
        out_specs=[out, out],
        out_shape=[jax.ShapeDtypeStruct(shp, jnp.int32), jax.ShapeDtypeStruct(shp, F32)],
        scratch_shapes=[pltpu.VMEM((SUBLANES * KEY_PITCH, LANES), F32),
                        vregs(PEER_KEYS, F32), vregs(PEER_KEYS, F32),
                        vregs(PEER_TOPK, F32), vregs(PEER_TOPK, jnp.int32),
                        vregs(PEER_TOPK, F32), vregs(PEER_TOPK, jnp.int32),
                        vregs(len(_CANDIDATES), F32), vregs(len(_CANDIDATES), jnp.int32),
                        vregs(PEER_TOPK, F32), vregs(PEER_TOPK, jnp.int32)],
        compiler_params=_cparams(("parallel",), VMEM_LIMIT),
        name="peer_topk",
    )(qp, keys_pad)
    to_tok = lambda a: jnp.transpose(a, (0, 3, 4, 1, 2)).reshape(n_tok, PEER_SEL)
    return to_tok(idx), to_tok(gate)


ROWS_PER_EXPERT = 4
PAIR_CHUNK = 32
CHUNK_ROWS = PAIR_CHUNK * ROWS_PER_EXPERT
SMEM_GROUP = 8
N_CHUNKS = PEER_SEL // PAIR_CHUNK
CHUNK_SHIFT = N_CHUNKS.bit_length() - 1
assert 1 << CHUNK_SHIFT == N_CHUNKS
_GROUP_SLOT = tuple(
    2 * (2 * (0 if (j % 2) else 1) + (1 if (j // 2) in (0, 2) else 0)) + (0 if (j // 2) < 2 else 1)
    for j in range(8))
_CHUNK_SLOT = tuple(8 * (j // 8) + _GROUP_SLOT[j % 8] for j in range(PAIR_CHUNK))


def _pack_table_kernel(t_ref, o_ref):
    rows, dm = t_ref.shape
    for c in range(ROWS_PER_EXPERT):
        lo = lax.bitcast_convert_type(t_ref[:, c * LANES:(c + 1) * LANES].astype(BF16).astype(F32), jnp.int32)
        hi = lax.bitcast_convert_type(
            t_ref[:, dm // 2 + c * LANES:dm // 2 + (c + 1) * LANES].astype(BF16).astype(F32), jnp.int32)
        o_ref[pl.ds(c, rows, stride=ROWS_PER_EXPERT), :] = lax.shift_right_logical(lo, 16) | hi


def _pack_table(tables, layer, *, rows=512):
    _, e, dm = tables.shape
    assert dm == 2 * ROWS_PER_EXPERT * LANES
    rows = min(rows, e)
    return pl.pallas_call(
        _pack_table_kernel,
        grid=(e // rows,),
        in_specs=[pl.BlockSpec((None, rows, dm), lambda i: (layer, i, 0))],
        out_specs=pl.BlockSpec((rows * ROWS_PER_EXPERT, LANES), lambda i: (i, 0)),
        out_shape=jax.ShapeDtypeStruct((e * ROWS_PER_EXPERT, LANES), jnp.int32),
        compiler_params=_cparams(("parallel",), VMEM_LIMIT),
        name="pack_table",
    )(tables)


def _unpack_words(w):
    lo = lax.bitcast_convert_type(w << 16, F32)
    hi = lax.bitcast_convert_type(w & jnp.int32(-65536), F32)
    return lo, hi


def _gather_chunk(idx_ref, tab_ref, buf_ref, c, group, slots):
    for g in range(PAIR_CHUNK // group):
        ids = idx_ref.at[pl.ds(pl.multiple_of(c * PAIR_CHUNK + g * group, group), group)]
        for i in range(group):
            e4 = pl.multiple_of(ids[i], ROWS_PER_EXPERT)
            s = slots[g * group + i] * ROWS_PER_EXPERT
            buf_ref[s:s + ROWS_PER_EXPERT, :] = tab_ref[pl.ds(e4, ROWS_PER_EXPERT), :]


def _chunk_loop(n_chunks, idx_ref, tab_ref, buf_a, buf_b, consume, init, group, slots, per_step=2, extra=None):
    last = n_chunks - 1
    gather = functools.partial(_gather_chunk, idx_ref, tab_ref, group=group, slots=slots)
    gather(buf_a, 0)
    bufs = (buf_a, buf_b)

    def body(i, carry):
        if extra is not None:
            extra(i)
        c = per_step * i
        for u in range(per_step):
            carry = consume(c + u, bufs[u % 2], carry)
            nxt = c + u + 1
            gather(bufs[(u + 1) % 2], jnp.minimum(nxt, last) if u == per_step - 1 else nxt)
        return carry

    return lax.fori_loop(0, n_chunks // per_step, body, init)


def _peer_u_kernel(idx_ref, h_ref, gate_ref, tab_ref, coef_ref, buf_a, buf_b, r_even, r_odd, d_ref, *, tt, n_tiles):
    step = pl.program_id(0)
    sub = lax.broadcasted_iota(jnp.int32, (SUBLANES, LANES), 0)
    low = sub < ROWS_PER_EXPERT
    m_a = ((sub % 4) >= 2)[None]
    m_b = ((sub % 2) == 1)[None]
    nv = PAIR_CHUNK // 2

    def pair_partials(r_ref, c, buf_ref, carry):
        t = c >> CHUNK_SHIFT
        h8 = h_ref[pl.ds(pl.multiple_of(t * SUBLANES, SUBLANES), SUBLANES), :]
        sw = pltpu.roll(h8, ROWS_PER_EXPERT, axis=0)
        ha = jnp.where(low, h8, sw)[None]
        hb = jnp.where(low, sw, h8)[None]
        lo, hi = _unpack_words(buf_ref[...])
        x = lo.reshape(nv, SUBLANES, LANES) * ha + hi.reshape(nv, SUBLANES, LANES) * hb
        x = x.reshape(nv // 2, 2, SUBLANES, LANES)
        xe, xo = x[:, 0], x[:, 1]
        a = xe + pltpu.roll(xe, 2, axis=1)
        b = xo + pltpu.roll(xo, 6, axis=1)
        m = jnp.where(m_a, a, b).reshape(nv // 4, 2, SUBLANES, LANES)
        me, mo = m[:, 0], m[:, 1]
        a2 = me + pltpu.roll(me, 1, axis=1)
        b2 = mo + pltpu.roll(mo, 7, axis=1)
        r_ref[pl.ds(pl.multiple_of(c * PAIR_CHUNK, PAIR_CHUNK), PAIR_CHUNK), :] = (
            jnp.where(m_b, a2, b2).reshape(PAIR_CHUNK, LANES))
        return carry

    def token_sum(r_prev, t):
        r_t = r_prev[pl.ds(pl.multiple_of(t * PEER_SEL, PEER_SEL), PEER_SEL), :]
        d_ref[pl.ds(t, 1), :] = jnp.sum(r_t.T, axis=0, keepdims=True)

    def loop(r_cur, **kw):
        _chunk_loop(tt * N_CHUNKS, idx_ref, tab_ref, buf_a, buf_b, functools.partial(pair_partials, r_cur), 0,
                    SMEM_GROUP, _CHUNK_SLOT, **kw)

    @pl.when(step == 0)
    def _():
        loop(r_even)

    for parity, (r_cur, r_prev) in enumerate(((r_even, r_odd), (r_odd, r_even))):
        @pl.when(jnp.logical_and(jnp.logical_and(step > 0, step < n_tiles), step % 2 == parity))
        def _(r_cur=r_cur, r_prev=r_prev):
            loop(r_cur, per_step=N_CHUNKS, extra=functools.partial(token_sum, r_prev))

    @pl.when(step == n_tiles)
    def _():
        def only_sums(t, carry):
            token_sum(r_even if n_tiles % 2 else r_odd, t)
            return carry
        lax.fori_loop(0, tt, only_sums, 0)

    @pl.when(step > 0)
    def _():
        coef_ref[...] = gate_ref[...] * _gelu(d_ref[...])


def _peer_u(idx4, h8, gates, tab, *, tt=128):
    n_tok = gates.shape[0]
    tt = min(tt, n_tok)
    n_tiles = n_tok // tt
    cur = lambda i: jnp.minimum(i, n_tiles - 1)
    done = lambda i: jnp.maximum(i - 1, 0)
    return pl.pallas_call(
        functools.partial(_peer_u_kernel, tt=tt, n_tiles=n_tiles),
        grid=(n_tiles + 1,),
        in_specs=[pl.BlockSpec((tt * PEER_SEL,), lambda i: (cur(i),), memory_space=pltpu.SMEM),
                  pl.BlockSpec((tt * SUBLANES, LANES), lambda i: (cur(i), 0)),
                  pl.BlockSpec((tt, PEER_SEL), lambda i: (done(i), 0)),
                  pl.BlockSpec(tab.shape, lambda i: (0, 0), pipeline_mode=pl.Buffered(1))],
        out_specs=pl.BlockSpec((tt, PEER_SEL), lambda i: (done(i), 0)),
        out_shape=jax.ShapeDtypeStruct((n_tok, PEER_SEL), F32),
        scratch_shapes=[pltpu.VMEM((CHUNK_ROWS, LANES), jnp.int32),
                        pltpu.VMEM((CHUNK_ROWS, LANES), jnp.int32),
                        pltpu.VMEM((tt * PEER_SEL, LANES), F32),
                        pltpu.VMEM((tt * PEER_SEL, LANES), F32),
                        pltpu.VMEM((tt, PEER_SEL), F32)],
        compiler_params=_cparams(("arbitrary",), PEER_U_VMEM_LIMIT),
        name="peer_u",
    )(idx4, h8, gates, tab)


V_TOKENS_PER_STEP = 16


def _peer_v_kernel(idx_ref, cx_ref, tab_ref, o_ref, *scratch, tt):
    bufs, parts_ref = (scratch[:N_CHUNKS], scratch[N_CHUNKS:2 * N_CHUNKS]), scratch[2 * N_CHUNKS]
    n_chunks = tt * N_CHUNKS
    row = lax.broadcasted_iota(jnp.int32, (SUBLANES, CHUNK_ROWS), 0)
    col = lax.broadcasted_iota(jnp.int32, (SUBLANES, CHUNK_ROWS), 1)
    quarter = (col % ROWS_PER_EXPERT) == (row % ROWS_PER_EXPERT)
    m_lo = jnp.logical_and(row < ROWS_PER_EXPERT, quarter)
    m_hi = jnp.logical_and(row >= ROWS_PER_EXPERT, quarter)
    gather = functools.partial(_gather_chunk, idx_ref, tab_ref, group=SMEM_GROUP, slots=tuple(range(PAIR_CHUNK)))

    def chunk_sum(t, parity, part, buf_ref):
        cx = jnp.broadcast_to(cx_ref[t, pl.ds(part, 1), :], (SUBLANES, CHUNK_ROWS))
        a = jnp.concatenate([jnp.where(m_lo, cx, 0.0), jnp.where(m_hi, cx, 0.0)], axis=1).astype(BF16)
        lo, hi = _unpack_words(buf_ref[...])
        b = jnp.concatenate([lo.astype(BF16), hi.astype(BF16)], axis=0)
        parts_ref[parity, part] = jnp.dot(a, b, preferred_element_type=F32)

    parts_ref[...] = jnp.zeros_like(parts_ref)
    for j in range(N_CHUNKS):
        gather(bufs[0][j], j)

    def body(i, carry):
        for s in range(V_TOKENS_PER_STEP):
            half = s % 2
            t = V_TOKENS_PER_STEP * i + s
            prev = jnp.maximum(t - 1, 0)
            o_ref[prev] = _tree(jnp.add, [parts_ref[1 - half, j] for j in range(N_CHUNKS)])
            for j in range(N_CHUNKS):
                chunk_sum(t, half, j, bufs[half][j])
            for j in range(N_CHUNKS):
                gather(bufs[1 - half][j], jnp.minimum((t + 1) * N_CHUNKS + j, n_chunks - 1))
        return carry

    lax.fori_loop(0, tt // V_TOKENS_PER_STEP, body, 0)
    o_ref[tt - 1] = _tree(jnp.add, [parts_ref[1, j] for j in range(N_CHUNKS)])


def _peer_v(idx4, coef, tab, *, tt=128):
    n_tok = idx4.shape[0] // PEER_SEL
    tt = min(tt, n_tok)
    assert V_TOKENS_PER_STEP % 2 == 0 and tt % V_TOKENS_PER_STEP == 0
    cx = jnp.repeat(coef, ROWS_PER_EXPERT, axis=1).reshape(n_tok, N_CHUNKS, CHUNK_ROWS)
    smem = pl.BlockSpec((tt * PEER_SEL,), lambda i: (i,), memory_space=pltpu.SMEM)
    return pl.pallas_call(
        functools.partial(_peer_v_kernel, tt=tt),
        grid=(n_tok // tt,),
        in_specs=[smem, pl.BlockSpec((tt, N_CHUNKS, CHUNK_ROWS), lambda i: (i, 0, 0)),
                  pl.BlockSpec(tab.shape, lambda i: (0, 0), pipeline_mode=pl.Buffered(1))],
        out_specs=pl.BlockSpec((tt, SUBLANES, LANES), lambda i: (i, 0, 0)),
        out_shape=jax.ShapeDtypeStruct((n_tok, SUBLANES, LANES), F32),
        scratch_shapes=[pltpu.VMEM((CHUNK_ROWS, LANES), jnp.int32)] * (2 * N_CHUNKS)
                       + [pltpu.VMEM((2, N_CHUNKS, SUBLANES, LANES), F32)],
        compiler_params=_cparams(("arbitrary",), VMEM_LIMIT),
        name="peer_v",
    )(idx4, cx, tab)


def _peer(h8, qp, sub_keys, tab_u, tab_v):
    n_tok = qp.shape[1]
    half = sub_keys.shape[-1]
    z = jnp.zeros_like(sub_keys[0])
    keys_pad = jnp.stack([jnp.concatenate([sub_keys[0], z], axis=-1),
                          jnp.concatenate([z, sub_keys[1]], axis=-1)]).astype(BF16)
    assert keys_pad.shape[-1] == 2 * half == qp.shape[-1]
    idx4, gates = _peer_topk(qp, keys_pad)
    idx4 = idx4.reshape(n_tok * PEER_SEL)
    coef = _peer_u(idx4, h8, gates, tab_u)
    out = _peer_v(idx4, coef, tab_v)
    return out.reshape(n_tok * SUBLANES, LANES)


def _final_kernel(x_ref, p_ref, g_ref, o_ref):
    o_ref[...] = _rms(x_ref[...] + _rows_from_chunks(p_ref, x_ref.shape[0]), g_ref[...])


def _final_norm(x, p, g, *, tr=1024):
    rows, dm = x.shape
    tr = min(tr, rows)
    blk = pl.BlockSpec((tr, dm), lambda i: (i, 0))
    return pl.pallas_call(
        _final_kernel,
        grid=(rows // tr,),
        in_specs=[blk, pl.BlockSpec((tr * SUBLANES, LANES), lambda i: (i, 0)),
                  pl.BlockSpec((1, dm), lambda i: (0, 0))],
        out_specs=blk,
        out_shape=jax.ShapeDtypeStruct((rows, dm), F32),
        compiler_params=_cparams(("parallel",), VMEM_LIMIT),
        name="final_norm",
    )(x, p, g.reshape(1, dm))


def kernel(x, w_in, w_out, rel_bias, g_attn, g_ssm, norm_mix, norm_ffn, lam_re, lam_im, log_step, b_re, b_im, c_re, c_im, d_skip, w_glu, w_query, sub_keys, expert_u, expert_v, norm_final):
    bsz, seq, dm = x.shape
    depth = w_in.shape[0]
    prev = None
    biases = [_attn_bias_tables(rel_bias, d) for _, d in DILATED_PATTERNS]
    for l in range(depth):
        x, q, k, v, u_tm = _in_proj(x, prev, norm_mix[l], w_in[l].astype(BF16))
        attn = _attention(q, k, v, biases)
        a, bw, cw = _ssm_params(lam_re[l], lam_im[l], log_step[l], b_re[l], b_im[l], c_re[l], c_im[l])
        y = _ssm_scan(u_tm, a, bw, cw, bsz=bsz)
        wo = w_out[l].astype(BF16)
        z = _ssm_post(y, u_tm, d_skip[l], w_glu[l].astype(BF16), g_ssm[l], wo[ATTN_WIDTH:], bsz=bsz)
        x, h, qp = _mix_out(x, z, attn,
                            g_attn[l], wo[:ATTN_WIDTH], norm_ffn[l], w_query[l].astype(BF16))
        prev = _peer(h, qp, sub_keys[l], _pack_table(expert_u, l), _pack_table(expert_v, l))
    out = _final_norm(x.reshape(bsz * seq, dm), prev, norm_final)
    return out.reshape(bsz, seq, dm)
```

```python
import functools
import math

import numpy as np
import jax
import jax.numpy as jnp
from jax import lax
from jax.experimental import pallas as pl
from jax.experimental.pallas import tpu as pltpu

F32 = jnp.float32
BF16 = jnp.bfloat16

EPS = 1e-6
NEG_INF = -1e30
HEAD_DIM = 64
ATTN_WIDTH = 512
SSM_WIDTH = 512
SSM_GROUP = 16
SSM_STATE = 64
DILATED_PATTERNS = ((128, 1), (512, 4), (2048, 16))
REL_BUCKETS = 32
REL_MAX_DISTANCE = 1024
PEER_HEADS = 8
PEER_KEYS = 128
PEER_TOPK = 16
PEER_SEL = PEER_HEADS * PEER_TOPK

LANES = 128
SUBLANES = 8
QBLK = 128
KWIN = 256
BAND = 64
VMEM_LIMIT = 52 * 1024 * 1024
PEER_U_VMEM_LIMIT = 58 * 1024 * 1024


def _cparams(sem, vmem=None):
    return pltpu.CompilerParams(dimension_semantics=sem, vmem_limit_bytes=vmem)


def _rms(x, g):
    return x * lax.rsqrt(jnp.mean(x * x, axis=-1, keepdims=True) + EPS) * g


def _gelu(x):
    return 0.5 * x * (1.0 + lax.erf(x * (1.0 / math.sqrt(2.0))))


def _rows_from_chunks(p_ref, n_rows):
    return jnp.concatenate([p_ref[pl.ds(c, n_rows, stride=SUBLANES), :] for c in range(SUBLANES)], axis=-1)


def _in_proj_kernel(*refs, has_prev):
    if has_prev:
        x_ref, p_ref, g_ref, w_ref, xo_ref, q_ref, k_ref, v_ref, u_ref = refs
        x = x_ref[0] + _rows_from_chunks(p_ref, x_ref.shape[1])
    else:
        x_ref, g_ref, w_ref, xo_ref, q_ref, k_ref, v_ref, u_ref = refs
        x = x_ref[0]
    xo_ref[0] = x
    h = _rms(x, g_ref[...]).astype(BF16)
    proj = jnp.dot(h, w_ref[...], preferred_element_type=F32)
    a = ATTN_WIDTH
    q_ref[0] = proj[:, :a] * (HEAD_DIM ** -0.5)
    k_ref[0] = proj[:, a:2 * a]
    v_ref[0] = proj[:, 2 * a:3 * a]
    b, bsz = pl.program_id(1), pl.num_programs(1)
    for c in range(SSM_WIDTH // LANES):
        u_ref[c, pl.ds(b, proj.shape[0], stride=bsz), :] = proj[:, 3 * a + c * LANES:3 * a + (c + 1) * LANES]


def _in_proj(x, prev, g, w_bf16, *, ts=256):
    bsz, seq, dm = x.shape
    ts = min(ts, seq)
    row = pl.BlockSpec((1, ts, dm), lambda i, b: (b, i, 0))
    qkv = pl.BlockSpec((1, ts, ATTN_WIDTH), lambda i, b: (b, i, 0))
    ns = seq // ts
    chunks = pl.BlockSpec((ts * SUBLANES, LANES), lambda i, b: (b * ns + i, 0))
    ins = [x] + ([prev] if prev is not None else []) + [g.reshape(1, dm), w_bf16]
    in_specs = [row] + ([chunks] if prev is not None else []) + [
        pl.BlockSpec((1, dm), lambda i, b: (0, 0)),
        pl.BlockSpec(w_bf16.shape, lambda i, b: (0, 0)),
    ]
    slabs = SSM_WIDTH // LANES
    return pl.pallas_call(
        functools.partial(_in_proj_kernel, has_prev=prev is not None),
        grid=(seq // ts, bsz),
        in_specs=in_specs,
        out_specs=[row, qkv, qkv, qkv, pl.BlockSpec((slabs, ts * bsz, LANES), lambda i, b: (0, i, 0))],
        out_shape=[
            jax.ShapeDtypeStruct((bsz, seq, dm), F32),
            jax.ShapeDtypeStruct((bsz, seq, ATTN_WIDTH), F32),
            jax.ShapeDtypeStruct((bsz, seq, ATTN_WIDTH), F32),
            jax.ShapeDtypeStruct((bsz, seq, ATTN_WIDTH), F32),
            jax.ShapeDtypeStruct((slabs, seq * bsz, LANES), F32),
        ],
        compiler_params=_cparams(("parallel", "arbitrary"), VMEM_LIMIT),
        name="in_proj",
    )(*ins)


def _t5_buckets(rel):
    half = REL_BUCKETS // 2
    max_exact = half // 2
    n = np.abs(rel)
    large = max_exact + (np.log(np.maximum(n, 1) / max_exact)
                         / np.log(REL_MAX_DISTANCE / max_exact) * (half - max_exact)).astype(np.int32)
    large = np.minimum(large, half - 1)
    return (np.where(rel > 0, half, 0) + np.where(n < max_exact, n, large)).astype(np.int32)


def _attn_bias_tables(rel_bias, dilation):
    ql = np.arange(QBLK)[:, None]
    kl = np.arange(KWIN)[None, :]
    delta = np.stack([kl + off - ql for off in (0, -BAND, -2 * BAND)])
    buckets = np.where(np.abs(delta) <= BAND, _t5_buckets(delta * dilation), -1)
    rb = rel_bias.astype(F32).T
    bk = jnp.asarray(buckets, jnp.int32)[None]
    tab = jnp.full((rb.shape[0],) + buckets.shape, NEG_INF, F32)
    for b in range(REL_BUCKETS):
        tab = jnp.where(bk == b, rb[:, b][:, None, None, None], tab)
    return tab


BLOCKS_PER_STEP = 4


def _attn_kernel(q_ref, k_ref, v_ref, *rest, seq):
    bias_refs, (o_ref, acc_ref, m_ref, z_ref) = rest[:len(DILATED_PATTERNS)], rest[len(DILATED_PATTERNS):]
    lane = lax.broadcasted_iota(jnp.int32, (QBLK, LANES), 1)
    is_h0 = lane < HEAD_DIM
    dn = (((1,), (1,)), ((), ()))
    nsteps = seq // QBLK

    def rows(start, size, d):
        return pl.ds(start, size) if d == 1 else pl.ds(start, size, stride=d)

    def block(n, d, bias_ref, first, last):
        length = seq // d
        nblk = length // QBLK
        r, i = n >> (nblk.bit_length() - 1), n & (nblk - 1)
        s = i * QBLK
        ks = jnp.clip(s - BAND, 0, length - KWIN)
        var = jnp.where(i == 0, 0, jnp.where(i == nblk - 1, 2, 1))
        q_rows = rows(r + d * s, QBLK, d)
        k_rows = rows(r + d * ks, KWIN, d)
        qb = q_ref[q_rows, :].astype(BF16)
        kb = k_ref[k_rows, :].astype(BF16)
        vb = v_ref[k_rows, :].astype(BF16)
        outs, ms, zs = [], [], []
        for h in range(2):
            keep = is_h0 if h == 0 else jnp.logical_not(is_h0)
            qh = jnp.where(keep, qb, jnp.zeros_like(qb))
            logits = lax.dot_general(qh, kb, dn, preferred_element_type=F32) + bias_ref[h, var]
            m = jnp.max(logits, axis=-1, keepdims=True)
            p = jnp.exp(logits - m)
            outs.append(jnp.dot(p.astype(BF16), vb, preferred_element_type=F32))
            ms.append(jnp.broadcast_to(m, (QBLK, LANES)))
            zs.append(jnp.broadcast_to(jnp.sum(p, axis=-1, keepdims=True), (QBLK, LANES)))
        o = jnp.where(is_h0, outs[0], outs[1])
        m = jnp.where(is_h0, ms[0], ms[1])
        z = jnp.where(is_h0, zs[0], zs[1])
        if not first:
            m_old = m_ref[q_rows, :]
            m_new = jnp.maximum(m_old, m)
            a, b = jnp.exp(m_old - m_new), jnp.exp(m - m_new)
            o = acc_ref[q_rows, :] * a + o * b
            z = z_ref[q_rows, :] * a + z * b
            m = m_new
        if last:
            o_ref[q_rows, :] = o / z
        else:
            acc_ref[q_rows, :] = o
            m_ref[q_rows, :] = m
            z_ref[q_rows, :] = z

    for p, ((_, d), bias_ref) in enumerate(zip(DILATED_PATTERNS, bias_refs)):
        def step(g, carry, d=d, bias_ref=bias_ref, p=p):
            for j in range(BLOCKS_PER_STEP):
                block(g * BLOCKS_PER_STEP + j, d, bias_ref, p == 0, p == len(DILATED_PATTERNS) - 1)
            return carry
        lax.fori_loop(0, nsteps // BLOCKS_PER_STEP, step, 0)


def _attention(q, k, v, biases):
    bsz, seq, width = q.shape
    for _, d in DILATED_PATTERNS:
        length = seq // d
        assert length >= KWIN and length % QBLK == 0 and (length // QBLK) & (length // QBLK - 1) == 0
    assert (seq // QBLK) % BLOCKS_PER_STEP == 0
    blk = pl.BlockSpec((None, seq, LANES), lambda b, c: (b, 0, c))
    bias_spec = pl.BlockSpec((2, 3, QBLK, KWIN), lambda b, c: (c, 0, 0, 0))
    return pl.pallas_call(
        functools.partial(_attn_kernel, seq=seq),
        grid=(bsz, width // LANES),
        in_specs=[blk, blk, blk] + [bias_spec] * len(biases),
        out_specs=blk,
        out_shape=jax.ShapeDtypeStruct((bsz, seq, width), F32),
        scratch_shapes=[pltpu.VMEM((seq, LANES), F32)] * 3,
        compiler_params=_cparams(("parallel", "arbitrary"), VMEM_LIMIT),
        name="attention",
    )(q, k, v, *biases)


SSM_LANE_GROUPS = SSM_WIDTH // LANES
SSM_GB_STATES = (LANES // SSM_GROUP) * SSM_STATE


def _ssm_params(lam_re, lam_im, log_step, b_re, b_im, c_re, c_im):
    f = lambda t: t.astype(F32)
    lr, li = f(lam_re), f(lam_im)
    step = jnp.exp(f(log_step))[..., None]
    mag = jnp.exp(lr * step)
    ar, ai = mag * jnp.cos(li * step), mag * jnp.sin(li * step)
    nr, ni = ar - 1.0, ai
    den = lr * lr + li * li
    cr, ci = (nr * lr + ni * li) / den, (ni * lr - nr * li) / den
    br, bi = f(b_re), f(b_im)
    bbr = cr[..., None] * br - ci[..., None] * bi
    bbi = cr[..., None] * bi + ci[..., None] * br
    gpb = LANES // SSM_GROUP
    eye = jnp.eye(gpb, dtype=F32)

    def in_map(t):
        t = t.reshape(2, SSM_LANE_GROUPS, gpb, SSM_STATE, SSM_GROUP)
        return jnp.einsum('dbgpc,gh->dbgchp', t, eye).reshape(2, SSM_LANE_GROUPS, LANES, SSM_GB_STATES)

    def out_map(t):
        t = t.reshape(2, SSM_LANE_GROUPS, gpb, SSM_GROUP, SSM_STATE)
        return jnp.einsum('dbgcp,gh->dbgphc', t, eye).reshape(2, SSM_LANE_GROUPS, SSM_GB_STATES, LANES)

    bw = jnp.concatenate([in_map(bbr), in_map(bbi)], axis=-1).astype(BF16)
    cw = jnp.concatenate([out_map(f(c_re)), -out_map(f(c_im))], axis=-2).astype(BF16)
    a = jnp.stack([ar.reshape(2, SSM_LANE_GROUPS, SSM_GB_STATES),
                   ai.reshape(2, SSM_LANE_GROUPS, SSM_GB_STATES)], axis=2)
    return a, bw, cw


def _ssm_kernel(u_ref, a_ref, bw_ref, cw_ref, y_ref, st_ref, bu_ref, *, ts, bsz):
    d = pl.program_id(0)
    ns = SSM_GB_STATES

    @pl.when(pl.program_id(1) == 0)
    def _():
        st_ref[...] = jnp.zeros_like(st_ref)

    for gb in range(SSM_LANE_GROUPS):
        ub = u_ref[gb].astype(BF16)
        bu_ref[...] = jnp.dot(ub, bw_ref[0, gb], preferred_element_type=F32)
        ar = jnp.broadcast_to(a_ref[0, gb, 0:1, :], (bsz, ns))
        ai = jnp.broadcast_to(a_ref[0, gb, 1:2, :], (bsz, ns))

        def step(j, carry, ar=ar, ai=ai):
            xr, xi = carry
            tl = jnp.where(d == 0, j, ts - 1 - j)
            r = pl.multiple_of(tl * bsz, bsz)
            nr = ar * xr - ai * xi + bu_ref[pl.ds(r, bsz), :ns]
            ni = ar * xi + ai * xr + bu_ref[pl.ds(r, bsz), ns:]
            bu_ref[pl.ds(r, bsz), :ns] = nr
            bu_ref[pl.ds(r, bsz), ns:] = ni
            return nr, ni

        xr, xi = lax.fori_loop(0, ts, step, (st_ref[gb, :, :ns], st_ref[gb, :, ns:]))
        st_ref[gb, :, :ns] = xr
        st_ref[gb, :, ns:] = xi
        y_ref[0, :, gb * LANES:(gb + 1) * LANES] = jnp.dot(
            bu_ref[...].astype(BF16), cw_ref[0, gb], preferred_element_type=F32)


def _ssm_scan(u_tm, a, bw, cw, *, bsz, ts=64):
    slabs, rows, _ = u_tm.shape
    width = slabs * LANES
    seq = rows // bsz
    ts = min(ts, seq)
    nt = seq // ts
    tblk = lambda d, i: jnp.where(d == 0, i, nt - 1 - i)
    return pl.pallas_call(
        functools.partial(_ssm_kernel, ts=ts, bsz=bsz),
        grid=(2, nt),
        in_specs=[
            pl.BlockSpec((slabs, ts * bsz, LANES), lambda d, i: (0, tblk(d, i), 0)),
            pl.BlockSpec((1,) + a.shape[1:], lambda d, i: (d, 0, 0, 0)),
            pl.BlockSpec((1,) + bw.shape[1:], lambda d, i: (d, 0, 0, 0)),
            pl.BlockSpec((1,) + cw.shape[1:], lambda d, i: (d, 0, 0, 0)),
        ],
        out_specs=pl.BlockSpec((1, ts * bsz, width), lambda d, i: (d, tblk(d, i), 0)),
        out_shape=jax.ShapeDtypeStruct((2, rows, width), F32),
        scratch_shapes=[pltpu.VMEM((SSM_LANE_GROUPS, bsz, 2 * SSM_GB_STATES), F32),
                        pltpu.VMEM((ts * bsz, 2 * SSM_GB_STATES), F32)],
        compiler_params=_cparams(("arbitrary", "arbitrary"), VMEM_LIMIT),
        name="ssm_scan",
    )(u_tm, a, bw, cw)


def _ssm_post_kernel(y_ref, u_ref, d_ref, wg_ref, g_ref, wo_ref, z_ref, slab_ref, *, bsz):
    u = jnp.concatenate([u_ref[c] for c in range(u_ref.shape[0])], axis=-1)
    y = _gelu(y_ref[0] + y_ref[1] + d_ref[...] * u).astype(BF16)
    ab = jnp.dot(y, wg_ref[...], preferred_element_type=F32)
    ssm = ab[:, :SSM_WIDTH] * jax.nn.sigmoid(ab[:, SSM_WIDTH:])
    n = _rms(ssm, g_ref[...]).astype(BF16)
    z = jnp.dot(n, wo_ref[...], preferred_element_type=F32)
    nt, dm = z.shape[0] // bsz, z.shape[1]
    for c in range(dm // LANES):
        slab_ref[c] = z[:, c * LANES:(c + 1) * LANES]
    for b in range(bsz):
        for c in range(dm // LANES):
            z_ref[:, b * dm + c * LANES:b * dm + (c + 1) * LANES] = slab_ref[c, pl.ds(b, nt, stride=bsz), :]


def _ssm_post(y, u_tm, d_skip, w_glu_bf16, g_ssm, w_out_ssm_bf16, *, bsz, tr=512):
    slabs, rows, _ = u_tm.shape
    width = slabs * LANES
    tr = min(tr, rows)
    assert tr % bsz == 0
    dm = w_out_ssm_bf16.shape[1]
    full = lambda a: pl.BlockSpec(a.shape, lambda i: (0,) * a.ndim)
    d2, g2 = d_skip.reshape(1, width), g_ssm.reshape(1, width)
    return pl.pallas_call(
        functools.partial(_ssm_post_kernel, bsz=bsz),
        grid=(rows // tr,),
        in_specs=[pl.BlockSpec((2, tr, width), lambda i: (0, i, 0)),
                  pl.BlockSpec((slabs, tr, LANES), lambda i: (0, i, 0)),
                  full(d2), full(w_glu_bf16), full(g2), full(w_out_ssm_bf16)],
        out_specs=pl.BlockSpec((tr // bsz, bsz * dm), lambda i: (i, 0)),
        out_shape=jax.ShapeDtypeStruct((rows // bsz, bsz * dm), F32),
        scratch_shapes=[pltpu.VMEM((dm // LANES, tr, LANES), F32)],
        compiler_params=_cparams(("parallel",), VMEM_LIMIT),
        name="ssm_post",
    )(y, u_tm, d2, w_glu_bf16, g2, w_out_ssm_bf16)


def _mix_out_kernel(x_ref, z_ref, a_ref, ga_ref, wo_ref, gf_ref, wq_ref, xn_ref, h_ref, q_ref):
    n = _rms(a_ref[0], ga_ref[...]).astype(BF16)
    xn = x_ref[0] + z_ref[...] + jnp.dot(n, wo_ref[...], preferred_element_type=F32)
    xn_ref[0] = xn
    h = _rms(xn, gf_ref[...])
    for c in range(SUBLANES):
        h_ref[pl.ds(c, h.shape[0], stride=SUBLANES), :] = h[:, c * LANES:(c + 1) * LANES]
    qp = jnp.dot(h.astype(BF16), wq_ref[...], preferred_element_type=F32)
    for hd in range(PEER_HEADS):
        q_ref[hd] = qp[:, hd * LANES:(hd + 1) * LANES]


def _mix_out(x, z_tm, attn, g_attn, w_out_attn_bf16, norm_ffn, w_query, *, ts=256):
    bsz, seq, dm = x.shape
    ts = min(ts, seq)
    ns = seq // ts
    row = pl.BlockSpec((1, ts, dm), lambda b, i: (b, i, 0))
    half = pl.BlockSpec((1, ts, ATTN_WIDTH), lambda b, i: (b, i, 0))
    full = lambda a: pl.BlockSpec(a.shape, lambda b, i: (0,) * a.ndim)
    ga, gf = g_attn.reshape(1, ATTN_WIDTH), norm_ffn.reshape(1, dm)
    qdim = w_query.shape[1] // PEER_HEADS
    return pl.pallas_call(
        _mix_out_kernel,
        grid=(bsz, ns),
        in_specs=[row, pl.BlockSpec((ts, dm), lambda b, i: (i, b)), half,
                  full(ga), full(w_out_attn_bf16), full(gf), full(w_query)],
        out_specs=[row, pl.BlockSpec((ts * SUBLANES, LANES), lambda b, i: (b * ns + i, 0)),
                   pl.BlockSpec((PEER_HEADS, ts, qdim), lambda b, i: (0, b * ns + i, 0))],
        out_shape=[jax.ShapeDtypeStruct((bsz, seq, dm), F32),
                   jax.ShapeDtypeStruct((bsz * seq * SUBLANES, LANES), F32),

                   jax.ShapeDtypeStruct((PEER_HEADS, bsz * seq, qdim), F32)],
        compiler_params=_cparams(("parallel", "arbitrary"), VMEM_LIMIT),
        name="mix_out",
    )(x, z_tm, attn, ga, w_out_attn_bf16, gf, w_query)


TOPK_TOKENS = SUBLANES * LANES
KEY_PITCH = PEER_KEYS + 4
_CANDIDATES = tuple((a, b) for a in range(PEER_TOPK) for b in range(PEER_TOPK) if (a + 1) * (b + 1) <= PEER_TOPK)


def _tree(op, xs):
    xs = list(xs)
    while len(xs) > 1:
        xs = [op(xs[i], xs[i + 1]) if i + 1 < len(xs) else xs[i] for i in range(0, len(xs), 2)]
    return xs[0]


def _extract16(problems):
    ninf = jnp.float32(-jnp.inf)

    def better(a, b):
        gt = b[0] > a[0]
        return tuple(jnp.where(gt, y, x) for x, y in zip(a, b))

    def step(r, carry):
        for p in problems:
            s_ref, order = p["s"], p["order"]
            assert list(order) == sorted(order)
            n = len(order)
            rows = [(s_ref[k], order[k]) + ((p["pay"][k],) if p.get("pay") is not None else ())
                    for k in range(n)]
            win = _tree(better, rows)
            m, am = win[0], win[1]
            for k in range(n):
                s_ref[k] = jnp.where(am == order[k], ninf, s_ref[k])
            p["vals"][r] = m
            p["picks"][r] = win[2] if len(win) > 2 else am.astype(jnp.int32)
        return carry

    lax.fori_loop(0, PEER_TOPK, step, 0)


def _peer_topk_kernel(q_ref, k_ref, idx_ref, gate_ref,
                      slab_ref, s1_ref, s2_ref, t1_ref, i1_ref, t2_ref, i2_ref, cand_ref, pay_ref, ts_ref, ex_ref):
    dn = (((1,), (1,)), ((), ()))
    keys = tuple(range(PEER_KEYS))

    def head(h, carry):
        for w, s_ref in ((0, s1_ref), (1, s2_ref)):
            for j in range(SUBLANES):
                slab_ref[j * KEY_PITCH:j * KEY_PITCH + PEER_KEYS, :] = lax.dot_general(
                    k_ref[w, h], q_ref[h, j * LANES:(j + 1) * LANES, :].astype(BF16), dn,
                    preferred_element_type=F32)
            for k in range(PEER_KEYS):
                s_ref[k] = slab_ref[pl.ds(k, SUBLANES, stride=KEY_PITCH), :]
        _extract16([dict(s=s1_ref, order=keys, vals=t1_ref, picks=i1_ref),
                    dict(s=s2_ref, order=keys, vals=t2_ref, picks=i2_ref)])
        for c, (a, b) in enumerate(_CANDIDATES):
            cand_ref[c] = t1_ref[a] + t2_ref[b]
            pay_ref[c] = i1_ref[a] * PEER_KEYS + i2_ref[b]
        _extract16([dict(s=cand_ref, order=tuple(a * PEER_TOPK + b for a, b in _CANDIDATES), pay=pay_ref,
                         vals=ts_ref, picks=ex_ref)])
        top_s = ts_ref[...]
        e = jnp.exp(top_s - jnp.max(top_s, axis=0, keepdims=True))
        gate_ref[0, h] = e / jnp.sum(e, axis=0, keepdims=True)
        idx_ref[0, h] = ex_ref[...] * ROWS_PER_EXPERT
        return carry

    lax.fori_loop(0, PEER_HEADS, head, 0)


def _peer_topk(qp, keys_pad):
    n_tok = qp.shape[1]
    tt = TOPK_TOKENS
    assert n_tok % tt == 0
    shp = (n_tok // tt, PEER_HEADS, PEER_TOPK, SUBLANES, LANES)
    out = pl.BlockSpec((1,) + shp[1:], lambda i: (i, 0, 0, 0, 0))
    vregs = lambda n, dt: pltpu.VMEM((n, SUBLANES, LANES), dt)
    idx, gate = pl.pallas_call(
        _peer_topk_kernel,
        grid=(n_tok // tt,),
        in_specs=[pl.BlockSpec((PEER_HEADS, tt, qp.shape[2]), lambda i: (0, i, 0)),
                  pl.BlockSpec(keys_pad.shape, lambda i: (0, 0, 0, 0))],
        out_specs=[out, out],
        out_shape=[jax.ShapeDtypeStruct(shp, jnp.int32), jax.ShapeDtypeStruct(shp, F32)],
        scratch_shapes=[pltpu.VMEM((SUBLANES * KEY_PITCH, LANES), F32),
                        vregs(PEER_KEYS, F32), vregs(PEER_KEYS, F32),
                        vregs(PEER_TOPK, F32), vregs(PEER_TOPK, jnp.int32),
                        vregs(PEER_TOPK, F32), vregs(PEER_TOPK, jnp.int32),
                        vregs(len(_CANDIDATES), F32), vregs(len(_CANDIDATES), jnp.int32),
                        vregs(PEER_TOPK, F32), vregs(PEER_TOPK, jnp.int32)],
        compiler_params=_cparams(("parallel",), VMEM_LIMIT),
        name="peer_topk",
    )(qp, keys_pad)
    to_tok = lambda a: jnp.transpose(a, (0, 3, 4, 1, 2)).reshape(n_tok, PEER_SEL)
    return to_tok(idx), to_tok(gate)


ROWS_PER_EXPERT = 4
PAIR_CHUNK = 32
CHUNK_ROWS = PAIR_CHUNK * ROWS_PER_EXPERT
SMEM_GROUP = 8
N_CHUNKS = PEER_SEL // PAIR_CHUNK
CHUNK_SHIFT = N_CHUNKS.bit_length() - 1
assert 1 << CHUNK_SHIFT == N_CHUNKS
_GROUP_SLOT = tuple(
    2 * (2 * (0 if (j % 2) else 1) + (1 if (j // 2) in (0, 2) else 0)) + (0 if (j // 2) < 2 else 1)
    for j in range(8))
_CHUNK_SLOT = tuple(8 * (j // 8) + _GROUP_SLOT[j % 8] for j in range(PAIR_CHUNK))


def _pack_table_kernel(t_ref, o_ref):
    rows, dm = t_ref.shape
    for c in range(ROWS_PER_EXPERT):
        lo = lax.bitcast_convert_type(t_ref[:, c * LANES:(c + 1) * LANES].astype(BF16).astype(F32), jnp.int32)
        hi = lax.bitcast_convert_type(
            t_ref[:, dm // 2 + c * LANES:dm // 2 + (c + 1) * LANES].astype(BF16).astype(F32), jnp.int32)
        o_ref[pl.ds(c, rows, stride=ROWS_PER_EXPERT), :] = lax.shift_right_logical(lo, 16) | hi


def _pack_table(tables, layer, *, rows=512):
    _, e, dm = tables.shape
    assert dm == 2 * ROWS_PER_EXPERT * LANES
    rows = min(rows, e)
    return pl.pallas_call(
        _pack_table_kernel,
        grid=(e // rows,),
        in_specs=[pl.BlockSpec((None, rows, dm), lambda i: (layer, i, 0))],
        out_specs=pl.BlockSpec((rows * ROWS_PER_EXPERT, LANES), lambda i: (i, 0)),
        out_shape=jax.ShapeDtypeStruct((e * ROWS_PER_EXPERT, LANES), jnp.int32),
        compiler_params=_cparams(("parallel",), VMEM_LIMIT),
        name="pack_table",
    )(tables)


def _unpack_words(w):
    lo = lax.bitcast_convert_type(w << 16, F32)
    hi = lax.bitcast_convert_type(w & jnp.int32(-65536), F32)
    return lo, hi


def _gather_chunk(idx_ref, tab_ref, buf_ref, c, group, slots):
    for g in range(PAIR_CHUNK // group):
        ids = idx_ref.at[pl.ds(pl.multiple_of(c * PAIR_CHUNK + g * group, group), group)]
        for i in range(group):
            e4 = pl.multiple_of(ids[i], ROWS_PER_EXPERT)
            s = slots[g * group + i] * ROWS_PER_EXPERT
            buf_ref[s:s + ROWS_PER_EXPERT, :] = tab_ref[pl.ds(e4, ROWS_PER_EXPERT), :]


def _chunk_loop(n_chunks, idx_ref, tab_ref, buf_a, buf_b, consume, init, group, slots, per_step=2, extra=None):
    last = n_chunks - 1
    gather = functools.partial(_gather_chunk, idx_ref, tab_ref, group=group, slots=slots)
    gather(buf_a, 0)
    bufs = (buf_a, buf_b)

    def body(i, carry):
        if extra is not None:
            extra(i)
        c = per_step * i
        for u in range(per_step):
            carry = consume(c + u, bufs[u % 2], carry)
            nxt = c + u + 1
            gather(bufs[(u + 1) % 2], jnp.minimum(nxt, last) if u == per_step - 1 else nxt)
        return carry

    return lax.fori_loop(0, n_chunks // per_step, body, init)


U_TOKENS_PER_STEP = 16


def _peer_u_kernel(idx_ref, h_ref, gate_ref, tab_ref, coef_ref, *scratch, tt):
    bufs, m_ref, d_ref = (scratch[:N_CHUNKS], scratch[N_CHUNKS:2 * N_CHUNKS]), scratch[-2], scratch[-1]
    n_chunks = tt * N_CHUNKS
    row = lax.broadcasted_iota(jnp.int32, (SUBLANES, CHUNK_ROWS), 0)
    col = lax.broadcasted_iota(jnp.int32, (SUBLANES, CHUNK_ROWS), 1)
    quarter = (col % ROWS_PER_EXPERT) == (row % ROWS_PER_EXPERT)
    m_lo = jnp.logical_and(row < ROWS_PER_EXPERT, quarter)
    m_hi = jnp.logical_and(row >= ROWS_PER_EXPERT, quarter)
    pr = lax.broadcasted_iota(jnp.int32, (CHUNK_ROWS, LANES), 0) // ROWS_PER_EXPERT
    pc = lax.broadcasted_iota(jnp.int32, (CHUNK_ROWS, LANES), 1)
    place = [(pr + part * PAIR_CHUNK == pc).astype(BF16) for part in range(N_CHUNKS)]
    dn = (((1,), (1,)), ((), ()))
    gather = functools.partial(_gather_chunk, idx_ref, tab_ref, group=SMEM_GROUP, slots=tuple(range(PAIR_CHUNK)))

    def chunk_dots(t, part, buf_ref):
        rows = pl.ds(pl.multiple_of(t * SUBLANES, SUBLANES), SUBLANES)
        x = h_ref[rows, :].astype(BF16)
        lo, hi = _unpack_words(buf_ref[...])
        b = jnp.concatenate([lo.astype(BF16), hi.astype(BF16)], axis=0)
        out = lax.dot_general(x, b, dn, preferred_element_type=F32)
        m_ref[part, rows, :] = (jnp.where(m_lo, out[:, :CHUNK_ROWS], 0.0)
                                + jnp.where(m_hi, out[:, CHUNK_ROWS:], 0.0))

    for j in range(N_CHUNKS):
        gather(bufs[0][j], j)

    def body(i, carry):
        for s in range(U_TOKENS_PER_STEP):
            half = s % 2
            t = U_TOKENS_PER_STEP * i + s
            for j in range(N_CHUNKS):
                chunk_dots(t, j, bufs[half][j])
            for j in range(N_CHUNKS):
                gather(bufs[1 - half][j], jnp.minimum((t + 1) * N_CHUNKS + j, n_chunks - 1))
        return carry

    lax.fori_loop(0, tt // U_TOKENS_PER_STEP, body, 0)

    acc = None
    for part in range(N_CHUNKS):
        m = m_ref[part]
        m_top = m.astype(BF16)
        m_rest = (m - m_top.astype(F32)).astype(BF16)
        s = (jnp.dot(m_top, place[part], preferred_element_type=F32)
             + jnp.dot(m_rest, place[part], preferred_element_type=F32))
        acc = s if acc is None else acc + s
    m_ref[0] = acc

    def token_rows(t, carry):
        d_ref[pl.ds(t, 1), :] = jnp.sum(m_ref[0, pl.ds(pl.multiple_of(t * SUBLANES, SUBLANES), SUBLANES), :],
                                        axis=0, keepdims=True)
        return carry

    lax.fori_loop(0, tt, token_rows, 0)
    coef_ref[...] = gate_ref[...] * _gelu(d_ref[...])


def _peer_u(idx4, h8, gates, tab, *, tt=128):
    n_tok = gates.shape[0]
    tt = min(tt, n_tok)
    assert U_TOKENS_PER_STEP % 2 == 0 and tt % U_TOKENS_PER_STEP == 0
    tok2 = pl.BlockSpec((tt, PEER_SEL), lambda i: (i, 0))
    return pl.pallas_call(
        functools.partial(_peer_u_kernel, tt=tt),
        grid=(n_tok // tt,),
        in_specs=[pl.BlockSpec((tt * PEER_SEL,), lambda i: (i,), memory_space=pltpu.SMEM),
                  pl.BlockSpec((tt * SUBLANES, LANES), lambda i: (i, 0)), tok2,
                  pl.BlockSpec(tab.shape, lambda i: (0, 0), pipeline_mode=pl.Buffered(1))],
        out_specs=tok2,
        out_shape=jax.ShapeDtypeStruct((n_tok, PEER_SEL), F32),
        scratch_shapes=[pltpu.VMEM((CHUNK_ROWS, LANES), jnp.int32)] * (2 * N_CHUNKS)
                       + [pltpu.VMEM((N_CHUNKS, tt * SUBLANES, LANES), F32), pltpu.VMEM((tt, PEER_SEL), F32)],
        compiler_params=_cparams(("arbitrary",), VMEM_LIMIT),
        name="peer_u",
    )(idx4, h8, gates, tab)


V_TOKENS_PER_STEP = 16


def _peer_v_kernel(idx_ref, cx_ref, tab_ref, o_ref, *scratch, tt):
    bufs, parts_ref = (scratch[:N_CHUNKS], scratch[N_CHUNKS:2 * N_CHUNKS]), scratch[2 * N_CHUNKS]
    n_chunks = tt * N_CHUNKS
    row = lax.broadcasted_iota(jnp.int32, (SUBLANES, CHUNK_ROWS), 0)
    col = lax.broadcasted_iota(jnp.int32, (SUBLANES, CHUNK_ROWS), 1)
    quarter = (col % ROWS_PER_EXPERT) == (row % ROWS_PER_EXPERT)
    m_lo = jnp.logical_and(row < ROWS_PER_EXPERT, quarter)
    m_hi = jnp.logical_and(row >= ROWS_PER_EXPERT, quarter)
    gather = functools.partial(_gather_chunk, idx_ref, tab_ref, group=SMEM_GROUP, slots=tuple(range(PAIR_CHUNK)))

    def chunk_sum(t, parity, part, buf_ref):
        cx = jnp.broadcast_to(cx_ref[t, pl.ds(part, 1), :], (SUBLANES, CHUNK_ROWS))
        a = jnp.concatenate([jnp.where(m_lo, cx, 0.0), jnp.where(m_hi, cx, 0.0)], axis=1).astype(BF16)
        lo, hi = _unpack_words(buf_ref[...])
        b = jnp.concatenate([lo.astype(BF16), hi.astype(BF16)], axis=0)
        parts_ref[parity, part] = jnp.dot(a, b, preferred_element_type=F32)

    parts_ref[...] = jnp.zeros_like(parts_ref)
    for j in range(N_CHUNKS):
        gather(bufs[0][j], j)

    def body(i, carry):
        for s in range(V_TOKENS_PER_STEP):
            half = s % 2
            t = V_TOKENS_PER_STEP * i + s
            prev = jnp.maximum(t - 1, 0)
            o_ref[prev] = _tree(jnp.add, [parts_ref[1 - half, j] for j in range(N_CHUNKS)])
            for j in range(N_CHUNKS):
                chunk_sum(t, half, j, bufs[half][j])
            for j in range(N_CHUNKS):
                gather(bufs[1 - half][j], jnp.minimum((t + 1) * N_CHUNKS + j, n_chunks - 1))
        return carry

    lax.fori_loop(0, tt // V_TOKENS_PER_STEP, body, 0)
    o_ref[tt - 1] = _tree(jnp.add, [parts_ref[1, j] for j in range(N_CHUNKS)])


def _peer_v(idx4, coef, tab, *, tt=128):
    n_tok = idx4.shape[0] // PEER_SEL
    tt = min(tt, n_tok)
    assert V_TOKENS_PER_STEP % 2 == 0 and tt % V_TOKENS_PER_STEP == 0
    cx = jnp.repeat(coef, ROWS_PER_EXPERT, axis=1).reshape(n_tok, N_CHUNKS, CHUNK_ROWS)
    smem = pl.BlockSpec((tt * PEER_SEL,), lambda i: (i,), memory_space=pltpu.SMEM)
    return pl.pallas_call(
        functools.partial(_peer_v_kernel, tt=tt),
        grid=(n_tok // tt,),
        in_specs=[smem, pl.BlockSpec((tt, N_CHUNKS, CHUNK_ROWS), lambda i: (i, 0, 0)),
                  pl.BlockSpec(tab.shape, lambda i: (0, 0), pipeline_mode=pl.Buffered(1))],
        out_specs=pl.BlockSpec((tt, SUBLANES, LANES), lambda i: (i, 0, 0)),
        out_shape=jax.ShapeDtypeStruct((n_tok, SUBLANES, LANES), F32),
        scratch_shapes=[pltpu.VMEM((CHUNK_ROWS, LANES), jnp.int32)] * (2 * N_CHUNKS)
                       + [pltpu.VMEM((2, N_CHUNKS, SUBLANES, LANES), F32)],
        compiler_params=_cparams(("arbitrary",), VMEM_LIMIT),
        name="peer_v",
    )(idx4, cx, tab)


def _peer(h8, qp, sub_keys, tab_u, tab_v):
    n_tok = qp.shape[1]
    half = sub_keys.shape[-1]
    z = jnp.zeros_like(sub_keys[0])
    keys_pad = jnp.stack([jnp.concatenate([sub_keys[0], z], axis=-1),
                          jnp.concatenate([z, sub_keys[1]], axis=-1)]).astype(BF16)
    assert keys_pad.shape[-1] == 2 * half == qp.shape[-1]
    idx4, gates = _peer_topk(qp, keys_pad)
    idx4 = idx4.reshape(n_tok * PEER_SEL)
    coef = _peer_u(idx4, h8, gates, tab_u)
    out = _peer_v(idx4, coef, tab_v)
    return out.reshape(n_tok * SUBLANES, LANES)


def _final_kernel(x_ref, p_ref, g_ref, o_ref):
    o_ref[...] = _rms(x_ref[...] + _rows_from_chunks(p_ref, x_ref.shape[0]), g_ref[...])


def _final_norm(x, p, g, *, tr=1024):
    rows, dm = x.shape
    tr = min(tr, rows)
    blk = pl.BlockSpec((tr, dm), lambda i: (i, 0))
    return pl.pallas_call(
        _final_kernel,
        grid=(rows // tr,),
        in_specs=[blk, pl.BlockSpec((tr * SUBLANES, LANES), lambda i: (i, 0)),
                  pl.BlockSpec((1, dm), lambda i: (0, 0))],
        out_specs=blk,
        out_shape=jax.ShapeDtypeStruct((rows, dm), F32),
        compiler_params=_cparams(("parallel",), VMEM_LIMIT),
        name="final_norm",
    )(x, p, g.reshape(1, dm))


def kernel(x, w_in, w_out, rel_bias, g_attn, g_ssm, norm_mix, norm_ffn, lam_re, lam_im, log_step, b_re, b_im, c_re, c_im, d_skip, w_glu, w_query, sub_keys, expert_u, expert_v, norm_final):
    bsz, seq, dm = x.shape
    depth = w_in.shape[0]
    prev = None
    biases = [_attn_bias_tables(rel_bias, d) for _, d in DILATED_PATTERNS]
    for l in range(depth):
        x, q, k, v, u_tm = _in_proj(x, prev, norm_mix[l], w_in[l].astype(BF16))
        attn = _attention(q, k, v, biases)
        a, bw, cw = _ssm_params(lam_re[l], lam_im[l], log_step[l], b_re[l], b_im[l], c_re[l], c_im[l])
        y = _ssm_scan(u_tm, a, bw, cw, bsz=bsz)
        wo = w_out[l].astype(BF16)
        z = _ssm_post(y, u_tm, d_skip[l], w_glu[l].astype(BF16), g_ssm[l], wo[ATTN_WIDTH:], bsz=bsz)
        x, h, qp = _mix_out(x, z, attn,
                            g_attn[l], wo[:ATTN_WIDTH], norm_ffn[l], w_query[l].astype(BF16))
        prev = _peer(h, qp, sub_keys[l], _pack_table(expert_u, l), _pack_table(expert_v, l))
    out = _final_norm(x.reshape(bsz * seq, dm), prev, norm_final)
    return out.reshape(bsz, seq, dm)
```

```python
import functools
import math

import numpy as np
import jax
import jax.numpy as jnp
from jax import lax
from jax.experimental import pallas as pl
from jax.experimental.pallas import tpu as pltpu

F32 = jnp.float32
BF16 = jnp.bfloat16

EPS = 1e-6
NEG_INF = -1e30
HEAD_DIM = 64
ATTN_WIDTH = 512
SSM_WIDTH = 512
SSM_GROUP = 16
SSM_STATE = 64
DILATED_PATTERNS = ((128, 1), (512, 4), (2048, 16))
REL_BUCKETS = 32
REL_MAX_DISTANCE = 1024
PEER_HEADS = 8
PEER_KEYS = 128
PEER_TOPK = 16
PEER_SEL = PEER_HEADS * PEER_TOPK

LANES = 128
SUBLANES = 8
QBLK = 128
KWIN = 256
BAND = 64
VMEM_LIMIT = 52 * 1024 * 1024
PEER_U_VMEM_LIMIT = 58 * 1024 * 1024


def _cparams(sem, vmem=None):
    return pltpu.CompilerParams(dimension_semantics=sem, vmem_limit_bytes=vmem)


def _rms(x, g):
    return x * lax.rsqrt(jnp.mean(x * x, axis=-1, keepdims=True) + EPS) * g


def _gelu(x):
    return 0.5 * x * (1.0 + lax.erf(x * (1.0 / math.sqrt(2.0))))


def _rows_from_chunks(p_ref, n_rows):
    return jnp.concatenate([p_ref[pl.ds(c, n_rows, stride=SUBLANES), :] for c in range(SUBLANES)], axis=-1)


def _in_proj_kernel(*refs, has_prev):
    if has_prev:
        x_ref, p_ref, g_ref, w_ref, xo_ref, q_ref, k_ref, v_ref, u_ref = refs
        x = x_ref[0] + _rows_from_chunks(p_ref, x_ref.shape[1])
    else:
        x_ref, g_ref, w_ref, xo_ref, q_ref, k_ref, v_ref, u_ref = refs
        x = x_ref[0]
    xo_ref[0] = x
    h = _rms(x, g_ref[...]).astype(BF16)
    proj = jnp.dot(h, w_ref[...], preferred_element_type=F32)
    a = ATTN_WIDTH
    q_ref[0] = proj[:, :a] * (HEAD_DIM ** -0.5)
    k_ref[0] = proj[:, a:2 * a]
    v_ref[0] = proj[:, 2 * a:3 * a]
    b, bsz = pl.program_id(1), pl.num_programs(1)
    for c in range(SSM_WIDTH // LANES):
        u_ref[c, pl.ds(b, proj.shape[0], stride=bsz), :] = proj[:, 3 * a + c * LANES:3 * a + (c + 1) * LANES]


def _in_proj(x, prev, g, w_bf16, *, ts=256):
    bsz, seq, dm = x.shape
    ts = min(ts, seq)
    row = pl.BlockSpec((1, ts, dm), lambda i, b: (b, i, 0))
    qkv = pl.BlockSpec((1, ts, ATTN_WIDTH), lambda i, b: (b, i, 0))
    ns = seq // ts
    chunks = pl.BlockSpec((ts * SUBLANES, LANES), lambda i, b: (b * ns + i, 0))
    ins = [x] + ([prev] if prev is not None else []) + [g.reshape(1, dm), w_bf16]
    in_specs = [row] + ([chunks] if prev is not None else []) + [
        pl.BlockSpec((1, dm), lambda i, b: (0, 0)),
        pl.BlockSpec(w_bf16.shape, lambda i, b: (0, 0)),
    ]
    slabs = SSM_WIDTH // LANES
    return pl.pallas_call(
        functools.partial(_in_proj_kernel, has_prev=prev is not None),
        grid=(seq // ts, bsz),
        in_specs=in_specs,
        out_specs=[row, qkv, qkv, qkv, pl.BlockSpec((slabs, ts * bsz, LANES), lambda i, b: (0, i, 0))],
        out_shape=[
            jax.ShapeDtypeStruct((bsz, seq, dm), F32),
            jax.ShapeDtypeStruct((bsz, seq, ATTN_WIDTH), F32),
            jax.ShapeDtypeStruct((bsz, seq, ATTN_WIDTH), F32),
            jax.ShapeDtypeStruct((bsz, seq, ATTN_WIDTH), F32),
            jax.ShapeDtypeStruct((slabs, seq * bsz, LANES), F32),
        ],
        compiler_params=_cparams(("parallel", "arbitrary"), VMEM_LIMIT),
        name="in_proj",
    )(*ins)


def _t5_buckets(rel):
    half = REL_BUCKETS // 2
    max_exact = half // 2
    n = np.abs(rel)
    large = max_exact + (np.log(np.maximum(n, 1) / max_exact)
                         / np.log(REL_MAX_DISTANCE / max_exact) * (half - max_exact)).astype(np.int32)
    large = np.minimum(large, half - 1)
    return (np.where(rel > 0, half, 0) + np.where(n < max_exact, n, large)).astype(np.int32)


def _attn_bias_tables(rel_bias, dilation):
    ql = np.arange(QBLK)[:, None]
    kl = np.arange(KWIN)[None, :]
    delta = np.stack([kl + off - ql for off in (0, -BAND, -2 * BAND)])
    buckets = np.where(np.abs(delta) <= BAND, _t5_buckets(delta * dilation), -1)
    rb = rel_bias.astype(F32).T
    bk = jnp.asarray(buckets, jnp.int32)[None]
    tab = jnp.full((rb.shape[0],) + buckets.shape, NEG_INF, F32)
    for b in range(REL_BUCKETS):
        tab = jnp.where(bk == b, rb[:, b][:, None, None, None], tab)
    return tab


BLOCKS_PER_STEP = 4


def _attn_kernel(q_ref, k_ref, v_ref, *rest, seq):
    bias_refs, (o_ref, acc_ref, m_ref, z_ref) = rest[:len(DILATED_PATTERNS)], rest[len(DILATED_PATTERNS):]
    lane = lax.broadcasted_iota(jnp.int32, (QBLK, LANES), 1)
    is_h0 = lane < HEAD_DIM
    dn = (((1,), (1,)), ((), ()))
    nsteps = seq // QBLK

    def rows(start, size, d):
        return pl.ds(start, size) if d == 1 else pl.ds(start, size, stride=d)

    def block(n, d, bias_ref, first, last):
        length = seq // d
        nblk = length // QBLK
        r, i = n >> (nblk.bit_length() - 1), n & (nblk - 1)
        s = i * QBLK
        ks = jnp.clip(s - BAND, 0, length - KWIN)
        var = jnp.where(i == 0, 0, jnp.where(i == nblk - 1, 2, 1))
        q_rows = rows(r + d * s, QBLK, d)
        k_rows = rows(r + d * ks, KWIN, d)
        qb = q_ref[q_rows, :].astype(BF16)
        kb = k_ref[k_rows, :].astype(BF16)
        vb = v_ref[k_rows, :].astype(BF16)
        outs, ms, zs = [], [], []
        for h in range(2):
            keep = is_h0 if h == 0 else jnp.logical_not(is_h0)
            qh = jnp.where(keep, qb, jnp.zeros_like(qb))
            logits = lax.dot_general(qh, kb, dn, preferred_element_type=F32) + bias_ref[h, var]
            m = jnp.max(logits, axis=-1, keepdims=True)
            p = jnp.exp(logits - m)
            outs.append(jnp.dot(p.astype(BF16), vb, preferred_element_type=F32))
            ms.append(jnp.broadcast_to(m, (QBLK, LANES)))
            zs.append(jnp.broadcast_to(jnp.sum(p, axis=-1, keepdims=True), (QBLK, LANES)))
        o = jnp.where(is_h0, outs[0], outs[1])
        m = jnp.where(is_h0, ms[0], ms[1])
        z = jnp.where(is_h0, zs[0], zs[1])
        if not first:
            m_old = m_ref[q_rows, :]
            m_new = jnp.maximum(m_old, m)
            a, b = jnp.exp(m_old - m_new), jnp.exp(m - m_new)
            o = acc_ref[q_rows, :] * a + o * b
            z = z_ref[q_rows, :] * a + z * b
            m = m_new
        if last:
            o_ref[q_rows, :] = o / z
        else:
            acc_ref[q_rows, :] = o
            m_ref[q_rows, :] = m
            z_ref[q_rows, :] = z

    for p, ((_, d), bias_ref) in enumerate(zip(DILATED_PATTERNS, bias_refs)):
        def step(g, carry, d=d, bias_ref=bias_ref, p=p):
            for j in range(BLOCKS_PER_STEP):
                block(g * BLOCKS_PER_STEP + j, d, bias_ref, p == 0, p == len(DILATED_PATTERNS) - 1)
            return carry
        lax.fori_loop(0, nsteps // BLOCKS_PER_STEP, step, 0)


def _attention(q, k, v, biases):
    bsz, seq, width = q.shape
    for _, d in DILATED_PATTERNS:
        length = seq // d
        assert length >= KWIN and length % QBLK == 0 and (length // QBLK) & (length // QBLK - 1) == 0
    assert (seq // QBLK) % BLOCKS_PER_STEP == 0
    blk = pl.BlockSpec((None, seq, LANES), lambda b, c: (b, 0, c))
    bias_spec = pl.BlockSpec((2, 3, QBLK, KWIN), lambda b, c: (c, 0, 0, 0))
    return pl.pallas_call(
        functools.partial(_attn_kernel, seq=seq),
        grid=(bsz, width // LANES),
        in_specs=[blk, blk, blk] + [bias_spec] * len(biases),
        out_specs=blk,
        out_shape=jax.ShapeDtypeStruct((bsz, seq, width), F32),
        scratch_shapes=[pltpu.VMEM((seq, LANES), F32)] * 3,
        compiler_params=_cparams(("parallel", "arbitrary"), VMEM_LIMIT),
        name="attention",
    )(q, k, v, *biases)


SSM_LANE_GROUPS = SSM_WIDTH // LANES
SSM_GB_STATES = (LANES // SSM_GROUP) * SSM_STATE


def _ssm_params(lam_re, lam_im, log_step, b_re, b_im, c_re, c_im):
    f = lambda t: t.astype(F32)
    lr, li = f(lam_re), f(lam_im)
    step = jnp.exp(f(log_step))[..., None]
    mag = jnp.exp(lr * step)
    ar, ai = mag * jnp.cos(li * step), mag * jnp.sin(li * step)
    nr, ni = ar - 1.0, ai
    den = lr * lr + li * li
    cr, ci = (nr * lr + ni * li) / den, (ni * lr - nr * li) / den
    br, bi = f(b_re), f(b_im)
    bbr = cr[..., None] * br - ci[..., None] * bi
    bbi = cr[..., None] * bi + ci[..., None] * br
    gpb = LANES // SSM_GROUP
    eye = jnp.eye(gpb, dtype=F32)

    def in_map(t):
        t = t.reshape(2, SSM_LANE_GROUPS, gpb, SSM_STATE, SSM_GROUP)
        return jnp.einsum('dbgpc,gh->dbgchp', t, eye).reshape(2, SSM_LANE_GROUPS, LANES, SSM_GB_STATES)

    def out_map(t):
        t = t.reshape(2, SSM_LANE_GROUPS, gpb, SSM_GROUP, SSM_STATE)
        return jnp.einsum('dbgcp,gh->dbgphc', t, eye).reshape(2, SSM_LANE_GROUPS, SSM_GB_STATES, LANES)

    bw = jnp.concatenate([in_map(bbr), in_map(bbi)], axis=-1).astype(BF16)
    cw = jnp.concatenate([out_map(f(c_re)), -out_map(f(c_im))], axis=-2).astype(BF16)
    a = jnp.stack([ar.reshape(2, SSM_LANE_GROUPS, SSM_GB_STATES),
                   ai.reshape(2, SSM_LANE_GROUPS, SSM_GB_STATES)], axis=2)
    return a, bw, cw


def _ssm_kernel(u_ref, a_ref, bw_ref, cw_ref, y_ref, st_ref, bu_ref, *, ts, bsz):
    d = pl.program_id(0)
    ns = SSM_GB_STATES

    @pl.when(pl.program_id(1) == 0)
    def _():
        st_ref[...] = jnp.zeros_like(st_ref)

    for gb in range(SSM_LANE_GROUPS):
        ub = u_ref[gb].astype(BF16)
        bu_ref[...] = jnp.dot(ub, bw_ref[0, gb], preferred_element_type=F32)
        ar = jnp.broadcast_to(a_ref[0, gb, 0:1, :], (bsz, ns))
        ai = jnp.broadcast_to(a_ref[0, gb, 1:2, :], (bsz, ns))

        def step(j, carry, ar=ar, ai=ai):
            xr, xi = carry
            tl = jnp.where(d == 0, j, ts - 1 - j)
            r = pl.multiple_of(tl * bsz, bsz)
            nr = ar * xr - ai * xi + bu_ref[pl.ds(r, bsz), :ns]
            ni = ar * xi + ai * xr + bu_ref[pl.ds(r, bsz), ns:]
            bu_ref[pl.ds(r, bsz), :ns] = nr
            bu_ref[pl.ds(r, bsz), ns:] = ni
            return nr, ni

        xr, xi = lax.fori_loop(0, ts, step, (st_ref[gb, :, :ns], st_ref[gb, :, ns:]))
        st_ref[gb, :, :ns] = xr
        st_ref[gb, :, ns:] = xi
        y_ref[0, :, gb * LANES:(gb + 1) * LANES] = jnp.dot(
            bu_ref[...].astype(BF16), cw_ref[0, gb], preferred_element_type=F32)


def _ssm_scan(u_tm, a, bw, cw, *, bsz, ts=64):
    slabs, rows, _ = u_tm.shape
    width = slabs * LANES
    seq = rows // bsz
    ts = min(ts, seq)
    nt = seq // ts
    tblk = lambda d, i: jnp.where(d == 0, i, nt - 1 - i)
    return pl.pallas_call(
        functools.partial(_ssm_kernel, ts=ts, bsz=bsz),
        grid=(2, nt),
        in_specs=[
            pl.BlockSpec((slabs, ts * bsz, LANES), lambda d, i: (0, tblk(d, i), 0)),
            pl.BlockSpec((1,) + a.shape[1:], lambda d, i: (d, 0, 0, 0)),
            pl.BlockSpec((1,) + bw.shape[1:], lambda d, i: (d, 0, 0, 0)),
            pl.BlockSpec((1,) + cw.shape[1:], lambda d, i: (d, 0, 0, 0)),
        ],
        out_specs=pl.BlockSpec((1, ts * bsz, width), lambda d, i: (d, tblk(d, i), 0)),
        out_shape=jax.ShapeDtypeStruct((2, rows, width), F32),
        scratch_shapes=[pltpu.VMEM((SSM_LANE_GROUPS, bsz, 2 * SSM_GB_STATES), F32),
                        pltpu.VMEM((ts * bsz, 2 * SSM_GB_STATES), F32)],
        compiler_params=_cparams(("arbitrary", "arbitrary"), VMEM_LIMIT),
        name="ssm_scan",
    )(u_tm, a, bw, cw)


def _ssm_post_kernel(y_ref, u_ref, d_ref, wg_ref, g_ref, wo_ref, z_ref, slab_ref, *, bsz):
    u = jnp.concatenate([u_ref[c] for c in range(u_ref.shape[0])], axis=-1)
    y = _gelu(y_ref[0] + y_ref[1] + d_ref[...] * u).astype(BF16)
    ab = jnp.dot(y, wg_ref[...], preferred_element_type=F32)
    ssm = ab[:, :SSM_WIDTH] * jax.nn.sigmoid(ab[:, SSM_WIDTH:])
    n = _rms(ssm, g_ref[...]).astype(BF16)
    z = jnp.dot(n, wo_ref[...], preferred_element_type=F32)
    nt, dm = z.shape[0] // bsz, z.shape[1]
    for c in range(dm // LANES):
        slab_ref[c] = z[:, c * LANES:(c + 1) * LANES]
    for b in range(bsz):
        for c in range(dm // LANES):
            z_ref[:, b * dm + c * LANES:b * dm + (c + 1) * LANES] = slab_ref[c, pl.ds(b, nt, stride=bsz), :]


def _ssm_post(y, u_tm, d_skip, w_glu_bf16, g_ssm, w_out_ssm_bf16, *, bsz, tr=512):
    slabs, rows, _ = u_tm.shape
    width = slabs * LANES
    tr = min(tr, rows)
    assert tr % bsz == 0
    dm = w_out_ssm_bf16.shape[1]
    full = lambda a: pl.BlockSpec(a.shape, lambda i: (0,) * a.ndim)
    d2, g2 = d_skip.reshape(1, width), g_ssm.reshape(1, width)
    return pl.pallas_call(
        functools.partial(_ssm_post_kernel, bsz=bsz),
        grid=(rows // tr,),
        in_specs=[pl.BlockSpec((2, tr, width), lambda i: (0, i, 0)),
                  pl.BlockSpec((slabs, tr, LANES), lambda i: (0, i, 0)),
                  full(d2), full(w_glu_bf16), full(g2), full(w_out_ssm_bf16)],
        out_specs=pl.BlockSpec((tr // bsz, bsz * dm), lambda i: (i, 0)),
        out_shape=jax.ShapeDtypeStruct((rows // bsz, bsz * dm), F32),
        scratch_shapes=[pltpu.VMEM((dm // LANES, tr, LANES), F32)],
        compiler_params=_cparams(("parallel",), VMEM_LIMIT),
        name="ssm_post",
    )(y, u_tm, d2, w_glu_bf16, g2, w_out_ssm_bf16)


def _mix_out_kernel(x_ref, z_ref, a_ref, ga_ref, wo_ref, gf_ref, wq_ref, xn_ref, h_ref, q_ref):
    n = _rms(a_ref[0], ga_ref[...]).astype(BF16)
    xn = x_ref[0] + z_ref[...] + jnp.dot(n, wo_ref[...], preferred_element_type=F32)
    xn_ref[0] = xn
    h = _rms(xn, gf_ref[...])
    for c in range(SUBLANES):
        h_ref[pl.ds(c, h.shape[0], stride=SUBLANES), :] = h[:, c * LANES:(c + 1) * LANES]
    qp = jnp.dot(h.astype(BF16), wq_ref[...], preferred_element_type=F32)
    for hd in range(PEER_HEADS):
        q_ref[hd] = qp[:, hd * LANES:(hd + 1) * LANES]


def _mix_out(x, z_tm, attn, g_attn, w_out_attn_bf16, norm_ffn, w_query, *, ts=256):
    bsz, seq, dm = x.shape
    ts = min(ts, seq)
    ns = seq // ts
    row = pl.BlockSpec((1, ts, dm), lambda b, i: (b, i, 0))
    half = pl.BlockSpec((1, ts, ATTN_WIDTH), lambda b, i: (b, i, 0))
    full = lambda a: pl.BlockSpec(a.shape, lambda b, i: (0,) * a.ndim)
    ga, gf = g_attn.reshape(1, ATTN_WIDTH), norm_ffn.reshape(1, dm)
    qdim = w_query.shape[1] // PEER_HEADS
    return pl.pallas_call(
        _mix_out_kernel,
        grid=(bsz, ns),
        in_specs=[row, pl.BlockSpec((ts, dm), lambda b, i: (i, b)), half,
                  full(ga), full(w_out_attn_bf16), full(gf), full(w_query)],
        out_specs=[row, pl.BlockSpec((ts * SUBLANES, LANES), lambda b, i: (b * ns + i, 0)),
                   pl.BlockSpec((PEER_HEADS, ts, qdim), lambda b, i: (0, b * ns + i, 0))],
        out_shape=[jax.ShapeDtypeStruct((bsz, seq, dm), F32),
                   jax.ShapeDtypeStruct((bsz * seq * SUBLANES, LANES), F32),

                   jax.ShapeDtypeStruct((PEER_HEADS, bsz * seq, qdim), F32)],
        compiler_params=_cparams(("parallel", "arbitrary"), VMEM_LIMIT),
        name="mix_out",
    )(x, z_tm, attn, ga, w_out_attn_bf16, gf, w_query)


TOPK_TOKENS = SUBLANES * LANES
KEY_PITCH = PEER_KEYS + 4
_CANDIDATES = tuple((a, b) for a in range(PEER_TOPK) for b in range(PEER_TOPK) if (a + 1) * (b + 1) <= PEER_TOPK)


def _tree(op, xs):
    xs = list(xs)
    while len(xs) > 1:
        xs = [op(xs[i], xs[i + 1]) if i + 1 < len(xs) else xs[i] for i in range(0, len(xs), 2)]
    return xs[0]


def _extract16(problems):
    ninf = jnp.float32(-jnp.inf)

    def better(a, b):
        gt = b[0] > a[0]
        return tuple(jnp.where(gt, y, x) for x, y in zip(a, b))

    def step(r, carry):
        for p in problems:
            s_ref, order = p["s"], p["order"]
            assert list(order) == sorted(order)
            n = len(order)
            rows = [(s_ref[k], order[k]) + ((p["pay"][k],) if p.get("pay") is not None else ())
                    for k in range(n)]
            win = _tree(better, rows)
            m, am = win[0], win[1]
            for k in range(n):
                s_ref[k] = jnp.where(am == order[k], ninf, s_ref[k])
            p["vals"][r] = m
            p["picks"][r] = win[2] if len(win) > 2 else am.astype(jnp.int32)
        return carry

    lax.fori_loop(0, PEER_TOPK, step, 0)


def _peer_topk_kernel(q_ref, k_ref, idx_ref, gate_ref,
                      slab_ref, s1_ref, s2_ref, t1_ref, i1_ref, t2_ref, i2_ref, cand_ref, pay_ref, ts_ref, ex_ref):
    dn = (((1,), (1,)), ((), ()))
    keys = tuple(range(PEER_KEYS))

    def head(h, carry):
        for w, s_ref in ((0, s1_ref), (1, s2_ref)):
            for j in range(SUBLANES):
                slab_ref[j * KEY_PITCH:j * KEY_PITCH + PEER_KEYS, :] = lax.dot_general(
                    k_ref[w, h], q_ref[h, j * LANES:(j + 1) * LANES, :].astype(BF16), dn,
                    preferred_element_type=F32)
            for k in range(PEER_KEYS):
                s_ref[k] = slab_ref[pl.ds(k, SUBLANES, stride=KEY_PITCH), :]
        _extract16([dict(s=s1_ref, order=keys, vals=t1_ref, picks=i1_ref),
                    dict(s=s2_ref, order=keys, vals=t2_ref, picks=i2_ref)])
        for c, (a, b) in enumerate(_CANDIDATES):
            cand_ref[c] = t1_ref[a] + t2_ref[b]
            pay_ref[c] = i1_ref[a] * PEER_KEYS + i2_ref[b]
        _extract16([dict(s=cand_ref, order=tuple(a * PEER_TOPK + b for a, b in _CANDIDATES), pay=pay_ref,
                         vals=ts_ref, picks=ex_ref)])
        top_s = ts_ref[...]
        e = jnp.exp(top_s - jnp.max(top_s, axis=0, keepdims=True))
        gate_ref[0, h] = e / jnp.sum(e, axis=0, keepdims=True)
        idx_ref[0, h] = ex_ref[...] * ROWS_PER_EXPERT
        return carry

    lax.fori_loop(0, PEER_HEADS, head, 0)


def _peer_topk(qp, keys_pad):
    n_tok = qp.shape[1]
    tt = TOPK_TOKENS
    assert n_tok % tt == 0
    shp = (n_tok // tt, PEER_HEADS, PEER_TOPK, SUBLANES, LANES)
    out = pl.BlockSpec((1,) + shp[1:], lambda i: (i, 0, 0, 0, 0))
    vregs = lambda n, dt: pltpu.VMEM((n, SUBLANES, LANES), dt)
    idx, gate = pl.pallas_call(
        _peer_topk_kernel,
        grid=(n_tok // tt,),
        in_specs=[pl.BlockSpec((PEER_HEADS, tt, qp.shape[2]), lambda i: (0, i, 0)),
                  pl.BlockSpec(keys_pad.shape, lambda i: (0, 0, 0, 0))],
        out_specs=[out, out],
        out_shape=[jax.ShapeDtypeStruct(shp, jnp.int32), jax.ShapeDtypeStruct(shp, F32)],
        scratch_shapes=[pltpu.VMEM((SUBLANES * KEY_PITCH, LANES), F32),
                        vregs(PEER_KEYS, F32), vregs(PEER_KEYS, F32),
                        vregs(PEER_TOPK, F32), vregs(PEER_TOPK, jnp.int32),
                        vregs(PEER_TOPK, F32), vregs(PEER_TOPK, jnp.int32),
                        vregs(len(_CANDIDATES), F32), vregs(len(_CANDIDATES), jnp.int32),
                        vregs(PEER_TOPK, F32), vregs(PEER_TOPK, jnp.int32)],
        compiler_params=_cparams(("parallel",), VMEM_LIMIT),
        name="peer_topk",
    )(qp, keys_pad)
    to_tok = lambda a: jnp.transpose(a, (0, 3, 4, 1, 2)).reshape(n_tok, PEER_SEL)
    return to_tok(idx), to_tok(gate)


ROWS_PER_EXPERT = 4
PAIR_CHUNK = 32
CHUNK_ROWS = PAIR_CHUNK * ROWS_PER_EXPERT
SMEM_GROUP = 8
N_CHUNKS = PEER_SEL // PAIR_CHUNK
CHUNK_SHIFT = N_CHUNKS.bit_length() - 1
assert 1 << CHUNK_SHIFT == N_CHUNKS
_GROUP_SLOT = tuple(
    2 * (2 * (0 if (j % 2) else 1) + (1 if (j // 2) in (0, 2) else 0)) + (0 if (j // 2) < 2 else 1)
    for j in range(8))
_CHUNK_SLOT = tuple(8 * (j // 8) + _GROUP_SLOT[j % 8] for j in range(PAIR_CHUNK))


def _pack_table_kernel(t_ref, o_ref):
    rows, dm = t_ref.shape
    for c in range(ROWS_PER_EXPERT):
        lo = lax.bitcast_convert_type(t_ref[:, c * LANES:(c + 1) * LANES].astype(BF16).astype(F32), jnp.int32)
        hi = lax.bitcast_convert_type(
            t_ref[:, dm // 2 + c * LANES:dm // 2 + (c + 1) * LANES].astype(BF16).astype(F32), jnp.int32)
        o_ref[pl.ds(c, rows, stride=ROWS_PER_EXPERT), :] = lax.shift_right_logical(lo, 16) | hi


def _pack_table(tables, layer, *, rows=512):
    _, e, dm = tables.shape
    assert dm == 2 * ROWS_PER_EXPERT * LANES
    rows = min(rows, e)
    return pl.pallas_call(
        _pack_table_kernel,
        grid=(e // rows,),
        in_specs=[pl.BlockSpec((None, rows, dm), lambda i: (layer, i, 0))],
        out_specs=pl.BlockSpec((rows * ROWS_PER_EXPERT, LANES), lambda i: (i, 0)),
        out_shape=jax.ShapeDtypeStruct((e * ROWS_PER_EXPERT, LANES), jnp.int32),
        compiler_params=_cparams(("parallel",), VMEM_LIMIT),
        name="pack_table",
    )(tables)


def _unpack_words(w):
    lo = lax.bitcast_convert_type(w << 16, F32)
    hi = lax.bitcast_convert_type(w & jnp.int32(-65536), F32)
    return lo, hi


def _gather_chunk(idx_ref, tab_ref, buf_ref, c, group, slots):
    for g in range(PAIR_CHUNK // group):
        ids = idx_ref.at[pl.ds(pl.multiple_of(c * PAIR_CHUNK + g * group, group), group)]
        for i in range(group):
            e4 = pl.multiple_of(ids[i], ROWS_PER_EXPERT)
            s = slots[g * group + i] * ROWS_PER_EXPERT
            buf_ref[s:s + ROWS_PER_EXPERT, :] = tab_ref[pl.ds(e4, ROWS_PER_EXPERT), :]


def _chunk_loop(n_chunks, idx_ref, tab_ref, buf_a, buf_b, consume, init, group, slots, per_step=2, extra=None):
    last = n_chunks - 1
    gather = functools.partial(_gather_chunk, idx_ref, tab_ref, group=group, slots=slots)
    gather(buf_a, 0)
    bufs = (buf_a, buf_b)

    def body(i, carry):
        if extra is not None:
            extra(i)
        c = per_step * i
        for u in range(per_step):
            carry = consume(c + u, bufs[u % 2], carry)
            nxt = c + u + 1
            gather(bufs[(u + 1) % 2], jnp.minimum(nxt, last) if u == per_step - 1 else nxt)
        return carry

    return lax.fori_loop(0, n_chunks // per_step, body, init)


U_TOKENS_PER_STEP = 16


def _peer_u_kernel(idx_ref, h_ref, gate_ref, tab_ref, coef_ref, *scratch, tt):
    bufs, m_ref = (scratch[:N_CHUNKS], scratch[N_CHUNKS:2 * N_CHUNKS]), scratch[-1]
    n_chunks = tt * N_CHUNKS
    row = lax.broadcasted_iota(jnp.int32, (SUBLANES, CHUNK_ROWS), 0)
    col = lax.broadcasted_iota(jnp.int32, (SUBLANES, CHUNK_ROWS), 1)
    quarter = (col % ROWS_PER_EXPERT) == (row % ROWS_PER_EXPERT)
    m_lo = jnp.logical_and(row < ROWS_PER_EXPERT, quarter)
    m_hi = jnp.logical_and(row >= ROWS_PER_EXPERT, quarter)
    pr = lax.broadcasted_iota(jnp.int32, (CHUNK_ROWS, LANES), 0) // ROWS_PER_EXPERT
    pc = lax.broadcasted_iota(jnp.int32, (CHUNK_ROWS, LANES), 1)
    place = [(pr + part * PAIR_CHUNK == pc).astype(BF16) for part in range(N_CHUNKS)]
    dn = (((1,), (1,)), ((), ()))
    gather = functools.partial(_gather_chunk, idx_ref, tab_ref, group=SMEM_GROUP, slots=tuple(range(PAIR_CHUNK)))

    def chunk_dots(t, part, buf_ref):
        rows = pl.ds(pl.multiple_of(t * SUBLANES, SUBLANES), SUBLANES)
        x = h_ref[rows, :].astype(BF16)
        lo, hi = _unpack_words(buf_ref[...])
        b = jnp.concatenate([lo.astype(BF16), hi.astype(BF16)], axis=0)
        out = lax.dot_general(x, b, dn, preferred_element_type=F32)
        m = jnp.where(m_lo, out[:, :CHUNK_ROWS], 0.0) + jnp.where(m_hi, out[:, CHUNK_ROWS:], 0.0)
        m_ref[part, pl.ds(t, 1), :] = jnp.sum(m, axis=0, keepdims=True)

    for j in range(N_CHUNKS):
        gather(bufs[0][j], j)

    def body(i, carry):
        for s in range(U_TOKENS_PER_STEP):
            half = s % 2
            t = U_TOKENS_PER_STEP * i + s
            for j in range(N_CHUNKS):
                chunk_dots(t, j, bufs[half][j])
            for j in range(N_CHUNKS):
                gather(bufs[1 - half][j], jnp.minimum((t + 1) * N_CHUNKS + j, n_chunks - 1))
        return carry

    lax.fori_loop(0, tt // U_TOKENS_PER_STEP, body, 0)

    dots = None
    for part in range(N_CHUNKS):
        m = m_ref[part]
        m_top = m.astype(BF16)
        m_rest = (m - m_top.astype(F32)).astype(BF16)
        s = (jnp.dot(m_top, place[part], preferred_element_type=F32)
             + jnp.dot(m_rest, place[part], preferred_element_type=F32))
        dots = s if dots is None else dots + s
    coef_ref[...] = gate_ref[...] * _gelu(dots)


def _peer_u(idx4, h8, gates, tab, *, tt=128):
    n_tok = gates.shape[0]
    tt = min(tt, n_tok)
    assert U_TOKENS_PER_STEP % 2 == 0 and tt % U_TOKENS_PER_STEP == 0
    tok2 = pl.BlockSpec((tt, PEER_SEL), lambda i: (i, 0))
    return pl.pallas_call(
        functools.partial(_peer_u_kernel, tt=tt),
        grid=(n_tok // tt,),
        in_specs=[pl.BlockSpec((tt * PEER_SEL,), lambda i: (i,), memory_space=pltpu.SMEM),
                  pl.BlockSpec((tt * SUBLANES, LANES), lambda i: (i, 0)), tok2,
                  pl.BlockSpec(tab.shape, lambda i: (0, 0), pipeline_mode=pl.Buffered(1))],
        out_specs=tok2,
        out_shape=jax.ShapeDtypeStruct((n_tok, PEER_SEL), F32),
        scratch_shapes=[pltpu.VMEM((CHUNK_ROWS, LANES), jnp.int32)] * (2 * N_CHUNKS)
                       + [pltpu.VMEM((N_CHUNKS, tt, LANES), F32)],
        compiler_params=_cparams(("arbitrary",), VMEM_LIMIT),
        name="peer_u",
    )(idx4, h8, gates, tab)


V_TOKENS_PER_STEP = 16


def _peer_v_kernel(idx_ref, cx_ref, tab_ref, o_ref, *scratch, tt):
    bufs, parts_ref = (scratch[:N_CHUNKS], scratch[N_CHUNKS:2 * N_CHUNKS]), scratch[2 * N_CHUNKS]
    n_chunks = tt * N_CHUNKS
    row = lax.broadcasted_iota(jnp.int32, (SUBLANES, CHUNK_ROWS), 0)
    col = lax.broadcasted_iota(jnp.int32, (SUBLANES, CHUNK_ROWS), 1)
    quarter = (col % ROWS_PER_EXPERT) == (row % ROWS_PER_EXPERT)
    m_lo = jnp.logical_and(row < ROWS_PER_EXPERT, quarter)
    m_hi = jnp.logical_and(row >= ROWS_PER_EXPERT, quarter)
    gather = functools.partial(_gather_chunk, idx_ref, tab_ref, group=SMEM_GROUP, slots=tuple(range(PAIR_CHUNK)))

    def chunk_sum(t, parity, part, buf_ref):
        cx = jnp.broadcast_to(cx_ref[t, pl.ds(part, 1), :], (SUBLANES, CHUNK_ROWS))
        a = jnp.concatenate([jnp.where(m_lo, cx, 0.0), jnp.where(m_hi, cx, 0.0)], axis=1).astype(BF16)
        lo, hi = _unpack_words(buf_ref[...])
        b = jnp.concatenate([lo.astype(BF16), hi.astype(BF16)], axis=0)
        parts_ref[parity, part] = jnp.dot(a, b, preferred_element_type=F32)

    parts_ref[...] = jnp.zeros_like(parts_ref)
    for j in range(N_CHUNKS):
        gather(bufs[0][j], j)

    def body(i, carry):
        for s in range(V_TOKENS_PER_STEP):
            half = s % 2
            t = V_TOKENS_PER_STEP * i + s
            prev = jnp.maximum(t - 1, 0)
            o_ref[prev] = _tree(jnp.add, [parts_ref[1 - half, j] for j in range(N_CHUNKS)])
            for j in range(N_CHUNKS):
                chunk_sum(t, half, j, bufs[half][j])
            for j in range(N_CHUNKS):
                gather(bufs[1 - half][j], jnp.minimum((t + 1) * N_CHUNKS + j, n_chunks - 1))
        return carry

    lax.fori_loop(0, tt // V_TOKENS_PER_STEP, body, 0)
    o_ref[tt - 1] = _tree(jnp.add, [parts_ref[1, j] for j in range(N_CHUNKS)])


def _peer_v(idx4, coef, tab, *, tt=128):
    n_tok = idx4.shape[0] // PEER_SEL
    tt = min(tt, n_tok)
    assert V_TOKENS_PER_STEP % 2 == 0 and tt % V_TOKENS_PER_STEP == 0
    cx = jnp.repeat(coef, ROWS_PER_EXPERT, axis=1).reshape(n_tok, N_CHUNKS, CHUNK_ROWS)
    smem = pl.BlockSpec((tt * PEER_SEL,), lambda i: (i,), memory_space=pltpu.SMEM)
    return pl.pallas_call(
        functools.partial(_peer_v_kernel, tt=tt),
        grid=(n_tok // tt,),
        in_specs=[smem, pl.BlockSpec((tt, N_CHUNKS, CHUNK_ROWS), lambda i: (i, 0, 0)),
                  pl.BlockSpec(tab.shape, lambda i: (0, 0), pipeline_mode=pl.Buffered(1))],
        out_specs=pl.BlockSpec((tt, SUBLANES, LANES), lambda i: (i, 0, 0)),
        out_shape=jax.ShapeDtypeStruct((n_tok, SUBLANES, LANES), F32),
        scratch_shapes=[pltpu.VMEM((CHUNK_ROWS, LANES), jnp.int32)] * (2 * N_CHUNKS)
                       + [pltpu.VMEM((2, N_CHUNKS, SUBLANES, LANES), F32)],
        compiler_params=_cparams(("arbitrary",), VMEM_LIMIT),
        name="peer_v",
    )(idx4, cx, tab)


def _peer(h8, qp, sub_keys, tab_u, tab_v):
    n_tok = qp.shape[1]
    half = sub_keys.shape[-1]
    z = jnp.zeros_like(sub_keys[0])
    keys_pad = jnp.stack([jnp.concatenate([sub_keys[0], z], axis=-1),
                          jnp.concatenate([z, sub_keys[1]], axis=-1)]).astype(BF16)
    assert keys_pad.shape[-1] == 2 * half == qp.shape[-1]
    idx4, gates = _peer_topk(qp, keys_pad)
    idx4 = idx4.reshape(n_tok * PEER_SEL)
    coef = _peer_u(idx4, h8, gates, tab_u)
    out = _peer_v(idx4, coef, tab_v)
    return out.reshape(n_tok * SUBLANES, LANES)


def _final_kernel(x_ref, p_ref, g_ref, o_ref):
    o_ref[...] = _rms(x_ref[...] + _rows_from_chunks(p_ref, x_ref.shape[0]), g_ref[...])


def _final_norm(x, p, g, *, tr=1024):
    rows, dm = x.shape
    tr = min(tr, rows)
    blk = pl.BlockSpec((tr, dm), lambda i: (i, 0))
    return pl.pallas_call(
        _final_kernel,
        grid=(rows // tr,),
        in_specs=[blk, pl.BlockSpec((tr * SUBLANES, LANES), lambda i: (i, 0)),
                  pl.BlockSpec((1, dm), lambda i: (0, 0))],
        out_specs=blk,
        out_shape=jax.ShapeDtypeStruct((rows, dm), F32),
        compiler_params=_cparams(("parallel",), VMEM_LIMIT),
        name="final_norm",
    )(x, p, g.reshape(1, dm))


def kernel(x, w_in, w_out, rel_bias, g_attn, g_ssm, norm_mix, norm_ffn, lam_re, lam_im, log_step, b_re, b_im, c_re, c_im, d_skip, w_glu, w_query, sub_keys, expert_u, expert_v, norm_final):
    bsz, seq, dm = x.shape
    depth = w_in.shape[0]
    prev = None
    biases = [_attn_bias_tables(rel_bias, d) for _, d in DILATED_PATTERNS]
    for l in range(depth):
        x, q, k, v, u_tm = _in_proj(x, prev, norm_mix[l], w_in[l].astype(BF16))
        attn = _attention(q, k, v, biases)
        a, bw, cw = _ssm_params(lam_re[l], lam_im[l], log_step[l], b_re[l], b_im[l], c_re[l], c_im[l])
        y = _ssm_scan(u_tm, a, bw, cw, bsz=bsz)
        wo = w_out[l].astype(BF16)
        z = _ssm_post(y, u_tm, d_skip[l], w_glu[l].astype(BF16), g_ssm[l], wo[ATTN_WIDTH:], bsz=bsz)
        x, h, qp = _mix_out(x, z, attn,
                            g_attn[l], wo[:ATTN_WIDTH], norm_ffn[l], w_query[l].astype(BF16))
        prev = _peer(h, qp, sub_keys[l], _pack_table(expert_u, l), _pack_table(expert_v, l))
    out = _final_norm(x.reshape(bsz * seq, dm), prev, norm_final)
    return out.reshape(bsz, seq, dm)
```

```python
import functools
import math

import numpy as np
import jax
import jax.numpy as jnp
from jax import lax
from jax.experimental import pallas as pl
from jax.experimental.pallas import tpu as pltpu

F32 = jnp.float32
BF16 = jnp.bfloat16

EPS = 1e-6
NEG_INF = -1e30
HEAD_DIM = 64
ATTN_WIDTH = 512
SSM_WIDTH = 512
SSM_GROUP = 16
SSM_STATE = 64
DILATED_PATTERNS = ((128, 1), (512, 4), (2048, 16))
REL_BUCKETS = 32
REL_MAX_DISTANCE = 1024
PEER_HEADS = 8
PEER_KEYS = 128
PEER_TOPK = 16
PEER_SEL = PEER_HEADS * PEER_TOPK

LANES = 128
SUBLANES = 8
QBLK = 128
KWIN = 256
BAND = 64
VMEM_LIMIT = 52 * 1024 * 1024


def _cparams(sem, vmem=None):
    return pltpu.CompilerParams(dimension_semantics=sem, vmem_limit_bytes=vmem)


def _rms(x, g):
    return x * lax.rsqrt(jnp.mean(x * x, axis=-1, keepdims=True) + EPS) * g


def _gelu(x):
    return 0.5 * x * (1.0 + lax.erf(x * (1.0 / math.sqrt(2.0))))


def _rows_from_chunks(p_ref, n_rows):
    return jnp.concatenate([p_ref[pl.ds(c, n_rows, stride=SUBLANES), :] for c in range(SUBLANES)], axis=-1)


def _in_proj_kernel(*refs, has_prev):
    if has_prev:
        x_ref, p_ref, g_ref, w_ref, xo_ref, q_ref, k_ref, v_ref, u_ref = refs
        x = x_ref[0] + _rows_from_chunks(p_ref, x_ref.shape[1])
    else:
        x_ref, g_ref, w_ref, xo_ref, q_ref, k_ref, v_ref, u_ref = refs
        x = x_ref[0]
    xo_ref[0] = x
    h = _rms(x, g_ref[...]).astype(BF16)
    proj = jnp.dot(h, w_ref[...], preferred_element_type=F32)
    a = ATTN_WIDTH
    q_ref[0] = proj[:, :a] * (HEAD_DIM ** -0.5)
    k_ref[0] = proj[:, a:2 * a]
    v_ref[0] = proj[:, 2 * a:3 * a]
    b, bsz = pl.program_id(1), pl.num_programs(1)
    for c in range(SSM_WIDTH // LANES):
        u_ref[c, pl.ds(b, proj.shape[0], stride=bsz), :] = proj[:, 3 * a + c * LANES:3 * a + (c + 1) * LANES]


def _in_proj(x, prev, g, w_bf16, *, ts=256):
    bsz, seq, dm = x.shape
    ts = min(ts, seq)
    row = pl.BlockSpec((1, ts, dm), lambda i, b: (b, i, 0))
    qkv = pl.BlockSpec((1, ts, ATTN_WIDTH), lambda i, b: (b, i, 0))
    ns = seq // ts
    chunks = pl.BlockSpec((ts * SUBLANES, LANES), lambda i, b: (b * ns + i, 0))
    ins = [x] + ([prev] if prev is not None else []) + [g.reshape(1, dm), w_bf16]
    in_specs = [row] + ([chunks] if prev is not None else []) + [
        pl.BlockSpec((1, dm), lambda i, b: (0, 0)),
        pl.BlockSpec(w_bf16.shape, lambda i, b: (0, 0)),
    ]
    slabs = SSM_WIDTH // LANES
    return pl.pallas_call(
        functools.partial(_in_proj_kernel, has_prev=prev is not None),
        grid=(seq // ts, bsz),
        in_specs=in_specs,
        out_specs=[row, qkv, qkv, qkv, pl.BlockSpec((slabs, ts * bsz, LANES), lambda i, b: (0, i, 0))],
        out_shape=[
            jax.ShapeDtypeStruct((bsz, seq, dm), F32),
            jax.ShapeDtypeStruct((bsz, seq, ATTN_WIDTH), F32),
            jax.ShapeDtypeStruct((bsz, seq, ATTN_WIDTH), F32),
            jax.ShapeDtypeStruct((bsz, seq, ATTN_WIDTH), F32),
            jax.ShapeDtypeStruct((slabs, seq * bsz, LANES), F32),
        ],
        compiler_params=_cparams(("parallel", "arbitrary"), VMEM_LIMIT),
        name="in_proj",
    )(*ins)


def _t5_buckets(rel):
    half = REL_BUCKETS // 2
    max_exact = half // 2
    n = np.abs(rel)
    large = max_exact + (np.log(np.maximum(n, 1) / max_exact)
                         / np.log(REL_MAX_DISTANCE / max_exact) * (half - max_exact)).astype(np.int32)
    large = np.minimum(large, half - 1)
    return (np.where(rel > 0, half, 0) + np.where(n < max_exact, n, large)).astype(np.int32)


def _attn_bias_tables(rel_bias, dilation):
    ql = np.arange(QBLK)[:, None]
    kl = np.arange(KWIN)[None, :]
    delta = np.stack([kl + off - ql for off in (0, -BAND, -2 * BAND)])
    buckets = np.where(np.abs(delta) <= BAND, _t5_buckets(delta * dilation), -1)
    rb = rel_bias.astype(F32).T
    bk = jnp.asarray(buckets, jnp.int32)[None]
    tab = jnp.full((rb.shape[0],) + buckets.shape, NEG_INF, F32)
    for b in range(REL_BUCKETS):
        tab = jnp.where(bk == b, rb[:, b][:, None, None, None], tab)
    return tab


BLOCKS_PER_STEP = 4


def _attn_kernel(q_ref, k_ref, v_ref, *rest, seq):
    bias_refs, (o_ref, acc_ref, m_ref, z_ref) = rest[:len(DILATED_PATTERNS)], rest[len(DILATED_PATTERNS):]
    lane = lax.broadcasted_iota(jnp.int32, (QBLK, LANES), 1)
    is_h0 = lane < HEAD_DIM
    dn = (((1,), (1,)), ((), ()))
    nsteps = seq // QBLK

    def rows(start, size, d):
        return pl.ds(start, size) if d == 1 else pl.ds(start, size, stride=d)

    def block(n, d, bias_ref, first, last):
        length = seq // d
        nblk = length // QBLK
        r, i = n >> (nblk.bit_length() - 1), n & (nblk - 1)
        s = i * QBLK
        ks = jnp.clip(s - BAND, 0, length - KWIN)
        var = jnp.where(i == 0, 0, jnp.where(i == nblk - 1, 2, 1))
        q_rows = rows(r + d * s, QBLK, d)
        k_rows = rows(r + d * ks, KWIN, d)
        qb = q_ref[q_rows, :].astype(BF16)
        kb = k_ref[k_rows, :].astype(BF16)
        vb = v_ref[k_rows, :].astype(BF16)
        outs, ms, zs = [], [], []
        for h in range(2):
            keep = is_h0 if h == 0 else jnp.logical_not(is_h0)
            qh = jnp.where(keep, qb, jnp.zeros_like(qb))
            logits = lax.dot_general(qh, kb, dn, preferred_element_type=F32) + bias_ref[h, var]
            m = jnp.max(logits, axis=-1, keepdims=True)
            p = jnp.exp(logits - m)
            outs.append(jnp.dot(p.astype(BF16), vb, preferred_element_type=F32))
            ms.append(jnp.broadcast_to(m, (QBLK, LANES)))
            zs.append(jnp.broadcast_to(jnp.sum(p, axis=-1, keepdims=True), (QBLK, LANES)))
        o = jnp.where(is_h0, outs[0], outs[1])
        m = jnp.where(is_h0, ms[0], ms[1])
        z = jnp.where(is_h0, zs[0], zs[1])
        if not first:
            m_old = m_ref[q_rows, :]
            m_new = jnp.maximum(m_old, m)
            a, b = jnp.exp(m_old - m_new), jnp.exp(m - m_new)
            o = acc_ref[q_rows, :] * a + o * b
            z = z_ref[q_rows, :] * a + z * b
            m = m_new
        if last:
            o_ref[q_rows, :] = o / z
        else:
            acc_ref[q_rows, :] = o
            m_ref[q_rows, :] = m
            z_ref[q_rows, :] = z

    for p, ((_, d), bias_ref) in enumerate(zip(DILATED_PATTERNS, bias_refs)):
        def step(g, carry, d=d, bias_ref=bias_ref, p=p):
            for j in range(BLOCKS_PER_STEP):
                block(g * BLOCKS_PER_STEP + j, d, bias_ref, p == 0, p == len(DILATED_PATTERNS) - 1)
            return carry
        lax.fori_loop(0, nsteps // BLOCKS_PER_STEP, step, 0)


def _attention(q, k, v, biases):
    bsz, seq, width = q.shape
    for _, d in DILATED_PATTERNS:
        length = seq // d
        assert length >= KWIN and length % QBLK == 0 and (length // QBLK) & (length // QBLK - 1) == 0
    assert (seq // QBLK) % BLOCKS_PER_STEP == 0
    blk = pl.BlockSpec((None, seq, LANES), lambda b, c: (b, 0, c))
    bias_spec = pl.BlockSpec((2, 3, QBLK, KWIN), lambda b, c: (c, 0, 0, 0))
    return pl.pallas_call(
        functools.partial(_attn_kernel, seq=seq),
        grid=(bsz, width // LANES),
        in_specs=[blk, blk, blk] + [bias_spec] * len(biases),
        out_specs=blk,
        out_shape=jax.ShapeDtypeStruct((bsz, seq, width), F32),
        scratch_shapes=[pltpu.VMEM((seq, LANES), F32)] * 3,
        compiler_params=_cparams(("parallel", "arbitrary"), VMEM_LIMIT),
        name="attention",
    )(q, k, v, *biases)


SSM_LANE_GROUPS = SSM_WIDTH // LANES
SSM_GB_STATES = (LANES // SSM_GROUP) * SSM_STATE


def _ssm_params(lam_re, lam_im, log_step, b_re, b_im, c_re, c_im):
    f = lambda t: t.astype(F32)
    lr, li = f(lam_re), f(lam_im)
    step = jnp.exp(f(log_step))[..., None]
    mag = jnp.exp(lr * step)
    ar, ai = mag * jnp.cos(li * step), mag * jnp.sin(li * step)
    nr, ni = ar - 1.0, ai
    den = lr * lr + li * li
    cr, ci = (nr * lr + ni * li) / den, (ni * lr - nr * li) / den
    br, bi = f(b_re), f(b_im)
    bbr = cr[..., None] * br - ci[..., None] * bi
    bbi = cr[..., None] * bi + ci[..., None] * br
    gpb = LANES // SSM_GROUP
    eye = jnp.eye(gpb, dtype=F32)

    def in_map(t):
        t = t.reshape(2, SSM_LANE_GROUPS, gpb, SSM_STATE, SSM_GROUP)
        return jnp.einsum('dbgpc,gh->dbgchp', t, eye).reshape(2, SSM_LANE_GROUPS, LANES, SSM_GB_STATES)

    def out_map(t):
        t = t.reshape(2, SSM_LANE_GROUPS, gpb, SSM_GROUP, SSM_STATE)
        return jnp.einsum('dbgcp,gh->dbgphc', t, eye).reshape(2, SSM_LANE_GROUPS, SSM_GB_STATES, LANES)

    bw = jnp.concatenate([in_map(bbr), in_map(bbi)], axis=-1).astype(BF16)
    cw = jnp.concatenate([out_map(f(c_re)), -out_map(f(c_im))], axis=-2).astype(BF16)
    a = jnp.stack([ar.reshape(2, SSM_LANE_GROUPS, SSM_GB_STATES),
                   ai.reshape(2, SSM_LANE_GROUPS, SSM_GB_STATES)], axis=2)
    return a, bw, cw


def _ssm_kernel(u_ref, a_ref, bw_ref, cw_ref, y_ref, st_ref, bu_ref, *, ts, bsz):
    d = pl.program_id(0)
    ns = SSM_GB_STATES

    @pl.when(pl.program_id(1) == 0)
    def _():
        st_ref[...] = jnp.zeros_like(st_ref)

    for gb in range(SSM_LANE_GROUPS):
        ub = u_ref[gb].astype(BF16)
        bu_ref[...] = jnp.dot(ub, bw_ref[0, gb], preferred_element_type=F32)
        ar = jnp.broadcast_to(a_ref[0, gb, 0:1, :], (bsz, ns))
        ai = jnp.broadcast_to(a_ref[0, gb, 1:2, :], (bsz, ns))

        def step(j, carry, ar=ar, ai=ai):
            xr, xi = carry
            tl = jnp.where(d == 0, j, ts - 1 - j)
            r = pl.multiple_of(tl * bsz, bsz)
            nr = ar * xr - ai * xi + bu_ref[pl.ds(r, bsz), :ns]
            ni = ar * xi + ai * xr + bu_ref[pl.ds(r, bsz), ns:]
            bu_ref[pl.ds(r, bsz), :ns] = nr
            bu_ref[pl.ds(r, bsz), ns:] = ni
            return nr, ni

        xr, xi = lax.fori_loop(0, ts, step, (st_ref[gb, :, :ns], st_ref[gb, :, ns:]))
        st_ref[gb, :, :ns] = xr
        st_ref[gb, :, ns:] = xi
        y_ref[0, :, gb * LANES:(gb + 1) * LANES] = jnp.dot(
            bu_ref[...].astype(BF16), cw_ref[0, gb], preferred_element_type=F32)


def _ssm_scan(u_tm, a, bw, cw, *, bsz, ts=64):
    slabs, rows, _ = u_tm.shape
    width = slabs * LANES
    seq = rows // bsz
    ts = min(ts, seq)
    nt = seq // ts
    tblk = lambda d, i: jnp.where(d == 0, i, nt - 1 - i)
    return pl.pallas_call(
        functools.partial(_ssm_kernel, ts=ts, bsz=bsz),
        grid=(2, nt),
        in_specs=[
            pl.BlockSpec((slabs, ts * bsz, LANES), lambda d, i: (0, tblk(d, i), 0)),
            pl.BlockSpec((1,) + a.shape[1:], lambda d, i: (d, 0, 0, 0)),
            pl.BlockSpec((1,) + bw.shape[1:], lambda d, i: (d, 0, 0, 0)),
            pl.BlockSpec((1,) + cw.shape[1:], lambda d, i: (d, 0, 0, 0)),
        ],
        out_specs=pl.BlockSpec((1, ts * bsz, width), lambda d, i: (d, tblk(d, i), 0)),
        out_shape=jax.ShapeDtypeStruct((2, rows, width), F32),
        scratch_shapes=[pltpu.VMEM((SSM_LANE_GROUPS, bsz, 2 * SSM_GB_STATES), F32),
                        pltpu.VMEM((ts * bsz, 2 * SSM_GB_STATES), F32)],
        compiler_params=_cparams(("arbitrary", "arbitrary"), VMEM_LIMIT),
        name="ssm_scan",
    )(u_tm, a, bw, cw)


def _ssm_post_kernel(y_ref, u_ref, d_ref, wg_ref, g_ref, wo_ref, z_ref, slab_ref, *, bsz):
    u = jnp.concatenate([u_ref[c] for c in range(u_ref.shape[0])], axis=-1)
    y = _gelu(y_ref[0] + y_ref[1] + d_ref[...] * u).astype(BF16)
    ab = jnp.dot(y, wg_ref[...], preferred_element_type=F32)
    ssm = ab[:, :SSM_WIDTH] * jax.nn.sigmoid(ab[:, SSM_WIDTH:])
    n = _rms(ssm, g_ref[...]).astype(BF16)
    z = jnp.dot(n, wo_ref[...], preferred_element_type=F32)
    nt, dm = z.shape[0] // bsz, z.shape[1]
    for c in range(dm // LANES):
        slab_ref[c] = z[:, c * LANES:(c + 1) * LANES]
    for b in range(bsz):
        for c in range(dm // LANES):
            z_ref[:, b * dm + c * LANES:b * dm + (c + 1) * LANES] = slab_ref[c, pl.ds(b, nt, stride=bsz), :]


def _ssm_post(y, u_tm, d_skip, w_glu_bf16, g_ssm, w_out_ssm_bf16, *, bsz, tr=512):
    slabs, rows, _ = u_tm.shape
    width = slabs * LANES
    tr = min(tr, rows)
    assert tr % bsz == 0
    dm = w_out_ssm_bf16.shape[1]
    full = lambda a: pl.BlockSpec(a.shape, lambda i: (0,) * a.ndim)
    d2, g2 = d_skip.reshape(1, width), g_ssm.reshape(1, width)
    return pl.pallas_call(
        functools.partial(_ssm_post_kernel, bsz=bsz),
        grid=(rows // tr,),
        in_specs=[pl.BlockSpec((2, tr, width), lambda i: (0, i, 0)),
                  pl.BlockSpec((slabs, tr, LANES), lambda i: (0, i, 0)),
                  full(d2), full(w_glu_bf16), full(g2), full(w_out_ssm_bf16)],
        out_specs=pl.BlockSpec((tr // bsz, bsz * dm), lambda i: (i, 0)),
        out_shape=jax.ShapeDtypeStruct((rows // bsz, bsz * dm), F32),
        scratch_shapes=[pltpu.VMEM((dm // LANES, tr, LANES), F32)],
        compiler_params=_cparams(("parallel",), VMEM_LIMIT),
        name="ssm_post",
    )(y, u_tm, d2, w_glu_bf16, g2, w_out_ssm_bf16)


def _mix_out_kernel(x_ref, z_ref, a_ref, ga_ref, wo_ref, gf_ref, wq_ref, xn_ref, h_ref, q_ref):
    n = _rms(a_ref[0], ga_ref[...]).astype(BF16)
    xn = x_ref[0] + z_ref[...] + jnp.dot(n, wo_ref[...], preferred_element_type=F32)
    xn_ref[0] = xn
    h = _rms(xn, gf_ref[...])
    for c in range(SUBLANES):
        h_ref[pl.ds(c, h.shape[0], stride=SUBLANES), :] = h[:, c * LANES:(c + 1) * LANES]
    qp = jnp.dot(h.astype(BF16), wq_ref[...], preferred_element_type=F32)
    for hd in range(PEER_HEADS):
        q_ref[hd] = qp[:, hd * LANES:(hd + 1) * LANES]


def _mix_out(x, z_tm, attn, g_attn, w_out_attn_bf16, norm_ffn, w_query, *, ts=256):
    bsz, seq, dm = x.shape
    ts = min(ts, seq)
    ns = seq // ts
    row = pl.BlockSpec((1, ts, dm), lambda b, i: (b, i, 0))
    half = pl.BlockSpec((1, ts, ATTN_WIDTH), lambda b, i: (b, i, 0))
    full = lambda a: pl.BlockSpec(a.shape, lambda b, i: (0,) * a.ndim)
    ga, gf = g_attn.reshape(1, ATTN_WIDTH), norm_ffn.reshape(1, dm)
    qdim = w_query.shape[1] // PEER_HEADS
    return pl.pallas_call(
        _mix_out_kernel,
        grid=(bsz, ns),
        in_specs=[row, pl.BlockSpec((ts, dm), lambda b, i: (i, b)), half,
                  full(ga), full(w_out_attn_bf16), full(gf), full(w_query)],
        out_specs=[row, pl.BlockSpec((ts * SUBLANES, LANES), lambda b, i: (b * ns + i, 0)),
                   pl.BlockSpec((PEER_HEADS, ts, qdim), lambda b, i: (0, b * ns + i, 0))],
        out_shape=[jax.ShapeDtypeStruct((bsz, seq, dm), F32),
                   jax.ShapeDtypeStruct((bsz * seq * SUBLANES, LANES), F32),

                   jax.ShapeDtypeStruct((PEER_HEADS, bsz * seq, qdim), F32)],
        compiler_params=_cparams(("parallel", "arbitrary"), VMEM_LIMIT),
        name="mix_out",
    )(x, z_tm, attn, ga, w_out_attn_bf16, gf, w_query)


TOPK_TOKENS = SUBLANES * LANES
KEY_PITCH = PEER_KEYS + 4
_CANDIDATES = tuple((a, b) for a in range(PEER_TOPK) for b in range(PEER_TOPK) if (a + 1) * (b + 1) <= PEER_TOPK)


def _tree(op, xs):
    xs = list(xs)
    while len(xs) > 1:
        xs = [op(xs[i], xs[i + 1]) if i + 1 < len(xs) else xs[i] for i in range(0, len(xs), 2)]
    return xs[0]


def _extract16(problems):
    ninf = jnp.float32(-jnp.inf)

    def better(a, b):
        gt = b[0] > a[0]
        return tuple(jnp.where(gt, y, x) for x, y in zip(a, b))

    def step(r, carry):
        for p in problems:
            s_ref, order = p["s"], p["order"]
            assert list(order) == sorted(order)
            n = len(order)
            rows = [(s_ref[k], order[k]) + ((p["pay"][k],) if p.get("pay") is not None else ())
                    for k in range(n)]
            win = _tree(better, rows)
            m, am = win[0], win[1]
            for k in range(n):
                s_ref[k] = jnp.where(am == order[k], ninf, s_ref[k])
            p["vals"][r] = m
            p["picks"][r] = win[2] if len(win) > 2 else am.astype(jnp.int32)
        return carry

    lax.fori_loop(0, PEER_TOPK, step, 0)


def _peer_topk_kernel(q_ref, k_ref, idx_ref, gate_ref,
                      slab_ref, s1_ref, s2_ref, t1_ref, i1_ref, t2_ref, i2_ref, cand_ref, pay_ref, ts_ref, ex_ref):
    dn = (((1,), (1,)), ((), ()))
    keys = tuple(range(PEER_KEYS))

    def head(h, carry):
        for w, s_ref in ((0, s1_ref), (1, s2_ref)):
            for j in range(SUBLANES):
                slab_ref[j * KEY_PITCH:j * KEY_PITCH + PEER_KEYS, :] = lax.dot_general(
                    k_ref[w, h], q_ref[h, j * LANES:(j + 1) * LANES, :].astype(BF16), dn,
                    preferred_element_type=F32)
            for k in range(PEER_KEYS):
                s_ref[k] = slab_ref[pl.ds(k, SUBLANES, stride=KEY_PITCH), :]
        _extract16([dict(s=s1_ref, order=keys, vals=t1_ref, picks=i1_ref),
                    dict(s=s2_ref, order=keys, vals=t2_ref, picks=i2_ref)])
        for c, (a, b) in enumerate(_CANDIDATES):
            cand_ref[c] = t1_ref[a] + t2_ref[b]
            pay_ref[c] = i1_ref[a] * PEER_KEYS + i2_ref[b]
        _extract16([dict(s=cand_ref, order=tuple(a * PEER_TOPK + b for a, b in _CANDIDATES), pay=pay_ref,
                         vals=ts_ref, picks=ex_ref)])
        top_s = ts_ref[...]
        e = jnp.exp(top_s - jnp.max(top_s, axis=0, keepdims=True))
        gate_ref[0, h] = e / jnp.sum(e, axis=0, keepdims=True)
        idx_ref[0, h] = ex_ref[...] * ROWS_PER_EXPERT
        return carry

    lax.fori_loop(0, PEER_HEADS, head, 0)


def _peer_topk(qp, keys_pad):
    n_tok = qp.shape[1]
    tt = TOPK_TOKENS
    assert n_tok % tt == 0
    shp = (n_tok // tt, PEER_HEADS, PEER_TOPK, SUBLANES, LANES)
    out = pl.BlockSpec((1,) + shp[1:], lambda i: (i, 0, 0, 0, 0))
    vregs = lambda n, dt: pltpu.VMEM((n, SUBLANES, LANES), dt)
    idx, gate = pl.pallas_call(
        _peer_topk_kernel,
        grid=(n_tok // tt,),
        in_specs=[pl.BlockSpec((PEER_HEADS, tt, qp.shape[2]), lambda i: (0, i, 0)),
                  pl.BlockSpec(keys_pad.shape, lambda i: (0, 0, 0, 0))],
        out_specs=[out, out],
        out_shape=[jax.ShapeDtypeStruct(shp, jnp.int32), jax.ShapeDtypeStruct(shp, F32)],
        scratch_shapes=[pltpu.VMEM((SUBLANES * KEY_PITCH, LANES), F32),
                        vregs(PEER_KEYS, F32), vregs(PEER_KEYS, F32),
                        vregs(PEER_TOPK, F32), vregs(PEER_TOPK, jnp.int32),
                        vregs(PEER_TOPK, F32), vregs(PEER_TOPK, jnp.int32),
                        vregs(len(_CANDIDATES), F32), vregs(len(_CANDIDATES), jnp.int32),
                        vregs(PEER_TOPK, F32), vregs(PEER_TOPK, jnp.int32)],
        compiler_params=_cparams(("parallel",), VMEM_LIMIT),
        name="peer_topk",
    )(qp, keys_pad)
    to_tok = lambda a: jnp.transpose(a, (0, 3, 4, 1, 2)).reshape(n_tok, PEER_SEL)
    return to_tok(idx), to_tok(gate)


ROWS_PER_EXPERT = 4
PAIR_CHUNK = 32
CHUNK_ROWS = PAIR_CHUNK * ROWS_PER_EXPERT
SMEM_GROUP = 8
N_CHUNKS = PEER_SEL // PAIR_CHUNK
TOKENS_PER_STEP = 32


def _pack_table_kernel(t_ref, o_ref):
    rows, dm = t_ref.shape
    for c in range(ROWS_PER_EXPERT):
        lo = lax.bitcast_convert_type(t_ref[:, c * LANES:(c + 1) * LANES].astype(BF16).astype(F32), jnp.int32)
        hi = lax.bitcast_convert_type(
            t_ref[:, dm // 2 + c * LANES:dm // 2 + (c + 1) * LANES].astype(BF16).astype(F32), jnp.int32)
        o_ref[pl.ds(c, rows, stride=ROWS_PER_EXPERT), :] = lax.shift_right_logical(lo, 16) | hi


def _pack_table(tables, layer, *, rows=512):
    _, e, dm = tables.shape
    assert dm == 2 * ROWS_PER_EXPERT * LANES
    rows = min(rows, e)
    return pl.pallas_call(
        _pack_table_kernel,
        grid=(e // rows,),
        in_specs=[pl.BlockSpec((None, rows, dm), lambda i: (layer, i, 0))],
        out_specs=pl.BlockSpec((rows * ROWS_PER_EXPERT, LANES), lambda i: (i, 0)),
        out_shape=jax.ShapeDtypeStruct((e * ROWS_PER_EXPERT, LANES), jnp.int32),
        compiler_params=_cparams(("parallel",), VMEM_LIMIT),
        name="pack_table",
    )(tables)


def _unpack_words(w):
    lo = lax.bitcast_convert_type(w << 16, F32)
    hi = lax.bitcast_convert_type(w & jnp.int32(-65536), F32)
    return lo, hi


def _gather_chunk(idx_ref, tab_ref, buf_ref, c):
    for g in range(PAIR_CHUNK // SMEM_GROUP):
        ids = idx_ref.at[pl.ds(pl.multiple_of(c * PAIR_CHUNK + g * SMEM_GROUP, SMEM_GROUP), SMEM_GROUP)]
        for i in range(SMEM_GROUP):
            e4 = pl.multiple_of(ids[i], ROWS_PER_EXPERT)
            s = (g * SMEM_GROUP + i) * ROWS_PER_EXPERT
            buf_ref[s:s + ROWS_PER_EXPERT, :] = tab_ref[pl.ds(e4, ROWS_PER_EXPERT), :]


def _peer_u_kernel(idx_ref, h_ref, gate_ref, tab_ref, coef_ref, *scratch, tt):
    bufs, m_ref = (scratch[:N_CHUNKS], scratch[N_CHUNKS:2 * N_CHUNKS]), scratch[-1]
    n_chunks = tt * N_CHUNKS
    row = lax.broadcasted_iota(jnp.int32, (SUBLANES, CHUNK_ROWS), 0)
    col = lax.broadcasted_iota(jnp.int32, (SUBLANES, CHUNK_ROWS), 1)
    quarter = (col % ROWS_PER_EXPERT) == (row % ROWS_PER_EXPERT)
    m_lo = jnp.logical_and(row < ROWS_PER_EXPERT, quarter)
    m_hi = jnp.logical_and(row >= ROWS_PER_EXPERT, quarter)
    pr = lax.broadcasted_iota(jnp.int32, (CHUNK_ROWS, LANES), 0) // ROWS_PER_EXPERT
    pc = lax.broadcasted_iota(jnp.int32, (CHUNK_ROWS, LANES), 1)
    place = [(pr + part * PAIR_CHUNK == pc).astype(BF16) for part in range(N_CHUNKS)]
    dn = (((1,), (1,)), ((), ()))
    gather = functools.partial(_gather_chunk, idx_ref, tab_ref)

    def chunk_dots(t, part, buf_ref):
        rows = pl.ds(pl.multiple_of(t * SUBLANES, SUBLANES), SUBLANES)
        x = h_ref[rows, :].astype(BF16)
        lo, hi = _unpack_words(buf_ref[...])
        b = jnp.concatenate([lo.astype(BF16), hi.astype(BF16)], axis=0)
        out = lax.dot_general(x, b, dn, preferred_element_type=F32)
        m = jnp.where(m_lo, out[:, :CHUNK_ROWS], 0.0) + jnp.where(m_hi, out[:, CHUNK_ROWS:], 0.0)
        m_ref[part, pl.ds(t, 1), :] = jnp.sum(m, axis=0, keepdims=True)

    for j in range(N_CHUNKS):
        gather(bufs[0][j], j)

    def body(i, carry):
        for s in range(TOKENS_PER_STEP):
            half = s % 2
            t = TOKENS_PER_STEP * i + s
            for j in range(N_CHUNKS):
                chunk_dots(t, j, bufs[half][j])
            for j in range(N_CHUNKS):
                gather(bufs[1 - half][j], jnp.minimum((t + 1) * N_CHUNKS + j, n_chunks - 1))
        return carry

    lax.fori_loop(0, tt // TOKENS_PER_STEP, body, 0)

    dots = None
    for part in range(N_CHUNKS):
        m = m_ref[part]
        m_top = m.astype(BF16)
        m_rest = (m - m_top.astype(F32)).astype(BF16)
        s = (jnp.dot(m_top, place[part], preferred_element_type=F32)
             + jnp.dot(m_rest, place[part], preferred_element_type=F32))
        dots = s if dots is None else dots + s
    coef_ref[...] = gate_ref[...] * _gelu(dots)


def _peer_u(idx4, h8, gates, tab, *, tt=128):
    n_tok = gates.shape[0]
    tt = min(tt, n_tok)
    assert TOKENS_PER_STEP % 2 == 0 and tt % TOKENS_PER_STEP == 0
    tok2 = pl.BlockSpec((tt, PEER_SEL), lambda i: (i, 0))
    return pl.pallas_call(
        functools.partial(_peer_u_kernel, tt=tt),
        grid=(n_tok // tt,),
        in_specs=[pl.BlockSpec((tt * PEER_SEL,), lambda i: (i,), memory_space=pltpu.SMEM),
                  pl.BlockSpec((tt * SUBLANES, LANES), lambda i: (i, 0)), tok2,
                  pl.BlockSpec(tab.shape, lambda i: (0, 0), pipeline_mode=pl.Buffered(1))],
        out_specs=tok2,
        out_shape=jax.ShapeDtypeStruct((n_tok, PEER_SEL), F32),
        scratch_shapes=[pltpu.VMEM((CHUNK_ROWS, LANES), jnp.int32)] * (2 * N_CHUNKS)
                       + [pltpu.VMEM((N_CHUNKS, tt, LANES), F32)],
        compiler_params=_cparams(("arbitrary",), VMEM_LIMIT),
        name="peer_u",
    )(idx4, h8, gates, tab)


def _peer_v_kernel(idx_ref, cx_ref, tab_ref, o_ref, *scratch, tt):
    bufs, parts_ref = (scratch[:N_CHUNKS], scratch[N_CHUNKS:2 * N_CHUNKS]), scratch[2 * N_CHUNKS]
    n_chunks = tt * N_CHUNKS
    row = lax.broadcasted_iota(jnp.int32, (SUBLANES, CHUNK_ROWS), 0)
    col = lax.broadcasted_iota(jnp.int32, (SUBLANES, CHUNK_ROWS), 1)
    quarter = (col % ROWS_PER_EXPERT) == (row % ROWS_PER_EXPERT)
    m_lo = jnp.logical_and(row < ROWS_PER_EXPERT, quarter)
    m_hi = jnp.logical_and(row >= ROWS_PER_EXPERT, quarter)
    gather = functools.partial(_gather_chunk, idx_ref, tab_ref)

    def chunk_sum(t, parity, part, buf_ref):
        cx = jnp.broadcast_to(cx_ref[t, pl.ds(part, 1), :], (SUBLANES, CHUNK_ROWS))
        a = jnp.concatenate([jnp.where(m_lo, cx, 0.0), jnp.where(m_hi, cx, 0.0)], axis=1).astype(BF16)
        lo, hi = _unpack_words(buf_ref[...])
        b = jnp.concatenate([lo.astype(BF16), hi.astype(BF16)], axis=0)
        parts_ref[parity, part] = jnp.dot(a, b, preferred_element_type=F32)

    parts_ref[...] = jnp.zeros_like(parts_ref)
    for j in range(N_CHUNKS):
        gather(bufs[0][j], j)

    def body(i, carry):
        for s in range(TOKENS_PER_STEP):
            half = s % 2
            t = TOKENS_PER_STEP * i + s
            prev = jnp.maximum(t - 1, 0)
            o_ref[prev] = _tree(jnp.add, [parts_ref[1 - half, j] for j in range(N_CHUNKS)])
            for j in range(N_CHUNKS):
                chunk_sum(t, half, j, bufs[half][j])
            for j in range(N_CHUNKS):
                gather(bufs[1 - half][j], jnp.minimum((t + 1) * N_CHUNKS + j, n_chunks - 1))
        return carry

    lax.fori_loop(0, tt // TOKENS_PER_STEP, body, 0)
    o_ref[tt - 1] = _tree(jnp.add, [parts_ref[1, j] for j in range(N_CHUNKS)])


def _peer_v(idx4, coef, tab, *, tt=128):
    n_tok = idx4.shape[0] // PEER_SEL
    tt = min(tt, n_tok)
    assert TOKENS_PER_STEP % 2 == 0 and tt % TOKENS_PER_STEP == 0
    cx = jnp.repeat(coef, ROWS_PER_EXPERT, axis=1).reshape(n_tok, N_CHUNKS, CHUNK_ROWS)
    smem = pl.BlockSpec((tt * PEER_SEL,), lambda i: (i,), memory_space=pltpu.SMEM)
    return pl.pallas_call(
        functools.partial(_peer_v_kernel, tt=tt),
        grid=(n_tok // tt,),
        in_specs=[smem, pl.BlockSpec((tt, N_CHUNKS, CHUNK_ROWS), lambda i: (i, 0, 0)),
                  pl.BlockSpec(tab.shape, lambda i: (0, 0), pipeline_mode=pl.Buffered(1))],
        out_specs=pl.BlockSpec((tt, SUBLANES, LANES), lambda i: (i, 0, 0)),
        out_shape=jax.ShapeDtypeStruct((n_tok, SUBLANES, LANES), F32),
        scratch_shapes=[pltpu.VMEM((CHUNK_ROWS, LANES), jnp.int32)] * (2 * N_CHUNKS)
                       + [pltpu.VMEM((2, N_CHUNKS, SUBLANES, LANES), F32)],
        compiler_params=_cparams(("arbitrary",), VMEM_LIMIT),
        name="peer_v",
    )(idx4, cx, tab)


def _peer(h8, qp, sub_keys, tab_u, tab_v):
    n_tok = qp.shape[1]
    half = sub_keys.shape[-1]
    z = jnp.zeros_like(sub_keys[0])
    keys_pad = jnp.stack([jnp.concatenate([sub_keys[0], z], axis=-1),
                          jnp.concatenate([z, sub_keys[1]], axis=-1)]).astype(BF16)
    assert keys_pad.shape[-1] == 2 * half == qp.shape[-1]
    idx4, gates = _peer_topk(qp, keys_pad)
    idx4 = idx4.reshape(n_tok * PEER_SEL)
    coef = _peer_u(idx4, h8, gates, tab_u)
    out = _peer_v(idx4, coef, tab_v)
    return out.reshape(n_tok * SUBLANES, LANES)


def _final_kernel(x_ref, p_ref, g_ref, o_ref):
    o_ref[...] = _rms(x_ref[...] + _rows_from_chunks(p_ref, x_ref.shape[0]), g_ref[...])


def _final_norm(x, p, g, *, tr=1024):
    rows, dm = x.shape
    tr = min(tr, rows)
    blk = pl.BlockSpec((tr, dm), lambda i: (i, 0))
    return pl.pallas_call(
        _final_kernel,
        grid=(rows // tr,),
        in_specs=[blk, pl.BlockSpec((tr * SUBLANES, LANES), lambda i: (i, 0)),
                  pl.BlockSpec((1, dm), lambda i: (0, 0))],
        out_specs=blk,
        out_shape=jax.ShapeDtypeStruct((rows, dm), F32),
        compiler_params=_cparams(("parallel",), VMEM_LIMIT),
        name="final_norm",
    )(x, p, g.reshape(1, dm))


def kernel(x, w_in, w_out, rel_bias, g_attn, g_ssm, norm_mix, norm_ffn, lam_re, lam_im, log_step, b_re, b_im, c_re, c_im, d_skip, w_glu, w_query, sub_keys, expert_u, expert_v, norm_final):
    bsz, seq, dm = x.shape
    depth = w_in.shape[0]
    prev = None
    biases = [_attn_bias_tables(rel_bias, d) for _, d in DILATED_PATTERNS]
    for l in range(depth):
        x, q, k, v, u_tm = _in_proj(x, prev, norm_mix[l], w_in[l].astype(BF16))
        attn = _attention(q, k, v, biases)
        a, bw, cw = _ssm_params(lam_re[l], lam_im[l], log_step[l], b_re[l], b_im[l], c_re[l], c_im[l])
        y = _ssm_scan(u_tm, a, bw, cw, bsz=bsz)
        wo = w_out[l].astype(BF16)
        z = _ssm_post(y, u_tm, d_skip[l], w_glu[l].astype(BF16), g_ssm[l], wo[ATTN_WIDTH:], bsz=bsz)
        x, h, qp = _mix_out(x, z, attn,
                            g_attn[l], wo[:ATTN_WIDTH], norm_ffn[l], w_query[l].astype(BF16))
        prev = _peer(h, qp, sub_keys[l], _pack_table(expert_u, l), _pack_table(expert_v, l))
    out = _final_norm(x.reshape(bsz * seq, dm), prev, norm_final)
    return out.reshape(bsz, seq, dm)
```

```python
import functools
import math

import numpy as np
import jax
import jax.numpy as jnp
from jax import lax
from jax.experimental import pallas as pl
from jax.experimental.pallas import tpu as pltpu

F32 = jnp.float32
BF16 = jnp.bfloat16

EPS = 1e-6
NEG_INF = -1e30
HEAD_DIM = 64
ATTN_WIDTH = 512
SSM_WIDTH = 512
SSM_GROUP = 16
SSM_STATE = 64
DILATED_PATTERNS = ((128, 1), (512, 4), (2048, 16))
REL_BUCKETS = 32
REL_MAX_DISTANCE = 1024
PEER_HEADS = 8
PEER_KEYS = 128
PEER_TOPK = 16
PEER_SEL = PEER_HEADS * PEER_TOPK

LANES = 128
SUBLANES = 8
QBLK = 128
KWIN = 256
BAND = 64
VMEM_LIMIT = 52 * 1024 * 1024


def _cparams(sem, vmem=None):
    return pltpu.CompilerParams(dimension_semantics=sem, vmem_limit_bytes=vmem)


def _rms(x, g):
    return x * lax.rsqrt(jnp.mean(x * x, axis=-1, keepdims=True) + EPS) * g


def _gelu(x):
    return 0.5 * x * (1.0 + lax.erf(x * (1.0 / math.sqrt(2.0))))


def _rows_from_chunks(p_ref, n_rows):
    return jnp.concatenate([p_ref[pl.ds(c, n_rows, stride=SUBLANES), :] for c in range(SUBLANES)], axis=-1)


def _in_proj_kernel(*refs, has_prev):
    if has_prev:
        x_ref, p_ref, g_ref, w_ref, xo_ref, q_ref, k_ref, v_ref, u_ref = refs
        x = x_ref[0] + _rows_from_chunks(p_ref, x_ref.shape[1])
    else:
        x_ref, g_ref, w_ref, xo_ref, q_ref, k_ref, v_ref, u_ref = refs
        x = x_ref[0]
    xo_ref[0] = x
    h = _rms(x, g_ref[...]).astype(BF16)
    proj = jnp.dot(h, w_ref[...], preferred_element_type=F32)
    a = ATTN_WIDTH
    q_ref[0] = proj[:, :a] * (HEAD_DIM ** -0.5)
    k_ref[0] = proj[:, a:2 * a]
    v_ref[0] = proj[:, 2 * a:3 * a]
    b, bsz = pl.program_id(1), pl.num_programs(1)
    for c in range(SSM_WIDTH // LANES):
        u_ref[c, pl.ds(b, proj.shape[0], stride=bsz), :] = proj[:, 3 * a + c * LANES:3 * a + (c + 1) * LANES]


def _in_proj(x, prev, g, w_bf16, *, ts=256):
    bsz, seq, dm = x.shape
    ts = min(ts, seq)
    row = pl.BlockSpec((1, ts, dm), lambda i, b: (b, i, 0))
    qkv = pl.BlockSpec((1, ts, ATTN_WIDTH), lambda i, b: (b, i, 0))
    ns = seq // ts
    chunks = pl.BlockSpec((ts * SUBLANES, LANES), lambda i, b: (b * ns + i, 0))
    ins = [x] + ([prev] if prev is not None else []) + [g.reshape(1, dm), w_bf16]
    in_specs = [row] + ([chunks] if prev is not None else []) + [
        pl.BlockSpec((1, dm), lambda i, b: (0, 0)),
        pl.BlockSpec(w_bf16.shape, lambda i, b: (0, 0)),
    ]
    slabs = SSM_WIDTH // LANES
    return pl.pallas_call(
        functools.partial(_in_proj_kernel, has_prev=prev is not None),
        grid=(seq // ts, bsz),
        in_specs=in_specs,
        out_specs=[row, qkv, qkv, qkv, pl.BlockSpec((slabs, ts * bsz, LANES), lambda i, b: (0, i, 0))],
        out_shape=[
            jax.ShapeDtypeStruct((bsz, seq, dm), F32),
            jax.ShapeDtypeStruct((bsz, seq, ATTN_WIDTH), F32),
            jax.ShapeDtypeStruct((bsz, seq, ATTN_WIDTH), F32),
            jax.ShapeDtypeStruct((bsz, seq, ATTN_WIDTH), F32),
            jax.ShapeDtypeStruct((slabs, seq * bsz, LANES), F32),
        ],
        compiler_params=_cparams(("parallel", "arbitrary"), VMEM_LIMIT),
        name="in_proj",
    )(*ins)


def _t5_buckets(rel):
    half = REL_BUCKETS // 2
    max_exact = half // 2
    n = np.abs(rel)
    large = max_exact + (np.log(np.maximum(n, 1) / max_exact)
                         / np.log(REL_MAX_DISTANCE / max_exact) * (half - max_exact)).astype(np.int32)
    large = np.minimum(large, half - 1)
    return (np.where(rel > 0, half, 0) + np.where(n < max_exact, n, large)).astype(np.int32)


def _attn_bias_tables(rel_bias, dilation):
    ql = np.arange(QBLK)[:, None]
    kl = np.arange(KWIN)[None, :]
    delta = np.stack([kl + off - ql for off in (0, -BAND, -2 * BAND)])
    buckets = np.where(np.abs(delta) <= BAND, _t5_buckets(delta * dilation), -1)
    rb = rel_bias.astype(F32).T
    bk = jnp.asarray(buckets, jnp.int32)[None]
    tab = jnp.full((rb.shape[0],) + buckets.shape, NEG_INF, F32)
    for b in range(REL_BUCKETS):
        tab = jnp.where(bk == b, rb[:, b][:, None, None, None], tab)
    return tab


BLOCKS_PER_STEP = 4


def _attn_kernel(q_ref, k_ref, v_ref, *rest, seq):
    bias_refs, (o_ref, acc_ref, m_ref, z_ref) = rest[:len(DILATED_PATTERNS)], rest[len(DILATED_PATTERNS):]
    lane = lax.broadcasted_iota(jnp.int32, (QBLK, LANES), 1)
    is_h0 = lane < HEAD_DIM
    dn = (((1,), (1,)), ((), ()))
    nsteps = seq // QBLK

    def rows(start, size, d):
        return pl.ds(start, size) if d == 1 else pl.ds(start, size, stride=d)

    def block(n, d, bias_ref, first, last):
        length = seq // d
        nblk = length // QBLK
        r, i = n >> (nblk.bit_length() - 1), n & (nblk - 1)
        s = i * QBLK
        ks = jnp.clip(s - BAND, 0, length - KWIN)
        var = jnp.where(i == 0, 0, jnp.where(i == nblk - 1, 2, 1))
        q_rows = rows(r + d * s, QBLK, d)
        k_rows = rows(r + d * ks, KWIN, d)
        qb = q_ref[q_rows, :].astype(BF16)
        kb = k_ref[k_rows, :].astype(BF16)
        vb = v_ref[k_rows, :].astype(BF16)
        outs, ms, zs = [], [], []
        for h in range(2):
            keep = is_h0 if h == 0 else jnp.logical_not(is_h0)
            qh = jnp.where(keep, qb, jnp.zeros_like(qb))
            logits = lax.dot_general(qh, kb, dn, preferred_element_type=F32) + bias_ref[h, var]
            m = jnp.max(logits, axis=-1, keepdims=True)
            p = jnp.exp(logits - m)
            outs.append(jnp.dot(p.astype(BF16), vb, preferred_element_type=F32))
            ms.append(jnp.broadcast_to(m, (QBLK, LANES)))
            zs.append(jnp.broadcast_to(jnp.sum(p, axis=-1, keepdims=True), (QBLK, LANES)))
        o = jnp.where(is_h0, outs[0], outs[1])
        m = jnp.where(is_h0, ms[0], ms[1])
        z = jnp.where(is_h0, zs[0], zs[1])
        if not first:
            m_old = m_ref[q_rows, :]
            m_new = jnp.maximum(m_old, m)
            a, b = jnp.exp(m_old - m_new), jnp.exp(m - m_new)
            o = acc_ref[q_rows, :] * a + o * b
            z = z_ref[q_rows, :] * a + z * b
            m = m_new
        if last:
            o_ref[q_rows, :] = o / z
        else:
            acc_ref[q_rows, :] = o
            m_ref[q_rows, :] = m
            z_ref[q_rows, :] = z

    for p, ((_, d), bias_ref) in enumerate(zip(DILATED_PATTERNS, bias_refs)):
        def step(g, carry, d=d, bias_ref=bias_ref, p=p):
            for j in range(BLOCKS_PER_STEP):
                block(g * BLOCKS_PER_STEP + j, d, bias_ref, p == 0, p == len(DILATED_PATTERNS) - 1)
            return carry
        lax.fori_loop(0, nsteps // BLOCKS_PER_STEP, step, 0)


def _attention(q, k, v, biases):
    bsz, seq, width = q.shape
    for _, d in DILATED_PATTERNS:
        length = seq // d
        assert length >= KWIN and length % QBLK == 0 and (length // QBLK) & (length // QBLK - 1) == 0
    assert (seq // QBLK) % BLOCKS_PER_STEP == 0
    blk = pl.BlockSpec((None, seq, LANES), lambda b, c: (b, 0, c))
    bias_spec = pl.BlockSpec((2, 3, QBLK, KWIN), lambda b, c: (c, 0, 0, 0))
    return pl.pallas_call(
        functools.partial(_attn_kernel, seq=seq),
        grid=(bsz, width // LANES),
        in_specs=[blk, blk, blk] + [bias_spec] * len(biases),
        out_specs=blk,
        out_shape=jax.ShapeDtypeStruct((bsz, seq, width), F32),
        scratch_shapes=[pltpu.VMEM((seq, LANES), F32)] * 3,
        compiler_params=_cparams(("parallel", "arbitrary"), VMEM_LIMIT),
        name="attention",
    )(q, k, v, *biases)


SSM_LANE_GROUPS = SSM_WIDTH // LANES
SSM_GB_STATES = (LANES // SSM_GROUP) * SSM_STATE


def _ssm_params(lam_re, lam_im, log_step, b_re, b_im, c_re, c_im):
    f = lambda t: t.astype(F32)
    lr, li = f(lam_re), f(lam_im)
    step = jnp.exp(f(log_step))[..., None]
    mag = jnp.exp(lr * step)
    ar, ai = mag * jnp.cos(li * step), mag * jnp.sin(li * step)
    nr, ni = ar - 1.0, ai
    den = lr * lr + li * li
    cr, ci = (nr * lr + ni * li) / den, (ni * lr - nr * li) / den
    br, bi = f(b_re), f(b_im)
    bbr = cr[..., None] * br - ci[..., None] * bi
    bbi = cr[..., None] * bi + ci[..., None] * br
    gpb = LANES // SSM_GROUP
    eye = jnp.eye(gpb, dtype=F32)

    def in_map(t):
        t = t.reshape(2, SSM_LANE_GROUPS, gpb, SSM_STATE, SSM_GROUP)
        return jnp.einsum('dbgpc,gh->dbgchp', t, eye).reshape(2, SSM_LANE_GROUPS, LANES, SSM_GB_STATES)

    def out_map(t):
        t = t.reshape(2, SSM_LANE_GROUPS, gpb, SSM_GROUP, SSM_STATE)
        return jnp.einsum('dbgcp,gh->dbgphc', t, eye).reshape(2, SSM_LANE_GROUPS, SSM_GB_STATES, LANES)

    bw = jnp.concatenate([in_map(bbr), in_map(bbi)], axis=-1).astype(BF16)
    cw = jnp.concatenate([out_map(f(c_re)), -out_map(f(c_im))], axis=-2).astype(BF16)
    a = jnp.stack([ar.reshape(2, SSM_LANE_GROUPS, SSM_GB_STATES),
                   ai.reshape(2, SSM_LANE_GROUPS, SSM_GB_STATES)], axis=2)
    return a, bw, cw


def _ssm_kernel(u_ref, a_ref, bw_ref, cw_ref, y_ref, st_ref, bu_ref, *, ts, bsz):
    d = pl.program_id(0)
    ns = SSM_GB_STATES

    @pl.when(pl.program_id(1) == 0)
    def _():
        st_ref[...] = jnp.zeros_like(st_ref)

    for gb in range(SSM_LANE_GROUPS):
        ub = u_ref[gb].astype(BF16)
        bu_ref[...] = jnp.dot(ub, bw_ref[0, gb], preferred_element_type=F32)
        ar = jnp.broadcast_to(a_ref[0, gb, 0:1, :], (bsz, ns))
        ai = jnp.broadcast_to(a_ref[0, gb, 1:2, :], (bsz, ns))

        def step(j, carry, ar=ar, ai=ai):
            xr, xi = carry
            tl = jnp.where(d == 0, j, ts - 1 - j)
            r = pl.multiple_of(tl * bsz, bsz)
            nr = ar * xr - ai * xi + bu_ref[pl.ds(r, bsz), :ns]
            ni = ar * xi + ai * xr + bu_ref[pl.ds(r, bsz), ns:]
            bu_ref[pl.ds(r, bsz), :ns] = nr
            bu_ref[pl.ds(r, bsz), ns:] = ni
            return nr, ni

        xr, xi = lax.fori_loop(0, ts, step, (st_ref[gb, :, :ns], st_ref[gb, :, ns:]))
        st_ref[gb, :, :ns] = xr
        st_ref[gb, :, ns:] = xi
        y_ref[0, :, gb * LANES:(gb + 1) * LANES] = jnp.dot(
            bu_ref[...].astype(BF16), cw_ref[0, gb], preferred_element_type=F32)


def _ssm_scan(u_tm, a, bw, cw, *, bsz, ts=64):
    slabs, rows, _ = u_tm.shape
    width = slabs * LANES
    seq = rows // bsz
    ts = min(ts, seq)
    nt = seq // ts
    tblk = lambda d, i: jnp.where(d == 0, i, nt - 1 - i)
    return pl.pallas_call(
        functools.partial(_ssm_kernel, ts=ts, bsz=bsz),
        grid=(2, nt),
        in_specs=[
            pl.BlockSpec((slabs, ts * bsz, LANES), lambda d, i: (0, tblk(d, i), 0)),
            pl.BlockSpec((1,) + a.shape[1:], lambda d, i: (d, 0, 0, 0)),
            pl.BlockSpec((1,) + bw.shape[1:], lambda d, i: (d, 0, 0, 0)),
            pl.BlockSpec((1,) + cw.shape[1:], lambda d, i: (d, 0, 0, 0)),
        ],
        out_specs=pl.BlockSpec((1, ts * bsz, width), lambda d, i: (d, tblk(d, i), 0)),
        out_shape=jax.ShapeDtypeStruct((2, rows, width), F32),
        scratch_shapes=[pltpu.VMEM((SSM_LANE_GROUPS, bsz, 2 * SSM_GB_STATES), F32),
                        pltpu.VMEM((ts * bsz, 2 * SSM_GB_STATES), F32)],
        compiler_params=_cparams(("arbitrary", "arbitrary"), VMEM_LIMIT),
        name="ssm_scan",
    )(u_tm, a, bw, cw)


def _ssm_post_kernel(y_ref, u_ref, d_ref, wg_ref, g_ref, wo_ref, z_ref, slab_ref, *, bsz):
    u = jnp.concatenate([u_ref[c] for c in range(u_ref.shape[0])], axis=-1)
    y = _gelu(y_ref[0] + y_ref[1] + d_ref[...] * u).astype(BF16)
    ab = jnp.dot(y, wg_ref[...], preferred_element_type=F32)
    ssm = ab[:, :SSM_WIDTH] * jax.nn.sigmoid(ab[:, SSM_WIDTH:])
    n = _rms(ssm, g_ref[...]).astype(BF16)
    z = jnp.dot(n, wo_ref[...], preferred_element_type=F32)
    nt, dm = z.shape[0] // bsz, z.shape[1]
    for c in range(dm // LANES):
        slab_ref[c] = z[:, c * LANES:(c + 1) * LANES]
    for b in range(bsz):
        for c in range(dm // LANES):
            z_ref[:, b * dm + c * LANES:b * dm + (c + 1) * LANES] = slab_ref[c, pl.ds(b, nt, stride=bsz), :]


def _ssm_post(y, u_tm, d_skip, w_glu_bf16, g_ssm, w_out_ssm_bf16, *, bsz, tr=512):
    slabs, rows, _ = u_tm.shape
    width = slabs * LANES
    tr = min(tr, rows)
    assert tr % bsz == 0
    dm = w_out_ssm_bf16.shape[1]
    full = lambda a: pl.BlockSpec(a.shape, lambda i: (0,) * a.ndim)
    d2, g2 = d_skip.reshape(1, width), g_ssm.reshape(1, width)
    return pl.pallas_call(
        functools.partial(_ssm_post_kernel, bsz=bsz),
        grid=(rows // tr,),
        in_specs=[pl.BlockSpec((2, tr, width), lambda i: (0, i, 0)),
                  pl.BlockSpec((slabs, tr, LANES), lambda i: (0, i, 0)),
                  full(d2), full(w_glu_bf16), full(g2), full(w_out_ssm_bf16)],
        out_specs=pl.BlockSpec((tr // bsz, bsz * dm), lambda i: (i, 0)),
        out_shape=jax.ShapeDtypeStruct((rows // bsz, bsz * dm), F32),
        scratch_shapes=[pltpu.VMEM((dm // LANES, tr, LANES), F32)],
        compiler_params=_cparams(("parallel",), VMEM_LIMIT),
        name="ssm_post",
    )(y, u_tm, d2, w_glu_bf16, g2, w_out_ssm_bf16)


def _mix_out_kernel(x_ref, z_ref, a_ref, ga_ref, wo_ref, gf_ref, wq_ref, xn_ref, h_ref, q_ref):
    n = _rms(a_ref[0], ga_ref[...]).astype(BF16)
    xn = x_ref[0] + z_ref[...] + jnp.dot(n, wo_ref[...], preferred_element_type=F32)
    xn_ref[0] = xn
    h = _rms(xn, gf_ref[...])
    for c in range(SUBLANES):
        h_ref[pl.ds(c, h.shape[0], stride=SUBLANES), :] = h[:, c * LANES:(c + 1) * LANES]
    qp = jnp.dot(h.astype(BF16), wq_ref[...], preferred_element_type=F32)
    for hd in range(PEER_HEADS):
        q_ref[hd] = qp[:, hd * LANES:(hd + 1) * LANES]


def _mix_out(x, z_tm, attn, g_attn, w_out_attn_bf16, norm_ffn, w_query, *, ts=256):
    bsz, seq, dm = x.shape
    ts = min(ts, seq)
    ns = seq // ts
    row = pl.BlockSpec((1, ts, dm), lambda b, i: (b, i, 0))
    half = pl.BlockSpec((1, ts, ATTN_WIDTH), lambda b, i: (b, i, 0))
    full = lambda a: pl.BlockSpec(a.shape, lambda b, i: (0,) * a.ndim)
    ga, gf = g_attn.reshape(1, ATTN_WIDTH), norm_ffn.reshape(1, dm)
    qdim = w_query.shape[1] // PEER_HEADS
    return pl.pallas_call(
        _mix_out_kernel,
        grid=(bsz, ns),
        in_specs=[row, pl.BlockSpec((ts, dm), lambda b, i: (i, b)), half,
                  full(ga), full(w_out_attn_bf16), full(gf), full(w_query)],
        out_specs=[row, pl.BlockSpec((ts * SUBLANES, LANES), lambda b, i: (b * ns + i, 0)),
                   pl.BlockSpec((PEER_HEADS, ts, qdim), lambda b, i: (0, b * ns + i, 0))],
        out_shape=[jax.ShapeDtypeStruct((bsz, seq, dm), F32),
                   jax.ShapeDtypeStruct((bsz * seq * SUBLANES, LANES), F32),

                   jax.ShapeDtypeStruct((PEER_HEADS, bsz * seq, qdim), F32)],
        compiler_params=_cparams(("parallel", "arbitrary"), VMEM_LIMIT),
        name="mix_out",
    )(x, z_tm, attn, ga, w_out_attn_bf16, gf, w_query)


TOPK_TOKENS = SUBLANES * LANES
KEY_PITCH = PEER_KEYS + 4
_CANDIDATES = tuple((a, b) for a in range(PEER_TOPK) for b in range(PEER_TOPK) if (a + 1) * (b + 1) <= PEER_TOPK)


def _tree(op, xs):
    xs = list(xs)
    while len(xs) > 1:
        xs = [op(xs[i], xs[i + 1]) if i + 1 < len(xs) else xs[i] for i in range(0, len(xs), 2)]
    return xs[0]


def _extract16(problems):
    ninf = jnp.float32(-jnp.inf)

    def better(a, b):
        gt = b[0] > a[0]
        return tuple(jnp.where(gt, y, x) for x, y in zip(a, b))

    def step(r, carry):
        for p in problems:
            s_ref, order = p["s"], p["order"]
            assert list(order) == sorted(order)
            n = len(order)
            rows = [(s_ref[k], order[k]) + ((p["pay"][k],) if p.get("pay") is not None else ())
                    for k in range(n)]
            win = _tree(better, rows)
            m, am = win[0], win[1]
            for k in range(n):
                s_ref[k] = jnp.where(am == order[k], ninf, s_ref[k])
            p["vals"][r] = m
            p["picks"][r] = win[2] if len(win) > 2 else am.astype(jnp.int32)
        return carry

    lax.fori_loop(0, PEER_TOPK, step, 0)


def _peer_topk_kernel(q_ref, k_ref, idx_ref, gate_ref,
                      slab_ref, s1_ref, s2_ref, t1_ref, i1_ref, t2_ref, i2_ref, cand_ref, pay_ref, ts_ref, ex_ref):
    dn = (((1,), (1,)), ((), ()))
    keys = tuple(range(PEER_KEYS))

    def head(h, carry):
        for w, s_ref in ((0, s1_ref), (1, s2_ref)):
            for j in range(SUBLANES):
                slab_ref[j * KEY_PITCH:j * KEY_PITCH + PEER_KEYS, :] = lax.dot_general(
                    k_ref[w, h], q_ref[h, j * LANES:(j + 1) * LANES, :].astype(BF16), dn,
                    preferred_element_type=F32)
            for k in range(PEER_KEYS):
                s_ref[k] = slab_ref[pl.ds(k, SUBLANES, stride=KEY_PITCH), :]
        _extract16([dict(s=s1_ref, order=keys, vals=t1_ref, picks=i1_ref),
                    dict(s=s2_ref, order=keys, vals=t2_ref, picks=i2_ref)])
        for c, (a, b) in enumerate(_CANDIDATES):
            cand_ref[c] = t1_ref[a] + t2_ref[b]
            pay_ref[c] = i1_ref[a] * PEER_KEYS + i2_ref[b]
        _extract16([dict(s=cand_ref, order=tuple(a * PEER_TOPK + b for a, b in _CANDIDATES), pay=pay_ref,
                         vals=ts_ref, picks=ex_ref)])
        top_s = ts_ref[...]
        e = jnp.exp(top_s - jnp.max(top_s, axis=0, keepdims=True))
        gate_ref[0, h] = e / jnp.sum(e, axis=0, keepdims=True)
        idx_ref[0, h] = ex_ref[...] * ROWS_PER_EXPERT
        return carry

    lax.fori_loop(0, PEER_HEADS, head, 0)


def _peer_topk(qp, keys_pad):
    n_tok = qp.shape[1]
    tt = TOPK_TOKENS
    assert n_tok % tt == 0
    shp = (n_tok // tt, PEER_HEADS, PEER_TOPK, SUBLANES, LANES)
    out = pl.BlockSpec((1,) + shp[1:], lambda i: (i, 0, 0, 0, 0))
    vregs = lambda n, dt: pltpu.VMEM((n, SUBLANES, LANES), dt)
    idx, gate = pl.pallas_call(
        _peer_topk_kernel,
        grid=(n_tok // tt,),
        in_specs=[pl.BlockSpec((PEER_HEADS, tt, qp.shape[2]), lambda i: (0, i, 0)),
                  pl.BlockSpec(keys_pad.shape, lambda i: (0, 0, 0, 0))],
        out_specs=[out, out],
        out_shape=[jax.ShapeDtypeStruct(shp, jnp.int32), jax.ShapeDtypeStruct(shp, F32)],
        scratch_shapes=[pltpu.VMEM((SUBLANES * KEY_PITCH, LANES), F32),
                        vregs(PEER_KEYS, F32), vregs(PEER_KEYS, F32),
                        vregs(PEER_TOPK, F32), vregs(PEER_TOPK, jnp.int32),
                        vregs(PEER_TOPK, F32), vregs(PEER_TOPK, jnp.int32),
                        vregs(len(_CANDIDATES), F32), vregs(len(_CANDIDATES), jnp.int32),
                        vregs(PEER_TOPK, F32), vregs(PEER_TOPK, jnp.int32)],
        compiler_params=_cparams(("parallel",), VMEM_LIMIT),
        name="peer_topk",
    )(qp, keys_pad)
    to_tok = lambda a: jnp.transpose(a, (0, 3, 4, 1, 2)).reshape(n_tok, PEER_SEL)
    return to_tok(idx), to_tok(gate)


ROWS_PER_EXPERT = 4
PAIR_CHUNK = 32
CHUNK_ROWS = PAIR_CHUNK * ROWS_PER_EXPERT
SMEM_GROUP = 8
N_CHUNKS = PEER_SEL // PAIR_CHUNK
TOKENS_PER_STEP = 16


def _pack_table_kernel(t_ref, o_ref):
    rows, dm = t_ref.shape
    for c in range(ROWS_PER_EXPERT):
        lo = lax.bitcast_convert_type(t_ref[:, c * LANES:(c + 1) * LANES].astype(BF16).astype(F32), jnp.int32)
        hi = lax.bitcast_convert_type(
            t_ref[:, dm // 2 + c * LANES:dm // 2 + (c + 1) * LANES].astype(BF16).astype(F32), jnp.int32)
        o_ref[pl.ds(c, rows, stride=ROWS_PER_EXPERT), :] = lax.shift_right_logical(lo, 16) | hi


def _pack_table(tables, layer, *, rows=512):
    _, e, dm = tables.shape
    assert dm == 2 * ROWS_PER_EXPERT * LANES
    rows = min(rows, e)
    return pl.pallas_call(
        _pack_table_kernel,
        grid=(e // rows,),
        in_specs=[pl.BlockSpec((None, rows, dm), lambda i: (layer, i, 0))],
        out_specs=pl.BlockSpec((rows * ROWS_PER_EXPERT, LANES), lambda i: (i, 0)),
        out_shape=jax.ShapeDtypeStruct((e * ROWS_PER_EXPERT, LANES), jnp.int32),
        compiler_params=_cparams(("parallel",), VMEM_LIMIT),
        name="pack_table",
    )(tables)


def _unpack_words(w):
    lo = lax.bitcast_convert_type(w << 16, F32)
    hi = lax.bitcast_convert_type(w & jnp.int32(-65536), F32)
    return lo, hi


def _gather_chunk(idx_ref, tab_ref, buf_ref, c):
    for g in range(PAIR_CHUNK // SMEM_GROUP):
        ids = idx_ref.at[pl.ds(pl.multiple_of(c * PAIR_CHUNK + g * SMEM_GROUP, SMEM_GROUP), SMEM_GROUP)]
        for i in range(SMEM_GROUP):
            e4 = pl.multiple_of(ids[i], ROWS_PER_EXPERT)
            s = (g * SMEM_GROUP + i) * ROWS_PER_EXPERT
            buf_ref[s:s + ROWS_PER_EXPERT, :] = tab_ref[pl.ds(e4, ROWS_PER_EXPERT), :]


def _peer_u_kernel(idx_ref, h_ref, gate_ref, tab_ref, coef_ref, *scratch, tt):
    bufs, m_ref = (scratch[:N_CHUNKS], scratch[N_CHUNKS:2 * N_CHUNKS]), scratch[-1]
    n_chunks = tt * N_CHUNKS
    row = lax.broadcasted_iota(jnp.int32, (SUBLANES, CHUNK_ROWS), 0)
    col = lax.broadcasted_iota(jnp.int32, (SUBLANES, CHUNK_ROWS), 1)
    quarter = (col % ROWS_PER_EXPERT) == (row % ROWS_PER_EXPERT)
    m_lo = jnp.logical_and(row < ROWS_PER_EXPERT, quarter)
    m_hi = jnp.logical_and(row >= ROWS_PER_EXPERT, quarter)
    pr = lax.broadcasted_iota(jnp.int32, (CHUNK_ROWS, LANES), 0) // ROWS_PER_EXPERT
    pc = lax.broadcasted_iota(jnp.int32, (CHUNK_ROWS, LANES), 1)
    place = [(pr + part * PAIR_CHUNK == pc).astype(BF16) for part in range(N_CHUNKS)]
    dn = (((1,), (1,)), ((), ()))
    gather = functools.partial(_gather_chunk, idx_ref, tab_ref)

    def chunk_dots(t, part, buf_ref):
        rows = pl.ds(pl.multiple_of(t * SUBLANES, SUBLANES), SUBLANES)
        x = h_ref[rows, :].astype(BF16)
        lo, hi = _unpack_words(buf_ref[...])
        b = jnp.concatenate([lo.astype(BF16), hi.astype(BF16)], axis=0)
        out = lax.dot_general(x, b, dn, preferred_element_type=F32)
        m = jnp.where(m_lo, out[:, :CHUNK_ROWS], 0.0) + jnp.where(m_hi, out[:, CHUNK_ROWS:], 0.0)
        m_ref[part, pl.ds(t, 1), :] = jnp.sum(m, axis=0, keepdims=True)

    for j in range(N_CHUNKS):
        gather(bufs[0][j], j)

    def body(i, carry):
        for s in range(TOKENS_PER_STEP):
            half = s % 2
            t = TOKENS_PER_STEP * i + s
            for j in range(N_CHUNKS):
                chunk_dots(t, j, bufs[half][j])
            for j in range(N_CHUNKS):
                gather(bufs[1 - half][j], jnp.minimum((t + 1) * N_CHUNKS + j, n_chunks - 1))
        return carry

    lax.fori_loop(0, tt // TOKENS_PER_STEP, body, 0)

    dots = None
    for part in range(N_CHUNKS):
        m = m_ref[part]
        m_top = m.astype(BF16)
        m_rest = (m - m_top.astype(F32)).astype(BF16)
        s = (jnp.dot(m_top, place[part], preferred_element_type=F32)
             + jnp.dot(m_rest, place[part], preferred_element_type=F32))
        dots = s if dots is None else dots + s
    coef_ref[...] = gate_ref[...] * _gelu(dots)


def _peer_u(idx4, h8, gates, tab, *, tt=128):
    n_tok = gates.shape[0]
    tt = min(tt, n_tok)
    assert TOKENS_PER_STEP % 2 == 0 and tt % TOKENS_PER_STEP == 0
    tok2 = pl.BlockSpec((tt, PEER_SEL), lambda i: (i, 0))
    return pl.pallas_call(
        functools.partial(_peer_u_kernel, tt=tt),
        grid=(n_tok // tt,),
        in_specs=[pl.BlockSpec((tt * PEER_SEL,), lambda i: (i,), memory_space=pltpu.SMEM),
                  pl.BlockSpec((tt * SUBLANES, LANES), lambda i: (i, 0)), tok2,
                  pl.BlockSpec(tab.shape, lambda i: (0, 0), pipeline_mode=pl.Buffered(1))],
        out_specs=tok2,
        out_shape=jax.ShapeDtypeStruct((n_tok, PEER_SEL), F32),
        scratch_shapes=[pltpu.VMEM((CHUNK_ROWS, LANES), jnp.int32)] * (2 * N_CHUNKS)
                       + [pltpu.VMEM((N_CHUNKS, tt, LANES), F32)],
        compiler_params=_cparams(("arbitrary",), VMEM_LIMIT),
        name="peer_u",
    )(idx4, h8, gates, tab)


def _peer_v_kernel(idx_ref, cx_ref, tab_ref, o_ref, *scratch, tt):
    bufs, parts_ref = (scratch[:N_CHUNKS], scratch[N_CHUNKS:2 * N_CHUNKS]), scratch[2 * N_CHUNKS]
    n_chunks = tt * N_CHUNKS
    row = lax.broadcasted_iota(jnp.int32, (SUBLANES, CHUNK_ROWS), 0)
    col = lax.broadcasted_iota(jnp.int32, (SUBLANES, CHUNK_ROWS), 1)
    quarter = (col % ROWS_PER_EXPERT) == (row % ROWS_PER_EXPERT)
    m_lo = jnp.logical_and(row < ROWS_PER_EXPERT, quarter)
    m_hi = jnp.logical_and(row >= ROWS_PER_EXPERT, quarter)
    gather = functools.partial(_gather_chunk, idx_ref, tab_ref)

    def chunk_sum(t, parity, part, buf_ref):
        cx = jnp.broadcast_to(cx_ref[t, pl.ds(part, 1), :], (SUBLANES, CHUNK_ROWS))
        a = jnp.concatenate([jnp.where(m_lo, cx, 0.0), jnp.where(m_hi, cx, 0.0)], axis=1).astype(BF16)
        lo, hi = _unpack_words(buf_ref[...])
        b = jnp.concatenate([lo.astype(BF16), hi.astype(BF16)], axis=0)
        parts_ref[parity, part] = jnp.dot(a, b, preferred_element_type=F32)

    parts_ref[...] = jnp.zeros_like(parts_ref)
    for j in range(N_CHUNKS):
        gather(bufs[0][j], j)

    def body(i, carry):
        for s in range(TOKENS_PER_STEP):
            half = s % 2
            t = TOKENS_PER_STEP * i + s
            prev = jnp.maximum(t - 1, 0)
            o_ref[prev] = _tree(jnp.add, [parts_ref[1 - half, j] for j in range(N_CHUNKS)])
            for j in range(N_CHUNKS):
                chunk_sum(t, half, j, bufs[half][j])
            for j in range(N_CHUNKS):
                gather(bufs[1 - half][j], jnp.minimum((t + 1) * N_CHUNKS + j, n_chunks - 1))
        return carry

    lax.fori_loop(0, tt // TOKENS_PER_STEP, body, 0)
    o_ref[tt - 1] = _tree(jnp.add, [parts_ref[1, j] for j in range(N_CHUNKS)])


def _peer_v(idx4, coef, tab, *, tt=128):
    n_tok = idx4.shape[0] // PEER_SEL
    tt = min(tt, n_tok)
    assert TOKENS_PER_STEP % 2 == 0 and tt % TOKENS_PER_STEP == 0
    cx = jnp.repeat(coef, ROWS_PER_EXPERT, axis=1).reshape(n_tok, N_CHUNKS, CHUNK_ROWS)
    smem = pl.BlockSpec((tt * PEER_SEL,), lambda i: (i,), memory_space=pltpu.SMEM)
    return pl.pallas_call(
        functools.partial(_peer_v_kernel, tt=tt),
        grid=(n_tok // tt,),
        in_specs=[smem, pl.BlockSpec((tt, N_CHUNKS, CHUNK_ROWS), lambda i: (i, 0, 0)),
                  pl.BlockSpec(tab.shape, lambda i: (0, 0), pipeline_mode=pl.Buffered(1))],
        out_specs=pl.BlockSpec((tt, SUBLANES, LANES), lambda i: (i, 0, 0)),
        out_shape=jax.ShapeDtypeStruct((n_tok, SUBLANES, LANES), F32),
        scratch_shapes=[pltpu.VMEM((CHUNK_ROWS, LANES), jnp.int32)] * (2 * N_CHUNKS)
                       + [pltpu.VMEM((2, N_CHUNKS, SUBLANES, LANES), F32)],
        compiler_params=_cparams(("arbitrary",), VMEM_LIMIT),
        name="peer_v",
    )(idx4, cx, tab)


def _peer(h8, qp, sub_keys, tab_u, tab_v):
    n_tok = qp.shape[1]
    half = sub_keys.shape[-1]
    z = jnp.zeros_like(sub_keys[0])
    keys_pad = jnp.stack([jnp.concatenate([sub_keys[0], z], axis=-1),
                          jnp.concatenate([z, sub_keys[1]], axis=-1)]).astype(BF16)
    assert keys_pad.shape[-1] == 2 * half == qp.shape[-1]
    idx4, gates = _peer_topk(qp, keys_pad)
    idx4 = idx4.reshape(n_tok * PEER_SEL)
    coef = _peer_u(idx4, h8, gates, tab_u)
    out = _peer_v(idx4, coef, tab_v)
    return out.reshape(n_tok * SUBLANES, LANES)


def _final_kernel(x_ref, p_ref, g_ref, o_ref):
    o_ref[...] = _rms(x_ref[...] + _rows_from_chunks(p_ref, x_ref.shape[0]), g_ref[...])


def _final_norm(x, p, g, *, tr=1024):
    rows, dm = x.shape
    tr = min(tr, rows)
    blk = pl.BlockSpec((tr, dm), lambda i: (i, 0))
    return pl.pallas_call(
        _final_kernel,
        grid=(rows // tr,),
        in_specs=[blk, pl.BlockSpec((tr * SUBLANES, LANES), lambda i: (i, 0)),
                  pl.BlockSpec((1, dm), lambda i: (0, 0))],
        out_specs=blk,
        out_shape=jax.ShapeDtypeStruct((rows, dm), F32),
        compiler_params=_cparams(("parallel",), VMEM_LIMIT),
        name="final_norm",
    )(x, p, g.reshape(1, dm))


def kernel(x, w_in, w_out, rel_bias, g_attn, g_ssm, norm_mix, norm_ffn, lam_re, lam_im, log_step, b_re, b_im, c_re, c_im, d_skip, w_glu, w_query, sub_keys, expert_u, expert_v, norm_final):
    bsz, seq, dm = x.shape
    depth = w_in.shape[0]
    prev = None
    biases = [_attn_bias_tables(rel_bias, d) for _, d in DILATED_PATTERNS]
    for l in range(depth):
        x, q, k, v, u_tm = _in_proj(x, prev, norm_mix[l], w_in[l].astype(BF16))
        attn = _attention(q, k, v, biases)
        a, bw, cw = _ssm_params(lam_re[l], lam_im[l], log_step[l], b_re[l], b_im[l], c_re[l], c_im[l])
        y = _ssm_scan(u_tm, a, bw, cw, bsz=bsz)
        wo = w_out[l].astype(BF16)
        z = _ssm_post(y, u_tm, d_skip[l], w_glu[l].astype(BF16), g_ssm[l], wo[ATTN_WIDTH:], bsz=bsz)
        x, h, qp = _mix_out(x, z, attn,
                            g_attn[l], wo[:ATTN_WIDTH], norm_ffn[l], w_query[l].astype(BF16))
        prev = _peer(h, qp, sub_keys[l], _pack_table(expert_u, l), _pack_table(expert_v, l))
    out = _final_norm(x.reshape(bsz * seq, dm), prev, norm_final)
    return out.reshape(bsz, seq, dm)
```

```python
import functools
import math

import numpy as np
import jax
import jax.numpy as jnp
from jax import lax
from jax.experimental import pallas as pl
from jax.experimental.pallas import tpu as pltpu

F32 = jnp.float32
BF16 = jnp.bfloat16

EPS = 1e-6
NEG_INF = -1e30
HEAD_DIM = 64
ATTN_WIDTH = 512
SSM_WIDTH = 512
SSM_GROUP = 16
SSM_STATE = 64
DILATED_PATTERNS = ((128, 1), (512, 4), (2048, 16))
REL_BUCKETS = 32
REL_MAX_DISTANCE = 1024
PEER_HEADS = 8
PEER_KEYS = 128
PEER_TOPK = 16
PEER_SEL = PEER_HEADS * PEER_TOPK

LANES = 128
SUBLANES = 8
QBLK = 128
KWIN = 256
BAND = 64
VMEM_LIMIT = 52 * 1024 * 1024


def _cparams(sem, vmem=None):
    return pltpu.CompilerParams(dimension_semantics=sem, vmem_limit_bytes=vmem)


def _rms(x, g):
    return x * lax.rsqrt(jnp.mean(x * x, axis=-1, keepdims=True) + EPS) * g


def _gelu(x):
    return 0.5 * x * (1.0 + lax.erf(x * (1.0 / math.sqrt(2.0))))


def _rows_from_chunks(p_ref, n_rows):
    return jnp.concatenate([p_ref[pl.ds(c, n_rows, stride=SUBLANES), :] for c in range(SUBLANES)], axis=-1)


def _in_proj_kernel(*refs, has_prev):
    if has_prev:
        x_ref, p_ref, g_ref, w_ref, xo_ref, q_ref, k_ref, v_ref, u_ref = refs
        x = x_ref[0] + _rows_from_chunks(p_ref, x_ref.shape[1])
    else:
        x_ref, g_ref, w_ref, xo_ref, q_ref, k_ref, v_ref, u_ref = refs
        x = x_ref[0]
    xo_ref[0] = x
    h = _rms(x, g_ref[...]).astype(BF16)
    proj = jnp.dot(h, w_ref[...], preferred_element_type=F32)
    a = ATTN_WIDTH
    q_ref[0] = proj[:, :a] * (HEAD_DIM ** -0.5)
    k_ref[0] = proj[:, a:2 * a]
    v_ref[0] = proj[:, 2 * a:3 * a]
    b, bsz = pl.program_id(1), pl.num_programs(1)
    for c in range(SSM_WIDTH // LANES):
        u_ref[c, pl.ds(b, proj.shape[0], stride=bsz), :] = proj[:, 3 * a + c * LANES:3 * a + (c + 1) * LANES]


def _in_proj(x, prev, g, w_bf16, *, ts=256):
    bsz, seq, dm = x.shape
    ts = min(ts, seq)
    row = pl.BlockSpec((1, ts, dm), lambda i, b: (b, i, 0))
    qkv = pl.BlockSpec((1, ts, ATTN_WIDTH), lambda i, b: (b, i, 0))
    ns = seq // ts
    chunks = pl.BlockSpec((ts * SUBLANES, LANES), lambda i, b: (b * ns + i, 0))
    ins = [x] + ([prev] if prev is not None else []) + [g.reshape(1, dm), w_bf16]
    in_specs = [row] + ([chunks] if prev is not None else []) + [
        pl.BlockSpec((1, dm), lambda i, b: (0, 0)),
        pl.BlockSpec(w_bf16.shape, lambda i, b: (0, 0)),
    ]
    slabs = SSM_WIDTH // LANES
    return pl.pallas_call(
        functools.partial(_in_proj_kernel, has_prev=prev is not None),
        grid=(seq // ts, bsz),
        in_specs=in_specs,
        out_specs=[row, qkv, qkv, qkv, pl.BlockSpec((slabs, ts * bsz, LANES), lambda i, b: (0, i, 0))],
        out_shape=[
            jax.ShapeDtypeStruct((bsz, seq, dm), F32),
            jax.ShapeDtypeStruct((bsz, seq, ATTN_WIDTH), F32),
            jax.ShapeDtypeStruct((bsz, seq, ATTN_WIDTH), F32),
            jax.ShapeDtypeStruct((bsz, seq, ATTN_WIDTH), F32),
            jax.ShapeDtypeStruct((slabs, seq * bsz, LANES), F32),
        ],
        compiler_params=_cparams(("parallel", "arbitrary"), VMEM_LIMIT),
        name="in_proj",
    )(*ins)


def _t5_buckets(rel):
    half = REL_BUCKETS // 2
    max_exact = half // 2
    n = np.abs(rel)
    large = max_exact + (np.log(np.maximum(n, 1) / max_exact)
                         / np.log(REL_MAX_DISTANCE / max_exact) * (half - max_exact)).astype(np.int32)
    large = np.minimum(large, half - 1)
    return (np.where(rel > 0, half, 0) + np.where(n < max_exact, n, large)).astype(np.int32)


def _attn_bias_tables(rel_bias, dilation):
    ql = np.arange(QBLK)[:, None]
    kl = np.arange(KWIN)[None, :]
    delta = np.stack([kl + off - ql for off in (0, -BAND, -2 * BAND)])
    buckets = np.where(np.abs(delta) <= BAND, _t5_buckets(delta * dilation), -1)
    rb = rel_bias.astype(F32).T
    bk = jnp.asarray(buckets, jnp.int32)[None]
    tab = jnp.full((rb.shape[0],) + buckets.shape, NEG_INF, F32)
    for b in range(REL_BUCKETS):
        tab = jnp.where(bk == b, rb[:, b][:, None, None, None], tab)
    return tab


BLOCKS_PER_STEP = 4


def _attn_kernel(q_ref, k_ref, v_ref, *rest, seq):
    bias_refs, (o_ref, acc_ref, m_ref, z_ref) = rest[:len(DILATED_PATTERNS)], rest[len(DILATED_PATTERNS):]
    lane = lax.broadcasted_iota(jnp.int32, (QBLK, LANES), 1)
    is_h0 = lane < HEAD_DIM
    dn = (((1,), (1,)), ((), ()))
    nsteps = seq // QBLK

    def rows(start, size, d):
        return pl.ds(start, size) if d == 1 else pl.ds(start, size, stride=d)

    def block(n, d, bias_ref, first, last):
        length = seq // d
        nblk = length // QBLK
        r, i = n >> (nblk.bit_length() - 1), n & (nblk - 1)
        s = i * QBLK
        ks = jnp.clip(s - BAND, 0, length - KWIN)
        var = jnp.where(i == 0, 0, jnp.where(i == nblk - 1, 2, 1))
        q_rows = rows(r + d * s, QBLK, d)
        k_rows = rows(r + d * ks, KWIN, d)
        qb = q_ref[q_rows, :].astype(BF16)
        kb = k_ref[k_rows, :].astype(BF16)
        vb = v_ref[k_rows, :].astype(BF16)
        outs, ms, zs = [], [], []
        for h in range(2):
            keep = is_h0 if h == 0 else jnp.logical_not(is_h0)
            qh = jnp.where(keep, qb, jnp.zeros_like(qb))
            logits = lax.dot_general(qh, kb, dn, preferred_element_type=F32) + bias_ref[h, var]
            m = jnp.max(logits, axis=-1, keepdims=True)
            p = jnp.exp(logits - m)
            outs.append(jnp.dot(p.astype(BF16), vb, preferred_element_type=F32))
            ms.append(jnp.broadcast_to(m, (QBLK, LANES)))
            zs.append(jnp.broadcast_to(jnp.sum(p, axis=-1, keepdims=True), (QBLK, LANES)))
        o = jnp.where(is_h0, outs[0], outs[1])
        m = jnp.where(is_h0, ms[0], ms[1])
        z = jnp.where(is_h0, zs[0], zs[1])
        if not first:
            m_old = m_ref[q_rows, :]
            m_new = jnp.maximum(m_old, m)
            a, b = jnp.exp(m_old - m_new), jnp.exp(m - m_new)
            o = acc_ref[q_rows, :] * a + o * b
            z = z_ref[q_rows, :] * a + z * b
            m = m_new
        if last:
            o_ref[q_rows, :] = o / z
        else:
            acc_ref[q_rows, :] = o
            m_ref[q_rows, :] = m
            z_ref[q_rows, :] = z

    for p, ((_, d), bias_ref) in enumerate(zip(DILATED_PATTERNS, bias_refs)):
        def step(g, carry, d=d, bias_ref=bias_ref, p=p):
            for j in range(BLOCKS_PER_STEP):
                block(g * BLOCKS_PER_STEP + j, d, bias_ref, p == 0, p == len(DILATED_PATTERNS) - 1)
            return carry
        lax.fori_loop(0, nsteps // BLOCKS_PER_STEP, step, 0)


def _attention(q, k, v, biases):
    bsz, seq, width = q.shape
    for _, d in DILATED_PATTERNS:
        length = seq // d
        assert length >= KWIN and length % QBLK == 0 and (length // QBLK) & (length // QBLK - 1) == 0
    assert (seq // QBLK) % BLOCKS_PER_STEP == 0
    blk = pl.BlockSpec((None, seq, LANES), lambda b, c: (b, 0, c))
    bias_spec = pl.BlockSpec((2, 3, QBLK, KWIN), lambda b, c: (c, 0, 0, 0))
    return pl.pallas_call(
        functools.partial(_attn_kernel, seq=seq),
        grid=(bsz, width // LANES),
        in_specs=[blk, blk, blk] + [bias_spec] * len(biases),
        out_specs=blk,
        out_shape=jax.ShapeDtypeStruct((bsz, seq, width), F32),
        scratch_shapes=[pltpu.VMEM((seq, LANES), F32)] * 3,
        compiler_params=_cparams(("parallel", "arbitrary"), VMEM_LIMIT),
        name="attention",
    )(q, k, v, *biases)


SSM_LANE_GROUPS = SSM_WIDTH // LANES
SSM_GB_STATES = (LANES // SSM_GROUP) * SSM_STATE


def _ssm_params(lam_re, lam_im, log_step, b_re, b_im, c_re, c_im):
    f = lambda t: t.astype(F32)
    lr, li = f(lam_re), f(lam_im)
    step = jnp.exp(f(log_step))[..., None]
    mag = jnp.exp(lr * step)
    ar, ai = mag * jnp.cos(li * step), mag * jnp.sin(li * step)
    nr, ni = ar - 1.0, ai
    den = lr * lr + li * li
    cr, ci = (nr * lr + ni * li) / den, (ni * lr - nr * li) / den
    br, bi = f(b_re), f(b_im)
    bbr = cr[..., None] * br - ci[..., None] * bi
    bbi = cr[..., None] * bi + ci[..., None] * br
    gpb = LANES // SSM_GROUP
    eye = jnp.eye(gpb, dtype=F32)

    def in_map(t):
        t = t.reshape(2, SSM_LANE_GROUPS, gpb, SSM_STATE, SSM_GROUP)
        return jnp.einsum('dbgpc,gh->dbgchp', t, eye).reshape(2, SSM_LANE_GROUPS, LANES, SSM_GB_STATES)

    def out_map(t):
        t = t.reshape(2, SSM_LANE_GROUPS, gpb, SSM_GROUP, SSM_STATE)
        return jnp.einsum('dbgcp,gh->dbgphc', t, eye).reshape(2, SSM_LANE_GROUPS, SSM_GB_STATES, LANES)

    bw = jnp.concatenate([in_map(bbr), in_map(bbi)], axis=-1).astype(BF16)
    cw = jnp.concatenate([out_map(f(c_re)), -out_map(f(c_im))], axis=-2).astype(BF16)
    a = jnp.stack([ar.reshape(2, SSM_LANE_GROUPS, SSM_GB_STATES),
                   ai.reshape(2, SSM_LANE_GROUPS, SSM_GB_STATES)], axis=2)
    return a, bw, cw


def _ssm_kernel(u_ref, a_ref, bw_ref, cw_ref, y_ref, st_ref, bu_ref, *, ts, bsz):
    d = pl.program_id(0)
    ns = SSM_GB_STATES

    @pl.when(pl.program_id(1) == 0)
    def _():
        st_ref[...] = jnp.zeros_like(st_ref)

    for gb in range(SSM_LANE_GROUPS):
        ub = u_ref[gb].astype(BF16)
        bu_ref[...] = jnp.dot(ub, bw_ref[0, gb], preferred_element_type=F32)
        ar = jnp.broadcast_to(a_ref[0, gb, 0:1, :], (bsz, ns))
        ai = jnp.broadcast_to(a_ref[0, gb, 1:2, :], (bsz, ns))

        def step(j, carry, ar=ar, ai=ai):
            xr, xi = carry
            tl = jnp.where(d == 0, j, ts - 1 - j)
            r = pl.multiple_of(tl * bsz, bsz)
            nr = ar * xr - ai * xi + bu_ref[pl.ds(r, bsz), :ns]
            ni = ar * xi + ai * xr + bu_ref[pl.ds(r, bsz), ns:]
            bu_ref[pl.ds(r, bsz), :ns] = nr
            bu_ref[pl.ds(r, bsz), ns:] = ni
            return nr, ni

        xr, xi = lax.fori_loop(0, ts, step, (st_ref[gb, :, :ns], st_ref[gb, :, ns:]))
        st_ref[gb, :, :ns] = xr
        st_ref[gb, :, ns:] = xi
        y_ref[0, :, gb * LANES:(gb + 1) * LANES] = jnp.dot(
            bu_ref[...].astype(BF16), cw_ref[0, gb], preferred_element_type=F32)


def _ssm_scan(u_tm, a, bw, cw, *, bsz, ts=64):
    slabs, rows, _ = u_tm.shape
    width = slabs * LANES
    seq = rows // bsz
    ts = min(ts, seq)
    nt = seq // ts
    tblk = lambda d, i: jnp.where(d == 0, i, nt - 1 - i)
    return pl.pallas_call(
        functools.partial(_ssm_kernel, ts=ts, bsz=bsz),
        grid=(2, nt),
        in_specs=[
            pl.BlockSpec((slabs, ts * bsz, LANES), lambda d, i: (0, tblk(d, i), 0)),
            pl.BlockSpec((1,) + a.shape[1:], lambda d, i: (d, 0, 0, 0)),
            pl.BlockSpec((1,) + bw.shape[1:], lambda d, i: (d, 0, 0, 0)),
            pl.BlockSpec((1,) + cw.shape[1:], lambda d, i: (d, 0, 0, 0)),
        ],
        out_specs=pl.BlockSpec((1, ts * bsz, width), lambda d, i: (d, tblk(d, i), 0)),
        out_shape=jax.ShapeDtypeStruct((2, rows, width), F32),
        scratch_shapes=[pltpu.VMEM((SSM_LANE_GROUPS, bsz, 2 * SSM_GB_STATES), F32),
                        pltpu.VMEM((ts * bsz, 2 * SSM_GB_STATES), F32)],
        compiler_params=_cparams(("arbitrary", "arbitrary"), VMEM_LIMIT),
        name="ssm_scan",
    )(u_tm, a, bw, cw)


def _ssm_post_kernel(y_ref, u_ref, d_ref, wg_ref, g_ref, wo_ref, z_ref, slab_ref, *, bsz):
    u = jnp.concatenate([u_ref[c] for c in range(u_ref.shape[0])], axis=-1)
    y = _gelu(y_ref[0] + y_ref[1] + d_ref[...] * u).astype(BF16)
    ab = jnp.dot(y, wg_ref[...], preferred_element_type=F32)
    ssm = ab[:, :SSM_WIDTH] * jax.nn.sigmoid(ab[:, SSM_WIDTH:])
    n = _rms(ssm, g_ref[...]).astype(BF16)
    z = jnp.dot(n, wo_ref[...], preferred_element_type=F32)
    nt, dm = z.shape[0] // bsz, z.shape[1]
    for c in range(dm // LANES):
        slab_ref[c] = z[:, c * LANES:(c + 1) * LANES]
    for b in range(bsz):
        for c in range(dm // LANES):
            z_ref[:, b * dm + c * LANES:b * dm + (c + 1) * LANES] = slab_ref[c, pl.ds(b, nt, stride=bsz), :]


def _ssm_post(y, u_tm, d_skip, w_glu_bf16, g_ssm, w_out_ssm_bf16, *, bsz, tr=512):
    slabs, rows, _ = u_tm.shape
    width = slabs * LANES
    tr = min(tr, rows)
    assert tr % bsz == 0
    dm = w_out_ssm_bf16.shape[1]
    full = lambda a: pl.BlockSpec(a.shape, lambda i: (0,) * a.ndim)
    d2, g2 = d_skip.reshape(1, width), g_ssm.reshape(1, width)
    return pl.pallas_call(
        functools.partial(_ssm_post_kernel, bsz=bsz),
        grid=(rows // tr,),
        in_specs=[pl.BlockSpec((2, tr, width), lambda i: (0, i, 0)),
                  pl.BlockSpec((slabs, tr, LANES), lambda i: (0, i, 0)),
                  full(d2), full(w_glu_bf16), full(g2), full(w_out_ssm_bf16)],
        out_specs=pl.BlockSpec((tr // bsz, bsz * dm), lambda i: (i, 0)),
        out_shape=jax.ShapeDtypeStruct((rows // bsz, bsz * dm), F32),
        scratch_shapes=[pltpu.VMEM((dm // LANES, tr, LANES), F32)],
        compiler_params=_cparams(("parallel",), VMEM_LIMIT),
        name="ssm_post",
    )(y, u_tm, d2, w_glu_bf16, g2, w_out_ssm_bf16)


def _mix_out_kernel(x_ref, z_ref, a_ref, ga_ref, wo_ref, gf_ref, wq_ref, xn_ref, h_ref, q_ref):
    n = _rms(a_ref[0], ga_ref[...]).astype(BF16)
    xn = x_ref[0] + z_ref[...] + jnp.dot(n, wo_ref[...], preferred_element_type=F32)
    xn_ref[0] = xn
    h = _rms(xn, gf_ref[...])
    for c in range(SUBLANES):
        h_ref[pl.ds(c, h.shape[0], stride=SUBLANES), :] = h[:, c * LANES:(c + 1) * LANES]
    qp = jnp.dot(h.astype(BF16), wq_ref[...], preferred_element_type=F32)
    for hd in range(PEER_HEADS):
        q_ref[hd] = qp[:, hd * LANES:(hd + 1) * LANES]


def _mix_out(x, z_tm, attn, g_attn, w_out_attn_bf16, norm_ffn, w_query, *, ts=256):
    bsz, seq, dm = x.shape
    ts = min(ts, seq)
    ns = seq // ts
    row = pl.BlockSpec((1, ts, dm), lambda b, i: (b, i, 0))
    half = pl.BlockSpec((1, ts, ATTN_WIDTH), lambda b, i: (b, i, 0))
    full = lambda a: pl.BlockSpec(a.shape, lambda b, i: (0,) * a.ndim)
    ga, gf = g_attn.reshape(1, ATTN_WIDTH), norm_ffn.reshape(1, dm)
    qdim = w_query.shape[1] // PEER_HEADS
    return pl.pallas_call(
        _mix_out_kernel,
        grid=(bsz, ns),
        in_specs=[row, pl.BlockSpec((ts, dm), lambda b, i: (i, b)), half,
                  full(ga), full(w_out_attn_bf16), full(gf), full(w_query)],
        out_specs=[row, pl.BlockSpec((ts * SUBLANES, LANES), lambda b, i: (b * ns + i, 0)),
                   pl.BlockSpec((PEER_HEADS, ts, qdim), lambda b, i: (0, b * ns + i, 0))],
        out_shape=[jax.ShapeDtypeStruct((bsz, seq, dm), F32),
                   jax.ShapeDtypeStruct((bsz * seq * SUBLANES, LANES), F32),

                   jax.ShapeDtypeStruct((PEER_HEADS, bsz * seq, qdim), F32)],
        compiler_params=_cparams(("parallel", "arbitrary"), VMEM_LIMIT),
        name="mix_out",
    )(x, z_tm, attn, ga, w_out_attn_bf16, gf, w_query)


TOPK_TOKENS = SUBLANES * LANES
KEY_PITCH = PEER_KEYS + 4
_CANDIDATES = tuple((a, b) for a in range(PEER_TOPK) for b in range(PEER_TOPK) if (a + 1) * (b + 1) <= PEER_TOPK)


def _tree(op, xs):
    xs = list(xs)
    while len(xs) > 1:
        xs = [op(xs[i], xs[i + 1]) if i + 1 < len(xs) else xs[i] for i in range(0, len(xs), 2)]
    return xs[0]


def _extract16(problems):
    ninf = jnp.float32(-jnp.inf)

    def better(a, b):
        gt = b[0] > a[0]
        return tuple(jnp.where(gt, y, x) for x, y in zip(a, b))

    def step(r, carry):
        for p in problems:
            s_ref, order = p["s"], p["order"]
            assert list(order) == sorted(order)
            n = len(order)
            rows = [(s_ref[k], order[k]) + ((p["pay"][k],) if p.get("pay") is not None else ())
                    for k in range(n)]
            win = _tree(better, rows)
            m, am = win[0], win[1]
            for k in range(n):
                s_ref[k] = jnp.where(am == order[k], ninf, s_ref[k])
            p["vals"][r] = m
            p["picks"][r] = win[2] if len(win) > 2 else am.astype(jnp.int32)
        return carry

    lax.fori_loop(0, PEER_TOPK, step, 0)


def _peer_topk_kernel(q_ref, k_ref, idx_ref, gate_ref,
                      slab_ref, s1_ref, s2_ref, t1_ref, i1_ref, t2_ref, i2_ref, cand_ref, pay_ref, ts_ref, ex_ref):
    dn = (((1,), (1,)), ((), ()))
    keys = tuple(range(PEER_KEYS))

    def head(h, carry):
        for w, s_ref in ((0, s1_ref), (1, s2_ref)):
            for j in range(SUBLANES):
                slab_ref[j * KEY_PITCH:j * KEY_PITCH + PEER_KEYS, :] = lax.dot_general(
                    k_ref[w, h], q_ref[h, j * LANES:(j + 1) * LANES, :].astype(BF16), dn,
                    preferred_element_type=F32)
            for k in range(PEER_KEYS):
                s_ref[k] = slab_ref[pl.ds(k, SUBLANES, stride=KEY_PITCH), :]
        _extract16([dict(s=s1_ref, order=keys, vals=t1_ref, picks=i1_ref),
                    dict(s=s2_ref, order=keys, vals=t2_ref, picks=i2_ref)])
        for c, (a, b) in enumerate(_CANDIDATES):
            cand_ref[c] = t1_ref[a] + t2_ref[b]
            pay_ref[c] = i1_ref[a] * PEER_KEYS + i2_ref[b]
        _extract16([dict(s=cand_ref, order=tuple(a * PEER_TOPK + b for a, b in _CANDIDATES), pay=pay_ref,
                         vals=ts_ref, picks=ex_ref)])
        top_s = ts_ref[...]
        e = jnp.exp(top_s - jnp.max(top_s, axis=0, keepdims=True))
        gate_ref[0, h] = e / jnp.sum(e, axis=0, keepdims=True)
        idx_ref[0, h] = ex_ref[...] * ROWS_PER_EXPERT
        return carry

    lax.fori_loop(0, PEER_HEADS, head, 0)


def _peer_topk(qp, keys_pad):
    n_tok = qp.shape[1]
    tt = TOPK_TOKENS
    assert n_tok % tt == 0
    shp = (n_tok // tt, PEER_HEADS, PEER_TOPK, SUBLANES, LANES)
    out = pl.BlockSpec((1,) + shp[1:], lambda i: (i, 0, 0, 0, 0))
    vregs = lambda n, dt: pltpu.VMEM((n, SUBLANES, LANES), dt)
    idx, gate = pl.pallas_call(
        _peer_topk_kernel,
        grid=(n_tok // tt,),
        in_specs=[pl.BlockSpec((PEER_HEADS, tt, qp.shape[2]), lambda i: (0, i, 0)),
                  pl.BlockSpec(keys_pad.shape, lambda i: (0, 0, 0, 0))],
        out_specs=[out, out],
        out_shape=[jax.ShapeDtypeStruct(shp, jnp.int32), jax.ShapeDtypeStruct(shp, F32)],
        scratch_shapes=[pltpu.VMEM((SUBLANES * KEY_PITCH, LANES), F32),
                        vregs(PEER_KEYS, F32), vregs(PEER_KEYS, F32),
                        vregs(PEER_TOPK, F32), vregs(PEER_TOPK, jnp.int32),
                        vregs(PEER_TOPK, F32), vregs(PEER_TOPK, jnp.int32),
                        vregs(len(_CANDIDATES), F32), vregs(len(_CANDIDATES), jnp.int32),
                        vregs(PEER_TOPK, F32), vregs(PEER_TOPK, jnp.int32)],
        compiler_params=_cparams(("parallel",), VMEM_LIMIT),
        name="peer_topk",
    )(qp, keys_pad)
    to_tok = lambda a: jnp.transpose(a, (0, 3, 4, 1, 2)).reshape(n_tok, PEER_SEL)
    return to_tok(idx), to_tok(gate)


ROWS_PER_EXPERT = 4
PAIR_CHUNK = 32
CHUNK_ROWS = PAIR_CHUNK * ROWS_PER_EXPERT
SMEM_GROUP = 8
N_CHUNKS = PEER_SEL // PAIR_CHUNK
TOKENS_PER_STEP = 16


def _pack_table_kernel(t_ref, o_ref):
    rows, dm = t_ref.shape
    for c in range(ROWS_PER_EXPERT):
        lo = lax.bitcast_convert_type(t_ref[:, c * LANES:(c + 1) * LANES].astype(BF16).astype(F32), jnp.int32)
        hi = lax.bitcast_convert_type(
            t_ref[:, dm // 2 + c * LANES:dm // 2 + (c + 1) * LANES].astype(BF16).astype(F32), jnp.int32)
        o_ref[pl.ds(c, rows, stride=ROWS_PER_EXPERT), :] = lax.shift_right_logical(lo, 16) | hi


def _pack_table(tables, layer, *, rows=512):
    _, e, dm = tables.shape
    assert dm == 2 * ROWS_PER_EXPERT * LANES
    rows = min(rows, e)
    return pl.pallas_call(
        _pack_table_kernel,
        grid=(e // rows,),
        in_specs=[pl.BlockSpec((None, rows, dm), lambda i: (layer, i, 0))],
        out_specs=pl.BlockSpec((rows * ROWS_PER_EXPERT, LANES), lambda i: (i, 0)),
        out_shape=jax.ShapeDtypeStruct((e * ROWS_PER_EXPERT, LANES), jnp.int32),
        compiler_params=_cparams(("parallel",), VMEM_LIMIT),
        name="pack_table",
    )(tables)


def _unpack_words(w):
    lo = lax.bitcast_convert_type(w << 16, F32)
    hi = lax.bitcast_convert_type(w & jnp.int32(-65536), F32)
    return lo, hi


def _gather_chunk(idx_ref, tab_ref, buf_ref, c):
    for g in range(PAIR_CHUNK // SMEM_GROUP):
        ids = idx_ref.at[pl.ds(pl.multiple_of(c * PAIR_CHUNK + g * SMEM_GROUP, SMEM_GROUP), SMEM_GROUP)]
        for i in range(SMEM_GROUP):
            e4 = pl.multiple_of(ids[i], ROWS_PER_EXPERT)
            s = (g * SMEM_GROUP + i) * ROWS_PER_EXPERT
            buf_ref[s:s + ROWS_PER_EXPERT, :] = tab_ref[pl.ds(e4, ROWS_PER_EXPERT), :]


def _peer_u_kernel(idx_ref, h_ref, gate_ref, tab_ref, coef_ref, *scratch, tt):
    bufs, m_ref = (scratch[:N_CHUNKS], scratch[N_CHUNKS:2 * N_CHUNKS]), scratch[-1]
    n_chunks = tt * N_CHUNKS
    row = lax.broadcasted_iota(jnp.int32, (SUBLANES, CHUNK_ROWS), 0)
    col = lax.broadcasted_iota(jnp.int32, (SUBLANES, CHUNK_ROWS), 1)
    quarter = (col % ROWS_PER_EXPERT) == (row % ROWS_PER_EXPERT)
    m_lo = jnp.logical_and(row < ROWS_PER_EXPERT, quarter)
    m_hi = jnp.logical_and(row >= ROWS_PER_EXPERT, quarter)
    pr = lax.broadcasted_iota(jnp.int32, (CHUNK_ROWS, LANES), 0) // ROWS_PER_EXPERT
    pc = lax.broadcasted_iota(jnp.int32, (CHUNK_ROWS, LANES), 1)
    place = [(pr + part * PAIR_CHUNK == pc).astype(BF16) for part in range(N_CHUNKS)]
    dn = (((1,), (1,)), ((), ()))
    gather = functools.partial(_gather_chunk, idx_ref, tab_ref)

    def chunk_dots(t, part, buf_ref):
        rows = pl.ds(pl.multiple_of(t * SUBLANES, SUBLANES), SUBLANES)
        x = h_ref[rows, :].astype(BF16)
        lo, hi = _unpack_words(buf_ref[...])
        b = jnp.concatenate([lo.astype(BF16), hi.astype(BF16)], axis=0)
        out = lax.dot_general(x, b, dn, preferred_element_type=F32)
        m = jnp.where(m_lo, out[:, :CHUNK_ROWS], 0.0) + jnp.where(m_hi, out[:, CHUNK_ROWS:], 0.0)
        m_ref[part, pl.ds(t, 1), :] = jnp.sum(m, axis=0, keepdims=True)

    for j in range(N_CHUNKS):
        gather(bufs[0][j], j)

    def body(i, carry):
        for s in range(TOKENS_PER_STEP):
            half = s % 2
            t = TOKENS_PER_STEP * i + s
            for j in range(N_CHUNKS):
                chunk_dots(t, j, bufs[half][j])
            for j in range(N_CHUNKS):
                gather(bufs[1 - half][j], jnp.minimum((t + 1) * N_CHUNKS + j, n_chunks - 1))
        return carry

    lax.fori_loop(0, tt // TOKENS_PER_STEP, body, 0)

    dots = None
    for part in range(N_CHUNKS):
        m = m_ref[part]
        m_top = m.astype(BF16)
        m_rest = (m - m_top.astype(F32)).astype(BF16)
        s = (jnp.dot(m_top, place[part], preferred_element_type=F32)
             + jnp.dot(m_rest, place[part], preferred_element_type=F32))
        dots = s if dots is None else dots + s
    coef = (gate_ref[...] * _gelu(dots)).astype(BF16)
    for part in range(N_CHUNKS):
        coef_ref[part] = lax.dot_general(coef, place[part], dn, preferred_element_type=F32)


def _peer_u(idx4, h8, gates, tab, *, tt=128):
    n_tok = gates.shape[0]
    tt = min(tt, n_tok)
    assert TOKENS_PER_STEP % 2 == 0 and tt % TOKENS_PER_STEP == 0
    tok2 = pl.BlockSpec((tt, PEER_SEL), lambda i: (i, 0))
    return pl.pallas_call(
        functools.partial(_peer_u_kernel, tt=tt),
        grid=(n_tok // tt,),
        in_specs=[pl.BlockSpec((tt * PEER_SEL,), lambda i: (i,), memory_space=pltpu.SMEM),
                  pl.BlockSpec((tt * SUBLANES, LANES), lambda i: (i, 0)), tok2,
                  pl.BlockSpec(tab.shape, lambda i: (0, 0), pipeline_mode=pl.Buffered(1))],
        out_specs=pl.BlockSpec((N_CHUNKS, tt, CHUNK_ROWS), lambda i: (0, i, 0)),
        out_shape=jax.ShapeDtypeStruct((N_CHUNKS, n_tok, CHUNK_ROWS), F32),
        scratch_shapes=[pltpu.VMEM((CHUNK_ROWS, LANES), jnp.int32)] * (2 * N_CHUNKS)
                       + [pltpu.VMEM((N_CHUNKS, tt, LANES), F32)],
        compiler_params=_cparams(("arbitrary",), VMEM_LIMIT),
        name="peer_u",
    )(idx4, h8, gates, tab)


def _peer_v_kernel(idx_ref, cx_ref, tab_ref, o_ref, *scratch, tt):
    bufs, parts_ref = (scratch[:N_CHUNKS], scratch[N_CHUNKS:2 * N_CHUNKS]), scratch[2 * N_CHUNKS]
    n_chunks = tt * N_CHUNKS
    row = lax.broadcasted_iota(jnp.int32, (SUBLANES, CHUNK_ROWS), 0)
    col = lax.broadcasted_iota(jnp.int32, (SUBLANES, CHUNK_ROWS), 1)
    quarter = (col % ROWS_PER_EXPERT) == (row % ROWS_PER_EXPERT)
    m_lo = jnp.logical_and(row < ROWS_PER_EXPERT, quarter)
    m_hi = jnp.logical_and(row >= ROWS_PER_EXPERT, quarter)
    gather = functools.partial(_gather_chunk, idx_ref, tab_ref)

    def chunk_sum(t, parity, part, buf_ref):
        cx = jnp.broadcast_to(cx_ref[part, pl.ds(t, 1), :], (SUBLANES, CHUNK_ROWS))
        a = jnp.concatenate([jnp.where(m_lo, cx, 0.0), jnp.where(m_hi, cx, 0.0)], axis=1).astype(BF16)
        lo, hi = _unpack_words(buf_ref[...])
        b = jnp.concatenate([lo.astype(BF16), hi.astype(BF16)], axis=0)
        parts_ref[parity, part] = jnp.dot(a, b, preferred_element_type=F32)

    parts_ref[...] = jnp.zeros_like(parts_ref)
    for j in range(N_CHUNKS):
        gather(bufs[0][j], j)

    def body(i, carry):
        for s in range(TOKENS_PER_STEP):
            half = s % 2
            t = TOKENS_PER_STEP * i + s
            prev = jnp.maximum(t - 1, 0)
            o_ref[prev] = _tree(jnp.add, [parts_ref[1 - half, j] for j in range(N_CHUNKS)])
            for j in range(N_CHUNKS):
                chunk_sum(t, half, j, bufs[half][j])
            for j in range(N_CHUNKS):
                gather(bufs[1 - half][j], jnp.minimum((t + 1) * N_CHUNKS + j, n_chunks - 1))
        return carry

    lax.fori_loop(0, tt // TOKENS_PER_STEP, body, 0)
    o_ref[tt - 1] = _tree(jnp.add, [parts_ref[1, j] for j in range(N_CHUNKS)])


def _peer_v(idx4, cx, tab, *, tt=128):
    n_tok = idx4.shape[0] // PEER_SEL
    tt = min(tt, n_tok)
    assert TOKENS_PER_STEP % 2 == 0 and tt % TOKENS_PER_STEP == 0
    smem = pl.BlockSpec((tt * PEER_SEL,), lambda i: (i,), memory_space=pltpu.SMEM)
    return pl.pallas_call(
        functools.partial(_peer_v_kernel, tt=tt),
        grid=(n_tok // tt,),
        in_specs=[smem, pl.BlockSpec((N_CHUNKS, tt, CHUNK_ROWS), lambda i: (0, i, 0)),
                  pl.BlockSpec(tab.shape, lambda i: (0, 0), pipeline_mode=pl.Buffered(1))],
        out_specs=pl.BlockSpec((tt, SUBLANES, LANES), lambda i: (i, 0, 0)),
        out_shape=jax.ShapeDtypeStruct((n_tok, SUBLANES, LANES), F32),
        scratch_shapes=[pltpu.VMEM((CHUNK_ROWS, LANES), jnp.int32)] * (2 * N_CHUNKS)
                       + [pltpu.VMEM((2, N_CHUNKS, SUBLANES, LANES), F32)],
        compiler_params=_cparams(("arbitrary",), VMEM_LIMIT),
        name="peer_v",
    )(idx4, cx, tab)


def _peer(h8, qp, sub_keys, tab_u, tab_v):
    n_tok = qp.shape[1]
    half = sub_keys.shape[-1]
    z = jnp.zeros_like(sub_keys[0])
    keys_pad = jnp.stack([jnp.concatenate([sub_keys[0], z], axis=-1),
                          jnp.concatenate([z, sub_keys[1]], axis=-1)]).astype(BF16)
    assert keys_pad.shape[-1] == 2 * half == qp.shape[-1]
    idx4, gates = _peer_topk(qp, keys_pad)
    idx4 = idx4.reshape(n_tok * PEER_SEL)
    coef = _peer_u(idx4, h8, gates, tab_u)
    out = _peer_v(idx4, coef, tab_v)
    return out.reshape(n_tok * SUBLANES, LANES)


def _final_kernel(x_ref, p_ref, g_ref, o_ref):
    o_ref[...] = _rms(x_ref[...] + _rows_from_chunks(p_ref, x_ref.shape[0]), g_ref[...])


def _final_norm(x, p, g, *, tr=1024):
    rows, dm = x.shape
    tr = min(tr, rows)
    blk = pl.BlockSpec((tr, dm), lambda i: (i, 0))
    return pl.pallas_call(
        _final_kernel,
        grid=(rows // tr,),
        in_specs=[blk, pl.BlockSpec((tr * SUBLANES, LANES), lambda i: (i, 0)),
                  pl.BlockSpec((1, dm), lambda i: (0, 0))],
        out_specs=blk,
        out_shape=jax.ShapeDtypeStruct((rows, dm), F32),
        compiler_params=_cparams(("parallel",), VMEM_LIMIT),
        name="final_norm",
    )(x, p, g.reshape(1, dm))


def kernel(x, w_in, w_out, rel_bias, g_attn, g_ssm, norm_mix, norm_ffn, lam_re, lam_im, log_step, b_re, b_im, c_re, c_im, d_skip, w_glu, w_query, sub_keys, expert_u, expert_v, norm_final):
    bsz, seq, dm = x.shape
    depth = w_in.shape[0]
    prev = None
    biases = [_attn_bias_tables(rel_bias, d) for _, d in DILATED_PATTERNS]
    for l in range(depth):
        x, q, k, v, u_tm = _in_proj(x, prev, norm_mix[l], w_in[l].astype(BF16))
        attn = _attention(q, k, v, biases)
        a, bw, cw = _ssm_params(lam_re[l], lam_im[l], log_step[l], b_re[l], b_im[l], c_re[l], c_im[l])
        y = _ssm_scan(u_tm, a, bw, cw, bsz=bsz)
        wo = w_out[l].astype(BF16)
        z = _ssm_post(y, u_tm, d_skip[l], w_glu[l].astype(BF16), g_ssm[l], wo[ATTN_WIDTH:], bsz=bsz)
        x, h, qp = _mix_out(x, z, attn,
                            g_attn[l], wo[:ATTN_WIDTH], norm_ffn[l], w_query[l].astype(BF16))
        prev = _peer(h, qp, sub_keys[l], _pack_table(expert_u, l), _pack_table(expert_v, l))
    out = _final_norm(x.reshape(bsz * seq, dm), prev, norm_final)
    return out.reshape(bsz, seq, dm)
```

```python
import functools
import math

import numpy as np
import jax
import jax.numpy as jnp
from jax import lax
from jax.experimental import pallas as pl
from jax.experimental.pallas import tpu as pltpu

F32 = jnp.float32
BF16 = jnp.bfloat16

EPS = 1e-6
NEG_INF = -1e30
HEAD_DIM = 64
ATTN_WIDTH = 512
SSM_WIDTH = 512
SSM_GROUP = 16
SSM_STATE = 64
DILATED_PATTERNS = ((128, 1), (512, 4), (2048, 16))
REL_BUCKETS = 32
REL_MAX_DISTANCE = 1024
PEER_HEADS = 8
PEER_KEYS = 128
PEER_TOPK = 16
PEER_SEL = PEER_HEADS * PEER_TOPK

LANES = 128
SUBLANES = 8
QBLK = 128
KWIN = 256
BAND = 64
VMEM_LIMIT = 52 * 1024 * 1024


def _cparams(sem, vmem=None):
    return pltpu.CompilerParams(dimension_semantics=sem, vmem_limit_bytes=vmem)


def _rms(x, g):
    return x * lax.rsqrt(jnp.mean(x * x, axis=-1, keepdims=True) + EPS) * g


def _gelu(x):
    return 0.5 * x * (1.0 + lax.erf(x * (1.0 / math.sqrt(2.0))))


def _rows_from_chunks(p_ref, n_rows):
    return jnp.concatenate([p_ref[pl.ds(c, n_rows, stride=SUBLANES), :] for c in range(SUBLANES)], axis=-1)


def _in_proj_kernel(*refs, has_prev):
    if has_prev:
        x_ref, p_ref, g_ref, w_ref, xo_ref, q_ref, k_ref, v_ref, u_ref = refs
        x = x_ref[0] + _rows_from_chunks(p_ref, x_ref.shape[1])
    else:
        x_ref, g_ref, w_ref, xo_ref, q_ref, k_ref, v_ref, u_ref = refs
        x = x_ref[0]
    xo_ref[0] = x
    h = _rms(x, g_ref[...]).astype(BF16)
    proj = jnp.dot(h, w_ref[...], preferred_element_type=F32)
    a = ATTN_WIDTH
    q_ref[0] = proj[:, :a] * (HEAD_DIM ** -0.5)
    k_ref[0] = proj[:, a:2 * a]
    v_ref[0] = proj[:, 2 * a:3 * a]
    b, bsz = pl.program_id(1), pl.num_programs(1)
    for c in range(SSM_WIDTH // LANES):
        u_ref[c, pl.ds(b, proj.shape[0], stride=bsz), :] = proj[:, 3 * a + c * LANES:3 * a + (c + 1) * LANES]


def _in_proj(x, prev, g, w_bf16, *, ts=256):
    bsz, seq, dm = x.shape
    ts = min(ts, seq)
    row = pl.BlockSpec((1, ts, dm), lambda i, b: (b, i, 0))
    qkv = pl.BlockSpec((1, ts, ATTN_WIDTH), lambda i, b: (b, i, 0))
    ns = seq // ts
    chunks = pl.BlockSpec((ts * SUBLANES, LANES), lambda i, b: (b * ns + i, 0))
    ins = [x] + ([prev] if prev is not None else []) + [g.reshape(1, dm), w_bf16]
    in_specs = [row] + ([chunks] if prev is not None else []) + [
        pl.BlockSpec((1, dm), lambda i, b: (0, 0)),
        pl.BlockSpec(w_bf16.shape, lambda i, b: (0, 0)),
    ]
    slabs = SSM_WIDTH // LANES
    return pl.pallas_call(
        functools.partial(_in_proj_kernel, has_prev=prev is not None),
        grid=(seq // ts, bsz),
        in_specs=in_specs,
        out_specs=[row, qkv, qkv, qkv, pl.BlockSpec((slabs, ts * bsz, LANES), lambda i, b: (0, i, 0))],
        out_shape=[
            jax.ShapeDtypeStruct((bsz, seq, dm), F32),
            jax.ShapeDtypeStruct((bsz, seq, ATTN_WIDTH), F32),
            jax.ShapeDtypeStruct((bsz, seq, ATTN_WIDTH), F32),
            jax.ShapeDtypeStruct((bsz, seq, ATTN_WIDTH), F32),
            jax.ShapeDtypeStruct((slabs, seq * bsz, LANES), F32),
        ],
        compiler_params=_cparams(("parallel", "arbitrary"), VMEM_LIMIT),
        name="in_proj",
    )(*ins)


def _t5_buckets(rel):
    half = REL_BUCKETS // 2
    max_exact = half // 2
    n = np.abs(rel)
    large = max_exact + (np.log(np.maximum(n, 1) / max_exact)
                         / np.log(REL_MAX_DISTANCE / max_exact) * (half - max_exact)).astype(np.int32)
    large = np.minimum(large, half - 1)
    return (np.where(rel > 0, half, 0) + np.where(n < max_exact, n, large)).astype(np.int32)


def _attn_bias_tables(rel_bias, dilation):
    ql = np.arange(QBLK)[:, None]
    kl = np.arange(KWIN)[None, :]
    delta = np.stack([kl + off - ql for off in (0, -BAND, -2 * BAND)])
    buckets = np.where(np.abs(delta) <= BAND, _t5_buckets(delta * dilation), -1)
    rb = rel_bias.astype(F32).T
    bk = jnp.asarray(buckets, jnp.int32)[None]
    tab = jnp.full((rb.shape[0],) + buckets.shape, NEG_INF, F32)
    for b in range(REL_BUCKETS):
        tab = jnp.where(bk == b, rb[:, b][:, None, None, None], tab)
    return tab


BLOCKS_PER_STEP = 4


def _attn_kernel(q_ref, k_ref, v_ref, *rest, seq):
    bias_refs, (o_ref, acc_ref, m_ref, z_ref) = rest[:len(DILATED_PATTERNS)], rest[len(DILATED_PATTERNS):]
    lane = lax.broadcasted_iota(jnp.int32, (QBLK, LANES), 1)
    is_h0 = lane < HEAD_DIM
    dn = (((1,), (1,)), ((), ()))
    nsteps = seq // QBLK

    def rows(start, size, d):
        return pl.ds(start, size) if d == 1 else pl.ds(start, size, stride=d)

    def block(n, d, bias_ref, first, last):
        length = seq // d
        nblk = length // QBLK
        r, i = n >> (nblk.bit_length() - 1), n & (nblk - 1)
        s = i * QBLK
        ks = jnp.clip(s - BAND, 0, length - KWIN)
        var = jnp.where(i == 0, 0, jnp.where(i == nblk - 1, 2, 1))
        q_rows = rows(r + d * s, QBLK, d)
        k_rows = rows(r + d * ks, KWIN, d)
        qb = q_ref[q_rows, :].astype(BF16)
        kb = k_ref[k_rows, :].astype(BF16)
        vb = v_ref[k_rows, :].astype(BF16)
        outs, ms, zs = [], [], []
        for h in range(2):
            keep = is_h0 if h == 0 else jnp.logical_not(is_h0)
            qh = jnp.where(keep, qb, jnp.zeros_like(qb))
            logits = lax.dot_general(qh, kb, dn, preferred_element_type=F32) + bias_ref[h, var]
            m = jnp.max(logits, axis=-1, keepdims=True)
            p = jnp.exp(logits - m)
            outs.append(jnp.dot(p.astype(BF16), vb, preferred_element_type=F32))
            ms.append(jnp.broadcast_to(m, (QBLK, LANES)))
            zs.append(jnp.broadcast_to(jnp.sum(p, axis=-1, keepdims=True), (QBLK, LANES)))
        o = jnp.where(is_h0, outs[0], outs[1])
        m = jnp.where(is_h0, ms[0], ms[1])
        z = jnp.where(is_h0, zs[0], zs[1])
        if not first:
            m_old = m_ref[q_rows, :]
            m_new = jnp.maximum(m_old, m)
            a, b = jnp.exp(m_old - m_new), jnp.exp(m - m_new)
            o = acc_ref[q_rows, :] * a + o * b
            z = z_ref[q_rows, :] * a + z * b
            m = m_new
        if last:
            o_ref[q_rows, :] = o / z
        else:
            acc_ref[q_rows, :] = o
            m_ref[q_rows, :] = m
            z_ref[q_rows, :] = z

    for p, ((_, d), bias_ref) in enumerate(zip(DILATED_PATTERNS, bias_refs)):
        def step(g, carry, d=d, bias_ref=bias_ref, p=p):
            for j in range(BLOCKS_PER_STEP):
                block(g * BLOCKS_PER_STEP + j, d, bias_ref, p == 0, p == len(DILATED_PATTERNS) - 1)
            return carry
        lax.fori_loop(0, nsteps // BLOCKS_PER_STEP, step, 0)


def _attention(q, k, v, biases):
    bsz, seq, width = q.shape
    for _, d in DILATED_PATTERNS:
        length = seq // d
        assert length >= KWIN and length % QBLK == 0 and (length // QBLK) & (length // QBLK - 1) == 0
    assert (seq // QBLK) % BLOCKS_PER_STEP == 0
    blk = pl.BlockSpec((None, seq, LANES), lambda b, c: (b, 0, c))
    bias_spec = pl.BlockSpec((2, 3, QBLK, KWIN), lambda b, c: (c, 0, 0, 0))
    return pl.pallas_call(
        functools.partial(_attn_kernel, seq=seq),
        grid=(bsz, width // LANES),
        in_specs=[blk, blk, blk] + [bias_spec] * len(biases),
        out_specs=blk,
        out_shape=jax.ShapeDtypeStruct((bsz, seq, width), F32),
        scratch_shapes=[pltpu.VMEM((seq, LANES), F32)] * 3,
        compiler_params=_cparams(("parallel", "arbitrary"), VMEM_LIMIT),
        name="attention",
    )(q, k, v, *biases)


SSM_LANE_GROUPS = SSM_WIDTH // LANES
SSM_GB_STATES = (LANES // SSM_GROUP) * SSM_STATE


def _ssm_params(lam_re, lam_im, log_step, b_re, b_im, c_re, c_im):
    f = lambda t: t.astype(F32)
    lr, li = f(lam_re), f(lam_im)
    step = jnp.exp(f(log_step))[..., None]
    mag = jnp.exp(lr * step)
    ar, ai = mag * jnp.cos(li * step), mag * jnp.sin(li * step)
    nr, ni = ar - 1.0, ai
    den = lr * lr + li * li
    cr, ci = (nr * lr + ni * li) / den, (ni * lr - nr * li) / den
    br, bi = f(b_re), f(b_im)
    bbr = cr[..., None] * br - ci[..., None] * bi
    bbi = cr[..., None] * bi + ci[..., None] * br
    gpb = LANES // SSM_GROUP
    eye = jnp.eye(gpb, dtype=F32)

    def in_map(t):
        t = t.reshape(2, SSM_LANE_GROUPS, gpb, SSM_STATE, SSM_GROUP)
        return jnp.einsum('dbgpc,gh->dbgchp', t, eye).reshape(2, SSM_LANE_GROUPS, LANES, SSM_GB_STATES)

    def out_map(t):
        t = t.reshape(2, SSM_LANE_GROUPS, gpb, SSM_GROUP, SSM_STATE)
        return jnp.einsum('dbgcp,gh->dbgphc', t, eye).reshape(2, SSM_LANE_GROUPS, SSM_GB_STATES, LANES)

    bw = jnp.concatenate([in_map(bbr), in_map(bbi)], axis=-1).astype(BF16)
    cw = jnp.concatenate([out_map(f(c_re)), -out_map(f(c_im))], axis=-2).astype(BF16)
    a = jnp.stack([ar.reshape(2, SSM_LANE_GROUPS, SSM_GB_STATES),
                   ai.reshape(2, SSM_LANE_GROUPS, SSM_GB_STATES)], axis=2)
    return a, bw, cw


def _ssm_kernel(u_ref, a_ref, bw_ref, cw_ref, y_ref, st_ref, bu0_ref, bu1_ref, *, ts, bsz):
    d = pl.program_id(0)
    ns = SSM_GB_STATES
    bu_refs = (bu0_ref, bu1_ref)

    @pl.when(pl.program_id(1) == 0)
    def _():
        st_ref[...] = jnp.zeros_like(st_ref)

    for g0 in range(0, SSM_LANE_GROUPS, 2):
        gbs = (g0, g0 + 1)
        for gb, bu_ref in zip(gbs, bu_refs):
            bu_ref[...] = jnp.dot(u_ref[gb].astype(BF16), bw_ref[0, gb], preferred_element_type=F32)
        coefs = [(jnp.broadcast_to(a_ref[0, gb, 0:1, :], (bsz, ns)), jnp.broadcast_to(a_ref[0, gb, 1:2, :], (bsz, ns)))
                 for gb in gbs]

        def step(j, carry, coefs=coefs):
            tl = jnp.where(d == 0, j, ts - 1 - j)
            r = pl.multiple_of(tl * bsz, bsz)
            out = []
            for (ar, ai), bu_ref, (xr, xi) in zip(coefs, bu_refs, carry):
                nr = ar * xr - ai * xi + bu_ref[pl.ds(r, bsz), :ns]
                ni = ar * xi + ai * xr + bu_ref[pl.ds(r, bsz), ns:]
                bu_ref[pl.ds(r, bsz), :ns] = nr
                bu_ref[pl.ds(r, bsz), ns:] = ni
                out.append((nr, ni))
            return tuple(out)

        final = lax.fori_loop(0, ts, step, tuple((st_ref[gb, :, :ns], st_ref[gb, :, ns:]) for gb in gbs))
        for gb, bu_ref, (xr, xi) in zip(gbs, bu_refs, final):
            st_ref[gb, :, :ns] = xr
            st_ref[gb, :, ns:] = xi
            y_ref[0, :, gb * LANES:(gb + 1) * LANES] = jnp.dot(
                bu_ref[...].astype(BF16), cw_ref[0, gb], preferred_element_type=F32)


def _ssm_scan(u_tm, a, bw, cw, *, bsz, ts=64):
    slabs, rows, _ = u_tm.shape
    width = slabs * LANES
    seq = rows // bsz
    ts = min(ts, seq)
    nt = seq // ts
    tblk = lambda d, i: jnp.where(d == 0, i, nt - 1 - i)
    return pl.pallas_call(
        functools.partial(_ssm_kernel, ts=ts, bsz=bsz),
        grid=(2, nt),
        in_specs=[
            pl.BlockSpec((slabs, ts * bsz, LANES), lambda d, i: (0, tblk(d, i), 0)),
            pl.BlockSpec((1,) + a.shape[1:], lambda d, i: (d, 0, 0, 0)),
            pl.BlockSpec((1,) + bw.shape[1:], lambda d, i: (d, 0, 0, 0)),
            pl.BlockSpec((1,) + cw.shape[1:], lambda d, i: (d, 0, 0, 0)),
        ],
        out_specs=pl.BlockSpec((1, ts * bsz, width), lambda d, i: (d, tblk(d, i), 0)),
        out_shape=jax.ShapeDtypeStruct((2, rows, width), F32),
        scratch_shapes=[pltpu.VMEM((SSM_LANE_GROUPS, bsz, 2 * SSM_GB_STATES), F32),
                        pltpu.VMEM((ts * bsz, 2 * SSM_GB_STATES), F32),
                        pltpu.VMEM((ts * bsz, 2 * SSM_GB_STATES), F32)],
        compiler_params=_cparams(("arbitrary", "arbitrary"), VMEM_LIMIT),
        name="ssm_scan",
    )(u_tm, a, bw, cw)


def _ssm_post_kernel(y_ref, u_ref, d_ref, wg_ref, g_ref, wo_ref, z_ref, slab_ref, *, bsz):
    u = jnp.concatenate([u_ref[c] for c in range(u_ref.shape[0])], axis=-1)
    y = _gelu(y_ref[0] + y_ref[1] + d_ref[...] * u).astype(BF16)
    ab = jnp.dot(y, wg_ref[...], preferred_element_type=F32)
    ssm = ab[:, :SSM_WIDTH] * jax.nn.sigmoid(ab[:, SSM_WIDTH:])
    n = _rms(ssm, g_ref[...]).astype(BF16)
    z = jnp.dot(n, wo_ref[...], preferred_element_type=F32)
    nt, dm = z.shape[0] // bsz, z.shape[1]
    for c in range(dm // LANES):
        slab_ref[c] = z[:, c * LANES:(c + 1) * LANES]
    for b in range(bsz):
        for c in range(dm // LANES):
            z_ref[:, b * dm + c * LANES:b * dm + (c + 1) * LANES] = slab_ref[c, pl.ds(b, nt, stride=bsz), :]


def _ssm_post(y, u_tm, d_skip, w_glu_bf16, g_ssm, w_out_ssm_bf16, *, bsz, tr=512):
    slabs, rows, _ = u_tm.shape
    width = slabs * LANES
    tr = min(tr, rows)
    assert tr % bsz == 0
    dm = w_out_ssm_bf16.shape[1]
    full = lambda a: pl.BlockSpec(a.shape, lambda i: (0,) * a.ndim)
    d2, g2 = d_skip.reshape(1, width), g_ssm.reshape(1, width)
    return pl.pallas_call(
        functools.partial(_ssm_post_kernel, bsz=bsz),
        grid=(rows // tr,),
        in_specs=[pl.BlockSpec((2, tr, width), lambda i: (0, i, 0)),
                  pl.BlockSpec((slabs, tr, LANES), lambda i: (0, i, 0)),
                  full(d2), full(w_glu_bf16), full(g2), full(w_out_ssm_bf16)],
        out_specs=pl.BlockSpec((tr // bsz, bsz * dm), lambda i: (i, 0)),
        out_shape=jax.ShapeDtypeStruct((rows // bsz, bsz * dm), F32),
        scratch_shapes=[pltpu.VMEM((dm // LANES, tr, LANES), F32)],
        compiler_params=_cparams(("parallel",), VMEM_LIMIT),
        name="ssm_post",
    )(y, u_tm, d2, w_glu_bf16, g2, w_out_ssm_bf16)


def _mix_out_kernel(x_ref, z_ref, a_ref, ga_ref, wo_ref, gf_ref, wq_ref, xn_ref, h_ref, q_ref):
    n = _rms(a_ref[0], ga_ref[...]).astype(BF16)
    xn = x_ref[0] + z_ref[...] + jnp.dot(n, wo_ref[...], preferred_element_type=F32)
    xn_ref[0] = xn
    h = _rms(xn, gf_ref[...])
    for c in range(SUBLANES):
        h_ref[pl.ds(c, h.shape[0], stride=SUBLANES), :] = h[:, c * LANES:(c + 1) * LANES]
    qp = jnp.dot(h.astype(BF16), wq_ref[...], preferred_element_type=F32)
    for hd in range(PEER_HEADS):
        q_ref[hd] = qp[:, hd * LANES:(hd + 1) * LANES]


def _mix_out(x, z_tm, attn, g_attn, w_out_attn_bf16, norm_ffn, w_query, *, ts=256):
    bsz, seq, dm = x.shape
    ts = min(ts, seq)
    ns = seq // ts
    row = pl.BlockSpec((1, ts, dm), lambda b, i: (b, i, 0))
    half = pl.BlockSpec((1, ts, ATTN_WIDTH), lambda b, i: (b, i, 0))
    full = lambda a: pl.BlockSpec(a.shape, lambda b, i: (0,) * a.ndim)
    ga, gf = g_attn.reshape(1, ATTN_WIDTH), norm_ffn.reshape(1, dm)
    qdim = w_query.shape[1] // PEER_HEADS
    return pl.pallas_call(
        _mix_out_kernel,
        grid=(bsz, ns),
        in_specs=[row, pl.BlockSpec((ts, dm), lambda b, i: (i, b)), half,
                  full(ga), full(w_out_attn_bf16), full(gf), full(w_query)],
        out_specs=[row, pl.BlockSpec((ts * SUBLANES, LANES), lambda b, i: (b * ns + i, 0)),
                   pl.BlockSpec((PEER_HEADS, ts, qdim), lambda b, i: (0, b * ns + i, 0))],
        out_shape=[jax.ShapeDtypeStruct((bsz, seq, dm), F32),
                   jax.ShapeDtypeStruct((bsz * seq * SUBLANES, LANES), F32),

                   jax.ShapeDtypeStruct((PEER_HEADS, bsz * seq, qdim), F32)],
        compiler_params=_cparams(("parallel", "arbitrary"), VMEM_LIMIT),
        name="mix_out",
    )(x, z_tm, attn, ga, w_out_attn_bf16, gf, w_query)


TOPK_TOKENS = SUBLANES * LANES
KEY_PITCH = PEER_KEYS + 4
_CANDIDATES = tuple((a, b) for a in range(PEER_TOPK) for b in range(PEER_TOPK) if (a + 1) * (b + 1) <= PEER_TOPK)


def _tree(op, xs):
    xs = list(xs)
    while len(xs) > 1:
        xs = [op(xs[i], xs[i + 1]) if i + 1 < len(xs) else xs[i] for i in range(0, len(xs), 2)]
    return xs[0]


def _extract16(problems):
    ninf = jnp.float32(-jnp.inf)

    def better(a, b):
        gt = b[0] > a[0]
        return tuple(jnp.where(gt, y, x) for x, y in zip(a, b))

    def step(r, carry):
        for p in problems:
            s_ref, order = p["s"], p["order"]
            assert list(order) == sorted(order)
            n = len(order)
            rows = [(s_ref[k], order[k]) + ((p["pay"][k],) if p.get("pay") is not None else ())
                    for k in range(n)]
            win = _tree(better, rows)
            m, am = win[0], win[1]
            for k in range(n):
                s_ref[k] = jnp.where(am == order[k], ninf, s_ref[k])
            p["vals"][r] = m
            p["picks"][r] = win[2] if len(win) > 2 else am.astype(jnp.int32)
        return carry

    lax.fori_loop(0, PEER_TOPK, step, 0)


def _peer_topk_kernel(q_ref, k_ref, idx_ref, gate_ref,
                      slab_ref, s1_ref, s2_ref, t1_ref, i1_ref, t2_ref, i2_ref, cand_ref, pay_ref, ts_ref, ex_ref):
    dn = (((1,), (1,)), ((), ()))
    keys = tuple(range(PEER_KEYS))

    def head(h, carry):
        for w, s_ref in ((0, s1_ref), (1, s2_ref)):
            for j in range(SUBLANES):
                slab_ref[j * KEY_PITCH:j * KEY_PITCH + PEER_KEYS, :] = lax.dot_general(
                    k_ref[w, h], q_ref[h, j * LANES:(j + 1) * LANES, :].astype(BF16), dn,
                    preferred_element_type=F32)
            for k in range(PEER_KEYS):
                s_ref[k] = slab_ref[pl.ds(k, SUBLANES, stride=KEY_PITCH), :]
        _extract16([dict(s=s1_ref, order=keys, vals=t1_ref, picks=i1_ref),
                    dict(s=s2_ref, order=keys, vals=t2_ref, picks=i2_ref)])
        for c, (a, b) in enumerate(_CANDIDATES):
            cand_ref[c] = t1_ref[a] + t2_ref[b]
            pay_ref[c] = i1_ref[a] * PEER_KEYS + i2_ref[b]
        _extract16([dict(s=cand_ref, order=tuple(a * PEER_TOPK + b for a, b in _CANDIDATES), pay=pay_ref,
                         vals=ts_ref, picks=ex_ref)])
        top_s = ts_ref[...]
        e = jnp.exp(top_s - jnp.max(top_s, axis=0, keepdims=True))
        gate_ref[0, h] = e / jnp.sum(e, axis=0, keepdims=True)
        idx_ref[0, h] = ex_ref[...] * ROWS_PER_EXPERT
        return carry

    lax.fori_loop(0, PEER_HEADS, head, 0)


def _peer_topk(qp, keys_pad):
    n_tok = qp.shape[1]
    tt = TOPK_TOKENS
    assert n_tok % tt == 0
    shp = (n_tok // tt, PEER_HEADS, PEER_TOPK, SUBLANES, LANES)
    out = pl.BlockSpec((1,) + shp[1:], lambda i: (i, 0, 0, 0, 0))
    vregs = lambda n, dt: pltpu.VMEM((n, SUBLANES, LANES), dt)
    idx, gate = pl.pallas_call(
        _peer_topk_kernel,
        grid=(n_tok // tt,),
        in_specs=[pl.BlockSpec((PEER_HEADS, tt, qp.shape[2]), lambda i: (0, i, 0)),
                  pl.BlockSpec(keys_pad.shape, lambda i: (0, 0, 0, 0))],
        out_specs=[out, out],
        out_shape=[jax.ShapeDtypeStruct(shp, jnp.int32), jax.ShapeDtypeStruct(shp, F32)],
        scratch_shapes=[pltpu.VMEM((SUBLANES * KEY_PITCH, LANES), F32),
                        vregs(PEER_KEYS, F32), vregs(PEER_KEYS, F32),
                        vregs(PEER_TOPK, F32), vregs(PEER_TOPK, jnp.int32),
                        vregs(PEER_TOPK, F32), vregs(PEER_TOPK, jnp.int32),
                        vregs(len(_CANDIDATES), F32), vregs(len(_CANDIDATES), jnp.int32),
                        vregs(PEER_TOPK, F32), vregs(PEER_TOPK, jnp.int32)],
        compiler_params=_cparams(("parallel",), VMEM_LIMIT),
        name="peer_topk",
    )(qp, keys_pad)
    to_tok = lambda a: jnp.transpose(a, (0, 3, 4, 1, 2)).reshape(n_tok, PEER_SEL)
    return to_tok(idx), to_tok(gate)


ROWS_PER_EXPERT = 4
PAIR_CHUNK = 32
CHUNK_ROWS = PAIR_CHUNK * ROWS_PER_EXPERT
SMEM_GROUP = 8
N_CHUNKS = PEER_SEL // PAIR_CHUNK
TOKENS_PER_STEP = 16


def _pack_table_kernel(t_ref, o_ref):
    rows, dm = t_ref.shape
    for c in range(ROWS_PER_EXPERT):
        lo = lax.bitcast_convert_type(t_ref[:, c * LANES:(c + 1) * LANES].astype(BF16).astype(F32), jnp.int32)
        hi = lax.bitcast_convert_type(
            t_ref[:, dm // 2 + c * LANES:dm // 2 + (c + 1) * LANES].astype(BF16).astype(F32), jnp.int32)
        o_ref[pl.ds(c, rows, stride=ROWS_PER_EXPERT), :] = lax.shift_right_logical(lo, 16) | hi


def _pack_table(tables, layer, *, rows=512):
    _, e, dm = tables.shape
    assert dm == 2 * ROWS_PER_EXPERT * LANES
    rows = min(rows, e)
    return pl.pallas_call(
        _pack_table_kernel,
        grid=(e // rows,),
        in_specs=[pl.BlockSpec((None, rows, dm), lambda i: (layer, i, 0))],
        out_specs=pl.BlockSpec((rows * ROWS_PER_EXPERT, LANES), lambda i: (i, 0)),
        out_shape=jax.ShapeDtypeStruct((e * ROWS_PER_EXPERT, LANES), jnp.int32),
        compiler_params=_cparams(("parallel",), VMEM_LIMIT),
        name="pack_table",
    )(tables)


def _unpack_words(w):
    lo = lax.bitcast_convert_type(w << 16, F32)
    hi = lax.bitcast_convert_type(w & jnp.int32(-65536), F32)
    return lo, hi


def _gather_chunk(idx_ref, tab_ref, buf_ref, c):
    for g in range(PAIR_CHUNK // SMEM_GROUP):
        ids = idx_ref.at[pl.ds(pl.multiple_of(c * PAIR_CHUNK + g * SMEM_GROUP, SMEM_GROUP), SMEM_GROUP)]
        for i in range(SMEM_GROUP):
            e4 = pl.multiple_of(ids[i], ROWS_PER_EXPERT)
            s = (g * SMEM_GROUP + i) * ROWS_PER_EXPERT
            buf_ref[s:s + ROWS_PER_EXPERT, :] = tab_ref[pl.ds(e4, ROWS_PER_EXPERT), :]


def _peer_u_kernel(idx_ref, h_ref, gate_ref, tab_ref, coef_ref, *scratch, tt):
    bufs, m_ref = (scratch[:N_CHUNKS], scratch[N_CHUNKS:2 * N_CHUNKS]), scratch[-1]
    n_chunks = tt * N_CHUNKS
    row = lax.broadcasted_iota(jnp.int32, (SUBLANES, CHUNK_ROWS), 0)
    col = lax.broadcasted_iota(jnp.int32, (SUBLANES, CHUNK_ROWS), 1)
    quarter = (col % ROWS_PER_EXPERT) == (row % ROWS_PER_EXPERT)
    m_lo = jnp.logical_and(row < ROWS_PER_EXPERT, quarter)
    m_hi = jnp.logical_and(row >= ROWS_PER_EXPERT, quarter)
    pr = lax.broadcasted_iota(jnp.int32, (CHUNK_ROWS, LANES), 0) // ROWS_PER_EXPERT
    pc = lax.broadcasted_iota(jnp.int32, (CHUNK_ROWS, LANES), 1)
    place = [(pr + part * PAIR_CHUNK == pc).astype(BF16) for part in range(N_CHUNKS)]
    dn = (((1,), (1,)), ((), ()))
    gather = functools.partial(_gather_chunk, idx_ref, tab_ref)

    def chunk_dots(t, part, buf_ref):
        rows = pl.ds(pl.multiple_of(t * SUBLANES, SUBLANES), SUBLANES)
        x = h_ref[rows, :].astype(BF16)
        lo, hi = _unpack_words(buf_ref[...])
        b = jnp.concatenate([lo.astype(BF16), hi.astype(BF16)], axis=0)
        out = lax.dot_general(x, b, dn, preferred_element_type=F32)
        m = jnp.where(m_lo, out[:, :CHUNK_ROWS], 0.0) + jnp.where(m_hi, out[:, CHUNK_ROWS:], 0.0)
        m_ref[part, pl.ds(t, 1), :] = jnp.sum(m, axis=0, keepdims=True)

    for j in range(N_CHUNKS):
        gather(bufs[0][j], j)

    def body(i, carry):
        for s in range(TOKENS_PER_STEP):
            half = s % 2
            t = TOKENS_PER_STEP * i + s
            for j in range(N_CHUNKS):
                chunk_dots(t, j, bufs[half][j])
            for j in range(N_CHUNKS):
                gather(bufs[1 - half][j], jnp.minimum((t + 1) * N_CHUNKS + j, n_chunks - 1))
        return carry

    lax.fori_loop(0, tt // TOKENS_PER_STEP, body, 0)

    dots = None
    for part in range(N_CHUNKS):
        m = m_ref[part]
        m_top = m.astype(BF16)
        m_rest = (m - m_top.astype(F32)).astype(BF16)
        s = (jnp.dot(m_top, place[part], preferred_element_type=F32)
             + jnp.dot(m_rest, place[part], preferred_element_type=F32))
        dots = s if dots is None else dots + s
    coef = (gate_ref[...] * _gelu(dots)).astype(BF16)
    for part in range(N_CHUNKS):
        coef_ref[part] = lax.dot_general(coef, place[part], dn, preferred_element_type=F32)


def _peer_u(idx4, h8, gates, tab, *, tt=128):
    n_tok = gates.shape[0]
    tt = min(tt, n_tok)
    assert TOKENS_PER_STEP % 2 == 0 and tt % TOKENS_PER_STEP == 0
    tok2 = pl.BlockSpec((tt, PEER_SEL), lambda i: (i, 0))
    return pl.pallas_call(
        functools.partial(_peer_u_kernel, tt=tt),
        grid=(n_tok // tt,),
        in_specs=[pl.BlockSpec((tt * PEER_SEL,), lambda i: (i,), memory_space=pltpu.SMEM),
                  pl.BlockSpec((tt * SUBLANES, LANES), lambda i: (i, 0)), tok2,
                  pl.BlockSpec(tab.shape, lambda i: (0, 0), pipeline_mode=pl.Buffered(1))],
        out_specs=pl.BlockSpec((N_CHUNKS, tt, CHUNK_ROWS), lambda i: (0, i, 0)),
        out_shape=jax.ShapeDtypeStruct((N_CHUNKS, n_tok, CHUNK_ROWS), F32),
        scratch_shapes=[pltpu.VMEM((CHUNK_ROWS, LANES), jnp.int32)] * (2 * N_CHUNKS)
                       + [pltpu.VMEM((N_CHUNKS, tt, LANES), F32)],
        compiler_params=_cparams(("arbitrary",), VMEM_LIMIT),
        name="peer_u",
    )(idx4, h8, gates, tab)


def _peer_v_kernel(idx_ref, cx_ref, tab_ref, o_ref, *scratch, tt):
    bufs, parts_ref = (scratch[:N_CHUNKS], scratch[N_CHUNKS:2 * N_CHUNKS]), scratch[2 * N_CHUNKS]
    n_chunks = tt * N_CHUNKS
    row = lax.broadcasted_iota(jnp.int32, (SUBLANES, CHUNK_ROWS), 0)
    col = lax.broadcasted_iota(jnp.int32, (SUBLANES, CHUNK_ROWS), 1)
    quarter = (col % ROWS_PER_EXPERT) == (row % ROWS_PER_EXPERT)
    m_lo = jnp.logical_and(row < ROWS_PER_EXPERT, quarter)
    m_hi = jnp.logical_and(row >= ROWS_PER_EXPERT, quarter)
    gather = functools.partial(_gather_chunk, idx_ref, tab_ref)

    def chunk_sum(t, parity, part, buf_ref):
        cx = jnp.broadcast_to(cx_ref[part, pl.ds(t, 1), :], (SUBLANES, CHUNK_ROWS))
        a = jnp.concatenate([jnp.where(m_lo, cx, 0.0), jnp.where(m_hi, cx, 0.0)], axis=1).astype(BF16)
        lo, hi = _unpack_words(buf_ref[...])
        b = jnp.concatenate([lo.astype(BF16), hi.astype(BF16)], axis=0)
        parts_ref[parity, part] = jnp.dot(a, b, preferred_element_type=F32)

    parts_ref[...] = jnp.zeros_like(parts_ref)
    for j in range(N_CHUNKS):
        gather(bufs[0][j], j)

    def body(i, carry):
        for s in range(TOKENS_PER_STEP):
            half = s % 2
            t = TOKENS_PER_STEP * i + s
            prev = jnp.maximum(t - 1, 0)
            o_ref[prev] = _tree(jnp.add, [parts_ref[1 - half, j] for j in range(N_CHUNKS)])
            for j in range(N_CHUNKS):
                chunk_sum(t, half, j, bufs[half][j])
            for j in range(N_CHUNKS):
                gather(bufs[1 - half][j], jnp.minimum((t + 1) * N_CHUNKS + j, n_chunks - 1))
        return carry

    lax.fori_loop(0, tt // TOKENS_PER_STEP, body, 0)
    o_ref[tt - 1] = _tree(jnp.add, [parts_ref[1, j] for j in range(N_CHUNKS)])


def _peer_v(idx4, cx, tab, *, tt=128):
    n_tok = idx4.shape[0] // PEER_SEL
    tt = min(tt, n_tok)
    assert TOKENS_PER_STEP % 2 == 0 and tt % TOKENS_PER_STEP == 0
    smem = pl.BlockSpec((tt * PEER_SEL,), lambda i: (i,), memory_space=pltpu.SMEM)
    return pl.pallas_call(
        functools.partial(_peer_v_kernel, tt=tt),
        grid=(n_tok // tt,),
        in_specs=[smem, pl.BlockSpec((N_CHUNKS, tt, CHUNK_ROWS), lambda i: (0, i, 0)),
                  pl.BlockSpec(tab.shape, lambda i: (0, 0), pipeline_mode=pl.Buffered(1))],
        out_specs=pl.BlockSpec((tt, SUBLANES, LANES), lambda i: (i, 0, 0)),
        out_shape=jax.ShapeDtypeStruct((n_tok, SUBLANES, LANES), F32),
        scratch_shapes=[pltpu.VMEM((CHUNK_ROWS, LANES), jnp.int32)] * (2 * N_CHUNKS)
                       + [pltpu.VMEM((2, N_CHUNKS, SUBLANES, LANES), F32)],
        compiler_params=_cparams(("arbitrary",), VMEM_LIMIT),
        name="peer_v",
    )(idx4, cx, tab)


def _peer(h8, qp, sub_keys, tab_u, tab_v):
    n_tok = qp.shape[1]
    half = sub_keys.shape[-1]
    z = jnp.zeros_like(sub_keys[0])
    keys_pad = jnp.stack([jnp.concatenate([sub_keys[0], z], axis=-1),
                          jnp.concatenate([z, sub_keys[1]], axis=-1)]).astype(BF16)
    assert keys_pad.shape[-1] == 2 * half == qp.shape[-1]
    idx4, gates = _peer_topk(qp, keys_pad)
    idx4 = idx4.reshape(n_tok * PEER_SEL)
    coef = _peer_u(idx4, h8, gates, tab_u)
    out = _peer_v(idx4, coef, tab_v)
    return out.reshape(n_tok * SUBLANES, LANES)


def _final_kernel(x_ref, p_ref, g_ref, o_ref):
    o_ref[...] = _rms(x_ref[...] + _rows_from_chunks(p_ref, x_ref.shape[0]), g_ref[...])


def _final_norm(x, p, g, *, tr=1024):
    rows, dm = x.shape
    tr = min(tr, rows)
    blk = pl.BlockSpec((tr, dm), lambda i: (i, 0))
    return pl.pallas_call(
        _final_kernel,
        grid=(rows // tr,),
        in_specs=[blk, pl.BlockSpec((tr * SUBLANES, LANES), lambda i: (i, 0)),
                  pl.BlockSpec((1, dm), lambda i: (0, 0))],
        out_specs=blk,
        out_shape=jax.ShapeDtypeStruct((rows, dm), F32),
        compiler_params=_cparams(("parallel",), VMEM_LIMIT),
        name="final_norm",
    )(x, p, g.reshape(1, dm))


def kernel(x, w_in, w_out, rel_bias, g_attn, g_ssm, norm_mix, norm_ffn, lam_re, lam_im, log_step, b_re, b_im, c_re, c_im, d_skip, w_glu, w_query, sub_keys, expert_u, expert_v, norm_final):
    bsz, seq, dm = x.shape
    depth = w_in.shape[0]
    prev = None
    biases = [_attn_bias_tables(rel_bias, d) for _, d in DILATED_PATTERNS]
    for l in range(depth):
        x, q, k, v, u_tm = _in_proj(x, prev, norm_mix[l], w_in[l].astype(BF16))
        attn = _attention(q, k, v, biases)
        a, bw, cw = _ssm_params(lam_re[l], lam_im[l], log_step[l], b_re[l], b_im[l], c_re[l], c_im[l])
        y = _ssm_scan(u_tm, a, bw, cw, bsz=bsz)
        wo = w_out[l].astype(BF16)
        z = _ssm_post(y, u_tm, d_skip[l], w_glu[l].astype(BF16), g_ssm[l], wo[ATTN_WIDTH:], bsz=bsz)
        x, h, qp = _mix_out(x, z, attn,
                            g_attn[l], wo[:ATTN_WIDTH], norm_ffn[l], w_query[l].astype(BF16))
        prev = _peer(h, qp, sub_keys[l], _pack_table(expert_u, l), _pack_table(expert_v, l))
    out = _final_norm(x.reshape(bsz * seq, dm), prev, norm_final)
    return out.reshape(bsz, seq, dm)
```

```python
import functools
import math

import numpy as np
import jax
import jax.numpy as jnp
from jax import lax
from jax.experimental import pallas as pl
from jax.experimental.pallas import tpu as pltpu

F32 = jnp.float32
BF16 = jnp.bfloat16

EPS = 1e-6
NEG_INF = -1e30
HEAD_DIM = 64
ATTN_WIDTH = 512
SSM_WIDTH = 512
SSM_GROUP = 16
SSM_STATE = 64
DILATED_PATTERNS = ((128, 1), (512, 4), (2048, 16))
REL_BUCKETS = 32
REL_MAX_DISTANCE = 1024
PEER_HEADS = 8
PEER_KEYS = 128
PEER_TOPK = 16
PEER_SEL = PEER_HEADS * PEER_TOPK

LANES = 128
SUBLANES = 8
QBLK = 128
KWIN = 256
BAND = 64
VMEM_LIMIT = 52 * 1024 * 1024


def _cparams(sem, vmem=None):
    return pltpu.CompilerParams(dimension_semantics=sem, vmem_limit_bytes=vmem)


def _rms(x, g):
    return x * lax.rsqrt(jnp.mean(x * x, axis=-1, keepdims=True) + EPS) * g


def _gelu(x):
    return 0.5 * x * (1.0 + lax.erf(x * (1.0 / math.sqrt(2.0))))


def _rows_from_chunks(p_ref, n_rows):
    return jnp.concatenate([p_ref[pl.ds(c, n_rows, stride=SUBLANES), :] for c in range(SUBLANES)], axis=-1)


def _in_proj_kernel(*refs, has_prev):
    if has_prev:
        x_ref, p_ref, g_ref, w_ref, xo_ref, q_ref, k_ref, v_ref, u_ref = refs
        x = x_ref[0] + _rows_from_chunks(p_ref, x_ref.shape[1])
    else:
        x_ref, g_ref, w_ref, xo_ref, q_ref, k_ref, v_ref, u_ref = refs
        x = x_ref[0]
    xo_ref[0] = x
    h = _rms(x, g_ref[...]).astype(BF16)
    proj = jnp.dot(h, w_ref[...], preferred_element_type=F32)
    a = ATTN_WIDTH
    q_ref[0] = proj[:, :a] * (HEAD_DIM ** -0.5)
    k_ref[0] = proj[:, a:2 * a]
    v_ref[0] = proj[:, 2 * a:3 * a]
    b, bsz = pl.program_id(1), pl.num_programs(1)
    for c in range(SSM_WIDTH // LANES):
        u_ref[c, pl.ds(b, proj.shape[0], stride=bsz), :] = proj[:, 3 * a + c * LANES:3 * a + (c + 1) * LANES]


def _in_proj(x, prev, g, w_bf16, *, ts=256):
    bsz, seq, dm = x.shape
    ts = min(ts, seq)
    row = pl.BlockSpec((1, ts, dm), lambda i, b: (b, i, 0))
    qkv = pl.BlockSpec((1, ts, ATTN_WIDTH), lambda i, b: (b, i, 0))
    ns = seq // ts
    chunks = pl.BlockSpec((ts * SUBLANES, LANES), lambda i, b: (b * ns + i, 0))
    ins = [x] + ([prev] if prev is not None else []) + [g.reshape(1, dm), w_bf16]
    in_specs = [row] + ([chunks] if prev is not None else []) + [
        pl.BlockSpec((1, dm), lambda i, b: (0, 0)),
        pl.BlockSpec(w_bf16.shape, lambda i, b: (0, 0)),
    ]
    slabs = SSM_WIDTH // LANES
    return pl.pallas_call(
        functools.partial(_in_proj_kernel, has_prev=prev is not None),
        grid=(seq // ts, bsz),
        in_specs=in_specs,
        out_specs=[row, qkv, qkv, qkv, pl.BlockSpec((slabs, ts * bsz, LANES), lambda i, b: (0, i, 0))],
        out_shape=[
            jax.ShapeDtypeStruct((bsz, seq, dm), F32),
            jax.ShapeDtypeStruct((bsz, seq, ATTN_WIDTH), F32),
            jax.ShapeDtypeStruct((bsz, seq, ATTN_WIDTH), F32),
            jax.ShapeDtypeStruct((bsz, seq, ATTN_WIDTH), F32),
            jax.ShapeDtypeStruct((slabs, seq * bsz, LANES), F32),
        ],
        compiler_params=_cparams(("parallel", "arbitrary"), VMEM_LIMIT),
        name="in_proj",
    )(*ins)


def _t5_buckets(rel):
    half = REL_BUCKETS // 2
    max_exact = half // 2
    n = np.abs(rel)
    large = max_exact + (np.log(np.maximum(n, 1) / max_exact)
                         / np.log(REL_MAX_DISTANCE / max_exact) * (half - max_exact)).astype(np.int32)
    large = np.minimum(large, half - 1)
    return (np.where(rel > 0, half, 0) + np.where(n < max_exact, n, large)).astype(np.int32)


def _attn_bias_tables(rel_bias, dilation):
    ql = np.arange(QBLK)[:, None]
    kl = np.arange(KWIN)[None, :]
    delta = np.stack([kl + off - ql for off in (0, -BAND, -2 * BAND)])
    buckets = np.where(np.abs(delta) <= BAND, _t5_buckets(delta * dilation), -1)
    rb = rel_bias.astype(F32).T
    bk = jnp.asarray(buckets, jnp.int32)[None]
    tab = jnp.full((rb.shape[0],) + buckets.shape, NEG_INF, F32)
    for b in range(REL_BUCKETS):
        tab = jnp.where(bk == b, rb[:, b][:, None, None, None], tab)
    return tab


BLOCKS_PER_STEP = 8


def _attn_kernel(q_ref, k_ref, v_ref, *rest, seq):
    bias_refs, (o_ref, acc_ref, m_ref, z_ref) = rest[:len(DILATED_PATTERNS)], rest[len(DILATED_PATTERNS):]
    lane = lax.broadcasted_iota(jnp.int32, (QBLK, LANES), 1)
    is_h0 = lane < HEAD_DIM
    dn = (((1,), (1,)), ((), ()))
    nsteps = seq // QBLK

    def rows(start, size, d):
        return pl.ds(start, size) if d == 1 else pl.ds(start, size, stride=d)

    def block(n, d, bias_ref, first, last):
        length = seq // d
        nblk = length // QBLK
        r, i = n >> (nblk.bit_length() - 1), n & (nblk - 1)
        s = i * QBLK
        ks = jnp.clip(s - BAND, 0, length - KWIN)
        var = jnp.where(i == 0, 0, jnp.where(i == nblk - 1, 2, 1))
        q_rows = rows(r + d * s, QBLK, d)
        k_rows = rows(r + d * ks, KWIN, d)
        qb = q_ref[q_rows, :].astype(BF16)
        kb = k_ref[k_rows, :].astype(BF16)
        vb = v_ref[k_rows, :].astype(BF16)
        outs, ms, zs = [], [], []
        for h in range(2):
            keep = is_h0 if h == 0 else jnp.logical_not(is_h0)
            qh = jnp.where(keep, qb, jnp.zeros_like(qb))
            logits = lax.dot_general(qh, kb, dn, preferred_element_type=F32) + bias_ref[h, var]
            m = jnp.max(logits, axis=-1, keepdims=True)
            p = jnp.exp(logits - m)
            outs.append(jnp.dot(p.astype(BF16), vb, preferred_element_type=F32))
            ms.append(jnp.broadcast_to(m, (QBLK, LANES)))
            zs.append(jnp.broadcast_to(jnp.sum(p, axis=-1, keepdims=True), (QBLK, LANES)))
        o = jnp.where(is_h0, outs[0], outs[1])
        m = jnp.where(is_h0, ms[0], ms[1])
        z = jnp.where(is_h0, zs[0], zs[1])
        if not first:
            m_old = m_ref[q_rows, :]
            m_new = jnp.maximum(m_old, m)
            a, b = jnp.exp(m_old - m_new), jnp.exp(m - m_new)
            o = acc_ref[q_rows, :] * a + o * b
            z = z_ref[q_rows, :] * a + z * b
            m = m_new
        if last:
            o_ref[q_rows, :] = o / z
        else:
            acc_ref[q_rows, :] = o
            m_ref[q_rows, :] = m
            z_ref[q_rows, :] = z

    for p, ((_, d), bias_ref) in enumerate(zip(DILATED_PATTERNS, bias_refs)):
        def step(g, carry, d=d, bias_ref=bias_ref, p=p):
            for j in range(BLOCKS_PER_STEP):
                block(g * BLOCKS_PER_STEP + j, d, bias_ref, p == 0, p == len(DILATED_PATTERNS) - 1)
            return carry
        lax.fori_loop(0, nsteps // BLOCKS_PER_STEP, step, 0)


def _attention(q, k, v, biases):
    bsz, seq, width = q.shape
    for _, d in DILATED_PATTERNS:
        length = seq // d
        assert length >= KWIN and length % QBLK == 0 and (length // QBLK) & (length // QBLK - 1) == 0
    assert (seq // QBLK) % BLOCKS_PER_STEP == 0
    blk = pl.BlockSpec((None, seq, LANES), lambda b, c: (b, 0, c))
    bias_spec = pl.BlockSpec((2, 3, QBLK, KWIN), lambda b, c: (c, 0, 0, 0))
    return pl.pallas_call(
        functools.partial(_attn_kernel, seq=seq),
        grid=(bsz, width // LANES),
        in_specs=[blk, blk, blk] + [bias_spec] * len(biases),
        out_specs=blk,
        out_shape=jax.ShapeDtypeStruct((bsz, seq, width), F32),
        scratch_shapes=[pltpu.VMEM((seq, LANES), F32)] * 3,
        compiler_params=_cparams(("parallel", "arbitrary"), VMEM_LIMIT),
        name="attention",
    )(q, k, v, *biases)


SSM_LANE_GROUPS = SSM_WIDTH // LANES
SSM_GB_STATES = (LANES // SSM_GROUP) * SSM_STATE


def _ssm_params(lam_re, lam_im, log_step, b_re, b_im, c_re, c_im):
    f = lambda t: t.astype(F32)
    lr, li = f(lam_re), f(lam_im)
    step = jnp.exp(f(log_step))[..., None]
    mag = jnp.exp(lr * step)
    ar, ai = mag * jnp.cos(li * step), mag * jnp.sin(li * step)
    nr, ni = ar - 1.0, ai
    den = lr * lr + li * li
    cr, ci = (nr * lr + ni * li) / den, (ni * lr - nr * li) / den
    br, bi = f(b_re), f(b_im)
    bbr = cr[..., None] * br - ci[..., None] * bi
    bbi = cr[..., None] * bi + ci[..., None] * br
    gpb = LANES // SSM_GROUP
    eye = jnp.eye(gpb, dtype=F32)

    def in_map(t):
        t = t.reshape(2, SSM_LANE_GROUPS, gpb, SSM_STATE, SSM_GROUP)
        return jnp.einsum('dbgpc,gh->dbgchp', t, eye).reshape(2, SSM_LANE_GROUPS, LANES, SSM_GB_STATES)

    def out_map(t):
        t = t.reshape(2, SSM_LANE_GROUPS, gpb, SSM_GROUP, SSM_STATE)
        return jnp.einsum('dbgcp,gh->dbgphc', t, eye).reshape(2, SSM_LANE_GROUPS, SSM_GB_STATES, LANES)

    bw = jnp.concatenate([in_map(bbr), in_map(bbi)], axis=-1).astype(BF16)
    cw = jnp.concatenate([out_map(f(c_re)), -out_map(f(c_im))], axis=-2).astype(BF16)
    a = jnp.stack([ar.reshape(2, SSM_LANE_GROUPS, SSM_GB_STATES),
                   ai.reshape(2, SSM_LANE_GROUPS, SSM_GB_STATES)], axis=2)
    return a, bw, cw


def _ssm_kernel(u_ref, a_ref, bw_ref, cw_ref, y_ref, st_ref, bu0_ref, bu1_ref, *, ts, bsz):
    d = pl.program_id(0)
    ns = SSM_GB_STATES
    bu_refs = (bu0_ref, bu1_ref)

    @pl.when(pl.program_id(1) == 0)
    def _():
        st_ref[...] = jnp.zeros_like(st_ref)

    for g0 in range(0, SSM_LANE_GROUPS, 2):
        gbs = (g0, g0 + 1)
        for gb, bu_ref in zip(gbs, bu_refs):
            bu_ref[...] = jnp.dot(u_ref[gb].astype(BF16), bw_ref[0, gb], preferred_element_type=F32)
        coefs = [(jnp.broadcast_to(a_ref[0, gb, 0:1, :], (bsz, ns)), jnp.broadcast_to(a_ref[0, gb, 1:2, :], (bsz, ns)))
                 for gb in gbs]

        def step(j, carry, coefs=coefs):
            tl = jnp.where(d == 0, j, ts - 1 - j)
            r = pl.multiple_of(tl * bsz, bsz)
            out = []
            for (ar, ai), bu_ref, (xr, xi) in zip(coefs, bu_refs, carry):
                nr = ar * xr - ai * xi + bu_ref[pl.ds(r, bsz), :ns]
                ni = ar * xi + ai * xr + bu_ref[pl.ds(r, bsz), ns:]
                bu_ref[pl.ds(r, bsz), :ns] = nr
                bu_ref[pl.ds(r, bsz), ns:] = ni
                out.append((nr, ni))
            return tuple(out)

        final = lax.fori_loop(0, ts, step, tuple((st_ref[gb, :, :ns], st_ref[gb, :, ns:]) for gb in gbs))
        for gb, bu_ref, (xr, xi) in zip(gbs, bu_refs, final):
            st_ref[gb, :, :ns] = xr
            st_ref[gb, :, ns:] = xi
            y_ref[0, :, gb * LANES:(gb + 1) * LANES] = jnp.dot(
                bu_ref[...].astype(BF16), cw_ref[0, gb], preferred_element_type=F32)


def _ssm_scan(u_tm, a, bw, cw, *, bsz, ts=64):
    slabs, rows, _ = u_tm.shape
    width = slabs * LANES
    seq = rows // bsz
    ts = min(ts, seq)
    nt = seq // ts
    tblk = lambda d, i: jnp.where(d == 0, i, nt - 1 - i)
    return pl.pallas_call(
        functools.partial(_ssm_kernel, ts=ts, bsz=bsz),
        grid=(2, nt),
        in_specs=[
            pl.BlockSpec((slabs, ts * bsz, LANES), lambda d, i: (0, tblk(d, i), 0)),
            pl.BlockSpec((1,) + a.shape[1:], lambda d, i: (d, 0, 0, 0)),
            pl.BlockSpec((1,) + bw.shape[1:], lambda d, i: (d, 0, 0, 0)),
            pl.BlockSpec((1,) + cw.shape[1:], lambda d, i: (d, 0, 0, 0)),
        ],
        out_specs=pl.BlockSpec((1, ts * bsz, width), lambda d, i: (d, tblk(d, i), 0)),
        out_shape=jax.ShapeDtypeStruct((2, rows, width), F32),
        scratch_shapes=[pltpu.VMEM((SSM_LANE_GROUPS, bsz, 2 * SSM_GB_STATES), F32),
                        pltpu.VMEM((ts * bsz, 2 * SSM_GB_STATES), F32),
                        pltpu.VMEM((ts * bsz, 2 * SSM_GB_STATES), F32)],
        compiler_params=_cparams(("arbitrary", "arbitrary"), VMEM_LIMIT),
        name="ssm_scan",
    )(u_tm, a, bw, cw)


def _ssm_post_kernel(y_ref, u_ref, d_ref, wg_ref, g_ref, wo_ref, z_ref, slab_ref, *, bsz):
    u = jnp.concatenate([u_ref[c] for c in range(u_ref.shape[0])], axis=-1)
    y = _gelu(y_ref[0] + y_ref[1] + d_ref[...] * u).astype(BF16)
    ab = jnp.dot(y, wg_ref[...], preferred_element_type=F32)
    ssm = ab[:, :SSM_WIDTH] * jax.nn.sigmoid(ab[:, SSM_WIDTH:])
    n = _rms(ssm, g_ref[...]).astype(BF16)
    z = jnp.dot(n, wo_ref[...], preferred_element_type=F32)
    nt, dm = z.shape[0] // bsz, z.shape[1]
    for c in range(dm // LANES):
        slab_ref[c] = z[:, c * LANES:(c + 1) * LANES]
    for b in range(bsz):
        for c in range(dm // LANES):
            z_ref[:, b * dm + c * LANES:b * dm + (c + 1) * LANES] = slab_ref[c, pl.ds(b, nt, stride=bsz), :]


def _ssm_post(y, u_tm, d_skip, w_glu_bf16, g_ssm, w_out_ssm_bf16, *, bsz, tr=512):
    slabs, rows, _ = u_tm.shape
    width = slabs * LANES
    tr = min(tr, rows)
    assert tr % bsz == 0
    dm = w_out_ssm_bf16.shape[1]
    full = lambda a: pl.BlockSpec(a.shape, lambda i: (0,) * a.ndim)
    d2, g2 = d_skip.reshape(1, width), g_ssm.reshape(1, width)
    return pl.pallas_call(
        functools.partial(_ssm_post_kernel, bsz=bsz),
        grid=(rows // tr,),
        in_specs=[pl.BlockSpec((2, tr, width), lambda i: (0, i, 0)),
                  pl.BlockSpec((slabs, tr, LANES), lambda i: (0, i, 0)),
                  full(d2), full(w_glu_bf16), full(g2), full(w_out_ssm_bf16)],
        out_specs=pl.BlockSpec((tr // bsz, bsz * dm), lambda i: (i, 0)),
        out_shape=jax.ShapeDtypeStruct((rows // bsz, bsz * dm), F32),
        scratch_shapes=[pltpu.VMEM((dm // LANES, tr, LANES), F32)],
        compiler_params=_cparams(("parallel",), VMEM_LIMIT),
        name="ssm_post",
    )(y, u_tm, d2, w_glu_bf16, g2, w_out_ssm_bf16)


def _mix_out_kernel(x_ref, z_ref, a_ref, ga_ref, wo_ref, gf_ref, wq_ref, xn_ref, h_ref, q_ref):
    n = _rms(a_ref[0], ga_ref[...]).astype(BF16)
    xn = x_ref[0] + z_ref[...] + jnp.dot(n, wo_ref[...], preferred_element_type=F32)
    xn_ref[0] = xn
    h = _rms(xn, gf_ref[...])
    for c in range(SUBLANES):
        h_ref[pl.ds(c, h.shape[0], stride=SUBLANES), :] = h[:, c * LANES:(c + 1) * LANES]
    qp = jnp.dot(h.astype(BF16), wq_ref[...], preferred_element_type=F32)
    for hd in range(PEER_HEADS):
        q_ref[hd] = qp[:, hd * LANES:(hd + 1) * LANES]


def _mix_out(x, z_tm, attn, g_attn, w_out_attn_bf16, norm_ffn, w_query, *, ts=256):
    bsz, seq, dm = x.shape
    ts = min(ts, seq)
    ns = seq // ts
    row = pl.BlockSpec((1, ts, dm), lambda b, i: (b, i, 0))
    half = pl.BlockSpec((1, ts, ATTN_WIDTH), lambda b, i: (b, i, 0))
    full = lambda a: pl.BlockSpec(a.shape, lambda b, i: (0,) * a.ndim)
    ga, gf = g_attn.reshape(1, ATTN_WIDTH), norm_ffn.reshape(1, dm)
    qdim = w_query.shape[1] // PEER_HEADS
    return pl.pallas_call(
        _mix_out_kernel,
        grid=(bsz, ns),
        in_specs=[row, pl.BlockSpec((ts, dm), lambda b, i: (i, b)), half,
                  full(ga), full(w_out_attn_bf16), full(gf), full(w_query)],
        out_specs=[row, pl.BlockSpec((ts * SUBLANES, LANES), lambda b, i: (b * ns + i, 0)),
                   pl.BlockSpec((PEER_HEADS, ts, qdim), lambda b, i: (0, b * ns + i, 0))],
        out_shape=[jax.ShapeDtypeStruct((bsz, seq, dm), F32),
                   jax.ShapeDtypeStruct((bsz * seq * SUBLANES, LANES), F32),

                   jax.ShapeDtypeStruct((PEER_HEADS, bsz * seq, qdim), F32)],
        compiler_params=_cparams(("parallel", "arbitrary"), VMEM_LIMIT),
        name="mix_out",
    )(x, z_tm, attn, ga, w_out_attn_bf16, gf, w_query)


TOPK_TOKENS = SUBLANES * LANES
KEY_PITCH = PEER_KEYS + 4
_CANDIDATES = tuple((a, b) for a in range(PEER_TOPK) for b in range(PEER_TOPK) if (a + 1) * (b + 1) <= PEER_TOPK)


def _tree(op, xs):
    xs = list(xs)
    while len(xs) > 1:
        xs = [op(xs[i], xs[i + 1]) if i + 1 < len(xs) else xs[i] for i in range(0, len(xs), 2)]
    return xs[0]


def _extract16(problems):
    ninf = jnp.float32(-jnp.inf)

    def better(a, b):
        gt = b[0] > a[0]
        return tuple(jnp.where(gt, y, x) for x, y in zip(a, b))

    def step(r, carry):
        for p in problems:
            s_ref, order = p["s"], p["order"]
            assert list(order) == sorted(order)
            n = len(order)
            rows = [(s_ref[k], order[k]) + ((p["pay"][k],) if p.get("pay") is not None else ())
                    for k in range(n)]
            win = _tree(better, rows)
            m, am = win[0], win[1]
            for k in range(n):
                s_ref[k] = jnp.where(am == order[k], ninf, s_ref[k])
            p["vals"][r] = m
            p["picks"][r] = win[2] if len(win) > 2 else am.astype(jnp.int32)
        return carry

    lax.fori_loop(0, PEER_TOPK, step, 0)


def _peer_topk_kernel(q_ref, k_ref, idx_ref, gate_ref,
                      slab_ref, s1_ref, s2_ref, t1_ref, i1_ref, t2_ref, i2_ref, cand_ref, pay_ref, ts_ref, ex_ref):
    dn = (((1,), (1,)), ((), ()))
    keys = tuple(range(PEER_KEYS))

    def head(h, carry):
        for w, s_ref in ((0, s1_ref), (1, s2_ref)):
            for j in range(SUBLANES):
                slab_ref[j * KEY_PITCH:j * KEY_PITCH + PEER_KEYS, :] = lax.dot_general(
                    k_ref[w, h], q_ref[h, j * LANES:(j + 1) * LANES, :].astype(BF16), dn,
                    preferred_element_type=F32)
            for k in range(PEER_KEYS):
                s_ref[k] = slab_ref[pl.ds(k, SUBLANES, stride=KEY_PITCH), :]
        _extract16([dict(s=s1_ref, order=keys, vals=t1_ref, picks=i1_ref),
                    dict(s=s2_ref, order=keys, vals=t2_ref, picks=i2_ref)])
        for c, (a, b) in enumerate(_CANDIDATES):
            cand_ref[c] = t1_ref[a] + t2_ref[b]
            pay_ref[c] = i1_ref[a] * PEER_KEYS + i2_ref[b]
        _extract16([dict(s=cand_ref, order=tuple(a * PEER_TOPK + b for a, b in _CANDIDATES), pay=pay_ref,
                         vals=ts_ref, picks=ex_ref)])
        top_s = ts_ref[...]
        e = jnp.exp(top_s - jnp.max(top_s, axis=0, keepdims=True))
        gate_ref[0, h] = e / jnp.sum(e, axis=0, keepdims=True)
        idx_ref[0, h] = ex_ref[...] * ROWS_PER_EXPERT
        return carry

    lax.fori_loop(0, PEER_HEADS, head, 0)


def _peer_topk(qp, keys_pad):
    n_tok = qp.shape[1]
    tt = TOPK_TOKENS
    assert n_tok % tt == 0
    shp = (n_tok // tt, PEER_HEADS, PEER_TOPK, SUBLANES, LANES)
    out = pl.BlockSpec((1,) + shp[1:], lambda i: (i, 0, 0, 0, 0))
    vregs = lambda n, dt: pltpu.VMEM((n, SUBLANES, LANES), dt)
    idx, gate = pl.pallas_call(
        _peer_topk_kernel,
        grid=(n_tok // tt,),
        in_specs=[pl.BlockSpec((PEER_HEADS, tt, qp.shape[2]), lambda i: (0, i, 0)),
                  pl.BlockSpec(keys_pad.shape, lambda i: (0, 0, 0, 0))],
        out_specs=[out, out],
        out_shape=[jax.ShapeDtypeStruct(shp, jnp.int32), jax.ShapeDtypeStruct(shp, F32)],
        scratch_shapes=[pltpu.VMEM((SUBLANES * KEY_PITCH, LANES), F32),
                        vregs(PEER_KEYS, F32), vregs(PEER_KEYS, F32),
                        vregs(PEER_TOPK, F32), vregs(PEER_TOPK, jnp.int32),
                        vregs(PEER_TOPK, F32), vregs(PEER_TOPK, jnp.int32),
                        vregs(len(_CANDIDATES), F32), vregs(len(_CANDIDATES), jnp.int32),
                        vregs(PEER_TOPK, F32), vregs(PEER_TOPK, jnp.int32)],
        compiler_params=_cparams(("parallel",), VMEM_LIMIT),
        name="peer_topk",
    )(qp, keys_pad)
    to_tok = lambda a: jnp.transpose(a, (0, 3, 4, 1, 2)).reshape(n_tok, PEER_SEL)
    return to_tok(idx), to_tok(gate)


ROWS_PER_EXPERT = 4
PAIR_CHUNK = 32
CHUNK_ROWS = PAIR_CHUNK * ROWS_PER_EXPERT
SMEM_GROUP = 8
N_CHUNKS = PEER_SEL // PAIR_CHUNK
TOKENS_PER_STEP = 16


def _pack_table_kernel(t_ref, o_ref):
    rows, dm = t_ref.shape
    for c in range(ROWS_PER_EXPERT):
        lo = lax.bitcast_convert_type(t_ref[:, c * LANES:(c + 1) * LANES].astype(BF16).astype(F32), jnp.int32)
        hi = lax.bitcast_convert_type(
            t_ref[:, dm // 2 + c * LANES:dm // 2 + (c + 1) * LANES].astype(BF16).astype(F32), jnp.int32)
        o_ref[pl.ds(c, rows, stride=ROWS_PER_EXPERT), :] = lax.shift_right_logical(lo, 16) | hi


def _pack_table(tables, layer, *, rows=512):
    _, e, dm = tables.shape
    assert dm == 2 * ROWS_PER_EXPERT * LANES
    rows = min(rows, e)
    return pl.pallas_call(
        _pack_table_kernel,
        grid=(e // rows,),
        in_specs=[pl.BlockSpec((None, rows, dm), lambda i: (layer, i, 0))],
        out_specs=pl.BlockSpec((rows * ROWS_PER_EXPERT, LANES), lambda i: (i, 0)),
        out_shape=jax.ShapeDtypeStruct((e * ROWS_PER_EXPERT, LANES), jnp.int32),
        compiler_params=_cparams(("parallel",), VMEM_LIMIT),
        name="pack_table",
    )(tables)


def _unpack_words(w):
    lo = lax.bitcast_convert_type(w << 16, F32)
    hi = lax.bitcast_convert_type(w & jnp.int32(-65536), F32)
    return lo, hi


def _gather_chunk(idx_ref, tab_ref, buf_ref, c):
    for g in range(PAIR_CHUNK // SMEM_GROUP):
        ids = idx_ref.at[pl.ds(pl.multiple_of(c * PAIR_CHUNK + g * SMEM_GROUP, SMEM_GROUP), SMEM_GROUP)]
        for i in range(SMEM_GROUP):
            e4 = pl.multiple_of(ids[i], ROWS_PER_EXPERT)
            s = (g * SMEM_GROUP + i) * ROWS_PER_EXPERT
            buf_ref[s:s + ROWS_PER_EXPERT, :] = tab_ref[pl.ds(e4, ROWS_PER_EXPERT), :]


def _peer_u_kernel(idx_ref, h_ref, gate_ref, tab_ref, coef_ref, *scratch, tt):
    bufs, m_ref = (scratch[:N_CHUNKS], scratch[N_CHUNKS:2 * N_CHUNKS]), scratch[-1]
    n_chunks = tt * N_CHUNKS
    row = lax.broadcasted_iota(jnp.int32, (SUBLANES, CHUNK_ROWS), 0)
    col = lax.broadcasted_iota(jnp.int32, (SUBLANES, CHUNK_ROWS), 1)
    quarter = (col % ROWS_PER_EXPERT) == (row % ROWS_PER_EXPERT)
    m_lo = jnp.logical_and(row < ROWS_PER_EXPERT, quarter)
    m_hi = jnp.logical_and(row >= ROWS_PER_EXPERT, quarter)
    pr = lax.broadcasted_iota(jnp.int32, (CHUNK_ROWS, LANES), 0) // ROWS_PER_EXPERT
    pc = lax.broadcasted_iota(jnp.int32, (CHUNK_ROWS, LANES), 1)
    place = [(pr + part * PAIR_CHUNK == pc).astype(BF16) for part in range(N_CHUNKS)]
    dn = (((1,), (1,)), ((), ()))
    gather = functools.partial(_gather_chunk, idx_ref, tab_ref)

    def chunk_dots(t, part, buf_ref):
        rows = pl.ds(pl.multiple_of(t * SUBLANES, SUBLANES), SUBLANES)
        x = h_ref[rows, :].astype(BF16)
        lo, hi = _unpack_words(buf_ref[...])
        b = jnp.concatenate([lo.astype(BF16), hi.astype(BF16)], axis=0)
        out = lax.dot_general(x, b, dn, preferred_element_type=F32)
        m = jnp.where(m_lo, out[:, :CHUNK_ROWS], 0.0) + jnp.where(m_hi, out[:, CHUNK_ROWS:], 0.0)
        m_ref[part, pl.ds(t, 1), :] = jnp.sum(m, axis=0, keepdims=True)

    for j in range(N_CHUNKS):
        gather(bufs[0][j], j)

    def body(i, carry):
        for s in range(TOKENS_PER_STEP):
            half = s % 2
            t = TOKENS_PER_STEP * i + s
            for j in range(N_CHUNKS):
                chunk_dots(t, j, bufs[half][j])
            for j in range(N_CHUNKS):
                gather(bufs[1 - half][j], jnp.minimum((t + 1) * N_CHUNKS + j, n_chunks - 1))
        return carry

    lax.fori_loop(0, tt // TOKENS_PER_STEP, body, 0)

    dots = None
    for part in range(N_CHUNKS):
        m = m_ref[part]
        m_top = m.astype(BF16)
        m_rest = (m - m_top.astype(F32)).astype(BF16)
        s = (jnp.dot(m_top, place[part], preferred_element_type=F32)
             + jnp.dot(m_rest, place[part], preferred_element_type=F32))
        dots = s if dots is None else dots + s
    coef = (gate_ref[...] * _gelu(dots)).astype(BF16)
    for part in range(N_CHUNKS):
        coef_ref[part] = lax.dot_general(coef, place[part], dn, preferred_element_type=F32)


def _peer_u(idx4, h8, gates, tab, *, tt=128):
    n_tok = gates.shape[0]
    tt = min(tt, n_tok)
    assert TOKENS_PER_STEP % 2 == 0 and tt % TOKENS_PER_STEP == 0
    tok2 = pl.BlockSpec((tt, PEER_SEL), lambda i: (i, 0))
    return pl.pallas_call(
        functools.partial(_peer_u_kernel, tt=tt),
        grid=(n_tok // tt,),
        in_specs=[pl.BlockSpec((tt * PEER_SEL,), lambda i: (i,), memory_space=pltpu.SMEM),
                  pl.BlockSpec((tt * SUBLANES, LANES), lambda i: (i, 0)), tok2,
                  pl.BlockSpec(tab.shape, lambda i: (0, 0), pipeline_mode=pl.Buffered(1))],
        out_specs=pl.BlockSpec((N_CHUNKS, tt, CHUNK_ROWS), lambda i: (0, i, 0)),
        out_shape=jax.ShapeDtypeStruct((N_CHUNKS, n_tok, CHUNK_ROWS), F32),
        scratch_shapes=[pltpu.VMEM((CHUNK_ROWS, LANES), jnp.int32)] * (2 * N_CHUNKS)
                       + [pltpu.VMEM((N_CHUNKS, tt, LANES), F32)],
        compiler_params=_cparams(("arbitrary",), VMEM_LIMIT),
        name="peer_u",
    )(idx4, h8, gates, tab)


def _peer_v_kernel(idx_ref, cx_ref, tab_ref, o_ref, *scratch, tt):
    bufs, parts_ref = (scratch[:N_CHUNKS], scratch[N_CHUNKS:2 * N_CHUNKS]), scratch[2 * N_CHUNKS]
    n_chunks = tt * N_CHUNKS
    row = lax.broadcasted_iota(jnp.int32, (SUBLANES, CHUNK_ROWS), 0)
    col = lax.broadcasted_iota(jnp.int32, (SUBLANES, CHUNK_ROWS), 1)
    quarter = (col % ROWS_PER_EXPERT) == (row % ROWS_PER_EXPERT)
    m_lo = jnp.logical_and(row < ROWS_PER_EXPERT, quarter)
    m_hi = jnp.logical_and(row >= ROWS_PER_EXPERT, quarter)
    gather = functools.partial(_gather_chunk, idx_ref, tab_ref)

    def chunk_sum(t, parity, part, buf_ref):
        cx = jnp.broadcast_to(cx_ref[part, pl.ds(t, 1), :], (SUBLANES, CHUNK_ROWS))
        a = jnp.concatenate([jnp.where(m_lo, cx, 0.0), jnp.where(m_hi, cx, 0.0)], axis=1).astype(BF16)
        lo, hi = _unpack_words(buf_ref[...])
        b = jnp.concatenate([lo.astype(BF16), hi.astype(BF16)], axis=0)
        parts_ref[parity, part] = jnp.dot(a, b, preferred_element_type=F32)

    parts_ref[...] = jnp.zeros_like(parts_ref)
    for j in range(N_CHUNKS):
        gather(bufs[0][j], j)

    def body(i, carry):
        for s in range(TOKENS_PER_STEP):
            half = s % 2
            t = TOKENS_PER_STEP * i + s
            prev = jnp.maximum(t - 1, 0)
            o_ref[prev] = _tree(jnp.add, [parts_ref[1 - half, j] for j in range(N_CHUNKS)])
            for j in range(N_CHUNKS):
                chunk_sum(t, half, j, bufs[half][j])
            for j in range(N_CHUNKS):
                gather(bufs[1 - half][j], jnp.minimum((t + 1) * N_CHUNKS + j, n_chunks - 1))
        return carry

    lax.fori_loop(0, tt // TOKENS_PER_STEP, body, 0)
    o_ref[tt - 1] = _tree(jnp.add, [parts_ref[1, j] for j in range(N_CHUNKS)])


def _peer_v(idx4, cx, tab, *, tt=128):
    n_tok = idx4.shape[0] // PEER_SEL
    tt = min(tt, n_tok)
    assert TOKENS_PER_STEP % 2 == 0 and tt % TOKENS_PER_STEP == 0
    smem = pl.BlockSpec((tt * PEER_SEL,), lambda i: (i,), memory_space=pltpu.SMEM)
    return pl.pallas_call(
        functools.partial(_peer_v_kernel, tt=tt),
        grid=(n_tok // tt,),
        in_specs=[smem, pl.BlockSpec((N_CHUNKS, tt, CHUNK_ROWS), lambda i: (0, i, 0)),
                  pl.BlockSpec(tab.shape, lambda i: (0, 0), pipeline_mode=pl.Buffered(1))],
        out_specs=pl.BlockSpec((tt, SUBLANES, LANES), lambda i: (i, 0, 0)),
        out_shape=jax.ShapeDtypeStruct((n_tok, SUBLANES, LANES), F32),
        scratch_shapes=[pltpu.VMEM((CHUNK_ROWS, LANES), jnp.int32)] * (2 * N_CHUNKS)
                       + [pltpu.VMEM((2, N_CHUNKS, SUBLANES, LANES), F32)],
        compiler_params=_cparams(("arbitrary",), VMEM_LIMIT),
        name="peer_v",
    )(idx4, cx, tab)


def _peer(h8, qp, sub_keys, tab_u, tab_v):
    n_tok = qp.shape[1]
    half = sub_keys.shape[-1]
    z = jnp.zeros_like(sub_keys[0])
    keys_pad = jnp.stack([jnp.concatenate([sub_keys[0], z], axis=-1),
                          jnp.concatenate([z, sub_keys[1]], axis=-1)]).astype(BF16)
    assert keys_pad.shape[-1] == 2 * half == qp.shape[-1]
    idx4, gates = _peer_topk(qp, keys_pad)
    idx4 = idx4.reshape(n_tok * PEER_SEL)
    coef = _peer_u(idx4, h8, gates, tab_u)
    out = _peer_v(idx4, coef, tab_v)
    return out.reshape(n_tok * SUBLANES, LANES)


def _final_kernel(x_ref, p_ref, g_ref, o_ref):
    o_ref[...] = _rms(x_ref[...] + _rows_from_chunks(p_ref, x_ref.shape[0]), g_ref[...])


def _final_norm(x, p, g, *, tr=1024):
    rows, dm = x.shape
    tr = min(tr, rows)
    blk = pl.BlockSpec((tr, dm), lambda i: (i, 0))
    return pl.pallas_call(
        _final_kernel,
        grid=(rows // tr,),
        in_specs=[blk, pl.BlockSpec((tr * SUBLANES, LANES), lambda i: (i, 0)),
                  pl.BlockSpec((1, dm), lambda i: (0, 0))],
        out_specs=blk,
        out_shape=jax.ShapeDtypeStruct((rows, dm), F32),
        compiler_params=_cparams(("parallel",), VMEM_LIMIT),
        name="final_norm",
    )(x, p, g.reshape(1, dm))


def kernel(x, w_in, w_out, rel_bias, g_attn, g_ssm, norm_mix, norm_ffn, lam_re, lam_im, log_step, b_re, b_im, c_re, c_im, d_skip, w_glu, w_query, sub_keys, expert_u, expert_v, norm_final):
    bsz, seq, dm = x.shape
    depth = w_in.shape[0]
    prev = None
    biases = [_attn_bias_tables(rel_bias, d) for _, d in DILATED_PATTERNS]
    for l in range(depth):
        x, q, k, v, u_tm = _in_proj(x, prev, norm_mix[l], w_in[l].astype(BF16))
        attn = _attention(q, k, v, biases)
        a, bw, cw = _ssm_params(lam_re[l], lam_im[l], log_step[l], b_re[l], b_im[l], c_re[l], c_im[l])
        y = _ssm_scan(u_tm, a, bw, cw, bsz=bsz)
        wo = w_out[l].astype(BF16)
        z = _ssm_post(y, u_tm, d_skip[l], w_glu[l].astype(BF16), g_ssm[l], wo[ATTN_WIDTH:], bsz=bsz)
        x, h, qp = _mix_out(x, z, attn,
                            g_attn[l], wo[:ATTN_WIDTH], norm_ffn[l], w_query[l].astype(BF16))
        prev = _peer(h, qp, sub_keys[l], _pack_table(expert_u, l), _pack_table(expert_v, l))
    out = _final_norm(x.reshape(bsz * seq, dm), prev, norm_final)
    return out.reshape(bsz, seq, dm)
```

```python
import functools
import math

import numpy as np
import jax
import jax.numpy as jnp
from jax import lax
from jax.experimental import pallas as pl
from jax.experimental.pallas import tpu as pltpu

F32 = jnp.float32
BF16 = jnp.bfloat16

EPS = 1e-6
NEG_INF = -1e30
HEAD_DIM = 64
ATTN_WIDTH = 512
SSM_WIDTH = 512
SSM_GROUP = 16
SSM_STATE = 64
DILATED_PATTERNS = ((128, 1), (512, 4), (2048, 16))
REL_BUCKETS = 32
REL_MAX_DISTANCE = 1024
PEER_HEADS = 8
PEER_KEYS = 128
PEER_TOPK = 16
PEER_SEL = PEER_HEADS * PEER_TOPK

LANES = 128
SUBLANES = 8
QBLK = 128
KWIN = 256
BAND = 64
VMEM_LIMIT = 52 * 1024 * 1024


def _cparams(sem, vmem=None):
    return pltpu.CompilerParams(dimension_semantics=sem, vmem_limit_bytes=vmem)


def _rms(x, g):
    return x * lax.rsqrt(jnp.mean(x * x, axis=-1, keepdims=True) + EPS) * g


def _gelu(x):
    return 0.5 * x * (1.0 + lax.erf(x * (1.0 / math.sqrt(2.0))))


def _rows_from_chunks(p_ref, n_rows):
    return jnp.concatenate([p_ref[pl.ds(c, n_rows, stride=SUBLANES), :] for c in range(SUBLANES)], axis=-1)


def _in_proj_kernel(*refs, has_prev):
    if has_prev:
        x_ref, p_ref, g_ref, w_ref, xo_ref, q_ref, k_ref, v_ref, u_ref = refs
        x = x_ref[0] + _rows_from_chunks(p_ref, x_ref.shape[1])
    else:
        x_ref, g_ref, w_ref, xo_ref, q_ref, k_ref, v_ref, u_ref = refs
        x = x_ref[0]
    xo_ref[0] = x
    h = _rms(x, g_ref[...]).astype(BF16)
    proj = jnp.dot(h, w_ref[...], preferred_element_type=F32)
    a = ATTN_WIDTH
    q_ref[0] = proj[:, :a] * (HEAD_DIM ** -0.5)
    k_ref[0] = proj[:, a:2 * a]
    v_ref[0] = proj[:, 2 * a:3 * a]
    b, bsz = pl.program_id(1), pl.num_programs(1)
    for c in range(SSM_WIDTH // LANES):
        u_ref[c, pl.ds(b, proj.shape[0], stride=bsz), :] = proj[:, 3 * a + c * LANES:3 * a + (c + 1) * LANES]


def _in_proj(x, prev, g, w_bf16, *, ts=256):
    bsz, seq, dm = x.shape
    ts = min(ts, seq)
    row = pl.BlockSpec((1, ts, dm), lambda i, b: (b, i, 0))
    qkv = pl.BlockSpec((1, ts, ATTN_WIDTH), lambda i, b: (b, i, 0))
    ns = seq // ts
    chunks = pl.BlockSpec((ts * SUBLANES, LANES), lambda i, b: (b * ns + i, 0))
    ins = [x] + ([prev] if prev is not None else []) + [g.reshape(1, dm), w_bf16]
    in_specs = [row] + ([chunks] if prev is not None else []) + [
        pl.BlockSpec((1, dm), lambda i, b: (0, 0)),
        pl.BlockSpec(w_bf16.shape, lambda i, b: (0, 0)),
    ]
    slabs = SSM_WIDTH // LANES
    return pl.pallas_call(
        functools.partial(_in_proj_kernel, has_prev=prev is not None),
        grid=(seq // ts, bsz),
        in_specs=in_specs,
        out_specs=[row, qkv, qkv, qkv, pl.BlockSpec((slabs, ts * bsz, LANES), lambda i, b: (0, i, 0))],
        out_shape=[
            jax.ShapeDtypeStruct((bsz, seq, dm), F32),
            jax.ShapeDtypeStruct((bsz, seq, ATTN_WIDTH), F32),
            jax.ShapeDtypeStruct((bsz, seq, ATTN_WIDTH), F32),
            jax.ShapeDtypeStruct((bsz, seq, ATTN_WIDTH), F32),
            jax.ShapeDtypeStruct((slabs, seq * bsz, LANES), F32),
        ],
        compiler_params=_cparams(("parallel", "arbitrary"), VMEM_LIMIT),
        name="in_proj",
    )(*ins)


def _t5_buckets(rel):
    half = REL_BUCKETS // 2
    max_exact = half // 2
    n = np.abs(rel)
    large = max_exact + (np.log(np.maximum(n, 1) / max_exact)
                         / np.log(REL_MAX_DISTANCE / max_exact) * (half - max_exact)).astype(np.int32)
    large = np.minimum(large, half - 1)
    return (np.where(rel > 0, half, 0) + np.where(n < max_exact, n, large)).astype(np.int32)


def _attn_bias_tables(rel_bias, dilation):
    ql = np.arange(QBLK)[:, None]
    kl = np.arange(KWIN)[None, :]
    delta = np.stack([kl + off - ql for off in (0, -BAND, -2 * BAND)])
    buckets = np.where(np.abs(delta) <= BAND, _t5_buckets(delta * dilation), -1)
    rb = rel_bias.astype(F32).T
    bk = jnp.asarray(buckets, jnp.int32)[None]
    tab = jnp.full((rb.shape[0],) + buckets.shape, NEG_INF, F32)
    for b in range(REL_BUCKETS):
        tab = jnp.where(bk == b, rb[:, b][:, None, None, None], tab)
    return tab


BLOCKS_PER_STEP = 16


def _attn_kernel(q_ref, k_ref, v_ref, *rest, seq):
    bias_refs, (o_ref, acc_ref, m_ref, z_ref) = rest[:len(DILATED_PATTERNS)], rest[len(DILATED_PATTERNS):]
    lane = lax.broadcasted_iota(jnp.int32, (QBLK, LANES), 1)
    is_h0 = lane < HEAD_DIM
    dn = (((1,), (1,)), ((), ()))
    nsteps = seq // QBLK

    def rows(start, size, d):
        return pl.ds(start, size) if d == 1 else pl.ds(start, size, stride=d)

    def block(n, d, bias_ref, first, last):
        length = seq // d
        nblk = length // QBLK
        r, i = n >> (nblk.bit_length() - 1), n & (nblk - 1)
        s = i * QBLK
        ks = jnp.clip(s - BAND, 0, length - KWIN)
        var = jnp.where(i == 0, 0, jnp.where(i == nblk - 1, 2, 1))
        q_rows = rows(r + d * s, QBLK, d)
        k_rows = rows(r + d * ks, KWIN, d)
        qb = q_ref[q_rows, :].astype(BF16)
        kb = k_ref[k_rows, :].astype(BF16)
        vb = v_ref[k_rows, :].astype(BF16)
        outs, ms, zs = [], [], []
        for h in range(2):
            keep = is_h0 if h == 0 else jnp.logical_not(is_h0)
            qh = jnp.where(keep, qb, jnp.zeros_like(qb))
            logits = lax.dot_general(qh, kb, dn, preferred_element_type=F32) + bias_ref[h, var]
            m = jnp.max(logits, axis=-1, keepdims=True)
            p = jnp.exp(logits - m)
            outs.append(jnp.dot(p.astype(BF16), vb, preferred_element_type=F32))
            ms.append(jnp.broadcast_to(m, (QBLK, LANES)))
            zs.append(jnp.broadcast_to(jnp.sum(p, axis=-1, keepdims=True), (QBLK, LANES)))
        o = jnp.where(is_h0, outs[0], outs[1])
        m = jnp.where(is_h0, ms[0], ms[1])
        z = jnp.where(is_h0, zs[0], zs[1])
        if not first:
            m_old = m_ref[q_rows, :]
            m_new = jnp.maximum(m_old, m)
            a, b = jnp.exp(m_old - m_new), jnp.exp(m - m_new)
            o = acc_ref[q_rows, :] * a + o * b
            z = z_ref[q_rows, :] * a + z * b
            m = m_new
        if last:
            o_ref[q_rows, :] = o / z
        else:
            acc_ref[q_rows, :] = o
            m_ref[q_rows, :] = m
            z_ref[q_rows, :] = z

    for p, ((_, d), bias_ref) in enumerate(zip(DILATED_PATTERNS, bias_refs)):
        def step(g, carry, d=d, bias_ref=bias_ref, p=p):
            for j in range(BLOCKS_PER_STEP):
                block(g * BLOCKS_PER_STEP + j, d, bias_ref, p == 0, p == len(DILATED_PATTERNS) - 1)
            return carry
        lax.fori_loop(0, nsteps // BLOCKS_PER_STEP, step, 0)


def _attention(q, k, v, biases):
    bsz, seq, width = q.shape
    for _, d in DILATED_PATTERNS:
        length = seq // d
        assert length >= KWIN and length % QBLK == 0 and (length // QBLK) & (length // QBLK - 1) == 0
    assert (seq // QBLK) % BLOCKS_PER_STEP == 0
    blk = pl.BlockSpec((None, seq, LANES), lambda b, c: (b, 0, c))
    bias_spec = pl.BlockSpec((2, 3, QBLK, KWIN), lambda b, c: (c, 0, 0, 0))
    return pl.pallas_call(
        functools.partial(_attn_kernel, seq=seq),
        grid=(bsz, width // LANES),
        in_specs=[blk, blk, blk] + [bias_spec] * len(biases),
        out_specs=blk,
        out_shape=jax.ShapeDtypeStruct((bsz, seq, width), F32),
        scratch_shapes=[pltpu.VMEM((seq, LANES), F32)] * 3,
        compiler_params=_cparams(("parallel", "arbitrary"), VMEM_LIMIT),
        name="attention",
    )(q, k, v, *biases)


SSM_LANE_GROUPS = SSM_WIDTH // LANES
SSM_GB_STATES = (LANES // SSM_GROUP) * SSM_STATE


def _ssm_params(lam_re, lam_im, log_step, b_re, b_im, c_re, c_im):
    f = lambda t: t.astype(F32)
    lr, li = f(lam_re), f(lam_im)
    step = jnp.exp(f(log_step))[..., None]
    mag = jnp.exp(lr * step)
    ar, ai = mag * jnp.cos(li * step), mag * jnp.sin(li * step)
    nr, ni = ar - 1.0, ai
    den = lr * lr + li * li
    cr, ci = (nr * lr + ni * li) / den, (ni * lr - nr * li) / den
    br, bi = f(b_re), f(b_im)
    bbr = cr[..., None] * br - ci[..., None] * bi
    bbi = cr[..., None] * bi + ci[..., None] * br
    gpb = LANES // SSM_GROUP
    eye = jnp.eye(gpb, dtype=F32)

    def in_map(t):
        t = t.reshape(2, SSM_LANE_GROUPS, gpb, SSM_STATE, SSM_GROUP)
        return jnp.einsum('dbgpc,gh->dbgchp', t, eye).reshape(2, SSM_LANE_GROUPS, LANES, SSM_GB_STATES)

    def out_map(t):
        t = t.reshape(2, SSM_LANE_GROUPS, gpb, SSM_GROUP, SSM_STATE)
        return jnp.einsum('dbgcp,gh->dbgphc', t, eye).reshape(2, SSM_LANE_GROUPS, SSM_GB_STATES, LANES)

    bw = jnp.concatenate([in_map(bbr), in_map(bbi)], axis=-1).astype(BF16)
    cw = jnp.concatenate([out_map(f(c_re)), -out_map(f(c_im))], axis=-2).astype(BF16)
    a = jnp.stack([ar.reshape(2, SSM_LANE_GROUPS, SSM_GB_STATES),
                   ai.reshape(2, SSM_LANE_GROUPS, SSM_GB_STATES)], axis=2)
    return a, bw, cw


def _ssm_kernel(u_ref, a_ref, bw_ref, cw_ref, y_ref, st_ref, bu0_ref, bu1_ref, *, ts, bsz):
    d = pl.program_id(0)
    ns = SSM_GB_STATES
    bu_refs = (bu0_ref, bu1_ref)

    @pl.when(pl.program_id(1) == 0)
    def _():
        st_ref[...] = jnp.zeros_like(st_ref)

    for g0 in range(0, SSM_LANE_GROUPS, 2):
        gbs = (g0, g0 + 1)
        for gb, bu_ref in zip(gbs, bu_refs):
            bu_ref[...] = jnp.dot(u_ref[gb].astype(BF16), bw_ref[0, gb], preferred_element_type=F32)
        coefs = [(jnp.broadcast_to(a_ref[0, gb, 0:1, :], (bsz, ns)), jnp.broadcast_to(a_ref[0, gb, 1:2, :], (bsz, ns)))
                 for gb in gbs]

        def step(j, carry, coefs=coefs):
            tl = jnp.where(d == 0, j, ts - 1 - j)
            r = pl.multiple_of(tl * bsz, bsz)
            out = []
            for (ar, ai), bu_ref, (xr, xi) in zip(coefs, bu_refs, carry):
                nr = ar * xr - ai * xi + bu_ref[pl.ds(r, bsz), :ns]
                ni = ar * xi + ai * xr + bu_ref[pl.ds(r, bsz), ns:]
                bu_ref[pl.ds(r, bsz), :ns] = nr
                bu_ref[pl.ds(r, bsz), ns:] = ni
                out.append((nr, ni))
            return tuple(out)

        final = lax.fori_loop(0, ts, step, tuple((st_ref[gb, :, :ns], st_ref[gb, :, ns:]) for gb in gbs))
        for gb, bu_ref, (xr, xi) in zip(gbs, bu_refs, final):
            st_ref[gb, :, :ns] = xr
            st_ref[gb, :, ns:] = xi
            y_ref[0, :, gb * LANES:(gb + 1) * LANES] = jnp.dot(
                bu_ref[...].astype(BF16), cw_ref[0, gb], preferred_element_type=F32)


def _ssm_scan(u_tm, a, bw, cw, *, bsz, ts=64):
    slabs, rows, _ = u_tm.shape
    width = slabs * LANES
    seq = rows // bsz
    ts = min(ts, seq)
    nt = seq // ts
    tblk = lambda d, i: jnp.where(d == 0, i, nt - 1 - i)
    return pl.pallas_call(
        functools.partial(_ssm_kernel, ts=ts, bsz=bsz),
        grid=(2, nt),
        in_specs=[
            pl.BlockSpec((slabs, ts * bsz, LANES), lambda d, i: (0, tblk(d, i), 0)),
            pl.BlockSpec((1,) + a.shape[1:], lambda d, i: (d, 0, 0, 0)),
            pl.BlockSpec((1,) + bw.shape[1:], lambda d, i: (d, 0, 0, 0)),
            pl.BlockSpec((1,) + cw.shape[1:], lambda d, i: (d, 0, 0, 0)),
        ],
        out_specs=pl.BlockSpec((1, ts * bsz, width), lambda d, i: (d, tblk(d, i), 0)),
        out_shape=jax.ShapeDtypeStruct((2, rows, width), F32),
        scratch_shapes=[pltpu.VMEM((SSM_LANE_GROUPS, bsz, 2 * SSM_GB_STATES), F32),
                        pltpu.VMEM((ts * bsz, 2 * SSM_GB_STATES), F32),
                        pltpu.VMEM((ts * bsz, 2 * SSM_GB_STATES), F32)],
        compiler_params=_cparams(("arbitrary", "arbitrary"), VMEM_LIMIT),
        name="ssm_scan",
    )(u_tm, a, bw, cw)


def _ssm_post_kernel(y_ref, u_ref, d_ref, wg_ref, g_ref, wo_ref, z_ref, slab_ref, *, bsz):
    u = jnp.concatenate([u_ref[c] for c in range(u_ref.shape[0])], axis=-1)
    y = _gelu(y_ref[0] + y_ref[1] + d_ref[...] * u).astype(BF16)
    ab = jnp.dot(y, wg_ref[...], preferred_element_type=F32)
    ssm = ab[:, :SSM_WIDTH] * jax.nn.sigmoid(ab[:, SSM_WIDTH:])
    n = _rms(ssm, g_ref[...]).astype(BF16)
    z = jnp.dot(n, wo_ref[...], preferred_element_type=F32)
    nt, dm = z.shape[0] // bsz, z.shape[1]
    for c in range(dm // LANES):
        slab_ref[c] = z[:, c * LANES:(c + 1) * LANES]
    for b in range(bsz):
        for c in range(dm // LANES):
            z_ref[:, b * dm + c * LANES:b * dm + (c + 1) * LANES] = slab_ref[c, pl.ds(b, nt, stride=bsz), :]


def _ssm_post(y, u_tm, d_skip, w_glu_bf16, g_ssm, w_out_ssm_bf16, *, bsz, tr=512):
    slabs, rows, _ = u_tm.shape
    width = slabs * LANES
    tr = min(tr, rows)
    assert tr % bsz == 0
    dm = w_out_ssm_bf16.shape[1]
    full = lambda a: pl.BlockSpec(a.shape, lambda i: (0,) * a.ndim)
    d2, g2 = d_skip.reshape(1, width), g_ssm.reshape(1, width)
    return pl.pallas_call(
        functools.partial(_ssm_post_kernel, bsz=bsz),
        grid=(rows // tr,),
        in_specs=[pl.BlockSpec((2, tr, width), lambda i: (0, i, 0)),
                  pl.BlockSpec((slabs, tr, LANES), lambda i: (0, i, 0)),
                  full(d2), full(w_glu_bf16), full(g2), full(w_out_ssm_bf16)],
        out_specs=pl.BlockSpec((tr // bsz, bsz * dm), lambda i: (i, 0)),
        out_shape=jax.ShapeDtypeStruct((rows // bsz, bsz * dm), F32),
        scratch_shapes=[pltpu.VMEM((dm // LANES, tr, LANES), F32)],
        compiler_params=_cparams(("parallel",), VMEM_LIMIT),
        name="ssm_post",
    )(y, u_tm, d2, w_glu_bf16, g2, w_out_ssm_bf16)


def _mix_out_kernel(x_ref, z_ref, a_ref, ga_ref, wo_ref, gf_ref, wq_ref, xn_ref, h_ref, q_ref):
    n = _rms(a_ref[0], ga_ref[...]).astype(BF16)
    xn = x_ref[0] + z_ref[...] + jnp.dot(n, wo_ref[...], preferred_element_type=F32)
    xn_ref[0] = xn
    h = _rms(xn, gf_ref[...])
    for c in range(SUBLANES):
        h_ref[pl.ds(c, h.shape[0], stride=SUBLANES), :] = h[:, c * LANES:(c + 1) * LANES]
    qp = jnp.dot(h.astype(BF16), wq_ref[...], preferred_element_type=F32)
    for hd in range(PEER_HEADS):
        q_ref[hd] = qp[:, hd * LANES:(hd + 1) * LANES]


def _mix_out(x, z_tm, attn, g_attn, w_out_attn_bf16, norm_ffn, w_query, *, ts=256):
    bsz, seq, dm = x.shape
    ts = min(ts, seq)
    ns = seq // ts
    row = pl.BlockSpec((1, ts, dm), lambda b, i: (b, i, 0))
    half = pl.BlockSpec((1, ts, ATTN_WIDTH), lambda b, i: (b, i, 0))
    full = lambda a: pl.BlockSpec(a.shape, lambda b, i: (0,) * a.ndim)
    ga, gf = g_attn.reshape(1, ATTN_WIDTH), norm_ffn.reshape(1, dm)
    qdim = w_query.shape[1] // PEER_HEADS
    return pl.pallas_call(
        _mix_out_kernel,
        grid=(bsz, ns),
        in_specs=[row, pl.BlockSpec((ts, dm), lambda b, i: (i, b)), half,
                  full(ga), full(w_out_attn_bf16), full(gf), full(w_query)],
        out_specs=[row, pl.BlockSpec((ts * SUBLANES, LANES), lambda b, i: (b * ns + i, 0)),
                   pl.BlockSpec((PEER_HEADS, ts, qdim), lambda b, i: (0, b * ns + i, 0))],
        out_shape=[jax.ShapeDtypeStruct((bsz, seq, dm), F32),
                   jax.ShapeDtypeStruct((bsz * seq * SUBLANES, LANES), F32),

                   jax.ShapeDtypeStruct((PEER_HEADS, bsz * seq, qdim), F32)],
        compiler_params=_cparams(("parallel", "arbitrary"), VMEM_LIMIT),
        name="mix_out",
    )(x, z_tm, attn, ga, w_out_attn_bf16, gf, w_query)


TOPK_TOKENS = SUBLANES * LANES
KEY_PITCH = PEER_KEYS + 4
_CANDIDATES = tuple((a, b) for a in range(PEER_TOPK) for b in range(PEER_TOPK) if (a + 1) * (b + 1) <= PEER_TOPK)


def _tree(op, xs):
    xs = list(xs)
    while len(xs) > 1:
        xs = [op(xs[i], xs[i + 1]) if i + 1 < len(xs) else xs[i] for i in range(0, len(xs), 2)]
    return xs[0]


def _extract16(problems):
    ninf = jnp.float32(-jnp.inf)

    def better(a, b):
        gt = b[0] > a[0]
        return tuple(jnp.where(gt, y, x) for x, y in zip(a, b))

    def step(r, carry):
        for p in problems:
            s_ref, order = p["s"], p["order"]
            assert list(order) == sorted(order)
            n = len(order)
            rows = [(s_ref[k], order[k]) + ((p["pay"][k],) if p.get("pay") is not None else ())
                    for k in range(n)]
            win = _tree(better, rows)
            m, am = win[0], win[1]
            for k in range(n):
                s_ref[k] = jnp.where(am == order[k], ninf, s_ref[k])
            p["vals"][r] = m
            p["picks"][r] = win[2] if len(win) > 2 else am.astype(jnp.int32)
        return carry

    lax.fori_loop(0, PEER_TOPK, step, 0)


def _peer_topk_kernel(q_ref, k_ref, idx_ref, gate_ref,
                      slab_ref, s1_ref, s2_ref, t1_ref, i1_ref, t2_ref, i2_ref, cand_ref, pay_ref, ts_ref, ex_ref):
    dn = (((1,), (1,)), ((), ()))
    keys = tuple(range(PEER_KEYS))

    def head(h, carry):
        for w, s_ref in ((0, s1_ref), (1, s2_ref)):
            for j in range(SUBLANES):
                slab_ref[j * KEY_PITCH:j * KEY_PITCH + PEER_KEYS, :] = lax.dot_general(
                    k_ref[w, h], q_ref[h, j * LANES:(j + 1) * LANES, :].astype(BF16), dn,
                    preferred_element_type=F32)
            for k in range(PEER_KEYS):
                s_ref[k] = slab_ref[pl.ds(k, SUBLANES, stride=KEY_PITCH), :]
        _extract16([dict(s=s1_ref, order=keys, vals=t1_ref, picks=i1_ref),
                    dict(s=s2_ref, order=keys, vals=t2_ref, picks=i2_ref)])
        for c, (a, b) in enumerate(_CANDIDATES):
            cand_ref[c] = t1_ref[a] + t2_ref[b]
            pay_ref[c] = i1_ref[a] * PEER_KEYS + i2_ref[b]
        _extract16([dict(s=cand_ref, order=tuple(a * PEER_TOPK + b for a, b in _CANDIDATES), pay=pay_ref,
                         vals=ts_ref, picks=ex_ref)])
        top_s = ts_ref[...]
        e = jnp.exp(top_s - jnp.max(top_s, axis=0, keepdims=True))
        gate_ref[0, h] = e / jnp.sum(e, axis=0, keepdims=True)
        idx_ref[0, h] = ex_ref[...] * ROWS_PER_EXPERT
        return carry

    lax.fori_loop(0, PEER_HEADS, head, 0)


def _peer_topk(qp, keys_pad):
    n_tok = qp.shape[1]
    tt = TOPK_TOKENS
    assert n_tok % tt == 0
    shp = (n_tok // tt, PEER_HEADS, PEER_TOPK, SUBLANES, LANES)
    out = pl.BlockSpec((1,) + shp[1:], lambda i: (i, 0, 0, 0, 0))
    vregs = lambda n, dt: pltpu.VMEM((n, SUBLANES, LANES), dt)
    idx, gate = pl.pallas_call(
        _peer_topk_kernel,
        grid=(n_tok // tt,),
        in_specs=[pl.BlockSpec((PEER_HEADS, tt, qp.shape[2]), lambda i: (0, i, 0)),
                  pl.BlockSpec(keys_pad.shape, lambda i: (0, 0, 0, 0))],
        out_specs=[out, out],
        out_shape=[jax.ShapeDtypeStruct(shp, jnp.int32), jax.ShapeDtypeStruct(shp, F32)],
        scratch_shapes=[pltpu.VMEM((SUBLANES * KEY_PITCH, LANES), F32),
                        vregs(PEER_KEYS, F32), vregs(PEER_KEYS, F32),
                        vregs(PEER_TOPK, F32), vregs(PEER_TOPK, jnp.int32),
                        vregs(PEER_TOPK, F32), vregs(PEER_TOPK, jnp.int32),
                        vregs(len(_CANDIDATES), F32), vregs(len(_CANDIDATES), jnp.int32),
                        vregs(PEER_TOPK, F32), vregs(PEER_TOPK, jnp.int32)],
        compiler_params=_cparams(("parallel",), VMEM_LIMIT),
        name="peer_topk",
    )(qp, keys_pad)
    to_tok = lambda a: jnp.transpose(a, (0, 3, 4, 1, 2)).reshape(n_tok, PEER_SEL)
    return to_tok(idx), to_tok(gate)


ROWS_PER_EXPERT = 4
PAIR_CHUNK = 32
CHUNK_ROWS = PAIR_CHUNK * ROWS_PER_EXPERT
SMEM_GROUP = 8
N_CHUNKS = PEER_SEL // PAIR_CHUNK
TOKENS_PER_STEP = 16


def _pack_table_kernel(t_ref, o_ref):
    rows, dm = t_ref.shape
    for c in range(ROWS_PER_EXPERT):
        lo = lax.bitcast_convert_type(t_ref[:, c * LANES:(c + 1) * LANES].astype(BF16).astype(F32), jnp.int32)
        hi = lax.bitcast_convert_type(
            t_ref[:, dm // 2 + c * LANES:dm // 2 + (c + 1) * LANES].astype(BF16).astype(F32), jnp.int32)
        o_ref[pl.ds(c, rows, stride=ROWS_PER_EXPERT), :] = lax.shift_right_logical(lo, 16) | hi


def _pack_table(tables, layer, *, rows=512):
    _, e, dm = tables.shape
    assert dm == 2 * ROWS_PER_EXPERT * LANES
    rows = min(rows, e)
    return pl.pallas_call(
        _pack_table_kernel,
        grid=(e // rows,),
        in_specs=[pl.BlockSpec((None, rows, dm), lambda i: (layer, i, 0))],
        out_specs=pl.BlockSpec((rows * ROWS_PER_EXPERT, LANES), lambda i: (i, 0)),
        out_shape=jax.ShapeDtypeStruct((e * ROWS_PER_EXPERT, LANES), jnp.int32),
        compiler_params=_cparams(("parallel",), VMEM_LIMIT),
        name="pack_table",
    )(tables)


def _unpack_words(w):
    lo = lax.bitcast_convert_type(w << 16, F32)
    hi = lax.bitcast_convert_type(w & jnp.int32(-65536), F32)
    return lo, hi


def _gather_chunk(idx_ref, tab_ref, buf_ref, c):
    for g in range(PAIR_CHUNK // SMEM_GROUP):
        ids = idx_ref.at[pl.ds(pl.multiple_of(c * PAIR_CHUNK + g * SMEM_GROUP, SMEM_GROUP), SMEM_GROUP)]
        for i in range(SMEM_GROUP):
            e4 = pl.multiple_of(ids[i], ROWS_PER_EXPERT)
            s = (g * SMEM_GROUP + i) * ROWS_PER_EXPERT
            buf_ref[s:s + ROWS_PER_EXPERT, :] = tab_ref[pl.ds(e4, ROWS_PER_EXPERT), :]


def _peer_u_kernel(idx_ref, h_ref, gate_ref, tab_ref, coef_ref, *scratch, tt):
    bufs, m_ref = (scratch[:N_CHUNKS], scratch[N_CHUNKS:2 * N_CHUNKS]), scratch[-1]
    n_chunks = tt * N_CHUNKS
    row = lax.broadcasted_iota(jnp.int32, (SUBLANES, CHUNK_ROWS), 0)
    col = lax.broadcasted_iota(jnp.int32, (SUBLANES, CHUNK_ROWS), 1)
    quarter = (col % ROWS_PER_EXPERT) == (row % ROWS_PER_EXPERT)
    m_lo = jnp.logical_and(row < ROWS_PER_EXPERT, quarter)
    m_hi = jnp.logical_and(row >= ROWS_PER_EXPERT, quarter)
    pr = lax.broadcasted_iota(jnp.int32, (CHUNK_ROWS, LANES), 0) // ROWS_PER_EXPERT
    pc = lax.broadcasted_iota(jnp.int32, (CHUNK_ROWS, LANES), 1)
    place = [(pr + part * PAIR_CHUNK == pc).astype(BF16) for part in range(N_CHUNKS)]
    dn = (((1,), (1,)), ((), ()))
    gather = functools.partial(_gather_chunk, idx_ref, tab_ref)

    def chunk_dots(t, part, buf_ref):
        rows = pl.ds(pl.multiple_of(t * SUBLANES, SUBLANES), SUBLANES)
        x = h_ref[rows, :].astype(BF16)
        lo, hi = _unpack_words(buf_ref[...])
        b = jnp.concatenate([lo.astype(BF16), hi.astype(BF16)], axis=0)
        out = lax.dot_general(x, b, dn, preferred_element_type=F32)
        m = jnp.where(m_lo, out[:, :CHUNK_ROWS], 0.0) + jnp.where(m_hi, out[:, CHUNK_ROWS:], 0.0)
        m_ref[part, pl.ds(t, 1), :] = jnp.sum(m, axis=0, keepdims=True)

    for j in range(N_CHUNKS):
        gather(bufs[0][j], j)

    def body(i, carry):
        for s in range(TOKENS_PER_STEP):
            half = s % 2
            t = TOKENS_PER_STEP * i + s
            for j in range(N_CHUNKS):
                chunk_dots(t, j, bufs[half][j])
            for j in range(N_CHUNKS):
                gather(bufs[1 - half][j], jnp.minimum((t + 1) * N_CHUNKS + j, n_chunks - 1))
        return carry

    lax.fori_loop(0, tt // TOKENS_PER_STEP, body, 0)

    dots = None
    for part in range(N_CHUNKS):
        m = m_ref[part]
        m_top = m.astype(BF16)
        m_rest = (m - m_top.astype(F32)).astype(BF16)
        s = (jnp.dot(m_top, place[part], preferred_element_type=F32)
             + jnp.dot(m_rest, place[part], preferred_element_type=F32))
        dots = s if dots is None else dots + s
    coef = (gate_ref[...] * _gelu(dots)).astype(BF16)
    for part in range(N_CHUNKS):
        coef_ref[part] = lax.dot_general(coef, place[part], dn, preferred_element_type=F32)


def _peer_u(idx4, h8, gates, tab, *, tt=128):
    n_tok = gates.shape[0]
    tt = min(tt, n_tok)
    assert TOKENS_PER_STEP % 2 == 0 and tt % TOKENS_PER_STEP == 0
    tok2 = pl.BlockSpec((tt, PEER_SEL), lambda i: (i, 0))
    return pl.pallas_call(
        functools.partial(_peer_u_kernel, tt=tt),
        grid=(n_tok // tt,),
        in_specs=[pl.BlockSpec((tt * PEER_SEL,), lambda i: (i,), memory_space=pltpu.SMEM),
                  pl.BlockSpec((tt * SUBLANES, LANES), lambda i: (i, 0)), tok2,
                  pl.BlockSpec(tab.shape, lambda i: (0, 0), pipeline_mode=pl.Buffered(1))],
        out_specs=pl.BlockSpec((N_CHUNKS, tt, CHUNK_ROWS), lambda i: (0, i, 0)),
        out_shape=jax.ShapeDtypeStruct((N_CHUNKS, n_tok, CHUNK_ROWS), F32),
        scratch_shapes=[pltpu.VMEM((CHUNK_ROWS, LANES), jnp.int32)] * (2 * N_CHUNKS)
                       + [pltpu.VMEM((N_CHUNKS, tt, LANES), F32)],
        compiler_params=_cparams(("arbitrary",), VMEM_LIMIT),
        name="peer_u",
    )(idx4, h8, gates, tab)


def _peer_v_kernel(idx_ref, cx_ref, tab_ref, o_ref, *scratch, tt):
    bufs, parts_ref = (scratch[:N_CHUNKS], scratch[N_CHUNKS:2 * N_CHUNKS]), scratch[2 * N_CHUNKS]
    n_chunks = tt * N_CHUNKS
    row = lax.broadcasted_iota(jnp.int32, (SUBLANES, CHUNK_ROWS), 0)
    col = lax.broadcasted_iota(jnp.int32, (SUBLANES, CHUNK_ROWS), 1)
    quarter = (col % ROWS_PER_EXPERT) == (row % ROWS_PER_EXPERT)
    m_lo = jnp.logical_and(row < ROWS_PER_EXPERT, quarter)
    m_hi = jnp.logical_and(row >= ROWS_PER_EXPERT, quarter)
    gather = functools.partial(_gather_chunk, idx_ref, tab_ref)

    def chunk_sum(t, parity, part, buf_ref):
        cx = jnp.broadcast_to(cx_ref[part, pl.ds(t, 1), :], (SUBLANES, CHUNK_ROWS))
        a = jnp.concatenate([jnp.where(m_lo, cx, 0.0), jnp.where(m_hi, cx, 0.0)], axis=1).astype(BF16)
        lo, hi = _unpack_words(buf_ref[...])
        b = jnp.concatenate([lo.astype(BF16), hi.astype(BF16)], axis=0)
        parts_ref[parity, part] = jnp.dot(a, b, preferred_element_type=F32)

    parts_ref[...] = jnp.zeros_like(parts_ref)
    for j in range(N_CHUNKS):
        gather(bufs[0][j], j)

    def body(i, carry):
        for s in range(TOKENS_PER_STEP):
            half = s % 2
            t = TOKENS_PER_STEP * i + s
            prev = jnp.maximum(t - 1, 0)
            o_ref[prev] = _tree(jnp.add, [parts_ref[1 - half, j] for j in range(N_CHUNKS)])
            for j in range(N_CHUNKS):
                chunk_sum(t, half, j, bufs[half][j])
            for j in range(N_CHUNKS):
                gather(bufs[1 - half][j], jnp.minimum((t + 1) * N_CHUNKS + j, n_chunks - 1))
        return carry

    lax.fori_loop(0, tt // TOKENS_PER_STEP, body, 0)
    o_ref[tt - 1] = _tree(jnp.add, [parts_ref[1, j] for j in range(N_CHUNKS)])


def _peer_v(idx4, cx, tab, *, tt=128):
    n_tok = idx4.shape[0] // PEER_SEL
    tt = min(tt, n_tok)
    assert TOKENS_PER_STEP % 2 == 0 and tt % TOKENS_PER_STEP == 0
    smem = pl.BlockSpec((tt * PEER_SEL,), lambda i: (i,), memory_space=pltpu.SMEM)
    return pl.pallas_call(
        functools.partial(_peer_v_kernel, tt=tt),
        grid=(n_tok // tt,),
        in_specs=[smem, pl.BlockSpec((N_CHUNKS, tt, CHUNK_ROWS), lambda i: (0, i, 0)),
                  pl.BlockSpec(tab.shape, lambda i: (0, 0), pipeline_mode=pl.Buffered(1))],
        out_specs=pl.BlockSpec((tt, SUBLANES, LANES), lambda i: (i, 0, 0)),
        out_shape=jax.ShapeDtypeStruct((n_tok, SUBLANES, LANES), F32),
        scratch_shapes=[pltpu.VMEM((CHUNK_ROWS, LANES), jnp.int32)] * (2 * N_CHUNKS)
                       + [pltpu.VMEM((2, N_CHUNKS, SUBLANES, LANES), F32)],
        compiler_params=_cparams(("arbitrary",), VMEM_LIMIT),
        name="peer_v",
    )(idx4, cx, tab)


def _peer(h8, qp, sub_keys, tab_u, tab_v):
    n_tok = qp.shape[1]
    half = sub_keys.shape[-1]
    z = jnp.zeros_like(sub_keys[0])
    keys_pad = jnp.stack([jnp.concatenate([sub_keys[0], z], axis=-1),
                          jnp.concatenate([z, sub_keys[1]], axis=-1)]).astype(BF16)
    assert keys_pad.shape[-1] == 2 * half == qp.shape[-1]
    idx4, gates = _peer_topk(qp, keys_pad)
    idx4 = idx4.reshape(n_tok * PEER_SEL)
    coef = _peer_u(idx4, h8, gates, tab_u)
    out = _peer_v(idx4, coef, tab_v)
    return out.reshape(n_tok * SUBLANES, LANES)


def _final_kernel(x_ref, p_ref, g_ref, o_ref):
    o_ref[...] = _rms(x_ref[...] + _rows_from_chunks(p_ref, x_ref.shape[0]), g_ref[...])


def _final_norm(x, p, g, *, tr=1024):
    rows, dm = x.shape
    tr = min(tr, rows)
    blk = pl.BlockSpec((tr, dm), lambda i: (i, 0))
    return pl.pallas_call(
        _final_kernel,
        grid=(rows // tr,),
        in_specs=[blk, pl.BlockSpec((tr * SUBLANES, LANES), lambda i: (i, 0)),
                  pl.BlockSpec((1, dm), lambda i: (0, 0))],
        out_specs=blk,
        out_shape=jax.ShapeDtypeStruct((rows, dm), F32),
        compiler_params=_cparams(("parallel",), VMEM_LIMIT),
        name="final_norm",
    )(x, p, g.reshape(1, dm))


def kernel(x, w_in, w_out, rel_bias, g_attn, g_ssm, norm_mix, norm_ffn, lam_re, lam_im, log_step, b_re, b_im, c_re, c_im, d_skip, w_glu, w_query, sub_keys, expert_u, expert_v, norm_final):
    bsz, seq, dm = x.shape
    depth = w_in.shape[0]
    prev = None
    biases = [_attn_bias_tables(rel_bias, d) for _, d in DILATED_PATTERNS]
    for l in range(depth):
        x, q, k, v, u_tm = _in_proj(x, prev, norm_mix[l], w_in[l].astype(BF16))
        attn = _attention(q, k, v, biases)
        a, bw, cw = _ssm_params(lam_re[l], lam_im[l], log_step[l], b_re[l], b_im[l], c_re[l], c_im[l])
        y = _ssm_scan(u_tm, a, bw, cw, bsz=bsz)
        wo = w_out[l].astype(BF16)
        z = _ssm_post(y, u_tm, d_skip[l], w_glu[l].astype(BF16), g_ssm[l], wo[ATTN_WIDTH:], bsz=bsz)
        x, h, qp = _mix_out(x, z, attn,
                            g_attn[l], wo[:ATTN_WIDTH], norm_ffn[l], w_query[l].astype(BF16))
        prev = _peer(h, qp, sub_keys[l], _pack_table(expert_u, l), _pack_table(expert_v, l))
    out = _final_norm(x.reshape(bsz * seq, dm), prev, norm_final)
    return out.reshape(bsz, seq, dm)
```

```python
import functools
import math

import numpy as np
import jax
import jax.numpy as jnp
from jax import lax
from jax.experimental import pallas as pl
from jax.experimental.pallas import tpu as pltpu

F32 = jnp.float32
BF16 = jnp.bfloat16

EPS = 1e-6
NEG_INF = -1e30
HEAD_DIM = 64
ATTN_WIDTH = 512
SSM_WIDTH = 512
SSM_GROUP = 16
SSM_STATE = 64
DILATED_PATTERNS = ((128, 1), (512, 4), (2048, 16))
REL_BUCKETS = 32
REL_MAX_DISTANCE = 1024
PEER_HEADS = 8
PEER_KEYS = 128
PEER_TOPK = 16
PEER_SEL = PEER_HEADS * PEER_TOPK

LANES = 128
SUBLANES = 8
QBLK = 128
KWIN = 256
BAND = 64
VMEM_LIMIT = 52 * 1024 * 1024


def _cparams(sem, vmem=None):
    return pltpu.CompilerParams(dimension_semantics=sem, vmem_limit_bytes=vmem)


def _rms(x, g):
    return x * lax.rsqrt(jnp.mean(x * x, axis=-1, keepdims=True) + EPS) * g


def _gelu(x):
    return 0.5 * x * (1.0 + lax.erf(x * (1.0 / math.sqrt(2.0))))


def _rows_from_chunks(p_ref, n_rows):
    return jnp.concatenate([p_ref[pl.ds(c, n_rows, stride=SUBLANES), :] for c in range(SUBLANES)], axis=-1)


def _in_proj_kernel(*refs, has_prev):
    if has_prev:
        x_ref, p_ref, g_ref, w_ref, xo_ref, q_ref, k_ref, v_ref, u_ref = refs
        x = x_ref[0] + _rows_from_chunks(p_ref, x_ref.shape[1])
    else:
        x_ref, g_ref, w_ref, xo_ref, q_ref, k_ref, v_ref, u_ref = refs
        x = x_ref[0]
    xo_ref[0] = x
    h = _rms(x, g_ref[...]).astype(BF16)
    proj = jnp.dot(h, w_ref[...], preferred_element_type=F32)
    a = ATTN_WIDTH
    q_ref[0] = proj[:, :a] * (HEAD_DIM ** -0.5)
    k_ref[0] = proj[:, a:2 * a]
    v_ref[0] = proj[:, 2 * a:3 * a]
    b, bsz = pl.program_id(1), pl.num_programs(1)
    for c in range(SSM_WIDTH // LANES):
        u_ref[c, pl.ds(b, proj.shape[0], stride=bsz), :] = proj[:, 3 * a + c * LANES:3 * a + (c + 1) * LANES]


def _in_proj(x, prev, g, w_bf16, *, ts=256):
    bsz, seq, dm = x.shape
    ts = min(ts, seq)
    row = pl.BlockSpec((1, ts, dm), lambda i, b: (b, i, 0))
    qkv = pl.BlockSpec((1, ts, ATTN_WIDTH), lambda i, b: (b, i, 0))
    ns = seq // ts
    chunks = pl.BlockSpec((ts * SUBLANES, LANES), lambda i, b: (b * ns + i, 0))
    ins = [x] + ([prev] if prev is not None else []) + [g.reshape(1, dm), w_bf16]
    in_specs = [row] + ([chunks] if prev is not None else []) + [
        pl.BlockSpec((1, dm), lambda i, b: (0, 0)),
        pl.BlockSpec(w_bf16.shape, lambda i, b: (0, 0)),
    ]
    slabs = SSM_WIDTH // LANES
    return pl.pallas_call(
        functools.partial(_in_proj_kernel, has_prev=prev is not None),
        grid=(seq // ts, bsz),
        in_specs=in_specs,
        out_specs=[row, qkv, qkv, qkv, pl.BlockSpec((slabs, ts * bsz, LANES), lambda i, b: (0, i, 0))],
        out_shape=[
            jax.ShapeDtypeStruct((bsz, seq, dm), F32),
            jax.ShapeDtypeStruct((bsz, seq, ATTN_WIDTH), F32),
            jax.ShapeDtypeStruct((bsz, seq, ATTN_WIDTH), F32),
            jax.ShapeDtypeStruct((bsz, seq, ATTN_WIDTH), F32),
            jax.ShapeDtypeStruct((slabs, seq * bsz, LANES), F32),
        ],
        compiler_params=_cparams(("parallel", "arbitrary"), VMEM_LIMIT),
        name="in_proj",
    )(*ins)


def _t5_buckets(rel):
    half = REL_BUCKETS // 2
    max_exact = half // 2
    n = np.abs(rel)
    large = max_exact + (np.log(np.maximum(n, 1) / max_exact)
                         / np.log(REL_MAX_DISTANCE / max_exact) * (half - max_exact)).astype(np.int32)
    large = np.minimum(large, half - 1)
    return (np.where(rel > 0, half, 0) + np.where(n < max_exact, n, large)).astype(np.int32)


def _attn_bias_tables(rel_bias, dilation):
    ql = np.arange(QBLK)[:, None]
    kl = np.arange(KWIN)[None, :]
    delta = np.stack([kl + off - ql for off in (0, -BAND, -2 * BAND)])
    buckets = np.where(np.abs(delta) <= BAND, _t5_buckets(delta * dilation), -1)
    rb = rel_bias.astype(F32).T
    bk = jnp.asarray(buckets, jnp.int32)[None]
    tab = jnp.full((rb.shape[0],) + buckets.shape, NEG_INF, F32)
    for b in range(REL_BUCKETS):
        tab = jnp.where(bk == b, rb[:, b][:, None, None, None], tab)
    return tab


BLOCKS_PER_STEP = 32


def _attn_kernel(q_ref, k_ref, v_ref, *rest, seq):
    bias_refs, (o_ref, acc_ref, m_ref, z_ref) = rest[:len(DILATED_PATTERNS)], rest[len(DILATED_PATTERNS):]
    lane = lax.broadcasted_iota(jnp.int32, (QBLK, LANES), 1)
    is_h0 = lane < HEAD_DIM
    dn = (((1,), (1,)), ((), ()))
    nsteps = seq // QBLK

    def rows(start, size, d):
        return pl.ds(start, size) if d == 1 else pl.ds(start, size, stride=d)

    def block(n, d, bias_ref, first, last):
        length = seq // d
        nblk = length // QBLK
        r, i = n >> (nblk.bit_length() - 1), n & (nblk - 1)
        s = i * QBLK
        ks = jnp.clip(s - BAND, 0, length - KWIN)
        var = jnp.where(i == 0, 0, jnp.where(i == nblk - 1, 2, 1))
        q_rows = rows(r + d * s, QBLK, d)
        k_rows = rows(r + d * ks, KWIN, d)
        qb = q_ref[q_rows, :].astype(BF16)
        kb = k_ref[k_rows, :].astype(BF16)
        vb = v_ref[k_rows, :].astype(BF16)
        outs, ms, zs = [], [], []
        for h in range(2):
            keep = is_h0 if h == 0 else jnp.logical_not(is_h0)
            qh = jnp.where(keep, qb, jnp.zeros_like(qb))
            logits = lax.dot_general(qh, kb, dn, preferred_element_type=F32) + bias_ref[h, var]
            m = jnp.max(logits, axis=-1, keepdims=True)
            p = jnp.exp(logits - m)
            outs.append(jnp.dot(p.astype(BF16), vb, preferred_element_type=F32))
            ms.append(jnp.broadcast_to(m, (QBLK, LANES)))
            zs.append(jnp.broadcast_to(jnp.sum(p, axis=-1, keepdims=True), (QBLK, LANES)))
        o = jnp.where(is_h0, outs[0], outs[1])
        m = jnp.where(is_h0, ms[0], ms[1])
        z = jnp.where(is_h0, zs[0], zs[1])
        if not first:
            m_old = m_ref[q_rows, :]
            m_new = jnp.maximum(m_old, m)
            a, b = jnp.exp(m_old - m_new), jnp.exp(m - m_new)
            o = acc_ref[q_rows, :] * a + o * b
            z = z_ref[q_rows, :] * a + z * b
            m = m_new
        if last:
            o_ref[q_rows, :] = o / z
        else:
            acc_ref[q_rows, :] = o
            m_ref[q_rows, :] = m
            z_ref[q_rows, :] = z

    for p, ((_, d), bias_ref) in enumerate(zip(DILATED_PATTERNS, bias_refs)):
        def step(g, carry, d=d, bias_ref=bias_ref, p=p):
            for j in range(BLOCKS_PER_STEP):
                block(g * BLOCKS_PER_STEP + j, d, bias_ref, p == 0, p == len(DILATED_PATTERNS) - 1)
            return carry
        lax.fori_loop(0, nsteps // BLOCKS_PER_STEP, step, 0)


def _attention(q, k, v, biases):
    bsz, seq, width = q.shape
    for _, d in DILATED_PATTERNS:
        length = seq // d
        assert length >= KWIN and length % QBLK == 0 and (length // QBLK) & (length // QBLK - 1) == 0
    assert (seq // QBLK) % BLOCKS_PER_STEP == 0
    blk = pl.BlockSpec((None, seq, LANES), lambda b, c: (b, 0, c))
    bias_spec = pl.BlockSpec((2, 3, QBLK, KWIN), lambda b, c: (c, 0, 0, 0))
    return pl.pallas_call(
        functools.partial(_attn_kernel, seq=seq),
        grid=(bsz, width // LANES),
        in_specs=[blk, blk, blk] + [bias_spec] * len(biases),
        out_specs=blk,
        out_shape=jax.ShapeDtypeStruct((bsz, seq, width), F32),
        scratch_shapes=[pltpu.VMEM((seq, LANES), F32)] * 3,
        compiler_params=_cparams(("parallel", "arbitrary"), VMEM_LIMIT),
        name="attention",
    )(q, k, v, *biases)


SSM_LANE_GROUPS = SSM_WIDTH // LANES
SSM_GB_STATES = (LANES // SSM_GROUP) * SSM_STATE


def _ssm_params(lam_re, lam_im, log_step, b_re, b_im, c_re, c_im):
    f = lambda t: t.astype(F32)
    lr, li = f(lam_re), f(lam_im)
    step = jnp.exp(f(log_step))[..., None]
    mag = jnp.exp(lr * step)
    ar, ai = mag * jnp.cos(li * step), mag * jnp.sin(li * step)
    nr, ni = ar - 1.0, ai
    den = lr * lr + li * li
    cr, ci = (nr * lr + ni * li) / den, (ni * lr - nr * li) / den
    br, bi = f(b_re), f(b_im)
    bbr = cr[..., None] * br - ci[..., None] * bi
    bbi = cr[..., None] * bi + ci[..., None] * br
    gpb = LANES // SSM_GROUP
    eye = jnp.eye(gpb, dtype=F32)

    def in_map(t):
        t = t.reshape(2, SSM_LANE_GROUPS, gpb, SSM_STATE, SSM_GROUP)
        return jnp.einsum('dbgpc,gh->dbgchp', t, eye).reshape(2, SSM_LANE_GROUPS, LANES, SSM_GB_STATES)

    def out_map(t):
        t = t.reshape(2, SSM_LANE_GROUPS, gpb, SSM_GROUP, SSM_STATE)
        return jnp.einsum('dbgcp,gh->dbgphc', t, eye).reshape(2, SSM_LANE_GROUPS, SSM_GB_STATES, LANES)

    bw = jnp.concatenate([in_map(bbr), in_map(bbi)], axis=-1).astype(BF16)
    cw = jnp.concatenate([out_map(f(c_re)), -out_map(f(c_im))], axis=-2).astype(BF16)
    a = jnp.stack([ar.reshape(2, SSM_LANE_GROUPS, SSM_GB_STATES),
                   ai.reshape(2, SSM_LANE_GROUPS, SSM_GB_STATES)], axis=2)
    return a, bw, cw


def _ssm_kernel(u_ref, a_ref, bw_ref, cw_ref, y_ref, st_ref, bu0_ref, bu1_ref, *, ts, bsz):
    d = pl.program_id(0)
    ns = SSM_GB_STATES
    bu_refs = (bu0_ref, bu1_ref)

    @pl.when(pl.program_id(1) == 0)
    def _():
        st_ref[...] = jnp.zeros_like(st_ref)

    for g0 in range(0, SSM_LANE_GROUPS, 2):
        gbs = (g0, g0 + 1)
        for gb, bu_ref in zip(gbs, bu_refs):
            bu_ref[...] = jnp.dot(u_ref[gb].astype(BF16), bw_ref[0, gb], preferred_element_type=F32)
        coefs = [(jnp.broadcast_to(a_ref[0, gb, 0:1, :], (bsz, ns)), jnp.broadcast_to(a_ref[0, gb, 1:2, :], (bsz, ns)))
                 for gb in gbs]

        def step(j, carry, coefs=coefs):
            tl = jnp.where(d == 0, j, ts - 1 - j)
            r = pl.multiple_of(tl * bsz, bsz)
            out = []
            for (ar, ai), bu_ref, (xr, xi) in zip(coefs, bu_refs, carry):
                nr = ar * xr - ai * xi + bu_ref[pl.ds(r, bsz), :ns]
                ni = ar * xi + ai * xr + bu_ref[pl.ds(r, bsz), ns:]
                bu_ref[pl.ds(r, bsz), :ns] = nr
                bu_ref[pl.ds(r, bsz), ns:] = ni
                out.append((nr, ni))
            return tuple(out)

        final = lax.fori_loop(0, ts, step, tuple((st_ref[gb, :, :ns], st_ref[gb, :, ns:]) for gb in gbs))
        for gb, bu_ref, (xr, xi) in zip(gbs, bu_refs, final):
            st_ref[gb, :, :ns] = xr
            st_ref[gb, :, ns:] = xi
            y_ref[0, :, gb * LANES:(gb + 1) * LANES] = jnp.dot(
                bu_ref[...].astype(BF16), cw_ref[0, gb], preferred_element_type=F32)


def _ssm_scan(u_tm, a, bw, cw, *, bsz, ts=64):
    slabs, rows, _ = u_tm.shape
    width = slabs * LANES
    seq = rows // bsz
    ts = min(ts, seq)
    nt = seq // ts
    tblk = lambda d, i: jnp.where(d == 0, i, nt - 1 - i)
    return pl.pallas_call(
        functools.partial(_ssm_kernel, ts=ts, bsz=bsz),
        grid=(2, nt),
        in_specs=[
            pl.BlockSpec((slabs, ts * bsz, LANES), lambda d, i: (0, tblk(d, i), 0)),
            pl.BlockSpec((1,) + a.shape[1:], lambda d, i: (d, 0, 0, 0)),
            pl.BlockSpec((1,) + bw.shape[1:], lambda d, i: (d, 0, 0, 0)),
            pl.BlockSpec((1,) + cw.shape[1:], lambda d, i: (d, 0, 0, 0)),
        ],
        out_specs=pl.BlockSpec((1, ts * bsz, width), lambda d, i: (d, tblk(d, i), 0)),
        out_shape=jax.ShapeDtypeStruct((2, rows, width), F32),
        scratch_shapes=[pltpu.VMEM((SSM_LANE_GROUPS, bsz, 2 * SSM_GB_STATES), F32),
                        pltpu.VMEM((ts * bsz, 2 * SSM_GB_STATES), F32),
                        pltpu.VMEM((ts * bsz, 2 * SSM_GB_STATES), F32)],
        compiler_params=_cparams(("arbitrary", "arbitrary"), VMEM_LIMIT),
        name="ssm_scan",
    )(u_tm, a, bw, cw)


def _ssm_post_kernel(y_ref, u_ref, d_ref, wg_ref, g_ref, wo_ref, z_ref, slab_ref, *, bsz):
    u = jnp.concatenate([u_ref[c] for c in range(u_ref.shape[0])], axis=-1)
    y = _gelu(y_ref[0] + y_ref[1] + d_ref[...] * u).astype(BF16)
    ab = jnp.dot(y, wg_ref[...], preferred_element_type=F32)
    ssm = ab[:, :SSM_WIDTH] * jax.nn.sigmoid(ab[:, SSM_WIDTH:])
    n = _rms(ssm, g_ref[...]).astype(BF16)
    z = jnp.dot(n, wo_ref[...], preferred_element_type=F32)
    nt, dm = z.shape[0] // bsz, z.shape[1]
    for c in range(dm // LANES):
        slab_ref[c] = z[:, c * LANES:(c + 1) * LANES]
    for b in range(bsz):
        for c in range(dm // LANES):
            z_ref[:, b * dm + c * LANES:b * dm + (c + 1) * LANES] = slab_ref[c, pl.ds(b, nt, stride=bsz), :]


def _ssm_post(y, u_tm, d_skip, w_glu_bf16, g_ssm, w_out_ssm_bf16, *, bsz, tr=512):
    slabs, rows, _ = u_tm.shape
    width = slabs * LANES
    tr = min(tr, rows)
    assert tr % bsz == 0
    dm = w_out_ssm_bf16.shape[1]
    full = lambda a: pl.BlockSpec(a.shape, lambda i: (0,) * a.ndim)
    d2, g2 = d_skip.reshape(1, width), g_ssm.reshape(1, width)
    return pl.pallas_call(
        functools.partial(_ssm_post_kernel, bsz=bsz),
        grid=(rows // tr,),
        in_specs=[pl.BlockSpec((2, tr, width), lambda i: (0, i, 0)),
                  pl.BlockSpec((slabs, tr, LANES), lambda i: (0, i, 0)),
                  full(d2), full(w_glu_bf16), full(g2), full(w_out_ssm_bf16)],
        out_specs=pl.BlockSpec((tr // bsz, bsz * dm), lambda i: (i, 0)),
        out_shape=jax.ShapeDtypeStruct((rows // bsz, bsz * dm), F32),
        scratch_shapes=[pltpu.VMEM((dm // LANES, tr, LANES), F32)],
        compiler_params=_cparams(("parallel",), VMEM_LIMIT),
        name="ssm_post",
    )(y, u_tm, d2, w_glu_bf16, g2, w_out_ssm_bf16)


def _mix_out_kernel(x_ref, z_ref, a_ref, ga_ref, wo_ref, gf_ref, wq_ref, xn_ref, h_ref, q_ref):
    n = _rms(a_ref[0], ga_ref[...]).astype(BF16)
    xn = x_ref[0] + z_ref[...] + jnp.dot(n, wo_ref[...], preferred_element_type=F32)
    xn_ref[0] = xn
    h = _rms(xn, gf_ref[...])
    for c in range(SUBLANES):
        h_ref[pl.ds(c, h.shape[0], stride=SUBLANES), :] = h[:, c * LANES:(c + 1) * LANES]
    qp = jnp.dot(h.astype(BF16), wq_ref[...], preferred_element_type=F32)
    for hd in range(PEER_HEADS):
        q_ref[hd] = qp[:, hd * LANES:(hd + 1) * LANES]


def _mix_out(x, z_tm, attn, g_attn, w_out_attn_bf16, norm_ffn, w_query, *, ts=256):
    bsz, seq, dm = x.shape
    ts = min(ts, seq)
    ns = seq // ts
    row = pl.BlockSpec((1, ts, dm), lambda b, i: (b, i, 0))
    half = pl.BlockSpec((1, ts, ATTN_WIDTH), lambda b, i: (b, i, 0))
    full = lambda a: pl.BlockSpec(a.shape, lambda b, i: (0,) * a.ndim)
    ga, gf = g_attn.reshape(1, ATTN_WIDTH), norm_ffn.reshape(1, dm)
    qdim = w_query.shape[1] // PEER_HEADS
    return pl.pallas_call(
        _mix_out_kernel,
        grid=(bsz, ns),
        in_specs=[row, pl.BlockSpec((ts, dm), lambda b, i: (i, b)), half,
                  full(ga), full(w_out_attn_bf16), full(gf), full(w_query)],
        out_specs=[row, pl.BlockSpec((ts * SUBLANES, LANES), lambda b, i: (b * ns + i, 0)),
                   pl.BlockSpec((PEER_HEADS, ts, qdim), lambda b, i: (0, b * ns + i, 0))],
        out_shape=[jax.ShapeDtypeStruct((bsz, seq, dm), F32),
                   jax.ShapeDtypeStruct((bsz * seq * SUBLANES, LANES), F32),

                   jax.ShapeDtypeStruct((PEER_HEADS, bsz * seq, qdim), F32)],
        compiler_params=_cparams(("parallel", "arbitrary"), VMEM_LIMIT),
        name="mix_out",
    )(x, z_tm, attn, ga, w_out_attn_bf16, gf, w_query)


TOPK_TOKENS = SUBLANES * LANES
KEY_PITCH = PEER_KEYS + 4
_CANDIDATES = tuple((a, b) for a in range(PEER_TOPK) for b in range(PEER_TOPK) if (a + 1) * (b + 1) <= PEER_TOPK)


def _tree(op, xs):
    xs = list(xs)
    while len(xs) > 1:
        xs = [op(xs[i], xs[i + 1]) if i + 1 < len(xs) else xs[i] for i in range(0, len(xs), 2)]
    return xs[0]


def _extract16(problems):
    ninf = jnp.float32(-jnp.inf)

    def better(a, b):
        gt = b[0] > a[0]
        return tuple(jnp.where(gt, y, x) for x, y in zip(a, b))

    def step(r, carry):
        for p in problems:
            s_ref, order = p["s"], p["order"]
            assert list(order) == sorted(order)
            n = len(order)
            rows = [(s_ref[k], order[k]) + ((p["pay"][k],) if p.get("pay") is not None else ())
                    for k in range(n)]
            win = _tree(better, rows)
            m, am = win[0], win[1]
            for k in range(n):
                s_ref[k] = jnp.where(am == order[k], ninf, s_ref[k])
            p["vals"][r] = m
            p["picks"][r] = win[2] if len(win) > 2 else am.astype(jnp.int32)
        return carry

    lax.fori_loop(0, PEER_TOPK, step, 0)


def _peer_topk_kernel(q_ref, k_ref, idx_ref, gate_ref,
                      slab_ref, s1_ref, s2_ref, t1_ref, i1_ref, t2_ref, i2_ref, cand_ref, pay_ref, ts_ref, ex_ref):
    dn = (((1,), (1,)), ((), ()))
    keys = tuple(range(PEER_KEYS))

    def head(h, carry):
        for w, s_ref in ((0, s1_ref), (1, s2_ref)):
            for j in range(SUBLANES):
                slab_ref[j * KEY_PITCH:j * KEY_PITCH + PEER_KEYS, :] = lax.dot_general(
                    k_ref[w, h], q_ref[h, j * LANES:(j + 1) * LANES, :].astype(BF16), dn,
                    preferred_element_type=F32)
            for k in range(PEER_KEYS):
                s_ref[k] = slab_ref[pl.ds(k, SUBLANES, stride=KEY_PITCH), :]
        _extract16([dict(s=s1_ref, order=keys, vals=t1_ref, picks=i1_ref),
                    dict(s=s2_ref, order=keys, vals=t2_ref, picks=i2_ref)])
        for c, (a, b) in enumerate(_CANDIDATES):
            cand_ref[c] = t1_ref[a] + t2_ref[b]
            pay_ref[c] = i1_ref[a] * PEER_KEYS + i2_ref[b]
        _extract16([dict(s=cand_ref, order=tuple(a * PEER_TOPK + b for a, b in _CANDIDATES), pay=pay_ref,
                         vals=ts_ref, picks=ex_ref)])
        top_s = ts_ref[...]
        e = jnp.exp(top_s - jnp.max(top_s, axis=0, keepdims=True))
        gate_ref[0, h] = e / jnp.sum(e, axis=0, keepdims=True)
        idx_ref[0, h] = ex_ref[...] * ROWS_PER_EXPERT
        return carry

    lax.fori_loop(0, PEER_HEADS, head, 0)


def _peer_topk(qp, keys_pad):
    n_tok = qp.shape[1]
    tt = TOPK_TOKENS
    assert n_tok % tt == 0
    shp = (n_tok // tt, PEER_HEADS, PEER_TOPK, SUBLANES, LANES)
    out = pl.BlockSpec((1,) + shp[1:], lambda i: (i, 0, 0, 0, 0))
    vregs = lambda n, dt: pltpu.VMEM((n, SUBLANES, LANES), dt)
    idx, gate = pl.pallas_call(
        _peer_topk_kernel,
        grid=(n_tok // tt,),
        in_specs=[pl.BlockSpec((PEER_HEADS, tt, qp.shape[2]), lambda i: (0, i, 0)),
                  pl.BlockSpec(keys_pad.shape, lambda i: (0, 0, 0, 0))],
        out_specs=[out, out],
        out_shape=[jax.ShapeDtypeStruct(shp, jnp.int32), jax.ShapeDtypeStruct(shp, F32)],
        scratch_shapes=[pltpu.VMEM((SUBLANES * KEY_PITCH, LANES), F32),
                        vregs(PEER_KEYS, F32), vregs(PEER_KEYS, F32),
                        vregs(PEER_TOPK, F32), vregs(PEER_TOPK, jnp.int32),
                        vregs(PEER_TOPK, F32), vregs(PEER_TOPK, jnp.int32),
                        vregs(len(_CANDIDATES), F32), vregs(len(_CANDIDATES), jnp.int32),
                        vregs(PEER_TOPK, F32), vregs(PEER_TOPK, jnp.int32)],
        compiler_params=_cparams(("parallel",), VMEM_LIMIT),
        name="peer_topk",
    )(qp, keys_pad)
    to_tok = lambda a: jnp.transpose(a, (0, 3, 4, 1, 2)).reshape(n_tok, PEER_SEL)
    return to_tok(idx), to_tok(gate)


ROWS_PER_EXPERT = 4
PAIR_CHUNK = 32
CHUNK_ROWS = PAIR_CHUNK * ROWS_PER_EXPERT
SMEM_GROUP = 8
N_CHUNKS = PEER_SEL // PAIR_CHUNK
TOKENS_PER_STEP = 16


def _pack_table_kernel(t_ref, o_ref):
    rows, dm = t_ref.shape
    for c in range(ROWS_PER_EXPERT):
        lo = lax.bitcast_convert_type(t_ref[:, c * LANES:(c + 1) * LANES].astype(BF16).astype(F32), jnp.int32)
        hi = lax.bitcast_convert_type(
            t_ref[:, dm // 2 + c * LANES:dm // 2 + (c + 1) * LANES].astype(BF16).astype(F32), jnp.int32)
        o_ref[pl.ds(c, rows, stride=ROWS_PER_EXPERT), :] = lax.shift_right_logical(lo, 16) | hi


def _pack_table(tables, layer, *, rows=512):
    _, e, dm = tables.shape
    assert dm == 2 * ROWS_PER_EXPERT * LANES
    rows = min(rows, e)
    return pl.pallas_call(
        _pack_table_kernel,
        grid=(e // rows,),
        in_specs=[pl.BlockSpec((None, rows, dm), lambda i: (layer, i, 0))],
        out_specs=pl.BlockSpec((rows * ROWS_PER_EXPERT, LANES), lambda i: (i, 0)),
        out_shape=jax.ShapeDtypeStruct((e * ROWS_PER_EXPERT, LANES), jnp.int32),
        compiler_params=_cparams(("parallel",), VMEM_LIMIT),
        name="pack_table",
    )(tables)


def _unpack_words(w):
    lo = lax.bitcast_convert_type(w << 16, F32)
    hi = lax.bitcast_convert_type(w & jnp.int32(-65536), F32)
    return lo, hi


def _gather_chunk(idx_ref, tab_ref, buf_ref, c):
    for g in range(PAIR_CHUNK // SMEM_GROUP):
        ids = idx_ref.at[pl.ds(pl.multiple_of(c * PAIR_CHUNK + g * SMEM_GROUP, SMEM_GROUP), SMEM_GROUP)]
        for i in range(SMEM_GROUP):
            e4 = pl.multiple_of(ids[i], ROWS_PER_EXPERT)
            s = (g * SMEM_GROUP + i) * ROWS_PER_EXPERT
            buf_ref[s:s + ROWS_PER_EXPERT, :] = tab_ref[pl.ds(e4, ROWS_PER_EXPERT), :]


def _peer_u_kernel(idx_ref, h_ref, gate_ref, tab_ref, coef_ref, *scratch, tt):
    bufs, m_ref = (scratch[:N_CHUNKS], scratch[N_CHUNKS:2 * N_CHUNKS]), scratch[-1]
    n_chunks = tt * N_CHUNKS
    row = lax.broadcasted_iota(jnp.int32, (SUBLANES, CHUNK_ROWS), 0)
    col = lax.broadcasted_iota(jnp.int32, (SUBLANES, CHUNK_ROWS), 1)
    quarter = (col % ROWS_PER_EXPERT) == (row % ROWS_PER_EXPERT)
    m_lo = jnp.logical_and(row < ROWS_PER_EXPERT, quarter)
    m_hi = jnp.logical_and(row >= ROWS_PER_EXPERT, quarter)
    pr = lax.broadcasted_iota(jnp.int32, (CHUNK_ROWS, LANES), 0) // ROWS_PER_EXPERT
    pc = lax.broadcasted_iota(jnp.int32, (CHUNK_ROWS, LANES), 1)
    place = [(pr + part * PAIR_CHUNK == pc).astype(BF16) for part in range(N_CHUNKS)]
    dn = (((1,), (1,)), ((), ()))
    gather = functools.partial(_gather_chunk, idx_ref, tab_ref)

    def chunk_dots(t, part, buf_ref):
        rows = pl.ds(pl.multiple_of(t * SUBLANES, SUBLANES), SUBLANES)
        x = h_ref[rows, :].astype(BF16)
        lo, hi = _unpack_words(buf_ref[...])
        b = jnp.concatenate([lo.astype(BF16), hi.astype(BF16)], axis=0)
        out = lax.dot_general(x, b, dn, preferred_element_type=F32)
        m = jnp.where(m_lo, out[:, :CHUNK_ROWS], 0.0) + jnp.where(m_hi, out[:, CHUNK_ROWS:], 0.0)
        m_ref[part, pl.ds(t, 1), :] = jnp.sum(m, axis=0, keepdims=True)

    for j in range(N_CHUNKS):
        gather(bufs[0][j], j)

    def body(i, carry):
        for s in range(TOKENS_PER_STEP):
            half = s % 2
            t = TOKENS_PER_STEP * i + s
            for j in range(N_CHUNKS):
                chunk_dots(t, j, bufs[half][j])
            for j in range(N_CHUNKS):
                gather(bufs[1 - half][j], jnp.minimum((t + 1) * N_CHUNKS + j, n_chunks - 1))
        return carry

    lax.fori_loop(0, tt // TOKENS_PER_STEP, body, 0)

    dots = None
    for part in range(N_CHUNKS):
        m = m_ref[part]
        m_top = m.astype(BF16)
        m_rest = (m - m_top.astype(F32)).astype(BF16)
        s = (jnp.dot(m_top, place[part], preferred_element_type=F32)
             + jnp.dot(m_rest, place[part], preferred_element_type=F32))
        dots = s if dots is None else dots + s
    coef = (gate_ref[...] * _gelu(dots)).astype(BF16)
    for part in range(N_CHUNKS):
        coef_ref[part] = lax.dot_general(coef, place[part], dn, preferred_element_type=F32)


def _peer_u(idx4, h8, gates, tab, *, tt=128):
    n_tok = gates.shape[0]
    tt = min(tt, n_tok)
    assert TOKENS_PER_STEP % 2 == 0 and tt % TOKENS_PER_STEP == 0
    tok2 = pl.BlockSpec((tt, PEER_SEL), lambda i: (i, 0))
    return pl.pallas_call(
        functools.partial(_peer_u_kernel, tt=tt),
        grid=(n_tok // tt,),
        in_specs=[pl.BlockSpec((tt * PEER_SEL,), lambda i: (i,), memory_space=pltpu.SMEM),
                  pl.BlockSpec((tt * SUBLANES, LANES), lambda i: (i, 0)), tok2,
                  pl.BlockSpec(tab.shape, lambda i: (0, 0), pipeline_mode=pl.Buffered(1))],
        out_specs=pl.BlockSpec((N_CHUNKS, tt, CHUNK_ROWS), lambda i: (0, i, 0)),
        out_shape=jax.ShapeDtypeStruct((N_CHUNKS, n_tok, CHUNK_ROWS), F32),
        scratch_shapes=[pltpu.VMEM((CHUNK_ROWS, LANES), jnp.int32)] * (2 * N_CHUNKS)
                       + [pltpu.VMEM((N_CHUNKS, tt, LANES), F32)],
        compiler_params=_cparams(("arbitrary",), VMEM_LIMIT),
        name="peer_u",
    )(idx4, h8, gates, tab)


def _peer_v_kernel(idx_ref, cx_ref, tab_ref, o_ref, *scratch, tt):
    bufs, parts_ref = (scratch[:N_CHUNKS], scratch[N_CHUNKS:2 * N_CHUNKS]), scratch[2 * N_CHUNKS]
    n_chunks = tt * N_CHUNKS
    row = lax.broadcasted_iota(jnp.int32, (SUBLANES, CHUNK_ROWS), 0)
    col = lax.broadcasted_iota(jnp.int32, (SUBLANES, CHUNK_ROWS), 1)
    quarter = (col % ROWS_PER_EXPERT) == (row % ROWS_PER_EXPERT)
    m_lo = jnp.logical_and(row < ROWS_PER_EXPERT, quarter)
    m_hi = jnp.logical_and(row >= ROWS_PER_EXPERT, quarter)
    gather = functools.partial(_gather_chunk, idx_ref, tab_ref)

    def chunk_sum(t, parity, part, buf_ref):
        cx = jnp.broadcast_to(cx_ref[part, pl.ds(t, 1), :], (SUBLANES, CHUNK_ROWS))
        a = jnp.concatenate([jnp.where(m_lo, cx, 0.0), jnp.where(m_hi, cx, 0.0)], axis=1).astype(BF16)
        lo, hi = _unpack_words(buf_ref[...])
        b = jnp.concatenate([lo.astype(BF16), hi.astype(BF16)], axis=0)
        parts_ref[parity, part] = jnp.dot(a, b, preferred_element_type=F32)

    parts_ref[...] = jnp.zeros_like(parts_ref)
    for j in range(N_CHUNKS):
        gather(bufs[0][j], j)

    def body(i, carry):
        for s in range(TOKENS_PER_STEP):
            half = s % 2
            t = TOKENS_PER_STEP * i + s
            prev = jnp.maximum(t - 1, 0)
            o_ref[prev] = _tree(jnp.add, [parts_ref[1 - half, j] for j in range(N_CHUNKS)])
            for j in range(N_CHUNKS):
                chunk_sum(t, half, j, bufs[half][j])
            for j in range(N_CHUNKS):
                gather(bufs[1 - half][j], jnp.minimum((t + 1) * N_CHUNKS + j, n_chunks - 1))
        return carry

    lax.fori_loop(0, tt // TOKENS_PER_STEP, body, 0)
    o_ref[tt - 1] = _tree(jnp.add, [parts_ref[1, j] for j in range(N_CHUNKS)])


def _peer_v(idx4, cx, tab, *, tt=128):
    n_tok = idx4.shape[0] // PEER_SEL
    tt = min(tt, n_tok)
    assert TOKENS_PER_STEP % 2 == 0 and tt % TOKENS_PER_STEP == 0
    smem = pl.BlockSpec((tt * PEER_SEL,), lambda i: (i,), memory_space=pltpu.SMEM)
    return pl.pallas_call(
        functools.partial(_peer_v_kernel, tt=tt),
        grid=(n_tok // tt,),
        in_specs=[smem, pl.BlockSpec((N_CHUNKS, tt, CHUNK_ROWS), lambda i: (0, i, 0)),
                  pl.BlockSpec(tab.shape, lambda i: (0, 0), pipeline_mode=pl.Buffered(1))],
        out_specs=pl.BlockSpec((tt, SUBLANES, LANES), lambda i: (i, 0, 0)),
        out_shape=jax.ShapeDtypeStruct((n_tok, SUBLANES, LANES), F32),
        scratch_shapes=[pltpu.VMEM((CHUNK_ROWS, LANES), jnp.int32)] * (2 * N_CHUNKS)
                       + [pltpu.VMEM((2, N_CHUNKS, SUBLANES, LANES), F32)],
        compiler_params=_cparams(("arbitrary",), VMEM_LIMIT),
        name="peer_v",
    )(idx4, cx, tab)


def _peer(h8, qp, sub_keys, tab_u, tab_v):
    n_tok = qp.shape[1]
    half = sub_keys.shape[-1]
    z = jnp.zeros_like(sub_keys[0])
    keys_pad = jnp.stack([jnp.concatenate([sub_keys[0], z], axis=-1),
                          jnp.concatenate([z, sub_keys[1]], axis=-1)]).astype(BF16)
    assert keys_pad.shape[-1] == 2 * half == qp.shape[-1]
    idx4, gates = _peer_topk(qp, keys_pad)
    idx4 = idx4.reshape(n_tok * PEER_SEL)
    coef = _peer_u(idx4, h8, gates, tab_u)
    out = _peer_v(idx4, coef, tab_v)
    return out.reshape(n_tok * SUBLANES, LANES)


def _final_kernel(x_ref, p_ref, g_ref, o_ref):
    o_ref[...] = _rms(x_ref[...] + _rows_from_chunks(p_ref, x_ref.shape[0]), g_ref[...])


def _final_norm(x, p, g, *, tr=1024):
    rows, dm = x.shape
    tr = min(tr, rows)
    blk = pl.BlockSpec((tr, dm), lambda i: (i, 0))
    return pl.pallas_call(
        _final_kernel,
        grid=(rows // tr,),
        in_specs=[blk, pl.BlockSpec((tr * SUBLANES, LANES), lambda i: (i, 0)),
                  pl.BlockSpec((1, dm), lambda i: (0, 0))],
        out_specs=blk,
        out_shape=jax.ShapeDtypeStruct((rows, dm), F32),
        compiler_params=_cparams(("parallel",), VMEM_LIMIT),
        name="final_norm",
    )(x, p, g.reshape(1, dm))


def kernel(x, w_in, w_out, rel_bias, g_attn, g_ssm, norm_mix, norm_ffn, lam_re, lam_im, log_step, b_re, b_im, c_re, c_im, d_skip, w_glu, w_query, sub_keys, expert_u, expert_v, norm_final):
    bsz, seq, dm = x.shape
    depth = w_in.shape[0]
    prev = None
    biases = [_attn_bias_tables(rel_bias, d) for _, d in DILATED_PATTERNS]
    for l in range(depth):
        x, q, k, v, u_tm = _in_proj(x, prev, norm_mix[l], w_in[l].astype(BF16))
        attn = _attention(q, k, v, biases)
        a, bw, cw = _ssm_params(lam_re[l], lam_im[l], log_step[l], b_re[l], b_im[l], c_re[l], c_im[l])
        y = _ssm_scan(u_tm, a, bw, cw, bsz=bsz)
        wo = w_out[l].astype(BF16)
        z = _ssm_post(y, u_tm, d_skip[l], w_glu[l].astype(BF16), g_ssm[l], wo[ATTN_WIDTH:], bsz=bsz)
        x, h, qp = _mix_out(x, z, attn,
                            g_attn[l], wo[:ATTN_WIDTH], norm_ffn[l], w_query[l].astype(BF16))
        prev = _peer(h, qp, sub_keys[l], _pack_table(expert_u, l), _pack_table(expert_v, l))
    out = _final_norm(x.reshape(bsz * seq, dm), prev, norm_final)
    return out.reshape(bsz, seq, dm)
```
